```python
import jax, jax.numpy as jnp
from jax import lax
import numpy as np

D_MODEL = 1024
BATCH = 8
SEQ = 8192
DEPTH = 1

PLE_DIM = 256
CONV_HEADS = 8
CONV_HEAD_DIM = 64
CONV_WIDTH = CONV_HEADS * CONV_HEAD_DIM
CONV_K = 3
MLA_HEADS = 8
Q_LORA = 256
KV_LORA = 128
QK_NOPE = 64
QK_ROPE = 32
V_HEAD = 64
MLA_WIDTH = MLA_HEADS * V_HEAD
D_MIX = CONV_WIDTH + MLA_WIDTH
D_IN = 3 * CONV_WIDTH + Q_LORA + KV_LORA + QK_ROPE
D_FF = 2816
FFN_CONV_K = 3
Q_BLOCK = 128
ROPE_THETA = 10000.0
EPS = 1e-6

kernel_name = "hybrid_shortconv_mla_convffn_ple"


def rmsnorm(x, g):
    xf = x.astype(jnp.float32)
    y = xf * lax.rsqrt(jnp.mean(xf * xf, axis=-1, keepdims=True) + EPS)
    return (y * g.astype(jnp.float32)).astype(x.dtype)


def dwconv_centred(x, w, b=None):
    s = x.shape[1]
    xp = jnp.pad(x, ((0, 0), (1, 1), (0, 0)))
    y = xp[:, 0:s] * w[0] + xp[:, 1:s + 1] * w[1] + xp[:, 2:s + 2] * w[2]
    if b is not None:
        y = y + b
    return y


def rope_tables(s, dim, dtype):
    pos = jnp.arange(s, dtype=jnp.float32)
    inv_freq = ROPE_THETA ** (-jnp.arange(0, dim, 2, dtype=jnp.float32) / dim)
    ang = pos[:, None] * inv_freq[None, :]
    ang = jnp.concatenate([ang, ang], axis=-1)
    return jnp.cos(ang).astype(dtype), jnp.sin(ang).astype(dtype)


def rotate_half(x):
    x1, x2 = jnp.split(x, 2, axis=-1)
    return jnp.concatenate([-x2, x1], axis=-1)


def mla_attention(q_nope, q_rope, k_nope, k_rope, v):
    b, s, h, _ = q_nope.shape
    nb = s // Q_BLOCK
    scale = (QK_NOPE + QK_ROPE) ** -0.5
    qn = q_nope.reshape(b, nb, Q_BLOCK, h, QK_NOPE).transpose(1, 0, 2, 3, 4)
    qr = q_rope.reshape(b, nb, Q_BLOCK, h, QK_ROPE).transpose(1, 0, 2, 3, 4)

    def block(args):
        qn_b, qr_b = args
        sc = (jnp.einsum('bqhd,bkhd->bhqk', qn_b, k_nope)
              + jnp.einsum('bqhr,bkr->bhqk', qr_b, k_rope))
        probs = jax.nn.softmax(sc.astype(jnp.float32) * scale, axis=-1).astype(v.dtype)
        return jnp.einsum('bhqk,bkhd->bqhd', probs, v)

    out = lax.map(block, (qn, qr))
    return out.transpose(1, 0, 2, 3, 4).reshape(b, s, h * V_HEAD)


def _fwd_setup_inputs(seed: int = 0) -> dict:
    key = jax.random.key(seed)
    ks = jax.random.split(key, 20)
    f32 = jnp.float32

    def nrm(k, shape, fan_in):
        return jax.random.normal(k, shape, f32) * (fan_in ** -0.5)

    def gain(k, dim):
        return 1.0 + 0.01 * jax.random.normal(k, (DEPTH, dim), f32)

    return {
        "x": jax.random.normal(ks[0], (BATCH, SEQ, D_MODEL), f32),
        "p": jax.random.normal(ks[1], (DEPTH, BATCH, SEQ, PLE_DIM), f32),
        "norm_mix_g": gain(ks[2], D_MODEL),
        "w_in": nrm(ks[3], (DEPTH, D_MODEL, D_IN), D_MODEL),
        "conv_w": nrm(ks[4], (DEPTH, CONV_K, CONV_WIDTH), CONV_K),
        "q_norm_g": gain(ks[5], Q_LORA),
        "w_uq": nrm(ks[6], (DEPTH, Q_LORA, MLA_HEADS * (QK_NOPE + QK_ROPE)), Q_LORA),
        "kv_norm_g": gain(ks[7], KV_LORA),
        "w_ukv": nrm(ks[8], (DEPTH, KV_LORA, MLA_HEADS * (QK_NOPE + V_HEAD)), KV_LORA),
        "w_o": nrm(ks[9], (DEPTH, D_MIX, D_MODEL), D_MIX),
        "norm_ffn_g": gain(ks[10], D_MODEL),
        "w_up": nrm(ks[11], (DEPTH, D_MODEL, 2 * D_FF), D_MODEL),
        "ffn_conv_w": nrm(ks[12], (DEPTH, FFN_CONV_K, 2 * D_FF), FFN_CONV_K),
        "ffn_conv_b": 0.01 * jax.random.normal(ks[13], (DEPTH, 2 * D_FF), f32),
        "w_down": nrm(ks[14], (DEPTH, D_FF, D_MODEL), D_FF),
        "ple_norm_g": gain(ks[15], D_MODEL),
        "w_ple_gate": nrm(ks[16], (DEPTH, D_MODEL, D_MODEL), D_MODEL),
        "w_ple_proj": nrm(ks[17], (DEPTH, PLE_DIM, D_MODEL), PLE_DIM),
        "final_norm_g": 1.0 + 0.01 * jax.random.normal(ks[18], (D_MODEL,), f32),
    }


def _fwd_reference(x, p, norm_mix_g, w_in, conv_w, q_norm_g, w_uq, kv_norm_g, w_ukv, w_o,
              norm_ffn_g, w_up, ffn_conv_w, ffn_conv_b, w_down, ple_norm_g,
              w_ple_gate, w_ple_proj, final_norm_g):
    b, s, _ = x.shape
    cos, sin = rope_tables(s, QK_ROPE, x.dtype)
    split_pts = np.cumsum([CONV_WIDTH, CONV_WIDTH, CONV_WIDTH, Q_LORA, KV_LORA])

    for i in range(DEPTH):
        h = rmsnorm(x, norm_mix_g[i])
        z = h @ w_in[i]
        xc, bg, cg, q_lat, kv_lat, k_r = jnp.split(z, split_pts, axis=-1)

        y_conv = bg * dwconv_centred(cg * xc, conv_w[i])

        q = (rmsnorm(q_lat, q_norm_g[i]) @ w_uq[i]).reshape(b, s, MLA_HEADS, QK_NOPE + QK_ROPE)
        q_nope, q_rope = q[..., :QK_NOPE], q[..., QK_NOPE:]
        q_rope = q_rope * cos[None, :, None, :] + rotate_half(q_rope) * sin[None, :, None, :]
        kv = (rmsnorm(kv_lat, kv_norm_g[i]) @ w_ukv[i]).reshape(b, s, MLA_HEADS, QK_NOPE + V_HEAD)
        k_nope, v = kv[..., :QK_NOPE], kv[..., QK_NOPE:]
        k_rope = k_r * cos[None] + rotate_half(k_r) * sin[None]
        y_mla = mla_attention(q_nope, q_rope, k_nope, k_rope, v)

        x = x + jnp.concatenate([y_conv, y_mla], axis=-1) @ w_o[i]

        hf = rmsnorm(x, norm_ffn_g[i])
        a = dwconv_centred(hf @ w_up[i], ffn_conv_w[i], ffn_conv_b[i])
        g, u = jnp.split(a, 2, axis=-1)
        x = x + (jax.nn.silu(g) * u) @ w_down[i]

        gate = jax.nn.sigmoid(rmsnorm(x, ple_norm_g[i]) @ w_ple_gate[i])
        x = x + gate * (p[i] @ w_ple_proj[i])

    return rmsnorm(x, final_norm_g)


import jax as _jax
import jax.numpy as _jnp

TWIN_FORMAT = 'train_step'
FWD_PARAMS = ['x', 'p', 'norm_mix_g', 'w_in', 'conv_w', 'q_norm_g', 'w_uq', 'kv_norm_g', 'w_ukv', 'w_o', 'norm_ffn_g', 'w_up', 'ffn_conv_w', 'ffn_conv_b', 'w_down', 'ple_norm_g', 'w_ple_gate', 'w_ple_proj', 'final_norm_g']
TWIN_WEIGHTS = ['norm_mix_g', 'w_in', 'conv_w', 'q_norm_g', 'w_uq', 'kv_norm_g', 'w_ukv', 'w_o', 'norm_ffn_g', 'w_up', 'ffn_conv_w', 'ffn_conv_b', 'w_down', 'ple_norm_g', 'w_ple_gate', 'w_ple_proj', 'final_norm_g']
TWIN_DIFF_INPUT = 'x'
TWIN_INPUTS = ['x', 'p', 'norm_mix_g', 'w_in', 'conv_w', 'q_norm_g', 'w_uq', 'kv_norm_g', 'w_ukv', 'w_o', 'norm_ffn_g', 'w_up', 'ffn_conv_w', 'ffn_conv_b', 'w_down', 'ple_norm_g', 'w_ple_gate', 'w_ple_proj', 'final_norm_g', 'loss_target', 'm_norm_mix_g', 'm_w_in', 'm_conv_w', 'm_q_norm_g', 'm_w_uq', 'm_kv_norm_g', 'm_w_ukv', 'm_w_o', 'm_norm_ffn_g', 'm_w_up', 'm_ffn_conv_w', 'm_ffn_conv_b', 'm_w_down', 'm_ple_norm_g', 'm_w_ple_gate', 'm_w_ple_proj', 'm_final_norm_g', 'v_norm_mix_g', 'v_w_in', 'v_conv_w', 'v_q_norm_g', 'v_w_uq', 'v_kv_norm_g', 'v_w_ukv', 'v_w_o', 'v_norm_ffn_g', 'v_w_up', 'v_ffn_conv_w', 'v_ffn_conv_b', 'v_w_down', 'v_ple_norm_g', 'v_w_ple_gate', 'v_w_ple_proj', 'v_final_norm_g']
TWIN_OUTPUTS = ['loss', 'grad_x', 'grad_norm_mix_g', 'grad_w_in', 'grad_conv_w', 'grad_q_norm_g', 'grad_w_uq', 'grad_kv_norm_g', 'grad_w_ukv', 'grad_w_o', 'grad_norm_ffn_g', 'grad_w_up', 'grad_ffn_conv_w', 'grad_ffn_conv_b', 'grad_w_down', 'grad_ple_norm_g', 'grad_w_ple_gate', 'grad_w_ple_proj', 'grad_final_norm_g', 'delta_norm_mix_g', 'delta_w_in', 'delta_conv_w', 'delta_q_norm_g', 'delta_w_uq', 'delta_kv_norm_g', 'delta_w_ukv', 'delta_w_o', 'delta_norm_ffn_g', 'delta_w_up', 'delta_ffn_conv_w', 'delta_ffn_conv_b', 'delta_w_down', 'delta_ple_norm_g', 'delta_w_ple_gate', 'delta_w_ple_proj', 'delta_final_norm_g', 'new_m_norm_mix_g', 'new_m_w_in', 'new_m_conv_w', 'new_m_q_norm_g', 'new_m_w_uq', 'new_m_kv_norm_g', 'new_m_w_ukv', 'new_m_w_o', 'new_m_norm_ffn_g', 'new_m_w_up', 'new_m_ffn_conv_w', 'new_m_ffn_conv_b', 'new_m_w_down', 'new_m_ple_norm_g', 'new_m_w_ple_gate', 'new_m_w_ple_proj', 'new_m_final_norm_g', 'new_v_norm_mix_g', 'new_v_w_in', 'new_v_conv_w', 'new_v_q_norm_g', 'new_v_w_uq', 'new_v_kv_norm_g', 'new_v_w_ukv', 'new_v_w_o', 'new_v_norm_ffn_g', 'new_v_w_up', 'new_v_ffn_conv_w', 'new_v_ffn_conv_b', 'new_v_w_down', 'new_v_ple_norm_g', 'new_v_w_ple_gate', 'new_v_w_ple_proj', 'new_v_final_norm_g']
TWIN_LEAF_KINDS = {'loss': 'loss', 'grad_x': 'grad_x', 'grad_norm_mix_g': 'grad_w', 'grad_w_in': 'grad_w', 'grad_conv_w': 'grad_w', 'grad_q_norm_g': 'grad_w', 'grad_w_uq': 'grad_w', 'grad_kv_norm_g': 'grad_w', 'grad_w_ukv': 'grad_w', 'grad_w_o': 'grad_w', 'grad_norm_ffn_g': 'grad_w', 'grad_w_up': 'grad_w', 'grad_ffn_conv_w': 'grad_w', 'grad_ffn_conv_b': 'grad_w', 'grad_w_down': 'grad_w', 'grad_ple_norm_g': 'grad_w', 'grad_w_ple_gate': 'grad_w', 'grad_w_ple_proj': 'grad_w', 'grad_final_norm_g': 'grad_w', 'delta_norm_mix_g': 'delta_w', 'delta_w_in': 'delta_w', 'delta_conv_w': 'delta_w', 'delta_q_norm_g': 'delta_w', 'delta_w_uq': 'delta_w', 'delta_kv_norm_g': 'delta_w', 'delta_w_ukv': 'delta_w', 'delta_w_o': 'delta_w', 'delta_norm_ffn_g': 'delta_w', 'delta_w_up': 'delta_w', 'delta_ffn_conv_w': 'delta_w', 'delta_ffn_conv_b': 'delta_w', 'delta_w_down': 'delta_w', 'delta_ple_norm_g': 'delta_w', 'delta_w_ple_gate': 'delta_w', 'delta_w_ple_proj': 'delta_w', 'delta_final_norm_g': 'delta_w', 'new_m_norm_mix_g': 'new_m', 'new_m_w_in': 'new_m', 'new_m_conv_w': 'new_m', 'new_m_q_norm_g': 'new_m', 'new_m_w_uq': 'new_m', 'new_m_kv_norm_g': 'new_m', 'new_m_w_ukv': 'new_m', 'new_m_w_o': 'new_m', 'new_m_norm_ffn_g': 'new_m', 'new_m_w_up': 'new_m', 'new_m_ffn_conv_w': 'new_m', 'new_m_ffn_conv_b': 'new_m', 'new_m_w_down': 'new_m', 'new_m_ple_norm_g': 'new_m', 'new_m_w_ple_gate': 'new_m', 'new_m_w_ple_proj': 'new_m', 'new_m_final_norm_g': 'new_m', 'new_v_norm_mix_g': 'new_v', 'new_v_w_in': 'new_v', 'new_v_conv_w': 'new_v', 'new_v_q_norm_g': 'new_v', 'new_v_w_uq': 'new_v', 'new_v_kv_norm_g': 'new_v', 'new_v_w_ukv': 'new_v', 'new_v_w_o': 'new_v', 'new_v_norm_ffn_g': 'new_v', 'new_v_w_up': 'new_v', 'new_v_ffn_conv_w': 'new_v', 'new_v_ffn_conv_b': 'new_v', 'new_v_w_down': 'new_v', 'new_v_ple_norm_g': 'new_v', 'new_v_w_ple_gate': 'new_v', 'new_v_w_ple_proj': 'new_v', 'new_v_final_norm_g': 'new_v'}


def _forward(args):
    return _fwd_reference(*[args[k] for k in FWD_PARAMS])


def _output_shape():
    def fwd():
        inp = _fwd_setup_inputs(0)
        return _fwd_reference(*[inp[k] for k in FWD_PARAMS])
    out = _jax.eval_shape(fwd)
    return out.shape, out.dtype

N_MICROBATCH = 1
ADAM_LR = 0.001
ADAM_B1 = 0.9
ADAM_B2 = 0.999
ADAM_EPS = 1e-08
ADAM_WD = 0.01
ADAM_STEP = 10
PER_EXAMPLE_BATCH_AXIS = {'x': 0, 'p': 1, 'loss_target': 0}
SHARED_INPUTS = []
_WEIGHT_DTYPES = {'norm_mix_g': _jnp.float32, 'w_in': _jnp.float32, 'conv_w': _jnp.float32, 'q_norm_g': _jnp.float32, 'w_uq': _jnp.float32, 'kv_norm_g': _jnp.float32, 'w_ukv': _jnp.float32, 'w_o': _jnp.float32, 'norm_ffn_g': _jnp.float32, 'w_up': _jnp.float32, 'ffn_conv_w': _jnp.float32, 'ffn_conv_b': _jnp.float32, 'w_down': _jnp.float32, 'ple_norm_g': _jnp.float32, 'w_ple_gate': _jnp.float32, 'w_ple_proj': _jnp.float32, 'final_norm_g': _jnp.float32}
MOMENT_SCALE = {'norm_mix_g': 2.699699e-01, 'w_in': 1.875277e-01, 'conv_w': 2.184417e-01, 'q_norm_g': 3.924077e-02, 'w_uq': 2.131520e-02, 'kv_norm_g': 7.783839e-02, 'w_ukv': 2.493940e-02, 'w_o': 1.489221e-01, 'norm_ffn_g': 1.519550e-01, 'w_up': 6.441322e-02, 'ffn_conv_w': 6.456019e-02, 'ffn_conv_b': 6.365377e-02, 'w_down': 1.046971e-01, 'ple_norm_g': 3.705283e-02, 'w_ple_gate': 3.609528e-02, 'w_ple_proj': 9.242067e-02, 'final_norm_g': 6.395875e+01}


def _to_microbatches(a, axis):
    t = _jnp.moveaxis(a, axis, 0)
    t = t.reshape((N_MICROBATCH, t.shape[0] // N_MICROBATCH) + t.shape[1:])
    return _jnp.moveaxis(t, 1, axis + 1)


def setup_inputs(seed: int = 0) -> dict:
    inp = _fwd_setup_inputs(seed)
    key = _jax.random.fold_in(_jax.random.key(seed), 7919)
    shape, _ = _output_shape()
    out = dict(inp)
    out["loss_target"] = _jax.random.normal(_jax.random.fold_in(key, 0), shape, _jnp.float32)
    for i, name in enumerate(TWIN_WEIGHTS):
        w = inp[name].astype(_jnp.float32)
        if MOMENT_SCALE is None:
            s = _jnp.sqrt(_jnp.mean(_jnp.square(w)) + 1e-30)
        else:
            s = MOMENT_SCALE[name]
        km, kv = _jax.random.split(_jax.random.fold_in(key, i + 1))
        out[name] = w
        out["m_" + name] = s * _jax.random.normal(km, w.shape, _jnp.float32)
        out["v_" + name] = (s * s) * _jax.random.uniform(kv, w.shape, _jnp.float32, 0.5, 1.5)
    if N_MICROBATCH > 1:
        for name, axis in PER_EXAMPLE_BATCH_AXIS.items():
            out[name] = _to_microbatches(out[name], axis)
    return {'x': out['x'], 'p': out['p'], 'norm_mix_g': out['norm_mix_g'], 'w_in': out['w_in'], 'conv_w': out['conv_w'], 'q_norm_g': out['q_norm_g'], 'w_uq': out['w_uq'], 'kv_norm_g': out['kv_norm_g'], 'w_ukv': out['w_ukv'], 'w_o': out['w_o'], 'norm_ffn_g': out['norm_ffn_g'], 'w_up': out['w_up'], 'ffn_conv_w': out['ffn_conv_w'], 'ffn_conv_b': out['ffn_conv_b'], 'w_down': out['w_down'], 'ple_norm_g': out['ple_norm_g'], 'w_ple_gate': out['w_ple_gate'], 'w_ple_proj': out['w_ple_proj'], 'final_norm_g': out['final_norm_g'], 'loss_target': out['loss_target'], 'm_norm_mix_g': out['m_norm_mix_g'], 'm_w_in': out['m_w_in'], 'm_conv_w': out['m_conv_w'], 'm_q_norm_g': out['m_q_norm_g'], 'm_w_uq': out['m_w_uq'], 'm_kv_norm_g': out['m_kv_norm_g'], 'm_w_ukv': out['m_w_ukv'], 'm_w_o': out['m_w_o'], 'm_norm_ffn_g': out['m_norm_ffn_g'], 'm_w_up': out['m_w_up'], 'm_ffn_conv_w': out['m_ffn_conv_w'], 'm_ffn_conv_b': out['m_ffn_conv_b'], 'm_w_down': out['m_w_down'], 'm_ple_norm_g': out['m_ple_norm_g'], 'm_w_ple_gate': out['m_w_ple_gate'], 'm_w_ple_proj': out['m_w_ple_proj'], 'm_final_norm_g': out['m_final_norm_g'], 'v_norm_mix_g': out['v_norm_mix_g'], 'v_w_in': out['v_w_in'], 'v_conv_w': out['v_conv_w'], 'v_q_norm_g': out['v_q_norm_g'], 'v_w_uq': out['v_w_uq'], 'v_kv_norm_g': out['v_kv_norm_g'], 'v_w_ukv': out['v_w_ukv'], 'v_w_o': out['v_w_o'], 'v_norm_ffn_g': out['v_norm_ffn_g'], 'v_w_up': out['v_w_up'], 'v_ffn_conv_w': out['v_ffn_conv_w'], 'v_ffn_conv_b': out['v_ffn_conv_b'], 'v_w_down': out['v_w_down'], 'v_ple_norm_g': out['v_ple_norm_g'], 'v_w_ple_gate': out['v_w_ple_gate'], 'v_w_ple_proj': out['v_w_ple_proj'], 'v_final_norm_g': out['v_final_norm_g']}


def _loss(weights, diff, rest, loss_target):
    with _jax.named_scope("forward"):
        args = {**rest, TWIN_DIFF_INPUT: diff, **{k: w.astype(_WEIGHT_DTYPES[k]) for k, w in weights.items()}}
        y = _forward(args)
    with _jax.named_scope("loss_head"):
        err = _jnp.square(y.astype(_jnp.float32) - loss_target)
        return 0.5 * _jnp.sum(_jnp.mean(err, axis=-1)) if err.ndim else 0.5 * err


def _adamw(w, g, m, v):
    m = ADAM_B1 * m + (1.0 - ADAM_B1) * g
    v = ADAM_B2 * v + (1.0 - ADAM_B2) * _jnp.square(g)
    m_hat = m / (1.0 - ADAM_B1 ** ADAM_STEP)
    v_hat = v / (1.0 - ADAM_B2 ** ADAM_STEP)
    delta = -ADAM_LR * (m_hat / (_jnp.sqrt(v_hat) + ADAM_EPS) + ADAM_WD * w)
    return delta, m, v


def reference(x, p, norm_mix_g, w_in, conv_w, q_norm_g, w_uq, kv_norm_g, w_ukv, w_o, norm_ffn_g, w_up, ffn_conv_w, ffn_conv_b, w_down, ple_norm_g, w_ple_gate, w_ple_proj, final_norm_g, loss_target, m_norm_mix_g, m_w_in, m_conv_w, m_q_norm_g, m_w_uq, m_kv_norm_g, m_w_ukv, m_w_o, m_norm_ffn_g, m_w_up, m_ffn_conv_w, m_ffn_conv_b, m_w_down, m_ple_norm_g, m_w_ple_gate, m_w_ple_proj, m_final_norm_g, v_norm_mix_g, v_w_in, v_conv_w, v_q_norm_g, v_w_uq, v_kv_norm_g, v_w_ukv, v_w_o, v_norm_ffn_g, v_w_up, v_ffn_conv_w, v_ffn_conv_b, v_w_down, v_ple_norm_g, v_w_ple_gate, v_w_ple_proj, v_final_norm_g):
    given = dict(x=x, p=p, norm_mix_g=norm_mix_g, w_in=w_in, conv_w=conv_w, q_norm_g=q_norm_g, w_uq=w_uq, kv_norm_g=kv_norm_g, w_ukv=w_ukv, w_o=w_o, norm_ffn_g=norm_ffn_g, w_up=w_up, ffn_conv_w=ffn_conv_w, ffn_conv_b=ffn_conv_b, w_down=w_down, ple_norm_g=ple_norm_g, w_ple_gate=w_ple_gate, w_ple_proj=w_ple_proj, final_norm_g=final_norm_g, loss_target=loss_target, m_norm_mix_g=m_norm_mix_g, m_w_in=m_w_in, m_conv_w=m_conv_w, m_q_norm_g=m_q_norm_g, m_w_uq=m_w_uq, m_kv_norm_g=m_kv_norm_g, m_w_ukv=m_w_ukv, m_w_o=m_w_o, m_norm_ffn_g=m_norm_ffn_g, m_w_up=m_w_up, m_ffn_conv_w=m_ffn_conv_w, m_ffn_conv_b=m_ffn_conv_b, m_w_down=m_w_down, m_ple_norm_g=m_ple_norm_g, m_w_ple_gate=m_w_ple_gate, m_w_ple_proj=m_w_ple_proj, m_final_norm_g=m_final_norm_g, v_norm_mix_g=v_norm_mix_g, v_w_in=v_w_in, v_conv_w=v_conv_w, v_q_norm_g=v_q_norm_g, v_w_uq=v_w_uq, v_kv_norm_g=v_kv_norm_g, v_w_ukv=v_w_ukv, v_w_o=v_w_o, v_norm_ffn_g=v_norm_ffn_g, v_w_up=v_w_up, v_ffn_conv_w=v_ffn_conv_w, v_ffn_conv_b=v_ffn_conv_b, v_w_down=v_w_down, v_ple_norm_g=v_ple_norm_g, v_w_ple_gate=v_w_ple_gate, v_w_ple_proj=v_w_ple_proj, v_final_norm_g=v_final_norm_g)
    weights = {n: given[n] for n in TWIN_WEIGHTS}
    shared = {n: given[n] for n in SHARED_INPUTS}
    per_example = {n: given[n] for n in ['x', 'p']}
    grad_fn = _jax.value_and_grad(_loss, argnums=(0, 1))

    def one_microbatch(ex, loss_target):
        ex = dict(ex)
        diff = ex.pop(TWIN_DIFF_INPUT)
        return grad_fn(weights, diff, {**shared, **ex}, loss_target)

    if N_MICROBATCH == 1:
        loss, (grad_w, grad_x) = one_microbatch(per_example, given["loss_target"])
    else:
        def body(carry, xs):
            loss_sum, grad_sum = carry
            l_k, (gw_k, gx_k) = one_microbatch(xs[0], xs[1])
            with _jax.named_scope("update"):
                return (loss_sum + l_k, _jax.tree.map(_jnp.add, grad_sum, gw_k)), gx_k

        init = (_jnp.zeros((), _jnp.float32), _jax.tree.map(_jnp.zeros_like, weights))
        (loss, grad_w), grad_x = _jax.lax.scan(body, init, (per_example, given["loss_target"]))
    with _jax.named_scope("update"):
        delta_w, new_m, new_v = {}, {}, {}
        for n in TWIN_WEIGHTS:
            delta_w[n], new_m[n], new_v[n] = _adamw(weights[n], grad_w[n], given["m_" + n], given["v_" + n])
    return (loss, grad_x, *[grad_w[n] for n in TWIN_WEIGHTS], *[delta_w[n] for n in TWIN_WEIGHTS],
            *[new_m[n] for n in TWIN_WEIGHTS], *[new_v[n] for n in TWIN_WEIGHTS])
```

```python
import functools

import jax
import jax.numpy as jnp
from jax import lax
from jax.experimental import pallas as pl
from jax.experimental.pallas import tpu as pltpu

F32 = jnp.float32
BF16 = jnp.bfloat16

D_MODEL = 1024
CONV_WIDTH = 512
Q_LORA = 256
KV_LORA = 128
QK_NOPE = 64
QK_ROPE = 32
V_HEAD = 64
N_HEADS = 8
HEAD_PAD = 128
D_ATT = N_HEADS * HEAD_PAD
D_IN = 3 * CONV_WIDTH + Q_LORA + KV_LORA + QK_ROPE
D_IN_PAD = 3 * CONV_WIDTH + Q_LORA + KV_LORA + HEAD_PAD
D_FF = 2816
ROPE_THETA = 10000.0
EPS = 1e-6
SM_SCALE = (QK_NOPE + QK_ROPE) ** -0.5

ADAM_LR = 0.001
ADAM_B1 = 0.9
ADAM_B2 = 0.999
ADAM_EPS = 1e-08
ADAM_WD = 0.01
ADAM_STEP = 10

N_CHIPS = 4
N_DEV = 8
MESH = pl.DeviceIdType.MESH
ANY = pl.BlockSpec(memory_space=pl.ANY)


def _params(sem):
    return pltpu.CompilerParams(dimension_semantics=sem)


def _mm(a, b, *, name, ta=False, tb=False, add=None, out_dtype=F32, tm=512, tn=512, tk=512,
        a_split=False, b_split=False, o_split=False):
    if a_split:
        _, m, kh = a.shape
        k = 2 * kh
    elif ta:
        k, m = a.shape
    else:
        m, k = a.shape
    if b_split:
        _, kb, nh = b.shape
        n = 2 * nh
    elif tb:
        n, kb = b.shape
    else:
        kb, n = b.shape
    assert kb == k, (name, a.shape, b.shape)
    tm, tn, tk = min(tm, m), min(tn, n), min(tk, k)
    assert m % tm == 0 and n % tn == 0 and k % tk == 0, (name, m, n, k, tm, tn, tk)
    gm, gn, gk = m // tm, n // tn, k // tk

    if a_split:
        assert gk % 2 == 0
        a_spec = pl.BlockSpec((None, tm, tk), lambda i, j, kk: (kk // (gk // 2), i, kk % (gk // 2)))
    elif ta:
        a_spec = pl.BlockSpec((tk, tm), lambda i, j, kk: (kk, i))
    else:
        a_spec = pl.BlockSpec((tm, tk), lambda i, j, kk: (i, kk))
    if b_split:
        assert gn % 2 == 0
        b_spec = pl.BlockSpec((None, tk, tn), lambda i, j, kk: (j // (gn // 2), kk, j % (gn // 2)))
    elif tb:
        b_spec = pl.BlockSpec((tn, tk), lambda i, j, kk: (j, kk))
    else:
        b_spec = pl.BlockSpec((tk, tn), lambda i, j, kk: (kk, j))
    if o_split:
        assert gn % 2 == 0
        o_spec = pl.BlockSpec((None, tm, tn), lambda i, j, kk: (j // (gn // 2), i, j % (gn // 2)))
        o_shape = jax.ShapeDtypeStruct((2, m, n // 2), out_dtype)
    else:
        o_spec = pl.BlockSpec((tm, tn), lambda i, j, kk: (i, j))
        o_shape = jax.ShapeDtypeStruct((m, n), out_dtype)
    dims = (((0 if ta else 1,), (1 if tb else 0,)), ((), ()))

    def body(*refs):
        if add is None:
            a_ref, b_ref, o_ref, acc_ref = refs
            add_ref = None
        else:
            a_ref, b_ref, add_ref, o_ref, acc_ref = refs
        kk = pl.program_id(2)

        @pl.when(kk == 0)
        def _():
            acc_ref[...] = jnp.zeros_like(acc_ref)

        acc_ref[...] += lax.dot_general(a_ref[...].astype(BF16), b_ref[...].astype(BF16), dims,
                                        preferred_element_type=F32)

        @pl.when(kk == gk - 1)
        def _():
            r = acc_ref[...]
            if add_ref is not None:
                r = r + add_ref[...]
            o_ref[...] = r.astype(o_ref.dtype)

    in_specs = [a_spec, b_spec]
    args = [a, b]
    if add is not None:
        in_specs.append(pl.BlockSpec((tm, tn), lambda i, j, kk: (i, j)))
        args.append(add)
    return pl.pallas_call(
        body, name=name, grid=(gm, gn, gk), in_specs=in_specs, out_specs=o_spec, out_shape=o_shape,
        scratch_shapes=[pltpu.VMEM((tm, tn), F32)],
        compiler_params=_params(("parallel", "parallel", "arbitrary")),
    )(*args)


def _rms_scale(v):
    return lax.rsqrt(jnp.mean(v * v, axis=-1, keepdims=True) + EPS)


def _rms_bwd_rows(v, g, dy):
    r = _rms_scale(v)
    vh = v * r
    dyg = dy * g
    dv = r * (dyg - vh * jnp.mean(dyg * vh, axis=-1, keepdims=True))
    return dv, dy * vh


def _shift_down(v, first_row):
    row = lax.broadcasted_iota(jnp.int32, v.shape, 0)
    return jnp.where(row == 0, first_row, pltpu.roll(v, 1, 0))


def _shift_up(v, last_row):
    n = v.shape[0]
    row = lax.broadcasted_iota(jnp.int32, v.shape, 0)
    return jnp.where(row == n - 1, last_row, pltpu.roll(v, n - 1, 0))


def _rope(t, cos, sin_a, sin_b):
    return t * cos + pltpu.roll(t, HEAD_PAD - 16, 1) * sin_a + pltpu.roll(t, 16, 1) * sin_b


def _rope_bwd(d, cos, sin_a, sin_b):
    return d * cos + pltpu.roll(d * sin_a, 16, 1) + pltpu.roll(d * sin_b, HEAD_PAD - 16, 1)


def _sigmoid(v):
    return 1.0 / (1.0 + jnp.exp(-v))


def _halo_specs(ts, s, width, col):
    nb = ts // 8
    last = s // 8 - 1
    prev = pl.BlockSpec((8, width), lambda i: (jnp.maximum(i * nb - 1, 0), col))
    nxt = pl.BlockSpec((8, width), lambda i: (jnp.minimum((i + 1) * nb, last), col))
    return prev, nxt


def _rms_fwd(x, g, *, name, ts=512):
    s, d = x.shape

    def body(x_ref, g_ref, h_ref):
        v = x_ref[...]
        h_ref[...] = (v * _rms_scale(v) * g_ref[...]).astype(h_ref.dtype)

    return pl.pallas_call(
        body, name=name, grid=(s // ts,),
        in_specs=[pl.BlockSpec((ts, d), lambda i: (i, 0)), pl.BlockSpec((1, d), lambda i: (0, 0))],
        out_specs=pl.BlockSpec((ts, d), lambda i: (i, 0)),
        out_shape=jax.ShapeDtypeStruct((s, d), BF16),
        compiler_params=_params(("parallel",)),
    )(x, g)


def _rms_bwd(x, g, dy, add, *, name, ts=256):
    s, d = x.shape

    def body(x_ref, g_ref, dy_ref, add_ref, dx_ref, dg_ref):
        @pl.when(pl.program_id(0) == 0)
        def _():
            dg_ref[...] = jnp.zeros_like(dg_ref)

        dv, dgr = _rms_bwd_rows(x_ref[...], g_ref[...], dy_ref[...])
        dx_ref[...] = dv + add_ref[...]
        dg_ref[...] += jnp.sum(dgr, axis=0, keepdims=True)

    row = pl.BlockSpec((ts, d), lambda i: (i, 0))
    vec = pl.BlockSpec((1, d), lambda i: (0, 0))
    return pl.pallas_call(
        body, name=name, grid=(s // ts,),
        in_specs=[row, vec, row, row], out_specs=[row, vec],
        out_shape=[jax.ShapeDtypeStruct((s, d), F32), jax.ShapeDtypeStruct((1, d), F32)],
        compiler_params=_params(("arbitrary",)),
    )(x, g, dy, add)


def _mix_pre(z, conv_w8, gq, gkv, cos, sin_a, sin_b, *, ts=256):
    s = z.shape[0]
    n = s // ts
    cw = CONV_WIDTH

    def body(z_ref, xcp, xcn, cgp, cgn, w_ref, gq_ref, gkv_ref, cos_ref, sa_ref, sb_ref,
             yc_ref, qn_ref, kvn_ref, kr_ref):
        i = pl.program_id(0)
        xc = z_ref[:, 0:cw]
        bg = z_ref[:, cw:2 * cw]
        cg = z_ref[:, 2 * cw:3 * cw]
        m = cg * xc
        m_prev = jnp.where(i > 0, xcp[7:8, :] * cgp[7:8, :], 0.0)
        m_next = jnp.where(i < n - 1, xcn[0:1, :] * cgn[0:1, :], 0.0)
        cm = _shift_down(m, m_prev) * w_ref[0:1, :] + m * w_ref[1:2, :] + _shift_up(m, m_next) * w_ref[2:3, :]
        yc_ref[...] = (bg * cm).astype(BF16)
        ql = z_ref[:, 3 * cw:3 * cw + Q_LORA]
        qn_ref[...] = (ql * _rms_scale(ql) * gq_ref[...]).astype(BF16)
        kvl = z_ref[:, 3 * cw + Q_LORA:3 * cw + Q_LORA + KV_LORA]
        kvn_ref[...] = (kvl * _rms_scale(kvl) * gkv_ref[...]).astype(BF16)
        kr_ref[...] = _rope(z_ref[:, D_IN_PAD - HEAD_PAD:D_IN_PAD], cos_ref[...], sa_ref[...], sb_ref[...])

    xcp, xcn = _halo_specs(ts, s, cw, 0)
    cgp, cgn = _halo_specs(ts, s, cw, 2)
    tab = pl.BlockSpec((ts, HEAD_PAD), lambda i: (i, 0))
    return pl.pallas_call(
        body, name="mix_pre", grid=(n,),
        in_specs=[pl.BlockSpec((ts, D_IN_PAD), lambda i: (i, 0)), xcp, xcn, cgp, cgn,
                  pl.BlockSpec((8, cw), lambda i: (0, 0)), pl.BlockSpec((1, Q_LORA), lambda i: (0, 0)),
                  pl.BlockSpec((1, KV_LORA), lambda i: (0, 0)), tab, tab, tab],
        out_specs=[pl.BlockSpec((ts, cw), lambda i: (i, 0)), pl.BlockSpec((ts, Q_LORA), lambda i: (i, 0)),
                   pl.BlockSpec((ts, KV_LORA), lambda i: (i, 0)), tab],
        out_shape=[jax.ShapeDtypeStruct((s, cw), BF16), jax.ShapeDtypeStruct((s, Q_LORA), BF16),
                   jax.ShapeDtypeStruct((s, KV_LORA), BF16), jax.ShapeDtypeStruct((s, HEAD_PAD), F32)],
        compiler_params=_params(("parallel",)),
    )(z, z, z, z, z, conv_w8, gq, gkv, cos, sin_a, sin_b)


def _mix_bwd(z, dyc, dqn, dkvn, dkr, conv_w8, gq, gkv, cos, sin_a, sin_b, *, ts=256):
    s = z.shape[0]
    n = s // ts
    cw = CONV_WIDTH

    def body(z_ref, xcp, xcn, bgp, bgn, cgp, cgn, dyc_ref, dycp, dycn, dqn_ref, dkvn_ref, dkr_ref,
             w_ref, gq_ref, gkv_ref, cos_ref, sa_ref, sb_ref,
             dz_ref, dw0_ref, dw1_ref, dw2_ref, dgq_ref, dgkv_ref):
        i = pl.program_id(0)

        @pl.when(i == 0)
        def _():
            for r in (dw0_ref, dw1_ref, dw2_ref, dgq_ref, dgkv_ref):
                r[...] = jnp.zeros_like(r)

        xc = z_ref[:, 0:cw]
        bg = z_ref[:, cw:2 * cw]
        cg = z_ref[:, 2 * cw:3 * cw]
        w0, w1, w2 = w_ref[0:1, :], w_ref[1:2, :], w_ref[2:3, :]
        m = cg * xc
        m_dn = _shift_down(m, jnp.where(i > 0, xcp[7:8, :] * cgp[7:8, :], 0.0))
        m_up = _shift_up(m, jnp.where(i < n - 1, xcn[0:1, :] * cgn[0:1, :], 0.0))
        cm = m_dn * w0 + m * w1 + m_up * w2
        dyc_v = dyc_ref[...]
        dcm = dyc_v * bg
        dcm_dn = _shift_down(dcm, jnp.where(i > 0, dycp[7:8, :] * bgp[7:8, :], 0.0))
        dcm_up = _shift_up(dcm, jnp.where(i < n - 1, dycn[0:1, :] * bgn[0:1, :], 0.0))
        dm = dcm_up * w0 + dcm * w1 + dcm_dn * w2
        dz_ref[:, 0:cw] = (dm * cg).astype(BF16)
        dz_ref[:, cw:2 * cw] = (dyc_v * cm).astype(BF16)
        dz_ref[:, 2 * cw:3 * cw] = (dm * xc).astype(BF16)
        dw0_ref[...] += jnp.sum(dcm * m_dn, axis=0, keepdims=True)
        dw1_ref[...] += jnp.sum(dcm * m, axis=0, keepdims=True)
        dw2_ref[...] += jnp.sum(dcm * m_up, axis=0, keepdims=True)

        dql, dgq_rows = _rms_bwd_rows(z_ref[:, 3 * cw:3 * cw + Q_LORA], gq_ref[...], dqn_ref[...])
        dz_ref[:, 3 * cw:3 * cw + Q_LORA] = dql.astype(BF16)
        dgq_ref[...] += jnp.sum(dgq_rows, axis=0, keepdims=True)
        dkvl, dgkv_rows = _rms_bwd_rows(z_ref[:, 3 * cw + Q_LORA:3 * cw + Q_LORA + KV_LORA], gkv_ref[...],
                                        dkvn_ref[...])
        dz_ref[:, 3 * cw + Q_LORA:3 * cw + Q_LORA + KV_LORA] = dkvl.astype(BF16)
        dgkv_ref[...] += jnp.sum(dgkv_rows, axis=0, keepdims=True)

        lane = lax.broadcasted_iota(jnp.int32, (ts, HEAD_PAD), 1)
        rope_lane = (lane >= QK_NOPE) & (lane < QK_NOPE + QK_ROPE)
        dk = _rope_bwd(dkr_ref[...], cos_ref[...], sa_ref[...], sb_ref[...])
        dz_ref[:, D_IN_PAD - HEAD_PAD:D_IN_PAD] = jnp.where(rope_lane, dk, 0.0).astype(BF16)

    xcp, xcn = _halo_specs(ts, s, cw, 0)
    bgp, bgn = _halo_specs(ts, s, cw, 1)
    cgp, cgn = _halo_specs(ts, s, cw, 2)
    dycp, dycn = _halo_specs(ts, s, cw, 0)
    tab = pl.BlockSpec((ts, HEAD_PAD), lambda i: (i, 0))

    def vec(width):
        return pl.BlockSpec((1, width), lambda i: (0, 0))

    outs = pl.pallas_call(
        body, name="mix_bwd", grid=(n,),
        in_specs=[pl.BlockSpec((ts, D_IN_PAD), lambda i: (i, 0)), xcp, xcn, bgp, bgn, cgp, cgn,
                  pl.BlockSpec((ts, cw), lambda i: (i, 0)), dycp, dycn,
                  pl.BlockSpec((ts, Q_LORA), lambda i: (i, 0)), pl.BlockSpec((ts, KV_LORA), lambda i: (i, 0)), tab,
                  pl.BlockSpec((8, cw), lambda i: (0, 0)), vec(Q_LORA), vec(KV_LORA), tab, tab, tab],
        out_specs=[pl.BlockSpec((ts, D_IN_PAD), lambda i: (i, 0)), vec(cw), vec(cw), vec(cw), vec(Q_LORA),
                   vec(KV_LORA)],
        out_shape=[jax.ShapeDtypeStruct((s, D_IN_PAD), BF16)] + [jax.ShapeDtypeStruct((1, cw), F32)] * 3
        + [jax.ShapeDtypeStruct((1, Q_LORA), F32), jax.ShapeDtypeStruct((1, KV_LORA), F32)],
        compiler_params=_params(("arbitrary",)),
    )(z, z, z, z, z, z, z, dyc, dyc, dyc, dqn, dkvn, dkr, conv_w8, gq, gkv, cos, sin_a, sin_b)
    dz, dw0, dw1, dw2, dgq, dgkv = outs
    return dz, jnp.concatenate([dw0, dw1, dw2], axis=0), dgq, dgkv


def _qkv_proj(qn, kvn, kr, w_uq_p, w_kv_p, cos, sin_a, sin_b, *, ts=512):
    s = qn.shape[0]

    def body(qn_ref, kvn_ref, kr_ref, wq_ref, wkv_ref, cos_ref, sa_ref, sb_ref, q_ref, k_ref, v_ref):
        cos_v, sa, sb = cos_ref[...], sa_ref[...], sb_ref[...]
        q = jnp.dot(qn_ref[...], wq_ref[...], preferred_element_type=F32)
        kv = jnp.dot(kvn_ref[...], wkv_ref[...], preferred_element_type=F32)
        kr_v = kr_ref[...]
        for h in range(N_HEADS):
            blk = slice(h * HEAD_PAD, (h + 1) * HEAD_PAD)
            q_ref[:, blk] = _rope(q[:, blk], cos_v, sa, sb).astype(BF16)
            k_ref[:, blk] = (kv[:, blk] + kr_v).astype(BF16)
        v_ref[...] = kv[:, D_ATT:].astype(BF16)

    tab = pl.BlockSpec((ts, HEAD_PAD), lambda i: (i, 0))
    wide = pl.BlockSpec((ts, D_ATT), lambda i: (i, 0))
    return pl.pallas_call(
        body, name="qkv_proj", grid=(s // ts,),
        in_specs=[pl.BlockSpec((ts, Q_LORA), lambda i: (i, 0)), pl.BlockSpec((ts, KV_LORA), lambda i: (i, 0)), tab,
                  pl.BlockSpec((Q_LORA, D_ATT), lambda i: (0, 0)), pl.BlockSpec((KV_LORA, 2 * D_ATT), lambda i: (0, 0)),
                  tab, tab, tab],
        out_specs=[wide, wide, wide],
        out_shape=[jax.ShapeDtypeStruct((s, D_ATT), BF16)] * 3,
        compiler_params=_params(("parallel",)),
    )(qn, kvn, kr, w_uq_p, w_kv_p, cos, sin_a, sin_b)


def _qk_bwd(dq, dk, cos, sin_a, sin_b, *, ts=256):
    s = dq.shape[0]

    def body(dq_ref, dk_ref, cos_ref, sa_ref, sb_ref, dqp_ref, dkr_ref):
        cos_v, sa, sb = cos_ref[...], sa_ref[...], sb_ref[...]
        tot = jnp.zeros((ts, HEAD_PAD), F32)
        for h in range(N_HEADS):
            blk = slice(h * HEAD_PAD, (h + 1) * HEAD_PAD)
            dqp_ref[:, blk] = _rope_bwd(dq_ref[:, blk], cos_v, sa, sb).astype(BF16)
            tot = tot + dk_ref[:, blk]
        dkr_ref[...] = tot

    tab = pl.BlockSpec((ts, HEAD_PAD), lambda i: (i, 0))
    wide = pl.BlockSpec((ts, D_ATT), lambda i: (i, 0))
    return pl.pallas_call(
        body, name="qk_bwd", grid=(s // ts,),
        in_specs=[wide, wide, tab, tab, tab], out_specs=[wide, tab],
        out_shape=[jax.ShapeDtypeStruct((s, D_ATT), BF16), jax.ShapeDtypeStruct((s, HEAD_PAD), F32)],
        compiler_params=_params(("parallel",)),
    )(dq, dk, cos, sin_a, sin_b)


_NT = (((1,), (1,)), ((), ()))
_TN = (((0,), (0,)), ((), ()))


def _flash_fwd(q, k, v, *, tq=512, tk=512):
    s = q.shape[0]
    tq, tk = min(tq, s), min(tk, s)
    nk = s // tk

    def body(q_ref, k_ref, v_ref, o_ref, lse_ref):
        qv = q_ref[...]

        def step(j, carry):
            m, l, acc = carry
            rows = pl.ds(pl.multiple_of(j * tk, tk), tk)
            sc = lax.dot_general(qv, k_ref[rows, :], _NT, preferred_element_type=F32) * SM_SCALE
            m_new = jnp.maximum(m, jnp.max(sc, axis=1, keepdims=True))
            alpha = jnp.exp(m - m_new)
            p = jnp.exp(sc - m_new)
            l = alpha * l + jnp.sum(p, axis=1, keepdims=True)
            acc = alpha * acc + jnp.dot(p.astype(BF16), v_ref[rows, :], preferred_element_type=F32)
            return m_new, l, acc

        init = (jnp.full((tq, 1), -jnp.inf, F32), jnp.zeros((tq, 1), F32), jnp.zeros((tq, HEAD_PAD), F32))
        m, l, acc = lax.fori_loop(0, nk, step, init)
        o_ref[...] = (acc / l).astype(BF16)
        lse_ref[...] = m + jnp.log(l)

    head = pl.BlockSpec((s, HEAD_PAD), lambda h, i: (0, h))
    return pl.pallas_call(
        body, name="flash_fwd", grid=(N_HEADS, s // tq),
        in_specs=[pl.BlockSpec((tq, HEAD_PAD), lambda h, i: (i, h)), head, head],
        out_specs=[pl.BlockSpec((tq, HEAD_PAD), lambda h, i: (i, h)),
                   pl.BlockSpec((None, tq, 1), lambda h, i: (h, i, 0))],
        out_shape=[jax.ShapeDtypeStruct((s, D_ATT), BF16), jax.ShapeDtypeStruct((N_HEADS, s, 1), F32)],
        compiler_params=_params(("parallel", "parallel")),
    )(q, k, v)


def _attn_delta(do, o, *, ts=512):
    s = do.shape[0]

    def body(do_ref, o_ref, dl_ref):
        for h in range(N_HEADS):
            blk = slice(h * HEAD_PAD, (h + 1) * HEAD_PAD)
            dl_ref[h] = jnp.sum(do_ref[:, blk].astype(F32) * o_ref[:, blk].astype(F32), axis=1, keepdims=True)

    wide = pl.BlockSpec((ts, D_ATT), lambda i: (i, 0))
    return pl.pallas_call(
        body, name="attn_delta", grid=(s // ts,), in_specs=[wide, wide],
        out_specs=pl.BlockSpec((N_HEADS, ts, 1), lambda i: (0, i, 0)),
        out_shape=jax.ShapeDtypeStruct((N_HEADS, s, 1), F32),
        compiler_params=_params(("parallel",)),
    )(do, o)


def _flash_bwd(q, k, v, do, lse, delta, *, tq=512, tk=512):
    s = q.shape[0]
    tq, tk = min(tq, s), min(tk, s)
    nq = s // tq

    def body(q_ref, do_ref, lse_ref, dl_ref, k_ref, v_ref, dq_ref, dk_ref, dv_ref):
        @pl.when(pl.program_id(1) == 0)
        def _():
            dq_ref[...] = jnp.zeros_like(dq_ref)

        kv, vv = k_ref[...], v_ref[...]

        def step(i, carry):
            dk, dv = carry
            rows = pl.ds(pl.multiple_of(i * tq, tq), tq)
            qv, dov = q_ref[rows, :], do_ref[rows, :]
            sc = lax.dot_general(qv, kv, _NT, preferred_element_type=F32) * SM_SCALE
            p = jnp.exp(sc - lse_ref[rows, :])
            dp = lax.dot_general(dov, vv, _NT, preferred_element_type=F32)
            ds = (p * (dp - dl_ref[rows, :]) * SM_SCALE).astype(BF16)
            dv = dv + lax.dot_general(p.astype(BF16), dov, _TN, preferred_element_type=F32)
            dk = dk + lax.dot_general(ds, qv, _TN, preferred_element_type=F32)
            dq_ref[rows, :] += jnp.dot(ds, kv, preferred_element_type=F32)
            return dk, dv

        zero = jnp.zeros((tk, HEAD_PAD), F32)
        dk, dv = lax.fori_loop(0, nq, step, (zero, zero))
        dk_ref[...] = dk
        dv_ref[...] = dv

    head = pl.BlockSpec((s, HEAD_PAD), lambda h, j: (0, h))
    stat = pl.BlockSpec((None, s, 1), lambda h, j: (h, 0, 0))
    blk = pl.BlockSpec((tk, HEAD_PAD), lambda h, j: (j, h))
    return pl.pallas_call(
        body, name="flash_bwd", grid=(N_HEADS, s // tk),
        in_specs=[head, head, stat, stat, blk, blk],
        out_specs=[head, blk, blk],
        out_shape=[jax.ShapeDtypeStruct((s, D_ATT), F32)] * 3,
        compiler_params=_params(("parallel", "arbitrary")),
    )(q, do, lse, delta, k, v)


FFN_TC = 256


def _ffn_specs(ts, s, tc, row_axis):
    nb = ts // 8
    last = s // 8 - 1
    if row_axis == 0:
        main = pl.BlockSpec((2, ts, tc), lambda i, j: (0, i, j))
        prev = pl.BlockSpec((2, 8, tc), lambda i, j: (0, jnp.maximum(i * nb - 1, 0), j))
        nxt = pl.BlockSpec((2, 8, tc), lambda i, j: (0, jnp.minimum((i + 1) * nb, last), j))
    else:
        main = pl.BlockSpec((2, ts, tc), lambda j, i: (0, i, j))
        prev = pl.BlockSpec((2, 8, tc), lambda j, i: (0, jnp.maximum(i * nb - 1, 0), j))
        nxt = pl.BlockSpec((2, 8, tc), lambda j, i: (0, jnp.minimum((i + 1) * nb, last), j))
    return main, prev, nxt


def _ffn_conv(a_ref, ap_ref, an_ref, w_ref, b_ref, half, first, last):
    a = a_ref[half]
    a_dn = _shift_down(a, jnp.where(first, 0.0, ap_ref[half, 7:8, :]))
    a_up = _shift_up(a, jnp.where(last, 0.0, an_ref[half, 0:1, :]))
    w = w_ref[half]
    out = a_dn * w[0:1, :] + a * w[1:2, :] + a_up * w[2:3, :] + b_ref[half]
    return out, a_dn, a, a_up


def _ffn_act(a_pre, w, b, *, ts=512, tc=FFN_TC):
    s = a_pre.shape[1]
    n = s // ts

    def body(a_ref, ap_ref, an_ref, w_ref, b_ref, o_ref):
        i = pl.program_id(0)
        g = _ffn_conv(a_ref, ap_ref, an_ref, w_ref, b_ref, 0, i == 0, i == n - 1)[0]
        u = _ffn_conv(a_ref, ap_ref, an_ref, w_ref, b_ref, 1, i == 0, i == n - 1)[0]
        o_ref[...] = (g * _sigmoid(g) * u).astype(BF16)

    main, prev, nxt = _ffn_specs(ts, s, tc, 0)
    return pl.pallas_call(
        body, name="ffn_act", grid=(n, D_FF // tc),
        in_specs=[main, prev, nxt, pl.BlockSpec((2, 8, tc), lambda i, j: (0, 0, j)),
                  pl.BlockSpec((2, 1, tc), lambda i, j: (0, 0, j))],
        out_specs=pl.BlockSpec((ts, tc), lambda i, j: (i, j)),
        out_shape=jax.ShapeDtypeStruct((s, D_FF), BF16),
        compiler_params=_params(("parallel", "parallel")),
    )(a_pre, a_pre, a_pre, w, b)


def _ffn_act_bwd(a_pre, dact, w, b, *, ts=512, tc=FFN_TC):
    s = a_pre.shape[1]
    n = s // ts

    def body(a_ref, ap_ref, an_ref, dact_ref, w_ref, b_ref, da_ref, dw_ref, db_ref):
        i = pl.program_id(1)

        @pl.when(i == 0)
        def _():
            dw_ref[...] = jnp.zeros_like(dw_ref)
            db_ref[...] = jnp.zeros_like(db_ref)

        g, g_dn, g_c, g_up = _ffn_conv(a_ref, ap_ref, an_ref, w_ref, b_ref, 0, i == 0, i == n - 1)
        u, u_dn, u_c, u_up = _ffn_conv(a_ref, ap_ref, an_ref, w_ref, b_ref, 1, i == 0, i == n - 1)
        dact_v = dact_ref[...]
        sg = _sigmoid(g)
        dg = dact_v * u * (sg * (1.0 + g * (1.0 - sg)))
        du = dact_v * (g * sg)
        da_ref[0] = dg
        da_ref[1] = du
        for half, (d, dn, c, up) in enumerate(((dg, g_dn, g_c, g_up), (du, u_dn, u_c, u_up))):
            dw_ref[half, 0:1, :] += jnp.sum(d * dn, axis=0, keepdims=True)
            dw_ref[half, 1:2, :] += jnp.sum(d * c, axis=0, keepdims=True)
            dw_ref[half, 2:3, :] += jnp.sum(d * up, axis=0, keepdims=True)
            db_ref[half] += jnp.sum(d, axis=0, keepdims=True)

    main, prev, nxt = _ffn_specs(ts, s, tc, 1)
    return pl.pallas_call(
        body, name="ffn_act_bwd", grid=(D_FF // tc, n),
        in_specs=[main, prev, nxt, pl.BlockSpec((ts, tc), lambda j, i: (i, j)),
                  pl.BlockSpec((2, 8, tc), lambda j, i: (0, 0, j)), pl.BlockSpec((2, 1, tc), lambda j, i: (0, 0, j))],
        out_specs=[main, pl.BlockSpec((2, 8, tc), lambda j, i: (0, 0, j)),
                   pl.BlockSpec((2, 1, tc), lambda j, i: (0, 0, j))],
        out_shape=[jax.ShapeDtypeStruct((2, s, D_FF), F32), jax.ShapeDtypeStruct((2, 8, D_FF), F32),
                   jax.ShapeDtypeStruct((2, 1, D_FF), F32)],
        compiler_params=_params(("parallel", "arbitrary")),
    )(a_pre, a_pre, a_pre, dact, w, b)


def _ffn_conv_t(da, w, *, ts=512, tc=FFN_TC):
    s = da.shape[1]
    n = s // ts

    def body(d_ref, dp_ref, dn_ref, w_ref, o_ref):
        i = pl.program_id(0)
        for half in range(2):
            d = d_ref[half]
            d_dn = _shift_down(d, jnp.where(i == 0, 0.0, dp_ref[half, 7:8, :]))
            d_up = _shift_up(d, jnp.where(i == n - 1, 0.0, dn_ref[half, 0:1, :]))
            wv = w_ref[half]
            o_ref[half] = (d_up * wv[0:1, :] + d * wv[1:2, :] + d_dn * wv[2:3, :]).astype(BF16)

    main, prev, nxt = _ffn_specs(ts, s, tc, 0)
    return pl.pallas_call(
        body, name="ffn_conv_t", grid=(n, D_FF // tc),
        in_specs=[main, prev, nxt, pl.BlockSpec((2, 8, tc), lambda i, j: (0, 0, j))],
        out_specs=main, out_shape=jax.ShapeDtypeStruct((2, s, D_FF), BF16),
        compiler_params=_params(("parallel", "parallel")),
    )(da, da, da, w)


def _ple_final(x2, gl, pp, target, gf, *, ts=256):
    s, d = x2.shape

    def body(x2_ref, gl_ref, pp_ref, t_ref, gf_ref, loss_ref, dx3_ref, dgl_ref, dpp_ref, dgf_ref):
        @pl.when(pl.program_id(0) == 0)
        def _():
            loss_ref[...] = jnp.zeros_like(loss_ref)
            dgf_ref[...] = jnp.zeros_like(dgf_ref)

        gate = _sigmoid(gl_ref[...])
        ppv = pp_ref[...]
        x3 = x2_ref[...] + gate * ppv
        gfv = gf_ref[...]
        err = x3 * _rms_scale(x3) * gfv - t_ref[...]
        loss_ref[...] += 0.5 * jnp.sum(jnp.mean(err * err, axis=-1, keepdims=True), axis=0, keepdims=True)
        dx3, dgf_rows = _rms_bwd_rows(x3, gfv, err * (1.0 / d))
        dgf_ref[...] += jnp.sum(dgf_rows, axis=0, keepdims=True)
        dx3_ref[...] = dx3
        dgl_ref[...] = (dx3 * ppv * gate * (1.0 - gate)).astype(BF16)
        dpp_ref[...] = (dx3 * gate).astype(BF16)

    row = pl.BlockSpec((ts, d), lambda i: (i, 0))
    vec = pl.BlockSpec((1, d), lambda i: (0, 0))
    return pl.pallas_call(
        body, name="ple_final", grid=(s // ts,),
        in_specs=[row, row, row, row, vec],
        out_specs=[pl.BlockSpec((1, 128), lambda i: (0, 0)), row, row, row, vec],
        out_shape=[jax.ShapeDtypeStruct((1, 128), F32), jax.ShapeDtypeStruct((s, d), F32),
                   jax.ShapeDtypeStruct((s, d), BF16), jax.ShapeDtypeStruct((s, d), BF16),
                   jax.ShapeDtypeStruct((1, d), F32)],
        compiler_params=_params(("arbitrary",)),
    )(x2, gl, pp, target, gf)


def _row_tile(rows, cols, n_arrays, budget=12 << 20):
    best = None
    for t in range(8, rows + 1, 8):
        if rows % t == 0 and t * cols * 4 * n_arrays <= budget:
            best = t
    return rows if best is None else best


def _add2(a, b, *, name):
    g, r, c = a.shape
    tr = _row_tile(r, c, 3)

    def body(a_ref, b_ref, o_ref):
        o_ref[...] = a_ref[...] + b_ref[...]

    blk = pl.BlockSpec((None, tr, c), lambda i, j: (i, j, 0))
    return pl.pallas_call(
        body, name=name, grid=(g, r // tr), in_specs=[blk, blk], out_specs=blk,
        out_shape=jax.ShapeDtypeStruct(a.shape, a.dtype), compiler_params=_params(("parallel", "parallel")),
    )(a, b)


def _sum_slots(a, *, name):
    g, r, c = a.shape
    tr = _row_tile(r, c, g + 1)

    def body(*refs):
        tot = refs[0][...]
        for ref in refs[1:g]:
            tot = tot + ref[...]
        refs[g][...] = tot

    specs = [pl.BlockSpec((None, tr, c), functools.partial(lambda i, slot: (slot, i, 0), slot=k)) for k in range(g)]
    return pl.pallas_call(
        body, name=name, grid=(r // tr,), in_specs=specs, out_specs=pl.BlockSpec((tr, c), lambda i: (i, 0)),
        out_shape=jax.ShapeDtypeStruct((r, c), a.dtype), compiler_params=_params(("parallel",)),
    )(*([a] * g))


def _adamw(w, g, m, v, *, name):
    r, c = w.shape
    tr = _row_tile(r, c, 7)

    def body(w_ref, g_ref, m_ref, v_ref, d_ref, mo_ref, vo_ref):
        gv = g_ref[...]
        mn = ADAM_B1 * m_ref[...] + (1.0 - ADAM_B1) * gv
        vn = ADAM_B2 * v_ref[...] + (1.0 - ADAM_B2) * (gv * gv)
        m_hat = mn / (1.0 - ADAM_B1 ** ADAM_STEP)
        v_hat = vn / (1.0 - ADAM_B2 ** ADAM_STEP)
        d_ref[...] = -ADAM_LR * (m_hat / (jnp.sqrt(v_hat) + ADAM_EPS) + ADAM_WD * w_ref[...])
        mo_ref[...] = mn
        vo_ref[...] = vn

    blk = pl.BlockSpec((tr, c), lambda i: (i, 0))
    return pl.pallas_call(
        body, name=name, grid=(r // tr,), in_specs=[blk] * 4, out_specs=[blk] * 3,
        out_shape=[jax.ShapeDtypeStruct((r, c), F32)] * 3, compiler_params=_params(("parallel",)),
    )(w, g, m, v)


def _position():
    x, y, c = lax.axis_index("x"), lax.axis_index("y"), lax.axis_index("c")
    return x, y, c


def _other_chips(x, y):
    return [(1 - x, y), (x, 1 - y), (1 - x, 1 - y)]


def _gather_chips(shards):
    n = len(shards)

    def body(*refs):
        ins, outs = refs[:n], refs[n:2 * n]
        send_sems, recv_sems, local_sems = refs[2 * n:]
        x, y, c = _position()
        me = 2 * x + y
        chips = _other_chips(x, y)
        local, remote = [], []
        for a in range(n):
            cp = pltpu.make_async_copy(ins[a], outs[a].at[me], local_sems.at[a])
            cp.start()
            local.append(cp)
            for k, (px, py) in enumerate(chips):
                rc = pltpu.make_async_remote_copy(
                    src_ref=ins[a], dst_ref=outs[a].at[me], send_sem=send_sems.at[3 * a + k],
                    recv_sem=recv_sems.at[3 * a + k], device_id=(px, py, c), device_id_type=MESH)
                rc.start()
                remote.append(rc)
        for a in range(n):
            for k, (px, py) in enumerate(chips):
                pltpu.make_async_remote_copy(
                    src_ref=ins[a], dst_ref=outs[a].at[2 * px + py], send_sem=send_sems.at[3 * a + k],
                    recv_sem=recv_sems.at[3 * a + k], device_id=(px, py, c), device_id_type=MESH).wait_recv()
        for rc in remote:
            rc.wait_send()
        for cp in local:
            cp.wait()

    return pl.pallas_call(
        body, name="gather_chips", in_specs=[ANY] * n, out_specs=[ANY] * n,
        out_shape=[jax.ShapeDtypeStruct((N_CHIPS,) + s.shape, s.dtype) for s in shards],
        scratch_shapes=[pltpu.SemaphoreType.DMA((3 * n,)), pltpu.SemaphoreType.DMA((3 * n,)),
                        pltpu.SemaphoreType.DMA((n,))],
        compiler_params=pltpu.CompilerParams(has_side_effects=True),
    )(*shards)


def _swap_halves(grads):
    n = len(grads)

    def body(*refs):
        ins, mine, sib = refs[:n], refs[n:2 * n], refs[2 * n:3 * n]
        send_sems, recv_sems, local_sems = refs[3 * n:]
        x, y, c = _position()
        local, remote = [], []
        for a in range(n):
            half = ins[a].shape[1] // 2
            keep = ins[a].at[:, pl.ds(pl.multiple_of(c * half, 8), half), :]
            give = ins[a].at[:, pl.ds(pl.multiple_of((1 - c) * half, 8), half), :]
            cp = pltpu.make_async_copy(keep, mine[a], local_sems.at[a])
            cp.start()
            local.append(cp)
            rc = pltpu.make_async_remote_copy(
                src_ref=give, dst_ref=sib[a], send_sem=send_sems.at[a], recv_sem=recv_sems.at[a],
                device_id=(x, y, 1 - c), device_id_type=MESH)
            rc.start()
            remote.append(rc)
        for rc in remote:
            rc.wait_recv()
        for rc in remote:
            rc.wait_send()
        for cp in local:
            cp.wait()

    half_shapes = [jax.ShapeDtypeStruct((g.shape[0], g.shape[1] // 2, g.shape[2]), g.dtype) for g in grads]
    outs = pl.pallas_call(
        body, name="swap_halves", in_specs=[ANY] * n, out_specs=[ANY] * (2 * n),
        out_shape=half_shapes + half_shapes,
        scratch_shapes=[pltpu.SemaphoreType.DMA((n,)), pltpu.SemaphoreType.DMA((n,)), pltpu.SemaphoreType.DMA((n,))],
        compiler_params=pltpu.CompilerParams(has_side_effects=True),
    )(*grads)
    return outs[:n], outs[n:]


def _scatter_chips(parts):
    n = len(parts)

    def body(*refs):
        ins, outs = refs[:n], refs[n:2 * n]
        send_sems, recv_sems, local_sems = refs[2 * n:]
        x, y, c = _position()
        me = 2 * x + y
        chips = _other_chips(x, y)
        local, remote = [], []
        for a in range(n):
            cp = pltpu.make_async_copy(ins[a].at[me], outs[a].at[me], local_sems.at[a])
            cp.start()
            local.append(cp)
            for k, (px, py) in enumerate(chips):
                rc = pltpu.make_async_remote_copy(
                    src_ref=ins[a].at[2 * px + py], dst_ref=outs[a].at[me], send_sem=send_sems.at[3 * a + k],
                    recv_sem=recv_sems.at[3 * a + k], device_id=(px, py, c), device_id_type=MESH)
                rc.start()
                remote.append(rc)
        for a in range(n):
            for k, (px, py) in enumerate(chips):
                pltpu.make_async_remote_copy(
                    src_ref=ins[a].at[me], dst_ref=outs[a].at[2 * px + py], send_sem=send_sems.at[3 * a + k],
                    recv_sem=recv_sems.at[3 * a + k], device_id=(px, py, c), device_id_type=MESH).wait_recv()
        for rc in remote:
            rc.wait_send()
        for cp in local:
            cp.wait()

    return pl.pallas_call(
        body, name="scatter_chips", in_specs=[ANY] * n, out_specs=[ANY] * n,
        out_shape=[jax.ShapeDtypeStruct(p.shape, p.dtype) for p in parts],
        scratch_shapes=[pltpu.SemaphoreType.DMA((3 * n,)), pltpu.SemaphoreType.DMA((3 * n,)),
                        pltpu.SemaphoreType.DMA((n,))],
        compiler_params=pltpu.CompilerParams(has_side_effects=True),
    )(*parts)


def _join_halves(halves):
    n = len(halves)

    def body(*refs):
        ins, outs = refs[:n], refs[n:2 * n]
        send_sems, recv_sems, local_sems = refs[2 * n:]
        x, y, c = _position()
        local, remote = [], []
        for a in range(n):
            cp = pltpu.make_async_copy(ins[a], outs[a].at[c], local_sems.at[a])
            cp.start()
            local.append(cp)
            rc = pltpu.make_async_remote_copy(
                src_ref=ins[a], dst_ref=outs[a].at[c], send_sem=send_sems.at[a], recv_sem=recv_sems.at[a],
                device_id=(x, y, 1 - c), device_id_type=MESH)
            rc.start()
            remote.append(rc)
        for a in range(n):
            pltpu.make_async_remote_copy(
                src_ref=ins[a], dst_ref=outs[a].at[1 - c], send_sem=send_sems.at[a], recv_sem=recv_sems.at[a],
                device_id=(x, y, 1 - c), device_id_type=MESH).wait_recv()
        for rc in remote:
            rc.wait_send()
        for cp in local:
            cp.wait()

    return pl.pallas_call(
        body, name="join_halves", in_specs=[ANY] * n, out_specs=[ANY] * n,
        out_shape=[jax.ShapeDtypeStruct((2,) + h.shape, h.dtype) for h in halves],
        scratch_shapes=[pltpu.SemaphoreType.DMA((n,)), pltpu.SemaphoreType.DMA((n,)), pltpu.SemaphoreType.DMA((n,))],
        compiler_params=pltpu.CompilerParams(has_side_effects=True),
    )(*halves)


def _gather_all(buf):
    def body(in_ref, out_ref, send_sems, recv_sems, local_sem):
        x, y, c = _position()
        me = 4 * x + 2 * y + c
        peers = [(x, y, 1 - c)] + [(px, py, pc) for (px, py) in _other_chips(x, y) for pc in (c, 1 - c)]
        cp = pltpu.make_async_copy(in_ref, out_ref.at[me], local_sem)
        cp.start()
        remote = []
        for k, peer in enumerate(peers):
            rc = pltpu.make_async_remote_copy(
                src_ref=in_ref, dst_ref=out_ref.at[me], send_sem=send_sems.at[k], recv_sem=recv_sems.at[k],
                device_id=peer, device_id_type=MESH)
            rc.start()
            remote.append(rc)
        for k, (px, py, pc) in enumerate(peers):
            pltpu.make_async_remote_copy(
                src_ref=in_ref, dst_ref=out_ref.at[4 * px + 2 * py + pc], send_sem=send_sems.at[k],
                recv_sem=recv_sems.at[k], device_id=(px, py, pc), device_id_type=MESH).wait_recv()
        for rc in remote:
            rc.wait_send()
        cp.wait()

    return pl.pallas_call(
        body, name="gather_all", in_specs=[ANY], out_specs=ANY,
        out_shape=jax.ShapeDtypeStruct((N_DEV,) + buf.shape, buf.dtype),
        scratch_shapes=[pltpu.SemaphoreType.DMA((N_DEV - 1,)), pltpu.SemaphoreType.DMA((N_DEV - 1,)),
                        pltpu.SemaphoreType.DMA],
        compiler_params=pltpu.CompilerParams(has_side_effects=True),
    )(buf)


def _cols_from_shards(g4):
    _, k, n = g4.shape
    return g4.transpose(1, 0, 2).reshape(k, N_CHIPS * n)


def _cols_to_shards(w):
    k, n = w.shape
    return w.reshape(k, N_CHIPS, n // N_CHIPS).transpose(1, 0, 2)


def _pad_heads(w, width):
    k = w.shape[0]
    w3 = w.reshape(k, N_HEADS, width)
    return jnp.pad(w3, ((0, 0), (0, 0), (0, HEAD_PAD - width))).reshape(k, D_ATT)


def _unpad_heads(w, width):
    k = w.shape[0]
    return w.reshape(k, N_HEADS, HEAD_PAD)[:, :, :width]


def _rope_tables(s):
    pos = jnp.arange(s, dtype=F32)
    inv_freq = ROPE_THETA ** (-jnp.arange(0, QK_ROPE, 2, dtype=F32) / QK_ROPE)
    ang = pos[:, None] * inv_freq[None, :]
    cos_h, sin_h = jnp.cos(ang), jnp.sin(ang)
    half = QK_ROPE // 2
    z = jnp.zeros((s, half), F32)
    ones = jnp.ones((s, QK_NOPE), F32)
    tail = jnp.zeros((s, HEAD_PAD - QK_NOPE - QK_ROPE), F32)
    cos = jnp.concatenate([ones, cos_h, cos_h, tail + 1.0], axis=1)
    sin_a = jnp.concatenate([ones * 0.0, -sin_h, z, tail], axis=1)
    sin_b = jnp.concatenate([ones * 0.0, z, sin_h, tail], axis=1)
    return cos, sin_a, sin_b


def _local_step(x, p, target, wts):
    s = x.shape[0]
    cos, sin_a, sin_b = _rope_tables(s)
    g1, gq, gkv, g2, g3, gf = (wts[k] for k in ("norm_mix_g", "q_norm_g", "kv_norm_g", "norm_ffn_g", "ple_norm_g",
                                                 "final_norm_g"))
    w_in_p, w_uq_p, w_kv_p = wts["w_in_p"], wts["w_uq_p"], wts["w_kv_p"]
    w_o_a, w_o_b, w_up, w_down = wts["w_o_a"], wts["w_o_b"], wts["w_up"], wts["w_down"]
    w_pg, w_pp = wts["w_ple_gate"], wts["w_ple_proj"]
    conv_w8, fconv_w, fconv_b = wts["conv_w8"], wts["ffn_conv_w"], wts["ffn_conv_b"]

    h = _rms_fwd(x, g1, name="rms_mix")
    z = _mm(h, w_in_p, name="mm_in", tm=512, tn=1024, tk=1024)
    y_conv, qn, kvn, kr = _mix_pre(z, conv_w8, gq, gkv, cos, sin_a, sin_b)
    q, k, v = _qkv_proj(qn, kvn, kr, w_uq_p, w_kv_p, cos, sin_a, sin_b)
    o, lse = _flash_fwd(q, k, v)
    t = _mm(y_conv, w_o_a, add=x, name="mm_o_conv", tm=512, tn=1024, tk=512)
    x1 = _mm(o, w_o_b, add=t, name="mm_o_att", tm=512, tn=1024, tk=1024)
    hf = _rms_fwd(x1, g2, name="rms_ffn")
    a_pre = _mm(hf, w_up, o_split=True, name="mm_up", tm=1024, tn=1408, tk=1024)
    act = _ffn_act(a_pre, fconv_w, fconv_b)
    x2 = _mm(act, w_down, add=x1, name="mm_down", tm=512, tn=1024, tk=1408)
    n3 = _rms_fwd(x2, g3, name="rms_ple")
    gl = _mm(n3, w_pg, name="mm_gate", tm=512, tn=1024, tk=1024)
    pp = _mm(p, w_pp, name="mm_ple", tm=512, tn=1024, tk=256)
    loss, dx3, dgl, dpp, d_gf = _ple_final(x2, gl, pp, target, gf)

    grads = {"final_norm_g": d_gf}
    grads["w_ple_proj"] = _mm(p, dpp, ta=True, name="mm_d_wpp", tm=256, tn=1024, tk=1024)
    grads["w_ple_gate"] = _mm(n3, dgl, ta=True, name="mm_d_wpg", tm=512, tn=1024, tk=1024)
    dn3 = _mm(dgl, w_pg, tb=True, name="mm_d_n3", tm=512, tn=1024, tk=1024)
    dx2, grads["ple_norm_g"] = _rms_bwd(x2, g3, dn3, dx3, name="rms_ple_bwd")
    grads["w_down"] = _mm(act, dx2, ta=True, name="mm_d_wdown", tm=1408, tn=512, tk=1024)
    dact = _mm(dx2, w_down, tb=True, name="mm_d_act", tm=512, tn=1408, tk=1024)
    da, d_fconv_w, d_fconv_b = _ffn_act_bwd(a_pre, dact, fconv_w, fconv_b)
    grads["ffn_conv_w"], grads["ffn_conv_b"] = d_fconv_w, d_fconv_b
    da_pre = _ffn_conv_t(da, fconv_w)
    grads["w_up"] = _mm(hf, da_pre, ta=True, b_split=True, name="mm_d_wup", tm=512, tn=1408, tk=1024)
    dhf = _mm(da_pre, w_up, tb=True, a_split=True, name="mm_d_hf", tm=512, tn=1024, tk=1408)
    dx1, grads["norm_ffn_g"] = _rms_bwd(x1, g2, dhf, dx2, name="rms_ffn_bwd")
    grads["w_o_a"] = _mm(y_conv, dx1, ta=True, name="mm_d_wo_conv", tm=512, tn=1024, tk=1024)
    grads["w_o_b"] = _mm(o, dx1, ta=True, name="mm_d_wo_att", tm=512, tn=1024, tk=1024)
    dyc = _mm(dx1, w_o_a, tb=True, name="mm_d_yconv", tm=512, tn=512, tk=1024)
    do = _mm(dx1, w_o_b, tb=True, out_dtype=BF16, name="mm_d_o", tm=512, tn=1024, tk=1024)
    delta = _attn_delta(do, o)
    dq, dk, dv = _flash_bwd(q, k, v, do, lse, delta)
    dq_pre, dkr = _qk_bwd(dq, dk, cos, sin_a, sin_b)
    grads["w_uq_p"] = _mm(qn, dq_pre, ta=True, name="mm_d_wuq", tm=256, tn=1024, tk=1024)
    dqn = _mm(dq_pre, w_uq_p, tb=True, name="mm_d_qn", tm=512, tn=256, tk=1024)
    grads["w_k_p"] = _mm(kvn, dk, ta=True, name="mm_d_wk", tm=128, tn=1024, tk=1024)
    grads["w_v_p"] = _mm(kvn, dv, ta=True, name="mm_d_wv", tm=128, tn=1024, tk=1024)
    dkvn_k = _mm(dk, w_kv_p[:, :D_ATT], tb=True, name="mm_d_kvn_k", tm=512, tn=128, tk=1024)
    dkvn = _mm(dv, w_kv_p[:, D_ATT:], tb=True, add=dkvn_k, name="mm_d_kvn_v", tm=512, tn=128, tk=1024)
    dz, grads["conv_w"], grads["q_norm_g"], grads["kv_norm_g"] = _mix_bwd(
        z, dyc, dqn, dkvn, dkr, conv_w8, gq, gkv, cos, sin_a, sin_b)
    grads["w_in_p"] = _mm(h, dz, ta=True, name="mm_d_win", tm=512, tn=1024, tk=1024)
    dh = _mm(dz, w_in_p, tb=True, name="mm_d_h", tm=512, tn=1024, tk=1024)
    grad_x, grads["norm_mix_g"] = _rms_bwd(x, g1, dh, dx1, name="rms_mix_bwd")
    return loss[0, 0], grad_x, grads


_BIG = ("w_in", "w_uq", "w_ukv", "w_o", "w_up", "w_down", "w_ple_gate", "w_ple_proj")
_COL_SHARDED = ("w_in", "w_uq", "w_ukv", "w_up", "w_ple_proj")
_SMALL = ("norm_mix_g", "conv_w", "q_norm_g", "kv_norm_g", "norm_ffn_g", "ffn_conv_w", "ffn_conv_b", "ple_norm_g",
          "final_norm_g")


def _gathered_weights(w):
    shards = [w[n][0].astype(BF16) for n in _BIG]
    shards.append(jnp.pad(w["conv_w"][0], ((0, 5), (0, 0))))
    shards.append(jnp.pad(w["ffn_conv_w"][0], ((0, 5), (0, 0))))
    got = _gather_chips(shards)
    full = {}
    for n, g4 in zip(_BIG, got[:len(_BIG)]):
        full[n] = _cols_from_shards(g4) if n in _COL_SHARDED else g4.reshape(-1, g4.shape[2])
    full["conv_w8"] = _cols_from_shards(got[len(_BIG)])
    full["ffn_conv_w8"] = _cols_from_shards(got[len(_BIG) + 1])
    return _layout_weights(full, w)


def _layout_weights(full, w):
    out = {n: w[n] for n in ("norm_mix_g", "q_norm_g", "kv_norm_g", "norm_ffn_g", "ple_norm_g")}
    out["final_norm_g"] = w["final_norm_g"][None, :]
    w_in = full["w_in"]
    zc = jnp.zeros((D_MODEL, QK_NOPE), BF16)
    zt = jnp.zeros((D_MODEL, HEAD_PAD - QK_NOPE - QK_ROPE), BF16)
    out["w_in_p"] = jnp.concatenate([w_in[:, :D_IN - QK_ROPE], zc, w_in[:, D_IN - QK_ROPE:], zt], axis=1)
    out["w_uq_p"] = _pad_heads(full["w_uq"], QK_NOPE + QK_ROPE)
    kv3 = full["w_ukv"].reshape(KV_LORA, N_HEADS, QK_NOPE + V_HEAD)
    out["w_kv_p"] = jnp.concatenate([_pad_heads(kv3[:, :, :QK_NOPE].reshape(KV_LORA, -1), QK_NOPE),
                                     _pad_heads(kv3[:, :, QK_NOPE:].reshape(KV_LORA, -1), V_HEAD)], axis=1)
    w_o = full["w_o"]
    out["w_o_a"] = w_o[:CONV_WIDTH]
    out["w_o_b"] = jnp.pad(w_o[CONV_WIDTH:].reshape(N_HEADS, V_HEAD, D_MODEL),
                           ((0, 0), (0, HEAD_PAD - V_HEAD), (0, 0))).reshape(D_ATT, D_MODEL)
    out["w_up"], out["w_down"] = full["w_up"], full["w_down"]
    out["w_ple_gate"], out["w_ple_proj"] = full["w_ple_gate"], full["w_ple_proj"]
    out["conv_w8"] = full["conv_w8"]
    fw = full["ffn_conv_w8"]
    out["ffn_conv_w"] = jnp.stack([fw[:, :D_FF], fw[:, D_FF:]])
    out["ffn_conv_b"] = w["ffn_conv_b"].reshape(2, 1, D_FF)
    return out


def _true_gradients(g):
    out = {}
    wp = g["w_in_p"]
    out["w_in"] = jnp.concatenate([wp[:, :D_IN - QK_ROPE], wp[:, D_IN_PAD - HEAD_PAD + QK_NOPE:
                                                              D_IN_PAD - HEAD_PAD + QK_NOPE + QK_ROPE]], axis=1)
    out["w_uq"] = _unpad_heads(g["w_uq_p"], QK_NOPE + QK_ROPE).reshape(Q_LORA, -1)
    out["w_ukv"] = jnp.concatenate([_unpad_heads(g["w_k_p"], QK_NOPE), _unpad_heads(g["w_v_p"], V_HEAD)],
                                   axis=2).reshape(KV_LORA, -1)
    out["w_o"] = jnp.concatenate([g["w_o_a"], g["w_o_b"].reshape(N_HEADS, HEAD_PAD, D_MODEL)[:, :V_HEAD]
                                  .reshape(N_HEADS * V_HEAD, D_MODEL)], axis=0)
    for n in ("w_up", "w_down", "w_ple_gate", "w_ple_proj"):
        out[n] = g[n]
    out["conv_w"] = g["conv_w"]
    fw = g["ffn_conv_w"]
    out["ffn_conv_w"] = jnp.concatenate([fw[0, :3], fw[1, :3]], axis=1)
    out["ffn_conv_b"] = g["ffn_conv_b"].reshape(1, 2 * D_FF)
    for n in ("norm_mix_g", "q_norm_g", "kv_norm_g", "norm_ffn_g", "ple_norm_g", "final_norm_g"):
        out[n] = g[n]
    return out


def _reduce_big(g):
    g4 = []
    for n in _BIG:
        if n in _COL_SHARDED:
            g4.append(_cols_to_shards(g[n]))
        else:
            g4.append(g[n].reshape(N_CHIPS, g[n].shape[0] // N_CHIPS, g[n].shape[1]))
    mine, sib = _swap_halves(g4)
    chip_sum = [_add2(a, b, name="add_cores_" + n) for n, a, b in zip(_BIG, mine, sib)]
    landed = _scatter_chips(chip_sum)
    halves = [_sum_slots(a, name="sum_chips_" + n) for n, a in zip(_BIG, landed)]
    whole = _join_halves(halves)
    return {n: a.reshape(-1, a.shape[2]) for n, a in zip(_BIG, whole)}


_SMALL_SIZES = {"norm_mix_g": D_MODEL, "conv_w": 3 * CONV_WIDTH, "q_norm_g": Q_LORA, "kv_norm_g": KV_LORA,
                "norm_ffn_g": D_MODEL, "ffn_conv_w": 6 * D_FF, "ffn_conv_b": 2 * D_FF, "ple_norm_g": D_MODEL,
                "final_norm_g": D_MODEL}


def _pack(parts, rows):
    flat = jnp.concatenate([a.reshape(-1) for a in parts])
    return jnp.pad(flat, (0, rows * 128 - flat.shape[0])).reshape(rows, 128)


def _unpack(buf, sizes):
    flat = buf.reshape(-1)
    out, at = [], 0
    for n in sizes:
        out.append(flat[at:at + n])
        at += n
    return out


def _reduce_small(g):
    total = sum(_SMALL_SIZES[n] for n in _SMALL)
    rows = -(-total // 1024) * 8
    slots = _gather_all(_pack([g[n] for n in _SMALL], rows))
    summed = _sum_slots(slots, name="sum_small")
    return dict(zip(_SMALL, _unpack(summed, [_SMALL_SIZES[n] for n in _SMALL])))


def kernel(x, p, norm_mix_g, w_in, conv_w, q_norm_g, w_uq, kv_norm_g, w_ukv, w_o, norm_ffn_g, w_up, ffn_conv_w, ffn_conv_b, w_down, ple_norm_g, w_ple_gate, w_ple_proj, final_norm_g, loss_target, m_norm_mix_g, m_w_in, m_conv_w, m_q_norm_g, m_w_uq, m_kv_norm_g, m_w_ukv, m_w_o, m_norm_ffn_g, m_w_up, m_ffn_conv_w, m_ffn_conv_b, m_w_down, m_ple_norm_g, m_w_ple_gate, m_w_ple_proj, m_final_norm_g, v_norm_mix_g, v_w_in, v_conv_w, v_q_norm_g, v_w_uq, v_kv_norm_g, v_w_ukv, v_w_o, v_norm_ffn_g, v_w_up, v_ffn_conv_w, v_ffn_conv_b, v_w_down, v_ple_norm_g, v_w_ple_gate, v_w_ple_proj, v_final_norm_g):
    names = ["norm_mix_g", "w_in", "conv_w", "q_norm_g", "w_uq", "kv_norm_g", "w_ukv", "w_o", "norm_ffn_g", "w_up",
             "ffn_conv_w", "ffn_conv_b", "w_down", "ple_norm_g", "w_ple_gate", "w_ple_proj", "final_norm_g"]
    w = dict(zip(names, (norm_mix_g, w_in, conv_w, q_norm_g, w_uq, kv_norm_g, w_ukv, w_o, norm_ffn_g, w_up,
                         ffn_conv_w, ffn_conv_b, w_down, ple_norm_g, w_ple_gate, w_ple_proj, final_norm_g)))
    m = dict(zip(names, (m_norm_mix_g, m_w_in, m_conv_w, m_q_norm_g, m_w_uq, m_kv_norm_g, m_w_ukv, m_w_o,
                         m_norm_ffn_g, m_w_up, m_ffn_conv_w, m_ffn_conv_b, m_w_down, m_ple_norm_g, m_w_ple_gate,
                         m_w_ple_proj, m_final_norm_g)))
    v = dict(zip(names, (v_norm_mix_g, v_w_in, v_conv_w, v_q_norm_g, v_w_uq, v_kv_norm_g, v_w_ukv, v_w_o,
                         v_norm_ffn_g, v_w_up, v_ffn_conv_w, v_ffn_conv_b, v_w_down, v_ple_norm_g, v_w_ple_gate,
                         v_w_ple_proj, v_final_norm_g)))

    wts = _gathered_weights(w)
    loss, grad_x, padded = _local_step(x[0], p[0, 0], loss_target[0], wts)
    g_full = _true_gradients(padded)
    loss = lax.psum(loss, ("x", "y", "c"))

    g_out, d_out, m_out, v_out = {}, {}, {}, {}
    big = _reduce_big(g_full)
    for n in _BIG:
        shape = w[n].shape
        g = big[n]
        d, mn, vn = _adamw(w[n][0], g, m[n][0], v[n][0], name="adamw_" + n)
        g_out[n], d_out[n], m_out[n], v_out[n] = (a.reshape(shape) for a in (g, d, mn, vn))

    small = _reduce_small(g_full)
    chip = 2 * lax.axis_index("x") + lax.axis_index("y")
    g_small = {}
    for n in _SMALL:
        shape = w[n].shape
        g = small[n]
        if n in ("conv_w", "ffn_conv_w"):
            width = shape[-1]
            g = lax.dynamic_slice(g.reshape(3, N_CHIPS * width), (0, chip * width), (3, width))
        g_small[n] = g.reshape(shape)
    sizes = [g_small[n].size for n in _SMALL]
    rows = -(-sum(sizes) // 1024) * 8
    packed = [_pack([src[n] for n in _SMALL], rows) for src in (w, g_small, m, v)]
    d_s, m_s, v_s = _adamw(*packed, name="adamw_small")
    for n, d, mn, vn in zip(_SMALL, _unpack(d_s, sizes), _unpack(m_s, sizes), _unpack(v_s, sizes)):
        shape = w[n].shape
        g_out[n], d_out[n], m_out[n], v_out[n] = g_small[n], d.reshape(shape), mn.reshape(shape), vn.reshape(shape)

    return (loss, grad_x[None], *[g_out[n] for n in names], *[d_out[n] for n in names],
            *[m_out[n] for n in names], *[v_out[n] for n in names])
```

```python
import functools

import jax
import jax.numpy as jnp
from jax import lax
from jax.experimental import pallas as pl
from jax.experimental.pallas import tpu as pltpu

F32 = jnp.float32
BF16 = jnp.bfloat16

D_MODEL = 1024
CONV_WIDTH = 512
Q_LORA = 256
KV_LORA = 128
QK_NOPE = 64
QK_ROPE = 32
V_HEAD = 64
N_HEADS = 8
HEAD_PAD = 128
D_ATT = N_HEADS * HEAD_PAD
D_IN = 3 * CONV_WIDTH + Q_LORA + KV_LORA + QK_ROPE
D_IN_PAD = 3 * CONV_WIDTH + Q_LORA + KV_LORA + HEAD_PAD
D_FF = 2816
ROPE_THETA = 10000.0
EPS = 1e-6
SM_SCALE = (QK_NOPE + QK_ROPE) ** -0.5
ONES_LANE = V_HEAD

ADAM_LR = 0.001
ADAM_B1 = 0.9
ADAM_B2 = 0.999
ADAM_EPS = 1e-08
ADAM_WD = 0.01
ADAM_STEP = 10

N_CHIPS = 4
N_DEV = 8
MESH = pl.DeviceIdType.MESH
ANY = pl.BlockSpec(memory_space=pl.ANY)


def _params(sem):
    return pltpu.CompilerParams(dimension_semantics=sem)


def _mm(a, b, *, name, ta=False, tb=False, add=None, out_dtype=F32, tm=512, tn=512, tk=512,
        a_split=False, b_split=False, o_split=False):
    if a_split:
        _, m, kh = a.shape
        k = 2 * kh
    elif ta:
        k, m = a.shape
    else:
        m, k = a.shape
    if b_split:
        _, kb, nh = b.shape
        n = 2 * nh
    elif tb:
        n, kb = b.shape
    else:
        kb, n = b.shape
    assert kb == k, (name, a.shape, b.shape)
    tm, tn, tk = min(tm, m), min(tn, n), min(tk, k)
    assert m % tm == 0 and n % tn == 0 and k % tk == 0, (name, m, n, k, tm, tn, tk)
    gm, gn, gk = m // tm, n // tn, k // tk

    if a_split:
        assert gk % 2 == 0
        a_spec = pl.BlockSpec((None, tm, tk), lambda i, j, kk: (kk // (gk // 2), i, kk % (gk // 2)))
    elif ta:
        a_spec = pl.BlockSpec((tk, tm), lambda i, j, kk: (kk, i))
    else:
        a_spec = pl.BlockSpec((tm, tk), lambda i, j, kk: (i, kk))
    if b_split:
        assert gn % 2 == 0
        b_spec = pl.BlockSpec((None, tk, tn), lambda i, j, kk: (j // (gn // 2), kk, j % (gn // 2)))
    elif tb:
        b_spec = pl.BlockSpec((tn, tk), lambda i, j, kk: (j, kk))
    else:
        b_spec = pl.BlockSpec((tk, tn), lambda i, j, kk: (kk, j))
    if o_split:
        assert gn % 2 == 0
        o_spec = pl.BlockSpec((None, tm, tn), lambda i, j, kk: (j // (gn // 2), i, j % (gn // 2)))
        o_shape = jax.ShapeDtypeStruct((2, m, n // 2), out_dtype)
    else:
        o_spec = pl.BlockSpec((tm, tn), lambda i, j, kk: (i, j))
        o_shape = jax.ShapeDtypeStruct((m, n), out_dtype)
    dims = (((0 if ta else 1,), (1 if tb else 0,)), ((), ()))

    def body(*refs):
        if add is None:
            a_ref, b_ref, o_ref, acc_ref = refs
            add_ref = None
        else:
            a_ref, b_ref, add_ref, o_ref, acc_ref = refs
        kk = pl.program_id(2)

        @pl.when(kk == 0)
        def _():
            acc_ref[...] = jnp.zeros_like(acc_ref)

        acc_ref[...] += lax.dot_general(a_ref[...].astype(BF16), b_ref[...].astype(BF16), dims,
                                        preferred_element_type=F32)

        @pl.when(kk == gk - 1)
        def _():
            r = acc_ref[...]
            if add_ref is not None:
                r = r + add_ref[...]
            o_ref[...] = r.astype(o_ref.dtype)

    in_specs = [a_spec, b_spec]
    args = [a, b]
    if add is not None:
        in_specs.append(pl.BlockSpec((tm, tn), lambda i, j, kk: (i, j)))
        args.append(add)
    return pl.pallas_call(
        body, name=name, grid=(gm, gn, gk), in_specs=in_specs, out_specs=o_spec, out_shape=o_shape,
        scratch_shapes=[pltpu.VMEM((tm, tn), F32)],
        compiler_params=_params(("parallel", "parallel", "arbitrary")),
    )(*args)


def _rms_scale(v):
    return lax.rsqrt(jnp.mean(v * v, axis=-1, keepdims=True) + EPS)


def _rms_bwd_rows(v, g, dy):
    r = _rms_scale(v)
    vh = v * r
    dyg = dy * g
    dv = r * (dyg - vh * jnp.mean(dyg * vh, axis=-1, keepdims=True))
    return dv, dy * vh


def _shift_down(v, first_row):
    row = lax.broadcasted_iota(jnp.int32, v.shape, 0)
    return jnp.where(row == 0, first_row, pltpu.roll(v, 1, 0))


def _shift_up(v, last_row):
    n = v.shape[0]
    row = lax.broadcasted_iota(jnp.int32, v.shape, 0)
    return jnp.where(row == n - 1, last_row, pltpu.roll(v, n - 1, 0))


def _rope(t, cos, sin_a, sin_b):
    return t * cos + pltpu.roll(t, HEAD_PAD - 16, 1) * sin_a + pltpu.roll(t, 16, 1) * sin_b


def _rope_bwd(d, cos, sin_a, sin_b):
    return d * cos + pltpu.roll(d * sin_a, 16, 1) + pltpu.roll(d * sin_b, HEAD_PAD - 16, 1)


def _sigmoid(v):
    return 1.0 / (1.0 + jnp.exp(-v))


def _halo_specs(ts, s, width, col):
    nb = ts // 8
    last = s // 8 - 1
    prev = pl.BlockSpec((8, width), lambda i: (jnp.maximum(i * nb - 1, 0), col))
    nxt = pl.BlockSpec((8, width), lambda i: (jnp.minimum((i + 1) * nb, last), col))
    return prev, nxt


def _rms_fwd(x, g, *, name, ts=512):
    s, d = x.shape

    def body(x_ref, g_ref, h_ref):
        v = x_ref[...]
        h_ref[...] = (v * _rms_scale(v) * g_ref[...]).astype(h_ref.dtype)

    return pl.pallas_call(
        body, name=name, grid=(s // ts,),
        in_specs=[pl.BlockSpec((ts, d), lambda i: (i, 0)), pl.BlockSpec((1, d), lambda i: (0, 0))],
        out_specs=pl.BlockSpec((ts, d), lambda i: (i, 0)),
        out_shape=jax.ShapeDtypeStruct((s, d), BF16),
        compiler_params=_params(("parallel",)),
    )(x, g)


def _rms_bwd(x, g, dy, add, *, name, ts=256):
    s, d = x.shape

    def body(x_ref, g_ref, dy_ref, add_ref, dx_ref, dg_ref):
        @pl.when(pl.program_id(0) == 0)
        def _():
            dg_ref[...] = jnp.zeros_like(dg_ref)

        dv, dgr = _rms_bwd_rows(x_ref[...], g_ref[...], dy_ref[...])
        dx_ref[...] = dv + add_ref[...]
        dg_ref[...] += jnp.sum(dgr, axis=0, keepdims=True)

    row = pl.BlockSpec((ts, d), lambda i: (i, 0))
    vec = pl.BlockSpec((1, d), lambda i: (0, 0))
    return pl.pallas_call(
        body, name=name, grid=(s // ts,),
        in_specs=[row, vec, row, row], out_specs=[row, vec],
        out_shape=[jax.ShapeDtypeStruct((s, d), F32), jax.ShapeDtypeStruct((1, d), F32)],
        compiler_params=_params(("arbitrary",)),
    )(x, g, dy, add)


def _mix_pre(z, conv_w8, gq, gkv, cos, sin_a, sin_b, *, ts=256):
    s = z.shape[0]
    n = s // ts
    cw = CONV_WIDTH

    def body(z_ref, xcp, xcn, cgp, cgn, w_ref, gq_ref, gkv_ref, cos_ref, sa_ref, sb_ref,
             yc_ref, qn_ref, kvn_ref, kr_ref):
        i = pl.program_id(0)
        xc = z_ref[:, 0:cw]
        bg = z_ref[:, cw:2 * cw]
        cg = z_ref[:, 2 * cw:3 * cw]
        m = cg * xc
        m_prev = jnp.where(i > 0, xcp[7:8, :] * cgp[7:8, :], 0.0)
        m_next = jnp.where(i < n - 1, xcn[0:1, :] * cgn[0:1, :], 0.0)
        cm = _shift_down(m, m_prev) * w_ref[0:1, :] + m * w_ref[1:2, :] + _shift_up(m, m_next) * w_ref[2:3, :]
        yc_ref[...] = (bg * cm).astype(BF16)
        ql = z_ref[:, 3 * cw:3 * cw + Q_LORA]
        qn_ref[...] = (ql * _rms_scale(ql) * gq_ref[...]).astype(BF16)
        kvl = z_ref[:, 3 * cw + Q_LORA:3 * cw + Q_LORA + KV_LORA]
        kvn_ref[...] = (kvl * _rms_scale(kvl) * gkv_ref[...]).astype(BF16)
        kr_ref[...] = _rope(z_ref[:, D_IN_PAD - HEAD_PAD:D_IN_PAD], cos_ref[...], sa_ref[...], sb_ref[...])

    xcp, xcn = _halo_specs(ts, s, cw, 0)
    cgp, cgn = _halo_specs(ts, s, cw, 2)
    tab = pl.BlockSpec((ts, HEAD_PAD), lambda i: (i, 0))
    return pl.pallas_call(
        body, name="mix_pre", grid=(n,),
        in_specs=[pl.BlockSpec((ts, D_IN_PAD), lambda i: (i, 0)), xcp, xcn, cgp, cgn,
                  pl.BlockSpec((8, cw), lambda i: (0, 0)), pl.BlockSpec((1, Q_LORA), lambda i: (0, 0)),
                  pl.BlockSpec((1, KV_LORA), lambda i: (0, 0)), tab, tab, tab],
        out_specs=[pl.BlockSpec((ts, cw), lambda i: (i, 0)), pl.BlockSpec((ts, Q_LORA), lambda i: (i, 0)),
                   pl.BlockSpec((ts, KV_LORA), lambda i: (i, 0)), tab],
        out_shape=[jax.ShapeDtypeStruct((s, cw), BF16), jax.ShapeDtypeStruct((s, Q_LORA), BF16),
                   jax.ShapeDtypeStruct((s, KV_LORA), BF16), jax.ShapeDtypeStruct((s, HEAD_PAD), F32)],
        compiler_params=_params(("parallel",)),
    )(z, z, z, z, z, conv_w8, gq, gkv, cos, sin_a, sin_b)


def _mix_bwd(z, dyc, dqn, dkvn, dkr, conv_w8, gq, gkv, cos, sin_a, sin_b, *, ts=256):
    s = z.shape[0]
    n = s // ts
    cw = CONV_WIDTH

    def body(z_ref, xcp, xcn, bgp, bgn, cgp, cgn, dyc_ref, dycp, dycn, dqn_ref, dkvn_ref, dkr_ref,
             w_ref, gq_ref, gkv_ref, cos_ref, sa_ref, sb_ref,
             dz_ref, dw0_ref, dw1_ref, dw2_ref, dgq_ref, dgkv_ref):
        i = pl.program_id(0)

        @pl.when(i == 0)
        def _():
            for r in (dw0_ref, dw1_ref, dw2_ref, dgq_ref, dgkv_ref):
                r[...] = jnp.zeros_like(r)

        xc = z_ref[:, 0:cw]
        bg = z_ref[:, cw:2 * cw]
        cg = z_ref[:, 2 * cw:3 * cw]
        w0, w1, w2 = w_ref[0:1, :], w_ref[1:2, :], w_ref[2:3, :]
        m = cg * xc
        m_dn = _shift_down(m, jnp.where(i > 0, xcp[7:8, :] * cgp[7:8, :], 0.0))
        m_up = _shift_up(m, jnp.where(i < n - 1, xcn[0:1, :] * cgn[0:1, :], 0.0))
        cm = m_dn * w0 + m * w1 + m_up * w2
        dyc_v = dyc_ref[...]
        dcm = dyc_v * bg
        dcm_dn = _shift_down(dcm, jnp.where(i > 0, dycp[7:8, :] * bgp[7:8, :], 0.0))
        dcm_up = _shift_up(dcm, jnp.where(i < n - 1, dycn[0:1, :] * bgn[0:1, :], 0.0))
        dm = dcm_up * w0 + dcm * w1 + dcm_dn * w2
        dz_ref[:, 0:cw] = (dm * cg).astype(BF16)
        dz_ref[:, cw:2 * cw] = (dyc_v * cm).astype(BF16)
        dz_ref[:, 2 * cw:3 * cw] = (dm * xc).astype(BF16)
        dw0_ref[...] += jnp.sum(dcm * m_dn, axis=0, keepdims=True)
        dw1_ref[...] += jnp.sum(dcm * m, axis=0, keepdims=True)
        dw2_ref[...] += jnp.sum(dcm * m_up, axis=0, keepdims=True)

        dql, dgq_rows = _rms_bwd_rows(z_ref[:, 3 * cw:3 * cw + Q_LORA], gq_ref[...], dqn_ref[...])
        dz_ref[:, 3 * cw:3 * cw + Q_LORA] = dql.astype(BF16)
        dgq_ref[...] += jnp.sum(dgq_rows, axis=0, keepdims=True)
        dkvl, dgkv_rows = _rms_bwd_rows(z_ref[:, 3 * cw + Q_LORA:3 * cw + Q_LORA + KV_LORA], gkv_ref[...],
                                        dkvn_ref[...])
        dz_ref[:, 3 * cw + Q_LORA:3 * cw + Q_LORA + KV_LORA] = dkvl.astype(BF16)
        dgkv_ref[...] += jnp.sum(dgkv_rows, axis=0, keepdims=True)

        lane = lax.broadcasted_iota(jnp.int32, (ts, HEAD_PAD), 1)
        rope_lane = (lane >= QK_NOPE) & (lane < QK_NOPE + QK_ROPE)
        dk = _rope_bwd(dkr_ref[...], cos_ref[...], sa_ref[...], sb_ref[...])
        dz_ref[:, D_IN_PAD - HEAD_PAD:D_IN_PAD] = jnp.where(rope_lane, dk, 0.0).astype(BF16)

    xcp, xcn = _halo_specs(ts, s, cw, 0)
    bgp, bgn = _halo_specs(ts, s, cw, 1)
    cgp, cgn = _halo_specs(ts, s, cw, 2)
    dycp, dycn = _halo_specs(ts, s, cw, 0)
    tab = pl.BlockSpec((ts, HEAD_PAD), lambda i: (i, 0))

    def vec(width):
        return pl.BlockSpec((1, width), lambda i: (0, 0))

    outs = pl.pallas_call(
        body, name="mix_bwd", grid=(n,),
        in_specs=[pl.BlockSpec((ts, D_IN_PAD), lambda i: (i, 0)), xcp, xcn, bgp, bgn, cgp, cgn,
                  pl.BlockSpec((ts, cw), lambda i: (i, 0)), dycp, dycn,
                  pl.BlockSpec((ts, Q_LORA), lambda i: (i, 0)), pl.BlockSpec((ts, KV_LORA), lambda i: (i, 0)), tab,
                  pl.BlockSpec((8, cw), lambda i: (0, 0)), vec(Q_LORA), vec(KV_LORA), tab, tab, tab],
        out_specs=[pl.BlockSpec((ts, D_IN_PAD), lambda i: (i, 0)), vec(cw), vec(cw), vec(cw), vec(Q_LORA),
                   vec(KV_LORA)],
        out_shape=[jax.ShapeDtypeStruct((s, D_IN_PAD), BF16)] + [jax.ShapeDtypeStruct((1, cw), F32)] * 3
        + [jax.ShapeDtypeStruct((1, Q_LORA), F32), jax.ShapeDtypeStruct((1, KV_LORA), F32)],
        compiler_params=_params(("arbitrary",)),
    )(z, z, z, z, z, z, z, dyc, dyc, dyc, dqn, dkvn, dkr, conv_w8, gq, gkv, cos, sin_a, sin_b)
    dz, dw0, dw1, dw2, dgq, dgkv = outs
    return dz, jnp.concatenate([dw0, dw1, dw2], axis=0), dgq, dgkv


def _qkv_proj(qn, kvn, kr, w_uq_p, w_kv_p, cos, sin_a, sin_b, *, ts=512):
    s = qn.shape[0]

    def body(qn_ref, kvn_ref, kr_ref, wq_ref, wkv_ref, cos_ref, sa_ref, sb_ref, q_ref, k_ref, v_ref):
        cos_v, sa, sb = cos_ref[...], sa_ref[...], sb_ref[...]
        q = jnp.dot(qn_ref[...], wq_ref[...], preferred_element_type=F32)
        kv = jnp.dot(kvn_ref[...], wkv_ref[...], preferred_element_type=F32)
        kr_v = kr_ref[...]
        lane = lax.broadcasted_iota(jnp.int32, (1, HEAD_PAD), 1)
        ones_lane = (lane == ONES_LANE).astype(F32)
        for h in range(N_HEADS):
            blk = slice(h * HEAD_PAD, (h + 1) * HEAD_PAD)
            q_ref[:, blk] = (_rope(q[:, blk], cos_v, sa, sb) * SM_SCALE).astype(BF16)
            k_ref[:, blk] = (kv[:, blk] + kr_v).astype(BF16)
            v_ref[:, blk] = (kv[:, D_ATT + h * HEAD_PAD:D_ATT + (h + 1) * HEAD_PAD] + ones_lane).astype(BF16)

    tab = pl.BlockSpec((ts, HEAD_PAD), lambda i: (i, 0))
    wide = pl.BlockSpec((ts, D_ATT), lambda i: (i, 0))
    return pl.pallas_call(
        body, name="qkv_proj", grid=(s // ts,),
        in_specs=[pl.BlockSpec((ts, Q_LORA), lambda i: (i, 0)), pl.BlockSpec((ts, KV_LORA), lambda i: (i, 0)), tab,
                  pl.BlockSpec((Q_LORA, D_ATT), lambda i: (0, 0)), pl.BlockSpec((KV_LORA, 2 * D_ATT), lambda i: (0, 0)),
                  tab, tab, tab],
        out_specs=[wide, wide, wide],
        out_shape=[jax.ShapeDtypeStruct((s, D_ATT), BF16)] * 3,
        compiler_params=_params(("parallel",)),
    )(qn, kvn, kr, w_uq_p, w_kv_p, cos, sin_a, sin_b)


def _qk_bwd(dq, dk, cos, sin_a, sin_b, *, ts=256):
    s = dq.shape[0]

    def body(dq_ref, dk_ref, cos_ref, sa_ref, sb_ref, dqp_ref, dkr_ref):
        cos_v, sa, sb = cos_ref[...], sa_ref[...], sb_ref[...]
        tot = jnp.zeros((ts, HEAD_PAD), F32)
        for h in range(N_HEADS):
            blk = slice(h * HEAD_PAD, (h + 1) * HEAD_PAD)
            dqp_ref[:, blk] = _rope_bwd(dq_ref[:, blk], cos_v, sa, sb).astype(BF16)
            tot = tot + dk_ref[:, blk]
        dkr_ref[...] = tot

    tab = pl.BlockSpec((ts, HEAD_PAD), lambda i: (i, 0))
    wide = pl.BlockSpec((ts, D_ATT), lambda i: (i, 0))
    return pl.pallas_call(
        body, name="qk_bwd", grid=(s // ts,),
        in_specs=[wide, wide, tab, tab, tab], out_specs=[wide, tab],
        out_shape=[jax.ShapeDtypeStruct((s, D_ATT), BF16), jax.ShapeDtypeStruct((s, HEAD_PAD), F32)],
        compiler_params=_params(("parallel",)),
    )(dq, dk, cos, sin_a, sin_b)


_NT = (((1,), (1,)), ((), ()))
_TN = (((0,), (0,)), ((), ()))


def _flash_fwd(q, k, v, *, tq=512, tk=1024):
    s = q.shape[0]
    tq, tk = min(tq, s), min(tk, s)
    nk = s // tk

    def body(q_ref, k_ref, v_ref, o_ref, lse_ref):
        qv = q_ref[...]

        def step(j, carry):
            m, acc = carry
            rows = pl.ds(pl.multiple_of(j * tk, tk), tk)
            sc = lax.dot_general(qv, k_ref[rows, :], _NT, preferred_element_type=F32)
            m_new = jnp.maximum(m, jnp.max(sc, axis=1, keepdims=True))
            p = jnp.exp(sc - m_new).astype(BF16)
            acc = jnp.exp(m - m_new) * acc + jnp.dot(p, v_ref[rows, :], preferred_element_type=F32)
            return m_new, acc

        init = (jnp.full((tq, 1), -jnp.inf, F32), jnp.zeros((tq, HEAD_PAD), F32))
        m, acc = lax.fori_loop(0, nk, step, init)
        l = acc[:, ONES_LANE:ONES_LANE + 1]
        o_ref[...] = (acc / l).astype(BF16)
        lse_ref[...] = m + jnp.log(l)

    head = pl.BlockSpec((s, HEAD_PAD), lambda h, i: (0, h))
    return pl.pallas_call(
        body, name="flash_fwd", grid=(N_HEADS, s // tq),
        in_specs=[pl.BlockSpec((tq, HEAD_PAD), lambda h, i: (i, h)), head, head],
        out_specs=[pl.BlockSpec((tq, HEAD_PAD), lambda h, i: (i, h)),
                   pl.BlockSpec((None, tq, 1), lambda h, i: (h, i, 0))],
        out_shape=[jax.ShapeDtypeStruct((s, D_ATT), BF16), jax.ShapeDtypeStruct((N_HEADS, s, 1), F32)],
        compiler_params=_params(("parallel", "parallel")),
    )(q, k, v)


def _attn_delta(do, o, *, ts=512):
    s = do.shape[0]

    def body(do_ref, o_ref, dl_ref):
        for h in range(N_HEADS):
            blk = slice(h * HEAD_PAD, (h + 1) * HEAD_PAD)
            dl_ref[h] = jnp.sum(do_ref[:, blk].astype(F32) * o_ref[:, blk].astype(F32), axis=1, keepdims=True)

    wide = pl.BlockSpec((ts, D_ATT), lambda i: (i, 0))
    return pl.pallas_call(
        body, name="attn_delta", grid=(s // ts,), in_specs=[wide, wide],
        out_specs=pl.BlockSpec((N_HEADS, ts, 1), lambda i: (0, i, 0)),
        out_shape=jax.ShapeDtypeStruct((N_HEADS, s, 1), F32),
        compiler_params=_params(("parallel",)),
    )(do, o)


def _flash_bwd(q, k, v, do, lse, delta, *, tq=512, tk=512):
    s = q.shape[0]
    tq, tk = min(tq, s), min(tk, s)
    nq = s // tq

    def body(q_ref, do_ref, lse_ref, dl_ref, k_ref, v_ref, dq_ref, dk_ref, dv_ref):
        j = pl.program_id(1)

        @pl.when(j == 0)
        def _():
            dq_ref[...] = jnp.zeros_like(dq_ref)

        kv, vv = k_ref[...], v_ref[...]

        def step(i, carry):
            dk, dv = carry
            rows = pl.ds(pl.multiple_of(i * tq, tq), tq)
            qv, dov = q_ref[rows, :], do_ref[rows, :]
            sc = lax.dot_general(qv, kv, _NT, preferred_element_type=F32)
            p = jnp.exp(sc - lse_ref[rows, :])
            dp = lax.dot_general(dov, vv, _NT, preferred_element_type=F32)
            ds = (p * (dp - dl_ref[rows, :])).astype(BF16)
            dv = dv + lax.dot_general(p.astype(BF16), dov, _TN, preferred_element_type=F32)
            dk = dk + lax.dot_general(ds, qv, _TN, preferred_element_type=F32)
            dq_ref[rows, :] += jnp.dot(ds, kv, preferred_element_type=F32)
            return dk, dv

        zero = jnp.zeros((tk, HEAD_PAD), F32)
        dk, dv = lax.fori_loop(0, nq, step, (zero, zero))
        dk_ref[...] = dk
        dv_ref[...] = dv

        @pl.when(j == pl.num_programs(1) - 1)
        def _():
            dq_ref[...] *= SM_SCALE

    head = pl.BlockSpec((s, HEAD_PAD), lambda h, j: (0, h))
    stat = pl.BlockSpec((None, s, 1), lambda h, j: (h, 0, 0))
    blk = pl.BlockSpec((tk, HEAD_PAD), lambda h, j: (j, h))
    return pl.pallas_call(
        body, name="flash_bwd", grid=(N_HEADS, s // tk),
        in_specs=[head, head, stat, stat, blk, blk],
        out_specs=[head, blk, blk],
        out_shape=[jax.ShapeDtypeStruct((s, D_ATT), F32)] * 3,
        compiler_params=_params(("parallel", "arbitrary")),
    )(q, do, lse, delta, k, v)


FFN_TC = 256


def _ffn_specs(ts, s, tc, row_axis):
    nb = ts // 8
    last = s // 8 - 1
    if row_axis == 0:
        main = pl.BlockSpec((2, ts, tc), lambda i, j: (0, i, j))
        prev = pl.BlockSpec((2, 8, tc), lambda i, j: (0, jnp.maximum(i * nb - 1, 0), j))
        nxt = pl.BlockSpec((2, 8, tc), lambda i, j: (0, jnp.minimum((i + 1) * nb, last), j))
    else:
        main = pl.BlockSpec((2, ts, tc), lambda j, i: (0, i, j))
        prev = pl.BlockSpec((2, 8, tc), lambda j, i: (0, jnp.maximum(i * nb - 1, 0), j))
        nxt = pl.BlockSpec((2, 8, tc), lambda j, i: (0, jnp.minimum((i + 1) * nb, last), j))
    return main, prev, nxt


def _ffn_conv(a_ref, ap_ref, an_ref, w_ref, b_ref, half, first, last):
    a = a_ref[half]
    a_dn = _shift_down(a, jnp.where(first, 0.0, ap_ref[half, 7:8, :]))
    a_up = _shift_up(a, jnp.where(last, 0.0, an_ref[half, 0:1, :]))
    w = w_ref[half]
    out = a_dn * w[0:1, :] + a * w[1:2, :] + a_up * w[2:3, :] + b_ref[half]
    return out, a_dn, a, a_up


def _ffn_act(a_pre, w, b, *, ts=512, tc=FFN_TC):
    s = a_pre.shape[1]
    n = s // ts

    def body(a_ref, ap_ref, an_ref, w_ref, b_ref, o_ref):
        i = pl.program_id(0)
        g = _ffn_conv(a_ref, ap_ref, an_ref, w_ref, b_ref, 0, i == 0, i == n - 1)[0]
        u = _ffn_conv(a_ref, ap_ref, an_ref, w_ref, b_ref, 1, i == 0, i == n - 1)[0]
        o_ref[...] = (g * _sigmoid(g) * u).astype(BF16)

    main, prev, nxt = _ffn_specs(ts, s, tc, 0)
    return pl.pallas_call(
        body, name="ffn_act", grid=(n, D_FF // tc),
        in_specs=[main, prev, nxt, pl.BlockSpec((2, 8, tc), lambda i, j: (0, 0, j)),
                  pl.BlockSpec((2, 1, tc), lambda i, j: (0, 0, j))],
        out_specs=pl.BlockSpec((ts, tc), lambda i, j: (i, j)),
        out_shape=jax.ShapeDtypeStruct((s, D_FF), BF16),
        compiler_params=_params(("parallel", "parallel")),
    )(a_pre, a_pre, a_pre, w, b)


def _ffn_act_bwd(a_pre, dact, w, b, *, ts=512, tc=FFN_TC):
    s = a_pre.shape[1]
    n = s // ts

    def body(a_ref, ap_ref, an_ref, dact_ref, w_ref, b_ref, da_ref, dw_ref, db_ref):
        i = pl.program_id(1)

        @pl.when(i == 0)
        def _():
            dw_ref[...] = jnp.zeros_like(dw_ref)
            db_ref[...] = jnp.zeros_like(db_ref)

        g, g_dn, g_c, g_up = _ffn_conv(a_ref, ap_ref, an_ref, w_ref, b_ref, 0, i == 0, i == n - 1)
        u, u_dn, u_c, u_up = _ffn_conv(a_ref, ap_ref, an_ref, w_ref, b_ref, 1, i == 0, i == n - 1)
        dact_v = dact_ref[...]
        sg = _sigmoid(g)
        dg = dact_v * u * (sg * (1.0 + g * (1.0 - sg)))
        du = dact_v * (g * sg)
        da_ref[0] = dg
        da_ref[1] = du
        for half, (d, dn, c, up) in enumerate(((dg, g_dn, g_c, g_up), (du, u_dn, u_c, u_up))):
            dw_ref[half, 0:1, :] += jnp.sum(d * dn, axis=0, keepdims=True)
            dw_ref[half, 1:2, :] += jnp.sum(d * c, axis=0, keepdims=True)
            dw_ref[half, 2:3, :] += jnp.sum(d * up, axis=0, keepdims=True)
            db_ref[half] += jnp.sum(d, axis=0, keepdims=True)

    main, prev, nxt = _ffn_specs(ts, s, tc, 1)
    return pl.pallas_call(
        body, name="ffn_act_bwd", grid=(D_FF // tc, n),
        in_specs=[main, prev, nxt, pl.BlockSpec((ts, tc), lambda j, i: (i, j)),
                  pl.BlockSpec((2, 8, tc), lambda j, i: (0, 0, j)), pl.BlockSpec((2, 1, tc), lambda j, i: (0, 0, j))],
        out_specs=[main, pl.BlockSpec((2, 8, tc), lambda j, i: (0, 0, j)),
                   pl.BlockSpec((2, 1, tc), lambda j, i: (0, 0, j))],
        out_shape=[jax.ShapeDtypeStruct((2, s, D_FF), F32), jax.ShapeDtypeStruct((2, 8, D_FF), F32),
                   jax.ShapeDtypeStruct((2, 1, D_FF), F32)],
        compiler_params=_params(("parallel", "arbitrary")),
    )(a_pre, a_pre, a_pre, dact, w, b)


def _ffn_conv_t(da, w, *, ts=512, tc=FFN_TC):
    s = da.shape[1]
    n = s // ts

    def body(d_ref, dp_ref, dn_ref, w_ref, o_ref):
        i = pl.program_id(0)
        for half in range(2):
            d = d_ref[half]
            d_dn = _shift_down(d, jnp.where(i == 0, 0.0, dp_ref[half, 7:8, :]))
            d_up = _shift_up(d, jnp.where(i == n - 1, 0.0, dn_ref[half, 0:1, :]))
            wv = w_ref[half]
            o_ref[half] = (d_up * wv[0:1, :] + d * wv[1:2, :] + d_dn * wv[2:3, :]).astype(BF16)

    main, prev, nxt = _ffn_specs(ts, s, tc, 0)
    return pl.pallas_call(
        body, name="ffn_conv_t", grid=(n, D_FF // tc),
        in_specs=[main, prev, nxt, pl.BlockSpec((2, 8, tc), lambda i, j: (0, 0, j))],
        out_specs=main, out_shape=jax.ShapeDtypeStruct((2, s, D_FF), BF16),
        compiler_params=_params(("parallel", "parallel")),
    )(da, da, da, w)


def _ple_final(x2, gl, pp, target, gf, *, ts=256):
    s, d = x2.shape

    def body(x2_ref, gl_ref, pp_ref, t_ref, gf_ref, loss_ref, dx3_ref, dgl_ref, dpp_ref, dgf_ref):
        @pl.when(pl.program_id(0) == 0)
        def _():
            loss_ref[...] = jnp.zeros_like(loss_ref)
            dgf_ref[...] = jnp.zeros_like(dgf_ref)

        gate = _sigmoid(gl_ref[...])
        ppv = pp_ref[...]
        x3 = x2_ref[...] + gate * ppv
        gfv = gf_ref[...]
        err = x3 * _rms_scale(x3) * gfv - t_ref[...]
        loss_ref[...] += 0.5 * jnp.sum(jnp.mean(err * err, axis=-1, keepdims=True), axis=0, keepdims=True)
        dx3, dgf_rows = _rms_bwd_rows(x3, gfv, err * (1.0 / d))
        dgf_ref[...] += jnp.sum(dgf_rows, axis=0, keepdims=True)
        dx3_ref[...] = dx3
        dgl_ref[...] = (dx3 * ppv * gate * (1.0 - gate)).astype(BF16)
        dpp_ref[...] = (dx3 * gate).astype(BF16)

    row = pl.BlockSpec((ts, d), lambda i: (i, 0))
    vec = pl.BlockSpec((1, d), lambda i: (0, 0))
    return pl.pallas_call(
        body, name="ple_final", grid=(s // ts,),
        in_specs=[row, row, row, row, vec],
        out_specs=[pl.BlockSpec((1, 128), lambda i: (0, 0)), row, row, row, vec],
        out_shape=[jax.ShapeDtypeStruct((1, 128), F32), jax.ShapeDtypeStruct((s, d), F32),
                   jax.ShapeDtypeStruct((s, d), BF16), jax.ShapeDtypeStruct((s, d), BF16),
                   jax.ShapeDtypeStruct((1, d), F32)],
        compiler_params=_params(("arbitrary",)),
    )(x2, gl, pp, target, gf)


def _row_tile(rows, cols, n_arrays, budget=12 << 20):
    best = None
    for t in range(8, rows + 1, 8):
        if rows % t == 0 and t * cols * 4 * n_arrays <= budget:
            best = t
    return rows if best is None else best


def _sum_slots(a, *, name):
    g, r, c = a.shape
    tr = _row_tile(r, c, g + 1)

    def body(*refs):
        tot = refs[0][...]
        for ref in refs[1:g]:
            tot = tot + ref[...]
        refs[g][...] = tot

    specs = [pl.BlockSpec((None, tr, c), functools.partial(lambda i, slot: (slot, i, 0), slot=k)) for k in range(g)]
    return pl.pallas_call(
        body, name=name, grid=(r // tr,), in_specs=specs, out_specs=pl.BlockSpec((tr, c), lambda i: (i, 0)),
        out_shape=jax.ShapeDtypeStruct((r, c), a.dtype), compiler_params=_params(("parallel",)),
    )(*([a] * g))


def _adamw(w, g, m, v, *, name):
    r, c = w.shape
    tr = _row_tile(r, c, 7)

    def body(w_ref, g_ref, m_ref, v_ref, d_ref, mo_ref, vo_ref):
        gv = g_ref[...]
        mn = ADAM_B1 * m_ref[...] + (1.0 - ADAM_B1) * gv
        vn = ADAM_B2 * v_ref[...] + (1.0 - ADAM_B2) * (gv * gv)
        m_hat = mn / (1.0 - ADAM_B1 ** ADAM_STEP)
        v_hat = vn / (1.0 - ADAM_B2 ** ADAM_STEP)
        d_ref[...] = -ADAM_LR * (m_hat / (jnp.sqrt(v_hat) + ADAM_EPS) + ADAM_WD * w_ref[...])
        mo_ref[...] = mn
        vo_ref[...] = vn

    blk = pl.BlockSpec((tr, c), lambda i: (i, 0))
    return pl.pallas_call(
        body, name=name, grid=(r // tr,), in_specs=[blk] * 4, out_specs=[blk] * 3,
        out_shape=[jax.ShapeDtypeStruct((r, c), F32)] * 3, compiler_params=_params(("parallel",)),
    )(w, g, m, v)


def _position():
    x, y, c = lax.axis_index("x"), lax.axis_index("y"), lax.axis_index("c")
    return x, y, c


def _other_chips(x, y):
    return [(1 - x, y), (x, 1 - y), (1 - x, 1 - y)]


def _stage_in(srcs, stage, sems):
    cps = [pltpu.make_async_copy(src, stage[a], sems.at[a]) for a, src in enumerate(srcs)]
    for cp in cps:
        cp.start()
    return cps


def _stage_out(staged, stage, dsts, sems):
    cps = []
    for a, dst in enumerate(dsts):
        staged[a].wait()
        cp = pltpu.make_async_copy(stage[a], dst, sems.at[a])
        cp.start()
        cps.append(cp)
    return cps


def _gather_chips(shards):
    n = len(shards)

    def body(*refs):
        ins, outs, stage = refs[:n], refs[n:2 * n], refs[2 * n:3 * n]
        send_sems, recv_sems, in_sems, out_sems = refs[3 * n:]
        x, y, c = _position()
        me = 2 * x + y
        chips = _other_chips(x, y)
        remote = []
        staged = _stage_in(ins, stage, in_sems)
        for a in range(n):
            for k, (px, py) in enumerate(chips):
                rc = pltpu.make_async_remote_copy(
                    src_ref=ins[a], dst_ref=outs[a].at[me], send_sem=send_sems.at[3 * a + k],
                    recv_sem=recv_sems.at[3 * a + k], device_id=(px, py, c), device_id_type=MESH)
                rc.start()
                remote.append(rc)
        local = _stage_out(staged, stage, [o.at[me] for o in outs], out_sems)
        for a in range(n):
            for k, (px, py) in enumerate(chips):
                pltpu.make_async_remote_copy(
                    src_ref=ins[a], dst_ref=outs[a].at[2 * px + py], send_sem=send_sems.at[3 * a + k],
                    recv_sem=recv_sems.at[3 * a + k], device_id=(px, py, c), device_id_type=MESH).wait_recv()
        for rc in remote:
            rc.wait_send()
        for cp in local:
            cp.wait()

    return pl.pallas_call(
        body, name="gather_chips", in_specs=[ANY] * n, out_specs=[ANY] * n,
        out_shape=[jax.ShapeDtypeStruct((N_CHIPS,) + s.shape, s.dtype) for s in shards],
        scratch_shapes=[pltpu.VMEM(s.shape, s.dtype) for s in shards]
        + [pltpu.SemaphoreType.DMA((3 * n,)), pltpu.SemaphoreType.DMA((3 * n,)),
           pltpu.SemaphoreType.DMA((n,)), pltpu.SemaphoreType.DMA((n,))],
        compiler_params=pltpu.CompilerParams(has_side_effects=True),
    )(*shards)


def _send_other_halves(grads):
    n = len(grads)

    def body(*refs):
        ins, sib = refs[:n], refs[n:2 * n]
        send_sems, recv_sems = refs[2 * n:]
        x, y, c = _position()
        remote = []
        for a in range(n):
            half = ins[a].shape[1] // 2
            give = ins[a].at[:, pl.ds(pl.multiple_of((1 - c) * half, 8), half), :]
            rc = pltpu.make_async_remote_copy(
                src_ref=give, dst_ref=sib[a], send_sem=send_sems.at[a], recv_sem=recv_sems.at[a],
                device_id=(x, y, 1 - c), device_id_type=MESH)
            rc.start()
            remote.append(rc)
        for rc in remote:
            rc.wait_recv()
        for rc in remote:
            rc.wait_send()

    return pl.pallas_call(
        body, name="send_other_halves", in_specs=[ANY] * n, out_specs=[ANY] * n,
        out_shape=[jax.ShapeDtypeStruct((g.shape[0], g.shape[1] // 2, g.shape[2]), g.dtype) for g in grads],
        scratch_shapes=[pltpu.SemaphoreType.DMA((n,)), pltpu.SemaphoreType.DMA((n,))],
        compiler_params=pltpu.CompilerParams(has_side_effects=True),
    )(*grads)


def _add_own_half(g4, sib, core, *, name):
    g, a2, c = sib.shape
    tr = _row_tile(a2, c, 3)

    def body(core_ref, a_ref, b_ref, o_ref):
        o_ref[...] = a_ref[...] + b_ref[...]

    blk = pl.BlockSpec((None, tr, c), lambda i, j, core_ref: (i, j, 0))
    return pl.pallas_call(
        body, name=name,
        grid_spec=pltpu.PrefetchScalarGridSpec(
            num_scalar_prefetch=1, grid=(g, a2 // tr),
            in_specs=[pl.BlockSpec((None, None, tr, c), lambda i, j, core_ref: (i, core_ref[0], j, 0)), blk],
            out_specs=blk),
        out_shape=jax.ShapeDtypeStruct(sib.shape, sib.dtype), compiler_params=_params(("parallel", "parallel")),
    )(core, g4.reshape(g, 2, a2, c), sib)


def _scatter_chips(parts):
    n = len(parts)

    def body(*refs):
        ins, outs, stage = refs[:n], refs[n:2 * n], refs[2 * n:3 * n]
        send_sems, recv_sems, in_sems, out_sems = refs[3 * n:]
        x, y, c = _position()
        me = 2 * x + y
        chips = _other_chips(x, y)
        remote = []
        staged = _stage_in([r.at[me] for r in ins], stage, in_sems)
        for a in range(n):
            for k, (px, py) in enumerate(chips):
                rc = pltpu.make_async_remote_copy(
                    src_ref=ins[a].at[2 * px + py], dst_ref=outs[a].at[me], send_sem=send_sems.at[3 * a + k],
                    recv_sem=recv_sems.at[3 * a + k], device_id=(px, py, c), device_id_type=MESH)
                rc.start()
                remote.append(rc)
        local = _stage_out(staged, stage, [o.at[me] for o in outs], out_sems)
        for a in range(n):
            for k, (px, py) in enumerate(chips):
                pltpu.make_async_remote_copy(
                    src_ref=ins[a].at[me], dst_ref=outs[a].at[2 * px + py], send_sem=send_sems.at[3 * a + k],
                    recv_sem=recv_sems.at[3 * a + k], device_id=(px, py, c), device_id_type=MESH).wait_recv()
        for rc in remote:
            rc.wait_send()
        for cp in local:
            cp.wait()

    return pl.pallas_call(
        body, name="scatter_chips", in_specs=[ANY] * n, out_specs=[ANY] * n,
        out_shape=[jax.ShapeDtypeStruct(p.shape, p.dtype) for p in parts],
        scratch_shapes=[pltpu.VMEM(p.shape[1:], p.dtype) for p in parts]
        + [pltpu.SemaphoreType.DMA((3 * n,)), pltpu.SemaphoreType.DMA((3 * n,)),
           pltpu.SemaphoreType.DMA((n,)), pltpu.SemaphoreType.DMA((n,))],
        compiler_params=pltpu.CompilerParams(has_side_effects=True),
    )(*parts)


def _join_halves(halves):
    n = len(halves)

    def body(*refs):
        ins, outs, stage = refs[:n], refs[n:2 * n], refs[2 * n:3 * n]
        send_sems, recv_sems, in_sems, out_sems = refs[3 * n:]
        x, y, c = _position()
        remote = []
        staged = _stage_in(ins, stage, in_sems)
        for a in range(n):
            rc = pltpu.make_async_remote_copy(
                src_ref=ins[a], dst_ref=outs[a].at[c], send_sem=send_sems.at[a], recv_sem=recv_sems.at[a],
                device_id=(x, y, 1 - c), device_id_type=MESH)
            rc.start()
            remote.append(rc)
        local = _stage_out(staged, stage, [o.at[c] for o in outs], out_sems)
        for a in range(n):
            pltpu.make_async_remote_copy(
                src_ref=ins[a], dst_ref=outs[a].at[1 - c], send_sem=send_sems.at[a], recv_sem=recv_sems.at[a],
                device_id=(x, y, 1 - c), device_id_type=MESH).wait_recv()
        for rc in remote:
            rc.wait_send()
        for cp in local:
            cp.wait()

    return pl.pallas_call(
        body, name="join_halves", in_specs=[ANY] * n, out_specs=[ANY] * n,
        out_shape=[jax.ShapeDtypeStruct((2,) + h.shape, h.dtype) for h in halves],
        scratch_shapes=[pltpu.VMEM(h.shape, h.dtype) for h in halves]
        + [pltpu.SemaphoreType.DMA((n,)), pltpu.SemaphoreType.DMA((n,)), pltpu.SemaphoreType.DMA((n,)),
           pltpu.SemaphoreType.DMA((n,))],
        compiler_params=pltpu.CompilerParams(has_side_effects=True),
    )(*halves)


def _gather_all(buf):
    def body(in_ref, out_ref, send_sems, recv_sems, local_sem):
        x, y, c = _position()
        me = 4 * x + 2 * y + c
        peers = [(x, y, 1 - c)] + [(px, py, pc) for (px, py) in _other_chips(x, y) for pc in (c, 1 - c)]
        cp = pltpu.make_async_copy(in_ref, out_ref.at[me], local_sem)
        cp.start()
        remote = []
        for k, peer in enumerate(peers):
            rc = pltpu.make_async_remote_copy(
                src_ref=in_ref, dst_ref=out_ref.at[me], send_sem=send_sems.at[k], recv_sem=recv_sems.at[k],
                device_id=peer, device_id_type=MESH)
            rc.start()
            remote.append(rc)
        for k, (px, py, pc) in enumerate(peers):
            pltpu.make_async_remote_copy(
                src_ref=in_ref, dst_ref=out_ref.at[4 * px + 2 * py + pc], send_sem=send_sems.at[k],
                recv_sem=recv_sems.at[k], device_id=(px, py, pc), device_id_type=MESH).wait_recv()
        for rc in remote:
            rc.wait_send()
        cp.wait()

    return pl.pallas_call(
        body, name="gather_all", in_specs=[ANY], out_specs=ANY,
        out_shape=jax.ShapeDtypeStruct((N_DEV,) + buf.shape, buf.dtype),
        scratch_shapes=[pltpu.SemaphoreType.DMA((N_DEV - 1,)), pltpu.SemaphoreType.DMA((N_DEV - 1,)),
                        pltpu.SemaphoreType.DMA],
        compiler_params=pltpu.CompilerParams(has_side_effects=True),
    )(buf)


def _cols_from_shards(g4):
    _, k, n = g4.shape
    return g4.transpose(1, 0, 2).reshape(k, N_CHIPS * n)


def _cols_to_shards(w):
    k, n = w.shape
    return w.reshape(k, N_CHIPS, n // N_CHIPS).transpose(1, 0, 2)


def _pad_heads(w, width):
    k = w.shape[0]
    w3 = w.reshape(k, N_HEADS, width)
    return jnp.pad(w3, ((0, 0), (0, 0), (0, HEAD_PAD - width))).reshape(k, D_ATT)


def _unpad_heads(w, width):
    k = w.shape[0]
    return w.reshape(k, N_HEADS, HEAD_PAD)[:, :, :width]


def _rope_tables(s):
    pos = jnp.arange(s, dtype=F32)
    inv_freq = ROPE_THETA ** (-jnp.arange(0, QK_ROPE, 2, dtype=F32) / QK_ROPE)
    ang = pos[:, None] * inv_freq[None, :]
    cos_h, sin_h = jnp.cos(ang), jnp.sin(ang)
    half = QK_ROPE // 2
    z = jnp.zeros((s, half), F32)
    ones = jnp.ones((s, QK_NOPE), F32)
    tail = jnp.zeros((s, HEAD_PAD - QK_NOPE - QK_ROPE), F32)
    cos = jnp.concatenate([ones, cos_h, cos_h, tail + 1.0], axis=1)
    sin_a = jnp.concatenate([ones * 0.0, -sin_h, z, tail], axis=1)
    sin_b = jnp.concatenate([ones * 0.0, z, sin_h, tail], axis=1)
    return cos, sin_a, sin_b


def _local_step(x, p, target, wts):
    s = x.shape[0]
    cos, sin_a, sin_b = _rope_tables(s)
    g1, gq, gkv, g2, g3, gf = (wts[k] for k in ("norm_mix_g", "q_norm_g", "kv_norm_g", "norm_ffn_g", "ple_norm_g",
                                                 "final_norm_g"))
    w_in_p, w_uq_p, w_kv_p = wts["w_in_p"], wts["w_uq_p"], wts["w_kv_p"]
    w_o_a, w_o_b, w_up, w_down = wts["w_o_a"], wts["w_o_b"], wts["w_up"], wts["w_down"]
    w_pg, w_pp = wts["w_ple_gate"], wts["w_ple_proj"]
    conv_w8, fconv_w, fconv_b = wts["conv_w8"], wts["ffn_conv_w"], wts["ffn_conv_b"]

    h = _rms_fwd(x, g1, name="rms_mix")
    z = _mm(h, w_in_p, name="mm_in", tm=512, tn=1024, tk=1024)
    y_conv, qn, kvn, kr = _mix_pre(z, conv_w8, gq, gkv, cos, sin_a, sin_b)
    q, k, v = _qkv_proj(qn, kvn, kr, w_uq_p, w_kv_p, cos, sin_a, sin_b)
    o, lse = _flash_fwd(q, k, v)
    t = _mm(y_conv, w_o_a, add=x, name="mm_o_conv", tm=512, tn=1024, tk=512)
    x1 = _mm(o, w_o_b, add=t, name="mm_o_att", tm=512, tn=1024, tk=1024)
    hf = _rms_fwd(x1, g2, name="rms_ffn")
    a_pre = _mm(hf, w_up, o_split=True, name="mm_up", tm=1024, tn=1408, tk=1024)
    act = _ffn_act(a_pre, fconv_w, fconv_b)
    x2 = _mm(act, w_down, add=x1, name="mm_down", tm=512, tn=1024, tk=1408)
    n3 = _rms_fwd(x2, g3, name="rms_ple")
    gl = _mm(n3, w_pg, name="mm_gate", tm=512, tn=1024, tk=1024)
    pp = _mm(p, w_pp, name="mm_ple", tm=512, tn=1024, tk=256)
    loss, dx3, dgl, dpp, d_gf = _ple_final(x2, gl, pp, target, gf)

    grads = {"final_norm_g": d_gf}
    grads["w_ple_proj"] = _mm(p, dpp, ta=True, name="mm_d_wpp", tm=256, tn=1024, tk=1024)
    grads["w_ple_gate"] = _mm(n3, dgl, ta=True, name="mm_d_wpg", tm=512, tn=1024, tk=1024)
    dn3 = _mm(dgl, w_pg, tb=True, name="mm_d_n3", tm=512, tn=1024, tk=1024)
    dx2, grads["ple_norm_g"] = _rms_bwd(x2, g3, dn3, dx3, name="rms_ple_bwd")
    grads["w_down"] = _mm(act, dx2, ta=True, name="mm_d_wdown", tm=1408, tn=512, tk=1024)
    dact = _mm(dx2, w_down, tb=True, name="mm_d_act", tm=512, tn=1408, tk=1024)
    da, d_fconv_w, d_fconv_b = _ffn_act_bwd(a_pre, dact, fconv_w, fconv_b)
    grads["ffn_conv_w"], grads["ffn_conv_b"] = d_fconv_w, d_fconv_b
    da_pre = _ffn_conv_t(da, fconv_w)
    grads["w_up"] = _mm(hf, da_pre, ta=True, b_split=True, name="mm_d_wup", tm=512, tn=1408, tk=1024)
    dhf = _mm(da_pre, w_up, tb=True, a_split=True, name="mm_d_hf", tm=512, tn=1024, tk=1408)
    dx1, grads["norm_ffn_g"] = _rms_bwd(x1, g2, dhf, dx2, name="rms_ffn_bwd")
    grads["w_o_a"] = _mm(y_conv, dx1, ta=True, name="mm_d_wo_conv", tm=512, tn=1024, tk=1024)
    grads["w_o_b"] = _mm(o, dx1, ta=True, name="mm_d_wo_att", tm=512, tn=1024, tk=1024)
    dyc = _mm(dx1, w_o_a, tb=True, name="mm_d_yconv", tm=512, tn=512, tk=1024)
    do = _mm(dx1, w_o_b, tb=True, out_dtype=BF16, name="mm_d_o", tm=512, tn=1024, tk=1024)
    delta = _attn_delta(do, o)
    dq, dk, dv = _flash_bwd(q, k, v, do, lse, delta)
    dq_pre, dkr = _qk_bwd(dq, dk, cos, sin_a, sin_b)
    grads["w_uq_p"] = _mm(qn, dq_pre, ta=True, name="mm_d_wuq", tm=256, tn=1024, tk=1024)
    dqn = _mm(dq_pre, w_uq_p, tb=True, name="mm_d_qn", tm=512, tn=256, tk=1024)
    grads["w_k_p"] = _mm(kvn, dk, ta=True, name="mm_d_wk", tm=128, tn=1024, tk=1024)
    grads["w_v_p"] = _mm(kvn, dv, ta=True, name="mm_d_wv", tm=128, tn=1024, tk=1024)
    dkvn_k = _mm(dk, w_kv_p[:, :D_ATT], tb=True, name="mm_d_kvn_k", tm=512, tn=128, tk=1024)
    dkvn = _mm(dv, w_kv_p[:, D_ATT:], tb=True, add=dkvn_k, name="mm_d_kvn_v", tm=512, tn=128, tk=1024)
    dz, grads["conv_w"], grads["q_norm_g"], grads["kv_norm_g"] = _mix_bwd(
        z, dyc, dqn, dkvn, dkr, conv_w8, gq, gkv, cos, sin_a, sin_b)
    grads["w_in_p"] = _mm(h, dz, ta=True, name="mm_d_win", tm=512, tn=1024, tk=1024)
    dh = _mm(dz, w_in_p, tb=True, name="mm_d_h", tm=512, tn=1024, tk=1024)
    grad_x, grads["norm_mix_g"] = _rms_bwd(x, g1, dh, dx1, name="rms_mix_bwd")
    return loss[0, 0], grad_x, grads


_BIG = ("w_in", "w_uq", "w_ukv", "w_o", "w_up", "w_down", "w_ple_gate", "w_ple_proj")
_COL_SHARDED = ("w_in", "w_uq", "w_ukv", "w_up", "w_ple_proj")
_SMALL = ("norm_mix_g", "conv_w", "q_norm_g", "kv_norm_g", "norm_ffn_g", "ffn_conv_w", "ffn_conv_b", "ple_norm_g",
          "final_norm_g")


def _gathered_weights(w):
    shards = [w[n][0].astype(BF16) for n in _BIG]
    shards.append(jnp.pad(w["conv_w"][0], ((0, 5), (0, 0))))
    shards.append(jnp.pad(w["ffn_conv_w"][0], ((0, 5), (0, 0))))
    got = _gather_chips(shards)
    full = {}
    for n, g4 in zip(_BIG, got[:len(_BIG)]):
        full[n] = _cols_from_shards(g4) if n in _COL_SHARDED else g4.reshape(-1, g4.shape[2])
    full["conv_w8"] = _cols_from_shards(got[len(_BIG)])
    full["ffn_conv_w8"] = _cols_from_shards(got[len(_BIG) + 1])
    return _layout_weights(full, w)


def _layout_weights(full, w):
    out = {n: w[n] for n in ("norm_mix_g", "q_norm_g", "kv_norm_g", "norm_ffn_g", "ple_norm_g")}
    out["final_norm_g"] = w["final_norm_g"][None, :]
    w_in = full["w_in"]
    zc = jnp.zeros((D_MODEL, QK_NOPE), BF16)
    zt = jnp.zeros((D_MODEL, HEAD_PAD - QK_NOPE - QK_ROPE), BF16)
    out["w_in_p"] = jnp.concatenate([w_in[:, :D_IN - QK_ROPE], zc, w_in[:, D_IN - QK_ROPE:], zt], axis=1)
    out["w_uq_p"] = _pad_heads(full["w_uq"], QK_NOPE + QK_ROPE)
    kv3 = full["w_ukv"].reshape(KV_LORA, N_HEADS, QK_NOPE + V_HEAD)
    out["w_kv_p"] = jnp.concatenate([_pad_heads(kv3[:, :, :QK_NOPE].reshape(KV_LORA, -1), QK_NOPE),
                                     _pad_heads(kv3[:, :, QK_NOPE:].reshape(KV_LORA, -1), V_HEAD)], axis=1)
    w_o = full["w_o"]
    out["w_o_a"] = w_o[:CONV_WIDTH]
    out["w_o_b"] = jnp.pad(w_o[CONV_WIDTH:].reshape(N_HEADS, V_HEAD, D_MODEL),
                           ((0, 0), (0, HEAD_PAD - V_HEAD), (0, 0))).reshape(D_ATT, D_MODEL)
    out["w_up"], out["w_down"] = full["w_up"], full["w_down"]
    out["w_ple_gate"], out["w_ple_proj"] = full["w_ple_gate"], full["w_ple_proj"]
    out["conv_w8"] = full["conv_w8"]
    fw = full["ffn_conv_w8"]
    out["ffn_conv_w"] = jnp.stack([fw[:, :D_FF], fw[:, D_FF:]])
    out["ffn_conv_b"] = w["ffn_conv_b"].reshape(2, 1, D_FF)
    return out


def _true_gradients(g):
    out = {}
    wp = g["w_in_p"]
    out["w_in"] = jnp.concatenate([wp[:, :D_IN - QK_ROPE], wp[:, D_IN_PAD - HEAD_PAD + QK_NOPE:
                                                              D_IN_PAD - HEAD_PAD + QK_NOPE + QK_ROPE]], axis=1)
    out["w_uq"] = _unpad_heads(g["w_uq_p"], QK_NOPE + QK_ROPE).reshape(Q_LORA, -1)
    out["w_ukv"] = jnp.concatenate([_unpad_heads(g["w_k_p"], QK_NOPE), _unpad_heads(g["w_v_p"], V_HEAD)],
                                   axis=2).reshape(KV_LORA, -1)
    out["w_o"] = jnp.concatenate([g["w_o_a"], g["w_o_b"].reshape(N_HEADS, HEAD_PAD, D_MODEL)[:, :V_HEAD]
                                  .reshape(N_HEADS * V_HEAD, D_MODEL)], axis=0)
    for n in ("w_up", "w_down", "w_ple_gate", "w_ple_proj"):
        out[n] = g[n]
    out["conv_w"] = g["conv_w"]
    fw = g["ffn_conv_w"]
    out["ffn_conv_w"] = jnp.concatenate([fw[0, :3], fw[1, :3]], axis=1)
    out["ffn_conv_b"] = g["ffn_conv_b"].reshape(1, 2 * D_FF)
    for n in ("norm_mix_g", "q_norm_g", "kv_norm_g", "norm_ffn_g", "ple_norm_g", "final_norm_g"):
        out[n] = g[n]
    return out


def _reduce_big(g):
    g4 = []
    for n in _BIG:
        if n in _COL_SHARDED:
            g4.append(_cols_to_shards(g[n]))
        else:
            g4.append(g[n].reshape(N_CHIPS, g[n].shape[0] // N_CHIPS, g[n].shape[1]))
    sib = _send_other_halves(g4)
    core = lax.axis_index("c").astype(jnp.int32).reshape(1)
    chip_sum = [_add_own_half(a, b, core, name="add_cores_" + n) for n, a, b in zip(_BIG, g4, sib)]
    landed = _scatter_chips(chip_sum)
    halves = [_sum_slots(a, name="sum_chips_" + n) for n, a in zip(_BIG, landed)]
    whole = _join_halves(halves)
    return {n: a.reshape(-1, a.shape[2]) for n, a in zip(_BIG, whole)}


_SMALL_SIZES = {"norm_mix_g": D_MODEL, "conv_w": 3 * CONV_WIDTH, "q_norm_g": Q_LORA, "kv_norm_g": KV_LORA,
                "norm_ffn_g": D_MODEL, "ffn_conv_w": 6 * D_FF, "ffn_conv_b": 2 * D_FF, "ple_norm_g": D_MODEL,
                "final_norm_g": D_MODEL}


def _pack(parts, rows):
    flat = jnp.concatenate([a.reshape(-1) for a in parts])
    return jnp.pad(flat, (0, rows * 128 - flat.shape[0])).reshape(rows, 128)


def _unpack(buf, sizes):
    flat = buf.reshape(-1)
    out, at = [], 0
    for n in sizes:
        out.append(flat[at:at + n])
        at += n
    return out


def _reduce_small(g):
    total = sum(_SMALL_SIZES[n] for n in _SMALL)
    rows = -(-total // 1024) * 8
    slots = _gather_all(_pack([g[n] for n in _SMALL], rows))
    summed = _sum_slots(slots, name="sum_small")
    return dict(zip(_SMALL, _unpack(summed, [_SMALL_SIZES[n] for n in _SMALL])))


def kernel(x, p, norm_mix_g, w_in, conv_w, q_norm_g, w_uq, kv_norm_g, w_ukv, w_o, norm_ffn_g, w_up, ffn_conv_w, ffn_conv_b, w_down, ple_norm_g, w_ple_gate, w_ple_proj, final_norm_g, loss_target, m_norm_mix_g, m_w_in, m_conv_w, m_q_norm_g, m_w_uq, m_kv_norm_g, m_w_ukv, m_w_o, m_norm_ffn_g, m_w_up, m_ffn_conv_w, m_ffn_conv_b, m_w_down, m_ple_norm_g, m_w_ple_gate, m_w_ple_proj, m_final_norm_g, v_norm_mix_g, v_w_in, v_conv_w, v_q_norm_g, v_w_uq, v_kv_norm_g, v_w_ukv, v_w_o, v_norm_ffn_g, v_w_up, v_ffn_conv_w, v_ffn_conv_b, v_w_down, v_ple_norm_g, v_w_ple_gate, v_w_ple_proj, v_final_norm_g):
    names = ["norm_mix_g", "w_in", "conv_w", "q_norm_g", "w_uq", "kv_norm_g", "w_ukv", "w_o", "norm_ffn_g", "w_up",
             "ffn_conv_w", "ffn_conv_b", "w_down", "ple_norm_g", "w_ple_gate", "w_ple_proj", "final_norm_g"]
    w = dict(zip(names, (norm_mix_g, w_in, conv_w, q_norm_g, w_uq, kv_norm_g, w_ukv, w_o, norm_ffn_g, w_up,
                         ffn_conv_w, ffn_conv_b, w_down, ple_norm_g, w_ple_gate, w_ple_proj, final_norm_g)))
    m = dict(zip(names, (m_norm_mix_g, m_w_in, m_conv_w, m_q_norm_g, m_w_uq, m_kv_norm_g, m_w_ukv, m_w_o,
                         m_norm_ffn_g, m_w_up, m_ffn_conv_w, m_ffn_conv_b, m_w_down, m_ple_norm_g, m_w_ple_gate,
                         m_w_ple_proj, m_final_norm_g)))
    v = dict(zip(names, (v_norm_mix_g, v_w_in, v_conv_w, v_q_norm_g, v_w_uq, v_kv_norm_g, v_w_ukv, v_w_o,
                         v_norm_ffn_g, v_w_up, v_ffn_conv_w, v_ffn_conv_b, v_w_down, v_ple_norm_g, v_w_ple_gate,
                         v_w_ple_proj, v_final_norm_g)))

    wts = _gathered_weights(w)
    loss, grad_x, padded = _local_step(x[0], p[0, 0], loss_target[0], wts)
    g_full = _true_gradients(padded)
    loss = lax.psum(loss, ("x", "y", "c"))

    g_out, d_out, m_out, v_out = {}, {}, {}, {}
    big = _reduce_big(g_full)
    for n in _BIG:
        shape = w[n].shape
        g = big[n]
        d, mn, vn = _adamw(w[n][0], g, m[n][0], v[n][0], name="adamw_" + n)
        g_out[n], d_out[n], m_out[n], v_out[n] = (a.reshape(shape) for a in (g, d, mn, vn))

    small = _reduce_small(g_full)
    chip = 2 * lax.axis_index("x") + lax.axis_index("y")
    g_small = {}
    for n in _SMALL:
        shape = w[n].shape
        g = small[n]
        if n in ("conv_w", "ffn_conv_w"):
            width = shape[-1]
            g = lax.dynamic_slice(g.reshape(3, N_CHIPS * width), (0, chip * width), (3, width))
        g_small[n] = g.reshape(shape)
    sizes = [g_small[n].size for n in _SMALL]
    rows = -(-sum(sizes) // 1024) * 8
    packed = [_pack([src[n] for n in _SMALL], rows) for src in (w, g_small, m, v)]
    d_s, m_s, v_s = _adamw(*packed, name="adamw_small")
    for n, d, mn, vn in zip(_SMALL, _unpack(d_s, sizes), _unpack(m_s, sizes), _unpack(v_s, sizes)):
        shape = w[n].shape
        g_out[n], d_out[n], m_out[n], v_out[n] = g_small[n], d.reshape(shape), mn.reshape(shape), vn.reshape(shape)

    return (loss, grad_x[None], *[g_out[n] for n in names], *[d_out[n] for n in names],
            *[m_out[n] for n in names], *[v_out[n] for n in names])
```

```python
import functools

import jax
import jax.numpy as jnp
from jax import lax
from jax.experimental import pallas as pl
from jax.experimental.pallas import tpu as pltpu

F32 = jnp.float32
BF16 = jnp.bfloat16

D_MODEL = 1024
CONV_WIDTH = 512
Q_LORA = 256
KV_LORA = 128
QK_NOPE = 64
QK_ROPE = 32
V_HEAD = 64
N_HEADS = 8
HEAD_PAD = 128
D_ATT = N_HEADS * HEAD_PAD
D_IN = 3 * CONV_WIDTH + Q_LORA + KV_LORA + QK_ROPE
D_IN_PAD = 3 * CONV_WIDTH + Q_LORA + KV_LORA + HEAD_PAD
D_FF = 2816
ROPE_THETA = 10000.0
EPS = 1e-6
SM_SCALE = (QK_NOPE + QK_ROPE) ** -0.5
ONES_LANE = V_HEAD

ADAM_LR = 0.001
ADAM_B1 = 0.9
ADAM_B2 = 0.999
ADAM_EPS = 1e-08
ADAM_WD = 0.01
ADAM_STEP = 10

N_CHIPS = 4
N_DEV = 8
MESH = pl.DeviceIdType.MESH
ANY = pl.BlockSpec(memory_space=pl.ANY)


def _params(sem):
    return pltpu.CompilerParams(dimension_semantics=sem)


def _mm(a, b, *, name, ta=False, tb=False, add=None, out_dtype=F32, tm=512, tn=512, tk=512,
        a_split=False, b_split=False, o_split=False):
    if a_split:
        _, m, kh = a.shape
        k = 2 * kh
    elif ta:
        k, m = a.shape
    else:
        m, k = a.shape
    if b_split:
        _, kb, nh = b.shape
        n = 2 * nh
    elif tb:
        n, kb = b.shape
    else:
        kb, n = b.shape
    assert kb == k, (name, a.shape, b.shape)
    tm, tn, tk = min(tm, m), min(tn, n), min(tk, k)
    assert m % tm == 0 and n % tn == 0 and k % tk == 0, (name, m, n, k, tm, tn, tk)
    gm, gn, gk = m // tm, n // tn, k // tk

    if a_split:
        assert gk % 2 == 0
        a_spec = pl.BlockSpec((None, tm, tk), lambda i, j, kk: (kk // (gk // 2), i, kk % (gk // 2)))
    elif ta:
        a_spec = pl.BlockSpec((tk, tm), lambda i, j, kk: (kk, i))
    else:
        a_spec = pl.BlockSpec((tm, tk), lambda i, j, kk: (i, kk))
    if b_split:
        assert gn % 2 == 0
        b_spec = pl.BlockSpec((None, tk, tn), lambda i, j, kk: (j // (gn // 2), kk, j % (gn // 2)))
    elif tb:
        b_spec = pl.BlockSpec((tn, tk), lambda i, j, kk: (j, kk))
    else:
        b_spec = pl.BlockSpec((tk, tn), lambda i, j, kk: (kk, j))
    if o_split:
        assert gn % 2 == 0
        o_spec = pl.BlockSpec((None, tm, tn), lambda i, j, kk: (j // (gn // 2), i, j % (gn // 2)))
        o_shape = jax.ShapeDtypeStruct((2, m, n // 2), out_dtype)
    else:
        o_spec = pl.BlockSpec((tm, tn), lambda i, j, kk: (i, j))
        o_shape = jax.ShapeDtypeStruct((m, n), out_dtype)
    dims = (((0 if ta else 1,), (1 if tb else 0,)), ((), ()))

    def body(*refs):
        if add is None:
            a_ref, b_ref, o_ref, acc_ref = refs
            add_ref = None
        else:
            a_ref, b_ref, add_ref, o_ref, acc_ref = refs
        kk = pl.program_id(2)

        @pl.when(kk == 0)
        def _():
            acc_ref[...] = jnp.zeros_like(acc_ref)

        acc_ref[...] += lax.dot_general(a_ref[...].astype(BF16), b_ref[...].astype(BF16), dims,
                                        preferred_element_type=F32)

        @pl.when(kk == gk - 1)
        def _():
            r = acc_ref[...]
            if add_ref is not None:
                r = r + add_ref[...]
            o_ref[...] = r.astype(o_ref.dtype)

    in_specs = [a_spec, b_spec]
    args = [a, b]
    if add is not None:
        in_specs.append(pl.BlockSpec((tm, tn), lambda i, j, kk: (i, j)))
        args.append(add)
    return pl.pallas_call(
        body, name=name, grid=(gm, gn, gk), in_specs=in_specs, out_specs=o_spec, out_shape=o_shape,
        scratch_shapes=[pltpu.VMEM((tm, tn), F32)],
        compiler_params=_params(("parallel", "parallel", "arbitrary")),
    )(*args)


def _rms_scale(v):
    return lax.rsqrt(jnp.mean(v * v, axis=-1, keepdims=True) + EPS)


def _rms_bwd_rows(v, g, dy):
    r = _rms_scale(v)
    vh = v * r
    dyg = dy * g
    dv = r * (dyg - vh * jnp.mean(dyg * vh, axis=-1, keepdims=True))
    return dv, dy * vh


def _shift_down(v, first_row):
    row = lax.broadcasted_iota(jnp.int32, v.shape, 0)
    return jnp.where(row == 0, first_row, pltpu.roll(v, 1, 0))


def _shift_up(v, last_row):
    n = v.shape[0]
    row = lax.broadcasted_iota(jnp.int32, v.shape, 0)
    return jnp.where(row == n - 1, last_row, pltpu.roll(v, n - 1, 0))


def _rope(t, cos, sin_a, sin_b):
    return t * cos + pltpu.roll(t, HEAD_PAD - 16, 1) * sin_a + pltpu.roll(t, 16, 1) * sin_b


def _rope_bwd(d, cos, sin_a, sin_b):
    return d * cos + pltpu.roll(d * sin_a, 16, 1) + pltpu.roll(d * sin_b, HEAD_PAD - 16, 1)


def _sigmoid(v):
    return 1.0 / (1.0 + jnp.exp(-v))


def _halo_specs(ts, s, width, col):
    nb = ts // 8
    last = s // 8 - 1
    prev = pl.BlockSpec((8, width), lambda i: (jnp.maximum(i * nb - 1, 0), col))
    nxt = pl.BlockSpec((8, width), lambda i: (jnp.minimum((i + 1) * nb, last), col))
    return prev, nxt


def _rms_fwd(x, g, *, name, ts=512):
    s, d = x.shape

    def body(x_ref, g_ref, h_ref):
        v = x_ref[...]
        h_ref[...] = (v * _rms_scale(v) * g_ref[...]).astype(h_ref.dtype)

    return pl.pallas_call(
        body, name=name, grid=(s // ts,),
        in_specs=[pl.BlockSpec((ts, d), lambda i: (i, 0)), pl.BlockSpec((1, d), lambda i: (0, 0))],
        out_specs=pl.BlockSpec((ts, d), lambda i: (i, 0)),
        out_shape=jax.ShapeDtypeStruct((s, d), BF16),
        compiler_params=_params(("parallel",)),
    )(x, g)


def _rms_bwd(x, g, dy, add, *, name, ts=256):
    s, d = x.shape

    def body(x_ref, g_ref, dy_ref, add_ref, dx_ref, dg_ref):
        @pl.when(pl.program_id(0) == 0)
        def _():
            dg_ref[...] = jnp.zeros_like(dg_ref)

        dv, dgr = _rms_bwd_rows(x_ref[...], g_ref[...], dy_ref[...])
        dx_ref[...] = dv + add_ref[...]
        dg_ref[...] += jnp.sum(dgr, axis=0, keepdims=True)

    row = pl.BlockSpec((ts, d), lambda i: (i, 0))
    vec = pl.BlockSpec((1, d), lambda i: (0, 0))
    return pl.pallas_call(
        body, name=name, grid=(s // ts,),
        in_specs=[row, vec, row, row], out_specs=[row, vec],
        out_shape=[jax.ShapeDtypeStruct((s, d), F32), jax.ShapeDtypeStruct((1, d), F32)],
        compiler_params=_params(("arbitrary",)),
    )(x, g, dy, add)


def _mix_pre(z, conv_w8, gq, gkv, cos, sin_a, sin_b, *, ts=256):
    s = z.shape[0]
    n = s // ts
    cw = CONV_WIDTH

    def body(z_ref, xcp, xcn, cgp, cgn, w_ref, gq_ref, gkv_ref, cos_ref, sa_ref, sb_ref,
             yc_ref, qn_ref, kvn_ref, kr_ref):
        i = pl.program_id(0)
        xc = z_ref[:, 0:cw]
        bg = z_ref[:, cw:2 * cw]
        cg = z_ref[:, 2 * cw:3 * cw]
        m = cg * xc
        m_prev = jnp.where(i > 0, xcp[7:8, :] * cgp[7:8, :], 0.0)
        m_next = jnp.where(i < n - 1, xcn[0:1, :] * cgn[0:1, :], 0.0)
        cm = _shift_down(m, m_prev) * w_ref[0:1, :] + m * w_ref[1:2, :] + _shift_up(m, m_next) * w_ref[2:3, :]
        yc_ref[...] = (bg * cm).astype(BF16)
        ql = z_ref[:, 3 * cw:3 * cw + Q_LORA]
        qn_ref[...] = (ql * _rms_scale(ql) * gq_ref[...]).astype(BF16)
        kvl = z_ref[:, 3 * cw + Q_LORA:3 * cw + Q_LORA + KV_LORA]
        kvn_ref[...] = (kvl * _rms_scale(kvl) * gkv_ref[...]).astype(BF16)
        kr_ref[...] = _rope(z_ref[:, D_IN_PAD - HEAD_PAD:D_IN_PAD], cos_ref[...], sa_ref[...], sb_ref[...])

    xcp, xcn = _halo_specs(ts, s, cw, 0)
    cgp, cgn = _halo_specs(ts, s, cw, 2)
    tab = pl.BlockSpec((ts, HEAD_PAD), lambda i: (i, 0))
    return pl.pallas_call(
        body, name="mix_pre", grid=(n,),
        in_specs=[pl.BlockSpec((ts, D_IN_PAD), lambda i: (i, 0)), xcp, xcn, cgp, cgn,
                  pl.BlockSpec((8, cw), lambda i: (0, 0)), pl.BlockSpec((1, Q_LORA), lambda i: (0, 0)),
                  pl.BlockSpec((1, KV_LORA), lambda i: (0, 0)), tab, tab, tab],
        out_specs=[pl.BlockSpec((ts, cw), lambda i: (i, 0)), pl.BlockSpec((ts, Q_LORA), lambda i: (i, 0)),
                   pl.BlockSpec((ts, KV_LORA), lambda i: (i, 0)), tab],
        out_shape=[jax.ShapeDtypeStruct((s, cw), BF16), jax.ShapeDtypeStruct((s, Q_LORA), BF16),
                   jax.ShapeDtypeStruct((s, KV_LORA), BF16), jax.ShapeDtypeStruct((s, HEAD_PAD), F32)],
        compiler_params=_params(("parallel",)),
    )(z, z, z, z, z, conv_w8, gq, gkv, cos, sin_a, sin_b)


def _mix_bwd(z, dyc, dqn, dkvn, dkr, conv_w8, gq, gkv, cos, sin_a, sin_b, *, ts=256):
    s = z.shape[0]
    n = s // ts
    cw = CONV_WIDTH

    def body(z_ref, xcp, xcn, bgp, bgn, cgp, cgn, dyc_ref, dycp, dycn, dqn_ref, dkvn_ref, dkr_ref,
             w_ref, gq_ref, gkv_ref, cos_ref, sa_ref, sb_ref,
             dz_ref, dw0_ref, dw1_ref, dw2_ref, dgq_ref, dgkv_ref):
        i = pl.program_id(0)

        @pl.when(i == 0)
        def _():
            for r in (dw0_ref, dw1_ref, dw2_ref, dgq_ref, dgkv_ref):
                r[...] = jnp.zeros_like(r)

        xc = z_ref[:, 0:cw]
        bg = z_ref[:, cw:2 * cw]
        cg = z_ref[:, 2 * cw:3 * cw]
        w0, w1, w2 = w_ref[0:1, :], w_ref[1:2, :], w_ref[2:3, :]
        m = cg * xc
        m_dn = _shift_down(m, jnp.where(i > 0, xcp[7:8, :] * cgp[7:8, :], 0.0))
        m_up = _shift_up(m, jnp.where(i < n - 1, xcn[0:1, :] * cgn[0:1, :], 0.0))
        cm = m_dn * w0 + m * w1 + m_up * w2
        dyc_v = dyc_ref[...]
        dcm = dyc_v * bg
        dcm_dn = _shift_down(dcm, jnp.where(i > 0, dycp[7:8, :] * bgp[7:8, :], 0.0))
        dcm_up = _shift_up(dcm, jnp.where(i < n - 1, dycn[0:1, :] * bgn[0:1, :], 0.0))
        dm = dcm_up * w0 + dcm * w1 + dcm_dn * w2
        dz_ref[:, 0:cw] = (dm * cg).astype(BF16)
        dz_ref[:, cw:2 * cw] = (dyc_v * cm).astype(BF16)
        dz_ref[:, 2 * cw:3 * cw] = (dm * xc).astype(BF16)
        dw0_ref[...] += jnp.sum(dcm * m_dn, axis=0, keepdims=True)
        dw1_ref[...] += jnp.sum(dcm * m, axis=0, keepdims=True)
        dw2_ref[...] += jnp.sum(dcm * m_up, axis=0, keepdims=True)

        dql, dgq_rows = _rms_bwd_rows(z_ref[:, 3 * cw:3 * cw + Q_LORA], gq_ref[...], dqn_ref[...])
        dz_ref[:, 3 * cw:3 * cw + Q_LORA] = dql.astype(BF16)
        dgq_ref[...] += jnp.sum(dgq_rows, axis=0, keepdims=True)
        dkvl, dgkv_rows = _rms_bwd_rows(z_ref[:, 3 * cw + Q_LORA:3 * cw + Q_LORA + KV_LORA], gkv_ref[...],
                                        dkvn_ref[...])
        dz_ref[:, 3 * cw + Q_LORA:3 * cw + Q_LORA + KV_LORA] = dkvl.astype(BF16)
        dgkv_ref[...] += jnp.sum(dgkv_rows, axis=0, keepdims=True)

        lane = lax.broadcasted_iota(jnp.int32, (ts, HEAD_PAD), 1)
        rope_lane = (lane >= QK_NOPE) & (lane < QK_NOPE + QK_ROPE)
        dk = _rope_bwd(dkr_ref[...], cos_ref[...], sa_ref[...], sb_ref[...])
        dz_ref[:, D_IN_PAD - HEAD_PAD:D_IN_PAD] = jnp.where(rope_lane, dk, 0.0).astype(BF16)

    xcp, xcn = _halo_specs(ts, s, cw, 0)
    bgp, bgn = _halo_specs(ts, s, cw, 1)
    cgp, cgn = _halo_specs(ts, s, cw, 2)
    dycp, dycn = _halo_specs(ts, s, cw, 0)
    tab = pl.BlockSpec((ts, HEAD_PAD), lambda i: (i, 0))

    def vec(width):
        return pl.BlockSpec((1, width), lambda i: (0, 0))

    outs = pl.pallas_call(
        body, name="mix_bwd", grid=(n,),
        in_specs=[pl.BlockSpec((ts, D_IN_PAD), lambda i: (i, 0)), xcp, xcn, bgp, bgn, cgp, cgn,
                  pl.BlockSpec((ts, cw), lambda i: (i, 0)), dycp, dycn,
                  pl.BlockSpec((ts, Q_LORA), lambda i: (i, 0)), pl.BlockSpec((ts, KV_LORA), lambda i: (i, 0)), tab,
                  pl.BlockSpec((8, cw), lambda i: (0, 0)), vec(Q_LORA), vec(KV_LORA), tab, tab, tab],
        out_specs=[pl.BlockSpec((ts, D_IN_PAD), lambda i: (i, 0)), vec(cw), vec(cw), vec(cw), vec(Q_LORA),
                   vec(KV_LORA)],
        out_shape=[jax.ShapeDtypeStruct((s, D_IN_PAD), BF16)] + [jax.ShapeDtypeStruct((1, cw), F32)] * 3
        + [jax.ShapeDtypeStruct((1, Q_LORA), F32), jax.ShapeDtypeStruct((1, KV_LORA), F32)],
        compiler_params=_params(("arbitrary",)),
    )(z, z, z, z, z, z, z, dyc, dyc, dyc, dqn, dkvn, dkr, conv_w8, gq, gkv, cos, sin_a, sin_b)
    dz, dw0, dw1, dw2, dgq, dgkv = outs
    return dz, jnp.concatenate([dw0, dw1, dw2], axis=0), dgq, dgkv


def _qkv_proj(qn, kvn, kr, w_uq_p, w_kv_p, cos, sin_a, sin_b, *, ts=512):
    s = qn.shape[0]

    def body(qn_ref, kvn_ref, kr_ref, wq_ref, wkv_ref, cos_ref, sa_ref, sb_ref, q_ref, k_ref, v_ref):
        cos_v, sa, sb = cos_ref[...], sa_ref[...], sb_ref[...]
        q = jnp.dot(qn_ref[...], wq_ref[...], preferred_element_type=F32)
        kv = jnp.dot(kvn_ref[...], wkv_ref[...], preferred_element_type=F32)
        kr_v = kr_ref[...]
        lane = lax.broadcasted_iota(jnp.int32, (1, HEAD_PAD), 1)
        ones_lane = (lane == ONES_LANE).astype(F32)
        for h in range(N_HEADS):
            blk = slice(h * HEAD_PAD, (h + 1) * HEAD_PAD)
            q_ref[:, blk] = (_rope(q[:, blk], cos_v, sa, sb) * SM_SCALE).astype(BF16)
            k_ref[:, blk] = (kv[:, blk] + kr_v).astype(BF16)
            v_ref[:, blk] = (kv[:, D_ATT + h * HEAD_PAD:D_ATT + (h + 1) * HEAD_PAD] + ones_lane).astype(BF16)

    tab = pl.BlockSpec((ts, HEAD_PAD), lambda i: (i, 0))
    wide = pl.BlockSpec((ts, D_ATT), lambda i: (i, 0))
    return pl.pallas_call(
        body, name="qkv_proj", grid=(s // ts,),
        in_specs=[pl.BlockSpec((ts, Q_LORA), lambda i: (i, 0)), pl.BlockSpec((ts, KV_LORA), lambda i: (i, 0)), tab,
                  pl.BlockSpec((Q_LORA, D_ATT), lambda i: (0, 0)), pl.BlockSpec((KV_LORA, 2 * D_ATT), lambda i: (0, 0)),
                  tab, tab, tab],
        out_specs=[wide, wide, wide],
        out_shape=[jax.ShapeDtypeStruct((s, D_ATT), BF16)] * 3,
        compiler_params=_params(("parallel",)),
    )(qn, kvn, kr, w_uq_p, w_kv_p, cos, sin_a, sin_b)


def _qk_bwd(dq, dk, cos, sin_a, sin_b, *, ts=256):
    s = dq.shape[0]

    def body(dq_ref, dk_ref, cos_ref, sa_ref, sb_ref, dqp_ref, dkr_ref):
        cos_v, sa, sb = cos_ref[...], sa_ref[...], sb_ref[...]
        tot = jnp.zeros((ts, HEAD_PAD), F32)
        for h in range(N_HEADS):
            blk = slice(h * HEAD_PAD, (h + 1) * HEAD_PAD)
            dqp_ref[:, blk] = _rope_bwd(dq_ref[:, blk], cos_v, sa, sb).astype(BF16)
            tot = tot + dk_ref[:, blk]
        dkr_ref[...] = tot

    tab = pl.BlockSpec((ts, HEAD_PAD), lambda i: (i, 0))
    wide = pl.BlockSpec((ts, D_ATT), lambda i: (i, 0))
    return pl.pallas_call(
        body, name="qk_bwd", grid=(s // ts,),
        in_specs=[wide, wide, tab, tab, tab], out_specs=[wide, tab],
        out_shape=[jax.ShapeDtypeStruct((s, D_ATT), BF16), jax.ShapeDtypeStruct((s, HEAD_PAD), F32)],
        compiler_params=_params(("parallel",)),
    )(dq, dk, cos, sin_a, sin_b)


_NT = (((1,), (1,)), ((), ()))
_TN = (((0,), (0,)), ((), ()))


def _flash_fwd(q, k, v, *, tq=512, tk=1024):
    s = q.shape[0]
    tq, tk = min(tq, s), min(tk, s)
    nk = s // tk

    def body(q_ref, k_ref, v_ref, o_ref, lse_ref):
        qv = q_ref[...]

        def step(j, carry):
            m, acc = carry
            rows = pl.ds(pl.multiple_of(j * tk, tk), tk)
            sc = lax.dot_general(qv, k_ref[rows, :], _NT, preferred_element_type=F32)
            m_new = jnp.maximum(m, jnp.max(sc, axis=1, keepdims=True))
            p = jnp.exp(sc - m_new).astype(BF16)
            acc = jnp.exp(m - m_new) * acc + jnp.dot(p, v_ref[rows, :], preferred_element_type=F32)
            return m_new, acc

        init = (jnp.full((tq, 1), -jnp.inf, F32), jnp.zeros((tq, HEAD_PAD), F32))
        m, acc = lax.fori_loop(0, nk, step, init)
        l = acc[:, ONES_LANE:ONES_LANE + 1]
        o_ref[...] = (acc / l).astype(BF16)
        lse_ref[...] = m + jnp.log(l)

    head = pl.BlockSpec((s, HEAD_PAD), lambda h, i: (0, h))
    return pl.pallas_call(
        body, name="flash_fwd", grid=(N_HEADS, s // tq),
        in_specs=[pl.BlockSpec((tq, HEAD_PAD), lambda h, i: (i, h)), head, head],
        out_specs=[pl.BlockSpec((tq, HEAD_PAD), lambda h, i: (i, h)),
                   pl.BlockSpec((None, tq, 1), lambda h, i: (h, i, 0))],
        out_shape=[jax.ShapeDtypeStruct((s, D_ATT), BF16), jax.ShapeDtypeStruct((N_HEADS, s, 1), F32)],
        compiler_params=_params(("parallel", "parallel")),
    )(q, k, v)


def _attn_delta(do, o, *, ts=512):
    s = do.shape[0]

    def body(do_ref, o_ref, dl_ref):
        for h in range(N_HEADS):
            blk = slice(h * HEAD_PAD, (h + 1) * HEAD_PAD)
            dl_ref[h] = jnp.sum(do_ref[:, blk].astype(F32) * o_ref[:, blk].astype(F32), axis=1, keepdims=True)

    wide = pl.BlockSpec((ts, D_ATT), lambda i: (i, 0))
    return pl.pallas_call(
        body, name="attn_delta", grid=(s // ts,), in_specs=[wide, wide],
        out_specs=pl.BlockSpec((N_HEADS, ts, 1), lambda i: (0, i, 0)),
        out_shape=jax.ShapeDtypeStruct((N_HEADS, s, 1), F32),
        compiler_params=_params(("parallel",)),
    )(do, o)


def _flash_bwd(q, qt, k, v, do, dot, lse, delta, *, tq=1024, tk=512):
    s = q.shape[0]
    tq, tk = min(tq, s), min(tk, s)
    nq = s // tq

    def body(q_ref, qt_ref, do_ref, dot_ref, lse_ref, dl_ref, k_ref, v_ref, dq_ref, dk_ref, dv_ref):
        j = pl.program_id(1)

        @pl.when(j == 0)
        def _():
            dq_ref[...] = jnp.zeros_like(dq_ref)

        kv, vv = k_ref[...], v_ref[...]

        def step(i, carry):
            dk_t, dv_t = carry
            at = pl.multiple_of(i * tq, tq)
            rows = pl.ds(at, tq)
            sc = lax.dot_general(q_ref[rows, :], kv, _NT, preferred_element_type=F32)
            p = jnp.exp(sc - lse_ref[rows, :])
            dp = lax.dot_general(do_ref[rows, :], vv, _NT, preferred_element_type=F32)
            ds = (p * (dp - dl_ref[rows, :])).astype(BF16)
            dv_t = dv_t + jnp.dot(dot_ref[:, rows], p.astype(BF16), preferred_element_type=F32)
            dk_t = dk_t + jnp.dot(qt_ref[:, rows], ds, preferred_element_type=F32)
            dq_ref[rows, :] += jnp.dot(ds, kv, preferred_element_type=F32)
            return dk_t, dv_t

        zero = jnp.zeros((HEAD_PAD, tk), F32)
        dk_t, dv_t = lax.fori_loop(0, nq, step, (zero, zero))
        dk_ref[...] = dk_t.T
        dv_ref[...] = dv_t.T

        @pl.when(j == pl.num_programs(1) - 1)
        def _():
            dq_ref[...] *= SM_SCALE

    head = pl.BlockSpec((s, HEAD_PAD), lambda h, j: (0, h))
    head_t = pl.BlockSpec((HEAD_PAD, s), lambda h, j: (h, 0))
    stat = pl.BlockSpec((None, s, 1), lambda h, j: (h, 0, 0))
    blk = pl.BlockSpec((tk, HEAD_PAD), lambda h, j: (j, h))
    return pl.pallas_call(
        body, name="flash_bwd", grid=(N_HEADS, s // tk),
        in_specs=[head, head_t, head, head_t, stat, stat, blk, blk],
        out_specs=[head, blk, blk],
        out_shape=[jax.ShapeDtypeStruct((s, D_ATT), F32)] * 3,
        compiler_params=_params(("parallel", "arbitrary")),
    )(q, qt, do, dot, lse, delta, k, v)


FFN_TC = 256


def _ffn_specs(ts, s, tc, row_axis):
    nb = ts // 8
    last = s // 8 - 1
    if row_axis == 0:
        main = pl.BlockSpec((2, ts, tc), lambda i, j: (0, i, j))
        prev = pl.BlockSpec((2, 8, tc), lambda i, j: (0, jnp.maximum(i * nb - 1, 0), j))
        nxt = pl.BlockSpec((2, 8, tc), lambda i, j: (0, jnp.minimum((i + 1) * nb, last), j))
    else:
        main = pl.BlockSpec((2, ts, tc), lambda j, i: (0, i, j))
        prev = pl.BlockSpec((2, 8, tc), lambda j, i: (0, jnp.maximum(i * nb - 1, 0), j))
        nxt = pl.BlockSpec((2, 8, tc), lambda j, i: (0, jnp.minimum((i + 1) * nb, last), j))
    return main, prev, nxt


def _ffn_conv(a_ref, ap_ref, an_ref, w_ref, b_ref, half, first, last):
    a = a_ref[half]
    a_dn = _shift_down(a, jnp.where(first, 0.0, ap_ref[half, 7:8, :]))
    a_up = _shift_up(a, jnp.where(last, 0.0, an_ref[half, 0:1, :]))
    w = w_ref[half]
    out = a_dn * w[0:1, :] + a * w[1:2, :] + a_up * w[2:3, :] + b_ref[half]
    return out, a_dn, a, a_up


def _ffn_act(a_pre, w, b, *, ts=512, tc=FFN_TC):
    s = a_pre.shape[1]
    n = s // ts

    def body(a_ref, ap_ref, an_ref, w_ref, b_ref, o_ref):
        i = pl.program_id(0)
        g = _ffn_conv(a_ref, ap_ref, an_ref, w_ref, b_ref, 0, i == 0, i == n - 1)[0]
        u = _ffn_conv(a_ref, ap_ref, an_ref, w_ref, b_ref, 1, i == 0, i == n - 1)[0]
        o_ref[...] = (g * _sigmoid(g) * u).astype(BF16)

    main, prev, nxt = _ffn_specs(ts, s, tc, 0)
    return pl.pallas_call(
        body, name="ffn_act", grid=(n, D_FF // tc),
        in_specs=[main, prev, nxt, pl.BlockSpec((2, 8, tc), lambda i, j: (0, 0, j)),
                  pl.BlockSpec((2, 1, tc), lambda i, j: (0, 0, j))],
        out_specs=pl.BlockSpec((ts, tc), lambda i, j: (i, j)),
        out_shape=jax.ShapeDtypeStruct((s, D_FF), BF16),
        compiler_params=_params(("parallel", "parallel")),
    )(a_pre, a_pre, a_pre, w, b)


def _ffn_act_bwd(a_pre, dact, w, b, *, ts=512, tc=FFN_TC):
    s = a_pre.shape[1]
    n = s // ts

    def body(a_ref, ap_ref, an_ref, dact_ref, w_ref, b_ref, da_ref, dw_ref, db_ref):
        i = pl.program_id(1)

        @pl.when(i == 0)
        def _():
            dw_ref[...] = jnp.zeros_like(dw_ref)
            db_ref[...] = jnp.zeros_like(db_ref)

        g, g_dn, g_c, g_up = _ffn_conv(a_ref, ap_ref, an_ref, w_ref, b_ref, 0, i == 0, i == n - 1)
        u, u_dn, u_c, u_up = _ffn_conv(a_ref, ap_ref, an_ref, w_ref, b_ref, 1, i == 0, i == n - 1)
        dact_v = dact_ref[...]
        sg = _sigmoid(g)
        dg = dact_v * u * (sg * (1.0 + g * (1.0 - sg)))
        du = dact_v * (g * sg)
        da_ref[0] = dg
        da_ref[1] = du
        for half, (d, dn, c, up) in enumerate(((dg, g_dn, g_c, g_up), (du, u_dn, u_c, u_up))):
            dw_ref[half, 0:1, :] += jnp.sum(d * dn, axis=0, keepdims=True)
            dw_ref[half, 1:2, :] += jnp.sum(d * c, axis=0, keepdims=True)
            dw_ref[half, 2:3, :] += jnp.sum(d * up, axis=0, keepdims=True)
            db_ref[half] += jnp.sum(d, axis=0, keepdims=True)

    main, prev, nxt = _ffn_specs(ts, s, tc, 1)
    return pl.pallas_call(
        body, name="ffn_act_bwd", grid=(D_FF // tc, n),
        in_specs=[main, prev, nxt, pl.BlockSpec((ts, tc), lambda j, i: (i, j)),
                  pl.BlockSpec((2, 8, tc), lambda j, i: (0, 0, j)), pl.BlockSpec((2, 1, tc), lambda j, i: (0, 0, j))],
        out_specs=[main, pl.BlockSpec((2, 8, tc), lambda j, i: (0, 0, j)),
                   pl.BlockSpec((2, 1, tc), lambda j, i: (0, 0, j))],
        out_shape=[jax.ShapeDtypeStruct((2, s, D_FF), F32), jax.ShapeDtypeStruct((2, 8, D_FF), F32),
                   jax.ShapeDtypeStruct((2, 1, D_FF), F32)],
        compiler_params=_params(("parallel", "arbitrary")),
    )(a_pre, a_pre, a_pre, dact, w, b)


def _ffn_conv_t(da, w, *, ts=512, tc=FFN_TC):
    s = da.shape[1]
    n = s // ts

    def body(d_ref, dp_ref, dn_ref, w_ref, o_ref):
        i = pl.program_id(0)
        for half in range(2):
            d = d_ref[half]
            d_dn = _shift_down(d, jnp.where(i == 0, 0.0, dp_ref[half, 7:8, :]))
            d_up = _shift_up(d, jnp.where(i == n - 1, 0.0, dn_ref[half, 0:1, :]))
            wv = w_ref[half]
            o_ref[half] = (d_up * wv[0:1, :] + d * wv[1:2, :] + d_dn * wv[2:3, :]).astype(BF16)

    main, prev, nxt = _ffn_specs(ts, s, tc, 0)
    return pl.pallas_call(
        body, name="ffn_conv_t", grid=(n, D_FF // tc),
        in_specs=[main, prev, nxt, pl.BlockSpec((2, 8, tc), lambda i, j: (0, 0, j))],
        out_specs=main, out_shape=jax.ShapeDtypeStruct((2, s, D_FF), BF16),
        compiler_params=_params(("parallel", "parallel")),
    )(da, da, da, w)


def _ple_final(x2, gl, pp, target, gf, *, ts=256):
    s, d = x2.shape

    def body(x2_ref, gl_ref, pp_ref, t_ref, gf_ref, loss_ref, dx3_ref, dgl_ref, dpp_ref, dgf_ref):
        @pl.when(pl.program_id(0) == 0)
        def _():
            loss_ref[...] = jnp.zeros_like(loss_ref)
            dgf_ref[...] = jnp.zeros_like(dgf_ref)

        gate = _sigmoid(gl_ref[...])
        ppv = pp_ref[...]
        x3 = x2_ref[...] + gate * ppv
        gfv = gf_ref[...]
        err = x3 * _rms_scale(x3) * gfv - t_ref[...]
        loss_ref[...] += 0.5 * jnp.sum(jnp.mean(err * err, axis=-1, keepdims=True), axis=0, keepdims=True)
        dx3, dgf_rows = _rms_bwd_rows(x3, gfv, err * (1.0 / d))
        dgf_ref[...] += jnp.sum(dgf_rows, axis=0, keepdims=True)
        dx3_ref[...] = dx3
        dgl_ref[...] = (dx3 * ppv * gate * (1.0 - gate)).astype(BF16)
        dpp_ref[...] = (dx3 * gate).astype(BF16)

    row = pl.BlockSpec((ts, d), lambda i: (i, 0))
    vec = pl.BlockSpec((1, d), lambda i: (0, 0))
    return pl.pallas_call(
        body, name="ple_final", grid=(s // ts,),
        in_specs=[row, row, row, row, vec],
        out_specs=[pl.BlockSpec((1, 128), lambda i: (0, 0)), row, row, row, vec],
        out_shape=[jax.ShapeDtypeStruct((1, 128), F32), jax.ShapeDtypeStruct((s, d), F32),
                   jax.ShapeDtypeStruct((s, d), BF16), jax.ShapeDtypeStruct((s, d), BF16),
                   jax.ShapeDtypeStruct((1, d), F32)],
        compiler_params=_params(("arbitrary",)),
    )(x2, gl, pp, target, gf)


def _row_tile(rows, cols, n_arrays, budget=12 << 20):
    best = None
    for t in range(8, rows + 1, 8):
        if rows % t == 0 and t * cols * 4 * n_arrays <= budget:
            best = t
    return rows if best is None else best


def _sum_slots(a, *, name):
    g, r, c = a.shape
    tr = _row_tile(r, c, g + 1)

    def body(*refs):
        tot = refs[0][...]
        for ref in refs[1:g]:
            tot = tot + ref[...]
        refs[g][...] = tot

    specs = [pl.BlockSpec((None, tr, c), functools.partial(lambda i, slot: (slot, i, 0), slot=k)) for k in range(g)]
    return pl.pallas_call(
        body, name=name, grid=(r // tr,), in_specs=specs, out_specs=pl.BlockSpec((tr, c), lambda i: (i, 0)),
        out_shape=jax.ShapeDtypeStruct((r, c), a.dtype), compiler_params=_params(("parallel",)),
    )(*([a] * g))


def _adamw(w, g, m, v, *, name):
    r, c = w.shape
    tr = _row_tile(r, c, 7)

    def body(w_ref, g_ref, m_ref, v_ref, d_ref, mo_ref, vo_ref):
        gv = g_ref[...]
        mn = ADAM_B1 * m_ref[...] + (1.0 - ADAM_B1) * gv
        vn = ADAM_B2 * v_ref[...] + (1.0 - ADAM_B2) * (gv * gv)
        m_hat = mn / (1.0 - ADAM_B1 ** ADAM_STEP)
        v_hat = vn / (1.0 - ADAM_B2 ** ADAM_STEP)
        d_ref[...] = -ADAM_LR * (m_hat / (jnp.sqrt(v_hat) + ADAM_EPS) + ADAM_WD * w_ref[...])
        mo_ref[...] = mn
        vo_ref[...] = vn

    blk = pl.BlockSpec((tr, c), lambda i: (i, 0))
    return pl.pallas_call(
        body, name=name, grid=(r // tr,), in_specs=[blk] * 4, out_specs=[blk] * 3,
        out_shape=[jax.ShapeDtypeStruct((r, c), F32)] * 3, compiler_params=_params(("parallel",)),
    )(w, g, m, v)


def _position():
    x, y, c = lax.axis_index("x"), lax.axis_index("y"), lax.axis_index("c")
    return x, y, c


def _other_chips(x, y):
    return [(1 - x, y), (x, 1 - y), (1 - x, 1 - y)]


def _stage_in(srcs, stage, sems):
    cps = [pltpu.make_async_copy(src, stage[a], sems.at[a]) for a, src in enumerate(srcs)]
    for cp in cps:
        cp.start()
    return cps


def _stage_out(staged, stage, dsts, sems):
    cps = []
    for a, dst in enumerate(dsts):
        staged[a].wait()
        cp = pltpu.make_async_copy(stage[a], dst, sems.at[a])
        cp.start()
        cps.append(cp)
    return cps


def _gather_chips(shards):
    n = len(shards)

    def body(*refs):
        ins, outs, stage = refs[:n], refs[n:2 * n], refs[2 * n:3 * n]
        send_sems, recv_sems, in_sems, out_sems = refs[3 * n:]
        x, y, c = _position()
        me = 2 * x + y
        chips = _other_chips(x, y)
        remote = []
        staged = _stage_in(ins, stage, in_sems)
        for a in range(n):
            for k, (px, py) in enumerate(chips):
                rc = pltpu.make_async_remote_copy(
                    src_ref=ins[a], dst_ref=outs[a].at[me], send_sem=send_sems.at[3 * a + k],
                    recv_sem=recv_sems.at[3 * a + k], device_id=(px, py, c), device_id_type=MESH)
                rc.start()
                remote.append(rc)
        local = _stage_out(staged, stage, [o.at[me] for o in outs], out_sems)
        for a in range(n):
            for k, (px, py) in enumerate(chips):
                pltpu.make_async_remote_copy(
                    src_ref=ins[a], dst_ref=outs[a].at[2 * px + py], send_sem=send_sems.at[3 * a + k],
                    recv_sem=recv_sems.at[3 * a + k], device_id=(px, py, c), device_id_type=MESH).wait_recv()
        for rc in remote:
            rc.wait_send()
        for cp in local:
            cp.wait()

    return pl.pallas_call(
        body, name="gather_chips", in_specs=[ANY] * n, out_specs=[ANY] * n,
        out_shape=[jax.ShapeDtypeStruct((N_CHIPS,) + s.shape, s.dtype) for s in shards],
        scratch_shapes=[pltpu.VMEM(s.shape, s.dtype) for s in shards]
        + [pltpu.SemaphoreType.DMA((3 * n,)), pltpu.SemaphoreType.DMA((3 * n,)),
           pltpu.SemaphoreType.DMA((n,)), pltpu.SemaphoreType.DMA((n,))],
        compiler_params=pltpu.CompilerParams(has_side_effects=True),
    )(*shards)


def _send_other_halves(grads):
    n = len(grads)

    def body(*refs):
        ins, sib = refs[:n], refs[n:2 * n]
        send_sems, recv_sems = refs[2 * n:]
        x, y, c = _position()
        remote = []
        for a in range(n):
            half = ins[a].shape[1] // 2
            give = ins[a].at[:, pl.ds(pl.multiple_of((1 - c) * half, 8), half), :]
            rc = pltpu.make_async_remote_copy(
                src_ref=give, dst_ref=sib[a], send_sem=send_sems.at[a], recv_sem=recv_sems.at[a],
                device_id=(x, y, 1 - c), device_id_type=MESH)
            rc.start()
            remote.append(rc)
        for rc in remote:
            rc.wait_recv()
        for rc in remote:
            rc.wait_send()

    return pl.pallas_call(
        body, name="send_other_halves", in_specs=[ANY] * n, out_specs=[ANY] * n,
        out_shape=[jax.ShapeDtypeStruct((g.shape[0], g.shape[1] // 2, g.shape[2]), g.dtype) for g in grads],
        scratch_shapes=[pltpu.SemaphoreType.DMA((n,)), pltpu.SemaphoreType.DMA((n,))],
        compiler_params=pltpu.CompilerParams(has_side_effects=True),
    )(*grads)


def _add_own_half(g4, sib, core, *, name):
    g, a2, c = sib.shape
    tr = _row_tile(a2, c, 4)

    def body(core_ref, a_ref, b_ref, o_ref, o16_ref):
        tot = a_ref[...] + b_ref[...]
        o_ref[...] = tot
        o16_ref[...] = tot.astype(BF16)

    blk = pl.BlockSpec((None, tr, c), lambda i, j, core_ref: (i, j, 0))
    return pl.pallas_call(
        body, name=name,
        grid_spec=pltpu.PrefetchScalarGridSpec(
            num_scalar_prefetch=1, grid=(g, a2 // tr),
            in_specs=[pl.BlockSpec((None, None, tr, c), lambda i, j, core_ref: (i, core_ref[0], j, 0)), blk],
            out_specs=[blk, blk]),
        out_shape=[jax.ShapeDtypeStruct(sib.shape, F32), jax.ShapeDtypeStruct(sib.shape, BF16)],
        compiler_params=_params(("parallel", "parallel")),
    )(core, g4.reshape(g, 2, a2, c), sib)


def _scatter_chips(parts):
    n = len(parts)

    def body(*refs):
        ins, outs = refs[:n], refs[n:2 * n]
        send_sems, recv_sems = refs[2 * n:]
        x, y, c = _position()
        me = 2 * x + y
        chips = _other_chips(x, y)
        remote = []
        for a in range(n):
            for k, (px, py) in enumerate(chips):
                rc = pltpu.make_async_remote_copy(
                    src_ref=ins[a].at[2 * px + py], dst_ref=outs[a].at[me], send_sem=send_sems.at[3 * a + k],
                    recv_sem=recv_sems.at[3 * a + k], device_id=(px, py, c), device_id_type=MESH)
                rc.start()
                remote.append(rc)
        for a in range(n):
            for k, (px, py) in enumerate(chips):
                pltpu.make_async_remote_copy(
                    src_ref=ins[a].at[me], dst_ref=outs[a].at[2 * px + py], send_sem=send_sems.at[3 * a + k],
                    recv_sem=recv_sems.at[3 * a + k], device_id=(px, py, c), device_id_type=MESH).wait_recv()
        for rc in remote:
            rc.wait_send()

    return pl.pallas_call(
        body, name="scatter_chips", in_specs=[ANY] * n, out_specs=[ANY] * n,
        out_shape=[jax.ShapeDtypeStruct(p.shape, p.dtype) for p in parts],
        scratch_shapes=[pltpu.SemaphoreType.DMA((3 * n,)), pltpu.SemaphoreType.DMA((3 * n,))],
        compiler_params=pltpu.CompilerParams(has_side_effects=True),
    )(*parts)


def _sum_chips(landed, own, chip, *, name):
    g, r, c = landed.shape
    tr = _row_tile(r, c, 5)

    def body(chip_ref, *refs):
        me = chip_ref[0]
        own_v = refs[g][...]
        tot = None
        for slot in range(g):
            term = jnp.where(me == slot, own_v, refs[slot][...].astype(F32))
            tot = term if tot is None else tot + term
        refs[g + 1][...] = tot

    def landed_spec(slot):
        return pl.BlockSpec((None, tr, c),
                            lambda i, chip_ref: (jnp.where(chip_ref[0] == slot, (slot + 1) % g, slot), i, 0))

    return pl.pallas_call(
        body, name=name,
        grid_spec=pltpu.PrefetchScalarGridSpec(
            num_scalar_prefetch=1, grid=(r // tr,),
            in_specs=[landed_spec(k) for k in range(g)]
            + [pl.BlockSpec((None, tr, c), lambda i, chip_ref: (chip_ref[0], i, 0))],
            out_specs=pl.BlockSpec((tr, c), lambda i, chip_ref: (i, 0))),
        out_shape=jax.ShapeDtypeStruct((r, c), F32), compiler_params=_params(("parallel",)),
    )(chip, *([landed] * g), own)


def _join_halves(halves):
    n = len(halves)

    def body(*refs):
        ins, outs, stage = refs[:n], refs[n:2 * n], refs[2 * n:3 * n]
        send_sems, recv_sems, in_sems, out_sems = refs[3 * n:]
        x, y, c = _position()
        remote = []
        staged = _stage_in(ins, stage, in_sems)
        for a in range(n):
            rc = pltpu.make_async_remote_copy(
                src_ref=ins[a], dst_ref=outs[a].at[c], send_sem=send_sems.at[a], recv_sem=recv_sems.at[a],
                device_id=(x, y, 1 - c), device_id_type=MESH)
            rc.start()
            remote.append(rc)
        local = _stage_out(staged, stage, [o.at[c] for o in outs], out_sems)
        for a in range(n):
            pltpu.make_async_remote_copy(
                src_ref=ins[a], dst_ref=outs[a].at[1 - c], send_sem=send_sems.at[a], recv_sem=recv_sems.at[a],
                device_id=(x, y, 1 - c), device_id_type=MESH).wait_recv()
        for rc in remote:
            rc.wait_send()
        for cp in local:
            cp.wait()

    return pl.pallas_call(
        body, name="join_halves", in_specs=[ANY] * n, out_specs=[ANY] * n,
        out_shape=[jax.ShapeDtypeStruct((2,) + h.shape, h.dtype) for h in halves],
        scratch_shapes=[pltpu.VMEM(h.shape, h.dtype) for h in halves]
        + [pltpu.SemaphoreType.DMA((n,)), pltpu.SemaphoreType.DMA((n,)), pltpu.SemaphoreType.DMA((n,)),
           pltpu.SemaphoreType.DMA((n,))],
        compiler_params=pltpu.CompilerParams(has_side_effects=True),
    )(*halves)


def _gather_all(buf):
    def body(in_ref, out_ref, send_sems, recv_sems, local_sem):
        x, y, c = _position()
        me = 4 * x + 2 * y + c
        peers = [(x, y, 1 - c)] + [(px, py, pc) for (px, py) in _other_chips(x, y) for pc in (c, 1 - c)]
        cp = pltpu.make_async_copy(in_ref, out_ref.at[me], local_sem)
        cp.start()
        remote = []
        for k, peer in enumerate(peers):
            rc = pltpu.make_async_remote_copy(
                src_ref=in_ref, dst_ref=out_ref.at[me], send_sem=send_sems.at[k], recv_sem=recv_sems.at[k],
                device_id=peer, device_id_type=MESH)
            rc.start()
            remote.append(rc)
        for k, (px, py, pc) in enumerate(peers):
            pltpu.make_async_remote_copy(
                src_ref=in_ref, dst_ref=out_ref.at[4 * px + 2 * py + pc], send_sem=send_sems.at[k],
                recv_sem=recv_sems.at[k], device_id=(px, py, pc), device_id_type=MESH).wait_recv()
        for rc in remote:
            rc.wait_send()
        cp.wait()

    return pl.pallas_call(
        body, name="gather_all", in_specs=[ANY], out_specs=ANY,
        out_shape=jax.ShapeDtypeStruct((N_DEV,) + buf.shape, buf.dtype),
        scratch_shapes=[pltpu.SemaphoreType.DMA((N_DEV - 1,)), pltpu.SemaphoreType.DMA((N_DEV - 1,)),
                        pltpu.SemaphoreType.DMA],
        compiler_params=pltpu.CompilerParams(has_side_effects=True),
    )(buf)


def _cols_from_shards(g4):
    _, k, n = g4.shape
    return g4.transpose(1, 0, 2).reshape(k, N_CHIPS * n)


def _cols_to_shards(w):
    k, n = w.shape
    return w.reshape(k, N_CHIPS, n // N_CHIPS).transpose(1, 0, 2)


def _pad_heads(w, width):
    k = w.shape[0]
    w3 = w.reshape(k, N_HEADS, width)
    return jnp.pad(w3, ((0, 0), (0, 0), (0, HEAD_PAD - width))).reshape(k, D_ATT)


def _unpad_heads(w, width):
    k = w.shape[0]
    return w.reshape(k, N_HEADS, HEAD_PAD)[:, :, :width]


def _rope_tables(s):
    pos = jnp.arange(s, dtype=F32)
    inv_freq = ROPE_THETA ** (-jnp.arange(0, QK_ROPE, 2, dtype=F32) / QK_ROPE)
    ang = pos[:, None] * inv_freq[None, :]
    cos_h, sin_h = jnp.cos(ang), jnp.sin(ang)
    half = QK_ROPE // 2
    z = jnp.zeros((s, half), F32)
    ones = jnp.ones((s, QK_NOPE), F32)
    tail = jnp.zeros((s, HEAD_PAD - QK_NOPE - QK_ROPE), F32)
    cos = jnp.concatenate([ones, cos_h, cos_h, tail + 1.0], axis=1)
    sin_a = jnp.concatenate([ones * 0.0, -sin_h, z, tail], axis=1)
    sin_b = jnp.concatenate([ones * 0.0, z, sin_h, tail], axis=1)
    return cos, sin_a, sin_b


def _local_step(x, p, target, wts):
    s = x.shape[0]
    cos, sin_a, sin_b = _rope_tables(s)
    g1, gq, gkv, g2, g3, gf = (wts[k] for k in ("norm_mix_g", "q_norm_g", "kv_norm_g", "norm_ffn_g", "ple_norm_g",
                                                 "final_norm_g"))
    w_in_p, w_uq_p, w_kv_p = wts["w_in_p"], wts["w_uq_p"], wts["w_kv_p"]
    w_o_a, w_o_b, w_up, w_down = wts["w_o_a"], wts["w_o_b"], wts["w_up"], wts["w_down"]
    w_pg, w_pp = wts["w_ple_gate"], wts["w_ple_proj"]
    conv_w8, fconv_w, fconv_b = wts["conv_w8"], wts["ffn_conv_w"], wts["ffn_conv_b"]

    h = _rms_fwd(x, g1, name="rms_mix")
    z = _mm(h, w_in_p, name="mm_in", tm=512, tn=1024, tk=1024)
    y_conv, qn, kvn, kr = _mix_pre(z, conv_w8, gq, gkv, cos, sin_a, sin_b)
    q, k, v = _qkv_proj(qn, kvn, kr, w_uq_p, w_kv_p, cos, sin_a, sin_b)
    o, lse = _flash_fwd(q, k, v)
    t = _mm(y_conv, w_o_a, add=x, name="mm_o_conv", tm=512, tn=1024, tk=512)
    x1 = _mm(o, w_o_b, add=t, name="mm_o_att", tm=512, tn=1024, tk=1024)
    hf = _rms_fwd(x1, g2, name="rms_ffn")
    a_pre = _mm(hf, w_up, o_split=True, name="mm_up", tm=1024, tn=1408, tk=1024)
    act = _ffn_act(a_pre, fconv_w, fconv_b)
    x2 = _mm(act, w_down, add=x1, name="mm_down", tm=512, tn=1024, tk=1408)
    n3 = _rms_fwd(x2, g3, name="rms_ple")
    gl = _mm(n3, w_pg, name="mm_gate", tm=512, tn=1024, tk=1024)
    pp = _mm(p, w_pp, name="mm_ple", tm=512, tn=1024, tk=256)
    loss, dx3, dgl, dpp, d_gf = _ple_final(x2, gl, pp, target, gf)

    grads = {"final_norm_g": d_gf}
    grads["w_ple_proj"] = _mm(p, dpp, ta=True, name="mm_d_wpp", tm=256, tn=1024, tk=1024)
    grads["w_ple_gate"] = _mm(n3, dgl, ta=True, name="mm_d_wpg", tm=512, tn=1024, tk=1024)
    dn3 = _mm(dgl, w_pg, tb=True, name="mm_d_n3", tm=512, tn=1024, tk=1024)
    dx2, grads["ple_norm_g"] = _rms_bwd(x2, g3, dn3, dx3, name="rms_ple_bwd")
    grads["w_down"] = _mm(act, dx2, ta=True, name="mm_d_wdown", tm=1408, tn=512, tk=1024)
    dact = _mm(dx2, w_down, tb=True, name="mm_d_act", tm=512, tn=1408, tk=1024)
    da, d_fconv_w, d_fconv_b = _ffn_act_bwd(a_pre, dact, fconv_w, fconv_b)
    grads["ffn_conv_w"], grads["ffn_conv_b"] = d_fconv_w, d_fconv_b
    da_pre = _ffn_conv_t(da, fconv_w)
    grads["w_up"] = _mm(hf, da_pre, ta=True, b_split=True, name="mm_d_wup", tm=512, tn=1408, tk=1024)
    dhf = _mm(da_pre, w_up, tb=True, a_split=True, name="mm_d_hf", tm=512, tn=1024, tk=1408)
    dx1, grads["norm_ffn_g"] = _rms_bwd(x1, g2, dhf, dx2, name="rms_ffn_bwd")
    grads["w_o_a"] = _mm(y_conv, dx1, ta=True, name="mm_d_wo_conv", tm=512, tn=1024, tk=1024)
    grads["w_o_b"] = _mm(o, dx1, ta=True, name="mm_d_wo_att", tm=512, tn=1024, tk=1024)
    dyc = _mm(dx1, w_o_a, tb=True, name="mm_d_yconv", tm=512, tn=512, tk=1024)
    do = _mm(dx1, w_o_b, tb=True, out_dtype=BF16, name="mm_d_o", tm=512, tn=1024, tk=1024)
    delta = _attn_delta(do, o)
    dq, dk, dv = _flash_bwd(q, q.T, k, v, do, do.T, lse, delta)
    dq_pre, dkr = _qk_bwd(dq, dk, cos, sin_a, sin_b)
    grads["w_uq_p"] = _mm(qn, dq_pre, ta=True, name="mm_d_wuq", tm=256, tn=1024, tk=1024)
    dqn = _mm(dq_pre, w_uq_p, tb=True, name="mm_d_qn", tm=512, tn=256, tk=1024)
    grads["w_k_p"] = _mm(kvn, dk, ta=True, name="mm_d_wk", tm=128, tn=1024, tk=1024)
    grads["w_v_p"] = _mm(kvn, dv, ta=True, name="mm_d_wv", tm=128, tn=1024, tk=1024)
    dkvn_k = _mm(dk, w_kv_p[:, :D_ATT], tb=True, name="mm_d_kvn_k", tm=512, tn=128, tk=1024)
    dkvn = _mm(dv, w_kv_p[:, D_ATT:], tb=True, add=dkvn_k, name="mm_d_kvn_v", tm=512, tn=128, tk=1024)
    dz, grads["conv_w"], grads["q_norm_g"], grads["kv_norm_g"] = _mix_bwd(
        z, dyc, dqn, dkvn, dkr, conv_w8, gq, gkv, cos, sin_a, sin_b)
    grads["w_in_p"] = _mm(h, dz, ta=True, name="mm_d_win", tm=512, tn=1024, tk=1024)
    dh = _mm(dz, w_in_p, tb=True, name="mm_d_h", tm=512, tn=1024, tk=1024)
    grad_x, grads["norm_mix_g"] = _rms_bwd(x, g1, dh, dx1, name="rms_mix_bwd")
    return loss[0, 0], grad_x, grads


_BIG = ("w_in", "w_uq", "w_ukv", "w_o", "w_up", "w_down", "w_ple_gate", "w_ple_proj")
_COL_SHARDED = ("w_in", "w_uq", "w_ukv", "w_up", "w_ple_proj")
_SMALL = ("norm_mix_g", "conv_w", "q_norm_g", "kv_norm_g", "norm_ffn_g", "ffn_conv_w", "ffn_conv_b", "ple_norm_g",
          "final_norm_g")


def _gathered_weights(w):
    shards = [w[n][0].astype(BF16) for n in _BIG]
    shards.append(jnp.pad(w["conv_w"][0], ((0, 5), (0, 0))))
    shards.append(jnp.pad(w["ffn_conv_w"][0], ((0, 5), (0, 0))))
    got = _gather_chips(shards)
    full = {}
    for n, g4 in zip(_BIG, got[:len(_BIG)]):
        full[n] = _cols_from_shards(g4) if n in _COL_SHARDED else g4.reshape(-1, g4.shape[2])
    full["conv_w8"] = _cols_from_shards(got[len(_BIG)])
    full["ffn_conv_w8"] = _cols_from_shards(got[len(_BIG) + 1])
    return _layout_weights(full, w)


def _layout_weights(full, w):
    out = {n: w[n] for n in ("norm_mix_g", "q_norm_g", "kv_norm_g", "norm_ffn_g", "ple_norm_g")}
    out["final_norm_g"] = w["final_norm_g"][None, :]
    w_in = full["w_in"]
    zc = jnp.zeros((D_MODEL, QK_NOPE), BF16)
    zt = jnp.zeros((D_MODEL, HEAD_PAD - QK_NOPE - QK_ROPE), BF16)
    out["w_in_p"] = jnp.concatenate([w_in[:, :D_IN - QK_ROPE], zc, w_in[:, D_IN - QK_ROPE:], zt], axis=1)
    out["w_uq_p"] = _pad_heads(full["w_uq"], QK_NOPE + QK_ROPE)
    kv3 = full["w_ukv"].reshape(KV_LORA, N_HEADS, QK_NOPE + V_HEAD)
    out["w_kv_p"] = jnp.concatenate([_pad_heads(kv3[:, :, :QK_NOPE].reshape(KV_LORA, -1), QK_NOPE),
                                     _pad_heads(kv3[:, :, QK_NOPE:].reshape(KV_LORA, -1), V_HEAD)], axis=1)
    w_o = full["w_o"]
    out["w_o_a"] = w_o[:CONV_WIDTH]
    out["w_o_b"] = jnp.pad(w_o[CONV_WIDTH:].reshape(N_HEADS, V_HEAD, D_MODEL),
                           ((0, 0), (0, HEAD_PAD - V_HEAD), (0, 0))).reshape(D_ATT, D_MODEL)
    out["w_up"], out["w_down"] = full["w_up"], full["w_down"]
    out["w_ple_gate"], out["w_ple_proj"] = full["w_ple_gate"], full["w_ple_proj"]
    out["conv_w8"] = full["conv_w8"]
    fw = full["ffn_conv_w8"]
    out["ffn_conv_w"] = jnp.stack([fw[:, :D_FF], fw[:, D_FF:]])
    out["ffn_conv_b"] = w["ffn_conv_b"].reshape(2, 1, D_FF)
    return out


def _true_gradients(g):
    out = {}
    wp = g["w_in_p"]
    out["w_in"] = jnp.concatenate([wp[:, :D_IN - QK_ROPE], wp[:, D_IN_PAD - HEAD_PAD + QK_NOPE:
                                                              D_IN_PAD - HEAD_PAD + QK_NOPE + QK_ROPE]], axis=1)
    out["w_uq"] = _unpad_heads(g["w_uq_p"], QK_NOPE + QK_ROPE).reshape(Q_LORA, -1)
    out["w_ukv"] = jnp.concatenate([_unpad_heads(g["w_k_p"], QK_NOPE), _unpad_heads(g["w_v_p"], V_HEAD)],
                                   axis=2).reshape(KV_LORA, -1)
    out["w_o"] = jnp.concatenate([g["w_o_a"], g["w_o_b"].reshape(N_HEADS, HEAD_PAD, D_MODEL)[:, :V_HEAD]
                                  .reshape(N_HEADS * V_HEAD, D_MODEL)], axis=0)
    for n in ("w_up", "w_down", "w_ple_gate", "w_ple_proj"):
        out[n] = g[n]
    out["conv_w"] = g["conv_w"]
    fw = g["ffn_conv_w"]
    out["ffn_conv_w"] = jnp.concatenate([fw[0, :3], fw[1, :3]], axis=1)
    out["ffn_conv_b"] = g["ffn_conv_b"].reshape(1, 2 * D_FF)
    for n in ("norm_mix_g", "q_norm_g", "kv_norm_g", "norm_ffn_g", "ple_norm_g", "final_norm_g"):
        out[n] = g[n]
    return out


def _reduce_big(g):
    g4 = []
    for n in _BIG:
        if n in _COL_SHARDED:
            g4.append(_cols_to_shards(g[n]))
        else:
            g4.append(g[n].reshape(N_CHIPS, g[n].shape[0] // N_CHIPS, g[n].shape[1]))
    sib = _send_other_halves(g4)
    core = lax.axis_index("c").astype(jnp.int32).reshape(1)
    chip = (2 * lax.axis_index("x") + lax.axis_index("y")).astype(jnp.int32).reshape(1)
    chip_sum = [_add_own_half(a, b, core, name="add_cores_" + n) for n, a, b in zip(_BIG, g4, sib)]
    landed = _scatter_chips([t16 for _, t16 in chip_sum])
    halves = [_sum_chips(a, t32, chip, name="sum_chips_" + n) for n, a, (t32, _) in zip(_BIG, landed, chip_sum)]
    whole = _join_halves(halves)
    return {n: a.reshape(-1, a.shape[2]) for n, a in zip(_BIG, whole)}


_SMALL_SIZES = {"norm_mix_g": D_MODEL, "conv_w": 3 * CONV_WIDTH, "q_norm_g": Q_LORA, "kv_norm_g": KV_LORA,
                "norm_ffn_g": D_MODEL, "ffn_conv_w": 6 * D_FF, "ffn_conv_b": 2 * D_FF, "ple_norm_g": D_MODEL,
                "final_norm_g": D_MODEL}


def _pack(parts, rows):
    flat = jnp.concatenate([a.reshape(-1) for a in parts])
    return jnp.pad(flat, (0, rows * 128 - flat.shape[0])).reshape(rows, 128)


def _unpack(buf, sizes):
    flat = buf.reshape(-1)
    out, at = [], 0
    for n in sizes:
        out.append(flat[at:at + n])
        at += n
    return out


def _reduce_small(g):
    total = sum(_SMALL_SIZES[n] for n in _SMALL)
    rows = -(-total // 1024) * 8
    slots = _gather_all(_pack([g[n] for n in _SMALL], rows))
    summed = _sum_slots(slots, name="sum_small")
    return dict(zip(_SMALL, _unpack(summed, [_SMALL_SIZES[n] for n in _SMALL])))


def kernel(x, p, norm_mix_g, w_in, conv_w, q_norm_g, w_uq, kv_norm_g, w_ukv, w_o, norm_ffn_g, w_up, ffn_conv_w, ffn_conv_b, w_down, ple_norm_g, w_ple_gate, w_ple_proj, final_norm_g, loss_target, m_norm_mix_g, m_w_in, m_conv_w, m_q_norm_g, m_w_uq, m_kv_norm_g, m_w_ukv, m_w_o, m_norm_ffn_g, m_w_up, m_ffn_conv_w, m_ffn_conv_b, m_w_down, m_ple_norm_g, m_w_ple_gate, m_w_ple_proj, m_final_norm_g, v_norm_mix_g, v_w_in, v_conv_w, v_q_norm_g, v_w_uq, v_kv_norm_g, v_w_ukv, v_w_o, v_norm_ffn_g, v_w_up, v_ffn_conv_w, v_ffn_conv_b, v_w_down, v_ple_norm_g, v_w_ple_gate, v_w_ple_proj, v_final_norm_g):
    names = ["norm_mix_g", "w_in", "conv_w", "q_norm_g", "w_uq", "kv_norm_g", "w_ukv", "w_o", "norm_ffn_g", "w_up",
             "ffn_conv_w", "ffn_conv_b", "w_down", "ple_norm_g", "w_ple_gate", "w_ple_proj", "final_norm_g"]
    w = dict(zip(names, (norm_mix_g, w_in, conv_w, q_norm_g, w_uq, kv_norm_g, w_ukv, w_o, norm_ffn_g, w_up,
                         ffn_conv_w, ffn_conv_b, w_down, ple_norm_g, w_ple_gate, w_ple_proj, final_norm_g)))
    m = dict(zip(names, (m_norm_mix_g, m_w_in, m_conv_w, m_q_norm_g, m_w_uq, m_kv_norm_g, m_w_ukv, m_w_o,
                         m_norm_ffn_g, m_w_up, m_ffn_conv_w, m_ffn_conv_b, m_w_down, m_ple_norm_g, m_w_ple_gate,
                         m_w_ple_proj, m_final_norm_g)))
    v = dict(zip(names, (v_norm_mix_g, v_w_in, v_conv_w, v_q_norm_g, v_w_uq, v_kv_norm_g, v_w_ukv, v_w_o,
                         v_norm_ffn_g, v_w_up, v_ffn_conv_w, v_ffn_conv_b, v_w_down, v_ple_norm_g, v_w_ple_gate,
                         v_w_ple_proj, v_final_norm_g)))

    wts = _gathered_weights(w)
    loss, grad_x, padded = _local_step(x[0], p[0, 0], loss_target[0], wts)
    g_full = _true_gradients(padded)
    loss = lax.psum(loss, ("x", "y", "c"))

    g_out, d_out, m_out, v_out = {}, {}, {}, {}
    big = _reduce_big(g_full)
    for n in _BIG:
        shape = w[n].shape
        g = big[n]
        d, mn, vn = _adamw(w[n][0], g, m[n][0], v[n][0], name="adamw_" + n)
        g_out[n], d_out[n], m_out[n], v_out[n] = (a.reshape(shape) for a in (g, d, mn, vn))

    small = _reduce_small(g_full)
    chip = 2 * lax.axis_index("x") + lax.axis_index("y")
    g_small = {}
    for n in _SMALL:
        shape = w[n].shape
        g = small[n]
        if n in ("conv_w", "ffn_conv_w"):
            width = shape[-1]
            g = lax.dynamic_slice(g.reshape(3, N_CHIPS * width), (0, chip * width), (3, width))
        g_small[n] = g.reshape(shape)
    sizes = [g_small[n].size for n in _SMALL]
    rows = -(-sum(sizes) // 1024) * 8
    packed = [_pack([src[n] for n in _SMALL], rows) for src in (w, g_small, m, v)]
    d_s, m_s, v_s = _adamw(*packed, name="adamw_small")
    for n, d, mn, vn in zip(_SMALL, _unpack(d_s, sizes), _unpack(m_s, sizes), _unpack(v_s, sizes)):
        shape = w[n].shape
        g_out[n], d_out[n], m_out[n], v_out[n] = g_small[n], d.reshape(shape), mn.reshape(shape), vn.reshape(shape)

    return (loss, grad_x[None], *[g_out[n] for n in names], *[d_out[n] for n in names],
            *[m_out[n] for n in names], *[v_out[n] for n in names])
```

```python
import functools

import jax
import jax.numpy as jnp
from jax import lax
from jax.experimental import pallas as pl
from jax.experimental.pallas import tpu as pltpu

F32 = jnp.float32
BF16 = jnp.bfloat16

D_MODEL = 1024
CONV_WIDTH = 512
Q_LORA = 256
KV_LORA = 128
QK_NOPE = 64
QK_ROPE = 32
V_HEAD = 64
N_HEADS = 8
HEAD_PAD = 128
D_ATT = N_HEADS * HEAD_PAD
D_IN = 3 * CONV_WIDTH + Q_LORA + KV_LORA + QK_ROPE
D_IN_PAD = 3 * CONV_WIDTH + Q_LORA + KV_LORA + HEAD_PAD
D_FF = 2816
ROPE_THETA = 10000.0
EPS = 1e-6
SM_SCALE = (QK_NOPE + QK_ROPE) ** -0.5
ONES_LANE = V_HEAD

ADAM_LR = 0.001
ADAM_B1 = 0.9
ADAM_B2 = 0.999
ADAM_EPS = 1e-08
ADAM_WD = 0.01
ADAM_STEP = 10

N_CHIPS = 4
N_DEV = 8
MESH = pl.DeviceIdType.MESH
ANY = pl.BlockSpec(memory_space=pl.ANY)


def _params(sem):
    return pltpu.CompilerParams(dimension_semantics=sem)


def _mm(a, b, *, name, ta=False, tb=False, add=None, out_dtype=F32, tm=512, tn=512, tk=512,
        a_split=False, b_split=False, o_split=False):
    if a_split:
        _, m, kh = a.shape
        k = 2 * kh
    elif ta:
        k, m = a.shape
    else:
        m, k = a.shape
    if b_split:
        _, kb, nh = b.shape
        n = 2 * nh
    elif tb:
        n, kb = b.shape
    else:
        kb, n = b.shape
    assert kb == k, (name, a.shape, b.shape)
    tm, tn, tk = min(tm, m), min(tn, n), min(tk, k)
    assert m % tm == 0 and n % tn == 0 and k % tk == 0, (name, m, n, k, tm, tn, tk)
    gm, gn, gk = m // tm, n // tn, k // tk

    if a_split:
        assert gk % 2 == 0
        a_spec = pl.BlockSpec((None, tm, tk), lambda i, j, kk: (kk // (gk // 2), i, kk % (gk // 2)))
    elif ta:
        a_spec = pl.BlockSpec((tk, tm), lambda i, j, kk: (kk, i))
    else:
        a_spec = pl.BlockSpec((tm, tk), lambda i, j, kk: (i, kk))
    if b_split:
        assert gn % 2 == 0
        b_spec = pl.BlockSpec((None, tk, tn), lambda i, j, kk: (j // (gn // 2), kk, j % (gn // 2)))
    elif tb:
        b_spec = pl.BlockSpec((tn, tk), lambda i, j, kk: (j, kk))
    else:
        b_spec = pl.BlockSpec((tk, tn), lambda i, j, kk: (kk, j))
    if o_split:
        assert gn % 2 == 0
        o_spec = pl.BlockSpec((None, tm, tn), lambda i, j, kk: (j // (gn // 2), i, j % (gn // 2)))
        o_shape = jax.ShapeDtypeStruct((2, m, n // 2), out_dtype)
    else:
        o_spec = pl.BlockSpec((tm, tn), lambda i, j, kk: (i, j))
        o_shape = jax.ShapeDtypeStruct((m, n), out_dtype)
    dims = (((0 if ta else 1,), (1 if tb else 0,)), ((), ()))

    def body(*refs):
        if add is None:
            a_ref, b_ref, o_ref, acc_ref = refs
            add_ref = None
        else:
            a_ref, b_ref, add_ref, o_ref, acc_ref = refs
        kk = pl.program_id(2)

        @pl.when(kk == 0)
        def _():
            acc_ref[...] = jnp.zeros_like(acc_ref)

        acc_ref[...] += lax.dot_general(a_ref[...].astype(BF16), b_ref[...].astype(BF16), dims,
                                        preferred_element_type=F32)

        @pl.when(kk == gk - 1)
        def _():
            r = acc_ref[...]
            if add_ref is not None:
                r = r + add_ref[...]
            o_ref[...] = r.astype(o_ref.dtype)

    in_specs = [a_spec, b_spec]
    args = [a, b]
    if add is not None:
        in_specs.append(pl.BlockSpec((tm, tn), lambda i, j, kk: (i, j)))
        args.append(add)
    return pl.pallas_call(
        body, name=name, grid=(gm, gn, gk), in_specs=in_specs, out_specs=o_spec, out_shape=o_shape,
        scratch_shapes=[pltpu.VMEM((tm, tn), F32)],
        compiler_params=_params(("parallel", "parallel", "arbitrary")),
    )(*args)


def _rms_scale(v):
    return lax.rsqrt(jnp.mean(v * v, axis=-1, keepdims=True) + EPS)


def _rms_bwd_rows(v, g, dy):
    r = _rms_scale(v)
    vh = v * r
    dyg = dy * g
    dv = r * (dyg - vh * jnp.mean(dyg * vh, axis=-1, keepdims=True))
    return dv, dy * vh


def _shift_down(v, first_row):
    row = lax.broadcasted_iota(jnp.int32, v.shape, 0)
    return jnp.where(row == 0, first_row, pltpu.roll(v, 1, 0))


def _shift_up(v, last_row):
    n = v.shape[0]
    row = lax.broadcasted_iota(jnp.int32, v.shape, 0)
    return jnp.where(row == n - 1, last_row, pltpu.roll(v, n - 1, 0))


def _rope(t, cos, sin_a, sin_b):
    return t * cos + pltpu.roll(t, HEAD_PAD - 16, 1) * sin_a + pltpu.roll(t, 16, 1) * sin_b


def _rope_bwd(d, cos, sin_a, sin_b):
    return d * cos + pltpu.roll(d * sin_a, 16, 1) + pltpu.roll(d * sin_b, HEAD_PAD - 16, 1)


def _sigmoid(v):
    return 1.0 / (1.0 + jnp.exp(-v))


def _halo_specs(ts, s, width, col):
    nb = ts // 8
    last = s // 8 - 1
    prev = pl.BlockSpec((8, width), lambda i: (jnp.maximum(i * nb - 1, 0), col))
    nxt = pl.BlockSpec((8, width), lambda i: (jnp.minimum((i + 1) * nb, last), col))
    return prev, nxt


def _rms_fwd(x, g, *, name, ts=512):
    s, d = x.shape

    def body(x_ref, g_ref, h_ref):
        v = x_ref[...]
        h_ref[...] = (v * _rms_scale(v) * g_ref[...]).astype(h_ref.dtype)

    return pl.pallas_call(
        body, name=name, grid=(s // ts,),
        in_specs=[pl.BlockSpec((ts, d), lambda i: (i, 0)), pl.BlockSpec((1, d), lambda i: (0, 0))],
        out_specs=pl.BlockSpec((ts, d), lambda i: (i, 0)),
        out_shape=jax.ShapeDtypeStruct((s, d), BF16),
        compiler_params=_params(("parallel",)),
    )(x, g)


def _rms_bwd(x, g, dy, add, *, name, ts=256):
    s, d = x.shape

    def body(x_ref, g_ref, dy_ref, add_ref, dx_ref, dg_ref):
        @pl.when(pl.program_id(0) == 0)
        def _():
            dg_ref[...] = jnp.zeros_like(dg_ref)

        dv, dgr = _rms_bwd_rows(x_ref[...], g_ref[...], dy_ref[...])
        dx_ref[...] = dv + add_ref[...]
        dg_ref[...] += jnp.sum(dgr, axis=0, keepdims=True)

    row = pl.BlockSpec((ts, d), lambda i: (i, 0))
    vec = pl.BlockSpec((1, d), lambda i: (0, 0))
    return pl.pallas_call(
        body, name=name, grid=(s // ts,),
        in_specs=[row, vec, row, row], out_specs=[row, vec],
        out_shape=[jax.ShapeDtypeStruct((s, d), F32), jax.ShapeDtypeStruct((1, d), F32)],
        compiler_params=_params(("arbitrary",)),
    )(x, g, dy, add)


def _mix_pre(z, conv_w8, gq, gkv, cos, sin_a, sin_b, *, ts=256):
    s = z.shape[0]
    n = s // ts
    cw = CONV_WIDTH

    def body(z_ref, xcp, xcn, cgp, cgn, w_ref, gq_ref, gkv_ref, cos_ref, sa_ref, sb_ref,
             yc_ref, qn_ref, kvn_ref, kr_ref):
        i = pl.program_id(0)
        xc = z_ref[:, 0:cw]
        bg = z_ref[:, cw:2 * cw]
        cg = z_ref[:, 2 * cw:3 * cw]
        m = cg * xc
        m_prev = jnp.where(i > 0, xcp[7:8, :] * cgp[7:8, :], 0.0)
        m_next = jnp.where(i < n - 1, xcn[0:1, :] * cgn[0:1, :], 0.0)
        cm = _shift_down(m, m_prev) * w_ref[0:1, :] + m * w_ref[1:2, :] + _shift_up(m, m_next) * w_ref[2:3, :]
        yc_ref[...] = (bg * cm).astype(BF16)
        ql = z_ref[:, 3 * cw:3 * cw + Q_LORA]
        qn_ref[...] = (ql * _rms_scale(ql) * gq_ref[...]).astype(BF16)
        kvl = z_ref[:, 3 * cw + Q_LORA:3 * cw + Q_LORA + KV_LORA]
        kvn_ref[...] = (kvl * _rms_scale(kvl) * gkv_ref[...]).astype(BF16)
        kr_ref[...] = _rope(z_ref[:, D_IN_PAD - HEAD_PAD:D_IN_PAD], cos_ref[...], sa_ref[...], sb_ref[...])

    xcp, xcn = _halo_specs(ts, s, cw, 0)
    cgp, cgn = _halo_specs(ts, s, cw, 2)
    tab = pl.BlockSpec((ts, HEAD_PAD), lambda i: (i, 0))
    return pl.pallas_call(
        body, name="mix_pre", grid=(n,),
        in_specs=[pl.BlockSpec((ts, D_IN_PAD), lambda i: (i, 0)), xcp, xcn, cgp, cgn,
                  pl.BlockSpec((8, cw), lambda i: (0, 0)), pl.BlockSpec((1, Q_LORA), lambda i: (0, 0)),
                  pl.BlockSpec((1, KV_LORA), lambda i: (0, 0)), tab, tab, tab],
        out_specs=[pl.BlockSpec((ts, cw), lambda i: (i, 0)), pl.BlockSpec((ts, Q_LORA), lambda i: (i, 0)),
                   pl.BlockSpec((ts, KV_LORA), lambda i: (i, 0)), tab],
        out_shape=[jax.ShapeDtypeStruct((s, cw), BF16), jax.ShapeDtypeStruct((s, Q_LORA), BF16),
                   jax.ShapeDtypeStruct((s, KV_LORA), BF16), jax.ShapeDtypeStruct((s, HEAD_PAD), F32)],
        compiler_params=_params(("parallel",)),
    )(z, z, z, z, z, conv_w8, gq, gkv, cos, sin_a, sin_b)


def _mix_bwd(z, dyc, dqn, dkvn, dkr, conv_w8, gq, gkv, cos, sin_a, sin_b, *, ts=256):
    s = z.shape[0]
    n = s // ts
    cw = CONV_WIDTH

    def body(z_ref, xcp, xcn, bgp, bgn, cgp, cgn, dyc_ref, dycp, dycn, dqn_ref, dkvn_ref, dkr_ref,
             w_ref, gq_ref, gkv_ref, cos_ref, sa_ref, sb_ref,
             dz_ref, dw0_ref, dw1_ref, dw2_ref, dgq_ref, dgkv_ref):
        i = pl.program_id(0)

        @pl.when(i == 0)
        def _():
            for r in (dw0_ref, dw1_ref, dw2_ref, dgq_ref, dgkv_ref):
                r[...] = jnp.zeros_like(r)

        xc = z_ref[:, 0:cw]
        bg = z_ref[:, cw:2 * cw]
        cg = z_ref[:, 2 * cw:3 * cw]
        w0, w1, w2 = w_ref[0:1, :], w_ref[1:2, :], w_ref[2:3, :]
        m = cg * xc
        m_dn = _shift_down(m, jnp.where(i > 0, xcp[7:8, :] * cgp[7:8, :], 0.0))
        m_up = _shift_up(m, jnp.where(i < n - 1, xcn[0:1, :] * cgn[0:1, :], 0.0))
        cm = m_dn * w0 + m * w1 + m_up * w2
        dyc_v = dyc_ref[...]
        dcm = dyc_v * bg
        dcm_dn = _shift_down(dcm, jnp.where(i > 0, dycp[7:8, :] * bgp[7:8, :], 0.0))
        dcm_up = _shift_up(dcm, jnp.where(i < n - 1, dycn[0:1, :] * bgn[0:1, :], 0.0))
        dm = dcm_up * w0 + dcm * w1 + dcm_dn * w2
        dz_ref[:, 0:cw] = (dm * cg).astype(BF16)
        dz_ref[:, cw:2 * cw] = (dyc_v * cm).astype(BF16)
        dz_ref[:, 2 * cw:3 * cw] = (dm * xc).astype(BF16)
        dw0_ref[...] += jnp.sum(dcm * m_dn, axis=0, keepdims=True)
        dw1_ref[...] += jnp.sum(dcm * m, axis=0, keepdims=True)
        dw2_ref[...] += jnp.sum(dcm * m_up, axis=0, keepdims=True)

        dql, dgq_rows = _rms_bwd_rows(z_ref[:, 3 * cw:3 * cw + Q_LORA], gq_ref[...], dqn_ref[...])
        dz_ref[:, 3 * cw:3 * cw + Q_LORA] = dql.astype(BF16)
        dgq_ref[...] += jnp.sum(dgq_rows, axis=0, keepdims=True)
        dkvl, dgkv_rows = _rms_bwd_rows(z_ref[:, 3 * cw + Q_LORA:3 * cw + Q_LORA + KV_LORA], gkv_ref[...],
                                        dkvn_ref[...])
        dz_ref[:, 3 * cw + Q_LORA:3 * cw + Q_LORA + KV_LORA] = dkvl.astype(BF16)
        dgkv_ref[...] += jnp.sum(dgkv_rows, axis=0, keepdims=True)

        lane = lax.broadcasted_iota(jnp.int32, (ts, HEAD_PAD), 1)
        rope_lane = (lane >= QK_NOPE) & (lane < QK_NOPE + QK_ROPE)
        dk = _rope_bwd(dkr_ref[...], cos_ref[...], sa_ref[...], sb_ref[...])
        dz_ref[:, D_IN_PAD - HEAD_PAD:D_IN_PAD] = jnp.where(rope_lane, dk, 0.0).astype(BF16)

    xcp, xcn = _halo_specs(ts, s, cw, 0)
    bgp, bgn = _halo_specs(ts, s, cw, 1)
    cgp, cgn = _halo_specs(ts, s, cw, 2)
    dycp, dycn = _halo_specs(ts, s, cw, 0)
    tab = pl.BlockSpec((ts, HEAD_PAD), lambda i: (i, 0))

    def vec(width):
        return pl.BlockSpec((1, width), lambda i: (0, 0))

    outs = pl.pallas_call(
        body, name="mix_bwd", grid=(n,),
        in_specs=[pl.BlockSpec((ts, D_IN_PAD), lambda i: (i, 0)), xcp, xcn, bgp, bgn, cgp, cgn,
                  pl.BlockSpec((ts, cw), lambda i: (i, 0)), dycp, dycn,
                  pl.BlockSpec((ts, Q_LORA), lambda i: (i, 0)), pl.BlockSpec((ts, KV_LORA), lambda i: (i, 0)), tab,
                  pl.BlockSpec((8, cw), lambda i: (0, 0)), vec(Q_LORA), vec(KV_LORA), tab, tab, tab],
        out_specs=[pl.BlockSpec((ts, D_IN_PAD), lambda i: (i, 0)), vec(cw), vec(cw), vec(cw), vec(Q_LORA),
                   vec(KV_LORA)],
        out_shape=[jax.ShapeDtypeStruct((s, D_IN_PAD), BF16)] + [jax.ShapeDtypeStruct((1, cw), F32)] * 3
        + [jax.ShapeDtypeStruct((1, Q_LORA), F32), jax.ShapeDtypeStruct((1, KV_LORA), F32)],
        compiler_params=_params(("arbitrary",)),
    )(z, z, z, z, z, z, z, dyc, dyc, dyc, dqn, dkvn, dkr, conv_w8, gq, gkv, cos, sin_a, sin_b)
    dz, dw0, dw1, dw2, dgq, dgkv = outs
    return dz, jnp.concatenate([dw0, dw1, dw2], axis=0), dgq, dgkv


def _qkv_proj(qn, kvn, kr, w_uq_p, w_kv_p, cos, sin_a, sin_b, *, ts=512):
    s = qn.shape[0]

    def body(qn_ref, kvn_ref, kr_ref, wq_ref, wkv_ref, cos_ref, sa_ref, sb_ref, q_ref, k_ref, v_ref):
        cos_v, sa, sb = cos_ref[...], sa_ref[...], sb_ref[...]
        q = jnp.dot(qn_ref[...], wq_ref[...], preferred_element_type=F32)
        kv = jnp.dot(kvn_ref[...], wkv_ref[...], preferred_element_type=F32)
        kr_v = kr_ref[...]
        lane = lax.broadcasted_iota(jnp.int32, (1, HEAD_PAD), 1)
        ones_lane = (lane == ONES_LANE).astype(F32)
        for h in range(N_HEADS):
            blk = slice(h * HEAD_PAD, (h + 1) * HEAD_PAD)
            q_ref[:, blk] = (_rope(q[:, blk], cos_v, sa, sb) * SM_SCALE).astype(BF16)
            k_ref[:, blk] = (kv[:, blk] + kr_v).astype(BF16)
            v_ref[:, blk] = (kv[:, D_ATT + h * HEAD_PAD:D_ATT + (h + 1) * HEAD_PAD] + ones_lane).astype(BF16)

    tab = pl.BlockSpec((ts, HEAD_PAD), lambda i: (i, 0))
    wide = pl.BlockSpec((ts, D_ATT), lambda i: (i, 0))
    return pl.pallas_call(
        body, name="qkv_proj", grid=(s // ts,),
        in_specs=[pl.BlockSpec((ts, Q_LORA), lambda i: (i, 0)), pl.BlockSpec((ts, KV_LORA), lambda i: (i, 0)), tab,
                  pl.BlockSpec((Q_LORA, D_ATT), lambda i: (0, 0)), pl.BlockSpec((KV_LORA, 2 * D_ATT), lambda i: (0, 0)),
                  tab, tab, tab],
        out_specs=[wide, wide, wide],
        out_shape=[jax.ShapeDtypeStruct((s, D_ATT), BF16)] * 3,
        compiler_params=_params(("parallel",)),
    )(qn, kvn, kr, w_uq_p, w_kv_p, cos, sin_a, sin_b)


def _qk_bwd(dq, dk, cos, sin_a, sin_b, *, ts=256):
    s = dq.shape[0]

    def body(dq_ref, dk_ref, cos_ref, sa_ref, sb_ref, dqp_ref, dkr_ref):
        cos_v, sa, sb = cos_ref[...], sa_ref[...], sb_ref[...]
        tot = jnp.zeros((ts, HEAD_PAD), F32)
        for h in range(N_HEADS):
            blk = slice(h * HEAD_PAD, (h + 1) * HEAD_PAD)
            dqp_ref[:, blk] = _rope_bwd(dq_ref[:, blk], cos_v, sa, sb).astype(BF16)
            tot = tot + dk_ref[:, blk]
        dkr_ref[...] = tot

    tab = pl.BlockSpec((ts, HEAD_PAD), lambda i: (i, 0))
    wide = pl.BlockSpec((ts, D_ATT), lambda i: (i, 0))
    return pl.pallas_call(
        body, name="qk_bwd", grid=(s // ts,),
        in_specs=[wide, wide, tab, tab, tab], out_specs=[wide, tab],
        out_shape=[jax.ShapeDtypeStruct((s, D_ATT), BF16), jax.ShapeDtypeStruct((s, HEAD_PAD), F32)],
        compiler_params=_params(("parallel",)),
    )(dq, dk, cos, sin_a, sin_b)


_NT = (((1,), (1,)), ((), ()))
_TN = (((0,), (0,)), ((), ()))


def _flash_fwd(q, k, v, *, tq=1024, tk=1024):
    s = q.shape[0]
    tq, tk = min(tq, s), min(tk, s)
    nk = s // tk

    def body(q_ref, k_ref, v_ref, o_ref, lse_ref):
        qv = q_ref[...]

        def step(j, carry):
            m, acc = carry
            rows = pl.ds(pl.multiple_of(j * tk, tk), tk)
            sc = lax.dot_general(qv, k_ref[rows, :], _NT, preferred_element_type=F32)
            m_new = jnp.maximum(m, jnp.max(sc, axis=1, keepdims=True))
            p = jnp.exp(sc - m_new).astype(BF16)
            acc = jnp.exp(m - m_new) * acc + jnp.dot(p, v_ref[rows, :], preferred_element_type=F32)
            return m_new, acc

        init = (jnp.full((tq, 1), -jnp.inf, F32), jnp.zeros((tq, HEAD_PAD), F32))
        m, acc = lax.fori_loop(0, nk, step, init)
        l = acc[:, ONES_LANE:ONES_LANE + 1]
        o_ref[...] = (acc / l).astype(BF16)
        lse_ref[...] = m + jnp.log(l)

    head = pl.BlockSpec((s, HEAD_PAD), lambda h, i: (0, h))
    return pl.pallas_call(
        body, name="flash_fwd", grid=(N_HEADS, s // tq),
        in_specs=[pl.BlockSpec((tq, HEAD_PAD), lambda h, i: (i, h)), head, head],
        out_specs=[pl.BlockSpec((tq, HEAD_PAD), lambda h, i: (i, h)),
                   pl.BlockSpec((None, tq, 1), lambda h, i: (h, i, 0))],
        out_shape=[jax.ShapeDtypeStruct((s, D_ATT), BF16), jax.ShapeDtypeStruct((N_HEADS, s, 1), F32)],
        compiler_params=_params(("parallel", "parallel")),
    )(q, k, v)


def _attn_delta(do, o, *, ts=512):
    s = do.shape[0]

    def body(do_ref, o_ref, dl_ref):
        for h in range(N_HEADS):
            blk = slice(h * HEAD_PAD, (h + 1) * HEAD_PAD)
            dl_ref[h] = jnp.sum(do_ref[:, blk].astype(F32) * o_ref[:, blk].astype(F32), axis=1, keepdims=True)

    wide = pl.BlockSpec((ts, D_ATT), lambda i: (i, 0))
    return pl.pallas_call(
        body, name="attn_delta", grid=(s // ts,), in_specs=[wide, wide],
        out_specs=pl.BlockSpec((N_HEADS, ts, 1), lambda i: (0, i, 0)),
        out_shape=jax.ShapeDtypeStruct((N_HEADS, s, 1), F32),
        compiler_params=_params(("parallel",)),
    )(do, o)


def _flash_bwd(q, qt, k, v, do, dot, lse, delta, *, tq=1024, tk=512):
    s = q.shape[0]
    tq, tk = min(tq, s), min(tk, s)
    nq = s // tq

    def body(q_ref, qt_ref, do_ref, dot_ref, lse_ref, dl_ref, k_ref, v_ref, dq_ref, dk_ref, dv_ref):
        j = pl.program_id(1)

        @pl.when(j == 0)
        def _():
            dq_ref[...] = jnp.zeros_like(dq_ref)

        kv, vv = k_ref[...], v_ref[...]

        def step(i, carry):
            dk_t, dv_t = carry
            at = pl.multiple_of(i * tq, tq)
            rows = pl.ds(at, tq)
            sc = lax.dot_general(q_ref[rows, :], kv, _NT, preferred_element_type=F32)
            p = jnp.exp(sc - lse_ref[rows, :])
            dp = lax.dot_general(do_ref[rows, :], vv, _NT, preferred_element_type=F32)
            ds = (p * (dp - dl_ref[rows, :])).astype(BF16)
            dv_t = dv_t + jnp.dot(dot_ref[:, rows], p.astype(BF16), preferred_element_type=F32)
            dk_t = dk_t + jnp.dot(qt_ref[:, rows], ds, preferred_element_type=F32)
            dq_ref[rows, :] += jnp.dot(ds, kv, preferred_element_type=F32)
            return dk_t, dv_t

        zero = jnp.zeros((HEAD_PAD, tk), F32)
        dk_t, dv_t = lax.fori_loop(0, nq, step, (zero, zero))
        dk_ref[...] = dk_t.T
        dv_ref[...] = dv_t.T

        @pl.when(j == pl.num_programs(1) - 1)
        def _():
            dq_ref[...] *= SM_SCALE

    head = pl.BlockSpec((s, HEAD_PAD), lambda h, j: (0, h))
    head_t = pl.BlockSpec((HEAD_PAD, s), lambda h, j: (h, 0))
    stat = pl.BlockSpec((None, s, 1), lambda h, j: (h, 0, 0))
    blk = pl.BlockSpec((tk, HEAD_PAD), lambda h, j: (j, h))
    return pl.pallas_call(
        body, name="flash_bwd", grid=(N_HEADS, s // tk),
        in_specs=[head, head_t, head, head_t, stat, stat, blk, blk],
        out_specs=[head, blk, blk],
        out_shape=[jax.ShapeDtypeStruct((s, D_ATT), F32)] * 3,
        compiler_params=_params(("parallel", "arbitrary")),
    )(q, qt, do, dot, lse, delta, k, v)


FFN_TC = 256


def _ffn_specs(ts, s, tc, row_axis):
    nb = ts // 8
    last = s // 8 - 1
    if row_axis == 0:
        main = pl.BlockSpec((2, ts, tc), lambda i, j: (0, i, j))
        prev = pl.BlockSpec((2, 8, tc), lambda i, j: (0, jnp.maximum(i * nb - 1, 0), j))
        nxt = pl.BlockSpec((2, 8, tc), lambda i, j: (0, jnp.minimum((i + 1) * nb, last), j))
    else:
        main = pl.BlockSpec((2, ts, tc), lambda j, i: (0, i, j))
        prev = pl.BlockSpec((2, 8, tc), lambda j, i: (0, jnp.maximum(i * nb - 1, 0), j))
        nxt = pl.BlockSpec((2, 8, tc), lambda j, i: (0, jnp.minimum((i + 1) * nb, last), j))
    return main, prev, nxt


def _ffn_conv(a_ref, ap_ref, an_ref, w_ref, b_ref, half, first, last):
    a = a_ref[half]
    a_dn = _shift_down(a, jnp.where(first, 0.0, ap_ref[half, 7:8, :]))
    a_up = _shift_up(a, jnp.where(last, 0.0, an_ref[half, 0:1, :]))
    w = w_ref[half]
    out = a_dn * w[0:1, :] + a * w[1:2, :] + a_up * w[2:3, :] + b_ref[half]
    return out, a_dn, a, a_up


def _ffn_act(a_pre, w, b, *, ts=512, tc=FFN_TC):
    s = a_pre.shape[1]
    n = s // ts

    def body(a_ref, ap_ref, an_ref, w_ref, b_ref, o_ref):
        i = pl.program_id(0)
        g = _ffn_conv(a_ref, ap_ref, an_ref, w_ref, b_ref, 0, i == 0, i == n - 1)[0]
        u = _ffn_conv(a_ref, ap_ref, an_ref, w_ref, b_ref, 1, i == 0, i == n - 1)[0]
        o_ref[...] = (g * _sigmoid(g) * u).astype(BF16)

    main, prev, nxt = _ffn_specs(ts, s, tc, 0)
    return pl.pallas_call(
        body, name="ffn_act", grid=(n, D_FF // tc),
        in_specs=[main, prev, nxt, pl.BlockSpec((2, 8, tc), lambda i, j: (0, 0, j)),
                  pl.BlockSpec((2, 1, tc), lambda i, j: (0, 0, j))],
        out_specs=pl.BlockSpec((ts, tc), lambda i, j: (i, j)),
        out_shape=jax.ShapeDtypeStruct((s, D_FF), BF16),
        compiler_params=_params(("parallel", "parallel")),
    )(a_pre, a_pre, a_pre, w, b)


def _ffn_act_bwd(a_pre, dact, w, b, *, ts=512, tc=FFN_TC):
    s = a_pre.shape[1]
    n = s // ts

    def body(a_ref, ap_ref, an_ref, dact_ref, w_ref, b_ref, da_ref, dw_ref, db_ref):
        i = pl.program_id(1)

        @pl.when(i == 0)
        def _():
            dw_ref[...] = jnp.zeros_like(dw_ref)
            db_ref[...] = jnp.zeros_like(db_ref)

        g, g_dn, g_c, g_up = _ffn_conv(a_ref, ap_ref, an_ref, w_ref, b_ref, 0, i == 0, i == n - 1)
        u, u_dn, u_c, u_up = _ffn_conv(a_ref, ap_ref, an_ref, w_ref, b_ref, 1, i == 0, i == n - 1)
        dact_v = dact_ref[...]
        sg = _sigmoid(g)
        dg = dact_v * u * (sg * (1.0 + g * (1.0 - sg)))
        du = dact_v * (g * sg)
        da_ref[0] = dg
        da_ref[1] = du
        for half, (d, dn, c, up) in enumerate(((dg, g_dn, g_c, g_up), (du, u_dn, u_c, u_up))):
            dw_ref[half, 0:1, :] += jnp.sum(d * dn, axis=0, keepdims=True)
            dw_ref[half, 1:2, :] += jnp.sum(d * c, axis=0, keepdims=True)
            dw_ref[half, 2:3, :] += jnp.sum(d * up, axis=0, keepdims=True)
            db_ref[half] += jnp.sum(d, axis=0, keepdims=True)

    main, prev, nxt = _ffn_specs(ts, s, tc, 1)
    return pl.pallas_call(
        body, name="ffn_act_bwd", grid=(D_FF // tc, n),
        in_specs=[main, prev, nxt, pl.BlockSpec((ts, tc), lambda j, i: (i, j)),
                  pl.BlockSpec((2, 8, tc), lambda j, i: (0, 0, j)), pl.BlockSpec((2, 1, tc), lambda j, i: (0, 0, j))],
        out_specs=[main, pl.BlockSpec((2, 8, tc), lambda j, i: (0, 0, j)),
                   pl.BlockSpec((2, 1, tc), lambda j, i: (0, 0, j))],
        out_shape=[jax.ShapeDtypeStruct((2, s, D_FF), F32), jax.ShapeDtypeStruct((2, 8, D_FF), F32),
                   jax.ShapeDtypeStruct((2, 1, D_FF), F32)],
        compiler_params=_params(("parallel", "arbitrary")),
    )(a_pre, a_pre, a_pre, dact, w, b)


def _ffn_conv_t(da, w, *, ts=512, tc=FFN_TC):
    s = da.shape[1]
    n = s // ts

    def body(d_ref, dp_ref, dn_ref, w_ref, o_ref):
        i = pl.program_id(0)
        for half in range(2):
            d = d_ref[half]
            d_dn = _shift_down(d, jnp.where(i == 0, 0.0, dp_ref[half, 7:8, :]))
            d_up = _shift_up(d, jnp.where(i == n - 1, 0.0, dn_ref[half, 0:1, :]))
            wv = w_ref[half]
            o_ref[half] = (d_up * wv[0:1, :] + d * wv[1:2, :] + d_dn * wv[2:3, :]).astype(BF16)

    main, prev, nxt = _ffn_specs(ts, s, tc, 0)
    return pl.pallas_call(
        body, name="ffn_conv_t", grid=(n, D_FF // tc),
        in_specs=[main, prev, nxt, pl.BlockSpec((2, 8, tc), lambda i, j: (0, 0, j))],
        out_specs=main, out_shape=jax.ShapeDtypeStruct((2, s, D_FF), BF16),
        compiler_params=_params(("parallel", "parallel")),
    )(da, da, da, w)


def _ple_final(x2, gl, pp, target, gf, *, ts=256):
    s, d = x2.shape

    def body(x2_ref, gl_ref, pp_ref, t_ref, gf_ref, loss_ref, dx3_ref, dgl_ref, dpp_ref, dgf_ref):
        @pl.when(pl.program_id(0) == 0)
        def _():
            loss_ref[...] = jnp.zeros_like(loss_ref)
            dgf_ref[...] = jnp.zeros_like(dgf_ref)

        gate = _sigmoid(gl_ref[...])
        ppv = pp_ref[...]
        x3 = x2_ref[...] + gate * ppv
        gfv = gf_ref[...]
        err = x3 * _rms_scale(x3) * gfv - t_ref[...]
        loss_ref[...] += 0.5 * jnp.sum(jnp.mean(err * err, axis=-1, keepdims=True), axis=0, keepdims=True)
        dx3, dgf_rows = _rms_bwd_rows(x3, gfv, err * (1.0 / d))
        dgf_ref[...] += jnp.sum(dgf_rows, axis=0, keepdims=True)
        dx3_ref[...] = dx3
        dgl_ref[...] = (dx3 * ppv * gate * (1.0 - gate)).astype(BF16)
        dpp_ref[...] = (dx3 * gate).astype(BF16)

    row = pl.BlockSpec((ts, d), lambda i: (i, 0))
    vec = pl.BlockSpec((1, d), lambda i: (0, 0))
    return pl.pallas_call(
        body, name="ple_final", grid=(s // ts,),
        in_specs=[row, row, row, row, vec],
        out_specs=[pl.BlockSpec((1, 128), lambda i: (0, 0)), row, row, row, vec],
        out_shape=[jax.ShapeDtypeStruct((1, 128), F32), jax.ShapeDtypeStruct((s, d), F32),
                   jax.ShapeDtypeStruct((s, d), BF16), jax.ShapeDtypeStruct((s, d), BF16),
                   jax.ShapeDtypeStruct((1, d), F32)],
        compiler_params=_params(("arbitrary",)),
    )(x2, gl, pp, target, gf)


def _row_tile(rows, cols, n_arrays, budget=12 << 20):
    best = None
    for t in range(8, rows + 1, 8):
        if rows % t == 0 and t * cols * 4 * n_arrays <= budget:
            best = t
    return rows if best is None else best


def _sum_slots(a, *, name):
    g, r, c = a.shape
    tr = _row_tile(r, c, g + 1)

    def body(*refs):
        tot = refs[0][...]
        for ref in refs[1:g]:
            tot = tot + ref[...]
        refs[g][...] = tot

    specs = [pl.BlockSpec((None, tr, c), functools.partial(lambda i, slot: (slot, i, 0), slot=k)) for k in range(g)]
    return pl.pallas_call(
        body, name=name, grid=(r // tr,), in_specs=specs, out_specs=pl.BlockSpec((tr, c), lambda i: (i, 0)),
        out_shape=jax.ShapeDtypeStruct((r, c), a.dtype), compiler_params=_params(("parallel",)),
    )(*([a] * g))


def _adamw(w, g, m, v, *, name):
    r, c = w.shape
    tr = _row_tile(r, c, 7)

    def body(w_ref, g_ref, m_ref, v_ref, d_ref, mo_ref, vo_ref):
        gv = g_ref[...]
        mn = ADAM_B1 * m_ref[...] + (1.0 - ADAM_B1) * gv
        vn = ADAM_B2 * v_ref[...] + (1.0 - ADAM_B2) * (gv * gv)
        m_hat = mn / (1.0 - ADAM_B1 ** ADAM_STEP)
        v_hat = vn / (1.0 - ADAM_B2 ** ADAM_STEP)
        d_ref[...] = -ADAM_LR * (m_hat / (jnp.sqrt(v_hat) + ADAM_EPS) + ADAM_WD * w_ref[...])
        mo_ref[...] = mn
        vo_ref[...] = vn

    blk = pl.BlockSpec((tr, c), lambda i: (i, 0))
    return pl.pallas_call(
        body, name=name, grid=(r // tr,), in_specs=[blk] * 4, out_specs=[blk] * 3,
        out_shape=[jax.ShapeDtypeStruct((r, c), F32)] * 3, compiler_params=_params(("parallel",)),
    )(w, g, m, v)


def _position():
    x, y, c = lax.axis_index("x"), lax.axis_index("y"), lax.axis_index("c")
    return x, y, c


def _other_chips(x, y):
    return [(1 - x, y), (x, 1 - y), (1 - x, 1 - y)]


def _stage_in(srcs, stage, sems):
    cps = [pltpu.make_async_copy(src, stage[a], sems.at[a]) for a, src in enumerate(srcs)]
    for cp in cps:
        cp.start()
    return cps


def _stage_out(staged, stage, dsts, sems):
    cps = []
    for a, dst in enumerate(dsts):
        staged[a].wait()
        cp = pltpu.make_async_copy(stage[a], dst, sems.at[a])
        cp.start()
        cps.append(cp)
    return cps


def _gather_chips(shards):
    n = len(shards)

    def body(*refs):
        ins, outs, stage = refs[:n], refs[n:2 * n], refs[2 * n:3 * n]
        send_sems, recv_sems, in_sems, out_sems = refs[3 * n:]
        x, y, c = _position()
        me = 2 * x + y
        chips = _other_chips(x, y)
        remote = []
        staged = _stage_in(ins, stage, in_sems)
        for a in range(n):
            for k, (px, py) in enumerate(chips):
                rc = pltpu.make_async_remote_copy(
                    src_ref=ins[a], dst_ref=outs[a].at[me], send_sem=send_sems.at[3 * a + k],
                    recv_sem=recv_sems.at[3 * a + k], device_id=(px, py, c), device_id_type=MESH)
                rc.start()
                remote.append(rc)
        local = _stage_out(staged, stage, [o.at[me] for o in outs], out_sems)
        for a in range(n):
            for k, (px, py) in enumerate(chips):
                pltpu.make_async_remote_copy(
                    src_ref=ins[a], dst_ref=outs[a].at[2 * px + py], send_sem=send_sems.at[3 * a + k],
                    recv_sem=recv_sems.at[3 * a + k], device_id=(px, py, c), device_id_type=MESH).wait_recv()
        for rc in remote:
            rc.wait_send()
        for cp in local:
            cp.wait()

    return pl.pallas_call(
        body, name="gather_chips", in_specs=[ANY] * n, out_specs=[ANY] * n,
        out_shape=[jax.ShapeDtypeStruct((N_CHIPS,) + s.shape, s.dtype) for s in shards],
        scratch_shapes=[pltpu.VMEM(s.shape, s.dtype) for s in shards]
        + [pltpu.SemaphoreType.DMA((3 * n,)), pltpu.SemaphoreType.DMA((3 * n,)),
           pltpu.SemaphoreType.DMA((n,)), pltpu.SemaphoreType.DMA((n,))],
        compiler_params=pltpu.CompilerParams(has_side_effects=True),
    )(*shards)


def _send_other_halves(grads, *, tag):
    n = len(grads)

    def body(*refs):
        ins, sib = refs[:n], refs[n:2 * n]
        send_sems, recv_sems = refs[2 * n:]
        x, y, c = _position()
        remote = []
        for a in range(n):
            half = ins[a].shape[1] // 2
            give = ins[a].at[:, pl.ds(pl.multiple_of((1 - c) * half, 8), half), :]
            rc = pltpu.make_async_remote_copy(
                src_ref=give, dst_ref=sib[a], send_sem=send_sems.at[a], recv_sem=recv_sems.at[a],
                device_id=(x, y, 1 - c), device_id_type=MESH)
            rc.start()
            remote.append(rc)
        for rc in remote:
            rc.wait_recv()
        for rc in remote:
            rc.wait_send()

    return pl.pallas_call(
        body, name="send_other_halves_" + tag, in_specs=[ANY] * n, out_specs=[ANY] * n,
        out_shape=[jax.ShapeDtypeStruct((g.shape[0], g.shape[1] // 2, g.shape[2]), g.dtype) for g in grads],
        scratch_shapes=[pltpu.SemaphoreType.DMA((n,)), pltpu.SemaphoreType.DMA((n,))],
        compiler_params=pltpu.CompilerParams(has_side_effects=True),
    )(*grads)


def _add_own_half(g4, sib, core, *, name):
    g, a2, c = sib.shape
    tr = _row_tile(a2, c, 4)

    def body(core_ref, a_ref, b_ref, o_ref, o16_ref):
        tot = a_ref[...] + b_ref[...]
        o_ref[...] = tot
        o16_ref[...] = tot.astype(BF16)

    blk = pl.BlockSpec((None, tr, c), lambda i, j, core_ref: (i, j, 0))
    return pl.pallas_call(
        body, name=name,
        grid_spec=pltpu.PrefetchScalarGridSpec(
            num_scalar_prefetch=1, grid=(g, a2 // tr),
            in_specs=[pl.BlockSpec((None, None, tr, c), lambda i, j, core_ref: (i, core_ref[0], j, 0)), blk],
            out_specs=[blk, blk]),
        out_shape=[jax.ShapeDtypeStruct(sib.shape, F32), jax.ShapeDtypeStruct(sib.shape, BF16)],
        compiler_params=_params(("parallel", "parallel")),
    )(core, g4.reshape(g, 2, a2, c), sib)


def _scatter_chips(parts):
    n = len(parts)

    def body(*refs):
        ins, outs = refs[:n], refs[n:2 * n]
        send_sems, recv_sems = refs[2 * n:]
        x, y, c = _position()
        me = 2 * x + y
        chips = _other_chips(x, y)
        remote = []
        for a in range(n):
            for k, (px, py) in enumerate(chips):
                rc = pltpu.make_async_remote_copy(
                    src_ref=ins[a].at[2 * px + py], dst_ref=outs[a].at[me], send_sem=send_sems.at[3 * a + k],
                    recv_sem=recv_sems.at[3 * a + k], device_id=(px, py, c), device_id_type=MESH)
                rc.start()
                remote.append(rc)
        for a in range(n):
            for k, (px, py) in enumerate(chips):
                pltpu.make_async_remote_copy(
                    src_ref=ins[a].at[me], dst_ref=outs[a].at[2 * px + py], send_sem=send_sems.at[3 * a + k],
                    recv_sem=recv_sems.at[3 * a + k], device_id=(px, py, c), device_id_type=MESH).wait_recv()
        for rc in remote:
            rc.wait_send()

    return pl.pallas_call(
        body, name="scatter_chips", in_specs=[ANY] * n, out_specs=[ANY] * n,
        out_shape=[jax.ShapeDtypeStruct(p.shape, p.dtype) for p in parts],
        scratch_shapes=[pltpu.SemaphoreType.DMA((3 * n,)), pltpu.SemaphoreType.DMA((3 * n,))],
        compiler_params=pltpu.CompilerParams(has_side_effects=True),
    )(*parts)


def _sum_chips(landed, own, chip, *, name):
    g, r, c = landed.shape
    tr = _row_tile(r, c, 5)

    def body(chip_ref, *refs):
        me = chip_ref[0]
        own_v = refs[g][...]
        tot = None
        for slot in range(g):
            term = jnp.where(me == slot, own_v, refs[slot][...].astype(F32))
            tot = term if tot is None else tot + term
        refs[g + 1][...] = tot

    def landed_spec(slot):
        return pl.BlockSpec((None, tr, c),
                            lambda i, chip_ref: (jnp.where(chip_ref[0] == slot, (slot + 1) % g, slot), i, 0))

    return pl.pallas_call(
        body, name=name,
        grid_spec=pltpu.PrefetchScalarGridSpec(
            num_scalar_prefetch=1, grid=(r // tr,),
            in_specs=[landed_spec(k) for k in range(g)]
            + [pl.BlockSpec((None, tr, c), lambda i, chip_ref: (chip_ref[0], i, 0))],
            out_specs=pl.BlockSpec((tr, c), lambda i, chip_ref: (i, 0))),
        out_shape=jax.ShapeDtypeStruct((r, c), F32), compiler_params=_params(("parallel",)),
    )(chip, *([landed] * g), own)


def _join_halves(halves):
    n = len(halves)

    def body(*refs):
        ins, outs, stage = refs[:n], refs[n:2 * n], refs[2 * n:3 * n]
        send_sems, recv_sems, in_sems, out_sems = refs[3 * n:]
        x, y, c = _position()
        remote = []
        staged = _stage_in(ins, stage, in_sems)
        for a in range(n):
            rc = pltpu.make_async_remote_copy(
                src_ref=ins[a], dst_ref=outs[a].at[c], send_sem=send_sems.at[a], recv_sem=recv_sems.at[a],
                device_id=(x, y, 1 - c), device_id_type=MESH)
            rc.start()
            remote.append(rc)
        local = _stage_out(staged, stage, [o.at[c] for o in outs], out_sems)
        for a in range(n):
            pltpu.make_async_remote_copy(
                src_ref=ins[a], dst_ref=outs[a].at[1 - c], send_sem=send_sems.at[a], recv_sem=recv_sems.at[a],
                device_id=(x, y, 1 - c), device_id_type=MESH).wait_recv()
        for rc in remote:
            rc.wait_send()
        for cp in local:
            cp.wait()

    return pl.pallas_call(
        body, name="join_halves", in_specs=[ANY] * n, out_specs=[ANY] * n,
        out_shape=[jax.ShapeDtypeStruct((2,) + h.shape, h.dtype) for h in halves],
        scratch_shapes=[pltpu.VMEM(h.shape, h.dtype) for h in halves]
        + [pltpu.SemaphoreType.DMA((n,)), pltpu.SemaphoreType.DMA((n,)), pltpu.SemaphoreType.DMA((n,)),
           pltpu.SemaphoreType.DMA((n,))],
        compiler_params=pltpu.CompilerParams(has_side_effects=True),
    )(*halves)


_HBM = pl.BlockSpec(memory_space=pltpu.HBM)
_SEM = pl.BlockSpec(memory_space=pltpu.SEMAPHORE)


def _chip_copies(srcs, lands, send_sems, recv_sems, scatter):
    x, y, c = _position()
    me = 2 * x + y
    outgoing, incoming = [], []
    for a, (src, land) in enumerate(zip(srcs, lands)):
        for k, (px, py) in enumerate(_other_chips(x, y)):
            peer = 2 * px + py
            sems = dict(send_sem=send_sems.at[3 * a + k], recv_sem=recv_sems.at[3 * a + k], device_id=(px, py, c),
                        device_id_type=MESH)
            outgoing.append(pltpu.make_async_remote_copy(
                src_ref=src.at[peer] if scatter else src, dst_ref=land.at[me], **sems))
            incoming.append(pltpu.make_async_remote_copy(
                src_ref=src.at[me] if scatter else src, dst_ref=land.at[peer], **sems))
    return outgoing, incoming


def _chips_start(srcs, *, scatter, name):
    n = len(srcs)
    lands = [lax.empty(a.shape if scatter else (N_CHIPS,) + a.shape, a.dtype) for a in srcs]

    def body(*refs):
        ins, send_sems, recv_sems, token = refs[:2 * n], refs[2 * n], refs[2 * n + 1], refs[-1]
        outgoing, _ = _chip_copies(ins[:n], ins[n:], send_sems, recv_sems, scatter)
        for cp in outgoing:
            cp.start()
        token[...] = jnp.zeros_like(token)

    bufs = list(srcs) + lands
    res = pl.pallas_call(
        body, name=name, in_specs=[_HBM] * (2 * n),
        out_specs=(_SEM, _SEM, *[_HBM] * (2 * n), pl.BlockSpec(memory_space=pltpu.VMEM)),
        out_shape=(pltpu.SemaphoreType.DMA((3 * n,)), pltpu.SemaphoreType.DMA((3 * n,)),
                   *[pltpu.HBM(a.shape, a.dtype) for a in bufs], jax.ShapeDtypeStruct((8, 128), F32)),
        input_output_aliases={i: 2 + i for i in range(2 * n)},
        compiler_params=pltpu.CompilerParams(has_side_effects=pltpu.SideEffectType.DATAFLOW_SIDE_EFFECTING),
    )(*[pltpu.with_memory_space_constraint(a, pltpu.HBM) for a in bufs])
    return res[0], res[1], list(res[2:2 + n]), list(res[2 + n:2 + 2 * n]), res[-1]


def _chips_wait(handle, after, *, scatter, name):
    send_sems, recv_sems, srcs, lands, _ = handle
    n = len(srcs)

    def body(*refs):
        ins, send_ref, recv_ref = refs[:2 * n], refs[2 * n], refs[2 * n + 1]
        outgoing, incoming = _chip_copies(ins[:n], ins[n:], send_ref, recv_ref, scatter)
        for cp in outgoing:
            cp.wait_send()
        for cp in incoming:
            cp.wait_recv()

    bufs = list(srcs) + list(lands)
    res = pl.pallas_call(
        body, name=name, in_specs=[_HBM] * (2 * n) + [_SEM, _SEM, ANY], out_specs=tuple([_HBM] * (2 * n)),
        out_shape=tuple(pltpu.HBM(a.shape, a.dtype) for a in bufs),
        input_output_aliases={i: i for i in range(2 * n)},
        compiler_params=pltpu.CompilerParams(has_side_effects=pltpu.SideEffectType.DATAFLOW_SIDE_EFFECTING),
    )(*bufs, send_sems, recv_sems, after)
    return list(res[:n]), list(res[n:])


def _gather_all(buf):
    def body(in_ref, out_ref, send_sems, recv_sems, local_sem):
        x, y, c = _position()
        me = 4 * x + 2 * y + c
        peers = [(x, y, 1 - c)] + [(px, py, pc) for (px, py) in _other_chips(x, y) for pc in (c, 1 - c)]
        cp = pltpu.make_async_copy(in_ref, out_ref.at[me], local_sem)
        cp.start()
        remote = []
        for k, peer in enumerate(peers):
            rc = pltpu.make_async_remote_copy(
                src_ref=in_ref, dst_ref=out_ref.at[me], send_sem=send_sems.at[k], recv_sem=recv_sems.at[k],
                device_id=peer, device_id_type=MESH)
            rc.start()
            remote.append(rc)
        for k, (px, py, pc) in enumerate(peers):
            pltpu.make_async_remote_copy(
                src_ref=in_ref, dst_ref=out_ref.at[4 * px + 2 * py + pc], send_sem=send_sems.at[k],
                recv_sem=recv_sems.at[k], device_id=(px, py, pc), device_id_type=MESH).wait_recv()
        for rc in remote:
            rc.wait_send()
        cp.wait()

    return pl.pallas_call(
        body, name="gather_all", in_specs=[ANY], out_specs=ANY,
        out_shape=jax.ShapeDtypeStruct((N_DEV,) + buf.shape, buf.dtype),
        scratch_shapes=[pltpu.SemaphoreType.DMA((N_DEV - 1,)), pltpu.SemaphoreType.DMA((N_DEV - 1,)),
                        pltpu.SemaphoreType.DMA],
        compiler_params=pltpu.CompilerParams(has_side_effects=True),
    )(buf)


def _cols_from_shards(g4):
    _, k, n = g4.shape
    return g4.transpose(1, 0, 2).reshape(k, N_CHIPS * n)


def _cols_to_shards(w):
    k, n = w.shape
    return w.reshape(k, N_CHIPS, n // N_CHIPS).transpose(1, 0, 2)


def _pad_heads(w, width):
    k = w.shape[0]
    w3 = w.reshape(k, N_HEADS, width)
    return jnp.pad(w3, ((0, 0), (0, 0), (0, HEAD_PAD - width))).reshape(k, D_ATT)


def _unpad_heads(w, width):
    k = w.shape[0]
    return w.reshape(k, N_HEADS, HEAD_PAD)[:, :, :width]


def _rope_tables(s):
    pos = jnp.arange(s, dtype=F32)
    inv_freq = ROPE_THETA ** (-jnp.arange(0, QK_ROPE, 2, dtype=F32) / QK_ROPE)
    ang = pos[:, None] * inv_freq[None, :]
    cos_h, sin_h = jnp.cos(ang), jnp.sin(ang)
    half = QK_ROPE // 2
    z = jnp.zeros((s, half), F32)
    ones = jnp.ones((s, QK_NOPE), F32)
    tail = jnp.zeros((s, HEAD_PAD - QK_NOPE - QK_ROPE), F32)
    cos = jnp.concatenate([ones, cos_h, cos_h, tail + 1.0], axis=1)
    sin_a = jnp.concatenate([ones * 0.0, -sin_h, z, tail], axis=1)
    sin_b = jnp.concatenate([ones * 0.0, z, sin_h, tail], axis=1)
    return cos, sin_a, sin_b


def _local_step(x, p, target, wts, late_weights, reduce_early):
    s = x.shape[0]
    cos, sin_a, sin_b = _rope_tables(s)
    g1, gq, gkv, g2, g3, gf = (wts[k] for k in ("norm_mix_g", "q_norm_g", "kv_norm_g", "norm_ffn_g", "ple_norm_g",
                                                 "final_norm_g"))
    w_in_p, w_uq_p, w_kv_p = wts["w_in_p"], wts["w_uq_p"], wts["w_kv_p"]
    conv_w8, fconv_w, fconv_b = wts["conv_w8"], wts["ffn_conv_w"], wts["ffn_conv_b"]

    h = _rms_fwd(x, g1, name="rms_mix")
    z = _mm(h, w_in_p, name="mm_in", tm=512, tn=1024, tk=1024)
    y_conv, qn, kvn, kr = _mix_pre(z, conv_w8, gq, gkv, cos, sin_a, sin_b)
    q, k, v = _qkv_proj(qn, kvn, kr, w_uq_p, w_kv_p, cos, sin_a, sin_b)
    o, lse = _flash_fwd(q, k, v)
    late = late_weights(lse)
    w_o_a, w_o_b, w_up, w_down = late["w_o_a"], late["w_o_b"], late["w_up"], late["w_down"]
    w_pg, w_pp = late["w_ple_gate"], late["w_ple_proj"]
    t = _mm(y_conv, w_o_a, add=x, name="mm_o_conv", tm=512, tn=1024, tk=512)
    x1 = _mm(o, w_o_b, add=t, name="mm_o_att", tm=512, tn=1024, tk=1024)
    hf = _rms_fwd(x1, g2, name="rms_ffn")
    a_pre = _mm(hf, w_up, o_split=True, name="mm_up", tm=1024, tn=1408, tk=1024)
    act = _ffn_act(a_pre, fconv_w, fconv_b)
    x2 = _mm(act, w_down, add=x1, name="mm_down", tm=512, tn=1024, tk=1408)
    n3 = _rms_fwd(x2, g3, name="rms_ple")
    gl = _mm(n3, w_pg, name="mm_gate", tm=512, tn=1024, tk=1024)
    pp = _mm(p, w_pp, name="mm_ple", tm=512, tn=1024, tk=256)
    loss, dx3, dgl, dpp, d_gf = _ple_final(x2, gl, pp, target, gf)

    grads, early = {"final_norm_g": d_gf}, {}
    early["w_ple_proj"] = _mm(p, dpp, ta=True, name="mm_d_wpp", tm=256, tn=1024, tk=1024)
    early["w_ple_gate"] = _mm(n3, dgl, ta=True, name="mm_d_wpg", tm=512, tn=1024, tk=1024)
    dn3 = _mm(dgl, w_pg, tb=True, name="mm_d_n3", tm=512, tn=1024, tk=1024)
    dx2, grads["ple_norm_g"] = _rms_bwd(x2, g3, dn3, dx3, name="rms_ple_bwd")
    early["w_down"] = _mm(act, dx2, ta=True, name="mm_d_wdown", tm=1408, tn=512, tk=1024)
    dact = _mm(dx2, w_down, tb=True, name="mm_d_act", tm=512, tn=1408, tk=1024)
    da, d_fconv_w, d_fconv_b = _ffn_act_bwd(a_pre, dact, fconv_w, fconv_b)
    grads["ffn_conv_w"], grads["ffn_conv_b"] = d_fconv_w, d_fconv_b
    da_pre = _ffn_conv_t(da, fconv_w)
    early["w_up"] = _mm(hf, da_pre, ta=True, b_split=True, name="mm_d_wup", tm=512, tn=1408, tk=1024)
    dhf = _mm(da_pre, w_up, tb=True, a_split=True, name="mm_d_hf", tm=512, tn=1024, tk=1408)
    dx1, grads["norm_ffn_g"] = _rms_bwd(x1, g2, dhf, dx2, name="rms_ffn_bwd")
    d_wo_a = _mm(y_conv, dx1, ta=True, name="mm_d_wo_conv", tm=512, tn=1024, tk=1024)
    d_wo_b = _mm(o, dx1, ta=True, name="mm_d_wo_att", tm=512, tn=1024, tk=1024)
    early["w_o"] = jnp.concatenate([d_wo_a, d_wo_b.reshape(N_HEADS, HEAD_PAD, D_MODEL)[:, :V_HEAD]
                                    .reshape(N_HEADS * V_HEAD, D_MODEL)], axis=0)
    zero, finish = reduce_early(early)
    dyc = _mm(dx1, w_o_a, tb=True, name="mm_d_yconv", tm=512, tn=512, tk=1024)
    do = _mm(dx1, w_o_b, tb=True, out_dtype=BF16, name="mm_d_o", tm=512, tn=1024, tk=1024)
    delta = _attn_delta(do, o) + zero
    dq, dk, dv = _flash_bwd(q, q.T, k, v, do, do.T, lse, delta)
    reduced_early = finish(dq)
    dq_pre, dkr = _qk_bwd(dq, dk, cos, sin_a, sin_b)
    grads["w_uq_p"] = _mm(qn, dq_pre, ta=True, name="mm_d_wuq", tm=256, tn=1024, tk=1024)
    dqn = _mm(dq_pre, w_uq_p, tb=True, name="mm_d_qn", tm=512, tn=256, tk=1024)
    grads["w_k_p"] = _mm(kvn, dk, ta=True, name="mm_d_wk", tm=128, tn=1024, tk=1024)
    grads["w_v_p"] = _mm(kvn, dv, ta=True, name="mm_d_wv", tm=128, tn=1024, tk=1024)
    dkvn_k = _mm(dk, w_kv_p[:, :D_ATT], tb=True, name="mm_d_kvn_k", tm=512, tn=128, tk=1024)
    dkvn = _mm(dv, w_kv_p[:, D_ATT:], tb=True, add=dkvn_k, name="mm_d_kvn_v", tm=512, tn=128, tk=1024)
    dz, grads["conv_w"], grads["q_norm_g"], grads["kv_norm_g"] = _mix_bwd(
        z, dyc, dqn, dkvn, dkr, conv_w8, gq, gkv, cos, sin_a, sin_b)
    grads["w_in_p"] = _mm(h, dz, ta=True, name="mm_d_win", tm=512, tn=1024, tk=1024)
    dh = _mm(dz, w_in_p, tb=True, name="mm_d_h", tm=512, tn=1024, tk=1024)
    grad_x, grads["norm_mix_g"] = _rms_bwd(x, g1, dh, dx1, name="rms_mix_bwd")
    return loss[0, 0], grad_x, grads, reduced_early


_EARLY_W = ("w_in", "w_uq", "w_ukv")
_LATE_W = ("w_o", "w_up", "w_down", "w_ple_gate", "w_ple_proj")
_BIG = _EARLY_W + _LATE_W
_COL_SHARDED = ("w_in", "w_uq", "w_ukv", "w_up", "w_ple_proj")
_SMALL = ("norm_mix_g", "conv_w", "q_norm_g", "kv_norm_g", "norm_ffn_g", "ffn_conv_w", "ffn_conv_b", "ple_norm_g",
          "final_norm_g")


def _full_from_slots(n, g4):
    return _cols_from_shards(g4) if n in _COL_SHARDED else g4.reshape(-1, g4.shape[2])


def _shard_major(n, g):
    return _cols_to_shards(g) if n in _COL_SHARDED else g.reshape(N_CHIPS, g.shape[0] // N_CHIPS, g.shape[1])


def _early_weights(w):
    shards = [w[n][0].astype(BF16) for n in _EARLY_W]
    shards.append(jnp.pad(w["conv_w"][0], ((0, 5), (0, 0))))
    shards.append(jnp.pad(w["ffn_conv_w"][0], ((0, 5), (0, 0))))
    got = _gather_chips(shards)
    full = {n: _full_from_slots(n, g4) for n, g4 in zip(_EARLY_W, got)}
    full["conv_w8"] = _cols_from_shards(got[len(_EARLY_W)])
    full["ffn_conv_w8"] = _cols_from_shards(got[len(_EARLY_W) + 1])
    return _layout_early(full, w)


def _layout_early(full, w):
    out = {n: w[n] for n in ("norm_mix_g", "q_norm_g", "kv_norm_g", "norm_ffn_g", "ple_norm_g")}
    out["final_norm_g"] = w["final_norm_g"][None, :]
    w_in = full["w_in"]
    zc = jnp.zeros((D_MODEL, QK_NOPE), BF16)
    zt = jnp.zeros((D_MODEL, HEAD_PAD - QK_NOPE - QK_ROPE), BF16)
    out["w_in_p"] = jnp.concatenate([w_in[:, :D_IN - QK_ROPE], zc, w_in[:, D_IN - QK_ROPE:], zt], axis=1)
    out["w_uq_p"] = _pad_heads(full["w_uq"], QK_NOPE + QK_ROPE)
    kv3 = full["w_ukv"].reshape(KV_LORA, N_HEADS, QK_NOPE + V_HEAD)
    out["w_kv_p"] = jnp.concatenate([_pad_heads(kv3[:, :, :QK_NOPE].reshape(KV_LORA, -1), QK_NOPE),
                                     _pad_heads(kv3[:, :, QK_NOPE:].reshape(KV_LORA, -1), V_HEAD)], axis=1)
    out["conv_w8"] = full["conv_w8"]
    fw = full["ffn_conv_w8"]
    out["ffn_conv_w"] = jnp.stack([fw[:, :D_FF], fw[:, D_FF:]])
    out["ffn_conv_b"] = w["ffn_conv_b"].reshape(2, 1, D_FF)
    return out


def _layout_late(full):
    w_o = full["w_o"]
    out = {"w_o_a": w_o[:CONV_WIDTH]}
    out["w_o_b"] = jnp.pad(w_o[CONV_WIDTH:].reshape(N_HEADS, V_HEAD, D_MODEL),
                           ((0, 0), (0, HEAD_PAD - V_HEAD), (0, 0))).reshape(D_ATT, D_MODEL)
    for n in ("w_up", "w_down", "w_ple_gate", "w_ple_proj"):
        out[n] = full[n]
    return out


def _true_gradients(g):
    out = {}
    wp = g["w_in_p"]
    out["w_in"] = jnp.concatenate([wp[:, :D_IN - QK_ROPE], wp[:, D_IN_PAD - HEAD_PAD + QK_NOPE:
                                                              D_IN_PAD - HEAD_PAD + QK_NOPE + QK_ROPE]], axis=1)
    out["w_uq"] = _unpad_heads(g["w_uq_p"], QK_NOPE + QK_ROPE).reshape(Q_LORA, -1)
    out["w_ukv"] = jnp.concatenate([_unpad_heads(g["w_k_p"], QK_NOPE), _unpad_heads(g["w_v_p"], V_HEAD)],
                                   axis=2).reshape(KV_LORA, -1)
    out["conv_w"] = g["conv_w"]
    fw = g["ffn_conv_w"]
    out["ffn_conv_w"] = jnp.concatenate([fw[0, :3], fw[1, :3]], axis=1)
    out["ffn_conv_b"] = g["ffn_conv_b"].reshape(1, 2 * D_FF)
    for n in ("norm_mix_g", "q_norm_g", "kv_norm_g", "norm_ffn_g", "ple_norm_g", "final_norm_g"):
        out[n] = g[n]
    return out


def _chip_partials(names, g, core, *, tag):
    g4 = [_shard_major(n, g[n]) for n in names]
    sib = _send_other_halves(g4, tag=tag)
    return [_add_own_half(a, b, core, name="add_cores_" + n) for n, a, b in zip(names, g4, sib)]


_SMALL_SIZES = {"norm_mix_g": D_MODEL, "conv_w": 3 * CONV_WIDTH, "q_norm_g": Q_LORA, "kv_norm_g": KV_LORA,
                "norm_ffn_g": D_MODEL, "ffn_conv_w": 6 * D_FF, "ffn_conv_b": 2 * D_FF, "ple_norm_g": D_MODEL,
                "final_norm_g": D_MODEL}


def _pack(parts, rows):
    flat = jnp.concatenate([a.reshape(-1) for a in parts])
    return jnp.pad(flat, (0, rows * 128 - flat.shape[0])).reshape(rows, 128)


def _unpack(buf, sizes):
    flat = buf.reshape(-1)
    out, at = [], 0
    for n in sizes:
        out.append(flat[at:at + n])
        at += n
    return out


def _reduce_small(g):
    total = sum(_SMALL_SIZES[n] for n in _SMALL)
    rows = -(-total // 1024) * 8
    slots = _gather_all(_pack([g[n] for n in _SMALL], rows))
    summed = _sum_slots(slots, name="sum_small")
    return dict(zip(_SMALL, _unpack(summed, [_SMALL_SIZES[n] for n in _SMALL])))


def kernel(x, p, norm_mix_g, w_in, conv_w, q_norm_g, w_uq, kv_norm_g, w_ukv, w_o, norm_ffn_g, w_up, ffn_conv_w, ffn_conv_b, w_down, ple_norm_g, w_ple_gate, w_ple_proj, final_norm_g, loss_target, m_norm_mix_g, m_w_in, m_conv_w, m_q_norm_g, m_w_uq, m_kv_norm_g, m_w_ukv, m_w_o, m_norm_ffn_g, m_w_up, m_ffn_conv_w, m_ffn_conv_b, m_w_down, m_ple_norm_g, m_w_ple_gate, m_w_ple_proj, m_final_norm_g, v_norm_mix_g, v_w_in, v_conv_w, v_q_norm_g, v_w_uq, v_kv_norm_g, v_w_ukv, v_w_o, v_norm_ffn_g, v_w_up, v_ffn_conv_w, v_ffn_conv_b, v_w_down, v_ple_norm_g, v_w_ple_gate, v_w_ple_proj, v_final_norm_g):
    names = ["norm_mix_g", "w_in", "conv_w", "q_norm_g", "w_uq", "kv_norm_g", "w_ukv", "w_o", "norm_ffn_g", "w_up",
             "ffn_conv_w", "ffn_conv_b", "w_down", "ple_norm_g", "w_ple_gate", "w_ple_proj", "final_norm_g"]
    w = dict(zip(names, (norm_mix_g, w_in, conv_w, q_norm_g, w_uq, kv_norm_g, w_ukv, w_o, norm_ffn_g, w_up,
                         ffn_conv_w, ffn_conv_b, w_down, ple_norm_g, w_ple_gate, w_ple_proj, final_norm_g)))
    m = dict(zip(names, (m_norm_mix_g, m_w_in, m_conv_w, m_q_norm_g, m_w_uq, m_kv_norm_g, m_w_ukv, m_w_o,
                         m_norm_ffn_g, m_w_up, m_ffn_conv_w, m_ffn_conv_b, m_w_down, m_ple_norm_g, m_w_ple_gate,
                         m_w_ple_proj, m_final_norm_g)))
    v = dict(zip(names, (v_norm_mix_g, v_w_in, v_conv_w, v_q_norm_g, v_w_uq, v_kv_norm_g, v_w_ukv, v_w_o,
                         v_norm_ffn_g, v_w_up, v_ffn_conv_w, v_ffn_conv_b, v_w_down, v_ple_norm_g, v_w_ple_gate,
                         v_w_ple_proj, v_final_norm_g)))

    core = lax.axis_index("c").astype(jnp.int32).reshape(1)
    chip = (2 * lax.axis_index("x") + lax.axis_index("y")).astype(jnp.int32).reshape(1)

    wts = _early_weights(w)
    gather = _chips_start([w[n][0].astype(BF16) for n in _LATE_W], scatter=False, name="gather_late_start")
    wts["norm_mix_g"] = wts["norm_mix_g"] + gather[4][0, 0]

    def late_weights(after):
        shards, landed = _chips_wait(gather, after, scatter=False, name="gather_late_wait")
        full = {n: _full_from_slots(n, lax.dynamic_update_slice(g4, own[None], (chip[0], 0, 0)))
                for n, own, g4 in zip(_LATE_W, shards, landed)}
        return _layout_late(full)

    def reduce_early(g):
        parts = _chip_partials(_LATE_W, g, core, tag="early")
        scatter = _chips_start([t16 for _, t16 in parts], scatter=True, name="scatter_early_start")

        def finish(after):
            _, landed = _chips_wait(scatter, after, scatter=True, name="scatter_early_wait")
            return [_sum_chips(a, t32, chip, name="sum_chips_" + n) for n, a, (t32, _) in zip(_LATE_W, landed, parts)]

        return scatter[4][0, 0], finish

    loss, grad_x, padded, halves_early = _local_step(x[0], p[0, 0], loss_target[0], wts, late_weights, reduce_early)
    g_full = _true_gradients(padded)
    loss = lax.psum(loss, ("x", "y", "c"))

    parts = _chip_partials(_EARLY_W, g_full, core, tag="late")
    landed = _scatter_chips([t16 for _, t16 in parts])
    halves = [_sum_chips(a, t32, chip, name="sum_chips_" + n) for n, a, (t32, _) in zip(_EARLY_W, landed, parts)]
    whole = _join_halves(halves + halves_early)
    big = {n: a.reshape(-1, a.shape[2]) for n, a in zip(_BIG, whole)}

    g_out, d_out, m_out, v_out = {}, {}, {}, {}
    for n in _BIG:
        shape = w[n].shape
        g = big[n]
        d, mn, vn = _adamw(w[n][0], g, m[n][0], v[n][0], name="adamw_" + n)
        g_out[n], d_out[n], m_out[n], v_out[n] = (a.reshape(shape) for a in (g, d, mn, vn))

    small = _reduce_small(g_full)
    chip = 2 * lax.axis_index("x") + lax.axis_index("y")
    g_small = {}
    for n in _SMALL:
        shape = w[n].shape
        g = small[n]
        if n in ("conv_w", "ffn_conv_w"):
            width = shape[-1]
            g = lax.dynamic_slice(g.reshape(3, N_CHIPS * width), (0, chip * width), (3, width))
        g_small[n] = g.reshape(shape)
    sizes = [g_small[n].size for n in _SMALL]
    rows = -(-sum(sizes) // 1024) * 8
    packed = [_pack([src[n] for n in _SMALL], rows) for src in (w, g_small, m, v)]
    d_s, m_s, v_s = _adamw(*packed, name="adamw_small")
    for n, d, mn, vn in zip(_SMALL, _unpack(d_s, sizes), _unpack(m_s, sizes), _unpack(v_s, sizes)):
        shape = w[n].shape
        g_out[n], d_out[n], m_out[n], v_out[n] = g_small[n], d.reshape(shape), mn.reshape(shape), vn.reshape(shape)

    return (loss, grad_x[None], *[g_out[n] for n in names], *[d_out[n] for n in names],
            *[m_out[n] for n in names], *[v_out[n] for n in names])
```

```python
import functools

import jax
import jax.numpy as jnp
from jax import lax
from jax.experimental import pallas as pl
from jax.experimental.pallas import tpu as pltpu

F32 = jnp.float32
BF16 = jnp.bfloat16

D_MODEL = 1024
CONV_WIDTH = 512
Q_LORA = 256
KV_LORA = 128
QK_NOPE = 64
QK_ROPE = 32
V_HEAD = 64
N_HEADS = 8
HEAD_PAD = 128
D_ATT = N_HEADS * HEAD_PAD
D_IN = 3 * CONV_WIDTH + Q_LORA + KV_LORA + QK_ROPE
D_IN_PAD = 3 * CONV_WIDTH + Q_LORA + KV_LORA + HEAD_PAD
D_FF = 2816
ROPE_THETA = 10000.0
EPS = 1e-6
SM_SCALE = (QK_NOPE + QK_ROPE) ** -0.5
ONES_LANE = V_HEAD

ADAM_LR = 0.001
ADAM_B1 = 0.9
ADAM_B2 = 0.999
ADAM_EPS = 1e-08
ADAM_WD = 0.01
ADAM_STEP = 10

N_CHIPS = 4
N_DEV = 8
MESH = pl.DeviceIdType.MESH
ANY = pl.BlockSpec(memory_space=pl.ANY)


def _params(sem):
    return pltpu.CompilerParams(dimension_semantics=sem)


def _mm(a, b, *, name, ta=False, tb=False, add=None, out_dtype=F32, tm=512, tn=512, tk=512,
        a_split=False, b_split=False, o_split=False):
    if a_split:
        _, m, kh = a.shape
        k = 2 * kh
    elif ta:
        k, m = a.shape
    else:
        m, k = a.shape
    if b_split:
        _, kb, nh = b.shape
        n = 2 * nh
    elif tb:
        n, kb = b.shape
    else:
        kb, n = b.shape
    assert kb == k, (name, a.shape, b.shape)
    tm, tn, tk = min(tm, m), min(tn, n), min(tk, k)
    assert m % tm == 0 and n % tn == 0 and k % tk == 0, (name, m, n, k, tm, tn, tk)
    gm, gn, gk = m // tm, n // tn, k // tk

    if a_split:
        assert gk % 2 == 0
        a_spec = pl.BlockSpec((None, tm, tk), lambda i, j, kk: (kk // (gk // 2), i, kk % (gk // 2)))
    elif ta:
        a_spec = pl.BlockSpec((tk, tm), lambda i, j, kk: (kk, i))
    else:
        a_spec = pl.BlockSpec((tm, tk), lambda i, j, kk: (i, kk))
    if b_split:
        assert gn % 2 == 0
        b_spec = pl.BlockSpec((None, tk, tn), lambda i, j, kk: (j // (gn // 2), kk, j % (gn // 2)))
    elif tb:
        b_spec = pl.BlockSpec((tn, tk), lambda i, j, kk: (j, kk))
    else:
        b_spec = pl.BlockSpec((tk, tn), lambda i, j, kk: (kk, j))
    if o_split:
        assert gn % 2 == 0
        o_spec = pl.BlockSpec((None, tm, tn), lambda i, j, kk: (j // (gn // 2), i, j % (gn // 2)))
        o_shape = jax.ShapeDtypeStruct((2, m, n // 2), out_dtype)
    else:
        o_spec = pl.BlockSpec((tm, tn), lambda i, j, kk: (i, j))
        o_shape = jax.ShapeDtypeStruct((m, n), out_dtype)
    dims = (((0 if ta else 1,), (1 if tb else 0,)), ((), ()))

    def body(*refs):
        if add is None:
            a_ref, b_ref, o_ref, acc_ref = refs
            add_ref = None
        else:
            a_ref, b_ref, add_ref, o_ref, acc_ref = refs
        kk = pl.program_id(2)

        @pl.when(kk == 0)
        def _():
            acc_ref[...] = jnp.zeros_like(acc_ref)

        acc_ref[...] += lax.dot_general(a_ref[...].astype(BF16), b_ref[...].astype(BF16), dims,
                                        preferred_element_type=F32)

        @pl.when(kk == gk - 1)
        def _():
            r = acc_ref[...]
            if add_ref is not None:
                r = r + add_ref[...]
            o_ref[...] = r.astype(o_ref.dtype)

    in_specs = [a_spec, b_spec]
    args = [a, b]
    if add is not None:
        in_specs.append(pl.BlockSpec((tm, tn), lambda i, j, kk: (i, j)))
        args.append(add)
    return pl.pallas_call(
        body, name=name, grid=(gm, gn, gk), in_specs=in_specs, out_specs=o_spec, out_shape=o_shape,
        scratch_shapes=[pltpu.VMEM((tm, tn), F32)],
        compiler_params=_params(("parallel", "parallel", "arbitrary")),
    )(*args)


def _rms_scale(v):
    return lax.rsqrt(jnp.mean(v * v, axis=-1, keepdims=True) + EPS)


def _rms_bwd_rows(v, g, dy):
    r = _rms_scale(v)
    vh = v * r
    dyg = dy * g
    dv = r * (dyg - vh * jnp.mean(dyg * vh, axis=-1, keepdims=True))
    return dv, dy * vh


def _shift_down(v, first_row):
    row = lax.broadcasted_iota(jnp.int32, v.shape, 0)
    return jnp.where(row == 0, first_row, pltpu.roll(v, 1, 0))


def _shift_up(v, last_row):
    n = v.shape[0]
    row = lax.broadcasted_iota(jnp.int32, v.shape, 0)
    return jnp.where(row == n - 1, last_row, pltpu.roll(v, n - 1, 0))


def _rope(t, cos, sin_a, sin_b):
    return t * cos + pltpu.roll(t, HEAD_PAD - 16, 1) * sin_a + pltpu.roll(t, 16, 1) * sin_b


def _rope_bwd(d, cos, sin_a, sin_b):
    return d * cos + pltpu.roll(d * sin_a, 16, 1) + pltpu.roll(d * sin_b, HEAD_PAD - 16, 1)


def _sigmoid(v):
    return 1.0 / (1.0 + jnp.exp(-v))


def _halo_specs(ts, s, width, col):
    nb = ts // 8
    last = s // 8 - 1
    prev = pl.BlockSpec((8, width), lambda i: (jnp.maximum(i * nb - 1, 0), col))
    nxt = pl.BlockSpec((8, width), lambda i: (jnp.minimum((i + 1) * nb, last), col))
    return prev, nxt


def _rms_fwd(x, g, *, name, ts=512):
    s, d = x.shape

    def body(x_ref, g_ref, h_ref):
        v = x_ref[...]
        h_ref[...] = (v * _rms_scale(v) * g_ref[...]).astype(h_ref.dtype)

    return pl.pallas_call(
        body, name=name, grid=(s // ts,),
        in_specs=[pl.BlockSpec((ts, d), lambda i: (i, 0)), pl.BlockSpec((1, d), lambda i: (0, 0))],
        out_specs=pl.BlockSpec((ts, d), lambda i: (i, 0)),
        out_shape=jax.ShapeDtypeStruct((s, d), BF16),
        compiler_params=_params(("parallel",)),
    )(x, g)


def _rms_bwd(x, g, dy, add, *, name, ts=256):
    s, d = x.shape

    def body(x_ref, g_ref, dy_ref, add_ref, dx_ref, dg_ref):
        @pl.when(pl.program_id(0) == 0)
        def _():
            dg_ref[...] = jnp.zeros_like(dg_ref)

        dv, dgr = _rms_bwd_rows(x_ref[...], g_ref[...], dy_ref[...])
        dx_ref[...] = dv + add_ref[...]
        dg_ref[...] += jnp.sum(dgr, axis=0, keepdims=True)

    row = pl.BlockSpec((ts, d), lambda i: (i, 0))
    vec = pl.BlockSpec((1, d), lambda i: (0, 0))
    return pl.pallas_call(
        body, name=name, grid=(s // ts,),
        in_specs=[row, vec, row, row], out_specs=[row, vec],
        out_shape=[jax.ShapeDtypeStruct((s, d), F32), jax.ShapeDtypeStruct((1, d), F32)],
        compiler_params=_params(("arbitrary",)),
    )(x, g, dy, add)


def _mix_pre(z, conv_w8, gq, gkv, cos, sin_a, sin_b, *, ts=256):
    s = z.shape[0]
    n = s // ts
    cw = CONV_WIDTH

    def body(z_ref, xcp, xcn, cgp, cgn, w_ref, gq_ref, gkv_ref, cos_ref, sa_ref, sb_ref,
             yc_ref, qn_ref, kvn_ref, kr_ref):
        i = pl.program_id(0)
        xc = z_ref[:, 0:cw]
        bg = z_ref[:, cw:2 * cw]
        cg = z_ref[:, 2 * cw:3 * cw]
        m = cg * xc
        m_prev = jnp.where(i > 0, xcp[7:8, :] * cgp[7:8, :], 0.0)
        m_next = jnp.where(i < n - 1, xcn[0:1, :] * cgn[0:1, :], 0.0)
        cm = _shift_down(m, m_prev) * w_ref[0:1, :] + m * w_ref[1:2, :] + _shift_up(m, m_next) * w_ref[2:3, :]
        yc_ref[...] = (bg * cm).astype(BF16)
        ql = z_ref[:, 3 * cw:3 * cw + Q_LORA]
        qn_ref[...] = (ql * _rms_scale(ql) * gq_ref[...]).astype(BF16)
        kvl = z_ref[:, 3 * cw + Q_LORA:3 * cw + Q_LORA + KV_LORA]
        kvn_ref[...] = (kvl * _rms_scale(kvl) * gkv_ref[...]).astype(BF16)
        kr_ref[...] = _rope(z_ref[:, D_IN_PAD - HEAD_PAD:D_IN_PAD], cos_ref[...], sa_ref[...], sb_ref[...])

    xcp, xcn = _halo_specs(ts, s, cw, 0)
    cgp, cgn = _halo_specs(ts, s, cw, 2)
    tab = pl.BlockSpec((ts, HEAD_PAD), lambda i: (i, 0))
    return pl.pallas_call(
        body, name="mix_pre", grid=(n,),
        in_specs=[pl.BlockSpec((ts, D_IN_PAD), lambda i: (i, 0)), xcp, xcn, cgp, cgn,
                  pl.BlockSpec((8, cw), lambda i: (0, 0)), pl.BlockSpec((1, Q_LORA), lambda i: (0, 0)),
                  pl.BlockSpec((1, KV_LORA), lambda i: (0, 0)), tab, tab, tab],
        out_specs=[pl.BlockSpec((ts, cw), lambda i: (i, 0)), pl.BlockSpec((ts, Q_LORA), lambda i: (i, 0)),
                   pl.BlockSpec((ts, KV_LORA), lambda i: (i, 0)), tab],
        out_shape=[jax.ShapeDtypeStruct((s, cw), BF16), jax.ShapeDtypeStruct((s, Q_LORA), BF16),
                   jax.ShapeDtypeStruct((s, KV_LORA), BF16), jax.ShapeDtypeStruct((s, HEAD_PAD), F32)],
        compiler_params=_params(("parallel",)),
    )(z, z, z, z, z, conv_w8, gq, gkv, cos, sin_a, sin_b)


def _mix_bwd(z, dyc, dqn, dkvn, dkr, conv_w8, gq, gkv, cos, sin_a, sin_b, *, ts=256):
    s = z.shape[0]
    n = s // ts
    cw = CONV_WIDTH

    def body(z_ref, xcp, xcn, bgp, bgn, cgp, cgn, dyc_ref, dycp, dycn, dqn_ref, dkvn_ref, dkr_ref,
             w_ref, gq_ref, gkv_ref, cos_ref, sa_ref, sb_ref,
             dz_ref, dw0_ref, dw1_ref, dw2_ref, dgq_ref, dgkv_ref):
        i = pl.program_id(0)

        @pl.when(i == 0)
        def _():
            for r in (dw0_ref, dw1_ref, dw2_ref, dgq_ref, dgkv_ref):
                r[...] = jnp.zeros_like(r)

        xc = z_ref[:, 0:cw]
        bg = z_ref[:, cw:2 * cw]
        cg = z_ref[:, 2 * cw:3 * cw]
        w0, w1, w2 = w_ref[0:1, :], w_ref[1:2, :], w_ref[2:3, :]
        m = cg * xc
        m_dn = _shift_down(m, jnp.where(i > 0, xcp[7:8, :] * cgp[7:8, :], 0.0))
        m_up = _shift_up(m, jnp.where(i < n - 1, xcn[0:1, :] * cgn[0:1, :], 0.0))
        cm = m_dn * w0 + m * w1 + m_up * w2
        dyc_v = dyc_ref[...]
        dcm = dyc_v * bg
        dcm_dn = _shift_down(dcm, jnp.where(i > 0, dycp[7:8, :] * bgp[7:8, :], 0.0))
        dcm_up = _shift_up(dcm, jnp.where(i < n - 1, dycn[0:1, :] * bgn[0:1, :], 0.0))
        dm = dcm_up * w0 + dcm * w1 + dcm_dn * w2
        dz_ref[:, 0:cw] = (dm * cg).astype(BF16)
        dz_ref[:, cw:2 * cw] = (dyc_v * cm).astype(BF16)
        dz_ref[:, 2 * cw:3 * cw] = (dm * xc).astype(BF16)
        dw0_ref[...] += jnp.sum(dcm * m_dn, axis=0, keepdims=True)
        dw1_ref[...] += jnp.sum(dcm * m, axis=0, keepdims=True)
        dw2_ref[...] += jnp.sum(dcm * m_up, axis=0, keepdims=True)

        dql, dgq_rows = _rms_bwd_rows(z_ref[:, 3 * cw:3 * cw + Q_LORA], gq_ref[...], dqn_ref[...])
        dz_ref[:, 3 * cw:3 * cw + Q_LORA] = dql.astype(BF16)
        dgq_ref[...] += jnp.sum(dgq_rows, axis=0, keepdims=True)
        dkvl, dgkv_rows = _rms_bwd_rows(z_ref[:, 3 * cw + Q_LORA:3 * cw + Q_LORA + KV_LORA], gkv_ref[...],
                                        dkvn_ref[...])
        dz_ref[:, 3 * cw + Q_LORA:3 * cw + Q_LORA + KV_LORA] = dkvl.astype(BF16)
        dgkv_ref[...] += jnp.sum(dgkv_rows, axis=0, keepdims=True)

        lane = lax.broadcasted_iota(jnp.int32, (ts, HEAD_PAD), 1)
        rope_lane = (lane >= QK_NOPE) & (lane < QK_NOPE + QK_ROPE)
        dk = _rope_bwd(dkr_ref[...], cos_ref[...], sa_ref[...], sb_ref[...])
        dz_ref[:, D_IN_PAD - HEAD_PAD:D_IN_PAD] = jnp.where(rope_lane, dk, 0.0).astype(BF16)

    xcp, xcn = _halo_specs(ts, s, cw, 0)
    bgp, bgn = _halo_specs(ts, s, cw, 1)
    cgp, cgn = _halo_specs(ts, s, cw, 2)
    dycp, dycn = _halo_specs(ts, s, cw, 0)
    tab = pl.BlockSpec((ts, HEAD_PAD), lambda i: (i, 0))

    def vec(width):
        return pl.BlockSpec((1, width), lambda i: (0, 0))

    outs = pl.pallas_call(
        body, name="mix_bwd", grid=(n,),
        in_specs=[pl.BlockSpec((ts, D_IN_PAD), lambda i: (i, 0)), xcp, xcn, bgp, bgn, cgp, cgn,
                  pl.BlockSpec((ts, cw), lambda i: (i, 0)), dycp, dycn,
                  pl.BlockSpec((ts, Q_LORA), lambda i: (i, 0)), pl.BlockSpec((ts, KV_LORA), lambda i: (i, 0)), tab,
                  pl.BlockSpec((8, cw), lambda i: (0, 0)), vec(Q_LORA), vec(KV_LORA), tab, tab, tab],
        out_specs=[pl.BlockSpec((ts, D_IN_PAD), lambda i: (i, 0)), vec(cw), vec(cw), vec(cw), vec(Q_LORA),
                   vec(KV_LORA)],
        out_shape=[jax.ShapeDtypeStruct((s, D_IN_PAD), BF16)] + [jax.ShapeDtypeStruct((1, cw), F32)] * 3
        + [jax.ShapeDtypeStruct((1, Q_LORA), F32), jax.ShapeDtypeStruct((1, KV_LORA), F32)],
        compiler_params=_params(("arbitrary",)),
    )(z, z, z, z, z, z, z, dyc, dyc, dyc, dqn, dkvn, dkr, conv_w8, gq, gkv, cos, sin_a, sin_b)
    dz, dw0, dw1, dw2, dgq, dgkv = outs
    return dz, jnp.concatenate([dw0, dw1, dw2], axis=0), dgq, dgkv


def _qkv_proj(qn, kvn, kr, w_uq_p, w_kv_p, cos, sin_a, sin_b, *, ts=512):
    s = qn.shape[0]

    def body(qn_ref, kvn_ref, kr_ref, wq_ref, wkv_ref, cos_ref, sa_ref, sb_ref, q_ref, k_ref, v_ref):
        cos_v, sa, sb = cos_ref[...], sa_ref[...], sb_ref[...]
        q = jnp.dot(qn_ref[...], wq_ref[...], preferred_element_type=F32)
        kv = jnp.dot(kvn_ref[...], wkv_ref[...], preferred_element_type=F32)
        kr_v = kr_ref[...]
        lane = lax.broadcasted_iota(jnp.int32, (1, HEAD_PAD), 1)
        ones_lane = (lane == ONES_LANE).astype(F32)
        for h in range(N_HEADS):
            blk = slice(h * HEAD_PAD, (h + 1) * HEAD_PAD)
            q_ref[:, blk] = (_rope(q[:, blk], cos_v, sa, sb) * SM_SCALE).astype(BF16)
            k_ref[:, blk] = (kv[:, blk] + kr_v).astype(BF16)
            v_ref[:, blk] = (kv[:, D_ATT + h * HEAD_PAD:D_ATT + (h + 1) * HEAD_PAD] + ones_lane).astype(BF16)

    tab = pl.BlockSpec((ts, HEAD_PAD), lambda i: (i, 0))
    wide = pl.BlockSpec((ts, D_ATT), lambda i: (i, 0))
    return pl.pallas_call(
        body, name="qkv_proj", grid=(s // ts,),
        in_specs=[pl.BlockSpec((ts, Q_LORA), lambda i: (i, 0)), pl.BlockSpec((ts, KV_LORA), lambda i: (i, 0)), tab,
                  pl.BlockSpec((Q_LORA, D_ATT), lambda i: (0, 0)), pl.BlockSpec((KV_LORA, 2 * D_ATT), lambda i: (0, 0)),
                  tab, tab, tab],
        out_specs=[wide, wide, wide],
        out_shape=[jax.ShapeDtypeStruct((s, D_ATT), BF16)] * 3,
        compiler_params=_params(("parallel",)),
    )(qn, kvn, kr, w_uq_p, w_kv_p, cos, sin_a, sin_b)


def _qk_bwd(dq, dk, cos, sin_a, sin_b, *, ts=256):
    s = dq.shape[0]

    def body(dq_ref, dk_ref, cos_ref, sa_ref, sb_ref, dqp_ref, dkr_ref):
        cos_v, sa, sb = cos_ref[...], sa_ref[...], sb_ref[...]
        tot = jnp.zeros((ts, HEAD_PAD), F32)
        for h in range(N_HEADS):
            blk = slice(h * HEAD_PAD, (h + 1) * HEAD_PAD)
            dqp_ref[:, blk] = _rope_bwd(dq_ref[:, blk], cos_v, sa, sb).astype(BF16)
            tot = tot + dk_ref[:, blk]
        dkr_ref[...] = tot

    tab = pl.BlockSpec((ts, HEAD_PAD), lambda i: (i, 0))
    wide = pl.BlockSpec((ts, D_ATT), lambda i: (i, 0))
    return pl.pallas_call(
        body, name="qk_bwd", grid=(s // ts,),
        in_specs=[wide, wide, tab, tab, tab], out_specs=[wide, tab],
        out_shape=[jax.ShapeDtypeStruct((s, D_ATT), BF16), jax.ShapeDtypeStruct((s, HEAD_PAD), F32)],
        compiler_params=_params(("parallel",)),
    )(dq, dk, cos, sin_a, sin_b)


_NT = (((1,), (1,)), ((), ()))
_TN = (((0,), (0,)), ((), ()))


def _flash_fwd(q, k, v, *, tq=1024, tk=1024):
    s = q.shape[0]
    tq, tk = min(tq, s), min(tk, s)
    nk = s // tk

    def body(q_ref, k_ref, v_ref, o_ref, lse_ref):
        qv = q_ref[...]

        def step(j, carry):
            m, acc = carry
            rows = pl.ds(pl.multiple_of(j * tk, tk), tk)
            sc = lax.dot_general(qv, k_ref[rows, :], _NT, preferred_element_type=F32)
            m_new = jnp.maximum(m, jnp.max(sc, axis=1, keepdims=True))
            p = jnp.exp(sc - m_new).astype(BF16)
            acc = jnp.exp(m - m_new) * acc + jnp.dot(p, v_ref[rows, :], preferred_element_type=F32)
            return m_new, acc

        init = (jnp.full((tq, 1), -jnp.inf, F32), jnp.zeros((tq, HEAD_PAD), F32))
        m, acc = lax.fori_loop(0, nk, step, init)
        l = acc[:, ONES_LANE:ONES_LANE + 1]
        o_ref[...] = (acc / l).astype(BF16)
        lse_ref[...] = m + jnp.log(l)

    head = pl.BlockSpec((s, HEAD_PAD), lambda h, i: (0, h))
    return pl.pallas_call(
        body, name="flash_fwd", grid=(N_HEADS, s // tq),
        in_specs=[pl.BlockSpec((tq, HEAD_PAD), lambda h, i: (i, h)), head, head],
        out_specs=[pl.BlockSpec((tq, HEAD_PAD), lambda h, i: (i, h)),
                   pl.BlockSpec((None, tq, 1), lambda h, i: (h, i, 0))],
        out_shape=[jax.ShapeDtypeStruct((s, D_ATT), BF16), jax.ShapeDtypeStruct((N_HEADS, s, 1), F32)],
        compiler_params=_params(("parallel", "parallel")),
    )(q, k, v)


def _attn_delta(do, o, *, ts=512):
    s = do.shape[0]

    def body(do_ref, o_ref, dl_ref):
        for h in range(N_HEADS):
            blk = slice(h * HEAD_PAD, (h + 1) * HEAD_PAD)
            dl_ref[h] = jnp.sum(do_ref[:, blk].astype(F32) * o_ref[:, blk].astype(F32), axis=1, keepdims=True)

    wide = pl.BlockSpec((ts, D_ATT), lambda i: (i, 0))
    return pl.pallas_call(
        body, name="attn_delta", grid=(s // ts,), in_specs=[wide, wide],
        out_specs=pl.BlockSpec((N_HEADS, ts, 1), lambda i: (0, i, 0)),
        out_shape=jax.ShapeDtypeStruct((N_HEADS, s, 1), F32),
        compiler_params=_params(("parallel",)),
    )(do, o)


def _flash_bwd(q, qt, k, v, do, dot, lse, delta, after, *, tq=1024, tk=512):
    s = q.shape[0]
    tq, tk = min(tq, s), min(tk, s)
    nq = s // tq

    def body(q_ref, qt_ref, do_ref, dot_ref, lse_ref, dl_ref, k_ref, v_ref, after_ref, dq_ref, dk_ref, dv_ref):
        j = pl.program_id(1)

        @pl.when(j == 0)
        def _():
            dq_ref[...] = jnp.zeros_like(dq_ref)

        kv, vv = k_ref[...], v_ref[...]

        def step(i, carry):
            dk_t, dv_t = carry
            at = pl.multiple_of(i * tq, tq)
            rows = pl.ds(at, tq)
            sc = lax.dot_general(q_ref[rows, :], kv, _NT, preferred_element_type=F32)
            p = jnp.exp(sc - lse_ref[rows, :])
            dp = lax.dot_general(do_ref[rows, :], vv, _NT, preferred_element_type=F32)
            ds = (p * (dp - dl_ref[rows, :])).astype(BF16)
            dv_t = dv_t + jnp.dot(dot_ref[:, rows], p.astype(BF16), preferred_element_type=F32)
            dk_t = dk_t + jnp.dot(qt_ref[:, rows], ds, preferred_element_type=F32)
            dq_ref[rows, :] += jnp.dot(ds, kv, preferred_element_type=F32)
            return dk_t, dv_t

        zero = jnp.zeros((HEAD_PAD, tk), F32)
        dk_t, dv_t = lax.fori_loop(0, nq, step, (zero, zero))
        dk_ref[...] = dk_t.T
        dv_ref[...] = dv_t.T

        @pl.when(j == pl.num_programs(1) - 1)
        def _():
            dq_ref[...] *= SM_SCALE

    head = pl.BlockSpec((s, HEAD_PAD), lambda h, j: (0, h))
    head_t = pl.BlockSpec((HEAD_PAD, s), lambda h, j: (h, 0))
    stat = pl.BlockSpec((None, s, 1), lambda h, j: (h, 0, 0))
    blk = pl.BlockSpec((tk, HEAD_PAD), lambda h, j: (j, h))
    return pl.pallas_call(
        body, name="flash_bwd", grid=(N_HEADS, s // tk),
        in_specs=[head, head_t, head, head_t, stat, stat, blk, blk, ANY],
        out_specs=[head, blk, blk],
        out_shape=[jax.ShapeDtypeStruct((s, D_ATT), F32)] * 3,
        compiler_params=_params(("parallel", "arbitrary")),
    )(q, qt, do, dot, lse, delta, k, v, after)


FFN_TC = 256


FFN_HALO_BF16 = 16
FFN_HALO_F32 = 8


def _row_halo_specs(ts, s, halo, width):
    nb = ts // halo
    last = s // halo - 1
    prev = pl.BlockSpec((halo, width), lambda i, j: (jnp.maximum(i * nb - 1, 0), 0))
    nxt = pl.BlockSpec((halo, width), lambda i, j: (jnp.minimum((i + 1) * nb, last), 0))
    return prev, nxt


def _ext_rows(prev, main, nxt, first, last):
    return jnp.concatenate([jnp.where(first, jnp.zeros_like(prev), prev), main,
                            jnp.where(last, jnp.zeros_like(nxt), nxt)], axis=0)


def _ext_conv(a, w):
    a_dn = pltpu.roll(a, 1, 0)
    a_up = pltpu.roll(a, a.shape[0] - 1, 0)
    return a_dn * w[0:1, :] + a * w[1:2, :] + a_up * w[2:3, :], a_dn, a_up


def _ffn_fwd(hf, w_up, w, b, *, ts=512, tc=FFN_TC):
    s = hf.shape[0]
    n, nj, halo = s // ts, D_FF // tc, FFN_HALO_BF16

    def body(h_ref, hp_ref, hn_ref, wg_ref, wu_ref, cw_ref, cb_ref, a_ref, act_ref):
        i = pl.program_id(0)
        ext = _ext_rows(hp_ref[...], h_ref[...], hn_ref[...], i == 0, i == n - 1)
        gate_up = []
        for half, w_ref in enumerate((wg_ref, wu_ref)):
            a_ext = jnp.dot(ext, w_ref[...], preferred_element_type=F32)
            a_ref[half] = a_ext[halo:halo + ts]
            gate_up.append(_ext_conv(a_ext, cw_ref[half])[0][halo:halo + ts] + cb_ref[half])
        g, u = gate_up
        act_ref[...] = (g * _sigmoid(g) * u).astype(BF16)

    prev, nxt = _row_halo_specs(ts, s, halo, D_MODEL)
    return pl.pallas_call(
        body, name="ffn_fwd", grid=(n, nj),
        in_specs=[pl.BlockSpec((ts, D_MODEL), lambda i, j: (i, 0)), prev, nxt,
                  pl.BlockSpec((D_MODEL, tc), lambda i, j: (0, j)), pl.BlockSpec((D_MODEL, tc), lambda i, j: (0, j + nj)),
                  pl.BlockSpec((2, 8, tc), lambda i, j: (0, 0, j)), pl.BlockSpec((2, 1, tc), lambda i, j: (0, 0, j))],
        out_specs=[pl.BlockSpec((2, ts, tc), lambda i, j: (0, i, j)), pl.BlockSpec((ts, tc), lambda i, j: (i, j))],
        out_shape=[jax.ShapeDtypeStruct((2, s, D_FF), F32), jax.ShapeDtypeStruct((s, D_FF), BF16)],
        compiler_params=_params(("parallel", "parallel")),
    )(hf, hf, hf, w_up, w_up, w, b)


def _ffn_bwd(dx2, w_down, a_pre, w, b, *, ts=512, tc=FFN_TC):
    s = dx2.shape[0]
    n, nj, halo = s // ts, D_FF // tc, FFN_HALO_F32
    main = slice(halo, halo + ts)

    def body(dx_ref, dxp_ref, dxn_ref, wd_ref, a_ref, ap_ref, an_ref, cw_ref, cb_ref, o_ref, dw_ref, db_ref):
        i, j = pl.program_id(0), pl.program_id(1)
        first, last = i == 0, i == n - 1

        @pl.when(first & (j == 0))
        def _():
            dw_ref[...] = jnp.zeros_like(dw_ref)
            db_ref[...] = jnp.zeros_like(db_ref)

        dx_ext = _ext_rows(dxp_ref[...], dx_ref[...], dxn_ref[...], first, last).astype(BF16)
        dact = lax.dot_general(dx_ext, wd_ref[...], _NT, preferred_element_type=F32)
        halves = []
        for half in range(2):
            a_ext = _ext_rows(ap_ref[half], a_ref[half], an_ref[half], first, last)
            conv, a_dn, a_up = _ext_conv(a_ext, cw_ref[half])
            halves.append((conv + cb_ref[half], a_dn, a_ext, a_up))
        g, u = halves[0][0], halves[1][0]
        sg = _sigmoid(g)
        grads = (dact * u * (sg * (1.0 + g * (1.0 - sg))), dact * (g * sg))
        for half in range(2):
            d = grads[half]
            _, a_dn, a_ext, a_up = halves[half]
            wv = cw_ref[half]
            d_pre = pltpu.roll(d, d.shape[0] - 1, 0) * wv[0:1, :] + d * wv[1:2, :] + pltpu.roll(d, 1, 0) * wv[2:3, :]
            o_ref[half] = d_pre[main].astype(BF16)
            dm = d[main]
            dw_ref[j, half, 0:1, :] += jnp.sum(dm * a_dn[main], axis=0, keepdims=True)
            dw_ref[j, half, 1:2, :] += jnp.sum(dm * a_ext[main], axis=0, keepdims=True)
            dw_ref[j, half, 2:3, :] += jnp.sum(dm * a_up[main], axis=0, keepdims=True)
            db_ref[j, half] += jnp.sum(dm, axis=0, keepdims=True)

    dxp, dxn = _row_halo_specs(ts, s, halo, D_MODEL)
    nb, lastb = ts // halo, s // halo - 1
    a_main = pl.BlockSpec((2, ts, tc), lambda i, j: (0, i, j))
    a_prev = pl.BlockSpec((2, halo, tc), lambda i, j: (0, jnp.maximum(i * nb - 1, 0), j))
    a_next = pl.BlockSpec((2, halo, tc), lambda i, j: (0, jnp.minimum((i + 1) * nb, lastb), j))
    da_pre, dw, db = pl.pallas_call(
        body, name="ffn_bwd", grid=(n, nj),
        in_specs=[pl.BlockSpec((ts, D_MODEL), lambda i, j: (i, 0)), dxp, dxn,
                  pl.BlockSpec((tc, D_MODEL), lambda i, j: (j, 0)), a_main, a_prev, a_next,
                  pl.BlockSpec((2, 8, tc), lambda i, j: (0, 0, j)), pl.BlockSpec((2, 1, tc), lambda i, j: (0, 0, j))],
        out_specs=[a_main, pl.BlockSpec((nj, 2, 8, tc), lambda i, j: (0, 0, 0, 0)),
                   pl.BlockSpec((nj, 2, 1, tc), lambda i, j: (0, 0, 0, 0))],
        out_shape=[jax.ShapeDtypeStruct((2, s, D_FF), BF16), jax.ShapeDtypeStruct((nj, 2, 8, tc), F32),
                   jax.ShapeDtypeStruct((nj, 2, 1, tc), F32)],
        compiler_params=_params(("arbitrary", "arbitrary")),
    )(dx2, dx2, dx2, w_down, a_pre, a_pre, a_pre, w, b)
    return (da_pre, dw.transpose(1, 2, 0, 3).reshape(2, 8, D_FF), db.transpose(1, 2, 0, 3).reshape(2, 1, D_FF))


def _ple_final(x2, gl, pp, target, gf, *, ts=256):
    s, d = x2.shape

    def body(x2_ref, gl_ref, pp_ref, t_ref, gf_ref, loss_ref, dx3_ref, dgl_ref, dpp_ref, dgf_ref):
        @pl.when(pl.program_id(0) == 0)
        def _():
            loss_ref[...] = jnp.zeros_like(loss_ref)
            dgf_ref[...] = jnp.zeros_like(dgf_ref)

        gate = _sigmoid(gl_ref[...])
        ppv = pp_ref[...]
        x3 = x2_ref[...] + gate * ppv
        gfv = gf_ref[...]
        err = x3 * _rms_scale(x3) * gfv - t_ref[...]
        loss_ref[...] += 0.5 * jnp.sum(jnp.mean(err * err, axis=-1, keepdims=True), axis=0, keepdims=True)
        dx3, dgf_rows = _rms_bwd_rows(x3, gfv, err * (1.0 / d))
        dgf_ref[...] += jnp.sum(dgf_rows, axis=0, keepdims=True)
        dx3_ref[...] = dx3
        dgl_ref[...] = (dx3 * ppv * gate * (1.0 - gate)).astype(BF16)
        dpp_ref[...] = (dx3 * gate).astype(BF16)

    row = pl.BlockSpec((ts, d), lambda i: (i, 0))
    vec = pl.BlockSpec((1, d), lambda i: (0, 0))
    return pl.pallas_call(
        body, name="ple_final", grid=(s // ts,),
        in_specs=[row, row, row, row, vec],
        out_specs=[pl.BlockSpec((1, 128), lambda i: (0, 0)), row, row, row, vec],
        out_shape=[jax.ShapeDtypeStruct((1, 128), F32), jax.ShapeDtypeStruct((s, d), F32),
                   jax.ShapeDtypeStruct((s, d), BF16), jax.ShapeDtypeStruct((s, d), BF16),
                   jax.ShapeDtypeStruct((1, d), F32)],
        compiler_params=_params(("arbitrary",)),
    )(x2, gl, pp, target, gf)


def _row_tile(rows, cols, n_arrays, budget=12 << 20):
    best = None
    for t in range(8, rows + 1, 8):
        if rows % t == 0 and t * cols * 4 * n_arrays <= budget:
            best = t
    return rows if best is None else best


def _sum_slots(a, *, name):
    g, r, c = a.shape
    tr = _row_tile(r, c, g + 1)

    def body(*refs):
        tot = refs[0][...]
        for ref in refs[1:g]:
            tot = tot + ref[...]
        refs[g][...] = tot

    specs = [pl.BlockSpec((None, tr, c), functools.partial(lambda i, slot: (slot, i, 0), slot=k)) for k in range(g)]
    return pl.pallas_call(
        body, name=name, grid=(r // tr,), in_specs=specs, out_specs=pl.BlockSpec((tr, c), lambda i: (i, 0)),
        out_shape=jax.ShapeDtypeStruct((r, c), a.dtype), compiler_params=_params(("parallel",)),
    )(*([a] * g))


def _adamw(w, g, m, v, *, name):
    r, c = w.shape
    tr = _row_tile(r, c, 7)

    def body(w_ref, g_ref, m_ref, v_ref, d_ref, mo_ref, vo_ref):
        gv = g_ref[...]
        mn = ADAM_B1 * m_ref[...] + (1.0 - ADAM_B1) * gv
        vn = ADAM_B2 * v_ref[...] + (1.0 - ADAM_B2) * (gv * gv)
        m_hat = mn / (1.0 - ADAM_B1 ** ADAM_STEP)
        v_hat = vn / (1.0 - ADAM_B2 ** ADAM_STEP)
        d_ref[...] = -ADAM_LR * (m_hat / (jnp.sqrt(v_hat) + ADAM_EPS) + ADAM_WD * w_ref[...])
        mo_ref[...] = mn
        vo_ref[...] = vn

    blk = pl.BlockSpec((tr, c), lambda i: (i, 0))
    return pl.pallas_call(
        body, name=name, grid=(r // tr,), in_specs=[blk] * 4, out_specs=[blk] * 3,
        out_shape=[jax.ShapeDtypeStruct((r, c), F32)] * 3, compiler_params=_params(("parallel",)),
    )(w, g, m, v)


def _position():
    x, y, c = lax.axis_index("x"), lax.axis_index("y"), lax.axis_index("c")
    return x, y, c


def _other_chips(x, y):
    return [(1 - x, y), (x, 1 - y), (1 - x, 1 - y)]


def _stage_in(srcs, stage, sems):
    cps = [pltpu.make_async_copy(src, stage[a], sems.at[a]) for a, src in enumerate(srcs)]
    for cp in cps:
        cp.start()
    return cps


def _stage_out(staged, stage, dsts, sems):
    cps = []
    for a, dst in enumerate(dsts):
        staged[a].wait()
        cp = pltpu.make_async_copy(stage[a], dst, sems.at[a])
        cp.start()
        cps.append(cp)
    return cps


def _gather_chips(shards):
    n = len(shards)

    def body(*refs):
        ins, outs, stage = refs[:n], refs[n:2 * n], refs[2 * n:3 * n]
        send_sems, recv_sems, in_sems, out_sems = refs[3 * n:]
        x, y, c = _position()
        me = 2 * x + y
        chips = _other_chips(x, y)
        remote = []
        staged = _stage_in(ins, stage, in_sems)
        for a in range(n):
            for k, (px, py) in enumerate(chips):
                rc = pltpu.make_async_remote_copy(
                    src_ref=ins[a], dst_ref=outs[a].at[me], send_sem=send_sems.at[3 * a + k],
                    recv_sem=recv_sems.at[3 * a + k], device_id=(px, py, c), device_id_type=MESH)
                rc.start()
                remote.append(rc)
        local = _stage_out(staged, stage, [o.at[me] for o in outs], out_sems)
        for a in range(n):
            for k, (px, py) in enumerate(chips):
                pltpu.make_async_remote_copy(
                    src_ref=ins[a], dst_ref=outs[a].at[2 * px + py], send_sem=send_sems.at[3 * a + k],
                    recv_sem=recv_sems.at[3 * a + k], device_id=(px, py, c), device_id_type=MESH).wait_recv()
        for rc in remote:
            rc.wait_send()
        for cp in local:
            cp.wait()

    return pl.pallas_call(
        body, name="gather_chips", in_specs=[ANY] * n, out_specs=[ANY] * n,
        out_shape=[jax.ShapeDtypeStruct((N_CHIPS,) + s.shape, s.dtype) for s in shards],
        scratch_shapes=[pltpu.VMEM(s.shape, s.dtype) for s in shards]
        + [pltpu.SemaphoreType.DMA((3 * n,)), pltpu.SemaphoreType.DMA((3 * n,)),
           pltpu.SemaphoreType.DMA((n,)), pltpu.SemaphoreType.DMA((n,))],
        compiler_params=pltpu.CompilerParams(has_side_effects=True),
    )(*shards)


def _send_other_halves(grads, *, tag):
    n = len(grads)

    def body(*refs):
        ins, sib = refs[:n], refs[n:2 * n]
        send_sems, recv_sems = refs[2 * n:]
        x, y, c = _position()
        remote = []
        for a in range(n):
            half = ins[a].shape[1] // 2
            give = ins[a].at[:, pl.ds(pl.multiple_of((1 - c) * half, 8), half), :]
            rc = pltpu.make_async_remote_copy(
                src_ref=give, dst_ref=sib[a], send_sem=send_sems.at[a], recv_sem=recv_sems.at[a],
                device_id=(x, y, 1 - c), device_id_type=MESH)
            rc.start()
            remote.append(rc)
        for rc in remote:
            rc.wait_recv()
        for rc in remote:
            rc.wait_send()

    return pl.pallas_call(
        body, name="send_other_halves_" + tag, in_specs=[ANY] * n, out_specs=[ANY] * n,
        out_shape=[jax.ShapeDtypeStruct((g.shape[0], g.shape[1] // 2, g.shape[2]), g.dtype) for g in grads],
        scratch_shapes=[pltpu.SemaphoreType.DMA((n,)), pltpu.SemaphoreType.DMA((n,))],
        compiler_params=pltpu.CompilerParams(has_side_effects=True),
    )(*grads)


def _add_own_half(g4, sib, core, *, name):
    g, a2, c = sib.shape
    tr = _row_tile(a2, c, 4)

    def body(core_ref, a_ref, b_ref, o_ref, o16_ref):
        tot = a_ref[...] + b_ref[...]
        o_ref[...] = tot
        o16_ref[...] = tot.astype(BF16)

    blk = pl.BlockSpec((None, tr, c), lambda i, j, core_ref: (i, j, 0))
    return pl.pallas_call(
        body, name=name,
        grid_spec=pltpu.PrefetchScalarGridSpec(
            num_scalar_prefetch=1, grid=(g, a2 // tr),
            in_specs=[pl.BlockSpec((None, None, tr, c), lambda i, j, core_ref: (i, core_ref[0], j, 0)), blk],
            out_specs=[blk, blk]),
        out_shape=[jax.ShapeDtypeStruct(sib.shape, F32), jax.ShapeDtypeStruct(sib.shape, BF16)],
        compiler_params=_params(("parallel", "parallel")),
    )(core, g4.reshape(g, 2, a2, c), sib)


def _scatter_chips(parts):
    n = len(parts)

    def body(*refs):
        ins, outs = refs[:n], refs[n:2 * n]
        send_sems, recv_sems = refs[2 * n:]
        x, y, c = _position()
        me = 2 * x + y
        chips = _other_chips(x, y)
        remote = []
        for a in range(n):
            for k, (px, py) in enumerate(chips):
                rc = pltpu.make_async_remote_copy(
                    src_ref=ins[a].at[2 * px + py], dst_ref=outs[a].at[me], send_sem=send_sems.at[3 * a + k],
                    recv_sem=recv_sems.at[3 * a + k], device_id=(px, py, c), device_id_type=MESH)
                rc.start()
                remote.append(rc)
        for a in range(n):
            for k, (px, py) in enumerate(chips):
                pltpu.make_async_remote_copy(
                    src_ref=ins[a].at[me], dst_ref=outs[a].at[2 * px + py], send_sem=send_sems.at[3 * a + k],
                    recv_sem=recv_sems.at[3 * a + k], device_id=(px, py, c), device_id_type=MESH).wait_recv()
        for rc in remote:
            rc.wait_send()

    return pl.pallas_call(
        body, name="scatter_chips", in_specs=[ANY] * n, out_specs=[ANY] * n,
        out_shape=[jax.ShapeDtypeStruct(p.shape, p.dtype) for p in parts],
        scratch_shapes=[pltpu.SemaphoreType.DMA((3 * n,)), pltpu.SemaphoreType.DMA((3 * n,))],
        compiler_params=pltpu.CompilerParams(has_side_effects=True),
    )(*parts)


def _sum_chips(landed, own, chip, *, name):
    g, r, c = landed.shape
    tr = _row_tile(r, c, 5)

    def body(chip_ref, *refs):
        me = chip_ref[0]
        own_v = refs[g][...]
        tot = None
        for slot in range(g):
            term = jnp.where(me == slot, own_v, refs[slot][...].astype(F32))
            tot = term if tot is None else tot + term
        refs[g + 1][...] = tot

    def landed_spec(slot):
        return pl.BlockSpec((None, tr, c),
                            lambda i, chip_ref: (jnp.where(chip_ref[0] == slot, (slot + 1) % g, slot), i, 0))

    return pl.pallas_call(
        body, name=name,
        grid_spec=pltpu.PrefetchScalarGridSpec(
            num_scalar_prefetch=1, grid=(r // tr,),
            in_specs=[landed_spec(k) for k in range(g)]
            + [pl.BlockSpec((None, tr, c), lambda i, chip_ref: (chip_ref[0], i, 0))],
            out_specs=pl.BlockSpec((tr, c), lambda i, chip_ref: (i, 0))),
        out_shape=jax.ShapeDtypeStruct((r, c), F32), compiler_params=_params(("parallel",)),
    )(chip, *([landed] * g), own)


def _join_halves(halves):
    n = len(halves)

    def body(*refs):
        ins, outs, stage = refs[:n], refs[n:2 * n], refs[2 * n:3 * n]
        send_sems, recv_sems, in_sems, out_sems = refs[3 * n:]
        x, y, c = _position()
        remote = []
        staged = _stage_in(ins, stage, in_sems)
        for a in range(n):
            rc = pltpu.make_async_remote_copy(
                src_ref=ins[a], dst_ref=outs[a].at[c], send_sem=send_sems.at[a], recv_sem=recv_sems.at[a],
                device_id=(x, y, 1 - c), device_id_type=MESH)
            rc.start()
            remote.append(rc)
        local = _stage_out(staged, stage, [o.at[c] for o in outs], out_sems)
        for a in range(n):
            pltpu.make_async_remote_copy(
                src_ref=ins[a], dst_ref=outs[a].at[1 - c], send_sem=send_sems.at[a], recv_sem=recv_sems.at[a],
                device_id=(x, y, 1 - c), device_id_type=MESH).wait_recv()
        for rc in remote:
            rc.wait_send()
        for cp in local:
            cp.wait()

    return pl.pallas_call(
        body, name="join_halves", in_specs=[ANY] * n, out_specs=[ANY] * n,
        out_shape=[jax.ShapeDtypeStruct((2,) + h.shape, h.dtype) for h in halves],
        scratch_shapes=[pltpu.VMEM(h.shape, h.dtype) for h in halves]
        + [pltpu.SemaphoreType.DMA((n,)), pltpu.SemaphoreType.DMA((n,)), pltpu.SemaphoreType.DMA((n,)),
           pltpu.SemaphoreType.DMA((n,))],
        compiler_params=pltpu.CompilerParams(has_side_effects=True),
    )(*halves)


_HBM = pl.BlockSpec(memory_space=pltpu.HBM)
_SEM = pl.BlockSpec(memory_space=pltpu.SEMAPHORE)


def _chip_copies(srcs, lands, send_sems, recv_sems, scatter):
    x, y, c = _position()
    me = 2 * x + y
    outgoing, incoming = [], []
    for a, (src, land) in enumerate(zip(srcs, lands)):
        for k, (px, py) in enumerate(_other_chips(x, y)):
            peer = 2 * px + py
            sems = dict(send_sem=send_sems.at[3 * a + k], recv_sem=recv_sems.at[3 * a + k], device_id=(px, py, c),
                        device_id_type=MESH)
            outgoing.append(pltpu.make_async_remote_copy(
                src_ref=src.at[peer] if scatter else src, dst_ref=land.at[me], **sems))
            incoming.append(pltpu.make_async_remote_copy(
                src_ref=src.at[me] if scatter else src, dst_ref=land.at[peer], **sems))
    return outgoing, incoming


def _chips_start(srcs, *, scatter, name):
    n = len(srcs)
    lands = [lax.empty(a.shape if scatter else (N_CHIPS,) + a.shape, a.dtype) for a in srcs]

    def body(*refs):
        ins, send_sems, recv_sems, token = refs[:2 * n], refs[2 * n], refs[2 * n + 1], refs[-1]
        outgoing, _ = _chip_copies(ins[:n], ins[n:], send_sems, recv_sems, scatter)
        for cp in outgoing:
            cp.start()
        token[...] = jnp.zeros_like(token)

    bufs = list(srcs) + lands
    res = pl.pallas_call(
        body, name=name, in_specs=[_HBM] * (2 * n),
        out_specs=(_SEM, _SEM, *[_HBM] * (2 * n), pl.BlockSpec(memory_space=pltpu.VMEM)),
        out_shape=(pltpu.SemaphoreType.DMA((3 * n,)), pltpu.SemaphoreType.DMA((3 * n,)),
                   *[pltpu.HBM(a.shape, a.dtype) for a in bufs], jax.ShapeDtypeStruct((8, 128), F32)),
        input_output_aliases={i: 2 + i for i in range(2 * n)},
        compiler_params=pltpu.CompilerParams(has_side_effects=pltpu.SideEffectType.DATAFLOW_SIDE_EFFECTING),
    )(*[pltpu.with_memory_space_constraint(a, pltpu.HBM) for a in bufs])
    return res[0], res[1], list(res[2:2 + n]), list(res[2 + n:2 + 2 * n]), res[-1]


def _chips_wait(handle, after, *, scatter, name):
    send_sems, recv_sems, srcs, lands, _ = handle
    n = len(srcs)

    def body(*refs):
        ins, send_ref, recv_ref = refs[:2 * n], refs[2 * n], refs[2 * n + 1]
        outgoing, incoming = _chip_copies(ins[:n], ins[n:], send_ref, recv_ref, scatter)
        for cp in outgoing:
            cp.wait_send()
        for cp in incoming:
            cp.wait_recv()

    bufs = list(srcs) + list(lands)
    res = pl.pallas_call(
        body, name=name, in_specs=[_HBM] * (2 * n) + [_SEM, _SEM, ANY], out_specs=tuple([_HBM] * (2 * n)),
        out_shape=tuple(pltpu.HBM(a.shape, a.dtype) for a in bufs),
        input_output_aliases={i: i for i in range(2 * n)},
        compiler_params=pltpu.CompilerParams(has_side_effects=pltpu.SideEffectType.DATAFLOW_SIDE_EFFECTING),
    )(*bufs, send_sems, recv_sems, after)
    return list(res[:n]), list(res[n:])


def _gather_all(buf):
    def body(in_ref, out_ref, send_sems, recv_sems, local_sem):
        x, y, c = _position()
        me = 4 * x + 2 * y + c
        peers = [(x, y, 1 - c)] + [(px, py, pc) for (px, py) in _other_chips(x, y) for pc in (c, 1 - c)]
        cp = pltpu.make_async_copy(in_ref, out_ref.at[me], local_sem)
        cp.start()
        remote = []
        for k, peer in enumerate(peers):
            rc = pltpu.make_async_remote_copy(
                src_ref=in_ref, dst_ref=out_ref.at[me], send_sem=send_sems.at[k], recv_sem=recv_sems.at[k],
                device_id=peer, device_id_type=MESH)
            rc.start()
            remote.append(rc)
        for k, (px, py, pc) in enumerate(peers):
            pltpu.make_async_remote_copy(
                src_ref=in_ref, dst_ref=out_ref.at[4 * px + 2 * py + pc], send_sem=send_sems.at[k],
                recv_sem=recv_sems.at[k], device_id=(px, py, pc), device_id_type=MESH).wait_recv()
        for rc in remote:
            rc.wait_send()
        cp.wait()

    return pl.pallas_call(
        body, name="gather_all", in_specs=[ANY], out_specs=ANY,
        out_shape=jax.ShapeDtypeStruct((N_DEV,) + buf.shape, buf.dtype),
        scratch_shapes=[pltpu.SemaphoreType.DMA((N_DEV - 1,)), pltpu.SemaphoreType.DMA((N_DEV - 1,)),
                        pltpu.SemaphoreType.DMA],
        compiler_params=pltpu.CompilerParams(has_side_effects=True),
    )(buf)


def _cols_from_shards(g4):
    _, k, n = g4.shape
    return g4.transpose(1, 0, 2).reshape(k, N_CHIPS * n)


def _cols_to_shards(w):
    k, n = w.shape
    return w.reshape(k, N_CHIPS, n // N_CHIPS).transpose(1, 0, 2)


def _pad_heads(w, width):
    k = w.shape[0]
    w3 = w.reshape(k, N_HEADS, width)
    return jnp.pad(w3, ((0, 0), (0, 0), (0, HEAD_PAD - width))).reshape(k, D_ATT)


def _unpad_heads(w, width):
    k = w.shape[0]
    return w.reshape(k, N_HEADS, HEAD_PAD)[:, :, :width]


def _rope_tables(s):
    pos = jnp.arange(s, dtype=F32)
    inv_freq = ROPE_THETA ** (-jnp.arange(0, QK_ROPE, 2, dtype=F32) / QK_ROPE)
    ang = pos[:, None] * inv_freq[None, :]
    cos_h, sin_h = jnp.cos(ang), jnp.sin(ang)
    half = QK_ROPE // 2
    z = jnp.zeros((s, half), F32)
    ones = jnp.ones((s, QK_NOPE), F32)
    tail = jnp.zeros((s, HEAD_PAD - QK_NOPE - QK_ROPE), F32)
    cos = jnp.concatenate([ones, cos_h, cos_h, tail + 1.0], axis=1)
    sin_a = jnp.concatenate([ones * 0.0, -sin_h, z, tail], axis=1)
    sin_b = jnp.concatenate([ones * 0.0, z, sin_h, tail], axis=1)
    return cos, sin_a, sin_b


def _local_step(x, p, target, wts, late_weights, reduce_early):
    s = x.shape[0]
    cos, sin_a, sin_b = _rope_tables(s)
    g1, gq, gkv, g2, g3, gf = (wts[k] for k in ("norm_mix_g", "q_norm_g", "kv_norm_g", "norm_ffn_g", "ple_norm_g",
                                                 "final_norm_g"))
    w_in_p, w_uq_p, w_kv_p = wts["w_in_p"], wts["w_uq_p"], wts["w_kv_p"]
    conv_w8, fconv_w, fconv_b = wts["conv_w8"], wts["ffn_conv_w"], wts["ffn_conv_b"]

    h = _rms_fwd(x, g1, name="rms_mix")
    z = _mm(h, w_in_p, name="mm_in", tm=512, tn=1024, tk=1024)
    y_conv, qn, kvn, kr = _mix_pre(z, conv_w8, gq, gkv, cos, sin_a, sin_b)
    q, k, v = _qkv_proj(qn, kvn, kr, w_uq_p, w_kv_p, cos, sin_a, sin_b)
    o, lse = _flash_fwd(q, k, v)
    late = late_weights(lse)
    w_o_a, w_o_b, w_up, w_down = late["w_o_a"], late["w_o_b"], late["w_up"], late["w_down"]
    w_pg, w_pp = late["w_ple_gate"], late["w_ple_proj"]
    t = _mm(y_conv, w_o_a, add=x, name="mm_o_conv", tm=512, tn=1024, tk=512)
    x1 = _mm(o, w_o_b, add=t, name="mm_o_att", tm=512, tn=1024, tk=1024)
    hf = _rms_fwd(x1, g2, name="rms_ffn")
    a_pre, act = _ffn_fwd(hf, w_up, fconv_w, fconv_b)
    x2 = _mm(act, w_down, add=x1, name="mm_down", tm=512, tn=1024, tk=1408)
    n3 = _rms_fwd(x2, g3, name="rms_ple")
    gl = _mm(n3, w_pg, name="mm_gate", tm=512, tn=1024, tk=1024)
    pp = _mm(p, w_pp, name="mm_ple", tm=512, tn=1024, tk=256)
    loss, dx3, dgl, dpp, d_gf = _ple_final(x2, gl, pp, target, gf)

    grads, early = {"final_norm_g": d_gf}, {}
    early["w_ple_proj"] = _mm(p, dpp, ta=True, name="mm_d_wpp", tm=256, tn=1024, tk=1024)
    early["w_ple_gate"] = _mm(n3, dgl, ta=True, name="mm_d_wpg", tm=512, tn=1024, tk=1024)
    dn3 = _mm(dgl, w_pg, tb=True, name="mm_d_n3", tm=512, tn=1024, tk=1024)
    dx2, grads["ple_norm_g"] = _rms_bwd(x2, g3, dn3, dx3, name="rms_ple_bwd")
    early["w_down"] = _mm(act, dx2, ta=True, name="mm_d_wdown", tm=1408, tn=512, tk=1024)
    da_pre, grads["ffn_conv_w"], grads["ffn_conv_b"] = _ffn_bwd(dx2, w_down, a_pre, fconv_w, fconv_b)
    early["w_up"] = _mm(hf, da_pre, ta=True, b_split=True, name="mm_d_wup", tm=512, tn=1408, tk=1024)
    dhf = _mm(da_pre, w_up, tb=True, a_split=True, name="mm_d_hf", tm=512, tn=1024, tk=1408)
    dx1, grads["norm_ffn_g"] = _rms_bwd(x1, g2, dhf, dx2, name="rms_ffn_bwd")
    d_wo_a = _mm(y_conv, dx1, ta=True, name="mm_d_wo_conv", tm=512, tn=1024, tk=1024)
    d_wo_b = _mm(o, dx1, ta=True, name="mm_d_wo_att", tm=512, tn=1024, tk=1024)
    early["w_o"] = jnp.concatenate([d_wo_a, d_wo_b.reshape(N_HEADS, HEAD_PAD, D_MODEL)[:, :V_HEAD]
                                    .reshape(N_HEADS * V_HEAD, D_MODEL)], axis=0)
    token, finish = reduce_early(early)
    dyc = _mm(dx1, w_o_a, tb=True, name="mm_d_yconv", tm=512, tn=512, tk=1024)
    do = _mm(dx1, w_o_b, tb=True, out_dtype=BF16, name="mm_d_o", tm=512, tn=1024, tk=1024)
    delta = _attn_delta(do, o)
    dq, dk, dv = _flash_bwd(q, q.T, k, v, do, do.T, lse, delta, token)
    reduced_early = finish(dq)
    dq_pre, dkr = _qk_bwd(dq, dk, cos, sin_a, sin_b)
    grads["w_uq_p"] = _mm(qn, dq_pre, ta=True, name="mm_d_wuq", tm=256, tn=1024, tk=1024)
    dqn = _mm(dq_pre, w_uq_p, tb=True, name="mm_d_qn", tm=512, tn=256, tk=1024)
    grads["w_k_p"] = _mm(kvn, dk, ta=True, name="mm_d_wk", tm=128, tn=1024, tk=1024)
    grads["w_v_p"] = _mm(kvn, dv, ta=True, name="mm_d_wv", tm=128, tn=1024, tk=1024)
    dkvn_k = _mm(dk, w_kv_p[:, :D_ATT], tb=True, name="mm_d_kvn_k", tm=512, tn=128, tk=1024)
    dkvn = _mm(dv, w_kv_p[:, D_ATT:], tb=True, add=dkvn_k, name="mm_d_kvn_v", tm=512, tn=128, tk=1024)
    dz, grads["conv_w"], grads["q_norm_g"], grads["kv_norm_g"] = _mix_bwd(
        z, dyc, dqn, dkvn, dkr, conv_w8, gq, gkv, cos, sin_a, sin_b)
    grads["w_in_p"] = _mm(h, dz, ta=True, name="mm_d_win", tm=512, tn=1024, tk=1024)
    dh = _mm(dz, w_in_p, tb=True, name="mm_d_h", tm=512, tn=1024, tk=1024)
    grad_x, grads["norm_mix_g"] = _rms_bwd(x, g1, dh, dx1, name="rms_mix_bwd")
    return loss[0, 0], grad_x, grads, reduced_early


_EARLY_W = ("w_in", "w_uq", "w_ukv")
_LATE_W = ("w_o", "w_up", "w_down", "w_ple_gate", "w_ple_proj")
_BIG = _EARLY_W + _LATE_W
_COL_SHARDED = ("w_in", "w_uq", "w_ukv", "w_up", "w_ple_proj")
_SMALL = ("norm_mix_g", "conv_w", "q_norm_g", "kv_norm_g", "norm_ffn_g", "ffn_conv_w", "ffn_conv_b", "ple_norm_g",
          "final_norm_g")


def _full_from_slots(n, g4):
    return _cols_from_shards(g4) if n in _COL_SHARDED else g4.reshape(-1, g4.shape[2])


def _shard_major(n, g):
    return _cols_to_shards(g) if n in _COL_SHARDED else g.reshape(N_CHIPS, g.shape[0] // N_CHIPS, g.shape[1])


def _early_weights(w):
    shards = [w[n][0].astype(BF16) for n in _EARLY_W]
    shards.append(jnp.pad(w["conv_w"][0], ((0, 5), (0, 0))))
    shards.append(jnp.pad(w["ffn_conv_w"][0], ((0, 5), (0, 0))))
    got = _gather_chips(shards)
    full = {n: _full_from_slots(n, g4) for n, g4 in zip(_EARLY_W, got)}
    full["conv_w8"] = _cols_from_shards(got[len(_EARLY_W)])
    full["ffn_conv_w8"] = _cols_from_shards(got[len(_EARLY_W) + 1])
    return _layout_early(full, w)


def _layout_early(full, w):
    out = {n: w[n] for n in ("norm_mix_g", "q_norm_g", "kv_norm_g", "norm_ffn_g", "ple_norm_g")}
    out["final_norm_g"] = w["final_norm_g"][None, :]
    w_in = full["w_in"]
    zc = jnp.zeros((D_MODEL, QK_NOPE), BF16)
    zt = jnp.zeros((D_MODEL, HEAD_PAD - QK_NOPE - QK_ROPE), BF16)
    out["w_in_p"] = jnp.concatenate([w_in[:, :D_IN - QK_ROPE], zc, w_in[:, D_IN - QK_ROPE:], zt], axis=1)
    out["w_uq_p"] = _pad_heads(full["w_uq"], QK_NOPE + QK_ROPE)
    kv3 = full["w_ukv"].reshape(KV_LORA, N_HEADS, QK_NOPE + V_HEAD)
    out["w_kv_p"] = jnp.concatenate([_pad_heads(kv3[:, :, :QK_NOPE].reshape(KV_LORA, -1), QK_NOPE),
                                     _pad_heads(kv3[:, :, QK_NOPE:].reshape(KV_LORA, -1), V_HEAD)], axis=1)
    out["conv_w8"] = full["conv_w8"]
    fw = full["ffn_conv_w8"]
    out["ffn_conv_w"] = jnp.stack([fw[:, :D_FF], fw[:, D_FF:]])
    out["ffn_conv_b"] = w["ffn_conv_b"].reshape(2, 1, D_FF)
    return out


def _layout_late(full):
    w_o = full["w_o"]
    out = {"w_o_a": w_o[:CONV_WIDTH]}
    out["w_o_b"] = jnp.pad(w_o[CONV_WIDTH:].reshape(N_HEADS, V_HEAD, D_MODEL),
                           ((0, 0), (0, HEAD_PAD - V_HEAD), (0, 0))).reshape(D_ATT, D_MODEL)
    for n in ("w_up", "w_down", "w_ple_gate", "w_ple_proj"):
        out[n] = full[n]
    return out


def _true_gradients(g):
    out = {}
    wp = g["w_in_p"]
    out["w_in"] = jnp.concatenate([wp[:, :D_IN - QK_ROPE], wp[:, D_IN_PAD - HEAD_PAD + QK_NOPE:
                                                              D_IN_PAD - HEAD_PAD + QK_NOPE + QK_ROPE]], axis=1)
    out["w_uq"] = _unpad_heads(g["w_uq_p"], QK_NOPE + QK_ROPE).reshape(Q_LORA, -1)
    out["w_ukv"] = jnp.concatenate([_unpad_heads(g["w_k_p"], QK_NOPE), _unpad_heads(g["w_v_p"], V_HEAD)],
                                   axis=2).reshape(KV_LORA, -1)
    out["conv_w"] = g["conv_w"]
    fw = g["ffn_conv_w"]
    out["ffn_conv_w"] = jnp.concatenate([fw[0, :3], fw[1, :3]], axis=1)
    out["ffn_conv_b"] = g["ffn_conv_b"].reshape(1, 2 * D_FF)
    for n in ("norm_mix_g", "q_norm_g", "kv_norm_g", "norm_ffn_g", "ple_norm_g", "final_norm_g"):
        out[n] = g[n]
    return out


def _chip_partials(names, g, core, *, tag):
    g4 = [_shard_major(n, g[n]) for n in names]
    sib = _send_other_halves(g4, tag=tag)
    return [_add_own_half(a, b, core, name="add_cores_" + n) for n, a, b in zip(names, g4, sib)]


_SMALL_SIZES = {"norm_mix_g": D_MODEL, "conv_w": 3 * CONV_WIDTH, "q_norm_g": Q_LORA, "kv_norm_g": KV_LORA,
                "norm_ffn_g": D_MODEL, "ffn_conv_w": 6 * D_FF, "ffn_conv_b": 2 * D_FF, "ple_norm_g": D_MODEL,
                "final_norm_g": D_MODEL}


def _pack(parts, rows):
    flat = jnp.concatenate([a.reshape(-1) for a in parts])
    return jnp.pad(flat, (0, rows * 128 - flat.shape[0])).reshape(rows, 128)


def _unpack(buf, sizes):
    flat = buf.reshape(-1)
    out, at = [], 0
    for n in sizes:
        out.append(flat[at:at + n])
        at += n
    return out


def _reduce_small(g):
    total = sum(_SMALL_SIZES[n] for n in _SMALL)
    rows = -(-total // 1024) * 8
    slots = _gather_all(_pack([g[n] for n in _SMALL], rows))
    summed = _sum_slots(slots, name="sum_small")
    return dict(zip(_SMALL, _unpack(summed, [_SMALL_SIZES[n] for n in _SMALL])))


def kernel(x, p, norm_mix_g, w_in, conv_w, q_norm_g, w_uq, kv_norm_g, w_ukv, w_o, norm_ffn_g, w_up, ffn_conv_w, ffn_conv_b, w_down, ple_norm_g, w_ple_gate, w_ple_proj, final_norm_g, loss_target, m_norm_mix_g, m_w_in, m_conv_w, m_q_norm_g, m_w_uq, m_kv_norm_g, m_w_ukv, m_w_o, m_norm_ffn_g, m_w_up, m_ffn_conv_w, m_ffn_conv_b, m_w_down, m_ple_norm_g, m_w_ple_gate, m_w_ple_proj, m_final_norm_g, v_norm_mix_g, v_w_in, v_conv_w, v_q_norm_g, v_w_uq, v_kv_norm_g, v_w_ukv, v_w_o, v_norm_ffn_g, v_w_up, v_ffn_conv_w, v_ffn_conv_b, v_w_down, v_ple_norm_g, v_w_ple_gate, v_w_ple_proj, v_final_norm_g):
    names = ["norm_mix_g", "w_in", "conv_w", "q_norm_g", "w_uq", "kv_norm_g", "w_ukv", "w_o", "norm_ffn_g", "w_up",
             "ffn_conv_w", "ffn_conv_b", "w_down", "ple_norm_g", "w_ple_gate", "w_ple_proj", "final_norm_g"]
    w = dict(zip(names, (norm_mix_g, w_in, conv_w, q_norm_g, w_uq, kv_norm_g, w_ukv, w_o, norm_ffn_g, w_up,
                         ffn_conv_w, ffn_conv_b, w_down, ple_norm_g, w_ple_gate, w_ple_proj, final_norm_g)))
    m = dict(zip(names, (m_norm_mix_g, m_w_in, m_conv_w, m_q_norm_g, m_w_uq, m_kv_norm_g, m_w_ukv, m_w_o,
                         m_norm_ffn_g, m_w_up, m_ffn_conv_w, m_ffn_conv_b, m_w_down, m_ple_norm_g, m_w_ple_gate,
                         m_w_ple_proj, m_final_norm_g)))
    v = dict(zip(names, (v_norm_mix_g, v_w_in, v_conv_w, v_q_norm_g, v_w_uq, v_kv_norm_g, v_w_ukv, v_w_o,
                         v_norm_ffn_g, v_w_up, v_ffn_conv_w, v_ffn_conv_b, v_w_down, v_ple_norm_g, v_w_ple_gate,
                         v_w_ple_proj, v_final_norm_g)))

    core = lax.axis_index("c").astype(jnp.int32).reshape(1)
    chip = (2 * lax.axis_index("x") + lax.axis_index("y")).astype(jnp.int32).reshape(1)

    wts = _early_weights(w)
    gather = _chips_start([w[n][0].astype(BF16) for n in _LATE_W], scatter=False, name="gather_late_start")
    wts["norm_mix_g"] = wts["norm_mix_g"] + gather[4][0, 0]

    def late_weights(after):
        shards, landed = _chips_wait(gather, after, scatter=False, name="gather_late_wait")
        full = {n: _full_from_slots(n, lax.dynamic_update_slice(g4, own[None], (chip[0], 0, 0)))
                for n, own, g4 in zip(_LATE_W, shards, landed)}
        return _layout_late(full)

    def reduce_early(g):
        parts = _chip_partials(_LATE_W, g, core, tag="early")
        scatter = _chips_start([t16 for _, t16 in parts], scatter=True, name="scatter_early_start")

        def finish(after):
            _, landed = _chips_wait(scatter, after, scatter=True, name="scatter_early_wait")
            return [_sum_chips(a, t32, chip, name="sum_chips_" + n) for n, a, (t32, _) in zip(_LATE_W, landed, parts)]

        return scatter[4], finish

    loss, grad_x, padded, halves_early = _local_step(x[0], p[0, 0], loss_target[0], wts, late_weights, reduce_early)
    g_full = _true_gradients(padded)
    loss = lax.psum(loss, ("x", "y", "c"))

    parts = _chip_partials(_EARLY_W, g_full, core, tag="late")
    landed = _scatter_chips([t16 for _, t16 in parts])
    halves = [_sum_chips(a, t32, chip, name="sum_chips_" + n) for n, a, (t32, _) in zip(_EARLY_W, landed, parts)]
    whole = _join_halves(halves + halves_early)
    big = {n: a.reshape(-1, a.shape[2]) for n, a in zip(_BIG, whole)}

    g_out, d_out, m_out, v_out = {}, {}, {}, {}
    for n in _BIG:
        shape = w[n].shape
        g = big[n]
        d, mn, vn = _adamw(w[n][0], g, m[n][0], v[n][0], name="adamw_" + n)
        g_out[n], d_out[n], m_out[n], v_out[n] = (a.reshape(shape) for a in (g, d, mn, vn))

    small = _reduce_small(g_full)
    chip = 2 * lax.axis_index("x") + lax.axis_index("y")
    g_small = {}
    for n in _SMALL:
        shape = w[n].shape
        g = small[n]
        if n in ("conv_w", "ffn_conv_w"):
            width = shape[-1]
            g = lax.dynamic_slice(g.reshape(3, N_CHIPS * width), (0, chip * width), (3, width))
        g_small[n] = g.reshape(shape)
    sizes = [g_small[n].size for n in _SMALL]
    rows = -(-sum(sizes) // 1024) * 8
    packed = [_pack([src[n] for n in _SMALL], rows) for src in (w, g_small, m, v)]
    d_s, m_s, v_s = _adamw(*packed, name="adamw_small")
    for n, d, mn, vn in zip(_SMALL, _unpack(d_s, sizes), _unpack(m_s, sizes), _unpack(v_s, sizes)):
        shape = w[n].shape
        g_out[n], d_out[n], m_out[n], v_out[n] = g_small[n], d.reshape(shape), mn.reshape(shape), vn.reshape(shape)

    return (loss, grad_x[None], *[g_out[n] for n in names], *[d_out[n] for n in names],
            *[m_out[n] for n in names], *[v_out[n] for n in names])
```

```python
import functools

import jax
import jax.numpy as jnp
from jax import lax
from jax.experimental import pallas as pl
from jax.experimental.pallas import tpu as pltpu

F32 = jnp.float32
BF16 = jnp.bfloat16

D_MODEL = 1024
CONV_WIDTH = 512
Q_LORA = 256
KV_LORA = 128
QK_NOPE = 64
QK_ROPE = 32
V_HEAD = 64
N_HEADS = 8
HEAD_PAD = 128
D_ATT = N_HEADS * HEAD_PAD
D_IN = 3 * CONV_WIDTH + Q_LORA + KV_LORA + QK_ROPE
D_IN_PAD = 3 * CONV_WIDTH + Q_LORA + KV_LORA + HEAD_PAD
D_FF = 2816
ROPE_THETA = 10000.0
EPS = 1e-6
SM_SCALE = (QK_NOPE + QK_ROPE) ** -0.5
ONES_LANE = V_HEAD

ADAM_LR = 0.001
ADAM_B1 = 0.9
ADAM_B2 = 0.999
ADAM_EPS = 1e-08
ADAM_WD = 0.01
ADAM_STEP = 10

N_CHIPS = 4
N_DEV = 8
MESH = pl.DeviceIdType.MESH
ANY = pl.BlockSpec(memory_space=pl.ANY)


def _params(sem):
    return pltpu.CompilerParams(dimension_semantics=sem)


def _mm(a, b, *, name, ta=False, tb=False, add=None, out_dtype=F32, tm=512, tn=512, tk=512,
        a_split=False, b_split=False, o_split=False):
    if a_split:
        _, m, kh = a.shape
        k = 2 * kh
    elif ta:
        k, m = a.shape
    else:
        m, k = a.shape
    if b_split:
        _, kb, nh = b.shape
        n = 2 * nh
    elif tb:
        n, kb = b.shape
    else:
        kb, n = b.shape
    assert kb == k, (name, a.shape, b.shape)
    tm, tn, tk = min(tm, m), min(tn, n), min(tk, k)
    assert m % tm == 0 and n % tn == 0 and k % tk == 0, (name, m, n, k, tm, tn, tk)
    gm, gn, gk = m // tm, n // tn, k // tk

    if a_split:
        assert gk % 2 == 0
        a_spec = pl.BlockSpec((None, tm, tk), lambda i, j, kk: (kk // (gk // 2), i, kk % (gk // 2)))
    elif ta:
        a_spec = pl.BlockSpec((tk, tm), lambda i, j, kk: (kk, i))
    else:
        a_spec = pl.BlockSpec((tm, tk), lambda i, j, kk: (i, kk))
    if b_split:
        assert gn % 2 == 0
        b_spec = pl.BlockSpec((None, tk, tn), lambda i, j, kk: (j // (gn // 2), kk, j % (gn // 2)))
    elif tb:
        b_spec = pl.BlockSpec((tn, tk), lambda i, j, kk: (j, kk))
    else:
        b_spec = pl.BlockSpec((tk, tn), lambda i, j, kk: (kk, j))
    if o_split:
        assert gn % 2 == 0
        o_spec = pl.BlockSpec((None, tm, tn), lambda i, j, kk: (j // (gn // 2), i, j % (gn // 2)))
        o_shape = jax.ShapeDtypeStruct((2, m, n // 2), out_dtype)
    else:
        o_spec = pl.BlockSpec((tm, tn), lambda i, j, kk: (i, j))
        o_shape = jax.ShapeDtypeStruct((m, n), out_dtype)
    dims = (((0 if ta else 1,), (1 if tb else 0,)), ((), ()))

    def body(*refs):
        a_ref, b_ref = refs[:2]
        add_ref = None if add is None else refs[2]
        o_ref = refs[2 if add is None else 3]

        def finish(r):
            if add_ref is not None:
                r = r + add_ref[...]
            o_ref[...] = r.astype(o_ref.dtype)

        part = lax.dot_general(a_ref[...].astype(BF16), b_ref[...].astype(BF16), dims, preferred_element_type=F32)
        if gk == 1:
            finish(part)
            return
        acc_ref = refs[-1]
        kk = pl.program_id(2)

        @pl.when(kk == 0)
        def _():
            acc_ref[...] = part

        @pl.when((kk > 0) & (kk < gk - 1))
        def _():
            acc_ref[...] += part

        @pl.when(kk == gk - 1)
        def _():
            finish(acc_ref[...] + part)

    in_specs = [a_spec, b_spec]
    args = [a, b]
    if add is not None:
        in_specs.append(pl.BlockSpec((tm, tn), lambda i, j, kk: (i, j)))
        args.append(add)
    return pl.pallas_call(
        body, name=name, grid=(gm, gn, gk), in_specs=in_specs, out_specs=o_spec, out_shape=o_shape,
        scratch_shapes=[] if gk == 1 else [pltpu.VMEM((tm, tn), F32)],
        compiler_params=_params(("parallel", "parallel", "arbitrary")),
    )(*args)


def _rms_scale(v):
    return lax.rsqrt(jnp.mean(v * v, axis=-1, keepdims=True) + EPS)


def _rms_bwd_rows(v, g, dy):
    r = _rms_scale(v)
    vh = v * r
    dyg = dy * g
    dv = r * (dyg - vh * jnp.mean(dyg * vh, axis=-1, keepdims=True))
    return dv, dy * vh


def _shift_down(v, first_row):
    row = lax.broadcasted_iota(jnp.int32, v.shape, 0)
    return jnp.where(row == 0, first_row, pltpu.roll(v, 1, 0))


def _shift_up(v, last_row):
    n = v.shape[0]
    row = lax.broadcasted_iota(jnp.int32, v.shape, 0)
    return jnp.where(row == n - 1, last_row, pltpu.roll(v, n - 1, 0))


def _rope(t, cos, sin_a, sin_b):
    return t * cos + pltpu.roll(t, HEAD_PAD - 16, 1) * sin_a + pltpu.roll(t, 16, 1) * sin_b


def _rope_bwd(d, cos, sin_a, sin_b):
    return d * cos + pltpu.roll(d * sin_a, 16, 1) + pltpu.roll(d * sin_b, HEAD_PAD - 16, 1)


def _sigmoid(v):
    return 1.0 / (1.0 + jnp.exp(-v))


def _halo_specs(ts, s, width, col):
    nb = ts // 8
    last = s // 8 - 1
    prev = pl.BlockSpec((8, width), lambda i: (jnp.maximum(i * nb - 1, 0), col))
    nxt = pl.BlockSpec((8, width), lambda i: (jnp.minimum((i + 1) * nb, last), col))
    return prev, nxt


def _rms_fwd(x, g, *, name, ts=512):
    s, d = x.shape

    def body(x_ref, g_ref, h_ref):
        v = x_ref[...]
        h_ref[...] = (v * _rms_scale(v) * g_ref[...]).astype(h_ref.dtype)

    return pl.pallas_call(
        body, name=name, grid=(s // ts,),
        in_specs=[pl.BlockSpec((ts, d), lambda i: (i, 0)), pl.BlockSpec((1, d), lambda i: (0, 0))],
        out_specs=pl.BlockSpec((ts, d), lambda i: (i, 0)),
        out_shape=jax.ShapeDtypeStruct((s, d), BF16),
        compiler_params=_params(("parallel",)),
    )(x, g)


def _rms_bwd(x, g, dy, add, *, name, ts=256):
    s, d = x.shape

    def body(x_ref, g_ref, dy_ref, add_ref, dx_ref, dg_ref):
        @pl.when(pl.program_id(0) == 0)
        def _():
            dg_ref[...] = jnp.zeros_like(dg_ref)

        dv, dgr = _rms_bwd_rows(x_ref[...], g_ref[...], dy_ref[...])
        dx_ref[...] = dv + add_ref[...]
        dg_ref[...] += jnp.sum(dgr, axis=0, keepdims=True)

    row = pl.BlockSpec((ts, d), lambda i: (i, 0))
    vec = pl.BlockSpec((1, d), lambda i: (0, 0))
    return pl.pallas_call(
        body, name=name, grid=(s // ts,),
        in_specs=[row, vec, row, row], out_specs=[row, vec],
        out_shape=[jax.ShapeDtypeStruct((s, d), F32), jax.ShapeDtypeStruct((1, d), F32)],
        compiler_params=_params(("arbitrary",)),
    )(x, g, dy, add)


def _mix_pre(z, conv_w8, gq, gkv, cos, sin_a, sin_b, *, ts=256):
    s = z.shape[0]
    n = s // ts
    cw = CONV_WIDTH

    def body(z_ref, xcp, xcn, cgp, cgn, w_ref, gq_ref, gkv_ref, cos_ref, sa_ref, sb_ref,
             yc_ref, qn_ref, kvn_ref, kr_ref):
        i = pl.program_id(0)
        xc = z_ref[:, 0:cw]
        bg = z_ref[:, cw:2 * cw]
        cg = z_ref[:, 2 * cw:3 * cw]
        m = cg * xc
        m_prev = jnp.where(i > 0, xcp[7:8, :] * cgp[7:8, :], 0.0)
        m_next = jnp.where(i < n - 1, xcn[0:1, :] * cgn[0:1, :], 0.0)
        cm = _shift_down(m, m_prev) * w_ref[0:1, :] + m * w_ref[1:2, :] + _shift_up(m, m_next) * w_ref[2:3, :]
        yc_ref[...] = (bg * cm).astype(BF16)
        ql = z_ref[:, 3 * cw:3 * cw + Q_LORA]
        qn_ref[...] = (ql * _rms_scale(ql) * gq_ref[...]).astype(BF16)
        kvl = z_ref[:, 3 * cw + Q_LORA:3 * cw + Q_LORA + KV_LORA]
        kvn_ref[...] = (kvl * _rms_scale(kvl) * gkv_ref[...]).astype(BF16)
        kr_ref[...] = _rope(z_ref[:, D_IN_PAD - HEAD_PAD:D_IN_PAD], cos_ref[...], sa_ref[...], sb_ref[...])

    xcp, xcn = _halo_specs(ts, s, cw, 0)
    cgp, cgn = _halo_specs(ts, s, cw, 2)
    tab = pl.BlockSpec((ts, HEAD_PAD), lambda i: (i, 0))
    return pl.pallas_call(
        body, name="mix_pre", grid=(n,),
        in_specs=[pl.BlockSpec((ts, D_IN_PAD), lambda i: (i, 0)), xcp, xcn, cgp, cgn,
                  pl.BlockSpec((8, cw), lambda i: (0, 0)), pl.BlockSpec((1, Q_LORA), lambda i: (0, 0)),
                  pl.BlockSpec((1, KV_LORA), lambda i: (0, 0)), tab, tab, tab],
        out_specs=[pl.BlockSpec((ts, cw), lambda i: (i, 0)), pl.BlockSpec((ts, Q_LORA), lambda i: (i, 0)),
                   pl.BlockSpec((ts, KV_LORA), lambda i: (i, 0)), tab],
        out_shape=[jax.ShapeDtypeStruct((s, cw), BF16), jax.ShapeDtypeStruct((s, Q_LORA), BF16),
                   jax.ShapeDtypeStruct((s, KV_LORA), BF16), jax.ShapeDtypeStruct((s, HEAD_PAD), F32)],
        compiler_params=_params(("parallel",)),
    )(z, z, z, z, z, conv_w8, gq, gkv, cos, sin_a, sin_b)


def _mix_bwd(z, dyc, dqn, dkvn, dkr, conv_w8, gq, gkv, cos, sin_a, sin_b, *, ts=256):
    s = z.shape[0]
    n = s // ts
    cw = CONV_WIDTH

    def body(z_ref, xcp, xcn, bgp, bgn, cgp, cgn, dyc_ref, dycp, dycn, dqn_ref, dkvn_ref, dkr_ref,
             w_ref, gq_ref, gkv_ref, cos_ref, sa_ref, sb_ref,
             dz_ref, dw0_ref, dw1_ref, dw2_ref, dgq_ref, dgkv_ref):
        i = pl.program_id(0)

        @pl.when(i == 0)
        def _():
            for r in (dw0_ref, dw1_ref, dw2_ref, dgq_ref, dgkv_ref):
                r[...] = jnp.zeros_like(r)

        xc = z_ref[:, 0:cw]
        bg = z_ref[:, cw:2 * cw]
        cg = z_ref[:, 2 * cw:3 * cw]
        w0, w1, w2 = w_ref[0:1, :], w_ref[1:2, :], w_ref[2:3, :]
        m = cg * xc
        m_dn = _shift_down(m, jnp.where(i > 0, xcp[7:8, :] * cgp[7:8, :], 0.0))
        m_up = _shift_up(m, jnp.where(i < n - 1, xcn[0:1, :] * cgn[0:1, :], 0.0))
        cm = m_dn * w0 + m * w1 + m_up * w2
        dyc_v = dyc_ref[...]
        dcm = dyc_v * bg
        dcm_dn = _shift_down(dcm, jnp.where(i > 0, dycp[7:8, :] * bgp[7:8, :], 0.0))
        dcm_up = _shift_up(dcm, jnp.where(i < n - 1, dycn[0:1, :] * bgn[0:1, :], 0.0))
        dm = dcm_up * w0 + dcm * w1 + dcm_dn * w2
        dz_ref[:, 0:cw] = (dm * cg).astype(BF16)
        dz_ref[:, cw:2 * cw] = (dyc_v * cm).astype(BF16)
        dz_ref[:, 2 * cw:3 * cw] = (dm * xc).astype(BF16)
        dw0_ref[...] += jnp.sum(dcm * m_dn, axis=0, keepdims=True)
        dw1_ref[...] += jnp.sum(dcm * m, axis=0, keepdims=True)
        dw2_ref[...] += jnp.sum(dcm * m_up, axis=0, keepdims=True)

        dql, dgq_rows = _rms_bwd_rows(z_ref[:, 3 * cw:3 * cw + Q_LORA], gq_ref[...], dqn_ref[...])
        dz_ref[:, 3 * cw:3 * cw + Q_LORA] = dql.astype(BF16)
        dgq_ref[...] += jnp.sum(dgq_rows, axis=0, keepdims=True)
        dkvl, dgkv_rows = _rms_bwd_rows(z_ref[:, 3 * cw + Q_LORA:3 * cw + Q_LORA + KV_LORA], gkv_ref[...],
                                        dkvn_ref[...])
        dz_ref[:, 3 * cw + Q_LORA:3 * cw + Q_LORA + KV_LORA] = dkvl.astype(BF16)
        dgkv_ref[...] += jnp.sum(dgkv_rows, axis=0, keepdims=True)

        lane = lax.broadcasted_iota(jnp.int32, (ts, HEAD_PAD), 1)
        rope_lane = (lane >= QK_NOPE) & (lane < QK_NOPE + QK_ROPE)
        dk = _rope_bwd(dkr_ref[...], cos_ref[...], sa_ref[...], sb_ref[...])
        dz_ref[:, D_IN_PAD - HEAD_PAD:D_IN_PAD] = jnp.where(rope_lane, dk, 0.0).astype(BF16)

    xcp, xcn = _halo_specs(ts, s, cw, 0)
    bgp, bgn = _halo_specs(ts, s, cw, 1)
    cgp, cgn = _halo_specs(ts, s, cw, 2)
    dycp, dycn = _halo_specs(ts, s, cw, 0)
    tab = pl.BlockSpec((ts, HEAD_PAD), lambda i: (i, 0))

    def vec(width):
        return pl.BlockSpec((1, width), lambda i: (0, 0))

    outs = pl.pallas_call(
        body, name="mix_bwd", grid=(n,),
        in_specs=[pl.BlockSpec((ts, D_IN_PAD), lambda i: (i, 0)), xcp, xcn, bgp, bgn, cgp, cgn,
                  pl.BlockSpec((ts, cw), lambda i: (i, 0)), dycp, dycn,
                  pl.BlockSpec((ts, Q_LORA), lambda i: (i, 0)), pl.BlockSpec((ts, KV_LORA), lambda i: (i, 0)), tab,
                  pl.BlockSpec((8, cw), lambda i: (0, 0)), vec(Q_LORA), vec(KV_LORA), tab, tab, tab],
        out_specs=[pl.BlockSpec((ts, D_IN_PAD), lambda i: (i, 0)), vec(cw), vec(cw), vec(cw), vec(Q_LORA),
                   vec(KV_LORA)],
        out_shape=[jax.ShapeDtypeStruct((s, D_IN_PAD), BF16)] + [jax.ShapeDtypeStruct((1, cw), F32)] * 3
        + [jax.ShapeDtypeStruct((1, Q_LORA), F32), jax.ShapeDtypeStruct((1, KV_LORA), F32)],
        compiler_params=_params(("arbitrary",)),
    )(z, z, z, z, z, z, z, dyc, dyc, dyc, dqn, dkvn, dkr, conv_w8, gq, gkv, cos, sin_a, sin_b)
    dz, dw0, dw1, dw2, dgq, dgkv = outs
    return dz, jnp.concatenate([dw0, dw1, dw2], axis=0), dgq, dgkv


def _qkv_proj(qn, kvn, kr, w_uq_p, w_kv_p, cos, sin_a, sin_b, *, ts=512):
    s = qn.shape[0]

    def body(qn_ref, kvn_ref, kr_ref, wq_ref, wkv_ref, cos_ref, sa_ref, sb_ref, q_ref, k_ref, v_ref):
        cos_v, sa, sb = cos_ref[...], sa_ref[...], sb_ref[...]
        q = jnp.dot(qn_ref[...], wq_ref[...], preferred_element_type=F32)
        kv = jnp.dot(kvn_ref[...], wkv_ref[...], preferred_element_type=F32)
        kr_v = kr_ref[...]
        lane = lax.broadcasted_iota(jnp.int32, (1, HEAD_PAD), 1)
        ones_lane = (lane == ONES_LANE).astype(F32)
        for h in range(N_HEADS):
            blk = slice(h * HEAD_PAD, (h + 1) * HEAD_PAD)
            q_ref[:, blk] = (_rope(q[:, blk], cos_v, sa, sb) * SM_SCALE).astype(BF16)
            k_ref[:, blk] = (kv[:, blk] + kr_v).astype(BF16)
            v_ref[:, blk] = (kv[:, D_ATT + h * HEAD_PAD:D_ATT + (h + 1) * HEAD_PAD] + ones_lane).astype(BF16)

    tab = pl.BlockSpec((ts, HEAD_PAD), lambda i: (i, 0))
    wide = pl.BlockSpec((ts, D_ATT), lambda i: (i, 0))
    return pl.pallas_call(
        body, name="qkv_proj", grid=(s // ts,),
        in_specs=[pl.BlockSpec((ts, Q_LORA), lambda i: (i, 0)), pl.BlockSpec((ts, KV_LORA), lambda i: (i, 0)), tab,
                  pl.BlockSpec((Q_LORA, D_ATT), lambda i: (0, 0)), pl.BlockSpec((KV_LORA, 2 * D_ATT), lambda i: (0, 0)),
                  tab, tab, tab],
        out_specs=[wide, wide, wide],
        out_shape=[jax.ShapeDtypeStruct((s, D_ATT), BF16)] * 3,
        compiler_params=_params(("parallel",)),
    )(qn, kvn, kr, w_uq_p, w_kv_p, cos, sin_a, sin_b)


def _qk_bwd(dq, dk, cos, sin_a, sin_b, *, ts=256):
    s = dq.shape[0]

    def body(dq_ref, dk_ref, cos_ref, sa_ref, sb_ref, dqp_ref, dkr_ref):
        cos_v, sa, sb = cos_ref[...], sa_ref[...], sb_ref[...]
        tot = jnp.zeros((ts, HEAD_PAD), F32)
        for h in range(N_HEADS):
            blk = slice(h * HEAD_PAD, (h + 1) * HEAD_PAD)
            dqp_ref[:, blk] = _rope_bwd(dq_ref[:, blk], cos_v, sa, sb).astype(BF16)
            tot = tot + dk_ref[:, blk]
        dkr_ref[...] = tot

    tab = pl.BlockSpec((ts, HEAD_PAD), lambda i: (i, 0))
    wide = pl.BlockSpec((ts, D_ATT), lambda i: (i, 0))
    return pl.pallas_call(
        body, name="qk_bwd", grid=(s // ts,),
        in_specs=[wide, wide, tab, tab, tab], out_specs=[wide, tab],
        out_shape=[jax.ShapeDtypeStruct((s, D_ATT), BF16), jax.ShapeDtypeStruct((s, HEAD_PAD), F32)],
        compiler_params=_params(("parallel",)),
    )(dq, dk, cos, sin_a, sin_b)


_NT = (((1,), (1,)), ((), ()))
_TN = (((0,), (0,)), ((), ()))


def _flash_fwd(q, k, v, *, tq=1024, tk=1024):
    s = q.shape[0]
    tq, tk = min(tq, s), min(tk, s)
    nk = s // tk

    def body(q_ref, k_ref, v_ref, o_ref, lse_ref):
        qv = q_ref[...]

        def step(j, carry):
            m, acc = carry
            rows = pl.ds(pl.multiple_of(j * tk, tk), tk)
            sc = lax.dot_general(qv, k_ref[rows, :], _NT, preferred_element_type=F32)
            m_new = jnp.maximum(m, jnp.max(sc, axis=1, keepdims=True))
            p = jnp.exp(sc - m_new).astype(BF16)
            acc = jnp.exp(m - m_new) * acc + jnp.dot(p, v_ref[rows, :], preferred_element_type=F32)
            return m_new, acc

        init = (jnp.full((tq, 1), -jnp.inf, F32), jnp.zeros((tq, HEAD_PAD), F32))
        m, acc = lax.fori_loop(0, nk, step, init)
        l = acc[:, ONES_LANE:ONES_LANE + 1]
        o_ref[...] = (acc / l).astype(BF16)
        lse_ref[...] = m + jnp.log(l)

    head = pl.BlockSpec((s, HEAD_PAD), lambda h, i: (0, h))
    return pl.pallas_call(
        body, name="flash_fwd", grid=(N_HEADS, s // tq),
        in_specs=[pl.BlockSpec((tq, HEAD_PAD), lambda h, i: (i, h)), head, head],
        out_specs=[pl.BlockSpec((tq, HEAD_PAD), lambda h, i: (i, h)),
                   pl.BlockSpec((None, tq, 1), lambda h, i: (h, i, 0))],
        out_shape=[jax.ShapeDtypeStruct((s, D_ATT), BF16), jax.ShapeDtypeStruct((N_HEADS, s, 1), F32)],
        compiler_params=_params(("parallel", "parallel")),
    )(q, k, v)


def _attn_delta(do, o, *, ts=512):
    s = do.shape[0]

    def body(do_ref, o_ref, dl_ref):
        for h in range(N_HEADS):
            blk = slice(h * HEAD_PAD, (h + 1) * HEAD_PAD)
            dl_ref[h] = jnp.sum(do_ref[:, blk].astype(F32) * o_ref[:, blk].astype(F32), axis=1, keepdims=True)

    wide = pl.BlockSpec((ts, D_ATT), lambda i: (i, 0))
    return pl.pallas_call(
        body, name="attn_delta", grid=(s // ts,), in_specs=[wide, wide],
        out_specs=pl.BlockSpec((N_HEADS, ts, 1), lambda i: (0, i, 0)),
        out_shape=jax.ShapeDtypeStruct((N_HEADS, s, 1), F32),
        compiler_params=_params(("parallel",)),
    )(do, o)


def _flash_bwd(q, qt, k, v, do, dot, lse, delta, after, *, tq=1024, tk=512):
    s = q.shape[0]
    tq, tk = min(tq, s), min(tk, s)
    nq = s // tq

    def body(q_ref, qt_ref, do_ref, dot_ref, lse_ref, dl_ref, k_ref, v_ref, after_ref, dq_ref, dk_ref, dv_ref):
        j = pl.program_id(1)

        @pl.when(j == 0)
        def _():
            dq_ref[...] = jnp.zeros_like(dq_ref)

        kv, vv = k_ref[...], v_ref[...]

        def step(i, carry):
            dk_t, dv_t = carry
            at = pl.multiple_of(i * tq, tq)
            rows = pl.ds(at, tq)
            sc = lax.dot_general(q_ref[rows, :], kv, _NT, preferred_element_type=F32)
            p = jnp.exp(sc - lse_ref[rows, :])
            dp = lax.dot_general(do_ref[rows, :], vv, _NT, preferred_element_type=F32)
            ds = (p * (dp - dl_ref[rows, :])).astype(BF16)
            dv_t = dv_t + jnp.dot(dot_ref[:, rows], p.astype(BF16), preferred_element_type=F32)
            dk_t = dk_t + jnp.dot(qt_ref[:, rows], ds, preferred_element_type=F32)
            dq_ref[rows, :] += jnp.dot(ds, kv, preferred_element_type=F32)
            return dk_t, dv_t

        zero = jnp.zeros((HEAD_PAD, tk), F32)
        dk_t, dv_t = lax.fori_loop(0, nq, step, (zero, zero))
        dk_ref[...] = dk_t.T
        dv_ref[...] = dv_t.T

        @pl.when(j == pl.num_programs(1) - 1)
        def _():
            dq_ref[...] *= SM_SCALE

    head = pl.BlockSpec((s, HEAD_PAD), lambda h, j: (0, h))
    head_t = pl.BlockSpec((HEAD_PAD, s), lambda h, j: (h, 0))
    stat = pl.BlockSpec((None, s, 1), lambda h, j: (h, 0, 0))
    blk = pl.BlockSpec((tk, HEAD_PAD), lambda h, j: (j, h))
    return pl.pallas_call(
        body, name="flash_bwd", grid=(N_HEADS, s // tk),
        in_specs=[head, head_t, head, head_t, stat, stat, blk, blk, ANY],
        out_specs=[head, blk, blk],
        out_shape=[jax.ShapeDtypeStruct((s, D_ATT), F32)] * 3,
        compiler_params=_params(("parallel", "arbitrary")),
    )(q, qt, do, dot, lse, delta, k, v, after)


FFN_TC = 256


FFN_HALO_BF16 = 16
FFN_HALO_F32 = 8


def _row_halo_specs(ts, s, halo, width):
    nb = ts // halo
    last = s // halo - 1
    prev = pl.BlockSpec((halo, width), lambda i, j: (jnp.maximum(i * nb - 1, 0), 0))
    nxt = pl.BlockSpec((halo, width), lambda i, j: (jnp.minimum((i + 1) * nb, last), 0))
    return prev, nxt


def _ext_rows(prev, main, nxt, first, last):
    return jnp.concatenate([jnp.where(first, jnp.zeros_like(prev), prev), main,
                            jnp.where(last, jnp.zeros_like(nxt), nxt)], axis=0)


def _ext_conv(a, w):
    a_dn = pltpu.roll(a, 1, 0)
    a_up = pltpu.roll(a, a.shape[0] - 1, 0)
    return a_dn * w[0:1, :] + a * w[1:2, :] + a_up * w[2:3, :], a_dn, a_up


def _ffn_fwd(hf, w_up, w, b, *, ts=512, tc=FFN_TC):
    s = hf.shape[0]
    n, nj, halo = s // ts, D_FF // tc, FFN_HALO_BF16

    def body(h_ref, hp_ref, hn_ref, wg_ref, wu_ref, cw_ref, cb_ref, a_ref, act_ref):
        i = pl.program_id(0)
        ext = _ext_rows(hp_ref[...], h_ref[...], hn_ref[...], i == 0, i == n - 1)
        gate_up = []
        for half, w_ref in enumerate((wg_ref, wu_ref)):
            a_ext = jnp.dot(ext, w_ref[...], preferred_element_type=F32)
            a_ref[half] = a_ext[halo:halo + ts]
            gate_up.append(_ext_conv(a_ext, cw_ref[half])[0][halo:halo + ts] + cb_ref[half])
        g, u = gate_up
        act_ref[...] = (g * _sigmoid(g) * u).astype(BF16)

    prev, nxt = _row_halo_specs(ts, s, halo, D_MODEL)
    return pl.pallas_call(
        body, name="ffn_fwd", grid=(n, nj),
        in_specs=[pl.BlockSpec((ts, D_MODEL), lambda i, j: (i, 0)), prev, nxt,
                  pl.BlockSpec((D_MODEL, tc), lambda i, j: (0, j)), pl.BlockSpec((D_MODEL, tc), lambda i, j: (0, j + nj)),
                  pl.BlockSpec((2, 8, tc), lambda i, j: (0, 0, j)), pl.BlockSpec((2, 1, tc), lambda i, j: (0, 0, j))],
        out_specs=[pl.BlockSpec((2, ts, tc), lambda i, j: (0, i, j)), pl.BlockSpec((ts, tc), lambda i, j: (i, j))],
        out_shape=[jax.ShapeDtypeStruct((2, s, D_FF), F32), jax.ShapeDtypeStruct((s, D_FF), BF16)],
        compiler_params=_params(("parallel", "parallel")),
    )(hf, hf, hf, w_up, w_up, w, b)


def _ffn_bwd(dx2, w_down, a_pre, w, b, *, ts=512, tc=FFN_TC):
    s = dx2.shape[0]
    n, nj, halo = s // ts, D_FF // tc, FFN_HALO_F32
    main = slice(halo, halo + ts)

    def body(dx_ref, dxp_ref, dxn_ref, wd_ref, a_ref, ap_ref, an_ref, cw_ref, cb_ref, o_ref, dw_ref, db_ref):
        i, j = pl.program_id(0), pl.program_id(1)
        first, last = i == 0, i == n - 1

        @pl.when(first & (j == 0))
        def _():
            dw_ref[...] = jnp.zeros_like(dw_ref)
            db_ref[...] = jnp.zeros_like(db_ref)

        dx_ext = _ext_rows(dxp_ref[...], dx_ref[...], dxn_ref[...], first, last).astype(BF16)
        dact = lax.dot_general(dx_ext, wd_ref[...], _NT, preferred_element_type=F32)
        halves = []
        for half in range(2):
            a_ext = _ext_rows(ap_ref[half], a_ref[half], an_ref[half], first, last)
            conv, a_dn, a_up = _ext_conv(a_ext, cw_ref[half])
            halves.append((conv + cb_ref[half], a_dn, a_ext, a_up))
        g, u = halves[0][0], halves[1][0]
        sg = _sigmoid(g)
        grads = (dact * u * (sg * (1.0 + g * (1.0 - sg))), dact * (g * sg))
        for half in range(2):
            d = grads[half]
            _, a_dn, a_ext, a_up = halves[half]
            wv = cw_ref[half]
            d_pre = pltpu.roll(d, d.shape[0] - 1, 0) * wv[0:1, :] + d * wv[1:2, :] + pltpu.roll(d, 1, 0) * wv[2:3, :]
            o_ref[half] = d_pre[main].astype(BF16)
            dm = d[main]
            dw_ref[j, half, 0:1, :] += jnp.sum(dm * a_dn[main], axis=0, keepdims=True)
            dw_ref[j, half, 1:2, :] += jnp.sum(dm * a_ext[main], axis=0, keepdims=True)
            dw_ref[j, half, 2:3, :] += jnp.sum(dm * a_up[main], axis=0, keepdims=True)
            db_ref[j, half] += jnp.sum(dm, axis=0, keepdims=True)

    dxp, dxn = _row_halo_specs(ts, s, halo, D_MODEL)
    nb, lastb = ts // halo, s // halo - 1
    a_main = pl.BlockSpec((2, ts, tc), lambda i, j: (0, i, j))
    a_prev = pl.BlockSpec((2, halo, tc), lambda i, j: (0, jnp.maximum(i * nb - 1, 0), j))
    a_next = pl.BlockSpec((2, halo, tc), lambda i, j: (0, jnp.minimum((i + 1) * nb, lastb), j))
    da_pre, dw, db = pl.pallas_call(
        body, name="ffn_bwd", grid=(n, nj),
        in_specs=[pl.BlockSpec((ts, D_MODEL), lambda i, j: (i, 0)), dxp, dxn,
                  pl.BlockSpec((tc, D_MODEL), lambda i, j: (j, 0)), a_main, a_prev, a_next,
                  pl.BlockSpec((2, 8, tc), lambda i, j: (0, 0, j)), pl.BlockSpec((2, 1, tc), lambda i, j: (0, 0, j))],
        out_specs=[a_main, pl.BlockSpec((nj, 2, 8, tc), lambda i, j: (0, 0, 0, 0)),
                   pl.BlockSpec((nj, 2, 1, tc), lambda i, j: (0, 0, 0, 0))],
        out_shape=[jax.ShapeDtypeStruct((2, s, D_FF), BF16), jax.ShapeDtypeStruct((nj, 2, 8, tc), F32),
                   jax.ShapeDtypeStruct((nj, 2, 1, tc), F32)],
        compiler_params=_params(("arbitrary", "arbitrary")),
    )(dx2, dx2, dx2, w_down, a_pre, a_pre, a_pre, w, b)
    return (da_pre, dw.transpose(1, 2, 0, 3).reshape(2, 8, D_FF), db.transpose(1, 2, 0, 3).reshape(2, 1, D_FF))


def _ple_final(x2, gl, pp, target, gf, *, ts=256):
    s, d = x2.shape

    def body(x2_ref, gl_ref, pp_ref, t_ref, gf_ref, loss_ref, dx3_ref, dgl_ref, dpp_ref, dgf_ref):
        @pl.when(pl.program_id(0) == 0)
        def _():
            loss_ref[...] = jnp.zeros_like(loss_ref)
            dgf_ref[...] = jnp.zeros_like(dgf_ref)

        gate = _sigmoid(gl_ref[...])
        ppv = pp_ref[...]
        x3 = x2_ref[...] + gate * ppv
        gfv = gf_ref[...]
        err = x3 * _rms_scale(x3) * gfv - t_ref[...]
        loss_ref[...] += 0.5 * jnp.sum(jnp.mean(err * err, axis=-1, keepdims=True), axis=0, keepdims=True)
        dx3, dgf_rows = _rms_bwd_rows(x3, gfv, err * (1.0 / d))
        dgf_ref[...] += jnp.sum(dgf_rows, axis=0, keepdims=True)
        dx3_ref[...] = dx3
        dgl_ref[...] = (dx3 * ppv * gate * (1.0 - gate)).astype(BF16)
        dpp_ref[...] = (dx3 * gate).astype(BF16)

    row = pl.BlockSpec((ts, d), lambda i: (i, 0))
    vec = pl.BlockSpec((1, d), lambda i: (0, 0))
    return pl.pallas_call(
        body, name="ple_final", grid=(s // ts,),
        in_specs=[row, row, row, row, vec],
        out_specs=[pl.BlockSpec((1, 128), lambda i: (0, 0)), row, row, row, vec],
        out_shape=[jax.ShapeDtypeStruct((1, 128), F32), jax.ShapeDtypeStruct((s, d), F32),
                   jax.ShapeDtypeStruct((s, d), BF16), jax.ShapeDtypeStruct((s, d), BF16),
                   jax.ShapeDtypeStruct((1, d), F32)],
        compiler_params=_params(("arbitrary",)),
    )(x2, gl, pp, target, gf)


def _row_tile(rows, cols, n_arrays, budget=12 << 20):
    best = None
    for t in range(8, rows + 1, 8):
        if rows % t == 0 and t * cols * 4 * n_arrays <= budget:
            best = t
    return rows if best is None else best


def _sum_slots(a, *, name):
    g, r, c = a.shape
    tr = _row_tile(r, c, g + 1)

    def body(*refs):
        tot = refs[0][...]
        for ref in refs[1:g]:
            tot = tot + ref[...]
        refs[g][...] = tot

    specs = [pl.BlockSpec((None, tr, c), functools.partial(lambda i, slot: (slot, i, 0), slot=k)) for k in range(g)]
    return pl.pallas_call(
        body, name=name, grid=(r // tr,), in_specs=specs, out_specs=pl.BlockSpec((tr, c), lambda i: (i, 0)),
        out_shape=jax.ShapeDtypeStruct((r, c), a.dtype), compiler_params=_params(("parallel",)),
    )(*([a] * g))


def _adamw(w, g, m, v, *, name):
    r, c = w.shape
    tr = _row_tile(r, c, 7)

    def body(w_ref, g_ref, m_ref, v_ref, d_ref, mo_ref, vo_ref):
        gv = g_ref[...]
        mn = ADAM_B1 * m_ref[...] + (1.0 - ADAM_B1) * gv
        vn = ADAM_B2 * v_ref[...] + (1.0 - ADAM_B2) * (gv * gv)
        m_hat = mn / (1.0 - ADAM_B1 ** ADAM_STEP)
        v_hat = vn / (1.0 - ADAM_B2 ** ADAM_STEP)
        d_ref[...] = -ADAM_LR * (m_hat / (jnp.sqrt(v_hat) + ADAM_EPS) + ADAM_WD * w_ref[...])
        mo_ref[...] = mn
        vo_ref[...] = vn

    blk = pl.BlockSpec((tr, c), lambda i: (i, 0))
    return pl.pallas_call(
        body, name=name, grid=(r // tr,), in_specs=[blk] * 4, out_specs=[blk] * 3,
        out_shape=[jax.ShapeDtypeStruct((r, c), F32)] * 3, compiler_params=_params(("parallel",)),
    )(w, g, m, v)


def _position():
    x, y, c = lax.axis_index("x"), lax.axis_index("y"), lax.axis_index("c")
    return x, y, c


def _other_chips(x, y):
    return [(1 - x, y), (x, 1 - y), (1 - x, 1 - y)]


def _stage_in(srcs, stage, sems):
    cps = [pltpu.make_async_copy(src, stage[a], sems.at[a]) for a, src in enumerate(srcs)]
    for cp in cps:
        cp.start()
    return cps


def _stage_out(staged, stage, dsts, sems):
    cps = []
    for a, dst in enumerate(dsts):
        staged[a].wait()
        cp = pltpu.make_async_copy(stage[a], dst, sems.at[a])
        cp.start()
        cps.append(cp)
    return cps


def _gather_chips(shards):
    n = len(shards)

    def body(*refs):
        ins, outs, stage = refs[:n], refs[n:2 * n], refs[2 * n:3 * n]
        send_sems, recv_sems, in_sems, out_sems = refs[3 * n:]
        x, y, c = _position()
        me = 2 * x + y
        chips = _other_chips(x, y)
        remote = []
        staged = _stage_in(ins, stage, in_sems)
        for a in range(n):
            for k, (px, py) in enumerate(chips):
                rc = pltpu.make_async_remote_copy(
                    src_ref=ins[a], dst_ref=outs[a].at[me], send_sem=send_sems.at[3 * a + k],
                    recv_sem=recv_sems.at[3 * a + k], device_id=(px, py, c), device_id_type=MESH)
                rc.start()
                remote.append(rc)
        local = _stage_out(staged, stage, [o.at[me] for o in outs], out_sems)
        for a in range(n):
            for k, (px, py) in enumerate(chips):
                pltpu.make_async_remote_copy(
                    src_ref=ins[a], dst_ref=outs[a].at[2 * px + py], send_sem=send_sems.at[3 * a + k],
                    recv_sem=recv_sems.at[3 * a + k], device_id=(px, py, c), device_id_type=MESH).wait_recv()
        for rc in remote:
            rc.wait_send()
        for cp in local:
            cp.wait()

    return pl.pallas_call(
        body, name="gather_chips", in_specs=[ANY] * n, out_specs=[ANY] * n,
        out_shape=[jax.ShapeDtypeStruct((N_CHIPS,) + s.shape, s.dtype) for s in shards],
        scratch_shapes=[pltpu.VMEM(s.shape, s.dtype) for s in shards]
        + [pltpu.SemaphoreType.DMA((3 * n,)), pltpu.SemaphoreType.DMA((3 * n,)),
           pltpu.SemaphoreType.DMA((n,)), pltpu.SemaphoreType.DMA((n,))],
        compiler_params=pltpu.CompilerParams(has_side_effects=True),
    )(*shards)


def _send_other_halves(grads, *, tag):
    n = len(grads)

    def body(*refs):
        ins, sib = refs[:n], refs[n:2 * n]
        send_sems, recv_sems = refs[2 * n:]
        x, y, c = _position()
        remote = []
        for a in range(n):
            half = ins[a].shape[1] // 2
            give = ins[a].at[:, pl.ds(pl.multiple_of((1 - c) * half, 8), half), :]
            rc = pltpu.make_async_remote_copy(
                src_ref=give, dst_ref=sib[a], send_sem=send_sems.at[a], recv_sem=recv_sems.at[a],
                device_id=(x, y, 1 - c), device_id_type=MESH)
            rc.start()
            remote.append(rc)
        for rc in remote:
            rc.wait_recv()
        for rc in remote:
            rc.wait_send()

    return pl.pallas_call(
        body, name="send_other_halves_" + tag, in_specs=[ANY] * n, out_specs=[ANY] * n,
        out_shape=[jax.ShapeDtypeStruct((g.shape[0], g.shape[1] // 2, g.shape[2]), g.dtype) for g in grads],
        scratch_shapes=[pltpu.SemaphoreType.DMA((n,)), pltpu.SemaphoreType.DMA((n,))],
        compiler_params=pltpu.CompilerParams(has_side_effects=True),
    )(*grads)


def _add_own_half(g4, sib, core, *, name):
    g, a2, c = sib.shape
    tr = _row_tile(a2, c, 4)

    def body(core_ref, a_ref, b_ref, o_ref, o16_ref):
        tot = a_ref[...] + b_ref[...]
        o_ref[...] = tot
        o16_ref[...] = tot.astype(BF16)

    blk = pl.BlockSpec((None, tr, c), lambda i, j, core_ref: (i, j, 0))
    return pl.pallas_call(
        body, name=name,
        grid_spec=pltpu.PrefetchScalarGridSpec(
            num_scalar_prefetch=1, grid=(g, a2 // tr),
            in_specs=[pl.BlockSpec((None, None, tr, c), lambda i, j, core_ref: (i, core_ref[0], j, 0)), blk],
            out_specs=[blk, blk]),
        out_shape=[jax.ShapeDtypeStruct(sib.shape, F32), jax.ShapeDtypeStruct(sib.shape, BF16)],
        compiler_params=_params(("parallel", "parallel")),
    )(core, g4.reshape(g, 2, a2, c), sib)


def _scatter_chips(parts):
    n = len(parts)

    def body(*refs):
        ins, outs = refs[:n], refs[n:2 * n]
        send_sems, recv_sems = refs[2 * n:]
        x, y, c = _position()
        me = 2 * x + y
        chips = _other_chips(x, y)
        remote = []
        for a in range(n):
            for k, (px, py) in enumerate(chips):
                rc = pltpu.make_async_remote_copy(
                    src_ref=ins[a].at[2 * px + py], dst_ref=outs[a].at[me], send_sem=send_sems.at[3 * a + k],
                    recv_sem=recv_sems.at[3 * a + k], device_id=(px, py, c), device_id_type=MESH)
                rc.start()
                remote.append(rc)
        for a in range(n):
            for k, (px, py) in enumerate(chips):
                pltpu.make_async_remote_copy(
                    src_ref=ins[a].at[me], dst_ref=outs[a].at[2 * px + py], send_sem=send_sems.at[3 * a + k],
                    recv_sem=recv_sems.at[3 * a + k], device_id=(px, py, c), device_id_type=MESH).wait_recv()
        for rc in remote:
            rc.wait_send()

    return pl.pallas_call(
        body, name="scatter_chips", in_specs=[ANY] * n, out_specs=[ANY] * n,
        out_shape=[jax.ShapeDtypeStruct(p.shape, p.dtype) for p in parts],
        scratch_shapes=[pltpu.SemaphoreType.DMA((3 * n,)), pltpu.SemaphoreType.DMA((3 * n,))],
        compiler_params=pltpu.CompilerParams(has_side_effects=True),
    )(*parts)


def _sum_chips(landed, own, chip, *, name):
    g, r, c = landed.shape
    tr = _row_tile(r, c, 5)

    def body(chip_ref, *refs):
        me = chip_ref[0]
        own_v = refs[g][...]
        tot = None
        for slot in range(g):
            term = jnp.where(me == slot, own_v, refs[slot][...].astype(F32))
            tot = term if tot is None else tot + term
        refs[g + 1][...] = tot

    def landed_spec(slot):
        return pl.BlockSpec((None, tr, c),
                            lambda i, chip_ref: (jnp.where(chip_ref[0] == slot, (slot + 1) % g, slot), i, 0))

    return pl.pallas_call(
        body, name=name,
        grid_spec=pltpu.PrefetchScalarGridSpec(
            num_scalar_prefetch=1, grid=(r // tr,),
            in_specs=[landed_spec(k) for k in range(g)]
            + [pl.BlockSpec((None, tr, c), lambda i, chip_ref: (chip_ref[0], i, 0))],
            out_specs=pl.BlockSpec((tr, c), lambda i, chip_ref: (i, 0))),
        out_shape=jax.ShapeDtypeStruct((r, c), F32), compiler_params=_params(("parallel",)),
    )(chip, *([landed] * g), own)


def _join_halves(halves):
    n = len(halves)

    def body(*refs):
        ins, outs, stage = refs[:n], refs[n:2 * n], refs[2 * n:3 * n]
        send_sems, recv_sems, in_sems, out_sems = refs[3 * n:]
        x, y, c = _position()
        remote = []
        staged = _stage_in(ins, stage, in_sems)
        for a in range(n):
            rc = pltpu.make_async_remote_copy(
                src_ref=ins[a], dst_ref=outs[a].at[c], send_sem=send_sems.at[a], recv_sem=recv_sems.at[a],
                device_id=(x, y, 1 - c), device_id_type=MESH)
            rc.start()
            remote.append(rc)
        local = _stage_out(staged, stage, [o.at[c] for o in outs], out_sems)
        for a in range(n):
            pltpu.make_async_remote_copy(
                src_ref=ins[a], dst_ref=outs[a].at[1 - c], send_sem=send_sems.at[a], recv_sem=recv_sems.at[a],
                device_id=(x, y, 1 - c), device_id_type=MESH).wait_recv()
        for rc in remote:
            rc.wait_send()
        for cp in local:
            cp.wait()

    return pl.pallas_call(
        body, name="join_halves", in_specs=[ANY] * n, out_specs=[ANY] * n,
        out_shape=[jax.ShapeDtypeStruct((2,) + h.shape, h.dtype) for h in halves],
        scratch_shapes=[pltpu.VMEM(h.shape, h.dtype) for h in halves]
        + [pltpu.SemaphoreType.DMA((n,)), pltpu.SemaphoreType.DMA((n,)), pltpu.SemaphoreType.DMA((n,)),
           pltpu.SemaphoreType.DMA((n,))],
        compiler_params=pltpu.CompilerParams(has_side_effects=True),
    )(*halves)


_HBM = pl.BlockSpec(memory_space=pltpu.HBM)
_SEM = pl.BlockSpec(memory_space=pltpu.SEMAPHORE)


def _chip_copies(srcs, lands, send_sems, recv_sems, scatter):
    x, y, c = _position()
    me = 2 * x + y
    outgoing, incoming = [], []
    for a, (src, land) in enumerate(zip(srcs, lands)):
        for k, (px, py) in enumerate(_other_chips(x, y)):
            peer = 2 * px + py
            sems = dict(send_sem=send_sems.at[3 * a + k], recv_sem=recv_sems.at[3 * a + k], device_id=(px, py, c),
                        device_id_type=MESH)
            outgoing.append(pltpu.make_async_remote_copy(
                src_ref=src.at[peer] if scatter else src, dst_ref=land.at[me], **sems))
            incoming.append(pltpu.make_async_remote_copy(
                src_ref=src.at[me] if scatter else src, dst_ref=land.at[peer], **sems))
    return outgoing, incoming


def _chips_start(srcs, *, scatter, name):
    n = len(srcs)
    lands = [lax.empty(a.shape if scatter else (N_CHIPS,) + a.shape, a.dtype) for a in srcs]

    def body(*refs):
        ins, send_sems, recv_sems, token = refs[:2 * n], refs[2 * n], refs[2 * n + 1], refs[-1]
        outgoing, _ = _chip_copies(ins[:n], ins[n:], send_sems, recv_sems, scatter)
        for cp in outgoing:
            cp.start()
        token[...] = jnp.zeros_like(token)

    bufs = list(srcs) + lands
    res = pl.pallas_call(
        body, name=name, in_specs=[_HBM] * (2 * n),
        out_specs=(_SEM, _SEM, *[_HBM] * (2 * n), pl.BlockSpec(memory_space=pltpu.VMEM)),
        out_shape=(pltpu.SemaphoreType.DMA((3 * n,)), pltpu.SemaphoreType.DMA((3 * n,)),
                   *[pltpu.HBM(a.shape, a.dtype) for a in bufs], jax.ShapeDtypeStruct((8, 128), F32)),
        input_output_aliases={i: 2 + i for i in range(2 * n)},
        compiler_params=pltpu.CompilerParams(has_side_effects=pltpu.SideEffectType.DATAFLOW_SIDE_EFFECTING),
    )(*[pltpu.with_memory_space_constraint(a, pltpu.HBM) for a in bufs])
    return res[0], res[1], list(res[2:2 + n]), list(res[2 + n:2 + 2 * n]), res[-1]


def _chips_wait(handle, after, *, scatter, name):
    send_sems, recv_sems, srcs, lands, _ = handle
    n = len(srcs)

    def body(*refs):
        ins, send_ref, recv_ref = refs[:2 * n], refs[2 * n], refs[2 * n + 1]
        outgoing, incoming = _chip_copies(ins[:n], ins[n:], send_ref, recv_ref, scatter)
        for cp in outgoing:
            cp.wait_send()
        for cp in incoming:
            cp.wait_recv()

    bufs = list(srcs) + list(lands)
    res = pl.pallas_call(
        body, name=name, in_specs=[_HBM] * (2 * n) + [_SEM, _SEM, ANY], out_specs=tuple([_HBM] * (2 * n)),
        out_shape=tuple(pltpu.HBM(a.shape, a.dtype) for a in bufs),
        input_output_aliases={i: i for i in range(2 * n)},
        compiler_params=pltpu.CompilerParams(has_side_effects=pltpu.SideEffectType.DATAFLOW_SIDE_EFFECTING),
    )(*bufs, send_sems, recv_sems, after)
    return list(res[:n]), list(res[n:])


def _gather_all(buf):
    def body(in_ref, out_ref, send_sems, recv_sems, local_sem):
        x, y, c = _position()
        me = 4 * x + 2 * y + c
        peers = [(x, y, 1 - c)] + [(px, py, pc) for (px, py) in _other_chips(x, y) for pc in (c, 1 - c)]
        cp = pltpu.make_async_copy(in_ref, out_ref.at[me], local_sem)
        cp.start()
        remote = []
        for k, peer in enumerate(peers):
            rc = pltpu.make_async_remote_copy(
                src_ref=in_ref, dst_ref=out_ref.at[me], send_sem=send_sems.at[k], recv_sem=recv_sems.at[k],
                device_id=peer, device_id_type=MESH)
            rc.start()
            remote.append(rc)
        for k, (px, py, pc) in enumerate(peers):
            pltpu.make_async_remote_copy(
                src_ref=in_ref, dst_ref=out_ref.at[4 * px + 2 * py + pc], send_sem=send_sems.at[k],
                recv_sem=recv_sems.at[k], device_id=(px, py, pc), device_id_type=MESH).wait_recv()
        for rc in remote:
            rc.wait_send()
        cp.wait()

    return pl.pallas_call(
        body, name="gather_all", in_specs=[ANY], out_specs=ANY,
        out_shape=jax.ShapeDtypeStruct((N_DEV,) + buf.shape, buf.dtype),
        scratch_shapes=[pltpu.SemaphoreType.DMA((N_DEV - 1,)), pltpu.SemaphoreType.DMA((N_DEV - 1,)),
                        pltpu.SemaphoreType.DMA],
        compiler_params=pltpu.CompilerParams(has_side_effects=True),
    )(buf)


def _cols_from_shards(g4):
    _, k, n = g4.shape
    return g4.transpose(1, 0, 2).reshape(k, N_CHIPS * n)


def _cols_to_shards(w):
    k, n = w.shape
    return w.reshape(k, N_CHIPS, n // N_CHIPS).transpose(1, 0, 2)


def _pad_heads(w, width):
    k = w.shape[0]
    w3 = w.reshape(k, N_HEADS, width)
    return jnp.pad(w3, ((0, 0), (0, 0), (0, HEAD_PAD - width))).reshape(k, D_ATT)


def _unpad_heads(w, width):
    k = w.shape[0]
    return w.reshape(k, N_HEADS, HEAD_PAD)[:, :, :width]


def _rope_tables(s):
    pos = jnp.arange(s, dtype=F32)
    inv_freq = ROPE_THETA ** (-jnp.arange(0, QK_ROPE, 2, dtype=F32) / QK_ROPE)
    ang = pos[:, None] * inv_freq[None, :]
    cos_h, sin_h = jnp.cos(ang), jnp.sin(ang)
    half = QK_ROPE // 2
    z = jnp.zeros((s, half), F32)
    ones = jnp.ones((s, QK_NOPE), F32)
    tail = jnp.zeros((s, HEAD_PAD - QK_NOPE - QK_ROPE), F32)
    cos = jnp.concatenate([ones, cos_h, cos_h, tail + 1.0], axis=1)
    sin_a = jnp.concatenate([ones * 0.0, -sin_h, z, tail], axis=1)
    sin_b = jnp.concatenate([ones * 0.0, z, sin_h, tail], axis=1)
    return cos, sin_a, sin_b


def _local_step(x, p, target, wts, late_weights, reduce_early):
    s = x.shape[0]
    cos, sin_a, sin_b = _rope_tables(s)
    g1, gq, gkv, g2, g3, gf = (wts[k] for k in ("norm_mix_g", "q_norm_g", "kv_norm_g", "norm_ffn_g", "ple_norm_g",
                                                 "final_norm_g"))
    w_in_p, w_uq_p, w_kv_p = wts["w_in_p"], wts["w_uq_p"], wts["w_kv_p"]
    conv_w8, fconv_w, fconv_b = wts["conv_w8"], wts["ffn_conv_w"], wts["ffn_conv_b"]

    h = _rms_fwd(x, g1, name="rms_mix")
    z = _mm(h, w_in_p, name="mm_in", tm=512, tn=1024, tk=1024)
    y_conv, qn, kvn, kr = _mix_pre(z, conv_w8, gq, gkv, cos, sin_a, sin_b)
    q, k, v = _qkv_proj(qn, kvn, kr, w_uq_p, w_kv_p, cos, sin_a, sin_b)
    o, lse = _flash_fwd(q, k, v)
    late = late_weights(lse)
    w_o_a, w_o_b, w_up, w_down = late["w_o_a"], late["w_o_b"], late["w_up"], late["w_down"]
    w_pg, w_pp = late["w_ple_gate"], late["w_ple_proj"]
    t = _mm(y_conv, w_o_a, add=x, name="mm_o_conv", tm=512, tn=1024, tk=512)
    x1 = _mm(o, w_o_b, add=t, name="mm_o_att", tm=512, tn=1024, tk=1024)
    hf = _rms_fwd(x1, g2, name="rms_ffn")
    a_pre, act = _ffn_fwd(hf, w_up, fconv_w, fconv_b)
    x2 = _mm(act, w_down, add=x1, name="mm_down", tm=512, tn=1024, tk=2816)
    n3 = _rms_fwd(x2, g3, name="rms_ple")
    gl = _mm(n3, w_pg, name="mm_gate", tm=512, tn=1024, tk=1024)
    pp = _mm(p, w_pp, name="mm_ple", tm=512, tn=1024, tk=256)
    loss, dx3, dgl, dpp, d_gf = _ple_final(x2, gl, pp, target, gf)

    grads, early = {"final_norm_g": d_gf}, {}
    early["w_ple_proj"] = _mm(p, dpp, ta=True, name="mm_d_wpp", tm=256, tn=1024, tk=2048)
    early["w_ple_gate"] = _mm(n3, dgl, ta=True, name="mm_d_wpg", tm=512, tn=1024, tk=2048)
    dn3 = _mm(dgl, w_pg, tb=True, name="mm_d_n3", tm=512, tn=1024, tk=1024)
    dx2, grads["ple_norm_g"] = _rms_bwd(x2, g3, dn3, dx3, name="rms_ple_bwd")
    early["w_down"] = _mm(act, dx2, ta=True, name="mm_d_wdown", tm=1408, tn=512, tk=2048)
    da_pre, grads["ffn_conv_w"], grads["ffn_conv_b"] = _ffn_bwd(dx2, w_down, a_pre, fconv_w, fconv_b)
    early["w_up"] = _mm(hf, da_pre, ta=True, b_split=True, name="mm_d_wup", tm=512, tn=1408, tk=2048)
    dhf = _mm(da_pre, w_up, tb=True, a_split=True, name="mm_d_hf", tm=512, tn=1024, tk=2816)
    dx1, grads["norm_ffn_g"] = _rms_bwd(x1, g2, dhf, dx2, name="rms_ffn_bwd")
    d_wo_a = _mm(y_conv, dx1, ta=True, name="mm_d_wo_conv", tm=512, tn=1024, tk=2048)
    d_wo_b = _mm(o, dx1, ta=True, name="mm_d_wo_att", tm=512, tn=1024, tk=2048)
    early["w_o"] = jnp.concatenate([d_wo_a, d_wo_b.reshape(N_HEADS, HEAD_PAD, D_MODEL)[:, :V_HEAD]
                                    .reshape(N_HEADS * V_HEAD, D_MODEL)], axis=0)
    token, finish = reduce_early(early)
    dyc = _mm(dx1, w_o_a, tb=True, name="mm_d_yconv", tm=512, tn=512, tk=1024)
    do = _mm(dx1, w_o_b, tb=True, out_dtype=BF16, name="mm_d_o", tm=512, tn=1024, tk=1024)
    delta = _attn_delta(do, o)
    dq, dk, dv = _flash_bwd(q, q.T, k, v, do, do.T, lse, delta, token)
    reduced_early = finish(dq)
    dq_pre, dkr = _qk_bwd(dq, dk, cos, sin_a, sin_b)
    grads["w_uq_p"] = _mm(qn, dq_pre, ta=True, name="mm_d_wuq", tm=256, tn=1024, tk=2048)
    dqn = _mm(dq_pre, w_uq_p, tb=True, name="mm_d_qn", tm=512, tn=256, tk=1024)
    grads["w_k_p"] = _mm(kvn, dk, ta=True, name="mm_d_wk", tm=128, tn=1024, tk=2048)
    grads["w_v_p"] = _mm(kvn, dv, ta=True, name="mm_d_wv", tm=128, tn=1024, tk=2048)
    dkvn_k = _mm(dk, w_kv_p[:, :D_ATT], tb=True, name="mm_d_kvn_k", tm=512, tn=128, tk=1024)
    dkvn = _mm(dv, w_kv_p[:, D_ATT:], tb=True, add=dkvn_k, name="mm_d_kvn_v", tm=512, tn=128, tk=1024)
    dz, grads["conv_w"], grads["q_norm_g"], grads["kv_norm_g"] = _mix_bwd(
        z, dyc, dqn, dkvn, dkr, conv_w8, gq, gkv, cos, sin_a, sin_b)
    grads["w_in_p"] = _mm(h, dz, ta=True, name="mm_d_win", tm=512, tn=1024, tk=2048)
    dh = _mm(dz, w_in_p, tb=True, name="mm_d_h", tm=512, tn=1024, tk=2048)
    grad_x, grads["norm_mix_g"] = _rms_bwd(x, g1, dh, dx1, name="rms_mix_bwd")
    return loss[0, 0], grad_x, grads, reduced_early


_EARLY_W = ("w_in", "w_uq", "w_ukv")
_LATE_W = ("w_o", "w_up", "w_down", "w_ple_gate", "w_ple_proj")
_BIG = _EARLY_W + _LATE_W
_COL_SHARDED = ("w_in", "w_uq", "w_ukv", "w_up", "w_ple_proj")
_SMALL = ("norm_mix_g", "conv_w", "q_norm_g", "kv_norm_g", "norm_ffn_g", "ffn_conv_w", "ffn_conv_b", "ple_norm_g",
          "final_norm_g")


def _full_from_slots(n, g4):
    return _cols_from_shards(g4) if n in _COL_SHARDED else g4.reshape(-1, g4.shape[2])


def _shard_major(n, g):
    return _cols_to_shards(g) if n in _COL_SHARDED else g.reshape(N_CHIPS, g.shape[0] // N_CHIPS, g.shape[1])


def _early_weights(w):
    shards = [w[n][0].astype(BF16) for n in _EARLY_W]
    shards.append(jnp.pad(w["conv_w"][0], ((0, 5), (0, 0))))
    shards.append(jnp.pad(w["ffn_conv_w"][0], ((0, 5), (0, 0))))
    got = _gather_chips(shards)
    full = {n: _full_from_slots(n, g4) for n, g4 in zip(_EARLY_W, got)}
    full["conv_w8"] = _cols_from_shards(got[len(_EARLY_W)])
    full["ffn_conv_w8"] = _cols_from_shards(got[len(_EARLY_W) + 1])
    return _layout_early(full, w)


def _layout_early(full, w):
    out = {n: w[n] for n in ("norm_mix_g", "q_norm_g", "kv_norm_g", "norm_ffn_g", "ple_norm_g")}
    out["final_norm_g"] = w["final_norm_g"][None, :]
    w_in = full["w_in"]
    zc = jnp.zeros((D_MODEL, QK_NOPE), BF16)
    zt = jnp.zeros((D_MODEL, HEAD_PAD - QK_NOPE - QK_ROPE), BF16)
    out["w_in_p"] = jnp.concatenate([w_in[:, :D_IN - QK_ROPE], zc, w_in[:, D_IN - QK_ROPE:], zt], axis=1)
    out["w_uq_p"] = _pad_heads(full["w_uq"], QK_NOPE + QK_ROPE)
    kv3 = full["w_ukv"].reshape(KV_LORA, N_HEADS, QK_NOPE + V_HEAD)
    out["w_kv_p"] = jnp.concatenate([_pad_heads(kv3[:, :, :QK_NOPE].reshape(KV_LORA, -1), QK_NOPE),
                                     _pad_heads(kv3[:, :, QK_NOPE:].reshape(KV_LORA, -1), V_HEAD)], axis=1)
    out["conv_w8"] = full["conv_w8"]
    fw = full["ffn_conv_w8"]
    out["ffn_conv_w"] = jnp.stack([fw[:, :D_FF], fw[:, D_FF:]])
    out["ffn_conv_b"] = w["ffn_conv_b"].reshape(2, 1, D_FF)
    return out


def _layout_late(full):
    w_o = full["w_o"]
    out = {"w_o_a": w_o[:CONV_WIDTH]}
    out["w_o_b"] = jnp.pad(w_o[CONV_WIDTH:].reshape(N_HEADS, V_HEAD, D_MODEL),
                           ((0, 0), (0, HEAD_PAD - V_HEAD), (0, 0))).reshape(D_ATT, D_MODEL)
    for n in ("w_up", "w_down", "w_ple_gate", "w_ple_proj"):
        out[n] = full[n]
    return out


def _true_gradients(g):
    out = {}
    wp = g["w_in_p"]
    out["w_in"] = jnp.concatenate([wp[:, :D_IN - QK_ROPE], wp[:, D_IN_PAD - HEAD_PAD + QK_NOPE:
                                                              D_IN_PAD - HEAD_PAD + QK_NOPE + QK_ROPE]], axis=1)
    out["w_uq"] = _unpad_heads(g["w_uq_p"], QK_NOPE + QK_ROPE).reshape(Q_LORA, -1)
    out["w_ukv"] = jnp.concatenate([_unpad_heads(g["w_k_p"], QK_NOPE), _unpad_heads(g["w_v_p"], V_HEAD)],
                                   axis=2).reshape(KV_LORA, -1)
    out["conv_w"] = g["conv_w"]
    fw = g["ffn_conv_w"]
    out["ffn_conv_w"] = jnp.concatenate([fw[0, :3], fw[1, :3]], axis=1)
    out["ffn_conv_b"] = g["ffn_conv_b"].reshape(1, 2 * D_FF)
    for n in ("norm_mix_g", "q_norm_g", "kv_norm_g", "norm_ffn_g", "ple_norm_g", "final_norm_g"):
        out[n] = g[n]
    return out


def _chip_partials(names, g, core, *, tag):
    g4 = [_shard_major(n, g[n]) for n in names]
    sib = _send_other_halves(g4, tag=tag)
    return [_add_own_half(a, b, core, name="add_cores_" + n) for n, a, b in zip(names, g4, sib)]


_SMALL_SIZES = {"norm_mix_g": D_MODEL, "conv_w": 3 * CONV_WIDTH, "q_norm_g": Q_LORA, "kv_norm_g": KV_LORA,
                "norm_ffn_g": D_MODEL, "ffn_conv_w": 6 * D_FF, "ffn_conv_b": 2 * D_FF, "ple_norm_g": D_MODEL,
                "final_norm_g": D_MODEL}


def _pack(parts, rows):
    flat = jnp.concatenate([a.reshape(-1) for a in parts])
    return jnp.pad(flat, (0, rows * 128 - flat.shape[0])).reshape(rows, 128)


def _unpack(buf, sizes):
    flat = buf.reshape(-1)
    out, at = [], 0
    for n in sizes:
        out.append(flat[at:at + n])
        at += n
    return out


def _reduce_small(g):
    total = sum(_SMALL_SIZES[n] for n in _SMALL)
    rows = -(-total // 1024) * 8
    slots = _gather_all(_pack([g[n] for n in _SMALL], rows))
    summed = _sum_slots(slots, name="sum_small")
    return dict(zip(_SMALL, _unpack(summed, [_SMALL_SIZES[n] for n in _SMALL])))


def kernel(x, p, norm_mix_g, w_in, conv_w, q_norm_g, w_uq, kv_norm_g, w_ukv, w_o, norm_ffn_g, w_up, ffn_conv_w, ffn_conv_b, w_down, ple_norm_g, w_ple_gate, w_ple_proj, final_norm_g, loss_target, m_norm_mix_g, m_w_in, m_conv_w, m_q_norm_g, m_w_uq, m_kv_norm_g, m_w_ukv, m_w_o, m_norm_ffn_g, m_w_up, m_ffn_conv_w, m_ffn_conv_b, m_w_down, m_ple_norm_g, m_w_ple_gate, m_w_ple_proj, m_final_norm_g, v_norm_mix_g, v_w_in, v_conv_w, v_q_norm_g, v_w_uq, v_kv_norm_g, v_w_ukv, v_w_o, v_norm_ffn_g, v_w_up, v_ffn_conv_w, v_ffn_conv_b, v_w_down, v_ple_norm_g, v_w_ple_gate, v_w_ple_proj, v_final_norm_g):
    names = ["norm_mix_g", "w_in", "conv_w", "q_norm_g", "w_uq", "kv_norm_g", "w_ukv", "w_o", "norm_ffn_g", "w_up",
             "ffn_conv_w", "ffn_conv_b", "w_down", "ple_norm_g", "w_ple_gate", "w_ple_proj", "final_norm_g"]
    w = dict(zip(names, (norm_mix_g, w_in, conv_w, q_norm_g, w_uq, kv_norm_g, w_ukv, w_o, norm_ffn_g, w_up,
                         ffn_conv_w, ffn_conv_b, w_down, ple_norm_g, w_ple_gate, w_ple_proj, final_norm_g)))
    m = dict(zip(names, (m_norm_mix_g, m_w_in, m_conv_w, m_q_norm_g, m_w_uq, m_kv_norm_g, m_w_ukv, m_w_o,
                         m_norm_ffn_g, m_w_up, m_ffn_conv_w, m_ffn_conv_b, m_w_down, m_ple_norm_g, m_w_ple_gate,
                         m_w_ple_proj, m_final_norm_g)))
    v = dict(zip(names, (v_norm_mix_g, v_w_in, v_conv_w, v_q_norm_g, v_w_uq, v_kv_norm_g, v_w_ukv, v_w_o,
                         v_norm_ffn_g, v_w_up, v_ffn_conv_w, v_ffn_conv_b, v_w_down, v_ple_norm_g, v_w_ple_gate,
                         v_w_ple_proj, v_final_norm_g)))

    core = lax.axis_index("c").astype(jnp.int32).reshape(1)
    chip = (2 * lax.axis_index("x") + lax.axis_index("y")).astype(jnp.int32).reshape(1)

    wts = _early_weights(w)
    gather = _chips_start([w[n][0].astype(BF16) for n in _LATE_W], scatter=False, name="gather_late_start")
    wts["norm_mix_g"] = wts["norm_mix_g"] + gather[4][0, 0]

    def late_weights(after):
        shards, landed = _chips_wait(gather, after, scatter=False, name="gather_late_wait")
        full = {n: _full_from_slots(n, lax.dynamic_update_slice(g4, own[None], (chip[0], 0, 0)))
                for n, own, g4 in zip(_LATE_W, shards, landed)}
        return _layout_late(full)

    def reduce_early(g):
        parts = _chip_partials(_LATE_W, g, core, tag="early")
        scatter = _chips_start([t16 for _, t16 in parts], scatter=True, name="scatter_early_start")

        def finish(after):
            _, landed = _chips_wait(scatter, after, scatter=True, name="scatter_early_wait")
            return [_sum_chips(a, t32, chip, name="sum_chips_" + n) for n, a, (t32, _) in zip(_LATE_W, landed, parts)]

        return scatter[4], finish

    loss, grad_x, padded, halves_early = _local_step(x[0], p[0, 0], loss_target[0], wts, late_weights, reduce_early)
    g_full = _true_gradients(padded)
    loss = lax.psum(loss, ("x", "y", "c"))

    parts = _chip_partials(_EARLY_W, g_full, core, tag="late")
    landed = _scatter_chips([t16 for _, t16 in parts])
    halves = [_sum_chips(a, t32, chip, name="sum_chips_" + n) for n, a, (t32, _) in zip(_EARLY_W, landed, parts)]
    whole = _join_halves(halves + halves_early)
    big = {n: a.reshape(-1, a.shape[2]) for n, a in zip(_BIG, whole)}

    g_out, d_out, m_out, v_out = {}, {}, {}, {}
    for n in _BIG:
        shape = w[n].shape
        g = big[n]
        d, mn, vn = _adamw(w[n][0], g, m[n][0], v[n][0], name="adamw_" + n)
        g_out[n], d_out[n], m_out[n], v_out[n] = (a.reshape(shape) for a in (g, d, mn, vn))

    small = _reduce_small(g_full)
    chip = 2 * lax.axis_index("x") + lax.axis_index("y")
    g_small = {}
    for n in _SMALL:
        shape = w[n].shape
        g = small[n]
        if n in ("conv_w", "ffn_conv_w"):
            width = shape[-1]
            g = lax.dynamic_slice(g.reshape(3, N_CHIPS * width), (0, chip * width), (3, width))
        g_small[n] = g.reshape(shape)
    sizes = [g_small[n].size for n in _SMALL]
    rows = -(-sum(sizes) // 1024) * 8
    packed = [_pack([src[n] for n in _SMALL], rows) for src in (w, g_small, m, v)]
    d_s, m_s, v_s = _adamw(*packed, name="adamw_small")
    for n, d, mn, vn in zip(_SMALL, _unpack(d_s, sizes), _unpack(m_s, sizes), _unpack(v_s, sizes)):
        shape = w[n].shape
        g_out[n], d_out[n], m_out[n], v_out[n] = g_small[n], d.reshape(shape), mn.reshape(shape), vn.reshape(shape)

    return (loss, grad_x[None], *[g_out[n] for n in names], *[d_out[n] for n in names],
            *[m_out[n] for n in names], *[v_out[n] for n in names])
```

```python
import functools

import jax
import jax.numpy as jnp
from jax import lax
from jax.experimental import pallas as pl
from jax.experimental.pallas import tpu as pltpu

F32 = jnp.float32
BF16 = jnp.bfloat16

D_MODEL = 1024
CONV_WIDTH = 512
Q_LORA = 256
KV_LORA = 128
QK_NOPE = 64
QK_ROPE = 32
V_HEAD = 64
N_HEADS = 8
HEAD_PAD = 128
D_ATT = N_HEADS * HEAD_PAD
D_IN = 3 * CONV_WIDTH + Q_LORA + KV_LORA + QK_ROPE
D_IN_PAD = 3 * CONV_WIDTH + Q_LORA + KV_LORA + HEAD_PAD
D_FF = 2816
ROPE_THETA = 10000.0
EPS = 1e-6
SM_SCALE = (QK_NOPE + QK_ROPE) ** -0.5
ONES_LANE = V_HEAD

ADAM_LR = 0.001
ADAM_B1 = 0.9
ADAM_B2 = 0.999
ADAM_EPS = 1e-08
ADAM_WD = 0.01
ADAM_STEP = 10

N_CHIPS = 4
N_DEV = 8
MESH = pl.DeviceIdType.MESH
ANY = pl.BlockSpec(memory_space=pl.ANY)


def _params(sem):
    return pltpu.CompilerParams(dimension_semantics=sem)


def _mm(a, b, *, name, ta=False, tb=False, add=None, out_dtype=F32, tm=512, tn=512, tk=512,
        a_split=False, b_split=False, o_split=False):
    if a_split:
        _, m, kh = a.shape
        k = 2 * kh
    elif ta:
        k, m = a.shape
    else:
        m, k = a.shape
    if b_split:
        _, kb, nh = b.shape
        n = 2 * nh
    elif tb:
        n, kb = b.shape
    else:
        kb, n = b.shape
    assert kb == k, (name, a.shape, b.shape)
    tm, tn, tk = min(tm, m), min(tn, n), min(tk, k)
    assert m % tm == 0 and n % tn == 0 and k % tk == 0, (name, m, n, k, tm, tn, tk)
    gm, gn, gk = m // tm, n // tn, k // tk

    if a_split:
        assert gk % 2 == 0
        a_spec = pl.BlockSpec((None, tm, tk), lambda i, j, kk: (kk // (gk // 2), i, kk % (gk // 2)))
    elif ta:
        a_spec = pl.BlockSpec((tk, tm), lambda i, j, kk: (kk, i))
    else:
        a_spec = pl.BlockSpec((tm, tk), lambda i, j, kk: (i, kk))
    if b_split:
        assert gn % 2 == 0
        b_spec = pl.BlockSpec((None, tk, tn), lambda i, j, kk: (j // (gn // 2), kk, j % (gn // 2)))
    elif tb:
        b_spec = pl.BlockSpec((tn, tk), lambda i, j, kk: (j, kk))
    else:
        b_spec = pl.BlockSpec((tk, tn), lambda i, j, kk: (kk, j))
    if o_split:
        assert gn % 2 == 0
        o_spec = pl.BlockSpec((None, tm, tn), lambda i, j, kk: (j // (gn // 2), i, j % (gn // 2)))
        o_shape = jax.ShapeDtypeStruct((2, m, n // 2), out_dtype)
    else:
        o_spec = pl.BlockSpec((tm, tn), lambda i, j, kk: (i, j))
        o_shape = jax.ShapeDtypeStruct((m, n), out_dtype)
    dims = (((0 if ta else 1,), (1 if tb else 0,)), ((), ()))

    def body(*refs):
        a_ref, b_ref = refs[:2]
        add_ref = None if add is None else refs[2]
        o_ref = refs[2 if add is None else 3]

        def finish(r):
            if add_ref is not None:
                r = r + add_ref[...]
            o_ref[...] = r.astype(o_ref.dtype)

        part = lax.dot_general(a_ref[...].astype(BF16), b_ref[...].astype(BF16), dims, preferred_element_type=F32)
        if gk == 1:
            finish(part)
            return
        acc_ref = refs[-1]
        kk = pl.program_id(2)

        @pl.when(kk == 0)
        def _():
            acc_ref[...] = part

        @pl.when((kk > 0) & (kk < gk - 1))
        def _():
            acc_ref[...] += part

        @pl.when(kk == gk - 1)
        def _():
            finish(acc_ref[...] + part)

    in_specs = [a_spec, b_spec]
    args = [a, b]
    if add is not None:
        in_specs.append(pl.BlockSpec((tm, tn), lambda i, j, kk: (i, j)))
        args.append(add)
    return pl.pallas_call(
        body, name=name, grid=(gm, gn, gk), in_specs=in_specs, out_specs=o_spec, out_shape=o_shape,
        scratch_shapes=[] if gk == 1 else [pltpu.VMEM((tm, tn), F32)],
        compiler_params=_params(("parallel", "parallel", "arbitrary")),
    )(*args)


def _rms_scale(v):
    return lax.rsqrt(jnp.mean(v * v, axis=-1, keepdims=True) + EPS)


def _rms_bwd_rows(v, g, dy):
    r = _rms_scale(v)
    vh = v * r
    dyg = dy * g
    dv = r * (dyg - vh * jnp.mean(dyg * vh, axis=-1, keepdims=True))
    return dv, dy * vh


def _shift_down(v, first_row):
    row = lax.broadcasted_iota(jnp.int32, v.shape, 0)
    return jnp.where(row == 0, first_row, pltpu.roll(v, 1, 0))


def _shift_up(v, last_row):
    n = v.shape[0]
    row = lax.broadcasted_iota(jnp.int32, v.shape, 0)
    return jnp.where(row == n - 1, last_row, pltpu.roll(v, n - 1, 0))


def _rope(t, cos, sin_a, sin_b):
    return t * cos + pltpu.roll(t, HEAD_PAD - 16, 1) * sin_a + pltpu.roll(t, 16, 1) * sin_b


def _rope_bwd(d, cos, sin_a, sin_b):
    return d * cos + pltpu.roll(d * sin_a, 16, 1) + pltpu.roll(d * sin_b, HEAD_PAD - 16, 1)


def _sigmoid(v):
    return 1.0 / (1.0 + jnp.exp(-v))


def _halo_specs(ts, s, width, col):
    nb = ts // 8
    last = s // 8 - 1
    prev = pl.BlockSpec((8, width), lambda i: (jnp.maximum(i * nb - 1, 0), col))
    nxt = pl.BlockSpec((8, width), lambda i: (jnp.minimum((i + 1) * nb, last), col))
    return prev, nxt


def _mm_fused(a, b, *, name, epilogue, row_outs, rows=(), vecs=(), n_vec_out=0, tb=False, a_split=False,
              prologue=None, tm=512, tk=None):
    if a_split:
        _, m, kh = a.shape
        k = 2 * kh
    else:
        m, k = a.shape
    n = b.shape[0] if tb else b.shape[1]
    assert (b.shape[1] if tb else b.shape[0]) == k, (name, a.shape, b.shape)
    tk = k if tk is None else tk
    assert m % tm == 0 and k % tk == 0, (name, m, k, tm, tk)
    gm, gk = m // tm, k // tk
    assert prologue is None or gk == 1
    nr, nv = len(rows), len(vecs)
    n_pro = 0 if prologue is None else 1
    dims = (((1,), (1 if tb else 0,)), ((), ()))

    def body(*refs):
        a_ref, b_ref = refs[:2]
        row_refs, vec_refs = refs[2:2 + nr], refs[2 + nr:2 + nr + nv]
        outs = refs[2 + nr + nv:]
        row_out_refs = outs[n_pro:n_pro + len(row_outs)]
        vec_out_refs = outs[n_pro + len(row_outs):n_pro + len(row_outs) + n_vec_out]
        i, kk = pl.program_id(0), pl.program_id(1)
        vec_vals = [v[...] for v in vec_refs]
        if prologue is None:
            lhs = a_ref[...].astype(BF16)
        else:
            lhs = prologue(a_ref[...], vec_vals)
            outs[0][...] = lhs

        def finish(r):
            row_vals, vec_parts = epilogue(r, [x[...] for x in row_refs], vec_vals)
            for ref, val in zip(row_out_refs, row_vals):
                ref[...] = val.astype(ref.dtype)
            if n_vec_out:
                @pl.when(i == 0)
                def _():
                    for ref in vec_out_refs:
                        ref[...] = jnp.zeros_like(ref)

                for ref, val in zip(vec_out_refs, vec_parts):
                    ref[...] += val

        part = lax.dot_general(lhs, b_ref[...].astype(BF16), dims, preferred_element_type=F32)
        if gk == 1:
            finish(part)
            return
        acc_ref = refs[-1]

        @pl.when(kk == 0)
        def _():
            acc_ref[...] = part

        @pl.when((kk > 0) & (kk < gk - 1))
        def _():
            acc_ref[...] += part

        @pl.when(kk == gk - 1)
        def _():
            finish(acc_ref[...] + part)

    if a_split:
        assert gk % 2 == 0
        a_spec = pl.BlockSpec((None, tm, tk), lambda i, kk: (kk // (gk // 2), i, kk % (gk // 2)))
    else:
        a_spec = pl.BlockSpec((tm, tk), lambda i, kk: (i, kk))
    b_spec = pl.BlockSpec((n, tk), lambda i, kk: (0, kk)) if tb else pl.BlockSpec((tk, n), lambda i, kk: (kk, 0))
    row_spec = pl.BlockSpec((tm, n), lambda i, kk: (i, 0))
    out_specs, out_shape = [], []
    if prologue is not None:
        out_specs.append(pl.BlockSpec((tm, k), lambda i, kk: (i, 0)))
        out_shape.append(jax.ShapeDtypeStruct((m, k), BF16))
    out_specs += [row_spec] * len(row_outs) + [pl.BlockSpec((1, n), lambda i, kk: (0, 0))] * n_vec_out
    out_shape += [jax.ShapeDtypeStruct((m, n), dt) for dt in row_outs] + [jax.ShapeDtypeStruct((1, n), F32)] * n_vec_out
    res = pl.pallas_call(
        body, name=name, grid=(gm, gk),
        in_specs=[a_spec, b_spec] + [row_spec] * nr + [pl.BlockSpec((1, v.shape[1]), lambda i, kk: (0, 0)) for v in vecs],
        out_specs=out_specs, out_shape=out_shape,
        scratch_shapes=[] if gk == 1 else [pltpu.VMEM((tm, n), F32)],
        compiler_params=_params(("arbitrary" if n_vec_out else "parallel", "arbitrary")),
    )(a, b, *rows, *vecs)
    split = n_pro + len(row_outs)
    return list(res[:split]), list(res[split:])


def _pro_rms(a, vecs):
    return (a * _rms_scale(a) * vecs[0]).astype(BF16)


def _epi_plain(r, rows, vecs):
    return [r], []


def _epi_add_rms(r, rows, vecs):
    xn = r + rows[0]
    return [xn, xn * _rms_scale(xn) * vecs[0]], []


def _epi_rms_bwd(r, rows, vecs):
    dv, dg_rows = _rms_bwd_rows(rows[0], vecs[0], r)
    return [dv + rows[1]], [jnp.sum(dg_rows, axis=0, keepdims=True)]


def _mix_pre(z, conv_w8, gq, gkv, cos, sin_a, sin_b, *, ts=256):
    s = z.shape[0]
    n = s // ts
    cw = CONV_WIDTH

    def body(z_ref, xcp, xcn, cgp, cgn, w_ref, gq_ref, gkv_ref, cos_ref, sa_ref, sb_ref,
             yc_ref, qn_ref, kvn_ref, kr_ref):
        i = pl.program_id(0)
        xc = z_ref[:, 0:cw]
        bg = z_ref[:, cw:2 * cw]
        cg = z_ref[:, 2 * cw:3 * cw]
        m = cg * xc
        m_prev = jnp.where(i > 0, xcp[7:8, :] * cgp[7:8, :], 0.0)
        m_next = jnp.where(i < n - 1, xcn[0:1, :] * cgn[0:1, :], 0.0)
        cm = _shift_down(m, m_prev) * w_ref[0:1, :] + m * w_ref[1:2, :] + _shift_up(m, m_next) * w_ref[2:3, :]
        yc_ref[...] = (bg * cm).astype(BF16)
        ql = z_ref[:, 3 * cw:3 * cw + Q_LORA]
        qn_ref[...] = (ql * _rms_scale(ql) * gq_ref[...]).astype(BF16)
        kvl = z_ref[:, 3 * cw + Q_LORA:3 * cw + Q_LORA + KV_LORA]
        kvn_ref[...] = (kvl * _rms_scale(kvl) * gkv_ref[...]).astype(BF16)
        kr_ref[...] = _rope(z_ref[:, D_IN_PAD - HEAD_PAD:D_IN_PAD], cos_ref[...], sa_ref[...], sb_ref[...])

    xcp, xcn = _halo_specs(ts, s, cw, 0)
    cgp, cgn = _halo_specs(ts, s, cw, 2)
    tab = pl.BlockSpec((ts, HEAD_PAD), lambda i: (i, 0))
    return pl.pallas_call(
        body, name="mix_pre", grid=(n,),
        in_specs=[pl.BlockSpec((ts, D_IN_PAD), lambda i: (i, 0)), xcp, xcn, cgp, cgn,
                  pl.BlockSpec((8, cw), lambda i: (0, 0)), pl.BlockSpec((1, Q_LORA), lambda i: (0, 0)),
                  pl.BlockSpec((1, KV_LORA), lambda i: (0, 0)), tab, tab, tab],
        out_specs=[pl.BlockSpec((ts, cw), lambda i: (i, 0)), pl.BlockSpec((ts, Q_LORA), lambda i: (i, 0)),
                   pl.BlockSpec((ts, KV_LORA), lambda i: (i, 0)), tab],
        out_shape=[jax.ShapeDtypeStruct((s, cw), BF16), jax.ShapeDtypeStruct((s, Q_LORA), BF16),
                   jax.ShapeDtypeStruct((s, KV_LORA), BF16), jax.ShapeDtypeStruct((s, HEAD_PAD), F32)],
        compiler_params=_params(("parallel",)),
    )(z, z, z, z, z, conv_w8, gq, gkv, cos, sin_a, sin_b)


def _mix_bwd(z, dyc, dqn, dkvn, dkr, conv_w8, gq, gkv, cos, sin_a, sin_b, *, ts=256):
    s = z.shape[0]
    n = s // ts
    cw = CONV_WIDTH

    def body(z_ref, xcp, xcn, bgp, bgn, cgp, cgn, dyc_ref, dycp, dycn, dqn_ref, dkvn_ref, dkr_ref,
             w_ref, gq_ref, gkv_ref, cos_ref, sa_ref, sb_ref,
             dz_ref, dw0_ref, dw1_ref, dw2_ref, dgq_ref, dgkv_ref):
        i = pl.program_id(0)

        @pl.when(i == 0)
        def _():
            for r in (dw0_ref, dw1_ref, dw2_ref, dgq_ref, dgkv_ref):
                r[...] = jnp.zeros_like(r)

        xc = z_ref[:, 0:cw]
        bg = z_ref[:, cw:2 * cw]
        cg = z_ref[:, 2 * cw:3 * cw]
        w0, w1, w2 = w_ref[0:1, :], w_ref[1:2, :], w_ref[2:3, :]
        m = cg * xc
        m_dn = _shift_down(m, jnp.where(i > 0, xcp[7:8, :] * cgp[7:8, :], 0.0))
        m_up = _shift_up(m, jnp.where(i < n - 1, xcn[0:1, :] * cgn[0:1, :], 0.0))
        cm = m_dn * w0 + m * w1 + m_up * w2
        dyc_v = dyc_ref[...]
        dcm = dyc_v * bg
        dcm_dn = _shift_down(dcm, jnp.where(i > 0, dycp[7:8, :] * bgp[7:8, :], 0.0))
        dcm_up = _shift_up(dcm, jnp.where(i < n - 1, dycn[0:1, :] * bgn[0:1, :], 0.0))
        dm = dcm_up * w0 + dcm * w1 + dcm_dn * w2
        dz_ref[:, 0:cw] = (dm * cg).astype(BF16)
        dz_ref[:, cw:2 * cw] = (dyc_v * cm).astype(BF16)
        dz_ref[:, 2 * cw:3 * cw] = (dm * xc).astype(BF16)
        dw0_ref[...] += jnp.sum(dcm * m_dn, axis=0, keepdims=True)
        dw1_ref[...] += jnp.sum(dcm * m, axis=0, keepdims=True)
        dw2_ref[...] += jnp.sum(dcm * m_up, axis=0, keepdims=True)

        dql, dgq_rows = _rms_bwd_rows(z_ref[:, 3 * cw:3 * cw + Q_LORA], gq_ref[...], dqn_ref[...])
        dz_ref[:, 3 * cw:3 * cw + Q_LORA] = dql.astype(BF16)
        dgq_ref[...] += jnp.sum(dgq_rows, axis=0, keepdims=True)
        dkvl, dgkv_rows = _rms_bwd_rows(z_ref[:, 3 * cw + Q_LORA:3 * cw + Q_LORA + KV_LORA], gkv_ref[...],
                                        dkvn_ref[...])
        dz_ref[:, 3 * cw + Q_LORA:3 * cw + Q_LORA + KV_LORA] = dkvl.astype(BF16)
        dgkv_ref[...] += jnp.sum(dgkv_rows, axis=0, keepdims=True)

        lane = lax.broadcasted_iota(jnp.int32, (ts, HEAD_PAD), 1)
        rope_lane = (lane >= QK_NOPE) & (lane < QK_NOPE + QK_ROPE)
        dk = _rope_bwd(dkr_ref[...], cos_ref[...], sa_ref[...], sb_ref[...])
        dz_ref[:, D_IN_PAD - HEAD_PAD:D_IN_PAD] = jnp.where(rope_lane, dk, 0.0).astype(BF16)

    xcp, xcn = _halo_specs(ts, s, cw, 0)
    bgp, bgn = _halo_specs(ts, s, cw, 1)
    cgp, cgn = _halo_specs(ts, s, cw, 2)
    dycp, dycn = _halo_specs(ts, s, cw, 0)
    tab = pl.BlockSpec((ts, HEAD_PAD), lambda i: (i, 0))

    def vec(width):
        return pl.BlockSpec((1, width), lambda i: (0, 0))

    outs = pl.pallas_call(
        body, name="mix_bwd", grid=(n,),
        in_specs=[pl.BlockSpec((ts, D_IN_PAD), lambda i: (i, 0)), xcp, xcn, bgp, bgn, cgp, cgn,
                  pl.BlockSpec((ts, cw), lambda i: (i, 0)), dycp, dycn,
                  pl.BlockSpec((ts, Q_LORA), lambda i: (i, 0)), pl.BlockSpec((ts, KV_LORA), lambda i: (i, 0)), tab,
                  pl.BlockSpec((8, cw), lambda i: (0, 0)), vec(Q_LORA), vec(KV_LORA), tab, tab, tab],
        out_specs=[pl.BlockSpec((ts, D_IN_PAD), lambda i: (i, 0)), vec(cw), vec(cw), vec(cw), vec(Q_LORA),
                   vec(KV_LORA)],
        out_shape=[jax.ShapeDtypeStruct((s, D_IN_PAD), BF16)] + [jax.ShapeDtypeStruct((1, cw), F32)] * 3
        + [jax.ShapeDtypeStruct((1, Q_LORA), F32), jax.ShapeDtypeStruct((1, KV_LORA), F32)],
        compiler_params=_params(("arbitrary",)),
    )(z, z, z, z, z, z, z, dyc, dyc, dyc, dqn, dkvn, dkr, conv_w8, gq, gkv, cos, sin_a, sin_b)
    dz, dw0, dw1, dw2, dgq, dgkv = outs
    return dz, jnp.concatenate([dw0, dw1, dw2], axis=0), dgq, dgkv


def _qkv_proj(qn, kvn, kr, w_uq_p, w_kv_p, cos, sin_a, sin_b, *, ts=512):
    s = qn.shape[0]

    def body(qn_ref, kvn_ref, kr_ref, wq_ref, wkv_ref, cos_ref, sa_ref, sb_ref, q_ref, k_ref, v_ref):
        cos_v, sa, sb = cos_ref[...], sa_ref[...], sb_ref[...]
        q = jnp.dot(qn_ref[...], wq_ref[...], preferred_element_type=F32)
        kv = jnp.dot(kvn_ref[...], wkv_ref[...], preferred_element_type=F32)
        kr_v = kr_ref[...]
        lane = lax.broadcasted_iota(jnp.int32, (1, HEAD_PAD), 1)
        ones_lane = (lane == ONES_LANE).astype(F32)
        for h in range(N_HEADS):
            blk = slice(h * HEAD_PAD, (h + 1) * HEAD_PAD)
            q_ref[:, blk] = (_rope(q[:, blk], cos_v, sa, sb) * SM_SCALE).astype(BF16)
            k_ref[:, blk] = (kv[:, blk] + kr_v).astype(BF16)
            v_ref[:, blk] = (kv[:, D_ATT + h * HEAD_PAD:D_ATT + (h + 1) * HEAD_PAD] + ones_lane).astype(BF16)

    tab = pl.BlockSpec((ts, HEAD_PAD), lambda i: (i, 0))
    wide = pl.BlockSpec((ts, D_ATT), lambda i: (i, 0))
    return pl.pallas_call(
        body, name="qkv_proj", grid=(s // ts,),
        in_specs=[pl.BlockSpec((ts, Q_LORA), lambda i: (i, 0)), pl.BlockSpec((ts, KV_LORA), lambda i: (i, 0)), tab,
                  pl.BlockSpec((Q_LORA, D_ATT), lambda i: (0, 0)), pl.BlockSpec((KV_LORA, 2 * D_ATT), lambda i: (0, 0)),
                  tab, tab, tab],
        out_specs=[wide, wide, wide],
        out_shape=[jax.ShapeDtypeStruct((s, D_ATT), BF16)] * 3,
        compiler_params=_params(("parallel",)),
    )(qn, kvn, kr, w_uq_p, w_kv_p, cos, sin_a, sin_b)


def _qk_bwd(dq, dk, cos, sin_a, sin_b, *, ts=256):
    s = dq.shape[0]

    def body(dq_ref, dk_ref, cos_ref, sa_ref, sb_ref, dqp_ref, dkr_ref):
        cos_v, sa, sb = cos_ref[...], sa_ref[...], sb_ref[...]
        tot = jnp.zeros((ts, HEAD_PAD), F32)
        for h in range(N_HEADS):
            blk = slice(h * HEAD_PAD, (h + 1) * HEAD_PAD)
            dqp_ref[:, blk] = _rope_bwd(dq_ref[:, blk], cos_v, sa, sb).astype(BF16)
            tot = tot + dk_ref[:, blk]
        dkr_ref[...] = tot

    tab = pl.BlockSpec((ts, HEAD_PAD), lambda i: (i, 0))
    wide = pl.BlockSpec((ts, D_ATT), lambda i: (i, 0))
    return pl.pallas_call(
        body, name="qk_bwd", grid=(s // ts,),
        in_specs=[wide, wide, tab, tab, tab], out_specs=[wide, tab],
        out_shape=[jax.ShapeDtypeStruct((s, D_ATT), BF16), jax.ShapeDtypeStruct((s, HEAD_PAD), F32)],
        compiler_params=_params(("parallel",)),
    )(dq, dk, cos, sin_a, sin_b)


_NT = (((1,), (1,)), ((), ()))
_TN = (((0,), (0,)), ((), ()))


def _flash_fwd(q, k, v, *, tq=1024, tk=1024):
    s = q.shape[0]
    tq, tk = min(tq, s), min(tk, s)
    nk = s // tk

    def body(q_ref, k_ref, v_ref, o_ref, lse_ref):
        qv = q_ref[...]

        def step(j, carry):
            m, acc = carry
            rows = pl.ds(pl.multiple_of(j * tk, tk), tk)
            sc = lax.dot_general(qv, k_ref[rows, :], _NT, preferred_element_type=F32)
            m_new = jnp.maximum(m, jnp.max(sc, axis=1, keepdims=True))
            p = jnp.exp(sc - m_new).astype(BF16)
            acc = jnp.exp(m - m_new) * acc + jnp.dot(p, v_ref[rows, :], preferred_element_type=F32)
            return m_new, acc

        init = (jnp.full((tq, 1), -jnp.inf, F32), jnp.zeros((tq, HEAD_PAD), F32))
        m, acc = lax.fori_loop(0, nk, step, init)
        l = acc[:, ONES_LANE:ONES_LANE + 1]
        o_ref[...] = (acc / l).astype(BF16)
        lse_ref[...] = m + jnp.log(l)

    head = pl.BlockSpec((s, HEAD_PAD), lambda h, i: (0, h))
    return pl.pallas_call(
        body, name="flash_fwd", grid=(N_HEADS, s // tq),
        in_specs=[pl.BlockSpec((tq, HEAD_PAD), lambda h, i: (i, h)), head, head],
        out_specs=[pl.BlockSpec((tq, HEAD_PAD), lambda h, i: (i, h)),
                   pl.BlockSpec((None, tq, 1), lambda h, i: (h, i, 0))],
        out_shape=[jax.ShapeDtypeStruct((s, D_ATT), BF16), jax.ShapeDtypeStruct((N_HEADS, s, 1), F32)],
        compiler_params=_params(("parallel", "parallel")),
    )(q, k, v)


def _attn_delta(do, o, *, ts=512):
    s = do.shape[0]

    def body(do_ref, o_ref, dl_ref):
        for h in range(N_HEADS):
            blk = slice(h * HEAD_PAD, (h + 1) * HEAD_PAD)
            dl_ref[h] = jnp.sum(do_ref[:, blk].astype(F32) * o_ref[:, blk].astype(F32), axis=1, keepdims=True)

    wide = pl.BlockSpec((ts, D_ATT), lambda i: (i, 0))
    return pl.pallas_call(
        body, name="attn_delta", grid=(s // ts,), in_specs=[wide, wide],
        out_specs=pl.BlockSpec((N_HEADS, ts, 1), lambda i: (0, i, 0)),
        out_shape=jax.ShapeDtypeStruct((N_HEADS, s, 1), F32),
        compiler_params=_params(("parallel",)),
    )(do, o)


def _flash_bwd(q, qt, k, v, do, dot, lse, delta, after, *, tq=1024, tk=512):
    s = q.shape[0]
    tq, tk = min(tq, s), min(tk, s)
    nq = s // tq

    def body(q_ref, qt_ref, do_ref, dot_ref, lse_ref, dl_ref, k_ref, v_ref, after_ref, dq_ref, dk_ref, dv_ref):
        j = pl.program_id(1)

        @pl.when(j == 0)
        def _():
            dq_ref[...] = jnp.zeros_like(dq_ref)

        kv, vv = k_ref[...], v_ref[...]

        def step(i, carry):
            dk_t, dv_t = carry
            at = pl.multiple_of(i * tq, tq)
            rows = pl.ds(at, tq)
            sc = lax.dot_general(q_ref[rows, :], kv, _NT, preferred_element_type=F32)
            p = jnp.exp(sc - lse_ref[rows, :])
            dp = lax.dot_general(do_ref[rows, :], vv, _NT, preferred_element_type=F32)
            ds = (p * (dp - dl_ref[rows, :])).astype(BF16)
            dv_t = dv_t + jnp.dot(dot_ref[:, rows], p.astype(BF16), preferred_element_type=F32)
            dk_t = dk_t + jnp.dot(qt_ref[:, rows], ds, preferred_element_type=F32)
            dq_ref[rows, :] += jnp.dot(ds, kv, preferred_element_type=F32)
            return dk_t, dv_t

        zero = jnp.zeros((HEAD_PAD, tk), F32)
        dk_t, dv_t = lax.fori_loop(0, nq, step, (zero, zero))
        dk_ref[...] = dk_t.T
        dv_ref[...] = dv_t.T

        @pl.when(j == pl.num_programs(1) - 1)
        def _():
            dq_ref[...] *= SM_SCALE

    head = pl.BlockSpec((s, HEAD_PAD), lambda h, j: (0, h))
    head_t = pl.BlockSpec((HEAD_PAD, s), lambda h, j: (h, 0))
    stat = pl.BlockSpec((None, s, 1), lambda h, j: (h, 0, 0))
    blk = pl.BlockSpec((tk, HEAD_PAD), lambda h, j: (j, h))
    return pl.pallas_call(
        body, name="flash_bwd", grid=(N_HEADS, s // tk),
        in_specs=[head, head_t, head, head_t, stat, stat, blk, blk, ANY],
        out_specs=[head, blk, blk],
        out_shape=[jax.ShapeDtypeStruct((s, D_ATT), F32)] * 3,
        compiler_params=_params(("parallel", "arbitrary")),
    )(q, qt, do, dot, lse, delta, k, v, after)


FFN_TC = 256


FFN_HALO_BF16 = 16
FFN_HALO_F32 = 8


def _row_halo_specs(ts, s, halo, width):
    nb = ts // halo
    last = s // halo - 1
    prev = pl.BlockSpec((halo, width), lambda i, j: (jnp.maximum(i * nb - 1, 0), 0))
    nxt = pl.BlockSpec((halo, width), lambda i, j: (jnp.minimum((i + 1) * nb, last), 0))
    return prev, nxt


def _ext_rows(prev, main, nxt, first, last):
    return jnp.concatenate([jnp.where(first, jnp.zeros_like(prev), prev), main,
                            jnp.where(last, jnp.zeros_like(nxt), nxt)], axis=0)


def _ext_conv(a, w):
    a_dn = pltpu.roll(a, 1, 0)
    a_up = pltpu.roll(a, a.shape[0] - 1, 0)
    return a_dn * w[0:1, :] + a * w[1:2, :] + a_up * w[2:3, :], a_dn, a_up


def _ffn_fwd(hf, w_up, w, b, *, ts=512, tc=FFN_TC):
    s = hf.shape[0]
    n, nj, halo = s // ts, D_FF // tc, FFN_HALO_BF16

    def body(h_ref, hp_ref, hn_ref, wg_ref, wu_ref, cw_ref, cb_ref, a_ref, act_ref):
        i = pl.program_id(0)
        ext = _ext_rows(hp_ref[...], h_ref[...], hn_ref[...], i == 0, i == n - 1)
        gate_up = []
        for half, w_ref in enumerate((wg_ref, wu_ref)):
            a_ext = jnp.dot(ext, w_ref[...], preferred_element_type=F32)
            a_ref[half] = a_ext[halo:halo + ts]
            gate_up.append(_ext_conv(a_ext, cw_ref[half])[0][halo:halo + ts] + cb_ref[half])
        g, u = gate_up
        act_ref[...] = (g * _sigmoid(g) * u).astype(BF16)

    prev, nxt = _row_halo_specs(ts, s, halo, D_MODEL)
    return pl.pallas_call(
        body, name="ffn_fwd", grid=(n, nj),
        in_specs=[pl.BlockSpec((ts, D_MODEL), lambda i, j: (i, 0)), prev, nxt,
                  pl.BlockSpec((D_MODEL, tc), lambda i, j: (0, j)), pl.BlockSpec((D_MODEL, tc), lambda i, j: (0, j + nj)),
                  pl.BlockSpec((2, 8, tc), lambda i, j: (0, 0, j)), pl.BlockSpec((2, 1, tc), lambda i, j: (0, 0, j))],
        out_specs=[pl.BlockSpec((2, ts, tc), lambda i, j: (0, i, j)), pl.BlockSpec((ts, tc), lambda i, j: (i, j))],
        out_shape=[jax.ShapeDtypeStruct((2, s, D_FF), F32), jax.ShapeDtypeStruct((s, D_FF), BF16)],
        compiler_params=_params(("parallel", "parallel")),
    )(hf, hf, hf, w_up, w_up, w, b)


def _ffn_bwd(dx2, w_down, a_pre, w, b, *, ts=512, tc=FFN_TC):
    s = dx2.shape[0]
    n, nj, halo = s // ts, D_FF // tc, FFN_HALO_F32
    main = slice(halo, halo + ts)

    def body(dx_ref, dxp_ref, dxn_ref, wd_ref, a_ref, ap_ref, an_ref, cw_ref, cb_ref, o_ref, dw_ref, db_ref):
        i, j = pl.program_id(0), pl.program_id(1)
        first, last = i == 0, i == n - 1

        @pl.when(first & (j == 0))
        def _():
            dw_ref[...] = jnp.zeros_like(dw_ref)
            db_ref[...] = jnp.zeros_like(db_ref)

        dx_ext = _ext_rows(dxp_ref[...], dx_ref[...], dxn_ref[...], first, last).astype(BF16)
        dact = lax.dot_general(dx_ext, wd_ref[...], _NT, preferred_element_type=F32)
        halves = []
        for half in range(2):
            a_ext = _ext_rows(ap_ref[half], a_ref[half], an_ref[half], first, last)
            conv, a_dn, a_up = _ext_conv(a_ext, cw_ref[half])
            halves.append((conv + cb_ref[half], a_dn, a_ext, a_up))
        g, u = halves[0][0], halves[1][0]
        sg = _sigmoid(g)
        grads = (dact * u * (sg * (1.0 + g * (1.0 - sg))), dact * (g * sg))
        for half in range(2):
            d = grads[half]
            _, a_dn, a_ext, a_up = halves[half]
            wv = cw_ref[half]
            d_pre = pltpu.roll(d, d.shape[0] - 1, 0) * wv[0:1, :] + d * wv[1:2, :] + pltpu.roll(d, 1, 0) * wv[2:3, :]
            o_ref[half] = d_pre[main].astype(BF16)
            dm = d[main]
            dw_ref[j, half, 0:1, :] += jnp.sum(dm * a_dn[main], axis=0, keepdims=True)
            dw_ref[j, half, 1:2, :] += jnp.sum(dm * a_ext[main], axis=0, keepdims=True)
            dw_ref[j, half, 2:3, :] += jnp.sum(dm * a_up[main], axis=0, keepdims=True)
            db_ref[j, half] += jnp.sum(dm, axis=0, keepdims=True)

    dxp, dxn = _row_halo_specs(ts, s, halo, D_MODEL)
    nb, lastb = ts // halo, s // halo - 1
    a_main = pl.BlockSpec((2, ts, tc), lambda i, j: (0, i, j))
    a_prev = pl.BlockSpec((2, halo, tc), lambda i, j: (0, jnp.maximum(i * nb - 1, 0), j))
    a_next = pl.BlockSpec((2, halo, tc), lambda i, j: (0, jnp.minimum((i + 1) * nb, lastb), j))
    da_pre, dw, db = pl.pallas_call(
        body, name="ffn_bwd", grid=(n, nj),
        in_specs=[pl.BlockSpec((ts, D_MODEL), lambda i, j: (i, 0)), dxp, dxn,
                  pl.BlockSpec((tc, D_MODEL), lambda i, j: (j, 0)), a_main, a_prev, a_next,
                  pl.BlockSpec((2, 8, tc), lambda i, j: (0, 0, j)), pl.BlockSpec((2, 1, tc), lambda i, j: (0, 0, j))],
        out_specs=[a_main, pl.BlockSpec((nj, 2, 8, tc), lambda i, j: (0, 0, 0, 0)),
                   pl.BlockSpec((nj, 2, 1, tc), lambda i, j: (0, 0, 0, 0))],
        out_shape=[jax.ShapeDtypeStruct((2, s, D_FF), BF16), jax.ShapeDtypeStruct((nj, 2, 8, tc), F32),
                   jax.ShapeDtypeStruct((nj, 2, 1, tc), F32)],
        compiler_params=_params(("arbitrary", "arbitrary")),
    )(dx2, dx2, dx2, w_down, a_pre, a_pre, a_pre, w, b)
    return (da_pre, dw.transpose(1, 2, 0, 3).reshape(2, 8, D_FF), db.transpose(1, 2, 0, 3).reshape(2, 1, D_FF))


def _ple_final(x2, n3, p, target, gf, w_pg, w_pp, *, ts=256):
    s, d = x2.shape
    dp = p.shape[1]

    def body(x2_ref, n3_ref, p_ref, t_ref, gf_ref, wg_ref, wp_ref, loss_ref, dx3_ref, dgl_ref, dpp_ref, dgf_ref):
        @pl.when(pl.program_id(0) == 0)
        def _():
            loss_ref[...] = jnp.zeros_like(loss_ref)
            dgf_ref[...] = jnp.zeros_like(dgf_ref)

        gate = _sigmoid(jnp.dot(n3_ref[...], wg_ref[...], preferred_element_type=F32))
        ppv = jnp.dot(p_ref[...].astype(BF16), wp_ref[...], preferred_element_type=F32)
        x3 = x2_ref[...] + gate * ppv
        gfv = gf_ref[...]
        err = x3 * _rms_scale(x3) * gfv - t_ref[...]
        loss_ref[...] += 0.5 * jnp.sum(jnp.mean(err * err, axis=-1, keepdims=True), axis=0, keepdims=True)
        dx3, dgf_rows = _rms_bwd_rows(x3, gfv, err * (1.0 / d))
        dgf_ref[...] += jnp.sum(dgf_rows, axis=0, keepdims=True)
        dx3_ref[...] = dx3
        dgl_ref[...] = (dx3 * ppv * gate * (1.0 - gate)).astype(BF16)
        dpp_ref[...] = (dx3 * gate).astype(BF16)

    row = pl.BlockSpec((ts, d), lambda i: (i, 0))
    vec = pl.BlockSpec((1, d), lambda i: (0, 0))
    return pl.pallas_call(
        body, name="ple_final", grid=(s // ts,),
        in_specs=[row, row, pl.BlockSpec((ts, dp), lambda i: (i, 0)), row, vec,
                  pl.BlockSpec((d, d), lambda i: (0, 0)), pl.BlockSpec((dp, d), lambda i: (0, 0))],
        out_specs=[pl.BlockSpec((1, 128), lambda i: (0, 0)), row, row, row, vec],
        out_shape=[jax.ShapeDtypeStruct((1, 128), F32), jax.ShapeDtypeStruct((s, d), F32),
                   jax.ShapeDtypeStruct((s, d), BF16), jax.ShapeDtypeStruct((s, d), BF16),
                   jax.ShapeDtypeStruct((1, d), F32)],
        compiler_params=_params(("arbitrary",)),
    )(x2, n3, p, target, gf, w_pg, w_pp)


def _row_tile(rows, cols, n_arrays, budget=12 << 20):
    best = None
    for t in range(8, rows + 1, 8):
        if rows % t == 0 and t * cols * 4 * n_arrays <= budget:
            best = t
    return rows if best is None else best


def _sum_slots(a, *, name):
    g, r, c = a.shape
    tr = _row_tile(r, c, g + 1)

    def body(*refs):
        tot = refs[0][...]
        for ref in refs[1:g]:
            tot = tot + ref[...]
        refs[g][...] = tot

    specs = [pl.BlockSpec((None, tr, c), functools.partial(lambda i, slot: (slot, i, 0), slot=k)) for k in range(g)]
    return pl.pallas_call(
        body, name=name, grid=(r // tr,), in_specs=specs, out_specs=pl.BlockSpec((tr, c), lambda i: (i, 0)),
        out_shape=jax.ShapeDtypeStruct((r, c), a.dtype), compiler_params=_params(("parallel",)),
    )(*([a] * g))


def _adamw(w, g, m, v, *, name):
    r, c = w.shape
    tr = _row_tile(r, c, 7)

    def body(w_ref, g_ref, m_ref, v_ref, d_ref, mo_ref, vo_ref):
        gv = g_ref[...]
        mn = ADAM_B1 * m_ref[...] + (1.0 - ADAM_B1) * gv
        vn = ADAM_B2 * v_ref[...] + (1.0 - ADAM_B2) * (gv * gv)
        m_hat = mn / (1.0 - ADAM_B1 ** ADAM_STEP)
        v_hat = vn / (1.0 - ADAM_B2 ** ADAM_STEP)
        d_ref[...] = -ADAM_LR * (m_hat / (jnp.sqrt(v_hat) + ADAM_EPS) + ADAM_WD * w_ref[...])
        mo_ref[...] = mn
        vo_ref[...] = vn

    blk = pl.BlockSpec((tr, c), lambda i: (i, 0))
    return pl.pallas_call(
        body, name=name, grid=(r // tr,), in_specs=[blk] * 4, out_specs=[blk] * 3,
        out_shape=[jax.ShapeDtypeStruct((r, c), F32)] * 3, compiler_params=_params(("parallel",)),
    )(w, g, m, v)


def _position():
    x, y, c = lax.axis_index("x"), lax.axis_index("y"), lax.axis_index("c")
    return x, y, c


def _other_chips(x, y):
    return [(1 - x, y), (x, 1 - y), (1 - x, 1 - y)]


def _stage_in(srcs, stage, sems):
    cps = [pltpu.make_async_copy(src, stage[a], sems.at[a]) for a, src in enumerate(srcs)]
    for cp in cps:
        cp.start()
    return cps


def _stage_out(staged, stage, dsts, sems):
    cps = []
    for a, dst in enumerate(dsts):
        staged[a].wait()
        cp = pltpu.make_async_copy(stage[a], dst, sems.at[a])
        cp.start()
        cps.append(cp)
    return cps


def _gather_chips(shards):
    n = len(shards)

    def body(*refs):
        ins, outs, stage = refs[:n], refs[n:2 * n], refs[2 * n:3 * n]
        send_sems, recv_sems, in_sems, out_sems = refs[3 * n:]
        x, y, c = _position()
        me = 2 * x + y
        chips = _other_chips(x, y)
        remote = []
        staged = _stage_in(ins, stage, in_sems)
        for a in range(n):
            for k, (px, py) in enumerate(chips):
                rc = pltpu.make_async_remote_copy(
                    src_ref=ins[a], dst_ref=outs[a].at[me], send_sem=send_sems.at[3 * a + k],
                    recv_sem=recv_sems.at[3 * a + k], device_id=(px, py, c), device_id_type=MESH)
                rc.start()
                remote.append(rc)
        local = _stage_out(staged, stage, [o.at[me] for o in outs], out_sems)
        for a in range(n):
            for k, (px, py) in enumerate(chips):
                pltpu.make_async_remote_copy(
                    src_ref=ins[a], dst_ref=outs[a].at[2 * px + py], send_sem=send_sems.at[3 * a + k],
                    recv_sem=recv_sems.at[3 * a + k], device_id=(px, py, c), device_id_type=MESH).wait_recv()
        for rc in remote:
            rc.wait_send()
        for cp in local:
            cp.wait()

    return pl.pallas_call(
        body, name="gather_chips", in_specs=[ANY] * n, out_specs=[ANY] * n,
        out_shape=[jax.ShapeDtypeStruct((N_CHIPS,) + s.shape, s.dtype) for s in shards],
        scratch_shapes=[pltpu.VMEM(s.shape, s.dtype) for s in shards]
        + [pltpu.SemaphoreType.DMA((3 * n,)), pltpu.SemaphoreType.DMA((3 * n,)),
           pltpu.SemaphoreType.DMA((n,)), pltpu.SemaphoreType.DMA((n,))],
        compiler_params=pltpu.CompilerParams(has_side_effects=True),
    )(*shards)


def _send_other_halves(grads, *, tag):
    n = len(grads)

    def body(*refs):
        ins, sib = refs[:n], refs[n:2 * n]
        send_sems, recv_sems = refs[2 * n:]
        x, y, c = _position()
        remote = []
        for a in range(n):
            half = ins[a].shape[1] // 2
            give = ins[a].at[:, pl.ds(pl.multiple_of((1 - c) * half, 8), half), :]
            rc = pltpu.make_async_remote_copy(
                src_ref=give, dst_ref=sib[a], send_sem=send_sems.at[a], recv_sem=recv_sems.at[a],
                device_id=(x, y, 1 - c), device_id_type=MESH)
            rc.start()
            remote.append(rc)
        for rc in remote:
            rc.wait_recv()
        for rc in remote:
            rc.wait_send()

    return pl.pallas_call(
        body, name="send_other_halves_" + tag, in_specs=[ANY] * n, out_specs=[ANY] * n,
        out_shape=[jax.ShapeDtypeStruct((g.shape[0], g.shape[1] // 2, g.shape[2]), g.dtype) for g in grads],
        scratch_shapes=[pltpu.SemaphoreType.DMA((n,)), pltpu.SemaphoreType.DMA((n,))],
        compiler_params=pltpu.CompilerParams(has_side_effects=True),
    )(*grads)


def _add_own_half(g4, sib, core, *, name):
    g, a2, c = sib.shape
    tr = _row_tile(a2, c, 4)

    def body(core_ref, a_ref, b_ref, o_ref, o16_ref):
        tot = a_ref[...] + b_ref[...]
        o_ref[...] = tot
        o16_ref[...] = tot.astype(BF16)

    blk = pl.BlockSpec((None, tr, c), lambda i, j, core_ref: (i, j, 0))
    return pl.pallas_call(
        body, name=name,
        grid_spec=pltpu.PrefetchScalarGridSpec(
            num_scalar_prefetch=1, grid=(g, a2 // tr),
            in_specs=[pl.BlockSpec((None, None, tr, c), lambda i, j, core_ref: (i, core_ref[0], j, 0)), blk],
            out_specs=[blk, blk]),
        out_shape=[jax.ShapeDtypeStruct(sib.shape, F32), jax.ShapeDtypeStruct(sib.shape, BF16)],
        compiler_params=_params(("parallel", "parallel")),
    )(core, g4.reshape(g, 2, a2, c), sib)


def _scatter_chips(parts):
    n = len(parts)

    def body(*refs):
        ins, outs = refs[:n], refs[n:2 * n]
        send_sems, recv_sems = refs[2 * n:]
        x, y, c = _position()
        me = 2 * x + y
        chips = _other_chips(x, y)
        remote = []
        for a in range(n):
            for k, (px, py) in enumerate(chips):
                rc = pltpu.make_async_remote_copy(
                    src_ref=ins[a].at[2 * px + py], dst_ref=outs[a].at[me], send_sem=send_sems.at[3 * a + k],
                    recv_sem=recv_sems.at[3 * a + k], device_id=(px, py, c), device_id_type=MESH)
                rc.start()
                remote.append(rc)
        for a in range(n):
            for k, (px, py) in enumerate(chips):
                pltpu.make_async_remote_copy(
                    src_ref=ins[a].at[me], dst_ref=outs[a].at[2 * px + py], send_sem=send_sems.at[3 * a + k],
                    recv_sem=recv_sems.at[3 * a + k], device_id=(px, py, c), device_id_type=MESH).wait_recv()
        for rc in remote:
            rc.wait_send()

    return pl.pallas_call(
        body, name="scatter_chips", in_specs=[ANY] * n, out_specs=[ANY] * n,
        out_shape=[jax.ShapeDtypeStruct(p.shape, p.dtype) for p in parts],
        scratch_shapes=[pltpu.SemaphoreType.DMA((3 * n,)), pltpu.SemaphoreType.DMA((3 * n,))],
        compiler_params=pltpu.CompilerParams(has_side_effects=True),
    )(*parts)


def _sum_chips(landed, own, chip, *, name):
    g, r, c = landed.shape
    tr = _row_tile(r, c, 5)

    def body(chip_ref, *refs):
        me = chip_ref[0]
        own_v = refs[g][...]
        tot = None
        for slot in range(g):
            term = jnp.where(me == slot, own_v, refs[slot][...].astype(F32))
            tot = term if tot is None else tot + term
        refs[g + 1][...] = tot

    def landed_spec(slot):
        return pl.BlockSpec((None, tr, c),
                            lambda i, chip_ref: (jnp.where(chip_ref[0] == slot, (slot + 1) % g, slot), i, 0))

    return pl.pallas_call(
        body, name=name,
        grid_spec=pltpu.PrefetchScalarGridSpec(
            num_scalar_prefetch=1, grid=(r // tr,),
            in_specs=[landed_spec(k) for k in range(g)]
            + [pl.BlockSpec((None, tr, c), lambda i, chip_ref: (chip_ref[0], i, 0))],
            out_specs=pl.BlockSpec((tr, c), lambda i, chip_ref: (i, 0))),
        out_shape=jax.ShapeDtypeStruct((r, c), F32), compiler_params=_params(("parallel",)),
    )(chip, *([landed] * g), own)


def _join_halves(halves):
    n = len(halves)

    def body(*refs):
        ins, outs, stage = refs[:n], refs[n:2 * n], refs[2 * n:3 * n]
        send_sems, recv_sems, in_sems, out_sems = refs[3 * n:]
        x, y, c = _position()
        remote = []
        staged = _stage_in(ins, stage, in_sems)
        for a in range(n):
            rc = pltpu.make_async_remote_copy(
                src_ref=ins[a], dst_ref=outs[a].at[c], send_sem=send_sems.at[a], recv_sem=recv_sems.at[a],
                device_id=(x, y, 1 - c), device_id_type=MESH)
            rc.start()
            remote.append(rc)
        local = _stage_out(staged, stage, [o.at[c] for o in outs], out_sems)
        for a in range(n):
            pltpu.make_async_remote_copy(
                src_ref=ins[a], dst_ref=outs[a].at[1 - c], send_sem=send_sems.at[a], recv_sem=recv_sems.at[a],
                device_id=(x, y, 1 - c), device_id_type=MESH).wait_recv()
        for rc in remote:
            rc.wait_send()
        for cp in local:
            cp.wait()

    return pl.pallas_call(
        body, name="join_halves", in_specs=[ANY] * n, out_specs=[ANY] * n,
        out_shape=[jax.ShapeDtypeStruct((2,) + h.shape, h.dtype) for h in halves],
        scratch_shapes=[pltpu.VMEM(h.shape, h.dtype) for h in halves]
        + [pltpu.SemaphoreType.DMA((n,)), pltpu.SemaphoreType.DMA((n,)), pltpu.SemaphoreType.DMA((n,)),
           pltpu.SemaphoreType.DMA((n,))],
        compiler_params=pltpu.CompilerParams(has_side_effects=True),
    )(*halves)


_HBM = pl.BlockSpec(memory_space=pltpu.HBM)
_SEM = pl.BlockSpec(memory_space=pltpu.SEMAPHORE)


def _chip_copies(srcs, lands, send_sems, recv_sems, scatter):
    x, y, c = _position()
    me = 2 * x + y
    outgoing, incoming = [], []
    for a, (src, land) in enumerate(zip(srcs, lands)):
        for k, (px, py) in enumerate(_other_chips(x, y)):
            peer = 2 * px + py
            sems = dict(send_sem=send_sems.at[3 * a + k], recv_sem=recv_sems.at[3 * a + k], device_id=(px, py, c),
                        device_id_type=MESH)
            outgoing.append(pltpu.make_async_remote_copy(
                src_ref=src.at[peer] if scatter else src, dst_ref=land.at[me], **sems))
            incoming.append(pltpu.make_async_remote_copy(
                src_ref=src.at[me] if scatter else src, dst_ref=land.at[peer], **sems))
    return outgoing, incoming


def _chips_start(srcs, *, scatter, name):
    n = len(srcs)
    lands = [lax.empty(a.shape if scatter else (N_CHIPS,) + a.shape, a.dtype) for a in srcs]

    def body(*refs):
        ins, send_sems, recv_sems, token = refs[:2 * n], refs[2 * n], refs[2 * n + 1], refs[-1]
        outgoing, _ = _chip_copies(ins[:n], ins[n:], send_sems, recv_sems, scatter)
        for cp in outgoing:
            cp.start()
        token[...] = jnp.zeros_like(token)

    bufs = list(srcs) + lands
    res = pl.pallas_call(
        body, name=name, in_specs=[_HBM] * (2 * n),
        out_specs=(_SEM, _SEM, *[_HBM] * (2 * n), pl.BlockSpec(memory_space=pltpu.VMEM)),
        out_shape=(pltpu.SemaphoreType.DMA((3 * n,)), pltpu.SemaphoreType.DMA((3 * n,)),
                   *[pltpu.HBM(a.shape, a.dtype) for a in bufs], jax.ShapeDtypeStruct((8, 128), F32)),
        input_output_aliases={i: 2 + i for i in range(2 * n)},
        compiler_params=pltpu.CompilerParams(has_side_effects=pltpu.SideEffectType.DATAFLOW_SIDE_EFFECTING),
    )(*[pltpu.with_memory_space_constraint(a, pltpu.HBM) for a in bufs])
    return res[0], res[1], list(res[2:2 + n]), list(res[2 + n:2 + 2 * n]), res[-1]


def _chips_wait(handle, after, *, scatter, name):
    send_sems, recv_sems, srcs, lands, _ = handle
    n = len(srcs)

    def body(*refs):
        ins, send_ref, recv_ref = refs[:2 * n], refs[2 * n], refs[2 * n + 1]
        outgoing, incoming = _chip_copies(ins[:n], ins[n:], send_ref, recv_ref, scatter)
        for cp in outgoing:
            cp.wait_send()
        for cp in incoming:
            cp.wait_recv()

    bufs = list(srcs) + list(lands)
    res = pl.pallas_call(
        body, name=name, in_specs=[_HBM] * (2 * n) + [_SEM, _SEM, ANY], out_specs=tuple([_HBM] * (2 * n)),
        out_shape=tuple(pltpu.HBM(a.shape, a.dtype) for a in bufs),
        input_output_aliases={i: i for i in range(2 * n)},
        compiler_params=pltpu.CompilerParams(has_side_effects=pltpu.SideEffectType.DATAFLOW_SIDE_EFFECTING),
    )(*bufs, send_sems, recv_sems, after)
    return list(res[:n]), list(res[n:])


def _gather_all(buf):
    def body(in_ref, out_ref, send_sems, recv_sems, local_sem):
        x, y, c = _position()
        me = 4 * x + 2 * y + c
        peers = [(x, y, 1 - c)] + [(px, py, pc) for (px, py) in _other_chips(x, y) for pc in (c, 1 - c)]
        cp = pltpu.make_async_copy(in_ref, out_ref.at[me], local_sem)
        cp.start()
        remote = []
        for k, peer in enumerate(peers):
            rc = pltpu.make_async_remote_copy(
                src_ref=in_ref, dst_ref=out_ref.at[me], send_sem=send_sems.at[k], recv_sem=recv_sems.at[k],
                device_id=peer, device_id_type=MESH)
            rc.start()
            remote.append(rc)
        for k, (px, py, pc) in enumerate(peers):
            pltpu.make_async_remote_copy(
                src_ref=in_ref, dst_ref=out_ref.at[4 * px + 2 * py + pc], send_sem=send_sems.at[k],
                recv_sem=recv_sems.at[k], device_id=(px, py, pc), device_id_type=MESH).wait_recv()
        for rc in remote:
            rc.wait_send()
        cp.wait()

    return pl.pallas_call(
        body, name="gather_all", in_specs=[ANY], out_specs=ANY,
        out_shape=jax.ShapeDtypeStruct((N_DEV,) + buf.shape, buf.dtype),
        scratch_shapes=[pltpu.SemaphoreType.DMA((N_DEV - 1,)), pltpu.SemaphoreType.DMA((N_DEV - 1,)),
                        pltpu.SemaphoreType.DMA],
        compiler_params=pltpu.CompilerParams(has_side_effects=True),
    )(buf)


def _cols_from_shards(g4):
    _, k, n = g4.shape
    return g4.transpose(1, 0, 2).reshape(k, N_CHIPS * n)


def _cols_to_shards(w):
    k, n = w.shape
    return w.reshape(k, N_CHIPS, n // N_CHIPS).transpose(1, 0, 2)


def _pad_heads(w, width):
    k = w.shape[0]
    w3 = w.reshape(k, N_HEADS, width)
    return jnp.pad(w3, ((0, 0), (0, 0), (0, HEAD_PAD - width))).reshape(k, D_ATT)


def _unpad_heads(w, width):
    k = w.shape[0]
    return w.reshape(k, N_HEADS, HEAD_PAD)[:, :, :width]


def _rope_tables(s):
    pos = jnp.arange(s, dtype=F32)
    inv_freq = ROPE_THETA ** (-jnp.arange(0, QK_ROPE, 2, dtype=F32) / QK_ROPE)
    ang = pos[:, None] * inv_freq[None, :]
    cos_h, sin_h = jnp.cos(ang), jnp.sin(ang)
    half = QK_ROPE // 2
    z = jnp.zeros((s, half), F32)
    ones = jnp.ones((s, QK_NOPE), F32)
    tail = jnp.zeros((s, HEAD_PAD - QK_NOPE - QK_ROPE), F32)
    cos = jnp.concatenate([ones, cos_h, cos_h, tail + 1.0], axis=1)
    sin_a = jnp.concatenate([ones * 0.0, -sin_h, z, tail], axis=1)
    sin_b = jnp.concatenate([ones * 0.0, z, sin_h, tail], axis=1)
    return cos, sin_a, sin_b


def _local_step(x, p, target, wts, late_weights, reduce_early):
    s = x.shape[0]
    cos, sin_a, sin_b = _rope_tables(s)
    g1, gq, gkv, g2, g3, gf = (wts[k] for k in ("norm_mix_g", "q_norm_g", "kv_norm_g", "norm_ffn_g", "ple_norm_g",
                                                 "final_norm_g"))
    w_in_p, w_uq_p, w_kv_p = wts["w_in_p"], wts["w_uq_p"], wts["w_kv_p"]
    conv_w8, fconv_w, fconv_b = wts["conv_w8"], wts["ffn_conv_w"], wts["ffn_conv_b"]

    (h, z), _ = _mm_fused(x, w_in_p, name="mm_in", prologue=_pro_rms, vecs=[g1], epilogue=_epi_plain, row_outs=[F32])
    y_conv, qn, kvn, kr = _mix_pre(z, conv_w8, gq, gkv, cos, sin_a, sin_b)
    q, k, v = _qkv_proj(qn, kvn, kr, w_uq_p, w_kv_p, cos, sin_a, sin_b)
    o, lse = _flash_fwd(q, k, v)
    late = late_weights(lse)
    w_o_a, w_o_b, w_up, w_down = late["w_o_a"], late["w_o_b"], late["w_up"], late["w_down"]
    w_pg, w_pp = late["w_ple_gate"], late["w_ple_proj"]
    t = _mm(y_conv, w_o_a, add=x, name="mm_o_conv", tm=512, tn=1024, tk=512)
    (x1, hf), _ = _mm_fused(o, w_o_b, name="mm_o_att", rows=[t], vecs=[g2], epilogue=_epi_add_rms,
                            row_outs=[F32, BF16])
    a_pre, act = _ffn_fwd(hf, w_up, fconv_w, fconv_b)
    (x2, n3), _ = _mm_fused(act, w_down, name="mm_down", rows=[x1], vecs=[g3], epilogue=_epi_add_rms,
                            row_outs=[F32, BF16])
    loss, dx3, dgl, dpp, d_gf = _ple_final(x2, n3, p, target, gf, w_pg, w_pp)

    grads, early = {"final_norm_g": d_gf}, {}
    early["w_ple_proj"] = _mm(p, dpp, ta=True, name="mm_d_wpp", tm=256, tn=1024, tk=2048)
    early["w_ple_gate"] = _mm(n3, dgl, ta=True, name="mm_d_wpg", tm=512, tn=1024, tk=2048)
    (dx2,), (grads["ple_norm_g"],) = _mm_fused(dgl, w_pg, tb=True, name="mm_d_n3", rows=[x2, dx3], vecs=[g3],
                                               epilogue=_epi_rms_bwd, row_outs=[F32], n_vec_out=1)
    early["w_down"] = _mm(act, dx2, ta=True, name="mm_d_wdown", tm=1408, tn=512, tk=2048)
    da_pre, grads["ffn_conv_w"], grads["ffn_conv_b"] = _ffn_bwd(dx2, w_down, a_pre, fconv_w, fconv_b)
    early["w_up"] = _mm(hf, da_pre, ta=True, b_split=True, name="mm_d_wup", tm=512, tn=1408, tk=2048)
    (dx1,), (grads["norm_ffn_g"],) = _mm_fused(da_pre, w_up, tb=True, a_split=True, name="mm_d_hf", rows=[x1, dx2],
                                               vecs=[g2], epilogue=_epi_rms_bwd, row_outs=[F32], n_vec_out=1, tk=D_FF)
    d_wo_a = _mm(y_conv, dx1, ta=True, name="mm_d_wo_conv", tm=512, tn=1024, tk=2048)
    d_wo_b = _mm(o, dx1, ta=True, name="mm_d_wo_att", tm=512, tn=1024, tk=2048)
    early["w_o"] = jnp.concatenate([d_wo_a, d_wo_b.reshape(N_HEADS, HEAD_PAD, D_MODEL)[:, :V_HEAD]
                                    .reshape(N_HEADS * V_HEAD, D_MODEL)], axis=0)
    token, finish = reduce_early(early)
    dyc = _mm(dx1, w_o_a, tb=True, name="mm_d_yconv", tm=512, tn=512, tk=1024)
    do = _mm(dx1, w_o_b, tb=True, out_dtype=BF16, name="mm_d_o", tm=512, tn=1024, tk=1024)
    delta = _attn_delta(do, o)
    dq, dk, dv = _flash_bwd(q, q.T, k, v, do, do.T, lse, delta, token)
    reduced_early = finish(dq)
    dq_pre, dkr = _qk_bwd(dq, dk, cos, sin_a, sin_b)
    grads["w_uq_p"] = _mm(qn, dq_pre, ta=True, name="mm_d_wuq", tm=256, tn=1024, tk=2048)
    dqn = _mm(dq_pre, w_uq_p, tb=True, name="mm_d_qn", tm=512, tn=256, tk=1024)
    grads["w_k_p"] = _mm(kvn, dk, ta=True, name="mm_d_wk", tm=128, tn=1024, tk=2048)
    grads["w_v_p"] = _mm(kvn, dv, ta=True, name="mm_d_wv", tm=128, tn=1024, tk=2048)
    dkvn_k = _mm(dk, w_kv_p[:, :D_ATT], tb=True, name="mm_d_kvn_k", tm=512, tn=128, tk=1024)
    dkvn = _mm(dv, w_kv_p[:, D_ATT:], tb=True, add=dkvn_k, name="mm_d_kvn_v", tm=512, tn=128, tk=1024)
    dz, grads["conv_w"], grads["q_norm_g"], grads["kv_norm_g"] = _mix_bwd(
        z, dyc, dqn, dkvn, dkr, conv_w8, gq, gkv, cos, sin_a, sin_b)
    grads["w_in_p"] = _mm(h, dz, ta=True, name="mm_d_win", tm=512, tn=1024, tk=2048)
    (grad_x,), (grads["norm_mix_g"],) = _mm_fused(dz, w_in_p, tb=True, name="mm_d_h", rows=[x, dx1], vecs=[g1],
                                                  epilogue=_epi_rms_bwd, row_outs=[F32], n_vec_out=1)
    return loss[0, 0], grad_x, grads, reduced_early


_EARLY_W = ("w_in", "w_uq", "w_ukv")
_LATE_W = ("w_o", "w_up", "w_down", "w_ple_gate", "w_ple_proj")
_BIG = _EARLY_W + _LATE_W
_COL_SHARDED = ("w_in", "w_uq", "w_ukv", "w_up", "w_ple_proj")
_SMALL = ("norm_mix_g", "conv_w", "q_norm_g", "kv_norm_g", "norm_ffn_g", "ffn_conv_w", "ffn_conv_b", "ple_norm_g",
          "final_norm_g")


def _full_from_slots(n, g4):
    return _cols_from_shards(g4) if n in _COL_SHARDED else g4.reshape(-1, g4.shape[2])


def _shard_major(n, g):
    return _cols_to_shards(g) if n in _COL_SHARDED else g.reshape(N_CHIPS, g.shape[0] // N_CHIPS, g.shape[1])


def _early_weights(w):
    shards = [w[n][0].astype(BF16) for n in _EARLY_W]
    shards.append(jnp.pad(w["conv_w"][0], ((0, 5), (0, 0))))
    shards.append(jnp.pad(w["ffn_conv_w"][0], ((0, 5), (0, 0))))
    got = _gather_chips(shards)
    full = {n: _full_from_slots(n, g4) for n, g4 in zip(_EARLY_W, got)}
    full["conv_w8"] = _cols_from_shards(got[len(_EARLY_W)])
    full["ffn_conv_w8"] = _cols_from_shards(got[len(_EARLY_W) + 1])
    return _layout_early(full, w)


def _layout_early(full, w):
    out = {n: w[n] for n in ("norm_mix_g", "q_norm_g", "kv_norm_g", "norm_ffn_g", "ple_norm_g")}
    out["final_norm_g"] = w["final_norm_g"][None, :]
    w_in = full["w_in"]
    zc = jnp.zeros((D_MODEL, QK_NOPE), BF16)
    zt = jnp.zeros((D_MODEL, HEAD_PAD - QK_NOPE - QK_ROPE), BF16)
    out["w_in_p"] = jnp.concatenate([w_in[:, :D_IN - QK_ROPE], zc, w_in[:, D_IN - QK_ROPE:], zt], axis=1)
    out["w_uq_p"] = _pad_heads(full["w_uq"], QK_NOPE + QK_ROPE)
    kv3 = full["w_ukv"].reshape(KV_LORA, N_HEADS, QK_NOPE + V_HEAD)
    out["w_kv_p"] = jnp.concatenate([_pad_heads(kv3[:, :, :QK_NOPE].reshape(KV_LORA, -1), QK_NOPE),
                                     _pad_heads(kv3[:, :, QK_NOPE:].reshape(KV_LORA, -1), V_HEAD)], axis=1)
    out["conv_w8"] = full["conv_w8"]
    fw = full["ffn_conv_w8"]
    out["ffn_conv_w"] = jnp.stack([fw[:, :D_FF], fw[:, D_FF:]])
    out["ffn_conv_b"] = w["ffn_conv_b"].reshape(2, 1, D_FF)
    return out


def _layout_late(full):
    w_o = full["w_o"]
    out = {"w_o_a": w_o[:CONV_WIDTH]}
    out["w_o_b"] = jnp.pad(w_o[CONV_WIDTH:].reshape(N_HEADS, V_HEAD, D_MODEL),
                           ((0, 0), (0, HEAD_PAD - V_HEAD), (0, 0))).reshape(D_ATT, D_MODEL)
    for n in ("w_up", "w_down", "w_ple_gate", "w_ple_proj"):
        out[n] = full[n]
    return out


def _true_gradients(g):
    out = {}
    wp = g["w_in_p"]
    out["w_in"] = jnp.concatenate([wp[:, :D_IN - QK_ROPE], wp[:, D_IN_PAD - HEAD_PAD + QK_NOPE:
                                                              D_IN_PAD - HEAD_PAD + QK_NOPE + QK_ROPE]], axis=1)
    out["w_uq"] = _unpad_heads(g["w_uq_p"], QK_NOPE + QK_ROPE).reshape(Q_LORA, -1)
    out["w_ukv"] = jnp.concatenate([_unpad_heads(g["w_k_p"], QK_NOPE), _unpad_heads(g["w_v_p"], V_HEAD)],
                                   axis=2).reshape(KV_LORA, -1)
    out["conv_w"] = g["conv_w"]
    fw = g["ffn_conv_w"]
    out["ffn_conv_w"] = jnp.concatenate([fw[0, :3], fw[1, :3]], axis=1)
    out["ffn_conv_b"] = g["ffn_conv_b"].reshape(1, 2 * D_FF)
    for n in ("norm_mix_g", "q_norm_g", "kv_norm_g", "norm_ffn_g", "ple_norm_g", "final_norm_g"):
        out[n] = g[n]
    return out


def _chip_partials(names, g, core, *, tag):
    g4 = [_shard_major(n, g[n]) for n in names]
    sib = _send_other_halves(g4, tag=tag)
    return [_add_own_half(a, b, core, name="add_cores_" + n) for n, a, b in zip(names, g4, sib)]


_SMALL_SIZES = {"norm_mix_g": D_MODEL, "conv_w": 3 * CONV_WIDTH, "q_norm_g": Q_LORA, "kv_norm_g": KV_LORA,
                "norm_ffn_g": D_MODEL, "ffn_conv_w": 6 * D_FF, "ffn_conv_b": 2 * D_FF, "ple_norm_g": D_MODEL,
                "final_norm_g": D_MODEL}


def _pack(parts, rows):
    flat = jnp.concatenate([a.reshape(-1) for a in parts])
    return jnp.pad(flat, (0, rows * 128 - flat.shape[0])).reshape(rows, 128)


def _unpack(buf, sizes):
    flat = buf.reshape(-1)
    out, at = [], 0
    for n in sizes:
        out.append(flat[at:at + n])
        at += n
    return out


def _reduce_small(g):
    total = sum(_SMALL_SIZES[n] for n in _SMALL)
    rows = -(-total // 1024) * 8
    slots = _gather_all(_pack([g[n] for n in _SMALL], rows))
    summed = _sum_slots(slots, name="sum_small")
    return dict(zip(_SMALL, _unpack(summed, [_SMALL_SIZES[n] for n in _SMALL])))


def kernel(x, p, norm_mix_g, w_in, conv_w, q_norm_g, w_uq, kv_norm_g, w_ukv, w_o, norm_ffn_g, w_up, ffn_conv_w, ffn_conv_b, w_down, ple_norm_g, w_ple_gate, w_ple_proj, final_norm_g, loss_target, m_norm_mix_g, m_w_in, m_conv_w, m_q_norm_g, m_w_uq, m_kv_norm_g, m_w_ukv, m_w_o, m_norm_ffn_g, m_w_up, m_ffn_conv_w, m_ffn_conv_b, m_w_down, m_ple_norm_g, m_w_ple_gate, m_w_ple_proj, m_final_norm_g, v_norm_mix_g, v_w_in, v_conv_w, v_q_norm_g, v_w_uq, v_kv_norm_g, v_w_ukv, v_w_o, v_norm_ffn_g, v_w_up, v_ffn_conv_w, v_ffn_conv_b, v_w_down, v_ple_norm_g, v_w_ple_gate, v_w_ple_proj, v_final_norm_g):
    names = ["norm_mix_g", "w_in", "conv_w", "q_norm_g", "w_uq", "kv_norm_g", "w_ukv", "w_o", "norm_ffn_g", "w_up",
             "ffn_conv_w", "ffn_conv_b", "w_down", "ple_norm_g", "w_ple_gate", "w_ple_proj", "final_norm_g"]
    w = dict(zip(names, (norm_mix_g, w_in, conv_w, q_norm_g, w_uq, kv_norm_g, w_ukv, w_o, norm_ffn_g, w_up,
                         ffn_conv_w, ffn_conv_b, w_down, ple_norm_g, w_ple_gate, w_ple_proj, final_norm_g)))
    m = dict(zip(names, (m_norm_mix_g, m_w_in, m_conv_w, m_q_norm_g, m_w_uq, m_kv_norm_g, m_w_ukv, m_w_o,
                         m_norm_ffn_g, m_w_up, m_ffn_conv_w, m_ffn_conv_b, m_w_down, m_ple_norm_g, m_w_ple_gate,
                         m_w_ple_proj, m_final_norm_g)))
    v = dict(zip(names, (v_norm_mix_g, v_w_in, v_conv_w, v_q_norm_g, v_w_uq, v_kv_norm_g, v_w_ukv, v_w_o,
                         v_norm_ffn_g, v_w_up, v_ffn_conv_w, v_ffn_conv_b, v_w_down, v_ple_norm_g, v_w_ple_gate,
                         v_w_ple_proj, v_final_norm_g)))

    core = lax.axis_index("c").astype(jnp.int32).reshape(1)
    chip = (2 * lax.axis_index("x") + lax.axis_index("y")).astype(jnp.int32).reshape(1)

    wts = _early_weights(w)
    gather = _chips_start([w[n][0].astype(BF16) for n in _LATE_W], scatter=False, name="gather_late_start")
    wts["norm_mix_g"] = wts["norm_mix_g"] + gather[4][0, 0]

    def late_weights(after):
        shards, landed = _chips_wait(gather, after, scatter=False, name="gather_late_wait")
        full = {n: _full_from_slots(n, lax.dynamic_update_slice(g4, own[None], (chip[0], 0, 0)))
                for n, own, g4 in zip(_LATE_W, shards, landed)}
        return _layout_late(full)

    def reduce_early(g):
        parts = _chip_partials(_LATE_W, g, core, tag="early")
        scatter = _chips_start([t16 for _, t16 in parts], scatter=True, name="scatter_early_start")

        def finish(after):
            _, landed = _chips_wait(scatter, after, scatter=True, name="scatter_early_wait")
            return [_sum_chips(a, t32, chip, name="sum_chips_" + n) for n, a, (t32, _) in zip(_LATE_W, landed, parts)]

        return scatter[4], finish

    loss, grad_x, padded, halves_early = _local_step(x[0], p[0, 0], loss_target[0], wts, late_weights, reduce_early)
    g_full = _true_gradients(padded)
    loss = lax.psum(loss, ("x", "y", "c"))

    parts = _chip_partials(_EARLY_W, g_full, core, tag="late")
    landed = _scatter_chips([t16 for _, t16 in parts])
    halves = [_sum_chips(a, t32, chip, name="sum_chips_" + n) for n, a, (t32, _) in zip(_EARLY_W, landed, parts)]
    whole = _join_halves(halves + halves_early)
    big = {n: a.reshape(-1, a.shape[2]) for n, a in zip(_BIG, whole)}

    g_out, d_out, m_out, v_out = {}, {}, {}, {}
    for n in _BIG:
        shape = w[n].shape
        g = big[n]
        d, mn, vn = _adamw(w[n][0], g, m[n][0], v[n][0], name="adamw_" + n)
        g_out[n], d_out[n], m_out[n], v_out[n] = (a.reshape(shape) for a in (g, d, mn, vn))

    small = _reduce_small(g_full)
    chip = 2 * lax.axis_index("x") + lax.axis_index("y")
    g_small = {}
    for n in _SMALL:
        shape = w[n].shape
        g = small[n]
        if n in ("conv_w", "ffn_conv_w"):
            width = shape[-1]
            g = lax.dynamic_slice(g.reshape(3, N_CHIPS * width), (0, chip * width), (3, width))
        g_small[n] = g.reshape(shape)
    sizes = [g_small[n].size for n in _SMALL]
    rows = -(-sum(sizes) // 1024) * 8
    packed = [_pack([src[n] for n in _SMALL], rows) for src in (w, g_small, m, v)]
    d_s, m_s, v_s = _adamw(*packed, name="adamw_small")
    for n, d, mn, vn in zip(_SMALL, _unpack(d_s, sizes), _unpack(m_s, sizes), _unpack(v_s, sizes)):
        shape = w[n].shape
        g_out[n], d_out[n], m_out[n], v_out[n] = g_small[n], d.reshape(shape), mn.reshape(shape), vn.reshape(shape)

    return (loss, grad_x[None], *[g_out[n] for n in names], *[d_out[n] for n in names],
            *[m_out[n] for n in names], *[v_out[n] for n in names])
```

```python
import functools

import jax
import jax.numpy as jnp
from jax import lax
from jax.experimental import pallas as pl
from jax.experimental.pallas import tpu as pltpu

F32 = jnp.float32
BF16 = jnp.bfloat16

D_MODEL = 1024
CONV_WIDTH = 512
Q_LORA = 256
KV_LORA = 128
QK_NOPE = 64
QK_ROPE = 32
V_HEAD = 64
N_HEADS = 8
HEAD_PAD = 128
D_ATT = N_HEADS * HEAD_PAD
D_IN = 3 * CONV_WIDTH + Q_LORA + KV_LORA + QK_ROPE
D_IN_PAD = 3 * CONV_WIDTH + Q_LORA + KV_LORA + HEAD_PAD
D_FF = 2816
ROPE_THETA = 10000.0
EPS = 1e-6
SM_SCALE = (QK_NOPE + QK_ROPE) ** -0.5
ONES_LANE = V_HEAD

ADAM_LR = 0.001
ADAM_B1 = 0.9
ADAM_B2 = 0.999
ADAM_EPS = 1e-08
ADAM_WD = 0.01
ADAM_STEP = 10

N_CHIPS = 4
N_DEV = 8
MESH = pl.DeviceIdType.MESH
ANY = pl.BlockSpec(memory_space=pl.ANY)


def _params(sem):
    return pltpu.CompilerParams(dimension_semantics=sem)


def _mm(a, b, *, name, ta=False, tb=False, add=None, out_dtype=F32, tm=512, tn=512, tk=512,
        a_split=False, b_split=False, o_split=False, o_shards=False):
    if a_split:
        _, m, kh = a.shape
        k = 2 * kh
    elif ta:
        k, m = a.shape
    else:
        m, k = a.shape
    if b_split:
        _, kb, nh = b.shape
        n = 2 * nh
    elif tb:
        n, kb = b.shape
    else:
        kb, n = b.shape
    assert kb == k, (name, a.shape, b.shape)
    tm, tn, tk = min(tm, m), min(tn, n), min(tk, k)
    assert m % tm == 0 and n % tn == 0 and k % tk == 0, (name, m, n, k, tm, tn, tk)
    gm, gn, gk = m // tm, n // tn, k // tk

    if a_split:
        assert gk % 2 == 0
        a_spec = pl.BlockSpec((None, tm, tk), lambda i, j, kk: (kk // (gk // 2), i, kk % (gk // 2)))
    elif ta:
        a_spec = pl.BlockSpec((tk, tm), lambda i, j, kk: (kk, i))
    else:
        a_spec = pl.BlockSpec((tm, tk), lambda i, j, kk: (i, kk))
    if b_split:
        assert gn % 2 == 0
        b_spec = pl.BlockSpec((None, tk, tn), lambda i, j, kk: (j // (gn // 2), kk, j % (gn // 2)))
    elif tb:
        b_spec = pl.BlockSpec((tn, tk), lambda i, j, kk: (j, kk))
    else:
        b_spec = pl.BlockSpec((tk, tn), lambda i, j, kk: (kk, j))
    if o_shards:
        o_spec = pl.BlockSpec((None, tm, tn), lambda i, j, kk: (j, i, 0))
        o_shape = jax.ShapeDtypeStruct((gn, m, tn), out_dtype)
    elif o_split:
        assert gn % 2 == 0
        o_spec = pl.BlockSpec((None, tm, tn), lambda i, j, kk: (j // (gn // 2), i, j % (gn // 2)))
        o_shape = jax.ShapeDtypeStruct((2, m, n // 2), out_dtype)
    else:
        o_spec = pl.BlockSpec((tm, tn), lambda i, j, kk: (i, j))
        o_shape = jax.ShapeDtypeStruct((m, n), out_dtype)
    dims = (((0 if ta else 1,), (1 if tb else 0,)), ((), ()))

    def body(*refs):
        a_ref, b_ref = refs[:2]
        add_ref = None if add is None else refs[2]
        o_ref = refs[2 if add is None else 3]

        def finish(r):
            if add_ref is not None:
                r = r + add_ref[...]
            o_ref[...] = r.astype(o_ref.dtype)

        part = lax.dot_general(a_ref[...].astype(BF16), b_ref[...].astype(BF16), dims, preferred_element_type=F32)
        if gk == 1:
            finish(part)
            return
        acc_ref = refs[-1]
        kk = pl.program_id(2)

        @pl.when(kk == 0)
        def _():
            acc_ref[...] = part

        @pl.when((kk > 0) & (kk < gk - 1))
        def _():
            acc_ref[...] += part

        @pl.when(kk == gk - 1)
        def _():
            finish(acc_ref[...] + part)

    in_specs = [a_spec, b_spec]
    args = [a, b]
    if add is not None:
        in_specs.append(pl.BlockSpec((tm, tn), lambda i, j, kk: (i, j)))
        args.append(add)
    return pl.pallas_call(
        body, name=name, grid=(gm, gn, gk), in_specs=in_specs, out_specs=o_spec, out_shape=o_shape,
        scratch_shapes=[] if gk == 1 else [pltpu.VMEM((tm, tn), F32)],
        compiler_params=_params(("parallel", "parallel", "arbitrary")),
    )(*args)


def _rms_scale(v):
    return lax.rsqrt(jnp.mean(v * v, axis=-1, keepdims=True) + EPS)


def _rms_bwd_rows(v, g, dy):
    r = _rms_scale(v)
    vh = v * r
    dyg = dy * g
    dv = r * (dyg - vh * jnp.mean(dyg * vh, axis=-1, keepdims=True))
    return dv, dy * vh


def _shift_down(v, first_row):
    row = lax.broadcasted_iota(jnp.int32, v.shape, 0)
    return jnp.where(row == 0, first_row, pltpu.roll(v, 1, 0))


def _shift_up(v, last_row):
    n = v.shape[0]
    row = lax.broadcasted_iota(jnp.int32, v.shape, 0)
    return jnp.where(row == n - 1, last_row, pltpu.roll(v, n - 1, 0))


def _rope(t, cos, sin_a, sin_b):
    return t * cos + pltpu.roll(t, HEAD_PAD - 16, 1) * sin_a + pltpu.roll(t, 16, 1) * sin_b


def _rope_bwd(d, cos, sin_a, sin_b):
    return d * cos + pltpu.roll(d * sin_a, 16, 1) + pltpu.roll(d * sin_b, HEAD_PAD - 16, 1)


def _sigmoid(v):
    return 1.0 / (1.0 + jnp.exp(-v))


def _halo_specs(ts, s, width, col):
    nb = ts // 8
    last = s // 8 - 1
    prev = pl.BlockSpec((8, width), lambda i: (jnp.maximum(i * nb - 1, 0), col))
    nxt = pl.BlockSpec((8, width), lambda i: (jnp.minimum((i + 1) * nb, last), col))
    return prev, nxt


def _mm_fused(a, b, *, name, epilogue, row_outs, rows=(), vecs=(), n_vec_out=0, tb=False, a_split=False,
              prologue=None, second=None, transposed_out=None, tm=512, tk=None):
    if a_split:
        _, m, kh = a.shape
        k = 2 * kh
    else:
        m, k = a.shape
    n = b.shape[0] if tb else b.shape[1]
    assert (b.shape[1] if tb else b.shape[0]) == k, (name, a.shape, b.shape)
    tk = k if tk is None else tk
    assert m % tm == 0 and k % tk == 0, (name, m, k, tm, tk)
    gm, gk = m // tm, k // tk
    assert prologue is None or gk == 1
    nr, nv = len(rows), len(vecs)
    n_pro = 0 if prologue is None else 1
    n_sec = 0 if second is None else 2
    n_t = 0 if transposed_out is None else 1
    dims = (((1,), (1 if tb else 0,)), ((), ()))

    def body(*refs):
        a_ref, b_ref = refs[:2]
        sec_refs = refs[2:2 + n_sec]
        row_refs, vec_refs = refs[2 + n_sec:2 + n_sec + nr], refs[2 + n_sec + nr:2 + n_sec + nr + nv]
        outs = refs[2 + n_sec + nr + nv:]
        row_out_refs = outs[n_pro:n_pro + len(row_outs)]
        t_out_refs = outs[n_pro + len(row_outs):n_pro + len(row_outs) + n_t]
        vec_out_refs = outs[n_pro + len(row_outs) + n_t:n_pro + len(row_outs) + n_t + n_vec_out]
        i, kk = pl.program_id(0), pl.program_id(1)
        vec_vals = [v[...] for v in vec_refs]
        if prologue is None:
            lhs = a_ref[...].astype(BF16)
        else:
            lhs = prologue(a_ref[...], vec_vals)
            outs[0][...] = lhs

        def finish(r):
            if second is not None:
                r = r + jnp.dot(sec_refs[0][...].astype(BF16), sec_refs[1][...].astype(BF16),
                                preferred_element_type=F32)
            row_vals, vec_parts = epilogue(r, [x[...] for x in row_refs], vec_vals)
            for ref, val in zip(row_out_refs, row_vals):
                ref[...] = val.astype(ref.dtype)
            for ref in t_out_refs:
                ref[...] = row_vals[0].T.astype(ref.dtype)
            if n_vec_out:
                @pl.when(i == 0)
                def _():
                    for ref in vec_out_refs:
                        ref[...] = jnp.zeros_like(ref)

                for ref, val in zip(vec_out_refs, vec_parts):
                    ref[...] += val

        part = lax.dot_general(lhs, b_ref[...].astype(BF16), dims, preferred_element_type=F32)
        if gk == 1:
            finish(part)
            return
        acc_ref = refs[-1]

        @pl.when(kk == 0)
        def _():
            acc_ref[...] = part

        @pl.when((kk > 0) & (kk < gk - 1))
        def _():
            acc_ref[...] += part

        @pl.when(kk == gk - 1)
        def _():
            finish(acc_ref[...] + part)

    if a_split:
        assert gk % 2 == 0
        a_spec = pl.BlockSpec((None, tm, tk), lambda i, kk: (kk // (gk // 2), i, kk % (gk // 2)))
    else:
        a_spec = pl.BlockSpec((tm, tk), lambda i, kk: (i, kk))
    b_spec = pl.BlockSpec((n, tk), lambda i, kk: (0, kk)) if tb else pl.BlockSpec((tk, n), lambda i, kk: (kk, 0))
    row_spec = pl.BlockSpec((tm, n), lambda i, kk: (i, 0))
    out_specs, out_shape = [], []
    if prologue is not None:
        out_specs.append(pl.BlockSpec((tm, k), lambda i, kk: (i, 0)))
        out_shape.append(jax.ShapeDtypeStruct((m, k), BF16))
    out_specs += [row_spec] * len(row_outs)
    out_shape += [jax.ShapeDtypeStruct((m, n), dt) for dt in row_outs]
    if transposed_out is not None:
        out_specs.append(pl.BlockSpec((n, tm), lambda i, kk: (0, i)))
        out_shape.append(jax.ShapeDtypeStruct((n, m), transposed_out))
    out_specs += [pl.BlockSpec((1, n), lambda i, kk: (0, 0))] * n_vec_out
    out_shape += [jax.ShapeDtypeStruct((1, n), F32)] * n_vec_out
    sec_specs, sec_args = [], []
    if second is not None:
        k2 = second[0].shape[1]
        sec_specs = [pl.BlockSpec((tm, k2), lambda i, kk: (i, 0)), pl.BlockSpec((k2, n), lambda i, kk: (0, 0))]
        sec_args = list(second)
    res = pl.pallas_call(
        body, name=name, grid=(gm, gk),
        in_specs=[a_spec, b_spec] + sec_specs + [row_spec] * nr
        + [pl.BlockSpec((1, v.shape[1]), lambda i, kk: (0, 0)) for v in vecs],
        out_specs=out_specs, out_shape=out_shape,
        scratch_shapes=[] if gk == 1 else [pltpu.VMEM((tm, n), F32)],
        compiler_params=_params(("arbitrary" if n_vec_out else "parallel", "arbitrary")),
    )(a, b, *sec_args, *rows, *vecs)
    split = n_pro + len(row_outs) + n_t
    return list(res[:split]), list(res[split:])


def _pro_rms(a, vecs):
    return (a * _rms_scale(a) * vecs[0]).astype(BF16)


def _epi_plain(r, rows, vecs):
    return [r], []


def _epi_add_rms(r, rows, vecs):
    xn = r + rows[0]
    return [xn, xn * _rms_scale(xn) * vecs[0]], []


def _epi_rms_bwd(r, rows, vecs):
    dv, dg_rows = _rms_bwd_rows(rows[0], vecs[0], r)
    return [dv + rows[1]], [jnp.sum(dg_rows, axis=0, keepdims=True)]


def _mix_pre(z, conv_w8, gq, gkv, cos, sin_a, sin_b, *, ts=256):
    s = z.shape[0]
    n = s // ts
    cw = CONV_WIDTH

    def body(z_ref, xcp, xcn, cgp, cgn, w_ref, gq_ref, gkv_ref, cos_ref, sa_ref, sb_ref,
             yc_ref, qn_ref, kvn_ref, kr_ref):
        i = pl.program_id(0)
        xc = z_ref[:, 0:cw]
        bg = z_ref[:, cw:2 * cw]
        cg = z_ref[:, 2 * cw:3 * cw]
        m = cg * xc
        m_prev = jnp.where(i > 0, xcp[7:8, :] * cgp[7:8, :], 0.0)
        m_next = jnp.where(i < n - 1, xcn[0:1, :] * cgn[0:1, :], 0.0)
        cm = _shift_down(m, m_prev) * w_ref[0:1, :] + m * w_ref[1:2, :] + _shift_up(m, m_next) * w_ref[2:3, :]
        yc_ref[...] = (bg * cm).astype(BF16)
        ql = z_ref[:, 3 * cw:3 * cw + Q_LORA]
        qn_ref[...] = (ql * _rms_scale(ql) * gq_ref[...]).astype(BF16)
        kvl = z_ref[:, 3 * cw + Q_LORA:3 * cw + Q_LORA + KV_LORA]
        kvn_ref[...] = (kvl * _rms_scale(kvl) * gkv_ref[...]).astype(BF16)
        kr_ref[...] = _rope(z_ref[:, D_IN_PAD - HEAD_PAD:D_IN_PAD], cos_ref[...], sa_ref[...], sb_ref[...])

    xcp, xcn = _halo_specs(ts, s, cw, 0)
    cgp, cgn = _halo_specs(ts, s, cw, 2)
    tab = pl.BlockSpec((ts, HEAD_PAD), lambda i: (i, 0))
    return pl.pallas_call(
        body, name="mix_pre", grid=(n,),
        in_specs=[pl.BlockSpec((ts, D_IN_PAD), lambda i: (i, 0)), xcp, xcn, cgp, cgn,
                  pl.BlockSpec((8, cw), lambda i: (0, 0)), pl.BlockSpec((1, Q_LORA), lambda i: (0, 0)),
                  pl.BlockSpec((1, KV_LORA), lambda i: (0, 0)), tab, tab, tab],
        out_specs=[pl.BlockSpec((ts, cw), lambda i: (i, 0)), pl.BlockSpec((ts, Q_LORA), lambda i: (i, 0)),
                   pl.BlockSpec((ts, KV_LORA), lambda i: (i, 0)), tab],
        out_shape=[jax.ShapeDtypeStruct((s, cw), BF16), jax.ShapeDtypeStruct((s, Q_LORA), BF16),
                   jax.ShapeDtypeStruct((s, KV_LORA), BF16), jax.ShapeDtypeStruct((s, HEAD_PAD), F32)],
        compiler_params=_params(("parallel",)),
    )(z, z, z, z, z, conv_w8, gq, gkv, cos, sin_a, sin_b)


def _mix_bwd(z, dyc, dqn, dkvn, dkr, conv_w8, gq, gkv, cos, sin_a, sin_b, *, ts=256):
    s = z.shape[0]
    n = s // ts
    cw = CONV_WIDTH

    def body(z_ref, xcp, xcn, bgp, bgn, cgp, cgn, dyc_ref, dycp, dycn, dqn_ref, dkvn_ref, dkr_ref,
             w_ref, gq_ref, gkv_ref, cos_ref, sa_ref, sb_ref,
             dz_ref, dw0_ref, dw1_ref, dw2_ref, dgq_ref, dgkv_ref):
        i = pl.program_id(0)

        @pl.when(i == 0)
        def _():
            for r in (dw0_ref, dw1_ref, dw2_ref, dgq_ref, dgkv_ref):
                r[...] = jnp.zeros_like(r)

        xc = z_ref[:, 0:cw]
        bg = z_ref[:, cw:2 * cw]
        cg = z_ref[:, 2 * cw:3 * cw]
        w0, w1, w2 = w_ref[0:1, :], w_ref[1:2, :], w_ref[2:3, :]
        m = cg * xc
        m_dn = _shift_down(m, jnp.where(i > 0, xcp[7:8, :] * cgp[7:8, :], 0.0))
        m_up = _shift_up(m, jnp.where(i < n - 1, xcn[0:1, :] * cgn[0:1, :], 0.0))
        cm = m_dn * w0 + m * w1 + m_up * w2
        dyc_v = dyc_ref[...]
        dcm = dyc_v * bg
        dcm_dn = _shift_down(dcm, jnp.where(i > 0, dycp[7:8, :] * bgp[7:8, :], 0.0))
        dcm_up = _shift_up(dcm, jnp.where(i < n - 1, dycn[0:1, :] * bgn[0:1, :], 0.0))
        dm = dcm_up * w0 + dcm * w1 + dcm_dn * w2
        dz_ref[:, 0:cw] = (dm * cg).astype(BF16)
        dz_ref[:, cw:2 * cw] = (dyc_v * cm).astype(BF16)
        dz_ref[:, 2 * cw:3 * cw] = (dm * xc).astype(BF16)
        dw0_ref[...] += jnp.sum(dcm * m_dn, axis=0, keepdims=True)
        dw1_ref[...] += jnp.sum(dcm * m, axis=0, keepdims=True)
        dw2_ref[...] += jnp.sum(dcm * m_up, axis=0, keepdims=True)

        dql, dgq_rows = _rms_bwd_rows(z_ref[:, 3 * cw:3 * cw + Q_LORA], gq_ref[...], dqn_ref[...])
        dz_ref[:, 3 * cw:3 * cw + Q_LORA] = dql.astype(BF16)
        dgq_ref[...] += jnp.sum(dgq_rows, axis=0, keepdims=True)
        dkvl, dgkv_rows = _rms_bwd_rows(z_ref[:, 3 * cw + Q_LORA:3 * cw + Q_LORA + KV_LORA], gkv_ref[...],
                                        dkvn_ref[...])
        dz_ref[:, 3 * cw + Q_LORA:3 * cw + Q_LORA + KV_LORA] = dkvl.astype(BF16)
        dgkv_ref[...] += jnp.sum(dgkv_rows, axis=0, keepdims=True)

        lane = lax.broadcasted_iota(jnp.int32, (ts, HEAD_PAD), 1)
        rope_lane = (lane >= QK_NOPE) & (lane < QK_NOPE + QK_ROPE)
        dk = _rope_bwd(dkr_ref[...], cos_ref[...], sa_ref[...], sb_ref[...])
        dz_ref[:, D_IN_PAD - HEAD_PAD:D_IN_PAD] = jnp.where(rope_lane, dk, 0.0).astype(BF16)

    xcp, xcn = _halo_specs(ts, s, cw, 0)
    bgp, bgn = _halo_specs(ts, s, cw, 1)
    cgp, cgn = _halo_specs(ts, s, cw, 2)
    dycp, dycn = _halo_specs(ts, s, cw, 0)
    tab = pl.BlockSpec((ts, HEAD_PAD), lambda i: (i, 0))

    def vec(width):
        return pl.BlockSpec((1, width), lambda i: (0, 0))

    outs = pl.pallas_call(
        body, name="mix_bwd", grid=(n,),
        in_specs=[pl.BlockSpec((ts, D_IN_PAD), lambda i: (i, 0)), xcp, xcn, bgp, bgn, cgp, cgn,
                  pl.BlockSpec((ts, cw), lambda i: (i, 0)), dycp, dycn,
                  pl.BlockSpec((ts, Q_LORA), lambda i: (i, 0)), pl.BlockSpec((ts, KV_LORA), lambda i: (i, 0)), tab,
                  pl.BlockSpec((8, cw), lambda i: (0, 0)), vec(Q_LORA), vec(KV_LORA), tab, tab, tab],
        out_specs=[pl.BlockSpec((ts, D_IN_PAD), lambda i: (i, 0)), vec(cw), vec(cw), vec(cw), vec(Q_LORA),
                   vec(KV_LORA)],
        out_shape=[jax.ShapeDtypeStruct((s, D_IN_PAD), BF16)] + [jax.ShapeDtypeStruct((1, cw), F32)] * 3
        + [jax.ShapeDtypeStruct((1, Q_LORA), F32), jax.ShapeDtypeStruct((1, KV_LORA), F32)],
        compiler_params=_params(("arbitrary",)),
    )(z, z, z, z, z, z, z, dyc, dyc, dyc, dqn, dkvn, dkr, conv_w8, gq, gkv, cos, sin_a, sin_b)
    dz, dw0, dw1, dw2, dgq, dgkv = outs
    return dz, jnp.concatenate([dw0, dw1, dw2], axis=0), dgq, dgkv


def _qkv_proj(qn, kvn, kr, w_uq_p, w_kv_p, cos, sin_a, sin_b, *, ts=512):
    s = qn.shape[0]

    def body(qn_ref, kvn_ref, kr_ref, wq_ref, wkv_ref, cos_ref, sa_ref, sb_ref, q_ref, k_ref, v_ref, qt_ref):
        cos_v, sa, sb = cos_ref[...], sa_ref[...], sb_ref[...]
        q = jnp.dot(qn_ref[...], wq_ref[...], preferred_element_type=F32)
        kv = jnp.dot(kvn_ref[...], wkv_ref[...], preferred_element_type=F32)
        kr_v = kr_ref[...]
        lane = lax.broadcasted_iota(jnp.int32, (1, HEAD_PAD), 1)
        ones_lane = (lane == ONES_LANE).astype(F32)
        for h in range(N_HEADS):
            blk = slice(h * HEAD_PAD, (h + 1) * HEAD_PAD)
            q_h = _rope(q[:, blk], cos_v, sa, sb) * SM_SCALE
            q_ref[:, blk] = q_h.astype(BF16)
            qt_ref[blk, :] = q_h.T.astype(BF16)
            k_ref[:, blk] = (kv[:, blk] + kr_v).astype(BF16)
            v_ref[:, blk] = (kv[:, D_ATT + h * HEAD_PAD:D_ATT + (h + 1) * HEAD_PAD] + ones_lane).astype(BF16)

    tab = pl.BlockSpec((ts, HEAD_PAD), lambda i: (i, 0))
    wide = pl.BlockSpec((ts, D_ATT), lambda i: (i, 0))
    return pl.pallas_call(
        body, name="qkv_proj", grid=(s // ts,),
        in_specs=[pl.BlockSpec((ts, Q_LORA), lambda i: (i, 0)), pl.BlockSpec((ts, KV_LORA), lambda i: (i, 0)), tab,
                  pl.BlockSpec((Q_LORA, D_ATT), lambda i: (0, 0)), pl.BlockSpec((KV_LORA, 2 * D_ATT), lambda i: (0, 0)),
                  tab, tab, tab],
        out_specs=[wide, wide, wide, pl.BlockSpec((D_ATT, ts), lambda i: (0, i))],
        out_shape=[jax.ShapeDtypeStruct((s, D_ATT), BF16)] * 3 + [jax.ShapeDtypeStruct((D_ATT, s), BF16)],
        compiler_params=_params(("parallel",)),
    )(qn, kvn, kr, w_uq_p, w_kv_p, cos, sin_a, sin_b)


def _qk_bwd(dq, dk, cos, sin_a, sin_b, *, ts=256):
    s = dq.shape[0]

    def body(dq_ref, dk_ref, cos_ref, sa_ref, sb_ref, dqp_ref, dkr_ref):
        cos_v, sa, sb = cos_ref[...], sa_ref[...], sb_ref[...]
        tot = jnp.zeros((ts, HEAD_PAD), F32)
        for h in range(N_HEADS):
            blk = slice(h * HEAD_PAD, (h + 1) * HEAD_PAD)
            dqp_ref[:, blk] = _rope_bwd(dq_ref[:, blk], cos_v, sa, sb).astype(BF16)
            tot = tot + dk_ref[:, blk]
        dkr_ref[...] = tot

    tab = pl.BlockSpec((ts, HEAD_PAD), lambda i: (i, 0))
    wide = pl.BlockSpec((ts, D_ATT), lambda i: (i, 0))
    return pl.pallas_call(
        body, name="qk_bwd", grid=(s // ts,),
        in_specs=[wide, wide, tab, tab, tab], out_specs=[wide, tab],
        out_shape=[jax.ShapeDtypeStruct((s, D_ATT), BF16), jax.ShapeDtypeStruct((s, HEAD_PAD), F32)],
        compiler_params=_params(("parallel",)),
    )(dq, dk, cos, sin_a, sin_b)


_NT = (((1,), (1,)), ((), ()))
_TN = (((0,), (0,)), ((), ()))


def _flash_fwd(q, k, v, *, tq=1024, tk=1024):
    s = q.shape[0]
    tq, tk = min(tq, s), min(tk, s)
    nk = s // tk

    def body(q_ref, k_ref, v_ref, o_ref, lse_ref):
        qv = q_ref[...]

        def step(j, carry):
            m, acc = carry
            rows = pl.ds(pl.multiple_of(j * tk, tk), tk)
            sc = lax.dot_general(qv, k_ref[rows, :], _NT, preferred_element_type=F32)
            m_new = jnp.maximum(m, jnp.max(sc, axis=1, keepdims=True))
            p = jnp.exp(sc - m_new).astype(BF16)
            acc = jnp.exp(m - m_new) * acc + jnp.dot(p, v_ref[rows, :], preferred_element_type=F32)
            return m_new, acc

        init = (jnp.full((tq, 1), -jnp.inf, F32), jnp.zeros((tq, HEAD_PAD), F32))
        m, acc = lax.fori_loop(0, nk, step, init)
        l = acc[:, ONES_LANE:ONES_LANE + 1]
        o_ref[...] = (acc / l).astype(BF16)
        lse_ref[...] = m + jnp.log(l)

    head = pl.BlockSpec((s, HEAD_PAD), lambda h, i: (0, h))
    return pl.pallas_call(
        body, name="flash_fwd", grid=(N_HEADS, s // tq),
        in_specs=[pl.BlockSpec((tq, HEAD_PAD), lambda h, i: (i, h)), head, head],
        out_specs=[pl.BlockSpec((tq, HEAD_PAD), lambda h, i: (i, h)),
                   pl.BlockSpec((None, tq, 1), lambda h, i: (h, i, 0))],
        out_shape=[jax.ShapeDtypeStruct((s, D_ATT), BF16), jax.ShapeDtypeStruct((N_HEADS, s, 1), F32)],
        compiler_params=_params(("parallel", "parallel")),
    )(q, k, v)


def _attn_delta(do, o, *, ts=512):
    s = do.shape[0]

    def body(do_ref, o_ref, dl_ref):
        for h in range(N_HEADS):
            blk = slice(h * HEAD_PAD, (h + 1) * HEAD_PAD)
            dl_ref[h] = jnp.sum(do_ref[:, blk].astype(F32) * o_ref[:, blk].astype(F32), axis=1, keepdims=True)

    wide = pl.BlockSpec((ts, D_ATT), lambda i: (i, 0))
    return pl.pallas_call(
        body, name="attn_delta", grid=(s // ts,), in_specs=[wide, wide],
        out_specs=pl.BlockSpec((N_HEADS, ts, 1), lambda i: (0, i, 0)),
        out_shape=jax.ShapeDtypeStruct((N_HEADS, s, 1), F32),
        compiler_params=_params(("parallel",)),
    )(do, o)


def _flash_bwd(q, qt, k, v, do, dot, lse, delta, after, *, tq=1024, tk=512):
    s = q.shape[0]
    tq, tk = min(tq, s), min(tk, s)
    nq = s // tq

    def body(q_ref, qt_ref, do_ref, dot_ref, lse_ref, dl_ref, k_ref, v_ref, after_ref, dq_ref, dk_ref, dv_ref):
        j = pl.program_id(1)

        @pl.when(j == 0)
        def _():
            dq_ref[...] = jnp.zeros_like(dq_ref)

        kv, vv = k_ref[...], v_ref[...]

        def step(i, carry):
            dk_t, dv_t = carry
            at = pl.multiple_of(i * tq, tq)
            rows = pl.ds(at, tq)
            sc = lax.dot_general(q_ref[rows, :], kv, _NT, preferred_element_type=F32)
            p = jnp.exp(sc - lse_ref[rows, :])
            dp = lax.dot_general(do_ref[rows, :], vv, _NT, preferred_element_type=F32)
            ds = (p * (dp - dl_ref[rows, :])).astype(BF16)
            dv_t = dv_t + jnp.dot(dot_ref[:, rows], p.astype(BF16), preferred_element_type=F32)
            dk_t = dk_t + jnp.dot(qt_ref[:, rows], ds, preferred_element_type=F32)
            dq_ref[rows, :] += jnp.dot(ds, kv, preferred_element_type=F32)
            return dk_t, dv_t

        zero = jnp.zeros((HEAD_PAD, tk), F32)
        dk_t, dv_t = lax.fori_loop(0, nq, step, (zero, zero))
        dk_ref[...] = dk_t.T
        dv_ref[...] = dv_t.T

        @pl.when(j == pl.num_programs(1) - 1)
        def _():
            dq_ref[...] *= SM_SCALE

    head = pl.BlockSpec((s, HEAD_PAD), lambda h, j: (0, h))
    head_t = pl.BlockSpec((HEAD_PAD, s), lambda h, j: (h, 0))
    stat = pl.BlockSpec((None, s, 1), lambda h, j: (h, 0, 0))
    blk = pl.BlockSpec((tk, HEAD_PAD), lambda h, j: (j, h))
    return pl.pallas_call(
        body, name="flash_bwd", grid=(N_HEADS, s // tk),
        in_specs=[head, head_t, head, head_t, stat, stat, blk, blk, ANY],
        out_specs=[head, blk, blk],
        out_shape=[jax.ShapeDtypeStruct((s, D_ATT), F32)] * 3,
        compiler_params=_params(("parallel", "arbitrary")),
    )(q, qt, do, dot, lse, delta, k, v, after)


FFN_TC = 256


FFN_HALO_BF16 = 16
FFN_HALO_F32 = 8


def _row_halo_specs(ts, s, halo, width):
    nb = ts // halo
    last = s // halo - 1
    prev = pl.BlockSpec((halo, width), lambda i, j: (jnp.maximum(i * nb - 1, 0), 0))
    nxt = pl.BlockSpec((halo, width), lambda i, j: (jnp.minimum((i + 1) * nb, last), 0))
    return prev, nxt


def _ext_rows(prev, main, nxt, first, last):
    return jnp.concatenate([jnp.where(first, jnp.zeros_like(prev), prev), main,
                            jnp.where(last, jnp.zeros_like(nxt), nxt)], axis=0)


def _ext_conv(a, w):
    a_dn = pltpu.roll(a, 1, 0)
    a_up = pltpu.roll(a, a.shape[0] - 1, 0)
    return a_dn * w[0:1, :] + a * w[1:2, :] + a_up * w[2:3, :], a_dn, a_up


def _ffn_fwd(hf, w_up, w, b, *, ts=512, tc=FFN_TC):
    s = hf.shape[0]
    n, nj, halo = s // ts, D_FF // tc, FFN_HALO_BF16

    def body(h_ref, hp_ref, hn_ref, wg_ref, wu_ref, cw_ref, cb_ref, a_ref, act_ref):
        i = pl.program_id(0)
        ext = _ext_rows(hp_ref[...], h_ref[...], hn_ref[...], i == 0, i == n - 1)
        gate_up = []
        for half, w_ref in enumerate((wg_ref, wu_ref)):
            a_ext = jnp.dot(ext, w_ref[...], preferred_element_type=F32)
            a_ref[half] = a_ext[halo:halo + ts]
            gate_up.append(_ext_conv(a_ext, cw_ref[half])[0][halo:halo + ts] + cb_ref[half])
        g, u = gate_up
        act_ref[...] = (g * _sigmoid(g) * u).astype(BF16)

    prev, nxt = _row_halo_specs(ts, s, halo, D_MODEL)
    return pl.pallas_call(
        body, name="ffn_fwd", grid=(n, nj),
        in_specs=[pl.BlockSpec((ts, D_MODEL), lambda i, j: (i, 0)), prev, nxt,
                  pl.BlockSpec((D_MODEL, tc), lambda i, j: (0, j)), pl.BlockSpec((D_MODEL, tc), lambda i, j: (0, j + nj)),
                  pl.BlockSpec((2, 8, tc), lambda i, j: (0, 0, j)), pl.BlockSpec((2, 1, tc), lambda i, j: (0, 0, j))],
        out_specs=[pl.BlockSpec((2, ts, tc), lambda i, j: (0, i, j)), pl.BlockSpec((ts, tc), lambda i, j: (i, j))],
        out_shape=[jax.ShapeDtypeStruct((2, s, D_FF), F32), jax.ShapeDtypeStruct((s, D_FF), BF16)],
        compiler_params=_params(("parallel", "parallel")),
    )(hf, hf, hf, w_up, w_up, w, b)


def _ffn_bwd(dx2, w_down, a_pre, w, b, *, ts=512, tc=FFN_TC):
    s = dx2.shape[0]
    n, nj, halo = s // ts, D_FF // tc, FFN_HALO_F32
    main = slice(halo, halo + ts)

    def body(dx_ref, dxp_ref, dxn_ref, wd_ref, a_ref, ap_ref, an_ref, cw_ref, cb_ref, o_ref, dw_ref, db_ref):
        i, j = pl.program_id(0), pl.program_id(1)
        first, last = i == 0, i == n - 1

        @pl.when(first & (j == 0))
        def _():
            dw_ref[...] = jnp.zeros_like(dw_ref)
            db_ref[...] = jnp.zeros_like(db_ref)

        dx_ext = _ext_rows(dxp_ref[...], dx_ref[...], dxn_ref[...], first, last).astype(BF16)
        dact = lax.dot_general(dx_ext, wd_ref[...], _NT, preferred_element_type=F32)
        halves = []
        for half in range(2):
            a_ext = _ext_rows(ap_ref[half], a_ref[half], an_ref[half], first, last)
            conv, a_dn, a_up = _ext_conv(a_ext, cw_ref[half])
            halves.append((conv + cb_ref[half], a_dn, a_ext, a_up))
        g, u = halves[0][0], halves[1][0]
        sg = _sigmoid(g)
        grads = (dact * u * (sg * (1.0 + g * (1.0 - sg))), dact * (g * sg))
        for half in range(2):
            d = grads[half]
            _, a_dn, a_ext, a_up = halves[half]
            wv = cw_ref[half]
            d_pre = pltpu.roll(d, d.shape[0] - 1, 0) * wv[0:1, :] + d * wv[1:2, :] + pltpu.roll(d, 1, 0) * wv[2:3, :]
            o_ref[half] = d_pre[main].astype(BF16)
            dm = d[main]
            dw_ref[j, half, 0:1, :] += jnp.sum(dm * a_dn[main], axis=0, keepdims=True)
            dw_ref[j, half, 1:2, :] += jnp.sum(dm * a_ext[main], axis=0, keepdims=True)
            dw_ref[j, half, 2:3, :] += jnp.sum(dm * a_up[main], axis=0, keepdims=True)
            db_ref[j, half] += jnp.sum(dm, axis=0, keepdims=True)

    dxp, dxn = _row_halo_specs(ts, s, halo, D_MODEL)
    nb, lastb = ts // halo, s // halo - 1
    a_main = pl.BlockSpec((2, ts, tc), lambda i, j: (0, i, j))
    a_prev = pl.BlockSpec((2, halo, tc), lambda i, j: (0, jnp.maximum(i * nb - 1, 0), j))
    a_next = pl.BlockSpec((2, halo, tc), lambda i, j: (0, jnp.minimum((i + 1) * nb, lastb), j))
    da_pre, dw, db = pl.pallas_call(
        body, name="ffn_bwd", grid=(n, nj),
        in_specs=[pl.BlockSpec((ts, D_MODEL), lambda i, j: (i, 0)), dxp, dxn,
                  pl.BlockSpec((tc, D_MODEL), lambda i, j: (j, 0)), a_main, a_prev, a_next,
                  pl.BlockSpec((2, 8, tc), lambda i, j: (0, 0, j)), pl.BlockSpec((2, 1, tc), lambda i, j: (0, 0, j))],
        out_specs=[a_main, pl.BlockSpec((nj, 2, 8, tc), lambda i, j: (0, 0, 0, 0)),
                   pl.BlockSpec((nj, 2, 1, tc), lambda i, j: (0, 0, 0, 0))],
        out_shape=[jax.ShapeDtypeStruct((2, s, D_FF), BF16), jax.ShapeDtypeStruct((nj, 2, 8, tc), F32),
                   jax.ShapeDtypeStruct((nj, 2, 1, tc), F32)],
        compiler_params=_params(("arbitrary", "arbitrary")),
    )(dx2, dx2, dx2, w_down, a_pre, a_pre, a_pre, w, b)
    return (da_pre, dw.transpose(1, 2, 0, 3).reshape(2, 8, D_FF), db.transpose(1, 2, 0, 3).reshape(2, 1, D_FF))


def _ple_final(x2, n3, p, target, gf, w_pg, w_pp, *, ts=256):
    s, d = x2.shape
    dp = p.shape[1]

    def body(x2_ref, n3_ref, p_ref, t_ref, gf_ref, wg_ref, wp_ref, loss_ref, dx3_ref, dgl_ref, dpp_ref, dgf_ref):
        @pl.when(pl.program_id(0) == 0)
        def _():
            loss_ref[...] = jnp.zeros_like(loss_ref)
            dgf_ref[...] = jnp.zeros_like(dgf_ref)

        gate = _sigmoid(jnp.dot(n3_ref[...], wg_ref[...], preferred_element_type=F32))
        ppv = jnp.dot(p_ref[...].astype(BF16), wp_ref[...], preferred_element_type=F32)
        x3 = x2_ref[...] + gate * ppv
        gfv = gf_ref[...]
        err = x3 * _rms_scale(x3) * gfv - t_ref[...]
        loss_ref[...] += 0.5 * jnp.sum(jnp.mean(err * err, axis=-1, keepdims=True), axis=0, keepdims=True)
        dx3, dgf_rows = _rms_bwd_rows(x3, gfv, err * (1.0 / d))
        dgf_ref[...] += jnp.sum(dgf_rows, axis=0, keepdims=True)
        dx3_ref[...] = dx3
        dgl_ref[...] = (dx3 * ppv * gate * (1.0 - gate)).astype(BF16)
        dpp_ref[...] = (dx3 * gate).astype(BF16)

    row = pl.BlockSpec((ts, d), lambda i: (i, 0))
    vec = pl.BlockSpec((1, d), lambda i: (0, 0))
    return pl.pallas_call(
        body, name="ple_final", grid=(s // ts,),
        in_specs=[row, row, pl.BlockSpec((ts, dp), lambda i: (i, 0)), row, vec,
                  pl.BlockSpec((d, d), lambda i: (0, 0)), pl.BlockSpec((dp, d), lambda i: (0, 0))],
        out_specs=[pl.BlockSpec((1, 128), lambda i: (0, 0)), row, row, row, vec],
        out_shape=[jax.ShapeDtypeStruct((1, 128), F32), jax.ShapeDtypeStruct((s, d), F32),
                   jax.ShapeDtypeStruct((s, d), BF16), jax.ShapeDtypeStruct((s, d), BF16),
                   jax.ShapeDtypeStruct((1, d), F32)],
        compiler_params=_params(("arbitrary",)),
    )(x2, n3, p, target, gf, w_pg, w_pp)


def _row_tile(rows, cols, n_arrays, budget=12 << 20):
    best = None
    for t in range(8, rows + 1, 8):
        if rows % t == 0 and t * cols * 4 * n_arrays <= budget:
            best = t
    return rows if best is None else best


def _sum_slots(a, *, name):
    g, r, c = a.shape
    tr = _row_tile(r, c, g + 1)

    def body(*refs):
        tot = refs[0][...]
        for ref in refs[1:g]:
            tot = tot + ref[...]
        refs[g][...] = tot

    specs = [pl.BlockSpec((None, tr, c), functools.partial(lambda i, slot: (slot, i, 0), slot=k)) for k in range(g)]
    return pl.pallas_call(
        body, name=name, grid=(r // tr,), in_specs=specs, out_specs=pl.BlockSpec((tr, c), lambda i: (i, 0)),
        out_shape=jax.ShapeDtypeStruct((r, c), a.dtype), compiler_params=_params(("parallel",)),
    )(*([a] * g))


def _adamw(w, g, m, v, *, name):
    r, c = w.shape
    tr = _row_tile(r, c, 7)

    def body(w_ref, g_ref, m_ref, v_ref, d_ref, mo_ref, vo_ref):
        gv = g_ref[...]
        mn = ADAM_B1 * m_ref[...] + (1.0 - ADAM_B1) * gv
        vn = ADAM_B2 * v_ref[...] + (1.0 - ADAM_B2) * (gv * gv)
        m_hat = mn / (1.0 - ADAM_B1 ** ADAM_STEP)
        v_hat = vn / (1.0 - ADAM_B2 ** ADAM_STEP)
        d_ref[...] = -ADAM_LR * (m_hat / (jnp.sqrt(v_hat) + ADAM_EPS) + ADAM_WD * w_ref[...])
        mo_ref[...] = mn
        vo_ref[...] = vn

    blk = pl.BlockSpec((tr, c), lambda i: (i, 0))
    return pl.pallas_call(
        body, name=name, grid=(r // tr,), in_specs=[blk] * 4, out_specs=[blk] * 3,
        out_shape=[jax.ShapeDtypeStruct((r, c), F32)] * 3, compiler_params=_params(("parallel",)),
    )(w, g, m, v)


def _position():
    x, y, c = lax.axis_index("x"), lax.axis_index("y"), lax.axis_index("c")
    return x, y, c


def _other_chips(x, y):
    return [(1 - x, y), (x, 1 - y), (1 - x, 1 - y)]


def _stage_in(srcs, stage, sems):
    cps = [pltpu.make_async_copy(src, stage[a], sems.at[a]) for a, src in enumerate(srcs)]
    for cp in cps:
        cp.start()
    return cps


def _stage_out(staged, stage, dsts, sems):
    cps = []
    for a, dst in enumerate(dsts):
        staged[a].wait()
        cp = pltpu.make_async_copy(stage[a], dst, sems.at[a])
        cp.start()
        cps.append(cp)
    return cps


def _gather_chips(shards):
    n = len(shards)

    def body(*refs):
        ins, outs, stage = refs[:n], refs[n:2 * n], refs[2 * n:3 * n]
        send_sems, recv_sems, in_sems, out_sems = refs[3 * n:]
        x, y, c = _position()
        me = 2 * x + y
        chips = _other_chips(x, y)
        remote = []
        staged = _stage_in(ins, stage, in_sems)
        for a in range(n):
            for k, (px, py) in enumerate(chips):
                rc = pltpu.make_async_remote_copy(
                    src_ref=ins[a], dst_ref=outs[a].at[me], send_sem=send_sems.at[3 * a + k],
                    recv_sem=recv_sems.at[3 * a + k], device_id=(px, py, c), device_id_type=MESH)
                rc.start()
                remote.append(rc)
        local = _stage_out(staged, stage, [o.at[me] for o in outs], out_sems)
        for a in range(n):
            for k, (px, py) in enumerate(chips):
                pltpu.make_async_remote_copy(
                    src_ref=ins[a], dst_ref=outs[a].at[2 * px + py], send_sem=send_sems.at[3 * a + k],
                    recv_sem=recv_sems.at[3 * a + k], device_id=(px, py, c), device_id_type=MESH).wait_recv()
        for rc in remote:
            rc.wait_send()
        for cp in local:
            cp.wait()

    return pl.pallas_call(
        body, name="gather_chips", in_specs=[ANY] * n, out_specs=[ANY] * n,
        out_shape=[jax.ShapeDtypeStruct((N_CHIPS,) + s.shape, s.dtype) for s in shards],
        scratch_shapes=[pltpu.VMEM(s.shape, s.dtype) for s in shards]
        + [pltpu.SemaphoreType.DMA((3 * n,)), pltpu.SemaphoreType.DMA((3 * n,)),
           pltpu.SemaphoreType.DMA((n,)), pltpu.SemaphoreType.DMA((n,))],
        compiler_params=pltpu.CompilerParams(has_side_effects=True),
    )(*shards)


def _send_other_halves(grads, *, tag):
    n = len(grads)

    def body(*refs):
        ins, sib = refs[:n], refs[n:2 * n]
        send_sems, recv_sems = refs[2 * n:]
        x, y, c = _position()
        remote = []
        for a in range(n):
            half = ins[a].shape[1] // 2
            give = ins[a].at[:, pl.ds(pl.multiple_of((1 - c) * half, 8), half), :]
            rc = pltpu.make_async_remote_copy(
                src_ref=give, dst_ref=sib[a], send_sem=send_sems.at[a], recv_sem=recv_sems.at[a],
                device_id=(x, y, 1 - c), device_id_type=MESH)
            rc.start()
            remote.append(rc)
        for rc in remote:
            rc.wait_recv()
        for rc in remote:
            rc.wait_send()

    return pl.pallas_call(
        body, name="send_other_halves_" + tag, in_specs=[ANY] * n, out_specs=[ANY] * n,
        out_shape=[jax.ShapeDtypeStruct((g.shape[0], g.shape[1] // 2, g.shape[2]), g.dtype) for g in grads],
        scratch_shapes=[pltpu.SemaphoreType.DMA((n,)), pltpu.SemaphoreType.DMA((n,))],
        compiler_params=pltpu.CompilerParams(has_side_effects=True),
    )(*grads)


def _add_own_half(g4, sib, core, *, name):
    g, a2, c = sib.shape
    tr = _row_tile(a2, c, 4)

    def body(core_ref, a_ref, b_ref, o_ref, o16_ref):
        tot = a_ref[...] + b_ref[...]
        o_ref[...] = tot
        o16_ref[...] = tot.astype(BF16)

    blk = pl.BlockSpec((None, tr, c), lambda i, j, core_ref: (i, j, 0))
    return pl.pallas_call(
        body, name=name,
        grid_spec=pltpu.PrefetchScalarGridSpec(
            num_scalar_prefetch=1, grid=(g, a2 // tr),
            in_specs=[pl.BlockSpec((None, None, tr, c), lambda i, j, core_ref: (i, core_ref[0], j, 0)), blk],
            out_specs=[blk, blk]),
        out_shape=[jax.ShapeDtypeStruct(sib.shape, F32), jax.ShapeDtypeStruct(sib.shape, BF16)],
        compiler_params=_params(("parallel", "parallel")),
    )(core, g4.reshape(g, 2, a2, c), sib)


def _scatter_chips(parts):
    n = len(parts)

    def body(*refs):
        ins, outs = refs[:n], refs[n:2 * n]
        send_sems, recv_sems = refs[2 * n:]
        x, y, c = _position()
        me = 2 * x + y
        chips = _other_chips(x, y)
        remote = []
        for a in range(n):
            for k, (px, py) in enumerate(chips):
                rc = pltpu.make_async_remote_copy(
                    src_ref=ins[a].at[2 * px + py], dst_ref=outs[a].at[me], send_sem=send_sems.at[3 * a + k],
                    recv_sem=recv_sems.at[3 * a + k], device_id=(px, py, c), device_id_type=MESH)
                rc.start()
                remote.append(rc)
        for a in range(n):
            for k, (px, py) in enumerate(chips):
                pltpu.make_async_remote_copy(
                    src_ref=ins[a].at[me], dst_ref=outs[a].at[2 * px + py], send_sem=send_sems.at[3 * a + k],
                    recv_sem=recv_sems.at[3 * a + k], device_id=(px, py, c), device_id_type=MESH).wait_recv()
        for rc in remote:
            rc.wait_send()

    return pl.pallas_call(
        body, name="scatter_chips", in_specs=[ANY] * n, out_specs=[ANY] * n,
        out_shape=[jax.ShapeDtypeStruct(p.shape, p.dtype) for p in parts],
        scratch_shapes=[pltpu.SemaphoreType.DMA((3 * n,)), pltpu.SemaphoreType.DMA((3 * n,))],
        compiler_params=pltpu.CompilerParams(has_side_effects=True),
    )(*parts)


def _sum_chips(landed, own, chip, *, name):
    g, r, c = landed.shape
    tr = _row_tile(r, c, 5)

    def body(chip_ref, *refs):
        me = chip_ref[0]
        own_v = refs[g][...]
        tot = None
        for slot in range(g):
            term = jnp.where(me == slot, own_v, refs[slot][...].astype(F32))
            tot = term if tot is None else tot + term
        refs[g + 1][...] = tot

    def landed_spec(slot):
        return pl.BlockSpec((None, tr, c),
                            lambda i, chip_ref: (jnp.where(chip_ref[0] == slot, (slot + 1) % g, slot), i, 0))

    return pl.pallas_call(
        body, name=name,
        grid_spec=pltpu.PrefetchScalarGridSpec(
            num_scalar_prefetch=1, grid=(r // tr,),
            in_specs=[landed_spec(k) for k in range(g)]
            + [pl.BlockSpec((None, tr, c), lambda i, chip_ref: (chip_ref[0], i, 0))],
            out_specs=pl.BlockSpec((tr, c), lambda i, chip_ref: (i, 0))),
        out_shape=jax.ShapeDtypeStruct((r, c), F32), compiler_params=_params(("parallel",)),
    )(chip, *([landed] * g), own)


def _join_halves(halves):
    n = len(halves)

    def body(*refs):
        ins, outs, stage = refs[:n], refs[n:2 * n], refs[2 * n:3 * n]
        send_sems, recv_sems, in_sems, out_sems = refs[3 * n:]
        x, y, c = _position()
        remote = []
        staged = _stage_in(ins, stage, in_sems)
        for a in range(n):
            rc = pltpu.make_async_remote_copy(
                src_ref=ins[a], dst_ref=outs[a].at[c], send_sem=send_sems.at[a], recv_sem=recv_sems.at[a],
                device_id=(x, y, 1 - c), device_id_type=MESH)
            rc.start()
            remote.append(rc)
        local = _stage_out(staged, stage, [o.at[c] for o in outs], out_sems)
        for a in range(n):
            pltpu.make_async_remote_copy(
                src_ref=ins[a], dst_ref=outs[a].at[1 - c], send_sem=send_sems.at[a], recv_sem=recv_sems.at[a],
                device_id=(x, y, 1 - c), device_id_type=MESH).wait_recv()
        for rc in remote:
            rc.wait_send()
        for cp in local:
            cp.wait()

    return pl.pallas_call(
        body, name="join_halves", in_specs=[ANY] * n, out_specs=[ANY] * n,
        out_shape=[jax.ShapeDtypeStruct((2,) + h.shape, h.dtype) for h in halves],
        scratch_shapes=[pltpu.VMEM(h.shape, h.dtype) for h in halves]
        + [pltpu.SemaphoreType.DMA((n,)), pltpu.SemaphoreType.DMA((n,)), pltpu.SemaphoreType.DMA((n,)),
           pltpu.SemaphoreType.DMA((n,))],
        compiler_params=pltpu.CompilerParams(has_side_effects=True),
    )(*halves)


_HBM = pl.BlockSpec(memory_space=pltpu.HBM)
_SEM = pl.BlockSpec(memory_space=pltpu.SEMAPHORE)


def _chip_copies(srcs, lands, send_sems, recv_sems, scatter):
    x, y, c = _position()
    me = 2 * x + y
    outgoing, incoming = [], []
    for a, (src, land) in enumerate(zip(srcs, lands)):
        for k, (px, py) in enumerate(_other_chips(x, y)):
            peer = 2 * px + py
            sems = dict(send_sem=send_sems.at[3 * a + k], recv_sem=recv_sems.at[3 * a + k], device_id=(px, py, c),
                        device_id_type=MESH)
            outgoing.append(pltpu.make_async_remote_copy(
                src_ref=src.at[peer] if scatter else src, dst_ref=land.at[me], **sems))
            incoming.append(pltpu.make_async_remote_copy(
                src_ref=src.at[me] if scatter else src, dst_ref=land.at[peer], **sems))
    return outgoing, incoming


def _chips_start(srcs, *, scatter, name):
    n = len(srcs)
    lands = [lax.empty(a.shape if scatter else (N_CHIPS,) + a.shape, a.dtype) for a in srcs]

    def body(*refs):
        ins, send_sems, recv_sems, token = refs[:2 * n], refs[2 * n], refs[2 * n + 1], refs[-1]
        outgoing, _ = _chip_copies(ins[:n], ins[n:], send_sems, recv_sems, scatter)
        for cp in outgoing:
            cp.start()
        token[...] = jnp.zeros_like(token)

    bufs = list(srcs) + lands
    res = pl.pallas_call(
        body, name=name, in_specs=[_HBM] * (2 * n),
        out_specs=(_SEM, _SEM, *[_HBM] * (2 * n), pl.BlockSpec(memory_space=pltpu.VMEM)),
        out_shape=(pltpu.SemaphoreType.DMA((3 * n,)), pltpu.SemaphoreType.DMA((3 * n,)),
                   *[pltpu.HBM(a.shape, a.dtype) for a in bufs], jax.ShapeDtypeStruct((8, 128), F32)),
        input_output_aliases={i: 2 + i for i in range(2 * n)},
        compiler_params=pltpu.CompilerParams(has_side_effects=pltpu.SideEffectType.DATAFLOW_SIDE_EFFECTING),
    )(*[pltpu.with_memory_space_constraint(a, pltpu.HBM) for a in bufs])
    return res[0], res[1], list(res[2:2 + n]), list(res[2 + n:2 + 2 * n]), res[-1]


def _chips_wait(handle, after, *, scatter, name):
    send_sems, recv_sems, srcs, lands, _ = handle
    n = len(srcs)

    def body(*refs):
        ins, send_ref, recv_ref = refs[:2 * n], refs[2 * n], refs[2 * n + 1]
        outgoing, incoming = _chip_copies(ins[:n], ins[n:], send_ref, recv_ref, scatter)
        for cp in outgoing:
            cp.wait_send()
        for cp in incoming:
            cp.wait_recv()

    bufs = list(srcs) + list(lands)
    res = pl.pallas_call(
        body, name=name, in_specs=[_HBM] * (2 * n) + [_SEM, _SEM, ANY], out_specs=tuple([_HBM] * (2 * n)),
        out_shape=tuple(pltpu.HBM(a.shape, a.dtype) for a in bufs),
        input_output_aliases={i: i for i in range(2 * n)},
        compiler_params=pltpu.CompilerParams(has_side_effects=pltpu.SideEffectType.DATAFLOW_SIDE_EFFECTING),
    )(*bufs, send_sems, recv_sems, after)
    return list(res[:n]), list(res[n:])


def _gather_all(buf):
    def body(in_ref, out_ref, send_sems, recv_sems, local_sem):
        x, y, c = _position()
        me = 4 * x + 2 * y + c
        peers = [(x, y, 1 - c)] + [(px, py, pc) for (px, py) in _other_chips(x, y) for pc in (c, 1 - c)]
        cp = pltpu.make_async_copy(in_ref, out_ref.at[me], local_sem)
        cp.start()
        remote = []
        for k, peer in enumerate(peers):
            rc = pltpu.make_async_remote_copy(
                src_ref=in_ref, dst_ref=out_ref.at[me], send_sem=send_sems.at[k], recv_sem=recv_sems.at[k],
                device_id=peer, device_id_type=MESH)
            rc.start()
            remote.append(rc)
        for k, (px, py, pc) in enumerate(peers):
            pltpu.make_async_remote_copy(
                src_ref=in_ref, dst_ref=out_ref.at[4 * px + 2 * py + pc], send_sem=send_sems.at[k],
                recv_sem=recv_sems.at[k], device_id=(px, py, pc), device_id_type=MESH).wait_recv()
        for rc in remote:
            rc.wait_send()
        cp.wait()

    return pl.pallas_call(
        body, name="gather_all", in_specs=[ANY], out_specs=ANY,
        out_shape=jax.ShapeDtypeStruct((N_DEV,) + buf.shape, buf.dtype),
        scratch_shapes=[pltpu.SemaphoreType.DMA((N_DEV - 1,)), pltpu.SemaphoreType.DMA((N_DEV - 1,)),
                        pltpu.SemaphoreType.DMA],
        compiler_params=pltpu.CompilerParams(has_side_effects=True),
    )(buf)


def _cols_from_shards(g4):
    _, k, n = g4.shape
    return g4.transpose(1, 0, 2).reshape(k, N_CHIPS * n)


def _cols_to_shards(w):
    k, n = w.shape
    return w.reshape(k, N_CHIPS, n // N_CHIPS).transpose(1, 0, 2)


def _pad_heads(w, width):
    k = w.shape[0]
    w3 = w.reshape(k, N_HEADS, width)
    return jnp.pad(w3, ((0, 0), (0, 0), (0, HEAD_PAD - width))).reshape(k, D_ATT)


def _unpad_heads(w, width):
    k = w.shape[0]
    return w.reshape(k, N_HEADS, HEAD_PAD)[:, :, :width]


def _rope_tables(s):
    pos = jnp.arange(s, dtype=F32)
    inv_freq = ROPE_THETA ** (-jnp.arange(0, QK_ROPE, 2, dtype=F32) / QK_ROPE)
    ang = pos[:, None] * inv_freq[None, :]
    cos_h, sin_h = jnp.cos(ang), jnp.sin(ang)
    half = QK_ROPE // 2
    z = jnp.zeros((s, half), F32)
    ones = jnp.ones((s, QK_NOPE), F32)
    tail = jnp.zeros((s, HEAD_PAD - QK_NOPE - QK_ROPE), F32)
    cos = jnp.concatenate([ones, cos_h, cos_h, tail + 1.0], axis=1)
    sin_a = jnp.concatenate([ones * 0.0, -sin_h, z, tail], axis=1)
    sin_b = jnp.concatenate([ones * 0.0, z, sin_h, tail], axis=1)
    return cos, sin_a, sin_b


def _local_step(x, p, target, wts, late_weights, reduce_early):
    s = x.shape[0]
    cos, sin_a, sin_b = _rope_tables(s)
    g1, gq, gkv, g2, g3, gf = (wts[k] for k in ("norm_mix_g", "q_norm_g", "kv_norm_g", "norm_ffn_g", "ple_norm_g",
                                                 "final_norm_g"))
    w_in_p, w_uq_p, w_kv_p = wts["w_in_p"], wts["w_uq_p"], wts["w_kv_p"]
    conv_w8, fconv_w, fconv_b = wts["conv_w8"], wts["ffn_conv_w"], wts["ffn_conv_b"]

    (h, z), _ = _mm_fused(x, w_in_p, name="mm_in", prologue=_pro_rms, vecs=[g1], epilogue=_epi_plain, row_outs=[F32])
    y_conv, qn, kvn, kr = _mix_pre(z, conv_w8, gq, gkv, cos, sin_a, sin_b)
    q, k, v, q_t = _qkv_proj(qn, kvn, kr, w_uq_p, w_kv_p, cos, sin_a, sin_b)
    o, lse = _flash_fwd(q, k, v)
    late = late_weights(lse)
    w_o_a, w_o_b, w_up, w_down = late["w_o_a"], late["w_o_b"], late["w_up"], late["w_down"]
    w_pg, w_pp = late["w_ple_gate"], late["w_ple_proj"]
    (x1, hf), _ = _mm_fused(o, w_o_b, second=(y_conv, w_o_a), name="mm_o", rows=[x], vecs=[g2],
                            epilogue=_epi_add_rms, row_outs=[F32, BF16])
    a_pre, act = _ffn_fwd(hf, w_up, fconv_w, fconv_b)
    (x2, n3), _ = _mm_fused(act, w_down, name="mm_down", rows=[x1], vecs=[g3], epilogue=_epi_add_rms,
                            row_outs=[F32, BF16])
    loss, dx3, dgl, dpp, d_gf = _ple_final(x2, n3, p, target, gf, w_pg, w_pp)

    grads, early = {"final_norm_g": d_gf}, {}
    early["w_ple_proj"] = _mm(p, dpp, ta=True, name="mm_d_wpp", tm=256, tn=1024, tk=2048)
    early["w_ple_gate"] = _mm(n3, dgl, ta=True, name="mm_d_wpg", tm=512, tn=1024, tk=2048)
    (dx2,), (grads["ple_norm_g"],) = _mm_fused(dgl, w_pg, tb=True, name="mm_d_n3", rows=[x2, dx3], vecs=[g3],
                                               epilogue=_epi_rms_bwd, row_outs=[F32], n_vec_out=1)
    early["w_down"] = _mm(act, dx2, ta=True, name="mm_d_wdown", tm=1408, tn=512, tk=2048)
    da_pre, grads["ffn_conv_w"], grads["ffn_conv_b"] = _ffn_bwd(dx2, w_down, a_pre, fconv_w, fconv_b)
    early["w_up"] = _mm(hf, da_pre, ta=True, b_split=True, name="mm_d_wup", tm=512, tn=1408, tk=2048,
                       o_shards=True)
    (dx1,), (grads["norm_ffn_g"],) = _mm_fused(da_pre, w_up, tb=True, a_split=True, name="mm_d_hf", rows=[x1, dx2],
                                               vecs=[g2], epilogue=_epi_rms_bwd, row_outs=[F32], n_vec_out=1, tk=D_FF)
    d_wo_a = _mm(y_conv, dx1, ta=True, name="mm_d_wo_conv", tm=512, tn=1024, tk=2048)
    d_wo_b = _mm(o, dx1, ta=True, name="mm_d_wo_att", tm=512, tn=1024, tk=2048)
    early["w_o"] = jnp.concatenate([d_wo_a, d_wo_b.reshape(N_HEADS, HEAD_PAD, D_MODEL)[:, :V_HEAD]
                                    .reshape(N_HEADS * V_HEAD, D_MODEL)], axis=0)
    token, finish = reduce_early(early)
    dyc = _mm(dx1, w_o_a, tb=True, name="mm_d_yconv", tm=512, tn=512, tk=1024)
    (do, do_t), _ = _mm_fused(dx1, w_o_b, tb=True, name="mm_d_o", epilogue=_epi_plain, row_outs=[BF16],
                              transposed_out=BF16)
    delta = _attn_delta(do, o)
    dq, dk, dv = _flash_bwd(q, q_t, k, v, do, do_t, lse, delta, token)
    reduced_early = finish(dq)
    dq_pre, dkr = _qk_bwd(dq, dk, cos, sin_a, sin_b)
    grads["w_uq_p"] = _mm(qn, dq_pre, ta=True, name="mm_d_wuq", tm=256, tn=1024, tk=2048)
    dqn = _mm(dq_pre, w_uq_p, tb=True, name="mm_d_qn", tm=512, tn=256, tk=1024)
    grads["w_k_p"] = _mm(kvn, dk, ta=True, name="mm_d_wk", tm=128, tn=1024, tk=2048)
    grads["w_v_p"] = _mm(kvn, dv, ta=True, name="mm_d_wv", tm=128, tn=1024, tk=2048)
    dkvn_k = _mm(dk, w_kv_p[:, :D_ATT], tb=True, name="mm_d_kvn_k", tm=512, tn=128, tk=1024)
    dkvn = _mm(dv, w_kv_p[:, D_ATT:], tb=True, add=dkvn_k, name="mm_d_kvn_v", tm=512, tn=128, tk=1024)
    dz, grads["conv_w"], grads["q_norm_g"], grads["kv_norm_g"] = _mix_bwd(
        z, dyc, dqn, dkvn, dkr, conv_w8, gq, gkv, cos, sin_a, sin_b)
    grads["w_in_p"] = _mm(h, dz, ta=True, name="mm_d_win", tm=512, tn=1024, tk=2048)
    (grad_x,), (grads["norm_mix_g"],) = _mm_fused(dz, w_in_p, tb=True, name="mm_d_h", rows=[x, dx1], vecs=[g1],
                                                  epilogue=_epi_rms_bwd, row_outs=[F32], n_vec_out=1)
    return loss[0, 0], grad_x, grads, reduced_early


_EARLY_W = ("w_in", "w_uq", "w_ukv")
_LATE_W = ("w_o", "w_up", "w_down", "w_ple_gate", "w_ple_proj")
_BIG = _EARLY_W + _LATE_W
_COL_SHARDED = ("w_in", "w_uq", "w_ukv", "w_up", "w_ple_proj")
_SMALL = ("norm_mix_g", "conv_w", "q_norm_g", "kv_norm_g", "norm_ffn_g", "ffn_conv_w", "ffn_conv_b", "ple_norm_g",
          "final_norm_g")


def _full_from_slots(n, g4):
    return _cols_from_shards(g4) if n in _COL_SHARDED else g4.reshape(-1, g4.shape[2])


def _shard_major(n, g):
    if g.ndim == 3:
        return g
    return _cols_to_shards(g) if n in _COL_SHARDED else g.reshape(N_CHIPS, g.shape[0] // N_CHIPS, g.shape[1])


def _early_weights(w):
    shards = [w[n][0].astype(BF16) for n in _EARLY_W]
    shards.append(jnp.pad(w["conv_w"][0], ((0, 5), (0, 0))))
    shards.append(jnp.pad(w["ffn_conv_w"][0], ((0, 5), (0, 0))))
    got = _gather_chips(shards)
    full = {n: _full_from_slots(n, g4) for n, g4 in zip(_EARLY_W, got)}
    full["conv_w8"] = _cols_from_shards(got[len(_EARLY_W)])
    full["ffn_conv_w8"] = _cols_from_shards(got[len(_EARLY_W) + 1])
    return _layout_early(full, w)


def _layout_early(full, w):
    out = {n: w[n] for n in ("norm_mix_g", "q_norm_g", "kv_norm_g", "norm_ffn_g", "ple_norm_g")}
    out["final_norm_g"] = w["final_norm_g"][None, :]
    w_in = full["w_in"]
    zc = jnp.zeros((D_MODEL, QK_NOPE), BF16)
    zt = jnp.zeros((D_MODEL, HEAD_PAD - QK_NOPE - QK_ROPE), BF16)
    out["w_in_p"] = jnp.concatenate([w_in[:, :D_IN - QK_ROPE], zc, w_in[:, D_IN - QK_ROPE:], zt], axis=1)
    out["w_uq_p"] = _pad_heads(full["w_uq"], QK_NOPE + QK_ROPE)
    kv3 = full["w_ukv"].reshape(KV_LORA, N_HEADS, QK_NOPE + V_HEAD)
    out["w_kv_p"] = jnp.concatenate([_pad_heads(kv3[:, :, :QK_NOPE].reshape(KV_LORA, -1), QK_NOPE),
                                     _pad_heads(kv3[:, :, QK_NOPE:].reshape(KV_LORA, -1), V_HEAD)], axis=1)
    out["conv_w8"] = full["conv_w8"]
    fw = full["ffn_conv_w8"]
    out["ffn_conv_w"] = jnp.stack([fw[:, :D_FF], fw[:, D_FF:]])
    out["ffn_conv_b"] = w["ffn_conv_b"].reshape(2, 1, D_FF)
    return out


def _layout_late(full):
    w_o = full["w_o"]
    out = {"w_o_a": w_o[:CONV_WIDTH]}
    out["w_o_b"] = jnp.pad(w_o[CONV_WIDTH:].reshape(N_HEADS, V_HEAD, D_MODEL),
                           ((0, 0), (0, HEAD_PAD - V_HEAD), (0, 0))).reshape(D_ATT, D_MODEL)
    for n in ("w_up", "w_down", "w_ple_gate", "w_ple_proj"):
        out[n] = full[n]
    return out


def _true_gradients(g):
    out = {}
    wp = g["w_in_p"]
    out["w_in"] = jnp.concatenate([wp[:, :D_IN - QK_ROPE], wp[:, D_IN_PAD - HEAD_PAD + QK_NOPE:
                                                              D_IN_PAD - HEAD_PAD + QK_NOPE + QK_ROPE]], axis=1)
    out["w_uq"] = _unpad_heads(g["w_uq_p"], QK_NOPE + QK_ROPE).reshape(Q_LORA, -1)
    out["w_ukv"] = jnp.concatenate([_unpad_heads(g["w_k_p"], QK_NOPE), _unpad_heads(g["w_v_p"], V_HEAD)],
                                   axis=2).reshape(KV_LORA, -1)
    out["conv_w"] = g["conv_w"]
    fw = g["ffn_conv_w"]
    out["ffn_conv_w"] = jnp.concatenate([fw[0, :3], fw[1, :3]], axis=1)
    out["ffn_conv_b"] = g["ffn_conv_b"].reshape(1, 2 * D_FF)
    for n in ("norm_mix_g", "q_norm_g", "kv_norm_g", "norm_ffn_g", "ple_norm_g", "final_norm_g"):
        out[n] = g[n]
    return out


def _chip_partials(names, g, core, *, tag):
    g4 = [_shard_major(n, g[n]) for n in names]
    sib = _send_other_halves(g4, tag=tag)
    return [_add_own_half(a, b, core, name="add_cores_" + n) for n, a, b in zip(names, g4, sib)]


_SMALL_SIZES = {"norm_mix_g": D_MODEL, "conv_w": 3 * CONV_WIDTH, "q_norm_g": Q_LORA, "kv_norm_g": KV_LORA,
                "norm_ffn_g": D_MODEL, "ffn_conv_w": 6 * D_FF, "ffn_conv_b": 2 * D_FF, "ple_norm_g": D_MODEL,
                "final_norm_g": D_MODEL}


def _pack(parts, rows):
    flat = jnp.concatenate([a.reshape(-1) for a in parts])
    return jnp.pad(flat, (0, rows * 128 - flat.shape[0])).reshape(rows, 128)


def _unpack(buf, sizes):
    flat = buf.reshape(-1)
    out, at = [], 0
    for n in sizes:
        out.append(flat[at:at + n])
        at += n
    return out


def _reduce_small(g, loss):
    sizes = [1] + [_SMALL_SIZES[n] for n in _SMALL]
    rows = -(-sum(sizes) // 1024) * 8
    slots = _gather_all(_pack([loss] + [g[n] for n in _SMALL], rows))
    parts = _unpack(_sum_slots(slots, name="sum_small"), sizes)
    return parts[0][0], dict(zip(_SMALL, parts[1:]))


def kernel(x, p, norm_mix_g, w_in, conv_w, q_norm_g, w_uq, kv_norm_g, w_ukv, w_o, norm_ffn_g, w_up, ffn_conv_w, ffn_conv_b, w_down, ple_norm_g, w_ple_gate, w_ple_proj, final_norm_g, loss_target, m_norm_mix_g, m_w_in, m_conv_w, m_q_norm_g, m_w_uq, m_kv_norm_g, m_w_ukv, m_w_o, m_norm_ffn_g, m_w_up, m_ffn_conv_w, m_ffn_conv_b, m_w_down, m_ple_norm_g, m_w_ple_gate, m_w_ple_proj, m_final_norm_g, v_norm_mix_g, v_w_in, v_conv_w, v_q_norm_g, v_w_uq, v_kv_norm_g, v_w_ukv, v_w_o, v_norm_ffn_g, v_w_up, v_ffn_conv_w, v_ffn_conv_b, v_w_down, v_ple_norm_g, v_w_ple_gate, v_w_ple_proj, v_final_norm_g):
    names = ["norm_mix_g", "w_in", "conv_w", "q_norm_g", "w_uq", "kv_norm_g", "w_ukv", "w_o", "norm_ffn_g", "w_up",
             "ffn_conv_w", "ffn_conv_b", "w_down", "ple_norm_g", "w_ple_gate", "w_ple_proj", "final_norm_g"]
    w = dict(zip(names, (norm_mix_g, w_in, conv_w, q_norm_g, w_uq, kv_norm_g, w_ukv, w_o, norm_ffn_g, w_up,
                         ffn_conv_w, ffn_conv_b, w_down, ple_norm_g, w_ple_gate, w_ple_proj, final_norm_g)))
    m = dict(zip(names, (m_norm_mix_g, m_w_in, m_conv_w, m_q_norm_g, m_w_uq, m_kv_norm_g, m_w_ukv, m_w_o,
                         m_norm_ffn_g, m_w_up, m_ffn_conv_w, m_ffn_conv_b, m_w_down, m_ple_norm_g, m_w_ple_gate,
                         m_w_ple_proj, m_final_norm_g)))
    v = dict(zip(names, (v_norm_mix_g, v_w_in, v_conv_w, v_q_norm_g, v_w_uq, v_kv_norm_g, v_w_ukv, v_w_o,
                         v_norm_ffn_g, v_w_up, v_ffn_conv_w, v_ffn_conv_b, v_w_down, v_ple_norm_g, v_w_ple_gate,
                         v_w_ple_proj, v_final_norm_g)))

    core = lax.axis_index("c").astype(jnp.int32).reshape(1)
    chip = (2 * lax.axis_index("x") + lax.axis_index("y")).astype(jnp.int32).reshape(1)

    wts = _early_weights(w)
    gather = _chips_start([w[n][0].astype(BF16) for n in _LATE_W], scatter=False, name="gather_late_start")
    wts["norm_mix_g"] = wts["norm_mix_g"] + gather[4][0, 0]

    def late_weights(after):
        shards, landed = _chips_wait(gather, after, scatter=False, name="gather_late_wait")
        full = {n: _full_from_slots(n, lax.dynamic_update_slice(g4, own[None], (chip[0], 0, 0)))
                for n, own, g4 in zip(_LATE_W, shards, landed)}
        return _layout_late(full)

    def reduce_early(g):
        parts = _chip_partials(_LATE_W, g, core, tag="early")
        scatter = _chips_start([t16 for _, t16 in parts], scatter=True, name="scatter_early_start")

        def finish(after):
            _, landed = _chips_wait(scatter, after, scatter=True, name="scatter_early_wait")
            return [_sum_chips(a, t32, chip, name="sum_chips_" + n) for n, a, (t32, _) in zip(_LATE_W, landed, parts)]

        return scatter[4], finish

    loss, grad_x, padded, halves_early = _local_step(x[0], p[0, 0], loss_target[0], wts, late_weights, reduce_early)
    g_full = _true_gradients(padded)

    parts = _chip_partials(_EARLY_W, g_full, core, tag="late")
    landed = _scatter_chips([t16 for _, t16 in parts])
    halves = [_sum_chips(a, t32, chip, name="sum_chips_" + n) for n, a, (t32, _) in zip(_EARLY_W, landed, parts)]
    whole = _join_halves(halves + halves_early)
    big = {n: a.reshape(-1, a.shape[2]) for n, a in zip(_BIG, whole)}

    g_out, d_out, m_out, v_out = {}, {}, {}, {}
    for n in _BIG:
        shape = w[n].shape
        g = big[n]
        d, mn, vn = _adamw(w[n][0], g, m[n][0], v[n][0], name="adamw_" + n)
        g_out[n], d_out[n], m_out[n], v_out[n] = (a.reshape(shape) for a in (g, d, mn, vn))

    loss, small = _reduce_small(g_full, loss)
    chip = 2 * lax.axis_index("x") + lax.axis_index("y")
    g_small = {}
    for n in _SMALL:
        shape = w[n].shape
        g = small[n]
        if n in ("conv_w", "ffn_conv_w"):
            width = shape[-1]
            g = lax.dynamic_slice(g.reshape(3, N_CHIPS * width), (0, chip * width), (3, width))
        g_small[n] = g.reshape(shape)
    sizes = [g_small[n].size for n in _SMALL]
    rows = -(-sum(sizes) // 1024) * 8
    packed = [_pack([src[n] for n in _SMALL], rows) for src in (w, g_small, m, v)]
    d_s, m_s, v_s = _adamw(*packed, name="adamw_small")
    for n, d, mn, vn in zip(_SMALL, _unpack(d_s, sizes), _unpack(m_s, sizes), _unpack(v_s, sizes)):
        shape = w[n].shape
        g_out[n], d_out[n], m_out[n], v_out[n] = g_small[n], d.reshape(shape), mn.reshape(shape), vn.reshape(shape)

    return (loss, grad_x[None], *[g_out[n] for n in names], *[d_out[n] for n in names],
            *[m_out[n] for n in names], *[v_out[n] for n in names])
```

```python
import functools

import jax
import jax.numpy as jnp
from jax import lax
from jax.experimental import pallas as pl
from jax.experimental.pallas import tpu as pltpu

F32 = jnp.float32
BF16 = jnp.bfloat16

D_MODEL = 1024
CONV_WIDTH = 512
Q_LORA = 256
KV_LORA = 128
QK_NOPE = 64
QK_ROPE = 32
V_HEAD = 64
N_HEADS = 8
HEAD_PAD = 128
D_ATT = N_HEADS * HEAD_PAD
D_IN = 3 * CONV_WIDTH + Q_LORA + KV_LORA + QK_ROPE
D_IN_PAD = 3 * CONV_WIDTH + Q_LORA + KV_LORA + HEAD_PAD
D_FF = 2816
ROPE_THETA = 10000.0
EPS = 1e-6
SM_SCALE = (QK_NOPE + QK_ROPE) ** -0.5
ONES_LANE = V_HEAD

ADAM_LR = 0.001
ADAM_B1 = 0.9
ADAM_B2 = 0.999
ADAM_EPS = 1e-08
ADAM_WD = 0.01
ADAM_STEP = 10

N_CHIPS = 4
N_DEV = 8
MESH = pl.DeviceIdType.MESH
ANY = pl.BlockSpec(memory_space=pl.ANY)


def _params(sem):
    return pltpu.CompilerParams(dimension_semantics=sem)


def _mm(a, b, *, name, ta=False, tb=False, add=None, out_dtype=F32, tm=512, tn=512, tk=512,
        a_split=False, b_split=False, o_split=False, o_shards=False):
    if a_split:
        _, m, kh = a.shape
        k = 2 * kh
    elif ta:
        k, m = a.shape
    else:
        m, k = a.shape
    if b_split:
        _, kb, nh = b.shape
        n = 2 * nh
    elif tb:
        n, kb = b.shape
    else:
        kb, n = b.shape
    assert kb == k, (name, a.shape, b.shape)
    tm, tn, tk = min(tm, m), min(tn, n), min(tk, k)
    assert m % tm == 0 and n % tn == 0 and k % tk == 0, (name, m, n, k, tm, tn, tk)
    gm, gn, gk = m // tm, n // tn, k // tk

    if a_split:
        assert gk % 2 == 0
        a_spec = pl.BlockSpec((None, tm, tk), lambda i, j, kk: (kk // (gk // 2), i, kk % (gk // 2)))
    elif ta:
        a_spec = pl.BlockSpec((tk, tm), lambda i, j, kk: (kk, i))
    else:
        a_spec = pl.BlockSpec((tm, tk), lambda i, j, kk: (i, kk))
    if b_split:
        assert gn % 2 == 0
        b_spec = pl.BlockSpec((None, tk, tn), lambda i, j, kk: (j // (gn // 2), kk, j % (gn // 2)))
    elif tb:
        b_spec = pl.BlockSpec((tn, tk), lambda i, j, kk: (j, kk))
    else:
        b_spec = pl.BlockSpec((tk, tn), lambda i, j, kk: (kk, j))
    if o_shards:
        o_spec = pl.BlockSpec((None, tm, tn), lambda i, j, kk: (j, i, 0))
        o_shape = jax.ShapeDtypeStruct((gn, m, tn), out_dtype)
    elif o_split:
        assert gn % 2 == 0
        o_spec = pl.BlockSpec((None, tm, tn), lambda i, j, kk: (j // (gn // 2), i, j % (gn // 2)))
        o_shape = jax.ShapeDtypeStruct((2, m, n // 2), out_dtype)
    else:
        o_spec = pl.BlockSpec((tm, tn), lambda i, j, kk: (i, j))
        o_shape = jax.ShapeDtypeStruct((m, n), out_dtype)
    dims = (((0 if ta else 1,), (1 if tb else 0,)), ((), ()))

    def body(*refs):
        a_ref, b_ref = refs[:2]
        add_ref = None if add is None else refs[2]
        o_ref = refs[2 if add is None else 3]

        def finish(r):
            if add_ref is not None:
                r = r + add_ref[...]
            o_ref[...] = r.astype(o_ref.dtype)

        part = lax.dot_general(a_ref[...].astype(BF16), b_ref[...].astype(BF16), dims, preferred_element_type=F32)
        if gk == 1:
            finish(part)
            return
        acc_ref = refs[-1]
        kk = pl.program_id(2)

        @pl.when(kk == 0)
        def _():
            acc_ref[...] = part

        @pl.when((kk > 0) & (kk < gk - 1))
        def _():
            acc_ref[...] += part

        @pl.when(kk == gk - 1)
        def _():
            finish(acc_ref[...] + part)

    in_specs = [a_spec, b_spec]
    args = [a, b]
    if add is not None:
        in_specs.append(pl.BlockSpec((tm, tn), lambda i, j, kk: (i, j)))
        args.append(add)
    return pl.pallas_call(
        body, name=name, grid=(gm, gn, gk), in_specs=in_specs, out_specs=o_spec, out_shape=o_shape,
        scratch_shapes=[] if gk == 1 else [pltpu.VMEM((tm, tn), F32)],
        compiler_params=_params(("parallel", "parallel", "arbitrary")),
    )(*args)


def _rms_scale(v):
    return lax.rsqrt(jnp.mean(v * v, axis=-1, keepdims=True) + EPS)


def _rms_bwd_rows(v, g, dy):
    r = _rms_scale(v)
    vh = v * r
    dyg = dy * g
    dv = r * (dyg - vh * jnp.mean(dyg * vh, axis=-1, keepdims=True))
    return dv, dy * vh


def _shift_down(v, first_row):
    row = lax.broadcasted_iota(jnp.int32, v.shape, 0)
    return jnp.where(row == 0, first_row, pltpu.roll(v, 1, 0))


def _shift_up(v, last_row):
    n = v.shape[0]
    row = lax.broadcasted_iota(jnp.int32, v.shape, 0)
    return jnp.where(row == n - 1, last_row, pltpu.roll(v, n - 1, 0))


def _rope(t, cos, sin_a, sin_b):
    return t * cos + pltpu.roll(t, HEAD_PAD - 16, 1) * sin_a + pltpu.roll(t, 16, 1) * sin_b


def _rope_bwd(d, cos, sin_a, sin_b):
    return d * cos + pltpu.roll(d * sin_a, 16, 1) + pltpu.roll(d * sin_b, HEAD_PAD - 16, 1)


def _sigmoid(v):
    return 1.0 / (1.0 + jnp.exp(-v))


def _halo_specs(ts, s, width, col):
    nb = ts // 8
    last = s // 8 - 1
    prev = pl.BlockSpec((8, width), lambda i: (jnp.maximum(i * nb - 1, 0), col))
    nxt = pl.BlockSpec((8, width), lambda i: (jnp.minimum((i + 1) * nb, last), col))
    return prev, nxt


def _mm_fused(a, b, *, name, epilogue, row_outs, rows=(), vecs=(), n_vec_out=0, tb=False, a_split=False,
              prologue=None, second=None, transposed_out=None, tm=512, tk=None):
    if a_split:
        _, m, kh = a.shape
        k = 2 * kh
    else:
        m, k = a.shape
    n = b.shape[0] if tb else b.shape[1]
    assert (b.shape[1] if tb else b.shape[0]) == k, (name, a.shape, b.shape)
    tk = k if tk is None else tk
    assert m % tm == 0 and k % tk == 0, (name, m, k, tm, tk)
    gm, gk = m // tm, k // tk
    assert prologue is None or gk == 1
    n_a = 2 if a_split and gk == 1 else 1
    nr, nv = len(rows), len(vecs)
    n_pro = 0 if prologue is None else 1
    n_sec = 0 if second is None else 2
    n_t = 0 if transposed_out is None else 1
    dims = (((1,), (1 if tb else 0,)), ((), ()))

    def body(*refs):
        a_ref, b_ref = refs[0], refs[n_a]
        refs = refs[n_a - 1:]
        sec_refs = refs[2:2 + n_sec]
        row_refs, vec_refs = refs[2 + n_sec:2 + n_sec + nr], refs[2 + n_sec + nr:2 + n_sec + nr + nv]
        outs = refs[2 + n_sec + nr + nv:]
        row_out_refs = outs[n_pro:n_pro + len(row_outs)]
        t_out_refs = outs[n_pro + len(row_outs):n_pro + len(row_outs) + n_t]
        vec_out_refs = outs[n_pro + len(row_outs) + n_t:n_pro + len(row_outs) + n_t + n_vec_out]
        i, kk = pl.program_id(0), pl.program_id(1)
        vec_vals = [v[...] for v in vec_refs]
        if prologue is None:
            lhs = a_ref[...].astype(BF16)
        else:
            lhs = prologue(a_ref[...], vec_vals)
            outs[0][...] = lhs

        def finish(r):
            if second is not None:
                r = r + jnp.dot(sec_refs[0][...].astype(BF16), sec_refs[1][...].astype(BF16),
                                preferred_element_type=F32)
            row_vals, vec_parts = epilogue(r, [x[...] for x in row_refs], vec_vals)
            for ref, val in zip(row_out_refs, row_vals):
                ref[...] = val.astype(ref.dtype)
            for ref in t_out_refs:
                ref[...] = row_vals[0].T.astype(ref.dtype)
            if n_vec_out:
                @pl.when(i == 0)
                def _():
                    for ref in vec_out_refs:
                        ref[...] = jnp.zeros_like(ref)

                for ref, val in zip(vec_out_refs, vec_parts):
                    ref[...] += val

        if n_a == 2:
            kh = k // 2
            halves = (b_ref[:, :kh], b_ref[:, kh:]) if tb else (b_ref[:kh, :], b_ref[kh:, :])
            part = (lax.dot_general(lhs, halves[0].astype(BF16), dims, preferred_element_type=F32)
                    + lax.dot_general(refs[0][...].astype(BF16), halves[1].astype(BF16), dims,
                                      preferred_element_type=F32))
        else:
            part = lax.dot_general(lhs, b_ref[...].astype(BF16), dims, preferred_element_type=F32)
        if gk == 1:
            finish(part)
            return
        acc_ref = refs[-1]

        @pl.when(kk == 0)
        def _():
            acc_ref[...] = part

        @pl.when((kk > 0) & (kk < gk - 1))
        def _():
            acc_ref[...] += part

        @pl.when(kk == gk - 1)
        def _():
            finish(acc_ref[...] + part)

    if n_a == 2:
        a_specs = [pl.BlockSpec((None, tm, k // 2), lambda i, kk: (0, i, 0)),
                   pl.BlockSpec((None, tm, k // 2), lambda i, kk: (1, i, 0))]
    elif a_split:
        assert gk % 2 == 0
        a_specs = [pl.BlockSpec((None, tm, tk), lambda i, kk: (kk // (gk // 2), i, kk % (gk // 2)))]
    else:
        a_specs = [pl.BlockSpec((tm, tk), lambda i, kk: (i, kk))]
    b_spec = pl.BlockSpec((n, tk), lambda i, kk: (0, kk)) if tb else pl.BlockSpec((tk, n), lambda i, kk: (kk, 0))
    row_spec = pl.BlockSpec((tm, n), lambda i, kk: (i, 0))
    out_specs, out_shape = [], []
    if prologue is not None:
        out_specs.append(pl.BlockSpec((tm, k), lambda i, kk: (i, 0)))
        out_shape.append(jax.ShapeDtypeStruct((m, k), BF16))
    out_specs += [row_spec] * len(row_outs)
    out_shape += [jax.ShapeDtypeStruct((m, n), dt) for dt in row_outs]
    if transposed_out is not None:
        out_specs.append(pl.BlockSpec((n, tm), lambda i, kk: (0, i)))
        out_shape.append(jax.ShapeDtypeStruct((n, m), transposed_out))
    out_specs += [pl.BlockSpec((1, n), lambda i, kk: (0, 0))] * n_vec_out
    out_shape += [jax.ShapeDtypeStruct((1, n), F32)] * n_vec_out
    sec_specs, sec_args = [], []
    if second is not None:
        k2 = second[0].shape[1]
        sec_specs = [pl.BlockSpec((tm, k2), lambda i, kk: (i, 0)), pl.BlockSpec((k2, n), lambda i, kk: (0, 0))]
        sec_args = list(second)
    res = pl.pallas_call(
        body, name=name, grid=(gm, gk),
        in_specs=a_specs + [b_spec] + sec_specs + [row_spec] * nr
        + [pl.BlockSpec((1, v.shape[1]), lambda i, kk: (0, 0)) for v in vecs],
        out_specs=out_specs, out_shape=out_shape,
        scratch_shapes=[] if gk == 1 else [pltpu.VMEM((tm, n), F32)],
        compiler_params=_params(("arbitrary" if n_vec_out else "parallel", "arbitrary")),
    )(*([a] * n_a), b, *sec_args, *rows, *vecs)
    split = n_pro + len(row_outs) + n_t
    return list(res[:split]), list(res[split:])


def _pro_rms(a, vecs):
    return (a * _rms_scale(a) * vecs[0]).astype(BF16)


def _epi_plain(r, rows, vecs):
    return [r], []


def _epi_add_rms(r, rows, vecs):
    xn = r + rows[0]
    return [xn, xn * _rms_scale(xn) * vecs[0]], []


def _epi_rms_bwd(r, rows, vecs):
    dv, dg_rows = _rms_bwd_rows(rows[0], vecs[0], r)
    return [dv + rows[1]], [jnp.sum(dg_rows, axis=0, keepdims=True)]


def _mix_pre(z, conv_w8, gq, gkv, cos, sin_a, sin_b, *, ts=256):
    s = z.shape[0]
    n = s // ts
    cw = CONV_WIDTH

    def body(z_ref, xcp, xcn, cgp, cgn, w_ref, gq_ref, gkv_ref, cos_ref, sa_ref, sb_ref,
             yc_ref, qn_ref, kvn_ref, kr_ref):
        i = pl.program_id(0)
        xc = z_ref[:, 0:cw]
        bg = z_ref[:, cw:2 * cw]
        cg = z_ref[:, 2 * cw:3 * cw]
        m = cg * xc
        m_prev = jnp.where(i > 0, xcp[7:8, :] * cgp[7:8, :], 0.0)
        m_next = jnp.where(i < n - 1, xcn[0:1, :] * cgn[0:1, :], 0.0)
        cm = _shift_down(m, m_prev) * w_ref[0:1, :] + m * w_ref[1:2, :] + _shift_up(m, m_next) * w_ref[2:3, :]
        yc_ref[...] = (bg * cm).astype(BF16)
        ql = z_ref[:, 3 * cw:3 * cw + Q_LORA]
        qn_ref[...] = (ql * _rms_scale(ql) * gq_ref[...]).astype(BF16)
        kvl = z_ref[:, 3 * cw + Q_LORA:3 * cw + Q_LORA + KV_LORA]
        kvn_ref[...] = (kvl * _rms_scale(kvl) * gkv_ref[...]).astype(BF16)
        kr_ref[...] = _rope(z_ref[:, D_IN_PAD - HEAD_PAD:D_IN_PAD], cos_ref[...], sa_ref[...], sb_ref[...])

    xcp, xcn = _halo_specs(ts, s, cw, 0)
    cgp, cgn = _halo_specs(ts, s, cw, 2)
    tab = pl.BlockSpec((ts, HEAD_PAD), lambda i: (i, 0))
    return pl.pallas_call(
        body, name="mix_pre", grid=(n,),
        in_specs=[pl.BlockSpec((ts, D_IN_PAD), lambda i: (i, 0)), xcp, xcn, cgp, cgn,
                  pl.BlockSpec((8, cw), lambda i: (0, 0)), pl.BlockSpec((1, Q_LORA), lambda i: (0, 0)),
                  pl.BlockSpec((1, KV_LORA), lambda i: (0, 0)), tab, tab, tab],
        out_specs=[pl.BlockSpec((ts, cw), lambda i: (i, 0)), pl.BlockSpec((ts, Q_LORA), lambda i: (i, 0)),
                   pl.BlockSpec((ts, KV_LORA), lambda i: (i, 0)), tab],
        out_shape=[jax.ShapeDtypeStruct((s, cw), BF16), jax.ShapeDtypeStruct((s, Q_LORA), BF16),
                   jax.ShapeDtypeStruct((s, KV_LORA), BF16), jax.ShapeDtypeStruct((s, HEAD_PAD), F32)],
        compiler_params=_params(("parallel",)),
    )(z, z, z, z, z, conv_w8, gq, gkv, cos, sin_a, sin_b)


def _mix_bwd(z, dyc, dqn, dkvn, dkr, conv_w8, gq, gkv, cos, sin_a, sin_b, *, ts=256):
    s = z.shape[0]
    n = s // ts
    cw = CONV_WIDTH

    def body(z_ref, xcp, xcn, bgp, bgn, cgp, cgn, dyc_ref, dycp, dycn, dqn_ref, dkvn_ref, dkr_ref,
             w_ref, gq_ref, gkv_ref, cos_ref, sa_ref, sb_ref,
             dz_ref, dw0_ref, dw1_ref, dw2_ref, dgq_ref, dgkv_ref):
        i = pl.program_id(0)

        @pl.when(i == 0)
        def _():
            for r in (dw0_ref, dw1_ref, dw2_ref, dgq_ref, dgkv_ref):
                r[...] = jnp.zeros_like(r)

        xc = z_ref[:, 0:cw]
        bg = z_ref[:, cw:2 * cw]
        cg = z_ref[:, 2 * cw:3 * cw]
        w0, w1, w2 = w_ref[0:1, :], w_ref[1:2, :], w_ref[2:3, :]
        m = cg * xc
        m_dn = _shift_down(m, jnp.where(i > 0, xcp[7:8, :] * cgp[7:8, :], 0.0))
        m_up = _shift_up(m, jnp.where(i < n - 1, xcn[0:1, :] * cgn[0:1, :], 0.0))
        cm = m_dn * w0 + m * w1 + m_up * w2
        dyc_v = dyc_ref[...]
        dcm = dyc_v * bg
        dcm_dn = _shift_down(dcm, jnp.where(i > 0, dycp[7:8, :] * bgp[7:8, :], 0.0))
        dcm_up = _shift_up(dcm, jnp.where(i < n - 1, dycn[0:1, :] * bgn[0:1, :], 0.0))
        dm = dcm_up * w0 + dcm * w1 + dcm_dn * w2
        dz_ref[:, 0:cw] = (dm * cg).astype(BF16)
        dz_ref[:, cw:2 * cw] = (dyc_v * cm).astype(BF16)
        dz_ref[:, 2 * cw:3 * cw] = (dm * xc).astype(BF16)
        dw0_ref[...] += jnp.sum(dcm * m_dn, axis=0, keepdims=True)
        dw1_ref[...] += jnp.sum(dcm * m, axis=0, keepdims=True)
        dw2_ref[...] += jnp.sum(dcm * m_up, axis=0, keepdims=True)

        dql, dgq_rows = _rms_bwd_rows(z_ref[:, 3 * cw:3 * cw + Q_LORA], gq_ref[...], dqn_ref[...])
        dz_ref[:, 3 * cw:3 * cw + Q_LORA] = dql.astype(BF16)
        dgq_ref[...] += jnp.sum(dgq_rows, axis=0, keepdims=True)
        dkvl, dgkv_rows = _rms_bwd_rows(z_ref[:, 3 * cw + Q_LORA:3 * cw + Q_LORA + KV_LORA], gkv_ref[...],
                                        dkvn_ref[...])
        dz_ref[:, 3 * cw + Q_LORA:3 * cw + Q_LORA + KV_LORA] = dkvl.astype(BF16)
        dgkv_ref[...] += jnp.sum(dgkv_rows, axis=0, keepdims=True)

        lane = lax.broadcasted_iota(jnp.int32, (ts, HEAD_PAD), 1)
        rope_lane = (lane >= QK_NOPE) & (lane < QK_NOPE + QK_ROPE)
        dk = _rope_bwd(dkr_ref[...], cos_ref[...], sa_ref[...], sb_ref[...])
        dz_ref[:, D_IN_PAD - HEAD_PAD:D_IN_PAD] = jnp.where(rope_lane, dk, 0.0).astype(BF16)

    xcp, xcn = _halo_specs(ts, s, cw, 0)
    bgp, bgn = _halo_specs(ts, s, cw, 1)
    cgp, cgn = _halo_specs(ts, s, cw, 2)
    dycp, dycn = _halo_specs(ts, s, cw, 0)
    tab = pl.BlockSpec((ts, HEAD_PAD), lambda i: (i, 0))

    def vec(width):
        return pl.BlockSpec((1, width), lambda i: (0, 0))

    outs = pl.pallas_call(
        body, name="mix_bwd", grid=(n,),
        in_specs=[pl.BlockSpec((ts, D_IN_PAD), lambda i: (i, 0)), xcp, xcn, bgp, bgn, cgp, cgn,
                  pl.BlockSpec((ts, cw), lambda i: (i, 0)), dycp, dycn,
                  pl.BlockSpec((ts, Q_LORA), lambda i: (i, 0)), pl.BlockSpec((ts, KV_LORA), lambda i: (i, 0)), tab,
                  pl.BlockSpec((8, cw), lambda i: (0, 0)), vec(Q_LORA), vec(KV_LORA), tab, tab, tab],
        out_specs=[pl.BlockSpec((ts, D_IN_PAD), lambda i: (i, 0)), vec(cw), vec(cw), vec(cw), vec(Q_LORA),
                   vec(KV_LORA)],
        out_shape=[jax.ShapeDtypeStruct((s, D_IN_PAD), BF16)] + [jax.ShapeDtypeStruct((1, cw), F32)] * 3
        + [jax.ShapeDtypeStruct((1, Q_LORA), F32), jax.ShapeDtypeStruct((1, KV_LORA), F32)],
        compiler_params=_params(("arbitrary",)),
    )(z, z, z, z, z, z, z, dyc, dyc, dyc, dqn, dkvn, dkr, conv_w8, gq, gkv, cos, sin_a, sin_b)
    dz, dw0, dw1, dw2, dgq, dgkv = outs
    return dz, jnp.concatenate([dw0, dw1, dw2], axis=0), dgq, dgkv


def _qkv_proj(qn, kvn, kr, w_uq_p, w_kv_p, cos, sin_a, sin_b, *, ts=512):
    s = qn.shape[0]

    def body(qn_ref, kvn_ref, kr_ref, wq_ref, wkv_ref, cos_ref, sa_ref, sb_ref, q_ref, k_ref, v_ref, qt_ref):
        cos_v, sa, sb = cos_ref[...], sa_ref[...], sb_ref[...]
        q = jnp.dot(qn_ref[...], wq_ref[...], preferred_element_type=F32)
        kv = jnp.dot(kvn_ref[...], wkv_ref[...], preferred_element_type=F32)
        kr_v = kr_ref[...]
        lane = lax.broadcasted_iota(jnp.int32, (1, HEAD_PAD), 1)
        ones_lane = (lane == ONES_LANE).astype(F32)
        for h in range(N_HEADS):
            blk = slice(h * HEAD_PAD, (h + 1) * HEAD_PAD)
            q_h = _rope(q[:, blk], cos_v, sa, sb) * SM_SCALE
            q_ref[:, blk] = q_h.astype(BF16)
            qt_ref[blk, :] = q_h.T.astype(BF16)
            k_ref[:, blk] = (kv[:, blk] + kr_v).astype(BF16)
            v_ref[:, blk] = (kv[:, D_ATT + h * HEAD_PAD:D_ATT + (h + 1) * HEAD_PAD] + ones_lane).astype(BF16)

    tab = pl.BlockSpec((ts, HEAD_PAD), lambda i: (i, 0))
    wide = pl.BlockSpec((ts, D_ATT), lambda i: (i, 0))
    return pl.pallas_call(
        body, name="qkv_proj", grid=(s // ts,),
        in_specs=[pl.BlockSpec((ts, Q_LORA), lambda i: (i, 0)), pl.BlockSpec((ts, KV_LORA), lambda i: (i, 0)), tab,
                  pl.BlockSpec((Q_LORA, D_ATT), lambda i: (0, 0)), pl.BlockSpec((KV_LORA, 2 * D_ATT), lambda i: (0, 0)),
                  tab, tab, tab],
        out_specs=[wide, wide, wide, pl.BlockSpec((D_ATT, ts), lambda i: (0, i))],
        out_shape=[jax.ShapeDtypeStruct((s, D_ATT), BF16)] * 3 + [jax.ShapeDtypeStruct((D_ATT, s), BF16)],
        compiler_params=_params(("parallel",)),
    )(qn, kvn, kr, w_uq_p, w_kv_p, cos, sin_a, sin_b)


def _qk_bwd(dq, dk, cos, sin_a, sin_b, *, ts=256):
    s = dq.shape[0]

    def body(dq_ref, dk_ref, cos_ref, sa_ref, sb_ref, dqp_ref, dkr_ref):
        cos_v, sa, sb = cos_ref[...], sa_ref[...], sb_ref[...]
        tot = jnp.zeros((ts, HEAD_PAD), F32)
        for h in range(N_HEADS):
            blk = slice(h * HEAD_PAD, (h + 1) * HEAD_PAD)
            dqp_ref[:, blk] = _rope_bwd(dq_ref[:, blk], cos_v, sa, sb).astype(BF16)
            tot = tot + dk_ref[:, blk]
        dkr_ref[...] = tot

    tab = pl.BlockSpec((ts, HEAD_PAD), lambda i: (i, 0))
    wide = pl.BlockSpec((ts, D_ATT), lambda i: (i, 0))
    return pl.pallas_call(
        body, name="qk_bwd", grid=(s // ts,),
        in_specs=[wide, wide, tab, tab, tab], out_specs=[wide, tab],
        out_shape=[jax.ShapeDtypeStruct((s, D_ATT), BF16), jax.ShapeDtypeStruct((s, HEAD_PAD), F32)],
        compiler_params=_params(("parallel",)),
    )(dq, dk, cos, sin_a, sin_b)


_NT = (((1,), (1,)), ((), ()))
_TN = (((0,), (0,)), ((), ()))


def _flash_fwd(q, k, v, *, tq=1024, tk=1024):
    s = q.shape[0]
    tq, tk = min(tq, s), min(tk, s)
    nk = s // tk

    def body(q_ref, k_ref, v_ref, o_ref, lse_ref):
        qv = q_ref[...]

        def step(j, carry):
            m, acc = carry
            rows = pl.ds(pl.multiple_of(j * tk, tk), tk)
            sc = lax.dot_general(qv, k_ref[rows, :], _NT, preferred_element_type=F32)
            m_new = jnp.maximum(m, jnp.max(sc, axis=1, keepdims=True))
            p = jnp.exp(sc - m_new).astype(BF16)
            acc = jnp.exp(m - m_new) * acc + jnp.dot(p, v_ref[rows, :], preferred_element_type=F32)
            return m_new, acc

        init = (jnp.full((tq, 1), -jnp.inf, F32), jnp.zeros((tq, HEAD_PAD), F32))
        m, acc = lax.fori_loop(0, nk, step, init)
        l = acc[:, ONES_LANE:ONES_LANE + 1]
        o_ref[...] = (acc / l).astype(BF16)
        lse_ref[...] = m + jnp.log(l)

    head = pl.BlockSpec((s, HEAD_PAD), lambda h, i: (0, h))
    return pl.pallas_call(
        body, name="flash_fwd", grid=(N_HEADS, s // tq),
        in_specs=[pl.BlockSpec((tq, HEAD_PAD), lambda h, i: (i, h)), head, head],
        out_specs=[pl.BlockSpec((tq, HEAD_PAD), lambda h, i: (i, h)),
                   pl.BlockSpec((None, tq, 1), lambda h, i: (h, i, 0))],
        out_shape=[jax.ShapeDtypeStruct((s, D_ATT), BF16), jax.ShapeDtypeStruct((N_HEADS, s, 1), F32)],
        compiler_params=_params(("parallel", "parallel")),
    )(q, k, v)


def _attn_delta(do, o, *, ts=512):
    s = do.shape[0]

    def body(do_ref, o_ref, dl_ref):
        for h in range(N_HEADS):
            blk = slice(h * HEAD_PAD, (h + 1) * HEAD_PAD)
            dl_ref[h] = jnp.sum(do_ref[:, blk].astype(F32) * o_ref[:, blk].astype(F32), axis=1, keepdims=True)

    wide = pl.BlockSpec((ts, D_ATT), lambda i: (i, 0))
    return pl.pallas_call(
        body, name="attn_delta", grid=(s // ts,), in_specs=[wide, wide],
        out_specs=pl.BlockSpec((N_HEADS, ts, 1), lambda i: (0, i, 0)),
        out_shape=jax.ShapeDtypeStruct((N_HEADS, s, 1), F32),
        compiler_params=_params(("parallel",)),
    )(do, o)


def _flash_bwd(q, qt, k, v, do, dot, lse, delta, after, *, tq=1024, tk=512):
    s = q.shape[0]
    tq, tk = min(tq, s), min(tk, s)
    nq = s // tq

    def body(q_ref, qt_ref, do_ref, dot_ref, lse_ref, dl_ref, k_ref, v_ref, after_ref, dq_ref, dk_ref, dv_ref):
        j = pl.program_id(1)

        @pl.when(j == 0)
        def _():
            dq_ref[...] = jnp.zeros_like(dq_ref)

        kv, vv = k_ref[...], v_ref[...]

        def step(i, carry):
            dk_t, dv_t = carry
            at = pl.multiple_of(i * tq, tq)
            rows = pl.ds(at, tq)
            sc = lax.dot_general(q_ref[rows, :], kv, _NT, preferred_element_type=F32)
            p = jnp.exp(sc - lse_ref[rows, :])
            dp = lax.dot_general(do_ref[rows, :], vv, _NT, preferred_element_type=F32)
            ds = (p * (dp - dl_ref[rows, :])).astype(BF16)
            dv_t = dv_t + jnp.dot(dot_ref[:, rows], p.astype(BF16), preferred_element_type=F32)
            dk_t = dk_t + jnp.dot(qt_ref[:, rows], ds, preferred_element_type=F32)
            dq_ref[rows, :] += jnp.dot(ds, kv, preferred_element_type=F32)
            return dk_t, dv_t

        zero = jnp.zeros((HEAD_PAD, tk), F32)
        dk_t, dv_t = lax.fori_loop(0, nq, step, (zero, zero))
        dk_ref[...] = dk_t.T
        dv_ref[...] = dv_t.T

        @pl.when(j == pl.num_programs(1) - 1)
        def _():
            dq_ref[...] *= SM_SCALE

    head = pl.BlockSpec((s, HEAD_PAD), lambda h, j: (0, h))
    head_t = pl.BlockSpec((HEAD_PAD, s), lambda h, j: (h, 0))
    stat = pl.BlockSpec((None, s, 1), lambda h, j: (h, 0, 0))
    blk = pl.BlockSpec((tk, HEAD_PAD), lambda h, j: (j, h))
    return pl.pallas_call(
        body, name="flash_bwd", grid=(N_HEADS, s // tk),
        in_specs=[head, head_t, head, head_t, stat, stat, blk, blk, ANY],
        out_specs=[head, blk, blk],
        out_shape=[jax.ShapeDtypeStruct((s, D_ATT), F32)] * 3,
        compiler_params=_params(("parallel", "arbitrary")),
    )(q, qt, do, dot, lse, delta, k, v, after)


FFN_TC = 256


FFN_HALO_BF16 = 16
FFN_HALO_F32 = 8


def _row_halo_specs(ts, s, halo, width):
    nb = ts // halo
    last = s // halo - 1
    prev = pl.BlockSpec((halo, width), lambda i, j: (jnp.maximum(i * nb - 1, 0), 0))
    nxt = pl.BlockSpec((halo, width), lambda i, j: (jnp.minimum((i + 1) * nb, last), 0))
    return prev, nxt


def _ext_rows(prev, main, nxt, first, last):
    return jnp.concatenate([jnp.where(first, jnp.zeros_like(prev), prev), main,
                            jnp.where(last, jnp.zeros_like(nxt), nxt)], axis=0)


def _ext_conv(a, w):
    a_dn = pltpu.roll(a, 1, 0)
    a_up = pltpu.roll(a, a.shape[0] - 1, 0)
    return a_dn * w[0:1, :] + a * w[1:2, :] + a_up * w[2:3, :], a_dn, a_up


def _ffn_fwd(hf, w_up, w, b, *, ts=512, tc=FFN_TC):
    s = hf.shape[0]
    n, nj, halo = s // ts, D_FF // tc, FFN_HALO_BF16

    def body(h_ref, hp_ref, hn_ref, wg_ref, wu_ref, cw_ref, cb_ref, a_ref, act_ref):
        i = pl.program_id(0)
        ext = _ext_rows(hp_ref[...], h_ref[...], hn_ref[...], i == 0, i == n - 1)
        gate_up = []
        for half, w_ref in enumerate((wg_ref, wu_ref)):
            a_ext = jnp.dot(ext, w_ref[...], preferred_element_type=F32)
            a_ref[half] = a_ext[halo:halo + ts]
            gate_up.append(_ext_conv(a_ext, cw_ref[half])[0][halo:halo + ts] + cb_ref[half])
        g, u = gate_up
        act_ref[...] = (g * _sigmoid(g) * u).astype(BF16)

    prev, nxt = _row_halo_specs(ts, s, halo, D_MODEL)
    return pl.pallas_call(
        body, name="ffn_fwd", grid=(n, nj),
        in_specs=[pl.BlockSpec((ts, D_MODEL), lambda i, j: (i, 0)), prev, nxt,
                  pl.BlockSpec((D_MODEL, tc), lambda i, j: (0, j)), pl.BlockSpec((D_MODEL, tc), lambda i, j: (0, j + nj)),
                  pl.BlockSpec((2, 8, tc), lambda i, j: (0, 0, j)), pl.BlockSpec((2, 1, tc), lambda i, j: (0, 0, j))],
        out_specs=[pl.BlockSpec((2, ts, tc), lambda i, j: (0, i, j)), pl.BlockSpec((ts, tc), lambda i, j: (i, j))],
        out_shape=[jax.ShapeDtypeStruct((2, s, D_FF), F32), jax.ShapeDtypeStruct((s, D_FF), BF16)],
        compiler_params=_params(("parallel", "parallel")),
    )(hf, hf, hf, w_up, w_up, w, b)


def _ffn_bwd(dx2, w_down, a_pre, w, b, *, ts=512, tc=FFN_TC):
    s = dx2.shape[0]
    n, nj, halo = s // ts, D_FF // tc, FFN_HALO_F32
    main = slice(halo, halo + ts)

    def body(dx_ref, dxp_ref, dxn_ref, wd_ref, a_ref, ap_ref, an_ref, cw_ref, cb_ref, o_ref, dw_ref, db_ref):
        i, j = pl.program_id(0), pl.program_id(1)
        first, last = i == 0, i == n - 1

        @pl.when(first & (j == 0))
        def _():
            dw_ref[...] = jnp.zeros_like(dw_ref)
            db_ref[...] = jnp.zeros_like(db_ref)

        dx_ext = _ext_rows(dxp_ref[...], dx_ref[...], dxn_ref[...], first, last).astype(BF16)
        dact = lax.dot_general(dx_ext, wd_ref[...], _NT, preferred_element_type=F32)
        halves = []
        for half in range(2):
            a_ext = _ext_rows(ap_ref[half], a_ref[half], an_ref[half], first, last)
            conv, a_dn, a_up = _ext_conv(a_ext, cw_ref[half])
            halves.append((conv + cb_ref[half], a_dn, a_ext, a_up))
        g, u = halves[0][0], halves[1][0]
        sg = _sigmoid(g)
        grads = (dact * u * (sg * (1.0 + g * (1.0 - sg))), dact * (g * sg))
        for half in range(2):
            d = grads[half]
            _, a_dn, a_ext, a_up = halves[half]
            wv = cw_ref[half]
            d_pre = pltpu.roll(d, d.shape[0] - 1, 0) * wv[0:1, :] + d * wv[1:2, :] + pltpu.roll(d, 1, 0) * wv[2:3, :]
            o_ref[half] = d_pre[main].astype(BF16)
            dm = d[main]
            dw_ref[j, half, 0:1, :] += jnp.sum(dm * a_dn[main], axis=0, keepdims=True)
            dw_ref[j, half, 1:2, :] += jnp.sum(dm * a_ext[main], axis=0, keepdims=True)
            dw_ref[j, half, 2:3, :] += jnp.sum(dm * a_up[main], axis=0, keepdims=True)
            db_ref[j, half] += jnp.sum(dm, axis=0, keepdims=True)

    dxp, dxn = _row_halo_specs(ts, s, halo, D_MODEL)
    nb, lastb = ts // halo, s // halo - 1
    a_main = pl.BlockSpec((2, ts, tc), lambda i, j: (0, i, j))
    a_prev = pl.BlockSpec((2, halo, tc), lambda i, j: (0, jnp.maximum(i * nb - 1, 0), j))
    a_next = pl.BlockSpec((2, halo, tc), lambda i, j: (0, jnp.minimum((i + 1) * nb, lastb), j))
    da_pre, dw, db = pl.pallas_call(
        body, name="ffn_bwd", grid=(n, nj),
        in_specs=[pl.BlockSpec((ts, D_MODEL), lambda i, j: (i, 0)), dxp, dxn,
                  pl.BlockSpec((tc, D_MODEL), lambda i, j: (j, 0)), a_main, a_prev, a_next,
                  pl.BlockSpec((2, 8, tc), lambda i, j: (0, 0, j)), pl.BlockSpec((2, 1, tc), lambda i, j: (0, 0, j))],
        out_specs=[a_main, pl.BlockSpec((nj, 2, 8, tc), lambda i, j: (0, 0, 0, 0)),
                   pl.BlockSpec((nj, 2, 1, tc), lambda i, j: (0, 0, 0, 0))],
        out_shape=[jax.ShapeDtypeStruct((2, s, D_FF), BF16), jax.ShapeDtypeStruct((nj, 2, 8, tc), F32),
                   jax.ShapeDtypeStruct((nj, 2, 1, tc), F32)],
        compiler_params=_params(("arbitrary", "arbitrary")),
    )(dx2, dx2, dx2, w_down, a_pre, a_pre, a_pre, w, b)
    return (da_pre, dw.transpose(1, 2, 0, 3).reshape(2, 8, D_FF), db.transpose(1, 2, 0, 3).reshape(2, 1, D_FF))


def _ple_final(x2, n3, p, target, gf, w_pg, w_pp, *, ts=256):
    s, d = x2.shape
    dp = p.shape[1]

    def body(x2_ref, n3_ref, p_ref, t_ref, gf_ref, wg_ref, wp_ref, loss_ref, dx3_ref, dgl_ref, dpp_ref, dgf_ref):
        @pl.when(pl.program_id(0) == 0)
        def _():
            loss_ref[...] = jnp.zeros_like(loss_ref)
            dgf_ref[...] = jnp.zeros_like(dgf_ref)

        gate = _sigmoid(jnp.dot(n3_ref[...], wg_ref[...], preferred_element_type=F32))
        ppv = jnp.dot(p_ref[...].astype(BF16), wp_ref[...], preferred_element_type=F32)
        x3 = x2_ref[...] + gate * ppv
        gfv = gf_ref[...]
        err = x3 * _rms_scale(x3) * gfv - t_ref[...]
        loss_ref[...] += 0.5 * jnp.sum(jnp.mean(err * err, axis=-1, keepdims=True), axis=0, keepdims=True)
        dx3, dgf_rows = _rms_bwd_rows(x3, gfv, err * (1.0 / d))
        dgf_ref[...] += jnp.sum(dgf_rows, axis=0, keepdims=True)
        dx3_ref[...] = dx3
        dgl_ref[...] = (dx3 * ppv * gate * (1.0 - gate)).astype(BF16)
        dpp_ref[...] = (dx3 * gate).astype(BF16)

    row = pl.BlockSpec((ts, d), lambda i: (i, 0))
    vec = pl.BlockSpec((1, d), lambda i: (0, 0))
    return pl.pallas_call(
        body, name="ple_final", grid=(s // ts,),
        in_specs=[row, row, pl.BlockSpec((ts, dp), lambda i: (i, 0)), row, vec,
                  pl.BlockSpec((d, d), lambda i: (0, 0)), pl.BlockSpec((dp, d), lambda i: (0, 0))],
        out_specs=[pl.BlockSpec((1, 128), lambda i: (0, 0)), row, row, row, vec],
        out_shape=[jax.ShapeDtypeStruct((1, 128), F32), jax.ShapeDtypeStruct((s, d), F32),
                   jax.ShapeDtypeStruct((s, d), BF16), jax.ShapeDtypeStruct((s, d), BF16),
                   jax.ShapeDtypeStruct((1, d), F32)],
        compiler_params=_params(("arbitrary",)),
    )(x2, n3, p, target, gf, w_pg, w_pp)


def _row_tile(rows, cols, n_arrays, budget=12 << 20):
    best = None
    for t in range(8, rows + 1, 8):
        if rows % t == 0 and t * cols * 4 * n_arrays <= budget:
            best = t
    return rows if best is None else best


def _sum_slots(a, *, name):
    g, r, c = a.shape
    tr = _row_tile(r, c, g + 1)

    def body(*refs):
        tot = refs[0][...]
        for ref in refs[1:g]:
            tot = tot + ref[...]
        refs[g][...] = tot

    specs = [pl.BlockSpec((None, tr, c), functools.partial(lambda i, slot: (slot, i, 0), slot=k)) for k in range(g)]
    return pl.pallas_call(
        body, name=name, grid=(r // tr,), in_specs=specs, out_specs=pl.BlockSpec((tr, c), lambda i: (i, 0)),
        out_shape=jax.ShapeDtypeStruct((r, c), a.dtype), compiler_params=_params(("parallel",)),
    )(*([a] * g))


def _adamw(w, g, m, v, *, name):
    r, c = w.shape
    tr = _row_tile(r, c, 7)

    def body(w_ref, g_ref, m_ref, v_ref, d_ref, mo_ref, vo_ref):
        gv = g_ref[...]
        mn = ADAM_B1 * m_ref[...] + (1.0 - ADAM_B1) * gv
        vn = ADAM_B2 * v_ref[...] + (1.0 - ADAM_B2) * (gv * gv)
        m_hat = mn / (1.0 - ADAM_B1 ** ADAM_STEP)
        v_hat = vn / (1.0 - ADAM_B2 ** ADAM_STEP)
        d_ref[...] = -ADAM_LR * (m_hat / (jnp.sqrt(v_hat) + ADAM_EPS) + ADAM_WD * w_ref[...])
        mo_ref[...] = mn
        vo_ref[...] = vn

    blk = pl.BlockSpec((tr, c), lambda i: (i, 0))
    return pl.pallas_call(
        body, name=name, grid=(r // tr,), in_specs=[blk] * 4, out_specs=[blk] * 3,
        out_shape=[jax.ShapeDtypeStruct((r, c), F32)] * 3, compiler_params=_params(("parallel",)),
    )(w, g, m, v)


def _position():
    x, y, c = lax.axis_index("x"), lax.axis_index("y"), lax.axis_index("c")
    return x, y, c


def _other_chips(x, y):
    return [(1 - x, y), (x, 1 - y), (1 - x, 1 - y)]


def _stage_in(srcs, stage, sems):
    cps = [pltpu.make_async_copy(src, stage[a], sems.at[a]) for a, src in enumerate(srcs)]
    for cp in cps:
        cp.start()
    return cps


def _stage_out(staged, stage, dsts, sems):
    cps = []
    for a, dst in enumerate(dsts):
        staged[a].wait()
        cp = pltpu.make_async_copy(stage[a], dst, sems.at[a])
        cp.start()
        cps.append(cp)
    return cps


def _gather_chips(shards):
    n = len(shards)

    def body(*refs):
        ins, outs, stage = refs[:n], refs[n:2 * n], refs[2 * n:3 * n]
        send_sems, recv_sems, in_sems, out_sems = refs[3 * n:]
        x, y, c = _position()
        me = 2 * x + y
        chips = _other_chips(x, y)
        remote = []
        staged = _stage_in(ins, stage, in_sems)
        for a in range(n):
            for k, (px, py) in enumerate(chips):
                rc = pltpu.make_async_remote_copy(
                    src_ref=ins[a], dst_ref=outs[a].at[me], send_sem=send_sems.at[3 * a + k],
                    recv_sem=recv_sems.at[3 * a + k], device_id=(px, py, c), device_id_type=MESH)
                rc.start()
                remote.append(rc)
        local = _stage_out(staged, stage, [o.at[me] for o in outs], out_sems)
        for a in range(n):
            for k, (px, py) in enumerate(chips):
                pltpu.make_async_remote_copy(
                    src_ref=ins[a], dst_ref=outs[a].at[2 * px + py], send_sem=send_sems.at[3 * a + k],
                    recv_sem=recv_sems.at[3 * a + k], device_id=(px, py, c), device_id_type=MESH).wait_recv()
        for rc in remote:
            rc.wait_send()
        for cp in local:
            cp.wait()

    return pl.pallas_call(
        body, name="gather_chips", in_specs=[ANY] * n, out_specs=[ANY] * n,
        out_shape=[jax.ShapeDtypeStruct((N_CHIPS,) + s.shape, s.dtype) for s in shards],
        scratch_shapes=[pltpu.VMEM(s.shape, s.dtype) for s in shards]
        + [pltpu.SemaphoreType.DMA((3 * n,)), pltpu.SemaphoreType.DMA((3 * n,)),
           pltpu.SemaphoreType.DMA((n,)), pltpu.SemaphoreType.DMA((n,))],
        compiler_params=pltpu.CompilerParams(has_side_effects=True),
    )(*shards)


def _send_other_halves(grads, *, tag):
    n = len(grads)

    def body(*refs):
        ins, sib = refs[:n], refs[n:2 * n]
        send_sems, recv_sems = refs[2 * n:]
        x, y, c = _position()
        remote = []
        for a in range(n):
            half = ins[a].shape[1] // 2
            give = ins[a].at[:, pl.ds(pl.multiple_of((1 - c) * half, 8), half), :]
            rc = pltpu.make_async_remote_copy(
                src_ref=give, dst_ref=sib[a], send_sem=send_sems.at[a], recv_sem=recv_sems.at[a],
                device_id=(x, y, 1 - c), device_id_type=MESH)
            rc.start()
            remote.append(rc)
        for rc in remote:
            rc.wait_recv()
        for rc in remote:
            rc.wait_send()

    return pl.pallas_call(
        body, name="send_other_halves_" + tag, in_specs=[ANY] * n, out_specs=[ANY] * n,
        out_shape=[jax.ShapeDtypeStruct((g.shape[0], g.shape[1] // 2, g.shape[2]), g.dtype) for g in grads],
        scratch_shapes=[pltpu.SemaphoreType.DMA((n,)), pltpu.SemaphoreType.DMA((n,))],
        compiler_params=pltpu.CompilerParams(has_side_effects=True),
    )(*grads)


def _add_own_half(g4, sib, core, *, name):
    g, a2, c = sib.shape
    tr = _row_tile(a2, c, 4)

    def body(core_ref, a_ref, b_ref, o_ref, o16_ref):
        tot = a_ref[...] + b_ref[...]
        o_ref[...] = tot
        o16_ref[...] = tot.astype(BF16)

    blk = pl.BlockSpec((None, tr, c), lambda i, j, core_ref: (i, j, 0))
    return pl.pallas_call(
        body, name=name,
        grid_spec=pltpu.PrefetchScalarGridSpec(
            num_scalar_prefetch=1, grid=(g, a2 // tr),
            in_specs=[pl.BlockSpec((None, None, tr, c), lambda i, j, core_ref: (i, core_ref[0], j, 0)), blk],
            out_specs=[blk, blk]),
        out_shape=[jax.ShapeDtypeStruct(sib.shape, F32), jax.ShapeDtypeStruct(sib.shape, BF16)],
        compiler_params=_params(("parallel", "parallel")),
    )(core, g4.reshape(g, 2, a2, c), sib)


def _sum_chips(landed, own, chip, *, name):
    g, r, c = landed.shape
    tr = _row_tile(r, c, 5)

    def body(chip_ref, *refs):
        me = chip_ref[0]
        own_v = refs[g][...]
        tot = None
        for slot in range(g):
            term = jnp.where(me == slot, own_v, refs[slot][...].astype(F32))
            tot = term if tot is None else tot + term
        refs[g + 1][...] = tot

    def landed_spec(slot):
        return pl.BlockSpec((None, tr, c),
                            lambda i, chip_ref: (jnp.where(chip_ref[0] == slot, (slot + 1) % g, slot), i, 0))

    return pl.pallas_call(
        body, name=name,
        grid_spec=pltpu.PrefetchScalarGridSpec(
            num_scalar_prefetch=1, grid=(r // tr,),
            in_specs=[landed_spec(k) for k in range(g)]
            + [pl.BlockSpec((None, tr, c), lambda i, chip_ref: (chip_ref[0], i, 0))],
            out_specs=pl.BlockSpec((tr, c), lambda i, chip_ref: (i, 0))),
        out_shape=jax.ShapeDtypeStruct((r, c), F32), compiler_params=_params(("parallel",)),
    )(chip, *([landed] * g), own)


def _join_halves(halves):
    n = len(halves)

    def body(*refs):
        ins, outs, stage = refs[:n], refs[n:2 * n], refs[2 * n:3 * n]
        send_sems, recv_sems, in_sems, out_sems = refs[3 * n:]
        x, y, c = _position()
        remote = []
        staged = _stage_in(ins, stage, in_sems)
        for a in range(n):
            rc = pltpu.make_async_remote_copy(
                src_ref=ins[a], dst_ref=outs[a].at[c], send_sem=send_sems.at[a], recv_sem=recv_sems.at[a],
                device_id=(x, y, 1 - c), device_id_type=MESH)
            rc.start()
            remote.append(rc)
        local = _stage_out(staged, stage, [o.at[c] for o in outs], out_sems)
        for a in range(n):
            pltpu.make_async_remote_copy(
                src_ref=ins[a], dst_ref=outs[a].at[1 - c], send_sem=send_sems.at[a], recv_sem=recv_sems.at[a],
                device_id=(x, y, 1 - c), device_id_type=MESH).wait_recv()
        for rc in remote:
            rc.wait_send()
        for cp in local:
            cp.wait()

    return pl.pallas_call(
        body, name="join_halves", in_specs=[ANY] * n, out_specs=[ANY] * n,
        out_shape=[jax.ShapeDtypeStruct((2,) + h.shape, h.dtype) for h in halves],
        scratch_shapes=[pltpu.VMEM(h.shape, h.dtype) for h in halves]
        + [pltpu.SemaphoreType.DMA((n,)), pltpu.SemaphoreType.DMA((n,)), pltpu.SemaphoreType.DMA((n,)),
           pltpu.SemaphoreType.DMA((n,))],
        compiler_params=pltpu.CompilerParams(has_side_effects=True),
    )(*halves)


_HBM = pl.BlockSpec(memory_space=pltpu.HBM)
_SEM = pl.BlockSpec(memory_space=pltpu.SEMAPHORE)


def _chip_copies(srcs, lands, send_sems, recv_sems, scatter):
    x, y, c = _position()
    me = 2 * x + y
    outgoing, incoming = [], []
    for a, (src, land) in enumerate(zip(srcs, lands)):
        for k, (px, py) in enumerate(_other_chips(x, y)):
            peer = 2 * px + py
            sems = dict(send_sem=send_sems.at[3 * a + k], recv_sem=recv_sems.at[3 * a + k], device_id=(px, py, c),
                        device_id_type=MESH)
            outgoing.append(pltpu.make_async_remote_copy(
                src_ref=src.at[peer] if scatter else src, dst_ref=land.at[me], **sems))
            incoming.append(pltpu.make_async_remote_copy(
                src_ref=src.at[me] if scatter else src, dst_ref=land.at[peer], **sems))
    return outgoing, incoming


def _chips_start(srcs, *, scatter, name):
    n = len(srcs)
    lands = [lax.empty(a.shape if scatter else (N_CHIPS,) + a.shape, a.dtype) for a in srcs]

    def body(*refs):
        ins, send_sems, recv_sems, token = refs[:2 * n], refs[2 * n], refs[2 * n + 1], refs[-1]
        outgoing, _ = _chip_copies(ins[:n], ins[n:], send_sems, recv_sems, scatter)
        for cp in outgoing:
            cp.start()
        token[...] = jnp.zeros_like(token)

    bufs = list(srcs) + lands
    res = pl.pallas_call(
        body, name=name, in_specs=[_HBM] * (2 * n),
        out_specs=(_SEM, _SEM, *[_HBM] * (2 * n), pl.BlockSpec(memory_space=pltpu.VMEM)),
        out_shape=(pltpu.SemaphoreType.DMA((3 * n,)), pltpu.SemaphoreType.DMA((3 * n,)),
                   *[pltpu.HBM(a.shape, a.dtype) for a in bufs], jax.ShapeDtypeStruct((8, 128), F32)),
        input_output_aliases={i: 2 + i for i in range(2 * n)},
        compiler_params=pltpu.CompilerParams(has_side_effects=pltpu.SideEffectType.DATAFLOW_SIDE_EFFECTING),
    )(*[pltpu.with_memory_space_constraint(a, pltpu.HBM) for a in bufs])
    return res[0], res[1], list(res[2:2 + n]), list(res[2 + n:2 + 2 * n]), res[-1]


def _chips_wait(handle, after, *, scatter, name):
    send_sems, recv_sems, srcs, lands, _ = handle
    n = len(srcs)

    def body(*refs):
        ins, send_ref, recv_ref = refs[:2 * n], refs[2 * n], refs[2 * n + 1]
        outgoing, incoming = _chip_copies(ins[:n], ins[n:], send_ref, recv_ref, scatter)
        for cp in outgoing:
            cp.wait_send()
        for cp in incoming:
            cp.wait_recv()

    bufs = list(srcs) + list(lands)
    res = pl.pallas_call(
        body, name=name, in_specs=[_HBM] * (2 * n) + [_SEM, _SEM, ANY], out_specs=tuple([_HBM] * (2 * n)),
        out_shape=tuple(pltpu.HBM(a.shape, a.dtype) for a in bufs),
        input_output_aliases={i: i for i in range(2 * n)},
        compiler_params=pltpu.CompilerParams(has_side_effects=pltpu.SideEffectType.DATAFLOW_SIDE_EFFECTING),
    )(*bufs, send_sems, recv_sems, after)
    return list(res[:n]), list(res[n:])


def _gather_all(buf):
    def body(in_ref, out_ref, send_sems, recv_sems, local_sem):
        x, y, c = _position()
        me = 4 * x + 2 * y + c
        peers = [(x, y, 1 - c)] + [(px, py, pc) for (px, py) in _other_chips(x, y) for pc in (c, 1 - c)]
        cp = pltpu.make_async_copy(in_ref, out_ref.at[me], local_sem)
        cp.start()
        remote = []
        for k, peer in enumerate(peers):
            rc = pltpu.make_async_remote_copy(
                src_ref=in_ref, dst_ref=out_ref.at[me], send_sem=send_sems.at[k], recv_sem=recv_sems.at[k],
                device_id=peer, device_id_type=MESH)
            rc.start()
            remote.append(rc)
        for k, (px, py, pc) in enumerate(peers):
            pltpu.make_async_remote_copy(
                src_ref=in_ref, dst_ref=out_ref.at[4 * px + 2 * py + pc], send_sem=send_sems.at[k],
                recv_sem=recv_sems.at[k], device_id=(px, py, pc), device_id_type=MESH).wait_recv()
        for rc in remote:
            rc.wait_send()
        cp.wait()

    return pl.pallas_call(
        body, name="gather_all", in_specs=[ANY], out_specs=ANY,
        out_shape=jax.ShapeDtypeStruct((N_DEV,) + buf.shape, buf.dtype),
        scratch_shapes=[pltpu.SemaphoreType.DMA((N_DEV - 1,)), pltpu.SemaphoreType.DMA((N_DEV - 1,)),
                        pltpu.SemaphoreType.DMA],
        compiler_params=pltpu.CompilerParams(has_side_effects=True),
    )(buf)


def _cols_from_shards(g4):
    _, k, n = g4.shape
    return g4.transpose(1, 0, 2).reshape(k, N_CHIPS * n)


def _cols_to_shards(w):
    k, n = w.shape
    return w.reshape(k, N_CHIPS, n // N_CHIPS).transpose(1, 0, 2)


def _pad_heads(w, width):
    k = w.shape[0]
    w3 = w.reshape(k, N_HEADS, width)
    return jnp.pad(w3, ((0, 0), (0, 0), (0, HEAD_PAD - width))).reshape(k, D_ATT)


def _unpad_heads(w, width):
    k = w.shape[0]
    return w.reshape(k, N_HEADS, HEAD_PAD)[:, :, :width]


def _rope_tables(s):
    pos = jnp.arange(s, dtype=F32)
    inv_freq = ROPE_THETA ** (-jnp.arange(0, QK_ROPE, 2, dtype=F32) / QK_ROPE)
    ang = pos[:, None] * inv_freq[None, :]
    cos_h, sin_h = jnp.cos(ang), jnp.sin(ang)
    half = QK_ROPE // 2
    z = jnp.zeros((s, half), F32)
    ones = jnp.ones((s, QK_NOPE), F32)
    tail = jnp.zeros((s, HEAD_PAD - QK_NOPE - QK_ROPE), F32)
    cos = jnp.concatenate([ones, cos_h, cos_h, tail + 1.0], axis=1)
    sin_a = jnp.concatenate([ones * 0.0, -sin_h, z, tail], axis=1)
    sin_b = jnp.concatenate([ones * 0.0, z, sin_h, tail], axis=1)
    return cos, sin_a, sin_b


def _local_step(x, p, target, wts, late_weights, reduce_early, reduce_last):
    s = x.shape[0]
    cos, sin_a, sin_b = _rope_tables(s)
    g1, gq, gkv, g2, g3, gf = (wts[k] for k in ("norm_mix_g", "q_norm_g", "kv_norm_g", "norm_ffn_g", "ple_norm_g",
                                                 "final_norm_g"))
    w_in_p, w_uq_p, w_kv_p = wts["w_in_p"], wts["w_uq_p"], wts["w_kv_p"]
    conv_w8, fconv_w, fconv_b = wts["conv_w8"], wts["ffn_conv_w"], wts["ffn_conv_b"]

    (h, z), _ = _mm_fused(x, w_in_p, name="mm_in", prologue=_pro_rms, vecs=[g1], epilogue=_epi_plain, row_outs=[F32])
    y_conv, qn, kvn, kr = _mix_pre(z, conv_w8, gq, gkv, cos, sin_a, sin_b)
    q, k, v, q_t = _qkv_proj(qn, kvn, kr, w_uq_p, w_kv_p, cos, sin_a, sin_b)
    o, lse = _flash_fwd(q, k, v)
    late = late_weights(lse)
    w_o_a, w_o_b, w_up, w_down = late["w_o_a"], late["w_o_b"], late["w_up"], late["w_down"]
    w_pg, w_pp = late["w_ple_gate"], late["w_ple_proj"]
    (x1, hf), _ = _mm_fused(o, w_o_b, second=(y_conv, w_o_a), name="mm_o", rows=[x], vecs=[g2],
                            epilogue=_epi_add_rms, row_outs=[F32, BF16])
    a_pre, act = _ffn_fwd(hf, w_up, fconv_w, fconv_b)
    (x2, n3), _ = _mm_fused(act, w_down, name="mm_down", rows=[x1], vecs=[g3], epilogue=_epi_add_rms,
                            row_outs=[F32, BF16])
    loss, dx3, dgl, dpp, d_gf = _ple_final(x2, n3, p, target, gf, w_pg, w_pp)

    grads, early = {"final_norm_g": d_gf}, {}
    early["w_ple_proj"] = _mm(p, dpp, ta=True, name="mm_d_wpp", tm=256, tn=1024, tk=2048)
    early["w_ple_gate"] = _mm(n3, dgl, ta=True, name="mm_d_wpg", tm=512, tn=1024, tk=2048)
    (dx2,), (grads["ple_norm_g"],) = _mm_fused(dgl, w_pg, tb=True, name="mm_d_n3", rows=[x2, dx3], vecs=[g3],
                                               epilogue=_epi_rms_bwd, row_outs=[F32], n_vec_out=1)
    early["w_down"] = _mm(act, dx2, ta=True, name="mm_d_wdown", tm=1408, tn=512, tk=2048)
    da_pre, grads["ffn_conv_w"], grads["ffn_conv_b"] = _ffn_bwd(dx2, w_down, a_pre, fconv_w, fconv_b)
    early["w_up"] = _mm(hf, da_pre, ta=True, b_split=True, name="mm_d_wup", tm=512, tn=1408, tk=2048,
                       o_shards=True)
    (dx1,), (grads["norm_ffn_g"],) = _mm_fused(da_pre, w_up, tb=True, a_split=True, name="mm_d_hf", rows=[x1, dx2],
                                               vecs=[g2], epilogue=_epi_rms_bwd, row_outs=[F32], n_vec_out=1)
    d_wo_a = _mm(y_conv, dx1, ta=True, name="mm_d_wo_conv", tm=512, tn=1024, tk=2048)
    d_wo_b = _mm(o, dx1, ta=True, name="mm_d_wo_att", tm=512, tn=1024, tk=2048)
    early["w_o"] = jnp.concatenate([d_wo_a, d_wo_b.reshape(N_HEADS, HEAD_PAD, D_MODEL)[:, :V_HEAD]
                                    .reshape(N_HEADS * V_HEAD, D_MODEL)], axis=0)
    token, finish = reduce_early(early)
    dyc = _mm(dx1, w_o_a, tb=True, name="mm_d_yconv", tm=512, tn=512, tk=1024)
    (do, do_t), _ = _mm_fused(dx1, w_o_b, tb=True, name="mm_d_o", epilogue=_epi_plain, row_outs=[BF16],
                              transposed_out=BF16)
    delta = _attn_delta(do, o)
    dq, dk, dv = _flash_bwd(q, q_t, k, v, do, do_t, lse, delta, token)
    reduced_early = finish(dq)
    dq_pre, dkr = _qk_bwd(dq, dk, cos, sin_a, sin_b)
    grads["w_uq_p"] = _mm(qn, dq_pre, ta=True, name="mm_d_wuq", tm=256, tn=1024, tk=2048)
    dqn = _mm(dq_pre, w_uq_p, tb=True, name="mm_d_qn", tm=512, tn=256, tk=1024)
    grads["w_k_p"] = _mm(kvn, dk, ta=True, name="mm_d_wk", tm=128, tn=1024, tk=2048)
    grads["w_v_p"] = _mm(kvn, dv, ta=True, name="mm_d_wv", tm=128, tn=1024, tk=2048)
    dkvn_k = _mm(dk, w_kv_p[:, :D_ATT], tb=True, name="mm_d_kvn_k", tm=512, tn=128, tk=1024)
    dkvn = _mm(dv, w_kv_p[:, D_ATT:], tb=True, add=dkvn_k, name="mm_d_kvn_v", tm=512, tn=128, tk=1024)
    dz, grads["conv_w"], grads["q_norm_g"], grads["kv_norm_g"] = _mix_bwd(
        z, dyc, dqn, dkvn, dkr, conv_w8, gq, gkv, cos, sin_a, sin_b)
    grads["w_in_p"] = _mm(h, dz, ta=True, name="mm_d_win", tm=512, tn=1024, tk=2048)
    token, finish = reduce_last({n: grads.pop(n) for n in ("w_in_p", "w_uq_p", "w_k_p", "w_v_p")})
    (grad_x,), (grads["norm_mix_g"],) = _mm_fused(dz, w_in_p, tb=True, name="mm_d_h", rows=[x, dx1],
                                                  vecs=[g1 + token[0, 0]], epilogue=_epi_rms_bwd, row_outs=[F32],
                                                  n_vec_out=1)
    return loss[0, 0], grad_x, grads, reduced_early, finish(grad_x)


_EARLY_W = ("w_in", "w_uq", "w_ukv")
_LATE_W = ("w_o", "w_up", "w_down", "w_ple_gate", "w_ple_proj")
_BIG = _EARLY_W + _LATE_W
_COL_SHARDED = ("w_in", "w_uq", "w_ukv", "w_up", "w_ple_proj")
_SMALL = ("norm_mix_g", "conv_w", "q_norm_g", "kv_norm_g", "norm_ffn_g", "ffn_conv_w", "ffn_conv_b", "ple_norm_g",
          "final_norm_g")


def _full_from_slots(n, g4):
    return _cols_from_shards(g4) if n in _COL_SHARDED else g4.reshape(-1, g4.shape[2])


def _shard_major(n, g):
    if g.ndim == 3:
        return g
    return _cols_to_shards(g) if n in _COL_SHARDED else g.reshape(N_CHIPS, g.shape[0] // N_CHIPS, g.shape[1])


def _early_weights(w):
    shards = [w[n][0].astype(BF16) for n in _EARLY_W]
    shards.append(jnp.pad(w["conv_w"][0], ((0, 5), (0, 0))))
    shards.append(jnp.pad(w["ffn_conv_w"][0], ((0, 5), (0, 0))))
    got = _gather_chips(shards)
    full = {n: _full_from_slots(n, g4) for n, g4 in zip(_EARLY_W, got)}
    full["conv_w8"] = _cols_from_shards(got[len(_EARLY_W)])
    full["ffn_conv_w8"] = _cols_from_shards(got[len(_EARLY_W) + 1])
    return _layout_early(full, w)


def _layout_early(full, w):
    out = {n: w[n] for n in ("norm_mix_g", "q_norm_g", "kv_norm_g", "norm_ffn_g", "ple_norm_g")}
    out["final_norm_g"] = w["final_norm_g"][None, :]
    w_in = full["w_in"]
    zc = jnp.zeros((D_MODEL, QK_NOPE), BF16)
    zt = jnp.zeros((D_MODEL, HEAD_PAD - QK_NOPE - QK_ROPE), BF16)
    out["w_in_p"] = jnp.concatenate([w_in[:, :D_IN - QK_ROPE], zc, w_in[:, D_IN - QK_ROPE:], zt], axis=1)
    out["w_uq_p"] = _pad_heads(full["w_uq"], QK_NOPE + QK_ROPE)
    kv3 = full["w_ukv"].reshape(KV_LORA, N_HEADS, QK_NOPE + V_HEAD)
    out["w_kv_p"] = jnp.concatenate([_pad_heads(kv3[:, :, :QK_NOPE].reshape(KV_LORA, -1), QK_NOPE),
                                     _pad_heads(kv3[:, :, QK_NOPE:].reshape(KV_LORA, -1), V_HEAD)], axis=1)
    out["conv_w8"] = full["conv_w8"]
    fw = full["ffn_conv_w8"]
    out["ffn_conv_w"] = jnp.stack([fw[:, :D_FF], fw[:, D_FF:]])
    out["ffn_conv_b"] = w["ffn_conv_b"].reshape(2, 1, D_FF)
    return out


def _layout_late(full):
    w_o = full["w_o"]
    out = {"w_o_a": w_o[:CONV_WIDTH]}
    out["w_o_b"] = jnp.pad(w_o[CONV_WIDTH:].reshape(N_HEADS, V_HEAD, D_MODEL),
                           ((0, 0), (0, HEAD_PAD - V_HEAD), (0, 0))).reshape(D_ATT, D_MODEL)
    for n in ("w_up", "w_down", "w_ple_gate", "w_ple_proj"):
        out[n] = full[n]
    return out


def _true_matrices(g):
    out = {}
    wp = g["w_in_p"]
    out["w_in"] = jnp.concatenate([wp[:, :D_IN - QK_ROPE], wp[:, D_IN_PAD - HEAD_PAD + QK_NOPE:
                                                              D_IN_PAD - HEAD_PAD + QK_NOPE + QK_ROPE]], axis=1)
    out["w_uq"] = _unpad_heads(g["w_uq_p"], QK_NOPE + QK_ROPE).reshape(Q_LORA, -1)
    out["w_ukv"] = jnp.concatenate([_unpad_heads(g["w_k_p"], QK_NOPE), _unpad_heads(g["w_v_p"], V_HEAD)],
                                   axis=2).reshape(KV_LORA, -1)
    return out


def _true_vectors(g):
    out = {}
    out["conv_w"] = g["conv_w"]
    fw = g["ffn_conv_w"]
    out["ffn_conv_w"] = jnp.concatenate([fw[0, :3], fw[1, :3]], axis=1)
    out["ffn_conv_b"] = g["ffn_conv_b"].reshape(1, 2 * D_FF)
    for n in ("norm_mix_g", "q_norm_g", "kv_norm_g", "norm_ffn_g", "ple_norm_g", "final_norm_g"):
        out[n] = g[n]
    return out


def _chip_partials(names, g, core, *, tag):
    g4 = [_shard_major(n, g[n]) for n in names]
    sib = _send_other_halves(g4, tag=tag)
    return [_add_own_half(a, b, core, name="add_cores_" + n) for n, a, b in zip(names, g4, sib)]


_SMALL_SIZES = {"norm_mix_g": D_MODEL, "conv_w": 3 * CONV_WIDTH, "q_norm_g": Q_LORA, "kv_norm_g": KV_LORA,
                "norm_ffn_g": D_MODEL, "ffn_conv_w": 6 * D_FF, "ffn_conv_b": 2 * D_FF, "ple_norm_g": D_MODEL,
                "final_norm_g": D_MODEL}


def _pack(parts, rows):
    flat = jnp.concatenate([a.reshape(-1) for a in parts])
    return jnp.pad(flat, (0, rows * 128 - flat.shape[0])).reshape(rows, 128)


def _unpack(buf, sizes):
    flat = buf.reshape(-1)
    out, at = [], 0
    for n in sizes:
        out.append(flat[at:at + n])
        at += n
    return out


def _reduce_small(g, loss):
    sizes = [1] + [_SMALL_SIZES[n] for n in _SMALL]
    rows = -(-sum(sizes) // 1024) * 8
    slots = _gather_all(_pack([loss] + [g[n] for n in _SMALL], rows))
    parts = _unpack(_sum_slots(slots, name="sum_small"), sizes)
    return parts[0][0], dict(zip(_SMALL, parts[1:]))


def kernel(x, p, norm_mix_g, w_in, conv_w, q_norm_g, w_uq, kv_norm_g, w_ukv, w_o, norm_ffn_g, w_up, ffn_conv_w, ffn_conv_b, w_down, ple_norm_g, w_ple_gate, w_ple_proj, final_norm_g, loss_target, m_norm_mix_g, m_w_in, m_conv_w, m_q_norm_g, m_w_uq, m_kv_norm_g, m_w_ukv, m_w_o, m_norm_ffn_g, m_w_up, m_ffn_conv_w, m_ffn_conv_b, m_w_down, m_ple_norm_g, m_w_ple_gate, m_w_ple_proj, m_final_norm_g, v_norm_mix_g, v_w_in, v_conv_w, v_q_norm_g, v_w_uq, v_kv_norm_g, v_w_ukv, v_w_o, v_norm_ffn_g, v_w_up, v_ffn_conv_w, v_ffn_conv_b, v_w_down, v_ple_norm_g, v_w_ple_gate, v_w_ple_proj, v_final_norm_g):
    names = ["norm_mix_g", "w_in", "conv_w", "q_norm_g", "w_uq", "kv_norm_g", "w_ukv", "w_o", "norm_ffn_g", "w_up",
             "ffn_conv_w", "ffn_conv_b", "w_down", "ple_norm_g", "w_ple_gate", "w_ple_proj", "final_norm_g"]
    w = dict(zip(names, (norm_mix_g, w_in, conv_w, q_norm_g, w_uq, kv_norm_g, w_ukv, w_o, norm_ffn_g, w_up,
                         ffn_conv_w, ffn_conv_b, w_down, ple_norm_g, w_ple_gate, w_ple_proj, final_norm_g)))
    m = dict(zip(names, (m_norm_mix_g, m_w_in, m_conv_w, m_q_norm_g, m_w_uq, m_kv_norm_g, m_w_ukv, m_w_o,
                         m_norm_ffn_g, m_w_up, m_ffn_conv_w, m_ffn_conv_b, m_w_down, m_ple_norm_g, m_w_ple_gate,
                         m_w_ple_proj, m_final_norm_g)))
    v = dict(zip(names, (v_norm_mix_g, v_w_in, v_conv_w, v_q_norm_g, v_w_uq, v_kv_norm_g, v_w_ukv, v_w_o,
                         v_norm_ffn_g, v_w_up, v_ffn_conv_w, v_ffn_conv_b, v_w_down, v_ple_norm_g, v_w_ple_gate,
                         v_w_ple_proj, v_final_norm_g)))

    core = lax.axis_index("c").astype(jnp.int32).reshape(1)
    chip = (2 * lax.axis_index("x") + lax.axis_index("y")).astype(jnp.int32).reshape(1)

    wts = _early_weights(w)
    gather = _chips_start([w[n][0].astype(BF16) for n in _LATE_W], scatter=False, name="gather_late_start")
    wts["norm_mix_g"] = wts["norm_mix_g"] + gather[4][0, 0]

    def late_weights(after):
        shards, landed = _chips_wait(gather, after, scatter=False, name="gather_late_wait")
        full = {n: _full_from_slots(n, lax.dynamic_update_slice(g4, own[None], (chip[0], 0, 0)))
                for n, own, g4 in zip(_LATE_W, shards, landed)}
        return _layout_late(full)

    def reduce_early(g):
        parts = _chip_partials(_LATE_W, g, core, tag="early")
        scatter = _chips_start([t16 for _, t16 in parts], scatter=True, name="scatter_early_start")

        def finish(after):
            _, landed = _chips_wait(scatter, after, scatter=True, name="scatter_early_wait")
            return [_sum_chips(a, t32, chip, name="sum_chips_" + n) for n, a, (t32, _) in zip(_LATE_W, landed, parts)]

        return scatter[4], finish

    def reduce_last(g):
        parts = _chip_partials(_EARLY_W, _true_matrices(g), core, tag="late")
        scatter = _chips_start([t16 for _, t16 in parts], scatter=True, name="scatter_late_start")

        def finish(after):
            _, landed = _chips_wait(scatter, after, scatter=True, name="scatter_late_wait")
            return [_sum_chips(a, t32, chip, name="sum_chips_" + n) for n, a, (t32, _) in zip(_EARLY_W, landed, parts)]

        return scatter[4], finish

    loss, grad_x, small_grads, halves_early, halves_last = _local_step(
        x[0], p[0, 0], loss_target[0], wts, late_weights, reduce_early, reduce_last)
    g_full = _true_vectors(small_grads)
    whole = _join_halves(halves_last + halves_early)
    big = {n: a.reshape(-1, a.shape[2]) for n, a in zip(_BIG, whole)}

    g_out, d_out, m_out, v_out = {}, {}, {}, {}
    for n in _BIG:
        shape = w[n].shape
        g = big[n]
        d, mn, vn = _adamw(w[n][0], g, m[n][0], v[n][0], name="adamw_" + n)
        g_out[n], d_out[n], m_out[n], v_out[n] = (a.reshape(shape) for a in (g, d, mn, vn))

    loss, small = _reduce_small(g_full, loss)
    chip = 2 * lax.axis_index("x") + lax.axis_index("y")
    g_small = {}
    for n in _SMALL:
        shape = w[n].shape
        g = small[n]
        if n in ("conv_w", "ffn_conv_w"):
            width = shape[-1]
            g = lax.dynamic_slice(g.reshape(3, N_CHIPS * width), (0, chip * width), (3, width))
        g_small[n] = g.reshape(shape)
    sizes = [g_small[n].size for n in _SMALL]
    rows = -(-sum(sizes) // 1024) * 8
    packed = [_pack([src[n] for n in _SMALL], rows) for src in (w, g_small, m, v)]
    d_s, m_s, v_s = _adamw(*packed, name="adamw_small")
    for n, d, mn, vn in zip(_SMALL, _unpack(d_s, sizes), _unpack(m_s, sizes), _unpack(v_s, sizes)):
        shape = w[n].shape
        g_out[n], d_out[n], m_out[n], v_out[n] = g_small[n], d.reshape(shape), mn.reshape(shape), vn.reshape(shape)

    return (loss, grad_x[None], *[g_out[n] for n in names], *[d_out[n] for n in names],
            *[m_out[n] for n in names], *[v_out[n] for n in names])
```

```python
import functools

import jax
import jax.numpy as jnp
from jax import lax
from jax.experimental import pallas as pl
from jax.experimental.pallas import tpu as pltpu

F32 = jnp.float32
BF16 = jnp.bfloat16

D_MODEL = 1024
CONV_WIDTH = 512
Q_LORA = 256
KV_LORA = 128
QK_NOPE = 64
QK_ROPE = 32
V_HEAD = 64
N_HEADS = 8
HEAD_PAD = 128
D_ATT = N_HEADS * HEAD_PAD
D_IN = 3 * CONV_WIDTH + Q_LORA + KV_LORA + QK_ROPE
D_IN_PAD = 3 * CONV_WIDTH + Q_LORA + KV_LORA + HEAD_PAD
D_FF = 2816
ROPE_THETA = 10000.0
EPS = 1e-6
SM_SCALE = (QK_NOPE + QK_ROPE) ** -0.5
ONES_LANE = V_HEAD

ADAM_LR = 0.001
ADAM_B1 = 0.9
ADAM_B2 = 0.999
ADAM_EPS = 1e-08
ADAM_WD = 0.01
ADAM_STEP = 10

N_CHIPS = 4
N_DEV = 8
MESH = pl.DeviceIdType.MESH
ANY = pl.BlockSpec(memory_space=pl.ANY)


def _params(sem):
    return pltpu.CompilerParams(dimension_semantics=sem)


def _mm(a, b, *, name, ta=False, tb=False, add=None, out_dtype=F32, tm=512, tn=512, tk=512,
        a_split=False, b_split=False, o_split=False, o_shards=False):
    if a_split:
        _, m, kh = a.shape
        k = 2 * kh
    elif ta:
        k, m = a.shape
    else:
        m, k = a.shape
    if b_split:
        _, kb, nh = b.shape
        n = 2 * nh
    elif tb:
        n, kb = b.shape
    else:
        kb, n = b.shape
    assert kb == k, (name, a.shape, b.shape)
    tm, tn, tk = min(tm, m), min(tn, n), min(tk, k)
    assert m % tm == 0 and n % tn == 0 and k % tk == 0, (name, m, n, k, tm, tn, tk)
    gm, gn, gk = m // tm, n // tn, k // tk

    if a_split:
        assert gk % 2 == 0
        a_spec = pl.BlockSpec((None, tm, tk), lambda i, j, kk: (kk // (gk // 2), i, kk % (gk // 2)))
    elif ta:
        a_spec = pl.BlockSpec((tk, tm), lambda i, j, kk: (kk, i))
    else:
        a_spec = pl.BlockSpec((tm, tk), lambda i, j, kk: (i, kk))
    if b_split:
        assert gn % 2 == 0
        b_spec = pl.BlockSpec((None, tk, tn), lambda i, j, kk: (j // (gn // 2), kk, j % (gn // 2)))
    elif tb:
        b_spec = pl.BlockSpec((tn, tk), lambda i, j, kk: (j, kk))
    else:
        b_spec = pl.BlockSpec((tk, tn), lambda i, j, kk: (kk, j))
    if o_shards:
        o_spec = pl.BlockSpec((None, tm, tn), lambda i, j, kk: (j, i, 0))
        o_shape = jax.ShapeDtypeStruct((gn, m, tn), out_dtype)
    elif o_split:
        assert gn % 2 == 0
        o_spec = pl.BlockSpec((None, tm, tn), lambda i, j, kk: (j // (gn // 2), i, j % (gn // 2)))
        o_shape = jax.ShapeDtypeStruct((2, m, n // 2), out_dtype)
    else:
        o_spec = pl.BlockSpec((tm, tn), lambda i, j, kk: (i, j))
        o_shape = jax.ShapeDtypeStruct((m, n), out_dtype)
    dims = (((0 if ta else 1,), (1 if tb else 0,)), ((), ()))

    def body(*refs):
        a_ref, b_ref = refs[:2]
        add_ref = None if add is None else refs[2]
        o_ref = refs[2 if add is None else 3]

        def finish(r):
            if add_ref is not None:
                r = r + add_ref[...]
            o_ref[...] = r.astype(o_ref.dtype)

        part = lax.dot_general(a_ref[...].astype(BF16), b_ref[...].astype(BF16), dims, preferred_element_type=F32)
        if gk == 1:
            finish(part)
            return
        acc_ref = refs[-1]
        kk = pl.program_id(2)

        @pl.when(kk == 0)
        def _():
            acc_ref[...] = part

        @pl.when((kk > 0) & (kk < gk - 1))
        def _():
            acc_ref[...] += part

        @pl.when(kk == gk - 1)
        def _():
            finish(acc_ref[...] + part)

    in_specs = [a_spec, b_spec]
    args = [a, b]
    if add is not None:
        in_specs.append(pl.BlockSpec((tm, tn), lambda i, j, kk: (i, j)))
        args.append(add)
    return pl.pallas_call(
        body, name=name, grid=(gm, gn, gk), in_specs=in_specs, out_specs=o_spec, out_shape=o_shape,
        scratch_shapes=[] if gk == 1 else [pltpu.VMEM((tm, tn), F32)],
        compiler_params=_params(("parallel", "parallel", "arbitrary")),
    )(*args)


def _rms_scale(v):
    return lax.rsqrt(jnp.mean(v * v, axis=-1, keepdims=True) + EPS)


def _rms_bwd_rows(v, g, dy):
    r = _rms_scale(v)
    vh = v * r
    dyg = dy * g
    dv = r * (dyg - vh * jnp.mean(dyg * vh, axis=-1, keepdims=True))
    return dv, dy * vh


def _shift_down(v, first_row):
    row = lax.broadcasted_iota(jnp.int32, v.shape, 0)
    return jnp.where(row == 0, first_row, pltpu.roll(v, 1, 0))


def _shift_up(v, last_row):
    n = v.shape[0]
    row = lax.broadcasted_iota(jnp.int32, v.shape, 0)
    return jnp.where(row == n - 1, last_row, pltpu.roll(v, n - 1, 0))


def _rope(t, cos, sin_a, sin_b):
    return t * cos + pltpu.roll(t, HEAD_PAD - 16, 1) * sin_a + pltpu.roll(t, 16, 1) * sin_b


def _rope_bwd(d, cos, sin_a, sin_b):
    return d * cos + pltpu.roll(d * sin_a, 16, 1) + pltpu.roll(d * sin_b, HEAD_PAD - 16, 1)


def _sigmoid(v):
    return 1.0 / (1.0 + jnp.exp(-v))


def _halo_specs(ts, s, width, col):
    nb = ts // 8
    last = s // 8 - 1
    prev = pl.BlockSpec((8, width), lambda i: (jnp.maximum(i * nb - 1, 0), col))
    nxt = pl.BlockSpec((8, width), lambda i: (jnp.minimum((i + 1) * nb, last), col))
    return prev, nxt


def _mm_fused(a, b, *, name, epilogue, row_outs, rows=(), vecs=(), n_vec_out=0, tb=False, a_split=False,
              prologue=None, second=None, transposed_out=None, tm=512, tk=None):
    if a_split:
        _, m, kh = a.shape
        k = 2 * kh
    else:
        m, k = a.shape
    n = b.shape[0] if tb else b.shape[1]
    assert (b.shape[1] if tb else b.shape[0]) == k, (name, a.shape, b.shape)
    tk = k if tk is None else tk
    assert m % tm == 0 and k % tk == 0, (name, m, k, tm, tk)
    gm, gk = m // tm, k // tk
    assert prologue is None or gk == 1
    n_a = 2 if a_split and gk == 1 else 1
    nr, nv = len(rows), len(vecs)
    n_pro = 0 if prologue is None else 1
    n_sec = 0 if second is None else 2
    n_t = 0 if transposed_out is None else 1
    dims = (((1,), (1 if tb else 0,)), ((), ()))

    def body(*refs):
        a_ref, b_ref = refs[0], refs[n_a]
        refs = refs[n_a - 1:]
        sec_refs = refs[2:2 + n_sec]
        row_refs, vec_refs = refs[2 + n_sec:2 + n_sec + nr], refs[2 + n_sec + nr:2 + n_sec + nr + nv]
        outs = refs[2 + n_sec + nr + nv:]
        row_out_refs = outs[n_pro:n_pro + len(row_outs)]
        t_out_refs = outs[n_pro + len(row_outs):n_pro + len(row_outs) + n_t]
        vec_out_refs = outs[n_pro + len(row_outs) + n_t:n_pro + len(row_outs) + n_t + n_vec_out]
        i, kk = pl.program_id(0), pl.program_id(1)
        vec_vals = [v[...] for v in vec_refs]
        if prologue is None:
            lhs = a_ref[...].astype(BF16)
        else:
            lhs = prologue(a_ref[...], vec_vals)
            outs[0][...] = lhs

        def finish(r):
            if second is not None:
                r = r + jnp.dot(sec_refs[0][...].astype(BF16), sec_refs[1][...].astype(BF16),
                                preferred_element_type=F32)
            row_vals, vec_parts = epilogue(r, [x[...] for x in row_refs], vec_vals)
            for ref, val in zip(row_out_refs, row_vals):
                ref[...] = val.astype(ref.dtype)
            for ref in t_out_refs:
                ref[...] = row_vals[0].T.astype(ref.dtype)
            if n_vec_out:
                @pl.when(i == 0)
                def _():
                    for ref in vec_out_refs:
                        ref[...] = jnp.zeros_like(ref)

                for ref, val in zip(vec_out_refs, vec_parts):
                    ref[...] += val

        if n_a == 2:
            kh = k // 2
            halves = (b_ref[:, :kh], b_ref[:, kh:]) if tb else (b_ref[:kh, :], b_ref[kh:, :])
            part = (lax.dot_general(lhs, halves[0].astype(BF16), dims, preferred_element_type=F32)
                    + lax.dot_general(refs[0][...].astype(BF16), halves[1].astype(BF16), dims,
                                      preferred_element_type=F32))
        else:
            part = lax.dot_general(lhs, b_ref[...].astype(BF16), dims, preferred_element_type=F32)
        if gk == 1:
            finish(part)
            return
        acc_ref = refs[-1]

        @pl.when(kk == 0)
        def _():
            acc_ref[...] = part

        @pl.when((kk > 0) & (kk < gk - 1))
        def _():
            acc_ref[...] += part

        @pl.when(kk == gk - 1)
        def _():
            finish(acc_ref[...] + part)

    if n_a == 2:
        a_specs = [pl.BlockSpec((None, tm, k // 2), lambda i, kk: (0, i, 0)),
                   pl.BlockSpec((None, tm, k // 2), lambda i, kk: (1, i, 0))]
    elif a_split:
        assert gk % 2 == 0
        a_specs = [pl.BlockSpec((None, tm, tk), lambda i, kk: (kk // (gk // 2), i, kk % (gk // 2)))]
    else:
        a_specs = [pl.BlockSpec((tm, tk), lambda i, kk: (i, kk))]
    b_spec = pl.BlockSpec((n, tk), lambda i, kk: (0, kk)) if tb else pl.BlockSpec((tk, n), lambda i, kk: (kk, 0))
    row_spec = pl.BlockSpec((tm, n), lambda i, kk: (i, 0))
    out_specs, out_shape = [], []
    if prologue is not None:
        out_specs.append(pl.BlockSpec((tm, k), lambda i, kk: (i, 0)))
        out_shape.append(jax.ShapeDtypeStruct((m, k), BF16))
    out_specs += [row_spec] * len(row_outs)
    out_shape += [jax.ShapeDtypeStruct((m, n), dt) for dt in row_outs]
    if transposed_out is not None:
        out_specs.append(pl.BlockSpec((n, tm), lambda i, kk: (0, i)))
        out_shape.append(jax.ShapeDtypeStruct((n, m), transposed_out))
    out_specs += [pl.BlockSpec((1, n), lambda i, kk: (0, 0))] * n_vec_out
    out_shape += [jax.ShapeDtypeStruct((1, n), F32)] * n_vec_out
    sec_specs, sec_args = [], []
    if second is not None:
        k2 = second[0].shape[1]
        sec_specs = [pl.BlockSpec((tm, k2), lambda i, kk: (i, 0)), pl.BlockSpec((k2, n), lambda i, kk: (0, 0))]
        sec_args = list(second)
    res = pl.pallas_call(
        body, name=name, grid=(gm, gk),
        in_specs=a_specs + [b_spec] + sec_specs + [row_spec] * nr
        + [pl.BlockSpec((1, v.shape[1]), lambda i, kk: (0, 0)) for v in vecs],
        out_specs=out_specs, out_shape=out_shape,
        scratch_shapes=[] if gk == 1 else [pltpu.VMEM((tm, n), F32)],
        compiler_params=_params(("arbitrary" if n_vec_out else "parallel", "arbitrary")),
    )(*([a] * n_a), b, *sec_args, *rows, *vecs)
    split = n_pro + len(row_outs) + n_t
    return list(res[:split]), list(res[split:])


def _pro_rms(a, vecs):
    return (a * _rms_scale(a) * vecs[0]).astype(BF16)


def _epi_plain(r, rows, vecs):
    return [r], []


def _epi_add_rms(r, rows, vecs):
    xn = r + rows[0]
    return [xn, xn * _rms_scale(xn) * vecs[0]], []


def _epi_rms_bwd(r, rows, vecs):
    dv, dg_rows = _rms_bwd_rows(rows[0], vecs[0], r)
    return [dv + rows[1]], [jnp.sum(dg_rows, axis=0, keepdims=True)]


def _mix_pre(z, conv_w8, gq, gkv, cos, sin_a, sin_b, *, ts=256):
    s = z.shape[0]
    n = s // ts
    cw = CONV_WIDTH

    def body(z_ref, xcp, xcn, cgp, cgn, w_ref, gq_ref, gkv_ref, cos_ref, sa_ref, sb_ref,
             yc_ref, qn_ref, kvn_ref, kr_ref):
        i = pl.program_id(0)
        xc = z_ref[:, 0:cw]
        bg = z_ref[:, cw:2 * cw]
        cg = z_ref[:, 2 * cw:3 * cw]
        m = cg * xc
        m_prev = jnp.where(i > 0, xcp[7:8, :] * cgp[7:8, :], 0.0)
        m_next = jnp.where(i < n - 1, xcn[0:1, :] * cgn[0:1, :], 0.0)
        cm = _shift_down(m, m_prev) * w_ref[0:1, :] + m * w_ref[1:2, :] + _shift_up(m, m_next) * w_ref[2:3, :]
        yc_ref[...] = (bg * cm).astype(BF16)
        ql = z_ref[:, 3 * cw:3 * cw + Q_LORA]
        qn_ref[...] = (ql * _rms_scale(ql) * gq_ref[...]).astype(BF16)
        kvl = z_ref[:, 3 * cw + Q_LORA:3 * cw + Q_LORA + KV_LORA]
        kvn_ref[...] = (kvl * _rms_scale(kvl) * gkv_ref[...]).astype(BF16)
        kr_ref[...] = _rope(z_ref[:, D_IN_PAD - HEAD_PAD:D_IN_PAD], cos_ref[...], sa_ref[...], sb_ref[...])

    xcp, xcn = _halo_specs(ts, s, cw, 0)
    cgp, cgn = _halo_specs(ts, s, cw, 2)
    tab = pl.BlockSpec((ts, HEAD_PAD), lambda i: (i, 0))
    return pl.pallas_call(
        body, name="mix_pre", grid=(n,),
        in_specs=[pl.BlockSpec((ts, D_IN_PAD), lambda i: (i, 0)), xcp, xcn, cgp, cgn,
                  pl.BlockSpec((8, cw), lambda i: (0, 0)), pl.BlockSpec((1, Q_LORA), lambda i: (0, 0)),
                  pl.BlockSpec((1, KV_LORA), lambda i: (0, 0)), tab, tab, tab],
        out_specs=[pl.BlockSpec((ts, cw), lambda i: (i, 0)), pl.BlockSpec((ts, Q_LORA), lambda i: (i, 0)),
                   pl.BlockSpec((ts, KV_LORA), lambda i: (i, 0)), tab],
        out_shape=[jax.ShapeDtypeStruct((s, cw), BF16), jax.ShapeDtypeStruct((s, Q_LORA), BF16),
                   jax.ShapeDtypeStruct((s, KV_LORA), BF16), jax.ShapeDtypeStruct((s, HEAD_PAD), F32)],
        compiler_params=_params(("parallel",)),
    )(z, z, z, z, z, conv_w8, gq, gkv, cos, sin_a, sin_b)


def _mix_bwd(z, dyc, dqn, dkvn, dkr, conv_w8, gq, gkv, cos, sin_a, sin_b, *, ts=256):
    s = z.shape[0]
    n = s // ts
    cw = CONV_WIDTH

    def body(z_ref, xcp, xcn, bgp, bgn, cgp, cgn, dyc_ref, dycp, dycn, dqn_ref, dkvn_ref, dkr_ref,
             w_ref, gq_ref, gkv_ref, cos_ref, sa_ref, sb_ref,
             dz_ref, dw0_ref, dw1_ref, dw2_ref, dgq_ref, dgkv_ref):
        i = pl.program_id(0)

        @pl.when(i == 0)
        def _():
            for r in (dw0_ref, dw1_ref, dw2_ref, dgq_ref, dgkv_ref):
                r[...] = jnp.zeros_like(r)

        xc = z_ref[:, 0:cw]
        bg = z_ref[:, cw:2 * cw]
        cg = z_ref[:, 2 * cw:3 * cw]
        w0, w1, w2 = w_ref[0:1, :], w_ref[1:2, :], w_ref[2:3, :]
        m = cg * xc
        m_dn = _shift_down(m, jnp.where(i > 0, xcp[7:8, :] * cgp[7:8, :], 0.0))
        m_up = _shift_up(m, jnp.where(i < n - 1, xcn[0:1, :] * cgn[0:1, :], 0.0))
        cm = m_dn * w0 + m * w1 + m_up * w2
        dyc_v = dyc_ref[...]
        dcm = dyc_v * bg
        dcm_dn = _shift_down(dcm, jnp.where(i > 0, dycp[7:8, :] * bgp[7:8, :], 0.0))
        dcm_up = _shift_up(dcm, jnp.where(i < n - 1, dycn[0:1, :] * bgn[0:1, :], 0.0))
        dm = dcm_up * w0 + dcm * w1 + dcm_dn * w2
        dz_ref[:, 0:cw] = (dm * cg).astype(BF16)
        dz_ref[:, cw:2 * cw] = (dyc_v * cm).astype(BF16)
        dz_ref[:, 2 * cw:3 * cw] = (dm * xc).astype(BF16)
        dw0_ref[...] += jnp.sum(dcm * m_dn, axis=0, keepdims=True)
        dw1_ref[...] += jnp.sum(dcm * m, axis=0, keepdims=True)
        dw2_ref[...] += jnp.sum(dcm * m_up, axis=0, keepdims=True)

        dql, dgq_rows = _rms_bwd_rows(z_ref[:, 3 * cw:3 * cw + Q_LORA], gq_ref[...], dqn_ref[...])
        dz_ref[:, 3 * cw:3 * cw + Q_LORA] = dql.astype(BF16)
        dgq_ref[...] += jnp.sum(dgq_rows, axis=0, keepdims=True)
        dkvl, dgkv_rows = _rms_bwd_rows(z_ref[:, 3 * cw + Q_LORA:3 * cw + Q_LORA + KV_LORA], gkv_ref[...],
                                        dkvn_ref[...])
        dz_ref[:, 3 * cw + Q_LORA:3 * cw + Q_LORA + KV_LORA] = dkvl.astype(BF16)
        dgkv_ref[...] += jnp.sum(dgkv_rows, axis=0, keepdims=True)

        lane = lax.broadcasted_iota(jnp.int32, (ts, HEAD_PAD), 1)
        rope_lane = (lane >= QK_NOPE) & (lane < QK_NOPE + QK_ROPE)
        dk = _rope_bwd(dkr_ref[...], cos_ref[...], sa_ref[...], sb_ref[...])
        dz_ref[:, D_IN_PAD - HEAD_PAD:D_IN_PAD] = jnp.where(rope_lane, dk, 0.0).astype(BF16)

    xcp, xcn = _halo_specs(ts, s, cw, 0)
    bgp, bgn = _halo_specs(ts, s, cw, 1)
    cgp, cgn = _halo_specs(ts, s, cw, 2)
    dycp, dycn = _halo_specs(ts, s, cw, 0)
    tab = pl.BlockSpec((ts, HEAD_PAD), lambda i: (i, 0))

    def vec(width):
        return pl.BlockSpec((1, width), lambda i: (0, 0))

    outs = pl.pallas_call(
        body, name="mix_bwd", grid=(n,),
        in_specs=[pl.BlockSpec((ts, D_IN_PAD), lambda i: (i, 0)), xcp, xcn, bgp, bgn, cgp, cgn,
                  pl.BlockSpec((ts, cw), lambda i: (i, 0)), dycp, dycn,
                  pl.BlockSpec((ts, Q_LORA), lambda i: (i, 0)), pl.BlockSpec((ts, KV_LORA), lambda i: (i, 0)), tab,
                  pl.BlockSpec((8, cw), lambda i: (0, 0)), vec(Q_LORA), vec(KV_LORA), tab, tab, tab],
        out_specs=[pl.BlockSpec((ts, D_IN_PAD), lambda i: (i, 0)), vec(cw), vec(cw), vec(cw), vec(Q_LORA),
                   vec(KV_LORA)],
        out_shape=[jax.ShapeDtypeStruct((s, D_IN_PAD), BF16)] + [jax.ShapeDtypeStruct((1, cw), F32)] * 3
        + [jax.ShapeDtypeStruct((1, Q_LORA), F32), jax.ShapeDtypeStruct((1, KV_LORA), F32)],
        compiler_params=_params(("arbitrary",)),
    )(z, z, z, z, z, z, z, dyc, dyc, dyc, dqn, dkvn, dkr, conv_w8, gq, gkv, cos, sin_a, sin_b)
    dz, dw0, dw1, dw2, dgq, dgkv = outs
    return dz, jnp.concatenate([dw0, dw1, dw2], axis=0), dgq, dgkv


def _qkv_proj(qn, kvn, kr, w_uq_p, w_kv_p, cos, sin_a, sin_b, *, ts=512):
    s = qn.shape[0]

    def body(qn_ref, kvn_ref, kr_ref, wq_ref, wkv_ref, cos_ref, sa_ref, sb_ref, q_ref, k_ref, v_ref, qt_ref):
        cos_v, sa, sb = cos_ref[...], sa_ref[...], sb_ref[...]
        q = jnp.dot(qn_ref[...], wq_ref[...], preferred_element_type=F32)
        kv = jnp.dot(kvn_ref[...], wkv_ref[...], preferred_element_type=F32)
        kr_v = kr_ref[...]
        lane = lax.broadcasted_iota(jnp.int32, (1, HEAD_PAD), 1)
        ones_lane = (lane == ONES_LANE).astype(F32)
        for h in range(N_HEADS):
            blk = slice(h * HEAD_PAD, (h + 1) * HEAD_PAD)
            q_h = _rope(q[:, blk], cos_v, sa, sb) * SM_SCALE
            q_ref[:, blk] = q_h.astype(BF16)
            qt_ref[blk, :] = q_h.T.astype(BF16)
            k_ref[:, blk] = (kv[:, blk] + kr_v).astype(BF16)
            v_ref[:, blk] = (kv[:, D_ATT + h * HEAD_PAD:D_ATT + (h + 1) * HEAD_PAD] + ones_lane).astype(BF16)

    tab = pl.BlockSpec((ts, HEAD_PAD), lambda i: (i, 0))
    wide = pl.BlockSpec((ts, D_ATT), lambda i: (i, 0))
    return pl.pallas_call(
        body, name="qkv_proj", grid=(s // ts,),
        in_specs=[pl.BlockSpec((ts, Q_LORA), lambda i: (i, 0)), pl.BlockSpec((ts, KV_LORA), lambda i: (i, 0)), tab,
                  pl.BlockSpec((Q_LORA, D_ATT), lambda i: (0, 0)), pl.BlockSpec((KV_LORA, 2 * D_ATT), lambda i: (0, 0)),
                  tab, tab, tab],
        out_specs=[wide, wide, wide, pl.BlockSpec((D_ATT, ts), lambda i: (0, i))],
        out_shape=[jax.ShapeDtypeStruct((s, D_ATT), BF16)] * 3 + [jax.ShapeDtypeStruct((D_ATT, s), BF16)],
        compiler_params=_params(("parallel",)),
    )(qn, kvn, kr, w_uq_p, w_kv_p, cos, sin_a, sin_b)


def _qk_bwd(dq, dk, cos, sin_a, sin_b, *, ts=256):
    s = dq.shape[0]

    def body(dq_ref, dk_ref, cos_ref, sa_ref, sb_ref, dqp_ref, dkr_ref):
        cos_v, sa, sb = cos_ref[...], sa_ref[...], sb_ref[...]
        tot = jnp.zeros((ts, HEAD_PAD), F32)
        for h in range(N_HEADS):
            blk = slice(h * HEAD_PAD, (h + 1) * HEAD_PAD)
            dqp_ref[:, blk] = _rope_bwd(dq_ref[:, blk], cos_v, sa, sb).astype(BF16)
            tot = tot + dk_ref[:, blk]
        dkr_ref[...] = tot

    tab = pl.BlockSpec((ts, HEAD_PAD), lambda i: (i, 0))
    wide = pl.BlockSpec((ts, D_ATT), lambda i: (i, 0))
    return pl.pallas_call(
        body, name="qk_bwd", grid=(s // ts,),
        in_specs=[wide, wide, tab, tab, tab], out_specs=[wide, tab],
        out_shape=[jax.ShapeDtypeStruct((s, D_ATT), BF16), jax.ShapeDtypeStruct((s, HEAD_PAD), F32)],
        compiler_params=_params(("parallel",)),
    )(dq, dk, cos, sin_a, sin_b)


_NT = (((1,), (1,)), ((), ()))
_TN = (((0,), (0,)), ((), ()))


def _flash_fwd(q, k, v, *, tq=1024, tk=1024):
    s = q.shape[0]
    tq, tk = min(tq, s), min(tk, s)
    nk = s // tk

    def body(q_ref, k_ref, v_ref, o_ref, lse_ref):
        qv = q_ref[...]

        def step(j, carry):
            m, acc = carry
            rows = pl.ds(pl.multiple_of(j * tk, tk), tk)
            sc = lax.dot_general(qv, k_ref[rows, :], _NT, preferred_element_type=F32)
            m_new = jnp.maximum(m, jnp.max(sc, axis=1, keepdims=True))
            p = jnp.exp(sc - m_new).astype(BF16)
            acc = jnp.exp(m - m_new) * acc + jnp.dot(p, v_ref[rows, :], preferred_element_type=F32)
            return m_new, acc

        init = (jnp.full((tq, 1), -jnp.inf, F32), jnp.zeros((tq, HEAD_PAD), F32))
        m, acc = lax.fori_loop(0, nk, step, init)
        l = acc[:, ONES_LANE:ONES_LANE + 1]
        o_ref[...] = (acc / l).astype(BF16)
        lse_ref[...] = m + jnp.log(l)

    head = pl.BlockSpec((s, HEAD_PAD), lambda h, i: (0, h))
    return pl.pallas_call(
        body, name="flash_fwd", grid=(N_HEADS, s // tq),
        in_specs=[pl.BlockSpec((tq, HEAD_PAD), lambda h, i: (i, h)), head, head],
        out_specs=[pl.BlockSpec((tq, HEAD_PAD), lambda h, i: (i, h)),
                   pl.BlockSpec((None, tq, 1), lambda h, i: (h, i, 0))],
        out_shape=[jax.ShapeDtypeStruct((s, D_ATT), BF16), jax.ShapeDtypeStruct((N_HEADS, s, 1), F32)],
        compiler_params=_params(("parallel", "parallel")),
    )(q, k, v)


def _attn_delta(do, o, *, ts=512):
    s = do.shape[0]

    def body(do_ref, o_ref, dl_ref):
        for h in range(N_HEADS):
            blk = slice(h * HEAD_PAD, (h + 1) * HEAD_PAD)
            dl_ref[h] = jnp.sum(do_ref[:, blk].astype(F32) * o_ref[:, blk].astype(F32), axis=1, keepdims=True)

    wide = pl.BlockSpec((ts, D_ATT), lambda i: (i, 0))
    return pl.pallas_call(
        body, name="attn_delta", grid=(s // ts,), in_specs=[wide, wide],
        out_specs=pl.BlockSpec((N_HEADS, ts, 1), lambda i: (0, i, 0)),
        out_shape=jax.ShapeDtypeStruct((N_HEADS, s, 1), F32),
        compiler_params=_params(("parallel",)),
    )(do, o)


def _flash_bwd(q, qt, k, v, do, dot, lse, delta, after, *, tq=1024, tk=512):
    s = q.shape[0]
    tq, tk = min(tq, s), min(tk, s)
    nq = s // tq

    def body(q_ref, qt_ref, do_ref, dot_ref, lse_ref, dl_ref, k_ref, v_ref, after_ref, dq_ref, dk_ref, dv_ref):
        j = pl.program_id(1)

        @pl.when(j == 0)
        def _():
            dq_ref[...] = jnp.zeros_like(dq_ref)

        kv, vv = k_ref[...], v_ref[...]

        def step(i, carry):
            dk_t, dv_t = carry
            at = pl.multiple_of(i * tq, tq)
            rows = pl.ds(at, tq)
            sc = lax.dot_general(q_ref[rows, :], kv, _NT, preferred_element_type=F32)
            p = jnp.exp(sc - lse_ref[rows, :])
            dp = lax.dot_general(do_ref[rows, :], vv, _NT, preferred_element_type=F32)
            ds = (p * (dp - dl_ref[rows, :])).astype(BF16)
            dv_t = dv_t + jnp.dot(dot_ref[:, rows], p.astype(BF16), preferred_element_type=F32)
            dk_t = dk_t + jnp.dot(qt_ref[:, rows], ds, preferred_element_type=F32)
            dq_ref[rows, :] += jnp.dot(ds, kv, preferred_element_type=F32)
            return dk_t, dv_t

        zero = jnp.zeros((HEAD_PAD, tk), F32)
        dk_t, dv_t = lax.fori_loop(0, nq, step, (zero, zero))
        dk_ref[...] = dk_t.T
        dv_ref[...] = dv_t.T

        @pl.when(j == pl.num_programs(1) - 1)
        def _():
            dq_ref[...] *= SM_SCALE

    head = pl.BlockSpec((s, HEAD_PAD), lambda h, j: (0, h))
    head_t = pl.BlockSpec((HEAD_PAD, s), lambda h, j: (h, 0))
    stat = pl.BlockSpec((None, s, 1), lambda h, j: (h, 0, 0))
    blk = pl.BlockSpec((tk, HEAD_PAD), lambda h, j: (j, h))
    return pl.pallas_call(
        body, name="flash_bwd", grid=(N_HEADS, s // tk),
        in_specs=[head, head_t, head, head_t, stat, stat, blk, blk, ANY],
        out_specs=[head, blk, blk],
        out_shape=[jax.ShapeDtypeStruct((s, D_ATT), F32)] * 3,
        compiler_params=_params(("parallel", "arbitrary")),
    )(q, qt, do, dot, lse, delta, k, v, after)


FFN_TC = 256


FFN_HALO_BF16 = 16
FFN_HALO_F32 = 8


def _row_halo_specs(ts, s, halo, width):
    nb = ts // halo
    last = s // halo - 1
    prev = pl.BlockSpec((halo, width), lambda i, j: (jnp.maximum(i * nb - 1, 0), 0))
    nxt = pl.BlockSpec((halo, width), lambda i, j: (jnp.minimum((i + 1) * nb, last), 0))
    return prev, nxt


def _ext_rows(prev, main, nxt, first, last):
    return jnp.concatenate([jnp.where(first, jnp.zeros_like(prev), prev), main,
                            jnp.where(last, jnp.zeros_like(nxt), nxt)], axis=0)


def _ext_conv(a, w):
    a_dn = pltpu.roll(a, 1, 0)
    a_up = pltpu.roll(a, a.shape[0] - 1, 0)
    return a_dn * w[0:1, :] + a * w[1:2, :] + a_up * w[2:3, :], a_dn, a_up


def _ffn_fwd(hf, w_up, w, b, *, ts=512, tc=FFN_TC):
    s = hf.shape[0]
    n, nj, halo = s // ts, D_FF // tc, FFN_HALO_BF16

    def body(h_ref, hp_ref, hn_ref, wg_ref, wu_ref, cw_ref, cb_ref, a_ref, act_ref):
        i = pl.program_id(0)
        ext = _ext_rows(hp_ref[...], h_ref[...], hn_ref[...], i == 0, i == n - 1)
        gate_up = []
        for half, w_ref in enumerate((wg_ref, wu_ref)):
            a_ext = jnp.dot(ext, w_ref[...], preferred_element_type=F32)
            a_ref[half] = a_ext[halo:halo + ts]
            gate_up.append(_ext_conv(a_ext, cw_ref[half])[0][halo:halo + ts] + cb_ref[half])
        g, u = gate_up
        act_ref[...] = (g * _sigmoid(g) * u).astype(BF16)

    prev, nxt = _row_halo_specs(ts, s, halo, D_MODEL)
    return pl.pallas_call(
        body, name="ffn_fwd", grid=(n, nj),
        in_specs=[pl.BlockSpec((ts, D_MODEL), lambda i, j: (i, 0)), prev, nxt,
                  pl.BlockSpec((D_MODEL, tc), lambda i, j: (0, j)), pl.BlockSpec((D_MODEL, tc), lambda i, j: (0, j + nj)),
                  pl.BlockSpec((2, 8, tc), lambda i, j: (0, 0, j)), pl.BlockSpec((2, 1, tc), lambda i, j: (0, 0, j))],
        out_specs=[pl.BlockSpec((2, ts, tc), lambda i, j: (0, i, j)), pl.BlockSpec((ts, tc), lambda i, j: (i, j))],
        out_shape=[jax.ShapeDtypeStruct((2, s, D_FF), F32), jax.ShapeDtypeStruct((s, D_FF), BF16)],
        compiler_params=_params(("parallel", "parallel")),
    )(hf, hf, hf, w_up, w_up, w, b)


def _ffn_bwd(dx2, w_down, a_pre, w, b, *, ts=512, tc=FFN_TC):
    s = dx2.shape[0]
    n, nj, halo = s // ts, D_FF // tc, FFN_HALO_F32
    main = slice(halo, halo + ts)

    def body(dx_ref, dxp_ref, dxn_ref, wd_ref, a_ref, ap_ref, an_ref, cw_ref, cb_ref, o_ref, dw_ref, db_ref):
        i, j = pl.program_id(0), pl.program_id(1)
        first, last = i == 0, i == n - 1

        @pl.when(first & (j == 0))
        def _():
            dw_ref[...] = jnp.zeros_like(dw_ref)
            db_ref[...] = jnp.zeros_like(db_ref)

        dx_ext = _ext_rows(dxp_ref[...], dx_ref[...], dxn_ref[...], first, last).astype(BF16)
        dact = lax.dot_general(dx_ext, wd_ref[...], _NT, preferred_element_type=F32)
        halves = []
        for half in range(2):
            a_ext = _ext_rows(ap_ref[half], a_ref[half], an_ref[half], first, last)
            conv, a_dn, a_up = _ext_conv(a_ext, cw_ref[half])
            halves.append((conv + cb_ref[half], a_dn, a_ext, a_up))
        g, u = halves[0][0], halves[1][0]
        sg = _sigmoid(g)
        grads = (dact * u * (sg * (1.0 + g * (1.0 - sg))), dact * (g * sg))
        for half in range(2):
            d = grads[half]
            _, a_dn, a_ext, a_up = halves[half]
            wv = cw_ref[half]
            d_pre = pltpu.roll(d, d.shape[0] - 1, 0) * wv[0:1, :] + d * wv[1:2, :] + pltpu.roll(d, 1, 0) * wv[2:3, :]
            o_ref[half] = d_pre[main].astype(BF16)
            dm = d[main]
            dw_ref[j, half, 0:1, :] += jnp.sum(dm * a_dn[main], axis=0, keepdims=True)
            dw_ref[j, half, 1:2, :] += jnp.sum(dm * a_ext[main], axis=0, keepdims=True)
            dw_ref[j, half, 2:3, :] += jnp.sum(dm * a_up[main], axis=0, keepdims=True)
            db_ref[j, half] += jnp.sum(dm, axis=0, keepdims=True)

    dxp, dxn = _row_halo_specs(ts, s, halo, D_MODEL)
    nb, lastb = ts // halo, s // halo - 1
    a_main = pl.BlockSpec((2, ts, tc), lambda i, j: (0, i, j))
    a_prev = pl.BlockSpec((2, halo, tc), lambda i, j: (0, jnp.maximum(i * nb - 1, 0), j))
    a_next = pl.BlockSpec((2, halo, tc), lambda i, j: (0, jnp.minimum((i + 1) * nb, lastb), j))
    da_pre, dw, db = pl.pallas_call(
        body, name="ffn_bwd", grid=(n, nj),
        in_specs=[pl.BlockSpec((ts, D_MODEL), lambda i, j: (i, 0)), dxp, dxn,
                  pl.BlockSpec((tc, D_MODEL), lambda i, j: (j, 0)), a_main, a_prev, a_next,
                  pl.BlockSpec((2, 8, tc), lambda i, j: (0, 0, j)), pl.BlockSpec((2, 1, tc), lambda i, j: (0, 0, j))],
        out_specs=[a_main, pl.BlockSpec((nj, 2, 8, tc), lambda i, j: (0, 0, 0, 0)),
                   pl.BlockSpec((nj, 2, 1, tc), lambda i, j: (0, 0, 0, 0))],
        out_shape=[jax.ShapeDtypeStruct((2, s, D_FF), BF16), jax.ShapeDtypeStruct((nj, 2, 8, tc), F32),
                   jax.ShapeDtypeStruct((nj, 2, 1, tc), F32)],
        compiler_params=_params(("arbitrary", "arbitrary")),
    )(dx2, dx2, dx2, w_down, a_pre, a_pre, a_pre, w, b)
    return (da_pre, dw.transpose(1, 2, 0, 3).reshape(2, 8, D_FF), db.transpose(1, 2, 0, 3).reshape(2, 1, D_FF))


def _ple_final(x2, n3, p, target, gf, w_pg, w_pp, *, ts=256):
    s, d = x2.shape
    dp = p.shape[1]

    def body(x2_ref, n3_ref, p_ref, t_ref, gf_ref, wg_ref, wp_ref, loss_ref, dx3_ref, dgl_ref, dpp_ref, dgf_ref):
        @pl.when(pl.program_id(0) == 0)
        def _():
            loss_ref[...] = jnp.zeros_like(loss_ref)
            dgf_ref[...] = jnp.zeros_like(dgf_ref)

        gate = _sigmoid(jnp.dot(n3_ref[...], wg_ref[...], preferred_element_type=F32))
        ppv = jnp.dot(p_ref[...].astype(BF16), wp_ref[...], preferred_element_type=F32)
        x3 = x2_ref[...] + gate * ppv
        gfv = gf_ref[...]
        err = x3 * _rms_scale(x3) * gfv - t_ref[...]
        loss_ref[...] += 0.5 * jnp.sum(jnp.mean(err * err, axis=-1, keepdims=True), axis=0, keepdims=True)
        dx3, dgf_rows = _rms_bwd_rows(x3, gfv, err * (1.0 / d))
        dgf_ref[...] += jnp.sum(dgf_rows, axis=0, keepdims=True)
        dx3_ref[...] = dx3
        dgl_ref[...] = (dx3 * ppv * gate * (1.0 - gate)).astype(BF16)
        dpp_ref[...] = (dx3 * gate).astype(BF16)

    row = pl.BlockSpec((ts, d), lambda i: (i, 0))
    vec = pl.BlockSpec((1, d), lambda i: (0, 0))
    return pl.pallas_call(
        body, name="ple_final", grid=(s // ts,),
        in_specs=[row, row, pl.BlockSpec((ts, dp), lambda i: (i, 0)), row, vec,
                  pl.BlockSpec((d, d), lambda i: (0, 0)), pl.BlockSpec((dp, d), lambda i: (0, 0))],
        out_specs=[pl.BlockSpec((1, 128), lambda i: (0, 0)), row, row, row, vec],
        out_shape=[jax.ShapeDtypeStruct((1, 128), F32), jax.ShapeDtypeStruct((s, d), F32),
                   jax.ShapeDtypeStruct((s, d), BF16), jax.ShapeDtypeStruct((s, d), BF16),
                   jax.ShapeDtypeStruct((1, d), F32)],
        compiler_params=_params(("arbitrary",)),
    )(x2, n3, p, target, gf, w_pg, w_pp)


def _row_tile(rows, cols, n_arrays, budget=12 << 20):
    best = None
    for t in range(8, rows + 1, 8):
        if rows % t == 0 and t * cols * 4 * n_arrays <= budget:
            best = t
    return rows if best is None else best


def _sum_slots(a, *, name):
    g, r, c = a.shape
    tr = _row_tile(r, c, g + 1)

    def body(*refs):
        tot = refs[0][...]
        for ref in refs[1:g]:
            tot = tot + ref[...]
        refs[g][...] = tot

    specs = [pl.BlockSpec((None, tr, c), functools.partial(lambda i, slot: (slot, i, 0), slot=k)) for k in range(g)]
    return pl.pallas_call(
        body, name=name, grid=(r // tr,), in_specs=specs, out_specs=pl.BlockSpec((tr, c), lambda i: (i, 0)),
        out_shape=jax.ShapeDtypeStruct((r, c), a.dtype), compiler_params=_params(("parallel",)),
    )(*([a] * g))


def _adamw(w, g, m, v, *, name):
    r, c = w.shape
    tr = _row_tile(r, c, 7)

    def body(w_ref, g_ref, m_ref, v_ref, d_ref, mo_ref, vo_ref):
        gv = g_ref[...]
        mn = ADAM_B1 * m_ref[...] + (1.0 - ADAM_B1) * gv
        vn = ADAM_B2 * v_ref[...] + (1.0 - ADAM_B2) * (gv * gv)
        m_hat = mn / (1.0 - ADAM_B1 ** ADAM_STEP)
        v_hat = vn / (1.0 - ADAM_B2 ** ADAM_STEP)
        d_ref[...] = -ADAM_LR * (m_hat / (jnp.sqrt(v_hat) + ADAM_EPS) + ADAM_WD * w_ref[...])
        mo_ref[...] = mn
        vo_ref[...] = vn

    blk = pl.BlockSpec((tr, c), lambda i: (i, 0))
    return pl.pallas_call(
        body, name=name, grid=(r // tr,), in_specs=[blk] * 4, out_specs=[blk] * 3,
        out_shape=[jax.ShapeDtypeStruct((r, c), F32)] * 3, compiler_params=_params(("parallel",)),
    )(w, g, m, v)


def _position():
    x, y, c = lax.axis_index("x"), lax.axis_index("y"), lax.axis_index("c")
    return x, y, c


def _other_chips(x, y):
    return [(1 - x, y), (x, 1 - y), (1 - x, 1 - y)]


def _stage_in(srcs, stage, sems):
    cps = [pltpu.make_async_copy(src, stage[a], sems.at[a]) for a, src in enumerate(srcs)]
    for cp in cps:
        cp.start()
    return cps


def _stage_out(staged, stage, dsts, sems):
    cps = []
    for a, dst in enumerate(dsts):
        staged[a].wait()
        cp = pltpu.make_async_copy(stage[a], dst, sems.at[a])
        cp.start()
        cps.append(cp)
    return cps


def _gather_chips(shards):
    n = len(shards)

    def body(*refs):
        ins, outs, stage = refs[:n], refs[n:2 * n], refs[2 * n:3 * n]
        send_sems, recv_sems, in_sems, out_sems = refs[3 * n:]
        x, y, c = _position()
        me = 2 * x + y
        chips = _other_chips(x, y)
        remote = []
        staged = _stage_in(ins, stage, in_sems)
        for a in range(n):
            for k, (px, py) in enumerate(chips):
                rc = pltpu.make_async_remote_copy(
                    src_ref=ins[a], dst_ref=outs[a].at[me], send_sem=send_sems.at[3 * a + k],
                    recv_sem=recv_sems.at[3 * a + k], device_id=(px, py, c), device_id_type=MESH)
                rc.start()
                remote.append(rc)
        local = _stage_out(staged, stage, [o.at[me] for o in outs], out_sems)
        for a in range(n):
            for k, (px, py) in enumerate(chips):
                pltpu.make_async_remote_copy(
                    src_ref=ins[a], dst_ref=outs[a].at[2 * px + py], send_sem=send_sems.at[3 * a + k],
                    recv_sem=recv_sems.at[3 * a + k], device_id=(px, py, c), device_id_type=MESH).wait_recv()
        for rc in remote:
            rc.wait_send()
        for cp in local:
            cp.wait()

    return pl.pallas_call(
        body, name="gather_chips", in_specs=[ANY] * n, out_specs=[ANY] * n,
        out_shape=[jax.ShapeDtypeStruct((N_CHIPS,) + s.shape, s.dtype) for s in shards],
        scratch_shapes=[pltpu.VMEM(s.shape, s.dtype) for s in shards]
        + [pltpu.SemaphoreType.DMA((3 * n,)), pltpu.SemaphoreType.DMA((3 * n,)),
           pltpu.SemaphoreType.DMA((n,)), pltpu.SemaphoreType.DMA((n,))],
        compiler_params=pltpu.CompilerParams(has_side_effects=True),
    )(*shards)


def _send_other_halves(grads, *, tag):
    n = len(grads)

    def body(*refs):
        ins, sib = refs[:n], refs[n:2 * n]
        send_sems, recv_sems = refs[2 * n:]
        x, y, c = _position()
        remote = []
        for a in range(n):
            half = ins[a].shape[1] // 2
            give = ins[a].at[:, pl.ds(pl.multiple_of((1 - c) * half, 8), half), :]
            rc = pltpu.make_async_remote_copy(
                src_ref=give, dst_ref=sib[a], send_sem=send_sems.at[a], recv_sem=recv_sems.at[a],
                device_id=(x, y, 1 - c), device_id_type=MESH)
            rc.start()
            remote.append(rc)
        for rc in remote:
            rc.wait_recv()
        for rc in remote:
            rc.wait_send()

    return pl.pallas_call(
        body, name="send_other_halves_" + tag, in_specs=[ANY] * n, out_specs=[ANY] * n,
        out_shape=[jax.ShapeDtypeStruct((g.shape[0], g.shape[1] // 2, g.shape[2]), g.dtype) for g in grads],
        scratch_shapes=[pltpu.SemaphoreType.DMA((n,)), pltpu.SemaphoreType.DMA((n,))],
        compiler_params=pltpu.CompilerParams(has_side_effects=True),
    )(*grads)


def _add_own_half(g4, sib, core, *, name):
    g, a2, c = sib.shape
    tr = _row_tile(a2, c, 4)

    def body(core_ref, a_ref, b_ref, o_ref, o16_ref):
        tot = a_ref[...] + b_ref[...]
        o_ref[...] = tot
        o16_ref[...] = tot.astype(BF16)

    blk = pl.BlockSpec((None, tr, c), lambda i, j, core_ref: (i, j, 0))
    return pl.pallas_call(
        body, name=name,
        grid_spec=pltpu.PrefetchScalarGridSpec(
            num_scalar_prefetch=1, grid=(g, a2 // tr),
            in_specs=[pl.BlockSpec((None, None, tr, c), lambda i, j, core_ref: (i, core_ref[0], j, 0)), blk],
            out_specs=[blk, blk]),
        out_shape=[jax.ShapeDtypeStruct(sib.shape, F32), jax.ShapeDtypeStruct(sib.shape, BF16)],
        compiler_params=_params(("parallel", "parallel")),
    )(core, g4.reshape(g, 2, a2, c), sib)


def _sum_chips(landed, own, chip, *, name):
    g, r, c = landed.shape
    tr = _row_tile(r, c, 5)

    def body(chip_ref, *refs):
        me = chip_ref[0]
        own_v = refs[g][...]
        tot = None
        for slot in range(g):
            term = jnp.where(me == slot, own_v, refs[slot][...].astype(F32))
            tot = term if tot is None else tot + term
        refs[g + 1][...] = tot

    def landed_spec(slot):
        return pl.BlockSpec((None, tr, c),
                            lambda i, chip_ref: (jnp.where(chip_ref[0] == slot, (slot + 1) % g, slot), i, 0))

    return pl.pallas_call(
        body, name=name,
        grid_spec=pltpu.PrefetchScalarGridSpec(
            num_scalar_prefetch=1, grid=(r // tr,),
            in_specs=[landed_spec(k) for k in range(g)]
            + [pl.BlockSpec((None, tr, c), lambda i, chip_ref: (chip_ref[0], i, 0))],
            out_specs=pl.BlockSpec((tr, c), lambda i, chip_ref: (i, 0))),
        out_shape=jax.ShapeDtypeStruct((r, c), F32), compiler_params=_params(("parallel",)),
    )(chip, *([landed] * g), own)


def _join_halves(halves):
    n = len(halves)

    def body(*refs):
        ins, outs, stage = refs[:n], refs[n:2 * n], refs[2 * n:3 * n]
        send_sems, recv_sems, in_sems, out_sems = refs[3 * n:]
        x, y, c = _position()
        remote = []
        staged = _stage_in(ins, stage, in_sems)
        for a in range(n):
            rc = pltpu.make_async_remote_copy(
                src_ref=ins[a], dst_ref=outs[a].at[c], send_sem=send_sems.at[a], recv_sem=recv_sems.at[a],
                device_id=(x, y, 1 - c), device_id_type=MESH)
            rc.start()
            remote.append(rc)
        local = _stage_out(staged, stage, [o.at[c] for o in outs], out_sems)
        for a in range(n):
            pltpu.make_async_remote_copy(
                src_ref=ins[a], dst_ref=outs[a].at[1 - c], send_sem=send_sems.at[a], recv_sem=recv_sems.at[a],
                device_id=(x, y, 1 - c), device_id_type=MESH).wait_recv()
        for rc in remote:
            rc.wait_send()
        for cp in local:
            cp.wait()

    return pl.pallas_call(
        body, name="join_halves", in_specs=[ANY] * n, out_specs=[ANY] * n,
        out_shape=[jax.ShapeDtypeStruct((2,) + h.shape, h.dtype) for h in halves],
        scratch_shapes=[pltpu.VMEM(h.shape, h.dtype) for h in halves]
        + [pltpu.SemaphoreType.DMA((n,)), pltpu.SemaphoreType.DMA((n,)), pltpu.SemaphoreType.DMA((n,)),
           pltpu.SemaphoreType.DMA((n,))],
        compiler_params=pltpu.CompilerParams(has_side_effects=True),
    )(*halves)


_HBM = pl.BlockSpec(memory_space=pltpu.HBM)
_SEM = pl.BlockSpec(memory_space=pltpu.SEMAPHORE)


def _chip_copies(srcs, lands, send_sems, recv_sems, scatter):
    x, y, c = _position()
    me = 2 * x + y
    outgoing, incoming = [], []
    for a, (src, land) in enumerate(zip(srcs, lands)):
        for k, (px, py) in enumerate(_other_chips(x, y)):
            peer = 2 * px + py
            sems = dict(send_sem=send_sems.at[3 * a + k], recv_sem=recv_sems.at[3 * a + k], device_id=(px, py, c),
                        device_id_type=MESH)
            outgoing.append(pltpu.make_async_remote_copy(
                src_ref=src.at[peer] if scatter else src, dst_ref=land.at[me], **sems))
            incoming.append(pltpu.make_async_remote_copy(
                src_ref=src.at[me] if scatter else src, dst_ref=land.at[peer], **sems))
    return outgoing, incoming


def _chips_start(srcs, *, scatter, name):
    n = len(srcs)
    lands = [lax.empty(a.shape if scatter else (N_CHIPS,) + a.shape, a.dtype) for a in srcs]

    def body(*refs):
        ins, send_sems, recv_sems, token = refs[:2 * n], refs[2 * n], refs[2 * n + 1], refs[-1]
        outgoing, _ = _chip_copies(ins[:n], ins[n:], send_sems, recv_sems, scatter)
        for cp in outgoing:
            cp.start()
        token[...] = jnp.zeros_like(token)

    bufs = list(srcs) + lands
    res = pl.pallas_call(
        body, name=name, in_specs=[_HBM] * (2 * n),
        out_specs=(_SEM, _SEM, *[_HBM] * (2 * n), pl.BlockSpec(memory_space=pltpu.VMEM)),
        out_shape=(pltpu.SemaphoreType.DMA((3 * n,)), pltpu.SemaphoreType.DMA((3 * n,)),
                   *[pltpu.HBM(a.shape, a.dtype) for a in bufs], jax.ShapeDtypeStruct((8, 128), F32)),
        input_output_aliases={i: 2 + i for i in range(2 * n)},
        compiler_params=pltpu.CompilerParams(has_side_effects=pltpu.SideEffectType.DATAFLOW_SIDE_EFFECTING),
    )(*[pltpu.with_memory_space_constraint(a, pltpu.HBM) for a in bufs])
    return res[0], res[1], list(res[2:2 + n]), list(res[2 + n:2 + 2 * n]), res[-1]


def _chips_wait(handle, after, *, scatter, name):
    send_sems, recv_sems, srcs, lands, _ = handle
    n = len(srcs)

    def body(*refs):
        ins, send_ref, recv_ref = refs[:2 * n], refs[2 * n], refs[2 * n + 1]
        outgoing, incoming = _chip_copies(ins[:n], ins[n:], send_ref, recv_ref, scatter)
        for cp in outgoing:
            cp.wait_send()
        for cp in incoming:
            cp.wait_recv()

    bufs = list(srcs) + list(lands)
    res = pl.pallas_call(
        body, name=name, in_specs=[_HBM] * (2 * n) + [_SEM, _SEM, ANY], out_specs=tuple([_HBM] * (2 * n)),
        out_shape=tuple(pltpu.HBM(a.shape, a.dtype) for a in bufs),
        input_output_aliases={i: i for i in range(2 * n)},
        compiler_params=pltpu.CompilerParams(has_side_effects=pltpu.SideEffectType.DATAFLOW_SIDE_EFFECTING),
    )(*bufs, send_sems, recv_sems, after)
    return list(res[:n]), list(res[n:])


def _gather_all(buf):
    def body(in_ref, out_ref, send_sems, recv_sems, local_sem):
        x, y, c = _position()
        me = 4 * x + 2 * y + c
        peers = [(x, y, 1 - c)] + [(px, py, pc) for (px, py) in _other_chips(x, y) for pc in (c, 1 - c)]
        cp = pltpu.make_async_copy(in_ref, out_ref.at[me], local_sem)
        cp.start()
        remote = []
        for k, peer in enumerate(peers):
            rc = pltpu.make_async_remote_copy(
                src_ref=in_ref, dst_ref=out_ref.at[me], send_sem=send_sems.at[k], recv_sem=recv_sems.at[k],
                device_id=peer, device_id_type=MESH)
            rc.start()
            remote.append(rc)
        for k, (px, py, pc) in enumerate(peers):
            pltpu.make_async_remote_copy(
                src_ref=in_ref, dst_ref=out_ref.at[4 * px + 2 * py + pc], send_sem=send_sems.at[k],
                recv_sem=recv_sems.at[k], device_id=(px, py, pc), device_id_type=MESH).wait_recv()
        for rc in remote:
            rc.wait_send()
        cp.wait()

    return pl.pallas_call(
        body, name="gather_all", in_specs=[ANY], out_specs=ANY,
        out_shape=jax.ShapeDtypeStruct((N_DEV,) + buf.shape, buf.dtype),
        scratch_shapes=[pltpu.SemaphoreType.DMA((N_DEV - 1,)), pltpu.SemaphoreType.DMA((N_DEV - 1,)),
                        pltpu.SemaphoreType.DMA],
        compiler_params=pltpu.CompilerParams(has_side_effects=True),
    )(buf)


def _cols_from_shards(g4):
    _, k, n = g4.shape
    return g4.transpose(1, 0, 2).reshape(k, N_CHIPS * n)


def _cols_to_shards(w):
    k, n = w.shape
    return w.reshape(k, N_CHIPS, n // N_CHIPS).transpose(1, 0, 2)


def _pad_heads(w, width):
    k = w.shape[0]
    w3 = w.reshape(k, N_HEADS, width)
    return jnp.pad(w3, ((0, 0), (0, 0), (0, HEAD_PAD - width))).reshape(k, D_ATT)


def _unpad_heads(w, width):
    k = w.shape[0]
    return w.reshape(k, N_HEADS, HEAD_PAD)[:, :, :width]


def _rope_tables(s):
    pos = jnp.arange(s, dtype=F32)
    inv_freq = ROPE_THETA ** (-jnp.arange(0, QK_ROPE, 2, dtype=F32) / QK_ROPE)
    ang = pos[:, None] * inv_freq[None, :]
    cos_h, sin_h = jnp.cos(ang), jnp.sin(ang)
    half = QK_ROPE // 2
    z = jnp.zeros((s, half), F32)
    ones = jnp.ones((s, QK_NOPE), F32)
    tail = jnp.zeros((s, HEAD_PAD - QK_NOPE - QK_ROPE), F32)
    cos = jnp.concatenate([ones, cos_h, cos_h, tail + 1.0], axis=1)
    sin_a = jnp.concatenate([ones * 0.0, -sin_h, z, tail], axis=1)
    sin_b = jnp.concatenate([ones * 0.0, z, sin_h, tail], axis=1)
    return cos, sin_a, sin_b


def _local_step(x, p, target, wts, late_weights, reduce_early, reduce_last):
    s = x.shape[0]
    cos, sin_a, sin_b = _rope_tables(s)
    g1, gq, gkv, g2, g3, gf = (wts[k] for k in ("norm_mix_g", "q_norm_g", "kv_norm_g", "norm_ffn_g", "ple_norm_g",
                                                 "final_norm_g"))
    w_in_p, w_uq_p, w_kv_p = wts["w_in_p"], wts["w_uq_p"], wts["w_kv_p"]
    conv_w8, fconv_w, fconv_b = wts["conv_w8"], wts["ffn_conv_w"], wts["ffn_conv_b"]

    (h, z), _ = _mm_fused(x, w_in_p, name="mm_in", prologue=_pro_rms, vecs=[g1], epilogue=_epi_plain, row_outs=[F32])
    y_conv, qn, kvn, kr = _mix_pre(z, conv_w8, gq, gkv, cos, sin_a, sin_b)
    q, k, v, q_t = _qkv_proj(qn, kvn, kr, w_uq_p, w_kv_p, cos, sin_a, sin_b)
    o, lse = _flash_fwd(q, k, v)
    late = late_weights(lse)
    w_o_a, w_o_b, w_up, w_down = late["w_o_a"], late["w_o_b"], late["w_up"], late["w_down"]
    w_pg, w_pp = late["w_ple_gate"], late["w_ple_proj"]
    (x1, hf), _ = _mm_fused(o, w_o_b, second=(y_conv, w_o_a), name="mm_o", rows=[x], vecs=[g2],
                            epilogue=_epi_add_rms, row_outs=[F32, BF16])
    a_pre, act = _ffn_fwd(hf, w_up, fconv_w, fconv_b)
    (x2, n3), _ = _mm_fused(act, w_down, name="mm_down", rows=[x1], vecs=[g3], epilogue=_epi_add_rms,
                            row_outs=[F32, BF16])
    loss, dx3, dgl, dpp, d_gf = _ple_final(x2, n3, p, target, gf, w_pg, w_pp)

    grads, early = {"final_norm_g": d_gf}, {}
    early["w_ple_proj"] = _mm(p, dpp, ta=True, name="mm_d_wpp", tm=256, tn=1024, tk=2048)
    early["w_ple_gate"] = _mm(n3, dgl, ta=True, name="mm_d_wpg", tm=1024, tn=1024, tk=2048)
    (dx2,), (grads["ple_norm_g"],) = _mm_fused(dgl, w_pg, tb=True, name="mm_d_n3", rows=[x2, dx3], vecs=[g3],
                                               epilogue=_epi_rms_bwd, row_outs=[F32], n_vec_out=1)
    early["w_down"] = _mm(act, dx2, ta=True, name="mm_d_wdown", tm=1408, tn=1024, tk=2048)
    da_pre, grads["ffn_conv_w"], grads["ffn_conv_b"] = _ffn_bwd(dx2, w_down, a_pre, fconv_w, fconv_b)
    early["w_up"] = _mm(hf, da_pre, ta=True, b_split=True, name="mm_d_wup", tm=1024, tn=1408, tk=2048,
                       o_shards=True)
    (dx1,), (grads["norm_ffn_g"],) = _mm_fused(da_pre, w_up, tb=True, a_split=True, name="mm_d_hf", rows=[x1, dx2],
                                               vecs=[g2], epilogue=_epi_rms_bwd, row_outs=[F32], n_vec_out=1)
    d_wo_a = _mm(y_conv, dx1, ta=True, name="mm_d_wo_conv", tm=512, tn=1024, tk=2048)
    d_wo_b = _mm(o, dx1, ta=True, name="mm_d_wo_att", tm=1024, tn=1024, tk=2048)
    early["w_o"] = jnp.concatenate([d_wo_a, d_wo_b.reshape(N_HEADS, HEAD_PAD, D_MODEL)[:, :V_HEAD]
                                    .reshape(N_HEADS * V_HEAD, D_MODEL)], axis=0)
    token, finish = reduce_early(early)
    dyc = _mm(dx1, w_o_a, tb=True, name="mm_d_yconv", tm=512, tn=512, tk=1024)
    (do, do_t), _ = _mm_fused(dx1, w_o_b, tb=True, name="mm_d_o", epilogue=_epi_plain, row_outs=[BF16],
                              transposed_out=BF16)
    delta = _attn_delta(do, o)
    dq, dk, dv = _flash_bwd(q, q_t, k, v, do, do_t, lse, delta, token)
    reduced_early = finish(dq)
    dq_pre, dkr = _qk_bwd(dq, dk, cos, sin_a, sin_b)
    grads["w_uq_p"] = _mm(qn, dq_pre, ta=True, name="mm_d_wuq", tm=256, tn=1024, tk=2048)
    dqn = _mm(dq_pre, w_uq_p, tb=True, name="mm_d_qn", tm=512, tn=256, tk=1024)
    grads["w_k_p"] = _mm(kvn, dk, ta=True, name="mm_d_wk", tm=128, tn=1024, tk=2048)
    grads["w_v_p"] = _mm(kvn, dv, ta=True, name="mm_d_wv", tm=128, tn=1024, tk=2048)
    dkvn_k = _mm(dk, w_kv_p[:, :D_ATT], tb=True, name="mm_d_kvn_k", tm=512, tn=128, tk=1024)
    dkvn = _mm(dv, w_kv_p[:, D_ATT:], tb=True, add=dkvn_k, name="mm_d_kvn_v", tm=512, tn=128, tk=1024)
    dz, grads["conv_w"], grads["q_norm_g"], grads["kv_norm_g"] = _mix_bwd(
        z, dyc, dqn, dkvn, dkr, conv_w8, gq, gkv, cos, sin_a, sin_b)
    grads["w_in_p"] = _mm(h, dz, ta=True, name="mm_d_win", tm=1024, tn=1024, tk=2048)
    token, finish = reduce_last({n: grads.pop(n) for n in ("w_in_p", "w_uq_p", "w_k_p", "w_v_p")})
    (grad_x,), (grads["norm_mix_g"],) = _mm_fused(dz, w_in_p, tb=True, name="mm_d_h", rows=[x, dx1],
                                                  vecs=[g1 + token[0, 0]], epilogue=_epi_rms_bwd, row_outs=[F32],
                                                  n_vec_out=1)
    return loss[0, 0], grad_x, grads, reduced_early, finish(grad_x)


_EARLY_W = ("w_in", "w_uq", "w_ukv")
_LATE_W = ("w_o", "w_up", "w_down", "w_ple_gate", "w_ple_proj")
_BIG = _EARLY_W + _LATE_W
_COL_SHARDED = ("w_in", "w_uq", "w_ukv", "w_up", "w_ple_proj")
_SMALL = ("norm_mix_g", "conv_w", "q_norm_g", "kv_norm_g", "norm_ffn_g", "ffn_conv_w", "ffn_conv_b", "ple_norm_g",
          "final_norm_g")


def _full_from_slots(n, g4):
    return _cols_from_shards(g4) if n in _COL_SHARDED else g4.reshape(-1, g4.shape[2])


def _shard_major(n, g):
    if g.ndim == 3:
        return g
    return _cols_to_shards(g) if n in _COL_SHARDED else g.reshape(N_CHIPS, g.shape[0] // N_CHIPS, g.shape[1])


def _early_weights(w):
    shards = [w[n][0].astype(BF16) for n in _EARLY_W]
    shards.append(jnp.pad(w["conv_w"][0], ((0, 5), (0, 0))))
    shards.append(jnp.pad(w["ffn_conv_w"][0], ((0, 5), (0, 0))))
    got = _gather_chips(shards)
    full = {n: _full_from_slots(n, g4) for n, g4 in zip(_EARLY_W, got)}
    full["conv_w8"] = _cols_from_shards(got[len(_EARLY_W)])
    full["ffn_conv_w8"] = _cols_from_shards(got[len(_EARLY_W) + 1])
    return _layout_early(full, w)


def _layout_early(full, w):
    out = {n: w[n] for n in ("norm_mix_g", "q_norm_g", "kv_norm_g", "norm_ffn_g", "ple_norm_g")}
    out["final_norm_g"] = w["final_norm_g"][None, :]
    w_in = full["w_in"]
    zc = jnp.zeros((D_MODEL, QK_NOPE), BF16)
    zt = jnp.zeros((D_MODEL, HEAD_PAD - QK_NOPE - QK_ROPE), BF16)
    out["w_in_p"] = jnp.concatenate([w_in[:, :D_IN - QK_ROPE], zc, w_in[:, D_IN - QK_ROPE:], zt], axis=1)
    out["w_uq_p"] = _pad_heads(full["w_uq"], QK_NOPE + QK_ROPE)
    kv3 = full["w_ukv"].reshape(KV_LORA, N_HEADS, QK_NOPE + V_HEAD)
    out["w_kv_p"] = jnp.concatenate([_pad_heads(kv3[:, :, :QK_NOPE].reshape(KV_LORA, -1), QK_NOPE),
                                     _pad_heads(kv3[:, :, QK_NOPE:].reshape(KV_LORA, -1), V_HEAD)], axis=1)
    out["conv_w8"] = full["conv_w8"]
    fw = full["ffn_conv_w8"]
    out["ffn_conv_w"] = jnp.stack([fw[:, :D_FF], fw[:, D_FF:]])
    out["ffn_conv_b"] = w["ffn_conv_b"].reshape(2, 1, D_FF)
    return out


def _layout_late(full):
    w_o = full["w_o"]
    out = {"w_o_a": w_o[:CONV_WIDTH]}
    out["w_o_b"] = jnp.pad(w_o[CONV_WIDTH:].reshape(N_HEADS, V_HEAD, D_MODEL),
                           ((0, 0), (0, HEAD_PAD - V_HEAD), (0, 0))).reshape(D_ATT, D_MODEL)
    for n in ("w_up", "w_down", "w_ple_gate", "w_ple_proj"):
        out[n] = full[n]
    return out


def _true_matrices(g):
    out = {}
    wp = g["w_in_p"]
    out["w_in"] = jnp.concatenate([wp[:, :D_IN - QK_ROPE], wp[:, D_IN_PAD - HEAD_PAD + QK_NOPE:
                                                              D_IN_PAD - HEAD_PAD + QK_NOPE + QK_ROPE]], axis=1)
    out["w_uq"] = _unpad_heads(g["w_uq_p"], QK_NOPE + QK_ROPE).reshape(Q_LORA, -1)
    out["w_ukv"] = jnp.concatenate([_unpad_heads(g["w_k_p"], QK_NOPE), _unpad_heads(g["w_v_p"], V_HEAD)],
                                   axis=2).reshape(KV_LORA, -1)
    return out


def _true_vectors(g):
    out = {}
    out["conv_w"] = g["conv_w"]
    fw = g["ffn_conv_w"]
    out["ffn_conv_w"] = jnp.concatenate([fw[0, :3], fw[1, :3]], axis=1)
    out["ffn_conv_b"] = g["ffn_conv_b"].reshape(1, 2 * D_FF)
    for n in ("norm_mix_g", "q_norm_g", "kv_norm_g", "norm_ffn_g", "ple_norm_g", "final_norm_g"):
        out[n] = g[n]
    return out


def _chip_partials(names, g, core, *, tag):
    g4 = [_shard_major(n, g[n]) for n in names]
    sib = _send_other_halves(g4, tag=tag)
    return [_add_own_half(a, b, core, name="add_cores_" + n) for n, a, b in zip(names, g4, sib)]


_SMALL_SIZES = {"norm_mix_g": D_MODEL, "conv_w": 3 * CONV_WIDTH, "q_norm_g": Q_LORA, "kv_norm_g": KV_LORA,
                "norm_ffn_g": D_MODEL, "ffn_conv_w": 6 * D_FF, "ffn_conv_b": 2 * D_FF, "ple_norm_g": D_MODEL,
                "final_norm_g": D_MODEL}


def _pack(parts, rows):
    flat = jnp.concatenate([a.reshape(-1) for a in parts])
    return jnp.pad(flat, (0, rows * 128 - flat.shape[0])).reshape(rows, 128)


def _unpack(buf, sizes):
    flat = buf.reshape(-1)
    out, at = [], 0
    for n in sizes:
        out.append(flat[at:at + n])
        at += n
    return out


def _reduce_small(g, loss):
    sizes = [1] + [_SMALL_SIZES[n] for n in _SMALL]
    rows = -(-sum(sizes) // 1024) * 8
    slots = _gather_all(_pack([loss] + [g[n] for n in _SMALL], rows))
    parts = _unpack(_sum_slots(slots, name="sum_small"), sizes)
    return parts[0][0], dict(zip(_SMALL, parts[1:]))


def kernel(x, p, norm_mix_g, w_in, conv_w, q_norm_g, w_uq, kv_norm_g, w_ukv, w_o, norm_ffn_g, w_up, ffn_conv_w, ffn_conv_b, w_down, ple_norm_g, w_ple_gate, w_ple_proj, final_norm_g, loss_target, m_norm_mix_g, m_w_in, m_conv_w, m_q_norm_g, m_w_uq, m_kv_norm_g, m_w_ukv, m_w_o, m_norm_ffn_g, m_w_up, m_ffn_conv_w, m_ffn_conv_b, m_w_down, m_ple_norm_g, m_w_ple_gate, m_w_ple_proj, m_final_norm_g, v_norm_mix_g, v_w_in, v_conv_w, v_q_norm_g, v_w_uq, v_kv_norm_g, v_w_ukv, v_w_o, v_norm_ffn_g, v_w_up, v_ffn_conv_w, v_ffn_conv_b, v_w_down, v_ple_norm_g, v_w_ple_gate, v_w_ple_proj, v_final_norm_g):
    names = ["norm_mix_g", "w_in", "conv_w", "q_norm_g", "w_uq", "kv_norm_g", "w_ukv", "w_o", "norm_ffn_g", "w_up",
             "ffn_conv_w", "ffn_conv_b", "w_down", "ple_norm_g", "w_ple_gate", "w_ple_proj", "final_norm_g"]
    w = dict(zip(names, (norm_mix_g, w_in, conv_w, q_norm_g, w_uq, kv_norm_g, w_ukv, w_o, norm_ffn_g, w_up,
                         ffn_conv_w, ffn_conv_b, w_down, ple_norm_g, w_ple_gate, w_ple_proj, final_norm_g)))
    m = dict(zip(names, (m_norm_mix_g, m_w_in, m_conv_w, m_q_norm_g, m_w_uq, m_kv_norm_g, m_w_ukv, m_w_o,
                         m_norm_ffn_g, m_w_up, m_ffn_conv_w, m_ffn_conv_b, m_w_down, m_ple_norm_g, m_w_ple_gate,
                         m_w_ple_proj, m_final_norm_g)))
    v = dict(zip(names, (v_norm_mix_g, v_w_in, v_conv_w, v_q_norm_g, v_w_uq, v_kv_norm_g, v_w_ukv, v_w_o,
                         v_norm_ffn_g, v_w_up, v_ffn_conv_w, v_ffn_conv_b, v_w_down, v_ple_norm_g, v_w_ple_gate,
                         v_w_ple_proj, v_final_norm_g)))

    core = lax.axis_index("c").astype(jnp.int32).reshape(1)
    chip = (2 * lax.axis_index("x") + lax.axis_index("y")).astype(jnp.int32).reshape(1)

    wts = _early_weights(w)
    gather = _chips_start([w[n][0].astype(BF16) for n in _LATE_W], scatter=False, name="gather_late_start")
    wts["norm_mix_g"] = wts["norm_mix_g"] + gather[4][0, 0]

    def late_weights(after):
        shards, landed = _chips_wait(gather, after, scatter=False, name="gather_late_wait")
        full = {n: _full_from_slots(n, lax.dynamic_update_slice(g4, own[None], (chip[0], 0, 0)))
                for n, own, g4 in zip(_LATE_W, shards, landed)}
        return _layout_late(full)

    def reduce_early(g):
        parts = _chip_partials(_LATE_W, g, core, tag="early")
        scatter = _chips_start([t16 for _, t16 in parts], scatter=True, name="scatter_early_start")

        def finish(after):
            _, landed = _chips_wait(scatter, after, scatter=True, name="scatter_early_wait")
            return [_sum_chips(a, t32, chip, name="sum_chips_" + n) for n, a, (t32, _) in zip(_LATE_W, landed, parts)]

        return scatter[4], finish

    def reduce_last(g):
        parts = _chip_partials(_EARLY_W, _true_matrices(g), core, tag="late")
        scatter = _chips_start([t16 for _, t16 in parts], scatter=True, name="scatter_late_start")

        def finish(after):
            _, landed = _chips_wait(scatter, after, scatter=True, name="scatter_late_wait")
            return [_sum_chips(a, t32, chip, name="sum_chips_" + n) for n, a, (t32, _) in zip(_EARLY_W, landed, parts)]

        return scatter[4], finish

    loss, grad_x, small_grads, halves_early, halves_last = _local_step(
        x[0], p[0, 0], loss_target[0], wts, late_weights, reduce_early, reduce_last)
    g_full = _true_vectors(small_grads)
    whole = _join_halves(halves_last + halves_early)
    big = {n: a.reshape(-1, a.shape[2]) for n, a in zip(_BIG, whole)}

    g_out, d_out, m_out, v_out = {}, {}, {}, {}
    for n in _BIG:
        shape = w[n].shape
        g = big[n]
        d, mn, vn = _adamw(w[n][0], g, m[n][0], v[n][0], name="adamw_" + n)
        g_out[n], d_out[n], m_out[n], v_out[n] = (a.reshape(shape) for a in (g, d, mn, vn))

    loss, small = _reduce_small(g_full, loss)
    chip = 2 * lax.axis_index("x") + lax.axis_index("y")
    g_small = {}
    for n in _SMALL:
        shape = w[n].shape
        g = small[n]
        if n in ("conv_w", "ffn_conv_w"):
            width = shape[-1]
            g = lax.dynamic_slice(g.reshape(3, N_CHIPS * width), (0, chip * width), (3, width))
        g_small[n] = g.reshape(shape)
    sizes = [g_small[n].size for n in _SMALL]
    rows = -(-sum(sizes) // 1024) * 8
    packed = [_pack([src[n] for n in _SMALL], rows) for src in (w, g_small, m, v)]
    d_s, m_s, v_s = _adamw(*packed, name="adamw_small")
    for n, d, mn, vn in zip(_SMALL, _unpack(d_s, sizes), _unpack(m_s, sizes), _unpack(v_s, sizes)):
        shape = w[n].shape
        g_out[n], d_out[n], m_out[n], v_out[n] = g_small[n], d.reshape(shape), mn.reshape(shape), vn.reshape(shape)

    return (loss, grad_x[None], *[g_out[n] for n in names], *[d_out[n] for n in names],
            *[m_out[n] for n in names], *[v_out[n] for n in names])
```

```python
import functools

import jax
import jax.numpy as jnp
from jax import lax
from jax.experimental import pallas as pl
from jax.experimental.pallas import tpu as pltpu

F32 = jnp.float32
BF16 = jnp.bfloat16

D_MODEL = 1024
CONV_WIDTH = 512
Q_LORA = 256
KV_LORA = 128
QK_NOPE = 64
QK_ROPE = 32
V_HEAD = 64
N_HEADS = 8
HEAD_PAD = 128
D_ATT = N_HEADS * HEAD_PAD
D_IN = 3 * CONV_WIDTH + Q_LORA + KV_LORA + QK_ROPE
D_IN_PAD = 3 * CONV_WIDTH + Q_LORA + KV_LORA + HEAD_PAD
D_FF = 2816
ROPE_THETA = 10000.0
EPS = 1e-6
SM_SCALE = (QK_NOPE + QK_ROPE) ** -0.5
ONES_LANE = V_HEAD

ADAM_LR = 0.001
ADAM_B1 = 0.9
ADAM_B2 = 0.999
ADAM_EPS = 1e-08
ADAM_WD = 0.01
ADAM_STEP = 10

N_CHIPS = 4
N_DEV = 8
MESH = pl.DeviceIdType.MESH
ANY = pl.BlockSpec(memory_space=pl.ANY)


def _params(sem):
    return pltpu.CompilerParams(dimension_semantics=sem)


def _mm(a, b, *, name, ta=False, tb=False, add=None, out_dtype=F32, tm=512, tn=512, tk=512,
        a_split=False, b_split=False, o_split=False, o_shards=False):
    if a_split:
        _, m, kh = a.shape
        k = 2 * kh
    elif ta:
        k, m = a.shape
    else:
        m, k = a.shape
    if b_split:
        _, kb, nh = b.shape
        n = 2 * nh
    elif tb:
        n, kb = b.shape
    else:
        kb, n = b.shape
    assert kb == k, (name, a.shape, b.shape)
    tm, tn, tk = min(tm, m), min(tn, n), min(tk, k)
    assert m % tm == 0 and n % tn == 0 and k % tk == 0, (name, m, n, k, tm, tn, tk)
    gm, gn, gk = m // tm, n // tn, k // tk

    if a_split:
        assert gk % 2 == 0
        a_spec = pl.BlockSpec((None, tm, tk), lambda i, j, kk: (kk // (gk // 2), i, kk % (gk // 2)))
    elif ta:
        a_spec = pl.BlockSpec((tk, tm), lambda i, j, kk: (kk, i))
    else:
        a_spec = pl.BlockSpec((tm, tk), lambda i, j, kk: (i, kk))
    if b_split:
        assert gn % 2 == 0
        b_spec = pl.BlockSpec((None, tk, tn), lambda i, j, kk: (j // (gn // 2), kk, j % (gn // 2)))
    elif tb:
        b_spec = pl.BlockSpec((tn, tk), lambda i, j, kk: (j, kk))
    else:
        b_spec = pl.BlockSpec((tk, tn), lambda i, j, kk: (kk, j))
    if o_shards:
        o_spec = pl.BlockSpec((None, tm, tn), lambda i, j, kk: (j, i, 0))
        o_shape = jax.ShapeDtypeStruct((gn, m, tn), out_dtype)
    elif o_split:
        assert gn % 2 == 0
        o_spec = pl.BlockSpec((None, tm, tn), lambda i, j, kk: (j // (gn // 2), i, j % (gn // 2)))
        o_shape = jax.ShapeDtypeStruct((2, m, n // 2), out_dtype)
    else:
        o_spec = pl.BlockSpec((tm, tn), lambda i, j, kk: (i, j))
        o_shape = jax.ShapeDtypeStruct((m, n), out_dtype)
    dims = (((0 if ta else 1,), (1 if tb else 0,)), ((), ()))

    def body(*refs):
        a_ref, b_ref = refs[:2]
        add_ref = None if add is None else refs[2]
        o_ref = refs[2 if add is None else 3]

        def finish(r):
            if add_ref is not None:
                r = r + add_ref[...]
            o_ref[...] = r.astype(o_ref.dtype)

        part = lax.dot_general(a_ref[...].astype(BF16), b_ref[...].astype(BF16), dims, preferred_element_type=F32)
        if gk == 1:
            finish(part)
            return
        acc_ref = refs[-1]
        kk = pl.program_id(2)

        @pl.when(kk == 0)
        def _():
            acc_ref[...] = part

        @pl.when((kk > 0) & (kk < gk - 1))
        def _():
            acc_ref[...] += part

        @pl.when(kk == gk - 1)
        def _():
            finish(acc_ref[...] + part)

    in_specs = [a_spec, b_spec]
    args = [a, b]
    if add is not None:
        in_specs.append(pl.BlockSpec((tm, tn), lambda i, j, kk: (i, j)))
        args.append(add)
    return pl.pallas_call(
        body, name=name, grid=(gm, gn, gk), in_specs=in_specs, out_specs=o_spec, out_shape=o_shape,
        scratch_shapes=[] if gk == 1 else [pltpu.VMEM((tm, tn), F32)],
        compiler_params=_params(("parallel", "parallel", "arbitrary")),
    )(*args)


def _rms_scale(v):
    return lax.rsqrt(jnp.mean(v * v, axis=-1, keepdims=True) + EPS)


def _rms_bwd_rows(v, g, dy):
    r = _rms_scale(v)
    vh = v * r
    dyg = dy * g
    dv = r * (dyg - vh * jnp.mean(dyg * vh, axis=-1, keepdims=True))
    return dv, dy * vh


def _shift_down(v, first_row):
    row = lax.broadcasted_iota(jnp.int32, v.shape, 0)
    return jnp.where(row == 0, first_row, pltpu.roll(v, 1, 0))


def _shift_up(v, last_row):
    n = v.shape[0]
    row = lax.broadcasted_iota(jnp.int32, v.shape, 0)
    return jnp.where(row == n - 1, last_row, pltpu.roll(v, n - 1, 0))


def _rope(t, cos, sin_a, sin_b):
    return t * cos + pltpu.roll(t, HEAD_PAD - 16, 1) * sin_a + pltpu.roll(t, 16, 1) * sin_b


def _rope_bwd(d, cos, sin_a, sin_b):
    return d * cos + pltpu.roll(d * sin_a, 16, 1) + pltpu.roll(d * sin_b, HEAD_PAD - 16, 1)


def _sigmoid(v):
    return 1.0 / (1.0 + jnp.exp(-v))


def _halo_specs(ts, s, width, col):
    nb = ts // 8
    last = s // 8 - 1
    prev = pl.BlockSpec((8, width), lambda i: (jnp.maximum(i * nb - 1, 0), col))
    nxt = pl.BlockSpec((8, width), lambda i: (jnp.minimum((i + 1) * nb, last), col))
    return prev, nxt


def _mm_fused(a, b, *, name, epilogue, row_outs, rows=(), vecs=(), n_vec_out=0, tb=False, a_split=False,
              prologue=None, second=None, transposed_out=None, tm=512, tk=None):
    if a_split:
        _, m, kh = a.shape
        k = 2 * kh
    else:
        m, k = a.shape
    n = b.shape[0] if tb else b.shape[1]
    assert (b.shape[1] if tb else b.shape[0]) == k, (name, a.shape, b.shape)
    tk = k if tk is None else tk
    assert m % tm == 0 and k % tk == 0, (name, m, k, tm, tk)
    gm, gk = m // tm, k // tk
    assert prologue is None or gk == 1
    n_a = 2 if a_split and gk == 1 else 1
    nr, nv = len(rows), len(vecs)
    n_pro = 0 if prologue is None else 1
    n_sec = 0 if second is None else 2
    n_t = 0 if transposed_out is None else 1
    dims = (((1,), (1 if tb else 0,)), ((), ()))

    def body(*refs):
        a_ref, b_ref = refs[0], refs[n_a]
        refs = refs[n_a - 1:]
        sec_refs = refs[2:2 + n_sec]
        row_refs, vec_refs = refs[2 + n_sec:2 + n_sec + nr], refs[2 + n_sec + nr:2 + n_sec + nr + nv]
        outs = refs[2 + n_sec + nr + nv:]
        row_out_refs = outs[n_pro:n_pro + len(row_outs)]
        t_out_refs = outs[n_pro + len(row_outs):n_pro + len(row_outs) + n_t]
        vec_out_refs = outs[n_pro + len(row_outs) + n_t:n_pro + len(row_outs) + n_t + n_vec_out]
        i, kk = pl.program_id(0), pl.program_id(1)
        vec_vals = [v[...] for v in vec_refs]
        if prologue is None:
            lhs = a_ref[...].astype(BF16)
        else:
            lhs = prologue(a_ref[...], vec_vals)
            outs[0][...] = lhs

        def finish(r):
            if second is not None:
                r = r + jnp.dot(sec_refs[0][...].astype(BF16), sec_refs[1][...].astype(BF16),
                                preferred_element_type=F32)
            row_vals, vec_parts = epilogue(r, [x[...] for x in row_refs], vec_vals)
            for ref, val in zip(row_out_refs, row_vals):
                ref[...] = val.astype(ref.dtype)
            for ref in t_out_refs:
                ref[...] = row_vals[0].T.astype(ref.dtype)
            if n_vec_out:
                @pl.when(i == 0)
                def _():
                    for ref in vec_out_refs:
                        ref[...] = jnp.zeros_like(ref)

                for ref, val in zip(vec_out_refs, vec_parts):
                    ref[...] += val

        if n_a == 2:
            kh = k // 2
            halves = (b_ref[:, :kh], b_ref[:, kh:]) if tb else (b_ref[:kh, :], b_ref[kh:, :])
            part = (lax.dot_general(lhs, halves[0].astype(BF16), dims, preferred_element_type=F32)
                    + lax.dot_general(refs[0][...].astype(BF16), halves[1].astype(BF16), dims,
                                      preferred_element_type=F32))
        else:
            part = lax.dot_general(lhs, b_ref[...].astype(BF16), dims, preferred_element_type=F32)
        if gk == 1:
            finish(part)
            return
        acc_ref = refs[-1]

        @pl.when(kk == 0)
        def _():
            acc_ref[...] = part

        @pl.when((kk > 0) & (kk < gk - 1))
        def _():
            acc_ref[...] += part

        @pl.when(kk == gk - 1)
        def _():
            finish(acc_ref[...] + part)

    if n_a == 2:
        a_specs = [pl.BlockSpec((None, tm, k // 2), lambda i, kk: (0, i, 0)),
                   pl.BlockSpec((None, tm, k // 2), lambda i, kk: (1, i, 0))]
    elif a_split:
        assert gk % 2 == 0
        a_specs = [pl.BlockSpec((None, tm, tk), lambda i, kk: (kk // (gk // 2), i, kk % (gk // 2)))]
    else:
        a_specs = [pl.BlockSpec((tm, tk), lambda i, kk: (i, kk))]
    b_spec = pl.BlockSpec((n, tk), lambda i, kk: (0, kk)) if tb else pl.BlockSpec((tk, n), lambda i, kk: (kk, 0))
    row_spec = pl.BlockSpec((tm, n), lambda i, kk: (i, 0))
    out_specs, out_shape = [], []
    if prologue is not None:
        out_specs.append(pl.BlockSpec((tm, k), lambda i, kk: (i, 0)))
        out_shape.append(jax.ShapeDtypeStruct((m, k), BF16))
    out_specs += [row_spec] * len(row_outs)
    out_shape += [jax.ShapeDtypeStruct((m, n), dt) for dt in row_outs]
    if transposed_out is not None:
        out_specs.append(pl.BlockSpec((n, tm), lambda i, kk: (0, i)))
        out_shape.append(jax.ShapeDtypeStruct((n, m), transposed_out))
    out_specs += [pl.BlockSpec((1, n), lambda i, kk: (0, 0))] * n_vec_out
    out_shape += [jax.ShapeDtypeStruct((1, n), F32)] * n_vec_out
    sec_specs, sec_args = [], []
    if second is not None:
        k2 = second[0].shape[1]
        sec_specs = [pl.BlockSpec((tm, k2), lambda i, kk: (i, 0)), pl.BlockSpec((k2, n), lambda i, kk: (0, 0))]
        sec_args = list(second)
    res = pl.pallas_call(
        body, name=name, grid=(gm, gk),
        in_specs=a_specs + [b_spec] + sec_specs + [row_spec] * nr
        + [pl.BlockSpec((1, v.shape[1]), lambda i, kk: (0, 0)) for v in vecs],
        out_specs=out_specs, out_shape=out_shape,
        scratch_shapes=[] if gk == 1 else [pltpu.VMEM((tm, n), F32)],
        compiler_params=_params(("arbitrary" if n_vec_out else "parallel", "arbitrary")),
    )(*([a] * n_a), b, *sec_args, *rows, *vecs)
    split = n_pro + len(row_outs) + n_t
    return list(res[:split]), list(res[split:])


def _pro_rms(a, vecs):
    return (a * _rms_scale(a) * vecs[0]).astype(BF16)


def _epi_plain(r, rows, vecs):
    return [r], []


def _epi_add_rms(r, rows, vecs):
    xn = r + rows[0]
    return [xn, xn * _rms_scale(xn) * vecs[0]], []


def _epi_rms_bwd(r, rows, vecs):
    dv, dg_rows = _rms_bwd_rows(rows[0], vecs[0], r)
    return [dv + rows[1]], [jnp.sum(dg_rows, axis=0, keepdims=True)]


def _mix_pre(z, conv_w8, gq, gkv, cos, sin_a, sin_b, *, ts=256):
    s = z.shape[0]
    n = s // ts
    cw = CONV_WIDTH

    def body(z_ref, xcp, xcn, cgp, cgn, w_ref, gq_ref, gkv_ref, cos_ref, sa_ref, sb_ref,
             yc_ref, qn_ref, kvn_ref, kr_ref):
        i = pl.program_id(0)
        xc = z_ref[:, 0:cw]
        bg = z_ref[:, cw:2 * cw]
        cg = z_ref[:, 2 * cw:3 * cw]
        m = cg * xc
        m_prev = jnp.where(i > 0, xcp[7:8, :] * cgp[7:8, :], 0.0)
        m_next = jnp.where(i < n - 1, xcn[0:1, :] * cgn[0:1, :], 0.0)
        cm = _shift_down(m, m_prev) * w_ref[0:1, :] + m * w_ref[1:2, :] + _shift_up(m, m_next) * w_ref[2:3, :]
        yc_ref[...] = (bg * cm).astype(BF16)
        ql = z_ref[:, 3 * cw:3 * cw + Q_LORA]
        qn_ref[...] = (ql * _rms_scale(ql) * gq_ref[...]).astype(BF16)
        kvl = z_ref[:, 3 * cw + Q_LORA:3 * cw + Q_LORA + KV_LORA]
        kvn_ref[...] = (kvl * _rms_scale(kvl) * gkv_ref[...]).astype(BF16)
        kr_ref[...] = _rope(z_ref[:, D_IN_PAD - HEAD_PAD:D_IN_PAD], cos_ref[...], sa_ref[...], sb_ref[...])

    xcp, xcn = _halo_specs(ts, s, cw, 0)
    cgp, cgn = _halo_specs(ts, s, cw, 2)
    tab = pl.BlockSpec((ts, HEAD_PAD), lambda i: (i, 0))
    return pl.pallas_call(
        body, name="mix_pre", grid=(n,),
        in_specs=[pl.BlockSpec((ts, D_IN_PAD), lambda i: (i, 0)), xcp, xcn, cgp, cgn,
                  pl.BlockSpec((8, cw), lambda i: (0, 0)), pl.BlockSpec((1, Q_LORA), lambda i: (0, 0)),
                  pl.BlockSpec((1, KV_LORA), lambda i: (0, 0)), tab, tab, tab],
        out_specs=[pl.BlockSpec((ts, cw), lambda i: (i, 0)), pl.BlockSpec((ts, Q_LORA), lambda i: (i, 0)),
                   pl.BlockSpec((ts, KV_LORA), lambda i: (i, 0)), tab],
        out_shape=[jax.ShapeDtypeStruct((s, cw), BF16), jax.ShapeDtypeStruct((s, Q_LORA), BF16),
                   jax.ShapeDtypeStruct((s, KV_LORA), BF16), jax.ShapeDtypeStruct((s, HEAD_PAD), F32)],
        compiler_params=_params(("parallel",)),
    )(z, z, z, z, z, conv_w8, gq, gkv, cos, sin_a, sin_b)


def _mix_bwd(z, dyc, dqn, dkvn, dkr, conv_w8, gq, gkv, cos, sin_a, sin_b, *, ts=256):
    s = z.shape[0]
    n = s // ts
    cw = CONV_WIDTH

    def body(z_ref, xcp, xcn, bgp, bgn, cgp, cgn, dyc_ref, dycp, dycn, dqn_ref, dkvn_ref, dkr_ref,
             w_ref, gq_ref, gkv_ref, cos_ref, sa_ref, sb_ref,
             dz_ref, dw0_ref, dw1_ref, dw2_ref, dgq_ref, dgkv_ref):
        i = pl.program_id(0)

        @pl.when(i == 0)
        def _():
            for r in (dw0_ref, dw1_ref, dw2_ref, dgq_ref, dgkv_ref):
                r[...] = jnp.zeros_like(r)

        xc = z_ref[:, 0:cw]
        bg = z_ref[:, cw:2 * cw]
        cg = z_ref[:, 2 * cw:3 * cw]
        w0, w1, w2 = w_ref[0:1, :], w_ref[1:2, :], w_ref[2:3, :]
        m = cg * xc
        m_dn = _shift_down(m, jnp.where(i > 0, xcp[7:8, :] * cgp[7:8, :], 0.0))
        m_up = _shift_up(m, jnp.where(i < n - 1, xcn[0:1, :] * cgn[0:1, :], 0.0))
        cm = m_dn * w0 + m * w1 + m_up * w2
        dyc_v = dyc_ref[...]
        dcm = dyc_v * bg
        dcm_dn = _shift_down(dcm, jnp.where(i > 0, dycp[7:8, :] * bgp[7:8, :], 0.0))
        dcm_up = _shift_up(dcm, jnp.where(i < n - 1, dycn[0:1, :] * bgn[0:1, :], 0.0))
        dm = dcm_up * w0 + dcm * w1 + dcm_dn * w2
        dz_ref[:, 0:cw] = (dm * cg).astype(BF16)
        dz_ref[:, cw:2 * cw] = (dyc_v * cm).astype(BF16)
        dz_ref[:, 2 * cw:3 * cw] = (dm * xc).astype(BF16)
        dw0_ref[...] += jnp.sum(dcm * m_dn, axis=0, keepdims=True)
        dw1_ref[...] += jnp.sum(dcm * m, axis=0, keepdims=True)
        dw2_ref[...] += jnp.sum(dcm * m_up, axis=0, keepdims=True)

        dql, dgq_rows = _rms_bwd_rows(z_ref[:, 3 * cw:3 * cw + Q_LORA], gq_ref[...], dqn_ref[...])
        dz_ref[:, 3 * cw:3 * cw + Q_LORA] = dql.astype(BF16)
        dgq_ref[...] += jnp.sum(dgq_rows, axis=0, keepdims=True)
        dkvl, dgkv_rows = _rms_bwd_rows(z_ref[:, 3 * cw + Q_LORA:3 * cw + Q_LORA + KV_LORA], gkv_ref[...],
                                        dkvn_ref[...])
        dz_ref[:, 3 * cw + Q_LORA:3 * cw + Q_LORA + KV_LORA] = dkvl.astype(BF16)
        dgkv_ref[...] += jnp.sum(dgkv_rows, axis=0, keepdims=True)

        lane = lax.broadcasted_iota(jnp.int32, (ts, HEAD_PAD), 1)
        rope_lane = (lane >= QK_NOPE) & (lane < QK_NOPE + QK_ROPE)
        dk = _rope_bwd(dkr_ref[...], cos_ref[...], sa_ref[...], sb_ref[...])
        dz_ref[:, D_IN_PAD - HEAD_PAD:D_IN_PAD] = jnp.where(rope_lane, dk, 0.0).astype(BF16)

    xcp, xcn = _halo_specs(ts, s, cw, 0)
    bgp, bgn = _halo_specs(ts, s, cw, 1)
    cgp, cgn = _halo_specs(ts, s, cw, 2)
    dycp, dycn = _halo_specs(ts, s, cw, 0)
    tab = pl.BlockSpec((ts, HEAD_PAD), lambda i: (i, 0))

    def vec(width):
        return pl.BlockSpec((1, width), lambda i: (0, 0))

    outs = pl.pallas_call(
        body, name="mix_bwd", grid=(n,),
        in_specs=[pl.BlockSpec((ts, D_IN_PAD), lambda i: (i, 0)), xcp, xcn, bgp, bgn, cgp, cgn,
                  pl.BlockSpec((ts, cw), lambda i: (i, 0)), dycp, dycn,
                  pl.BlockSpec((ts, Q_LORA), lambda i: (i, 0)), pl.BlockSpec((ts, KV_LORA), lambda i: (i, 0)), tab,
                  pl.BlockSpec((8, cw), lambda i: (0, 0)), vec(Q_LORA), vec(KV_LORA), tab, tab, tab],
        out_specs=[pl.BlockSpec((ts, D_IN_PAD), lambda i: (i, 0)), vec(cw), vec(cw), vec(cw), vec(Q_LORA),
                   vec(KV_LORA)],
        out_shape=[jax.ShapeDtypeStruct((s, D_IN_PAD), BF16)] + [jax.ShapeDtypeStruct((1, cw), F32)] * 3
        + [jax.ShapeDtypeStruct((1, Q_LORA), F32), jax.ShapeDtypeStruct((1, KV_LORA), F32)],
        compiler_params=_params(("arbitrary",)),
    )(z, z, z, z, z, z, z, dyc, dyc, dyc, dqn, dkvn, dkr, conv_w8, gq, gkv, cos, sin_a, sin_b)
    dz, dw0, dw1, dw2, dgq, dgkv = outs
    return dz, jnp.concatenate([dw0, dw1, dw2], axis=0), dgq, dgkv


def _qkv_proj(qn, kvn, kr, w_uq_p, w_kv_p, cos, sin_a, sin_b, *, ts=512):
    s = qn.shape[0]

    def body(qn_ref, kvn_ref, kr_ref, wq_ref, wkv_ref, cos_ref, sa_ref, sb_ref, q_ref, k_ref, v_ref, qt_ref):
        cos_v, sa, sb = cos_ref[...], sa_ref[...], sb_ref[...]
        q = jnp.dot(qn_ref[...], wq_ref[...], preferred_element_type=F32)
        kv = jnp.dot(kvn_ref[...], wkv_ref[...], preferred_element_type=F32)
        kr_v = kr_ref[...]
        lane = lax.broadcasted_iota(jnp.int32, (1, HEAD_PAD), 1)
        ones_lane = (lane == ONES_LANE).astype(F32)
        for h in range(N_HEADS):
            blk = slice(h * HEAD_PAD, (h + 1) * HEAD_PAD)
            q_h = _rope(q[:, blk], cos_v, sa, sb) * SM_SCALE
            q_ref[:, blk] = q_h.astype(BF16)
            qt_ref[blk, :] = q_h.T.astype(BF16)
            k_ref[:, blk] = (kv[:, blk] + kr_v).astype(BF16)
            v_ref[:, blk] = (kv[:, D_ATT + h * HEAD_PAD:D_ATT + (h + 1) * HEAD_PAD] + ones_lane).astype(BF16)

    tab = pl.BlockSpec((ts, HEAD_PAD), lambda i: (i, 0))
    wide = pl.BlockSpec((ts, D_ATT), lambda i: (i, 0))
    return pl.pallas_call(
        body, name="qkv_proj", grid=(s // ts,),
        in_specs=[pl.BlockSpec((ts, Q_LORA), lambda i: (i, 0)), pl.BlockSpec((ts, KV_LORA), lambda i: (i, 0)), tab,
                  pl.BlockSpec((Q_LORA, D_ATT), lambda i: (0, 0)), pl.BlockSpec((KV_LORA, 2 * D_ATT), lambda i: (0, 0)),
                  tab, tab, tab],
        out_specs=[wide, wide, wide, pl.BlockSpec((D_ATT, ts), lambda i: (0, i))],
        out_shape=[jax.ShapeDtypeStruct((s, D_ATT), BF16)] * 3 + [jax.ShapeDtypeStruct((D_ATT, s), BF16)],
        compiler_params=_params(("parallel",)),
    )(qn, kvn, kr, w_uq_p, w_kv_p, cos, sin_a, sin_b)


def _qk_bwd(dq, dk, cos, sin_a, sin_b, *, ts=256):
    s = dq.shape[0]

    def body(dq_ref, dk_ref, cos_ref, sa_ref, sb_ref, dqp_ref, dkr_ref):
        cos_v, sa, sb = cos_ref[...], sa_ref[...], sb_ref[...]
        tot = jnp.zeros((ts, HEAD_PAD), F32)
        for h in range(N_HEADS):
            blk = slice(h * HEAD_PAD, (h + 1) * HEAD_PAD)
            dqp_ref[:, blk] = _rope_bwd(dq_ref[:, blk], cos_v, sa, sb).astype(BF16)
            tot = tot + dk_ref[:, blk]
        dkr_ref[...] = tot

    tab = pl.BlockSpec((ts, HEAD_PAD), lambda i: (i, 0))
    wide = pl.BlockSpec((ts, D_ATT), lambda i: (i, 0))
    return pl.pallas_call(
        body, name="qk_bwd", grid=(s // ts,),
        in_specs=[wide, wide, tab, tab, tab], out_specs=[wide, tab],
        out_shape=[jax.ShapeDtypeStruct((s, D_ATT), BF16), jax.ShapeDtypeStruct((s, HEAD_PAD), F32)],
        compiler_params=_params(("parallel",)),
    )(dq, dk, cos, sin_a, sin_b)


_NT = (((1,), (1,)), ((), ()))
_TN = (((0,), (0,)), ((), ()))


def _flash_fwd(q, k, v, *, tq=1024, tk=1024, per_trip=8):
    s = q.shape[0]
    tq, tk = min(tq, s), min(tk, s)
    nk = s // tk
    per_trip = min(per_trip, nk)
    assert nk % per_trip == 0

    def body(q_ref, k_ref, v_ref, o_ref, lse_ref):
        qv = q_ref[...]

        def step(j, carry):
            m, acc = carry
            rows = pl.ds(pl.multiple_of(j * tk, tk), tk)
            sc = lax.dot_general(qv, k_ref[rows, :], _NT, preferred_element_type=F32)
            m_new = jnp.maximum(m, jnp.max(sc, axis=1, keepdims=True))
            p = jnp.exp(sc - m_new).astype(BF16)
            acc = jnp.exp(m - m_new) * acc + jnp.dot(p, v_ref[rows, :], preferred_element_type=F32)
            return m_new, acc

        def trip(t, carry):
            for c in range(per_trip):
                carry = step(per_trip * t + c, carry)
            return carry

        init = (jnp.full((tq, 1), -jnp.inf, F32), jnp.zeros((tq, HEAD_PAD), F32))
        m, acc = lax.fori_loop(0, nk // per_trip, trip, init)
        l = acc[:, ONES_LANE:ONES_LANE + 1]
        o_ref[...] = (acc / l).astype(BF16)
        lse_ref[...] = m + jnp.log(l)

    head = pl.BlockSpec((s, HEAD_PAD), lambda h, i: (0, h))
    return pl.pallas_call(
        body, name="flash_fwd", grid=(N_HEADS, s // tq),
        in_specs=[pl.BlockSpec((tq, HEAD_PAD), lambda h, i: (i, h)), head, head],
        out_specs=[pl.BlockSpec((tq, HEAD_PAD), lambda h, i: (i, h)),
                   pl.BlockSpec((None, tq, 1), lambda h, i: (h, i, 0))],
        out_shape=[jax.ShapeDtypeStruct((s, D_ATT), BF16), jax.ShapeDtypeStruct((N_HEADS, s, 1), F32)],
        compiler_params=_params(("parallel", "parallel")),
    )(q, k, v)


def _attn_delta(do, o, *, ts=512):
    s = do.shape[0]

    def body(do_ref, o_ref, dl_ref):
        for h in range(N_HEADS):
            blk = slice(h * HEAD_PAD, (h + 1) * HEAD_PAD)
            dl_ref[h] = jnp.sum(do_ref[:, blk].astype(F32) * o_ref[:, blk].astype(F32), axis=1, keepdims=True)

    wide = pl.BlockSpec((ts, D_ATT), lambda i: (i, 0))
    return pl.pallas_call(
        body, name="attn_delta", grid=(s // ts,), in_specs=[wide, wide],
        out_specs=pl.BlockSpec((N_HEADS, ts, 1), lambda i: (0, i, 0)),
        out_shape=jax.ShapeDtypeStruct((N_HEADS, s, 1), F32),
        compiler_params=_params(("parallel",)),
    )(do, o)


def _flash_bwd(q, qt, k, v, do, dot, lse, delta, after, *, tq=1024, tk=512, per_trip=8):
    s = q.shape[0]
    tq, tk = min(tq, s), min(tk, s)
    nq = s // tq
    per_trip = min(per_trip, nq)
    assert nq % per_trip == 0

    def body(q_ref, qt_ref, do_ref, dot_ref, lse_ref, dl_ref, k_ref, v_ref, after_ref, dq_ref, dk_ref, dv_ref):
        j = pl.program_id(1)

        @pl.when(j == 0)
        def _():
            dq_ref[...] = jnp.zeros_like(dq_ref)

        kv, vv = k_ref[...], v_ref[...]

        def chunk(i, dk_t, dv_t):
            at = pl.multiple_of(i * tq, tq)
            rows = pl.ds(at, tq)
            sc = lax.dot_general(q_ref[rows, :], kv, _NT, preferred_element_type=F32)
            p = jnp.exp(sc - lse_ref[rows, :])
            dp = lax.dot_general(do_ref[rows, :], vv, _NT, preferred_element_type=F32)
            ds = (p * (dp - dl_ref[rows, :])).astype(BF16)
            dv_t = dv_t + jnp.dot(dot_ref[:, rows], p.astype(BF16), preferred_element_type=F32)
            dk_t = dk_t + jnp.dot(qt_ref[:, rows], ds, preferred_element_type=F32)
            dq_ref[rows, :] += jnp.dot(ds, kv, preferred_element_type=F32)
            return dk_t, dv_t

        def step(i, carry):
            for c in range(per_trip):
                carry = chunk(per_trip * i + c, *carry)
            return carry

        zero = jnp.zeros((HEAD_PAD, tk), F32)
        dk_t, dv_t = lax.fori_loop(0, nq // per_trip, step, (zero, zero))
        dk_ref[...] = dk_t.T
        dv_ref[...] = dv_t.T

        @pl.when(j == pl.num_programs(1) - 1)
        def _():
            dq_ref[...] *= SM_SCALE

    head = pl.BlockSpec((s, HEAD_PAD), lambda h, j: (0, h))
    head_t = pl.BlockSpec((HEAD_PAD, s), lambda h, j: (h, 0))
    stat = pl.BlockSpec((None, s, 1), lambda h, j: (h, 0, 0))
    blk = pl.BlockSpec((tk, HEAD_PAD), lambda h, j: (j, h))
    return pl.pallas_call(
        body, name="flash_bwd", grid=(N_HEADS, s // tk),
        in_specs=[head, head_t, head, head_t, stat, stat, blk, blk, ANY],
        out_specs=[head, blk, blk],
        out_shape=[jax.ShapeDtypeStruct((s, D_ATT), F32)] * 3,
        compiler_params=_params(("parallel", "arbitrary")),
    )(q, qt, do, dot, lse, delta, k, v, after)


FFN_TC = 256


FFN_HALO_BF16 = 16
FFN_HALO_F32 = 8


def _row_halo_specs(ts, s, halo, width):
    nb = ts // halo
    last = s // halo - 1
    prev = pl.BlockSpec((halo, width), lambda i, j: (jnp.maximum(i * nb - 1, 0), 0))
    nxt = pl.BlockSpec((halo, width), lambda i, j: (jnp.minimum((i + 1) * nb, last), 0))
    return prev, nxt


def _ext_rows(prev, main, nxt, first, last):
    return jnp.concatenate([jnp.where(first, jnp.zeros_like(prev), prev), main,
                            jnp.where(last, jnp.zeros_like(nxt), nxt)], axis=0)


def _ext_conv(a, w):
    a_dn = pltpu.roll(a, 1, 0)
    a_up = pltpu.roll(a, a.shape[0] - 1, 0)
    return a_dn * w[0:1, :] + a * w[1:2, :] + a_up * w[2:3, :], a_dn, a_up


def _ffn_fwd(hf, w_up, w, b, *, ts=512, tc=FFN_TC):
    s = hf.shape[0]
    n, nj, halo = s // ts, D_FF // tc, FFN_HALO_BF16

    def body(h_ref, hp_ref, hn_ref, wg_ref, wu_ref, cw_ref, cb_ref, a_ref, act_ref):
        i = pl.program_id(0)
        ext = _ext_rows(hp_ref[...], h_ref[...], hn_ref[...], i == 0, i == n - 1)
        gate_up = []
        for half, w_ref in enumerate((wg_ref, wu_ref)):
            a_ext = jnp.dot(ext, w_ref[...], preferred_element_type=F32)
            a_ref[half] = a_ext[halo:halo + ts]
            gate_up.append(_ext_conv(a_ext, cw_ref[half])[0][halo:halo + ts] + cb_ref[half])
        g, u = gate_up
        act_ref[...] = (g * _sigmoid(g) * u).astype(BF16)

    prev, nxt = _row_halo_specs(ts, s, halo, D_MODEL)
    return pl.pallas_call(
        body, name="ffn_fwd", grid=(n, nj),
        in_specs=[pl.BlockSpec((ts, D_MODEL), lambda i, j: (i, 0)), prev, nxt,
                  pl.BlockSpec((D_MODEL, tc), lambda i, j: (0, j)), pl.BlockSpec((D_MODEL, tc), lambda i, j: (0, j + nj)),
                  pl.BlockSpec((2, 8, tc), lambda i, j: (0, 0, j)), pl.BlockSpec((2, 1, tc), lambda i, j: (0, 0, j))],
        out_specs=[pl.BlockSpec((2, ts, tc), lambda i, j: (0, i, j)), pl.BlockSpec((ts, tc), lambda i, j: (i, j))],
        out_shape=[jax.ShapeDtypeStruct((2, s, D_FF), F32), jax.ShapeDtypeStruct((s, D_FF), BF16)],
        compiler_params=_params(("parallel", "parallel")),
    )(hf, hf, hf, w_up, w_up, w, b)


def _ffn_bwd(dx2, w_down, a_pre, w, b, *, ts=512, tc=FFN_TC):
    s = dx2.shape[0]
    n, nj, halo = s // ts, D_FF // tc, FFN_HALO_F32
    main = slice(halo, halo + ts)

    def body(dx_ref, dxp_ref, dxn_ref, wd_ref, a_ref, ap_ref, an_ref, cw_ref, cb_ref, o_ref, dw_ref, db_ref):
        i, j = pl.program_id(0), pl.program_id(1)
        first, last = i == 0, i == n - 1

        @pl.when(first & (j == 0))
        def _():
            dw_ref[...] = jnp.zeros_like(dw_ref)
            db_ref[...] = jnp.zeros_like(db_ref)

        dx_ext = _ext_rows(dxp_ref[...], dx_ref[...], dxn_ref[...], first, last).astype(BF16)
        dact = lax.dot_general(dx_ext, wd_ref[...], _NT, preferred_element_type=F32)
        halves = []
        for half in range(2):
            a_ext = _ext_rows(ap_ref[half], a_ref[half], an_ref[half], first, last)
            conv, a_dn, a_up = _ext_conv(a_ext, cw_ref[half])
            halves.append((conv + cb_ref[half], a_dn, a_ext, a_up))
        g, u = halves[0][0], halves[1][0]
        sg = _sigmoid(g)
        grads = (dact * u * (sg * (1.0 + g * (1.0 - sg))), dact * (g * sg))
        for half in range(2):
            d = grads[half]
            _, a_dn, a_ext, a_up = halves[half]
            wv = cw_ref[half]
            d_pre = pltpu.roll(d, d.shape[0] - 1, 0) * wv[0:1, :] + d * wv[1:2, :] + pltpu.roll(d, 1, 0) * wv[2:3, :]
            o_ref[half] = d_pre[main].astype(BF16)
            dm = d[main]
            dw_ref[j, half, 0:1, :] += jnp.sum(dm * a_dn[main], axis=0, keepdims=True)
            dw_ref[j, half, 1:2, :] += jnp.sum(dm * a_ext[main], axis=0, keepdims=True)
            dw_ref[j, half, 2:3, :] += jnp.sum(dm * a_up[main], axis=0, keepdims=True)
            db_ref[j, half] += jnp.sum(dm, axis=0, keepdims=True)

    dxp, dxn = _row_halo_specs(ts, s, halo, D_MODEL)
    nb, lastb = ts // halo, s // halo - 1
    a_main = pl.BlockSpec((2, ts, tc), lambda i, j: (0, i, j))
    a_prev = pl.BlockSpec((2, halo, tc), lambda i, j: (0, jnp.maximum(i * nb - 1, 0), j))
    a_next = pl.BlockSpec((2, halo, tc), lambda i, j: (0, jnp.minimum((i + 1) * nb, lastb), j))
    da_pre, dw, db = pl.pallas_call(
        body, name="ffn_bwd", grid=(n, nj),
        in_specs=[pl.BlockSpec((ts, D_MODEL), lambda i, j: (i, 0)), dxp, dxn,
                  pl.BlockSpec((tc, D_MODEL), lambda i, j: (j, 0)), a_main, a_prev, a_next,
                  pl.BlockSpec((2, 8, tc), lambda i, j: (0, 0, j)), pl.BlockSpec((2, 1, tc), lambda i, j: (0, 0, j))],
        out_specs=[a_main, pl.BlockSpec((nj, 2, 8, tc), lambda i, j: (0, 0, 0, 0)),
                   pl.BlockSpec((nj, 2, 1, tc), lambda i, j: (0, 0, 0, 0))],
        out_shape=[jax.ShapeDtypeStruct((2, s, D_FF), BF16), jax.ShapeDtypeStruct((nj, 2, 8, tc), F32),
                   jax.ShapeDtypeStruct((nj, 2, 1, tc), F32)],
        compiler_params=_params(("arbitrary", "arbitrary")),
    )(dx2, dx2, dx2, w_down, a_pre, a_pre, a_pre, w, b)
    return (da_pre, dw.transpose(1, 2, 0, 3).reshape(2, 8, D_FF), db.transpose(1, 2, 0, 3).reshape(2, 1, D_FF))


def _ple_final(x2, n3, p, target, gf, w_pg, w_pp, *, ts=256):
    s, d = x2.shape
    dp = p.shape[1]

    def body(x2_ref, n3_ref, p_ref, t_ref, gf_ref, wg_ref, wp_ref, loss_ref, dx3_ref, dgl_ref, dpp_ref, dgf_ref):
        @pl.when(pl.program_id(0) == 0)
        def _():
            loss_ref[...] = jnp.zeros_like(loss_ref)
            dgf_ref[...] = jnp.zeros_like(dgf_ref)

        gate = _sigmoid(jnp.dot(n3_ref[...], wg_ref[...], preferred_element_type=F32))
        ppv = jnp.dot(p_ref[...].astype(BF16), wp_ref[...], preferred_element_type=F32)
        x3 = x2_ref[...] + gate * ppv
        gfv = gf_ref[...]
        err = x3 * _rms_scale(x3) * gfv - t_ref[...]
        loss_ref[...] += 0.5 * jnp.sum(jnp.mean(err * err, axis=-1, keepdims=True), axis=0, keepdims=True)
        dx3, dgf_rows = _rms_bwd_rows(x3, gfv, err * (1.0 / d))
        dgf_ref[...] += jnp.sum(dgf_rows, axis=0, keepdims=True)
        dx3_ref[...] = dx3
        dgl_ref[...] = (dx3 * ppv * gate * (1.0 - gate)).astype(BF16)
        dpp_ref[...] = (dx3 * gate).astype(BF16)

    row = pl.BlockSpec((ts, d), lambda i: (i, 0))
    vec = pl.BlockSpec((1, d), lambda i: (0, 0))
    return pl.pallas_call(
        body, name="ple_final", grid=(s // ts,),
        in_specs=[row, row, pl.BlockSpec((ts, dp), lambda i: (i, 0)), row, vec,
                  pl.BlockSpec((d, d), lambda i: (0, 0)), pl.BlockSpec((dp, d), lambda i: (0, 0))],
        out_specs=[pl.BlockSpec((1, 128), lambda i: (0, 0)), row, row, row, vec],
        out_shape=[jax.ShapeDtypeStruct((1, 128), F32), jax.ShapeDtypeStruct((s, d), F32),
                   jax.ShapeDtypeStruct((s, d), BF16), jax.ShapeDtypeStruct((s, d), BF16),
                   jax.ShapeDtypeStruct((1, d), F32)],
        compiler_params=_params(("arbitrary",)),
    )(x2, n3, p, target, gf, w_pg, w_pp)


def _row_tile(rows, cols, n_arrays, budget=12 << 20):
    best = None
    for t in range(8, rows + 1, 8):
        if rows % t == 0 and t * cols * 4 * n_arrays <= budget:
            best = t
    return rows if best is None else best


def _sum_slots(a, *, name):
    g, r, c = a.shape
    tr = _row_tile(r, c, g + 1)

    def body(*refs):
        tot = refs[0][...]
        for ref in refs[1:g]:
            tot = tot + ref[...]
        refs[g][...] = tot

    specs = [pl.BlockSpec((None, tr, c), functools.partial(lambda i, slot: (slot, i, 0), slot=k)) for k in range(g)]
    return pl.pallas_call(
        body, name=name, grid=(r // tr,), in_specs=specs, out_specs=pl.BlockSpec((tr, c), lambda i: (i, 0)),
        out_shape=jax.ShapeDtypeStruct((r, c), a.dtype), compiler_params=_params(("parallel",)),
    )(*([a] * g))


def _adamw(w, g, m, v, *, name):
    r, c = w.shape
    tr = _row_tile(r, c, 7)

    def body(w_ref, g_ref, m_ref, v_ref, d_ref, mo_ref, vo_ref):
        gv = g_ref[...]
        mn = ADAM_B1 * m_ref[...] + (1.0 - ADAM_B1) * gv
        vn = ADAM_B2 * v_ref[...] + (1.0 - ADAM_B2) * (gv * gv)
        m_hat = mn / (1.0 - ADAM_B1 ** ADAM_STEP)
        v_hat = vn / (1.0 - ADAM_B2 ** ADAM_STEP)
        d_ref[...] = -ADAM_LR * (m_hat / (jnp.sqrt(v_hat) + ADAM_EPS) + ADAM_WD * w_ref[...])
        mo_ref[...] = mn
        vo_ref[...] = vn

    blk = pl.BlockSpec((tr, c), lambda i: (i, 0))
    return pl.pallas_call(
        body, name=name, grid=(r // tr,), in_specs=[blk] * 4, out_specs=[blk] * 3,
        out_shape=[jax.ShapeDtypeStruct((r, c), F32)] * 3, compiler_params=_params(("parallel",)),
    )(w, g, m, v)


def _position():
    x, y, c = lax.axis_index("x"), lax.axis_index("y"), lax.axis_index("c")
    return x, y, c


def _other_chips(x, y):
    return [(1 - x, y), (x, 1 - y), (1 - x, 1 - y)]


def _stage_in(srcs, stage, sems):
    cps = [pltpu.make_async_copy(src, stage[a], sems.at[a]) for a, src in enumerate(srcs)]
    for cp in cps:
        cp.start()
    return cps


def _stage_out(staged, stage, dsts, sems):
    cps = []
    for a, dst in enumerate(dsts):
        staged[a].wait()
        cp = pltpu.make_async_copy(stage[a], dst, sems.at[a])
        cp.start()
        cps.append(cp)
    return cps


def _gather_chips(shards):
    n = len(shards)

    def body(*refs):
        ins, outs, stage = refs[:n], refs[n:2 * n], refs[2 * n:3 * n]
        send_sems, recv_sems, in_sems, out_sems = refs[3 * n:]
        x, y, c = _position()
        me = 2 * x + y
        chips = _other_chips(x, y)
        remote = []
        staged = _stage_in(ins, stage, in_sems)
        for a in range(n):
            for k, (px, py) in enumerate(chips):
                rc = pltpu.make_async_remote_copy(
                    src_ref=ins[a], dst_ref=outs[a].at[me], send_sem=send_sems.at[3 * a + k],
                    recv_sem=recv_sems.at[3 * a + k], device_id=(px, py, c), device_id_type=MESH)
                rc.start()
                remote.append(rc)
        local = _stage_out(staged, stage, [o.at[me] for o in outs], out_sems)
        for a in range(n):
            for k, (px, py) in enumerate(chips):
                pltpu.make_async_remote_copy(
                    src_ref=ins[a], dst_ref=outs[a].at[2 * px + py], send_sem=send_sems.at[3 * a + k],
                    recv_sem=recv_sems.at[3 * a + k], device_id=(px, py, c), device_id_type=MESH).wait_recv()
        for rc in remote:
            rc.wait_send()
        for cp in local:
            cp.wait()

    return pl.pallas_call(
        body, name="gather_chips", in_specs=[ANY] * n, out_specs=[ANY] * n,
        out_shape=[jax.ShapeDtypeStruct((N_CHIPS,) + s.shape, s.dtype) for s in shards],
        scratch_shapes=[pltpu.VMEM(s.shape, s.dtype) for s in shards]
        + [pltpu.SemaphoreType.DMA((3 * n,)), pltpu.SemaphoreType.DMA((3 * n,)),
           pltpu.SemaphoreType.DMA((n,)), pltpu.SemaphoreType.DMA((n,))],
        compiler_params=pltpu.CompilerParams(has_side_effects=True),
    )(*shards)


def _send_other_halves(grads, *, tag):
    n = len(grads)

    def body(*refs):
        ins, sib = refs[:n], refs[n:2 * n]
        send_sems, recv_sems = refs[2 * n:]
        x, y, c = _position()
        remote = []
        for a in range(n):
            half = ins[a].shape[1] // 2
            give = ins[a].at[:, pl.ds(pl.multiple_of((1 - c) * half, 8), half), :]
            rc = pltpu.make_async_remote_copy(
                src_ref=give, dst_ref=sib[a], send_sem=send_sems.at[a], recv_sem=recv_sems.at[a],
                device_id=(x, y, 1 - c), device_id_type=MESH)
            rc.start()
            remote.append(rc)
        for rc in remote:
            rc.wait_recv()
        for rc in remote:
            rc.wait_send()

    return pl.pallas_call(
        body, name="send_other_halves_" + tag, in_specs=[ANY] * n, out_specs=[ANY] * n,
        out_shape=[jax.ShapeDtypeStruct((g.shape[0], g.shape[1] // 2, g.shape[2]), g.dtype) for g in grads],
        scratch_shapes=[pltpu.SemaphoreType.DMA((n,)), pltpu.SemaphoreType.DMA((n,))],
        compiler_params=pltpu.CompilerParams(has_side_effects=True),
    )(*grads)


def _add_own_half(g4, sib, core, *, name):
    g, a2, c = sib.shape
    tr = _row_tile(a2, c, 4)

    def body(core_ref, a_ref, b_ref, o_ref, o16_ref):
        tot = a_ref[...] + b_ref[...]
        o_ref[...] = tot
        o16_ref[...] = tot.astype(BF16)

    blk = pl.BlockSpec((None, tr, c), lambda i, j, core_ref: (i, j, 0))
    return pl.pallas_call(
        body, name=name,
        grid_spec=pltpu.PrefetchScalarGridSpec(
            num_scalar_prefetch=1, grid=(g, a2 // tr),
            in_specs=[pl.BlockSpec((None, None, tr, c), lambda i, j, core_ref: (i, core_ref[0], j, 0)), blk],
            out_specs=[blk, blk]),
        out_shape=[jax.ShapeDtypeStruct(sib.shape, F32), jax.ShapeDtypeStruct(sib.shape, BF16)],
        compiler_params=_params(("parallel", "parallel")),
    )(core, g4.reshape(g, 2, a2, c), sib)


def _sum_chips(landed, own, chip, *, name):
    g, r, c = landed.shape
    tr = _row_tile(r, c, 5)

    def body(chip_ref, *refs):
        me = chip_ref[0]
        own_v = refs[g][...]
        tot = None
        for slot in range(g):
            term = jnp.where(me == slot, own_v, refs[slot][...].astype(F32))
            tot = term if tot is None else tot + term
        refs[g + 1][...] = tot

    def landed_spec(slot):
        return pl.BlockSpec((None, tr, c),
                            lambda i, chip_ref: (jnp.where(chip_ref[0] == slot, (slot + 1) % g, slot), i, 0))

    return pl.pallas_call(
        body, name=name,
        grid_spec=pltpu.PrefetchScalarGridSpec(
            num_scalar_prefetch=1, grid=(r // tr,),
            in_specs=[landed_spec(k) for k in range(g)]
            + [pl.BlockSpec((None, tr, c), lambda i, chip_ref: (chip_ref[0], i, 0))],
            out_specs=pl.BlockSpec((tr, c), lambda i, chip_ref: (i, 0))),
        out_shape=jax.ShapeDtypeStruct((r, c), F32), compiler_params=_params(("parallel",)),
    )(chip, *([landed] * g), own)


def _join_halves(halves):
    n = len(halves)

    def body(*refs):
        ins, outs, stage = refs[:n], refs[n:2 * n], refs[2 * n:3 * n]
        send_sems, recv_sems, in_sems, out_sems = refs[3 * n:]
        x, y, c = _position()
        remote = []
        staged = _stage_in(ins, stage, in_sems)
        for a in range(n):
            rc = pltpu.make_async_remote_copy(
                src_ref=ins[a], dst_ref=outs[a].at[c], send_sem=send_sems.at[a], recv_sem=recv_sems.at[a],
                device_id=(x, y, 1 - c), device_id_type=MESH)
            rc.start()
            remote.append(rc)
        local = _stage_out(staged, stage, [o.at[c] for o in outs], out_sems)
        for a in range(n):
            pltpu.make_async_remote_copy(
                src_ref=ins[a], dst_ref=outs[a].at[1 - c], send_sem=send_sems.at[a], recv_sem=recv_sems.at[a],
                device_id=(x, y, 1 - c), device_id_type=MESH).wait_recv()
        for rc in remote:
            rc.wait_send()
        for cp in local:
            cp.wait()

    return pl.pallas_call(
        body, name="join_halves", in_specs=[ANY] * n, out_specs=[ANY] * n,
        out_shape=[jax.ShapeDtypeStruct((2,) + h.shape, h.dtype) for h in halves],
        scratch_shapes=[pltpu.VMEM(h.shape, h.dtype) for h in halves]
        + [pltpu.SemaphoreType.DMA((n,)), pltpu.SemaphoreType.DMA((n,)), pltpu.SemaphoreType.DMA((n,)),
           pltpu.SemaphoreType.DMA((n,))],
        compiler_params=pltpu.CompilerParams(has_side_effects=True),
    )(*halves)


_HBM = pl.BlockSpec(memory_space=pltpu.HBM)
_SEM = pl.BlockSpec(memory_space=pltpu.SEMAPHORE)


def _chip_copies(srcs, lands, send_sems, recv_sems, scatter):
    x, y, c = _position()
    me = 2 * x + y
    outgoing, incoming = [], []
    for a, (src, land) in enumerate(zip(srcs, lands)):
        for k, (px, py) in enumerate(_other_chips(x, y)):
            peer = 2 * px + py
            sems = dict(send_sem=send_sems.at[3 * a + k], recv_sem=recv_sems.at[3 * a + k], device_id=(px, py, c),
                        device_id_type=MESH)
            outgoing.append(pltpu.make_async_remote_copy(
                src_ref=src.at[peer] if scatter else src, dst_ref=land.at[me], **sems))
            incoming.append(pltpu.make_async_remote_copy(
                src_ref=src.at[me] if scatter else src, dst_ref=land.at[peer], **sems))
    return outgoing, incoming


def _chips_start(srcs, *, scatter, name):
    n = len(srcs)
    lands = [lax.empty(a.shape if scatter else (N_CHIPS,) + a.shape, a.dtype) for a in srcs]

    def body(*refs):
        ins, send_sems, recv_sems, token = refs[:2 * n], refs[2 * n], refs[2 * n + 1], refs[-1]
        outgoing, _ = _chip_copies(ins[:n], ins[n:], send_sems, recv_sems, scatter)
        for cp in outgoing:
            cp.start()
        token[...] = jnp.zeros_like(token)

    bufs = list(srcs) + lands
    res = pl.pallas_call(
        body, name=name, in_specs=[_HBM] * (2 * n),
        out_specs=(_SEM, _SEM, *[_HBM] * (2 * n), pl.BlockSpec(memory_space=pltpu.VMEM)),
        out_shape=(pltpu.SemaphoreType.DMA((3 * n,)), pltpu.SemaphoreType.DMA((3 * n,)),
                   *[pltpu.HBM(a.shape, a.dtype) for a in bufs], jax.ShapeDtypeStruct((8, 128), F32)),
        input_output_aliases={i: 2 + i for i in range(2 * n)},
        compiler_params=pltpu.CompilerParams(has_side_effects=pltpu.SideEffectType.DATAFLOW_SIDE_EFFECTING),
    )(*[pltpu.with_memory_space_constraint(a, pltpu.HBM) for a in bufs])
    return res[0], res[1], list(res[2:2 + n]), list(res[2 + n:2 + 2 * n]), res[-1]


def _chips_wait(handle, after, *, scatter, name):
    send_sems, recv_sems, srcs, lands, _ = handle
    n = len(srcs)

    def body(*refs):
        ins, send_ref, recv_ref = refs[:2 * n], refs[2 * n], refs[2 * n + 1]
        outgoing, incoming = _chip_copies(ins[:n], ins[n:], send_ref, recv_ref, scatter)
        for cp in outgoing:
            cp.wait_send()
        for cp in incoming:
            cp.wait_recv()

    bufs = list(srcs) + list(lands)
    res = pl.pallas_call(
        body, name=name, in_specs=[_HBM] * (2 * n) + [_SEM, _SEM, ANY], out_specs=tuple([_HBM] * (2 * n)),
        out_shape=tuple(pltpu.HBM(a.shape, a.dtype) for a in bufs),
        input_output_aliases={i: i for i in range(2 * n)},
        compiler_params=pltpu.CompilerParams(has_side_effects=pltpu.SideEffectType.DATAFLOW_SIDE_EFFECTING),
    )(*bufs, send_sems, recv_sems, after)
    return list(res[:n]), list(res[n:])


def _gather_all(buf):
    def body(in_ref, out_ref, send_sems, recv_sems, local_sem):
        x, y, c = _position()
        me = 4 * x + 2 * y + c
        peers = [(x, y, 1 - c)] + [(px, py, pc) for (px, py) in _other_chips(x, y) for pc in (c, 1 - c)]
        cp = pltpu.make_async_copy(in_ref, out_ref.at[me], local_sem)
        cp.start()
        remote = []
        for k, peer in enumerate(peers):
            rc = pltpu.make_async_remote_copy(
                src_ref=in_ref, dst_ref=out_ref.at[me], send_sem=send_sems.at[k], recv_sem=recv_sems.at[k],
                device_id=peer, device_id_type=MESH)
            rc.start()
            remote.append(rc)
        for k, (px, py, pc) in enumerate(peers):
            pltpu.make_async_remote_copy(
                src_ref=in_ref, dst_ref=out_ref.at[4 * px + 2 * py + pc], send_sem=send_sems.at[k],
                recv_sem=recv_sems.at[k], device_id=(px, py, pc), device_id_type=MESH).wait_recv()
        for rc in remote:
            rc.wait_send()
        cp.wait()

    return pl.pallas_call(
        body, name="gather_all", in_specs=[ANY], out_specs=ANY,
        out_shape=jax.ShapeDtypeStruct((N_DEV,) + buf.shape, buf.dtype),
        scratch_shapes=[pltpu.SemaphoreType.DMA((N_DEV - 1,)), pltpu.SemaphoreType.DMA((N_DEV - 1,)),
                        pltpu.SemaphoreType.DMA],
        compiler_params=pltpu.CompilerParams(has_side_effects=True),
    )(buf)


def _cols_from_shards(g4):
    _, k, n = g4.shape
    return g4.transpose(1, 0, 2).reshape(k, N_CHIPS * n)


def _cols_to_shards(w):
    k, n = w.shape
    return w.reshape(k, N_CHIPS, n // N_CHIPS).transpose(1, 0, 2)


def _pad_heads(w, width):
    k = w.shape[0]
    w3 = w.reshape(k, N_HEADS, width)
    return jnp.pad(w3, ((0, 0), (0, 0), (0, HEAD_PAD - width))).reshape(k, D_ATT)


def _unpad_heads(w, width):
    k = w.shape[0]
    return w.reshape(k, N_HEADS, HEAD_PAD)[:, :, :width]


def _rope_tables(s):
    pos = jnp.arange(s, dtype=F32)
    inv_freq = ROPE_THETA ** (-jnp.arange(0, QK_ROPE, 2, dtype=F32) / QK_ROPE)
    ang = pos[:, None] * inv_freq[None, :]
    cos_h, sin_h = jnp.cos(ang), jnp.sin(ang)
    half = QK_ROPE // 2
    z = jnp.zeros((s, half), F32)
    ones = jnp.ones((s, QK_NOPE), F32)
    tail = jnp.zeros((s, HEAD_PAD - QK_NOPE - QK_ROPE), F32)
    cos = jnp.concatenate([ones, cos_h, cos_h, tail + 1.0], axis=1)
    sin_a = jnp.concatenate([ones * 0.0, -sin_h, z, tail], axis=1)
    sin_b = jnp.concatenate([ones * 0.0, z, sin_h, tail], axis=1)
    return cos, sin_a, sin_b


def _local_step(x, p, target, wts, late_weights, reduce_early, reduce_last):
    s = x.shape[0]
    cos, sin_a, sin_b = _rope_tables(s)
    g1, gq, gkv, g2, g3, gf = (wts[k] for k in ("norm_mix_g", "q_norm_g", "kv_norm_g", "norm_ffn_g", "ple_norm_g",
                                                 "final_norm_g"))
    w_in_p, w_uq_p, w_kv_p = wts["w_in_p"], wts["w_uq_p"], wts["w_kv_p"]
    conv_w8, fconv_w, fconv_b = wts["conv_w8"], wts["ffn_conv_w"], wts["ffn_conv_b"]

    (h, z), _ = _mm_fused(x, w_in_p, name="mm_in", prologue=_pro_rms, vecs=[g1], epilogue=_epi_plain, row_outs=[F32])
    y_conv, qn, kvn, kr = _mix_pre(z, conv_w8, gq, gkv, cos, sin_a, sin_b)
    q, k, v, q_t = _qkv_proj(qn, kvn, kr, w_uq_p, w_kv_p, cos, sin_a, sin_b)
    o, lse = _flash_fwd(q, k, v)
    late = late_weights(lse)
    w_o_a, w_o_b, w_up, w_down = late["w_o_a"], late["w_o_b"], late["w_up"], late["w_down"]
    w_pg, w_pp = late["w_ple_gate"], late["w_ple_proj"]
    (x1, hf), _ = _mm_fused(o, w_o_b, second=(y_conv, w_o_a), name="mm_o", rows=[x], vecs=[g2],
                            epilogue=_epi_add_rms, row_outs=[F32, BF16])
    a_pre, act = _ffn_fwd(hf, w_up, fconv_w, fconv_b)
    (x2, n3), _ = _mm_fused(act, w_down, name="mm_down", rows=[x1], vecs=[g3], epilogue=_epi_add_rms,
                            row_outs=[F32, BF16])
    loss, dx3, dgl, dpp, d_gf = _ple_final(x2, n3, p, target, gf, w_pg, w_pp)

    grads, early = {"final_norm_g": d_gf}, {}
    early["w_ple_proj"] = _mm(p, dpp, ta=True, name="mm_d_wpp", tm=256, tn=1024, tk=2048)
    early["w_ple_gate"] = _mm(n3, dgl, ta=True, name="mm_d_wpg", tm=1024, tn=1024, tk=2048)
    (dx2,), (grads["ple_norm_g"],) = _mm_fused(dgl, w_pg, tb=True, name="mm_d_n3", rows=[x2, dx3], vecs=[g3],
                                               epilogue=_epi_rms_bwd, row_outs=[F32], n_vec_out=1)
    early["w_down"] = _mm(act, dx2, ta=True, name="mm_d_wdown", tm=1408, tn=1024, tk=2048)
    da_pre, grads["ffn_conv_w"], grads["ffn_conv_b"] = _ffn_bwd(dx2, w_down, a_pre, fconv_w, fconv_b)
    early["w_up"] = _mm(hf, da_pre, ta=True, b_split=True, name="mm_d_wup", tm=1024, tn=1408, tk=2048,
                       o_shards=True)
    (dx1,), (grads["norm_ffn_g"],) = _mm_fused(da_pre, w_up, tb=True, a_split=True, name="mm_d_hf", rows=[x1, dx2],
                                               vecs=[g2], epilogue=_epi_rms_bwd, row_outs=[F32], n_vec_out=1)
    d_wo_a = _mm(y_conv, dx1, ta=True, name="mm_d_wo_conv", tm=512, tn=1024, tk=2048)
    d_wo_b = _mm(o, dx1, ta=True, name="mm_d_wo_att", tm=1024, tn=1024, tk=2048)
    early["w_o"] = jnp.concatenate([d_wo_a, d_wo_b.reshape(N_HEADS, HEAD_PAD, D_MODEL)[:, :V_HEAD]
                                    .reshape(N_HEADS * V_HEAD, D_MODEL)], axis=0)
    token, finish = reduce_early(early)
    dyc = _mm(dx1, w_o_a, tb=True, name="mm_d_yconv", tm=512, tn=512, tk=1024)
    (do, do_t), _ = _mm_fused(dx1, w_o_b, tb=True, name="mm_d_o", epilogue=_epi_plain, row_outs=[BF16],
                              transposed_out=BF16)
    delta = _attn_delta(do, o)
    dq, dk, dv = _flash_bwd(q, q_t, k, v, do, do_t, lse, delta, token)
    reduced_early = finish(dq)
    dq_pre, dkr = _qk_bwd(dq, dk, cos, sin_a, sin_b)
    grads["w_uq_p"] = _mm(qn, dq_pre, ta=True, name="mm_d_wuq", tm=256, tn=1024, tk=2048)
    dqn = _mm(dq_pre, w_uq_p, tb=True, name="mm_d_qn", tm=512, tn=256, tk=1024)
    grads["w_k_p"] = _mm(kvn, dk, ta=True, name="mm_d_wk", tm=128, tn=1024, tk=2048)
    grads["w_v_p"] = _mm(kvn, dv, ta=True, name="mm_d_wv", tm=128, tn=1024, tk=2048)
    dkvn_k = _mm(dk, w_kv_p[:, :D_ATT], tb=True, name="mm_d_kvn_k", tm=512, tn=128, tk=1024)
    dkvn = _mm(dv, w_kv_p[:, D_ATT:], tb=True, add=dkvn_k, name="mm_d_kvn_v", tm=512, tn=128, tk=1024)
    dz, grads["conv_w"], grads["q_norm_g"], grads["kv_norm_g"] = _mix_bwd(
        z, dyc, dqn, dkvn, dkr, conv_w8, gq, gkv, cos, sin_a, sin_b)
    grads["w_in_p"] = _mm(h, dz, ta=True, name="mm_d_win", tm=1024, tn=1024, tk=2048)
    token, finish = reduce_last({n: grads.pop(n) for n in ("w_in_p", "w_uq_p", "w_k_p", "w_v_p")})
    (grad_x,), (grads["norm_mix_g"],) = _mm_fused(dz, w_in_p, tb=True, name="mm_d_h", rows=[x, dx1],
                                                  vecs=[g1 + token[0, 0]], epilogue=_epi_rms_bwd, row_outs=[F32],
                                                  n_vec_out=1)
    return loss[0, 0], grad_x, grads, reduced_early, finish(grad_x)


_EARLY_W = ("w_in", "w_uq", "w_ukv")
_LATE_W = ("w_o", "w_up", "w_down", "w_ple_gate", "w_ple_proj")
_BIG = _EARLY_W + _LATE_W
_COL_SHARDED = ("w_in", "w_uq", "w_ukv", "w_up", "w_ple_proj")
_SMALL = ("norm_mix_g", "conv_w", "q_norm_g", "kv_norm_g", "norm_ffn_g", "ffn_conv_w", "ffn_conv_b", "ple_norm_g",
          "final_norm_g")


def _full_from_slots(n, g4):
    return _cols_from_shards(g4) if n in _COL_SHARDED else g4.reshape(-1, g4.shape[2])


def _shard_major(n, g):
    if g.ndim == 3:
        return g
    return _cols_to_shards(g) if n in _COL_SHARDED else g.reshape(N_CHIPS, g.shape[0] // N_CHIPS, g.shape[1])


def _early_weights(w):
    shards = [w[n][0].astype(BF16) for n in _EARLY_W]
    shards.append(jnp.pad(w["conv_w"][0], ((0, 5), (0, 0))))
    shards.append(jnp.pad(w["ffn_conv_w"][0], ((0, 5), (0, 0))))
    got = _gather_chips(shards)
    full = {n: _full_from_slots(n, g4) for n, g4 in zip(_EARLY_W, got)}
    full["conv_w8"] = _cols_from_shards(got[len(_EARLY_W)])
    full["ffn_conv_w8"] = _cols_from_shards(got[len(_EARLY_W) + 1])
    return _layout_early(full, w)


def _layout_early(full, w):
    out = {n: w[n] for n in ("norm_mix_g", "q_norm_g", "kv_norm_g", "norm_ffn_g", "ple_norm_g")}
    out["final_norm_g"] = w["final_norm_g"][None, :]
    w_in = full["w_in"]
    zc = jnp.zeros((D_MODEL, QK_NOPE), BF16)
    zt = jnp.zeros((D_MODEL, HEAD_PAD - QK_NOPE - QK_ROPE), BF16)
    out["w_in_p"] = jnp.concatenate([w_in[:, :D_IN - QK_ROPE], zc, w_in[:, D_IN - QK_ROPE:], zt], axis=1)
    out["w_uq_p"] = _pad_heads(full["w_uq"], QK_NOPE + QK_ROPE)
    kv3 = full["w_ukv"].reshape(KV_LORA, N_HEADS, QK_NOPE + V_HEAD)
    out["w_kv_p"] = jnp.concatenate([_pad_heads(kv3[:, :, :QK_NOPE].reshape(KV_LORA, -1), QK_NOPE),
                                     _pad_heads(kv3[:, :, QK_NOPE:].reshape(KV_LORA, -1), V_HEAD)], axis=1)
    out["conv_w8"] = full["conv_w8"]
    fw = full["ffn_conv_w8"]
    out["ffn_conv_w"] = jnp.stack([fw[:, :D_FF], fw[:, D_FF:]])
    out["ffn_conv_b"] = w["ffn_conv_b"].reshape(2, 1, D_FF)
    return out


def _layout_late(full):
    w_o = full["w_o"]
    out = {"w_o_a": w_o[:CONV_WIDTH]}
    out["w_o_b"] = jnp.pad(w_o[CONV_WIDTH:].reshape(N_HEADS, V_HEAD, D_MODEL),
                           ((0, 0), (0, HEAD_PAD - V_HEAD), (0, 0))).reshape(D_ATT, D_MODEL)
    for n in ("w_up", "w_down", "w_ple_gate", "w_ple_proj"):
        out[n] = full[n]
    return out


def _true_matrices(g):
    out = {}
    wp = g["w_in_p"]
    out["w_in"] = jnp.concatenate([wp[:, :D_IN - QK_ROPE], wp[:, D_IN_PAD - HEAD_PAD + QK_NOPE:
                                                              D_IN_PAD - HEAD_PAD + QK_NOPE + QK_ROPE]], axis=1)
    out["w_uq"] = _unpad_heads(g["w_uq_p"], QK_NOPE + QK_ROPE).reshape(Q_LORA, -1)
    out["w_ukv"] = jnp.concatenate([_unpad_heads(g["w_k_p"], QK_NOPE), _unpad_heads(g["w_v_p"], V_HEAD)],
                                   axis=2).reshape(KV_LORA, -1)
    return out


def _true_vectors(g):
    out = {}
    out["conv_w"] = g["conv_w"]
    fw = g["ffn_conv_w"]
    out["ffn_conv_w"] = jnp.concatenate([fw[0, :3], fw[1, :3]], axis=1)
    out["ffn_conv_b"] = g["ffn_conv_b"].reshape(1, 2 * D_FF)
    for n in ("norm_mix_g", "q_norm_g", "kv_norm_g", "norm_ffn_g", "ple_norm_g", "final_norm_g"):
        out[n] = g[n]
    return out


def _chip_partials(names, g, core, *, tag):
    g4 = [_shard_major(n, g[n]) for n in names]
    sib = _send_other_halves(g4, tag=tag)
    return [_add_own_half(a, b, core, name="add_cores_" + n) for n, a, b in zip(names, g4, sib)]


_SMALL_SIZES = {"norm_mix_g": D_MODEL, "conv_w": 3 * CONV_WIDTH, "q_norm_g": Q_LORA, "kv_norm_g": KV_LORA,
                "norm_ffn_g": D_MODEL, "ffn_conv_w": 6 * D_FF, "ffn_conv_b": 2 * D_FF, "ple_norm_g": D_MODEL,
                "final_norm_g": D_MODEL}


def _pack(parts, rows):
    flat = jnp.concatenate([a.reshape(-1) for a in parts])
    return jnp.pad(flat, (0, rows * 128 - flat.shape[0])).reshape(rows, 128)


def _unpack(buf, sizes):
    flat = buf.reshape(-1)
    out, at = [], 0
    for n in sizes:
        out.append(flat[at:at + n])
        at += n
    return out


def _reduce_small(g, loss):
    sizes = [1] + [_SMALL_SIZES[n] for n in _SMALL]
    rows = -(-sum(sizes) // 1024) * 8
    slots = _gather_all(_pack([loss] + [g[n] for n in _SMALL], rows))
    parts = _unpack(_sum_slots(slots, name="sum_small"), sizes)
    return parts[0][0], dict(zip(_SMALL, parts[1:]))


def kernel(x, p, norm_mix_g, w_in, conv_w, q_norm_g, w_uq, kv_norm_g, w_ukv, w_o, norm_ffn_g, w_up, ffn_conv_w, ffn_conv_b, w_down, ple_norm_g, w_ple_gate, w_ple_proj, final_norm_g, loss_target, m_norm_mix_g, m_w_in, m_conv_w, m_q_norm_g, m_w_uq, m_kv_norm_g, m_w_ukv, m_w_o, m_norm_ffn_g, m_w_up, m_ffn_conv_w, m_ffn_conv_b, m_w_down, m_ple_norm_g, m_w_ple_gate, m_w_ple_proj, m_final_norm_g, v_norm_mix_g, v_w_in, v_conv_w, v_q_norm_g, v_w_uq, v_kv_norm_g, v_w_ukv, v_w_o, v_norm_ffn_g, v_w_up, v_ffn_conv_w, v_ffn_conv_b, v_w_down, v_ple_norm_g, v_w_ple_gate, v_w_ple_proj, v_final_norm_g):
    names = ["norm_mix_g", "w_in", "conv_w", "q_norm_g", "w_uq", "kv_norm_g", "w_ukv", "w_o", "norm_ffn_g", "w_up",
             "ffn_conv_w", "ffn_conv_b", "w_down", "ple_norm_g", "w_ple_gate", "w_ple_proj", "final_norm_g"]
    w = dict(zip(names, (norm_mix_g, w_in, conv_w, q_norm_g, w_uq, kv_norm_g, w_ukv, w_o, norm_ffn_g, w_up,
                         ffn_conv_w, ffn_conv_b, w_down, ple_norm_g, w_ple_gate, w_ple_proj, final_norm_g)))
    m = dict(zip(names, (m_norm_mix_g, m_w_in, m_conv_w, m_q_norm_g, m_w_uq, m_kv_norm_g, m_w_ukv, m_w_o,
                         m_norm_ffn_g, m_w_up, m_ffn_conv_w, m_ffn_conv_b, m_w_down, m_ple_norm_g, m_w_ple_gate,
                         m_w_ple_proj, m_final_norm_g)))
    v = dict(zip(names, (v_norm_mix_g, v_w_in, v_conv_w, v_q_norm_g, v_w_uq, v_kv_norm_g, v_w_ukv, v_w_o,
                         v_norm_ffn_g, v_w_up, v_ffn_conv_w, v_ffn_conv_b, v_w_down, v_ple_norm_g, v_w_ple_gate,
                         v_w_ple_proj, v_final_norm_g)))

    core = lax.axis_index("c").astype(jnp.int32).reshape(1)
    chip = (2 * lax.axis_index("x") + lax.axis_index("y")).astype(jnp.int32).reshape(1)

    wts = _early_weights(w)
    gather = _chips_start([w[n][0].astype(BF16) for n in _LATE_W], scatter=False, name="gather_late_start")
    wts["norm_mix_g"] = wts["norm_mix_g"] + gather[4][0, 0]

    def late_weights(after):
        shards, landed = _chips_wait(gather, after, scatter=False, name="gather_late_wait")
        full = {n: _full_from_slots(n, lax.dynamic_update_slice(g4, own[None], (chip[0], 0, 0)))
                for n, own, g4 in zip(_LATE_W, shards, landed)}
        return _layout_late(full)

    def reduce_early(g):
        parts = _chip_partials(_LATE_W, g, core, tag="early")
        scatter = _chips_start([t16 for _, t16 in parts], scatter=True, name="scatter_early_start")

        def finish(after):
            _, landed = _chips_wait(scatter, after, scatter=True, name="scatter_early_wait")
            return [_sum_chips(a, t32, chip, name="sum_chips_" + n) for n, a, (t32, _) in zip(_LATE_W, landed, parts)]

        return scatter[4], finish

    def reduce_last(g):
        parts = _chip_partials(_EARLY_W, _true_matrices(g), core, tag="late")
        scatter = _chips_start([t16 for _, t16 in parts], scatter=True, name="scatter_late_start")

        def finish(after):
            _, landed = _chips_wait(scatter, after, scatter=True, name="scatter_late_wait")
            return [_sum_chips(a, t32, chip, name="sum_chips_" + n) for n, a, (t32, _) in zip(_EARLY_W, landed, parts)]

        return scatter[4], finish

    loss, grad_x, small_grads, halves_early, halves_last = _local_step(
        x[0], p[0, 0], loss_target[0], wts, late_weights, reduce_early, reduce_last)
    g_full = _true_vectors(small_grads)
    whole = _join_halves(halves_last + halves_early)
    big = {n: a.reshape(-1, a.shape[2]) for n, a in zip(_BIG, whole)}

    g_out, d_out, m_out, v_out = {}, {}, {}, {}
    for n in _BIG:
        shape = w[n].shape
        g = big[n]
        d, mn, vn = _adamw(w[n][0], g, m[n][0], v[n][0], name="adamw_" + n)
        g_out[n], d_out[n], m_out[n], v_out[n] = (a.reshape(shape) for a in (g, d, mn, vn))

    loss, small = _reduce_small(g_full, loss)
    chip = 2 * lax.axis_index("x") + lax.axis_index("y")
    g_small = {}
    for n in _SMALL:
        shape = w[n].shape
        g = small[n]
        if n in ("conv_w", "ffn_conv_w"):
            width = shape[-1]
            g = lax.dynamic_slice(g.reshape(3, N_CHIPS * width), (0, chip * width), (3, width))
        g_small[n] = g.reshape(shape)
    sizes = [g_small[n].size for n in _SMALL]
    rows = -(-sum(sizes) // 1024) * 8
    packed = [_pack([src[n] for n in _SMALL], rows) for src in (w, g_small, m, v)]
    d_s, m_s, v_s = _adamw(*packed, name="adamw_small")
    for n, d, mn, vn in zip(_SMALL, _unpack(d_s, sizes), _unpack(m_s, sizes), _unpack(v_s, sizes)):
        shape = w[n].shape
        g_out[n], d_out[n], m_out[n], v_out[n] = g_small[n], d.reshape(shape), mn.reshape(shape), vn.reshape(shape)

    return (loss, grad_x[None], *[g_out[n] for n in names], *[d_out[n] for n in names],
            *[m_out[n] for n in names], *[v_out[n] for n in names])
```

```python
import functools

import jax
import jax.numpy as jnp
from jax import lax
from jax.experimental import pallas as pl
from jax.experimental.pallas import tpu as pltpu

F32 = jnp.float32
BF16 = jnp.bfloat16

D_MODEL = 1024
CONV_WIDTH = 512
Q_LORA = 256
KV_LORA = 128
QK_NOPE = 64
QK_ROPE = 32
V_HEAD = 64
N_HEADS = 8
HEAD_PAD = 128
D_ATT = N_HEADS * HEAD_PAD
D_IN = 3 * CONV_WIDTH + Q_LORA + KV_LORA + QK_ROPE
D_IN_PAD = 3 * CONV_WIDTH + Q_LORA + KV_LORA + HEAD_PAD
D_FF = 2816
ROPE_THETA = 10000.0
EPS = 1e-6
SM_SCALE = (QK_NOPE + QK_ROPE) ** -0.5
ONES_LANE = V_HEAD

ADAM_LR = 0.001
ADAM_B1 = 0.9
ADAM_B2 = 0.999
ADAM_EPS = 1e-08
ADAM_WD = 0.01
ADAM_STEP = 10

N_CHIPS = 4
N_DEV = 8
MESH = pl.DeviceIdType.MESH
ANY = pl.BlockSpec(memory_space=pl.ANY)


def _params(sem):
    return pltpu.CompilerParams(dimension_semantics=sem)


def _mm(a, b, *, name, ta=False, tb=False, add=None, out_dtype=F32, tm=512, tn=512, tk=512,
        a_split=False, b_split=False, o_split=False, o_shards=False):
    if a_split:
        _, m, kh = a.shape
        k = 2 * kh
    elif ta:
        k, m = a.shape
    else:
        m, k = a.shape
    if b_split:
        _, kb, nh = b.shape
        n = 2 * nh
    elif tb:
        n, kb = b.shape
    else:
        kb, n = b.shape
    assert kb == k, (name, a.shape, b.shape)
    tm, tn, tk = min(tm, m), min(tn, n), min(tk, k)
    assert m % tm == 0 and n % tn == 0 and k % tk == 0, (name, m, n, k, tm, tn, tk)
    gm, gn, gk = m // tm, n // tn, k // tk

    if a_split:
        assert gk % 2 == 0
        a_spec = pl.BlockSpec((None, tm, tk), lambda i, j, kk: (kk // (gk // 2), i, kk % (gk // 2)))
    elif ta:
        a_spec = pl.BlockSpec((tk, tm), lambda i, j, kk: (kk, i))
    else:
        a_spec = pl.BlockSpec((tm, tk), lambda i, j, kk: (i, kk))
    if b_split:
        assert gn % 2 == 0
        b_spec = pl.BlockSpec((None, tk, tn), lambda i, j, kk: (j // (gn // 2), kk, j % (gn // 2)))
    elif tb:
        b_spec = pl.BlockSpec((tn, tk), lambda i, j, kk: (j, kk))
    else:
        b_spec = pl.BlockSpec((tk, tn), lambda i, j, kk: (kk, j))
    if o_shards:
        o_spec = pl.BlockSpec((None, tm, tn), lambda i, j, kk: (j, i, 0))
        o_shape = jax.ShapeDtypeStruct((gn, m, tn), out_dtype)
    elif o_split:
        assert gn % 2 == 0
        o_spec = pl.BlockSpec((None, tm, tn), lambda i, j, kk: (j // (gn // 2), i, j % (gn // 2)))
        o_shape = jax.ShapeDtypeStruct((2, m, n // 2), out_dtype)
    else:
        o_spec = pl.BlockSpec((tm, tn), lambda i, j, kk: (i, j))
        o_shape = jax.ShapeDtypeStruct((m, n), out_dtype)
    dims = (((0 if ta else 1,), (1 if tb else 0,)), ((), ()))

    def body(*refs):
        a_ref, b_ref = refs[:2]
        add_ref = None if add is None else refs[2]
        o_ref = refs[2 if add is None else 3]

        def finish(r):
            if add_ref is not None:
                r = r + add_ref[...]
            o_ref[...] = r.astype(o_ref.dtype)

        part = lax.dot_general(a_ref[...].astype(BF16), b_ref[...].astype(BF16), dims, preferred_element_type=F32)
        if gk == 1:
            finish(part)
            return
        acc_ref = refs[-1]
        kk = pl.program_id(2)

        @pl.when(kk == 0)
        def _():
            acc_ref[...] = part

        @pl.when((kk > 0) & (kk < gk - 1))
        def _():
            acc_ref[...] += part

        @pl.when(kk == gk - 1)
        def _():
            finish(acc_ref[...] + part)

    in_specs = [a_spec, b_spec]
    args = [a, b]
    if add is not None:
        in_specs.append(pl.BlockSpec((tm, tn), lambda i, j, kk: (i, j)))
        args.append(add)
    return pl.pallas_call(
        body, name=name, grid=(gm, gn, gk), in_specs=in_specs, out_specs=o_spec, out_shape=o_shape,
        scratch_shapes=[] if gk == 1 else [pltpu.VMEM((tm, tn), F32)],
        compiler_params=_params(("parallel", "parallel", "arbitrary")),
    )(*args)


def _rms_scale(v):
    return lax.rsqrt(jnp.mean(v * v, axis=-1, keepdims=True) + EPS)


def _rms_bwd_rows(v, g, dy):
    r = _rms_scale(v)
    vh = v * r
    dyg = dy * g
    dv = r * (dyg - vh * jnp.mean(dyg * vh, axis=-1, keepdims=True))
    return dv, dy * vh


def _shift_down(v, first_row):
    row = lax.broadcasted_iota(jnp.int32, v.shape, 0)
    return jnp.where(row == 0, first_row, pltpu.roll(v, 1, 0))


def _shift_up(v, last_row):
    n = v.shape[0]
    row = lax.broadcasted_iota(jnp.int32, v.shape, 0)
    return jnp.where(row == n - 1, last_row, pltpu.roll(v, n - 1, 0))


def _rope(t, cos, sin_a, sin_b):
    return t * cos + pltpu.roll(t, HEAD_PAD - 16, 1) * sin_a + pltpu.roll(t, 16, 1) * sin_b


def _rope_bwd(d, cos, sin_a, sin_b):
    return d * cos + pltpu.roll(d * sin_a, 16, 1) + pltpu.roll(d * sin_b, HEAD_PAD - 16, 1)


def _sigmoid(v):
    return 1.0 / (1.0 + jnp.exp(-v))


def _halo_specs(ts, s, width, col):
    nb = ts // 8
    last = s // 8 - 1
    prev = pl.BlockSpec((8, width), lambda i: (jnp.maximum(i * nb - 1, 0), col))
    nxt = pl.BlockSpec((8, width), lambda i: (jnp.minimum((i + 1) * nb, last), col))
    return prev, nxt


def _mm_fused(a, b, *, name, epilogue, row_outs, rows=(), vecs=(), n_vec_out=0, tb=False, a_split=False,
              prologue=None, second=None, transposed_out=None, tm=512, tk=None):
    if a_split:
        _, m, kh = a.shape
        k = 2 * kh
    else:
        m, k = a.shape
    n = b.shape[0] if tb else b.shape[1]
    assert (b.shape[1] if tb else b.shape[0]) == k, (name, a.shape, b.shape)
    tk = k if tk is None else tk
    assert m % tm == 0 and k % tk == 0, (name, m, k, tm, tk)
    gm, gk = m // tm, k // tk
    assert prologue is None or gk == 1
    n_a = 2 if a_split and gk == 1 else 1
    nr, nv = len(rows), len(vecs)
    n_pro = 0 if prologue is None else 1
    n_sec = 0 if second is None else 2
    n_t = 0 if transposed_out is None else 1
    dims = (((1,), (1 if tb else 0,)), ((), ()))

    def body(*refs):
        a_ref, b_ref = refs[0], refs[n_a]
        refs = refs[n_a - 1:]
        sec_refs = refs[2:2 + n_sec]
        row_refs, vec_refs = refs[2 + n_sec:2 + n_sec + nr], refs[2 + n_sec + nr:2 + n_sec + nr + nv]
        outs = refs[2 + n_sec + nr + nv:]
        row_out_refs = outs[n_pro:n_pro + len(row_outs)]
        t_out_refs = outs[n_pro + len(row_outs):n_pro + len(row_outs) + n_t]
        vec_out_refs = outs[n_pro + len(row_outs) + n_t:n_pro + len(row_outs) + n_t + n_vec_out]
        i, kk = pl.program_id(0), pl.program_id(1)
        vec_vals = [v[...] for v in vec_refs]
        if prologue is None:
            lhs = a_ref[...].astype(BF16)
        else:
            lhs = prologue(a_ref[...], vec_vals)
            outs[0][...] = lhs

        def finish(r):
            if second is not None:
                r = r + jnp.dot(sec_refs[0][...].astype(BF16), sec_refs[1][...].astype(BF16),
                                preferred_element_type=F32)
            row_vals, vec_parts = epilogue(r, [x[...] for x in row_refs], vec_vals)
            for ref, val in zip(row_out_refs, row_vals):
                ref[...] = val.astype(ref.dtype)
            for ref in t_out_refs:
                ref[...] = row_vals[0].T.astype(ref.dtype)
            if n_vec_out:
                @pl.when(i == 0)
                def _():
                    for ref in vec_out_refs:
                        ref[...] = jnp.zeros_like(ref)

                for ref, val in zip(vec_out_refs, vec_parts):
                    ref[...] += val

        if n_a == 2:
            kh = k // 2
            halves = (b_ref[:, :kh], b_ref[:, kh:]) if tb else (b_ref[:kh, :], b_ref[kh:, :])
            part = (lax.dot_general(lhs, halves[0].astype(BF16), dims, preferred_element_type=F32)
                    + lax.dot_general(refs[0][...].astype(BF16), halves[1].astype(BF16), dims,
                                      preferred_element_type=F32))
        else:
            part = lax.dot_general(lhs, b_ref[...].astype(BF16), dims, preferred_element_type=F32)
        if gk == 1:
            finish(part)
            return
        acc_ref = refs[-1]

        @pl.when(kk == 0)
        def _():
            acc_ref[...] = part

        @pl.when((kk > 0) & (kk < gk - 1))
        def _():
            acc_ref[...] += part

        @pl.when(kk == gk - 1)
        def _():
            finish(acc_ref[...] + part)

    if n_a == 2:
        a_specs = [pl.BlockSpec((None, tm, k // 2), lambda i, kk: (0, i, 0)),
                   pl.BlockSpec((None, tm, k // 2), lambda i, kk: (1, i, 0))]
    elif a_split:
        assert gk % 2 == 0
        a_specs = [pl.BlockSpec((None, tm, tk), lambda i, kk: (kk // (gk // 2), i, kk % (gk // 2)))]
    else:
        a_specs = [pl.BlockSpec((tm, tk), lambda i, kk: (i, kk))]
    b_spec = pl.BlockSpec((n, tk), lambda i, kk: (0, kk)) if tb else pl.BlockSpec((tk, n), lambda i, kk: (kk, 0))
    row_spec = pl.BlockSpec((tm, n), lambda i, kk: (i, 0))
    out_specs, out_shape = [], []
    if prologue is not None:
        out_specs.append(pl.BlockSpec((tm, k), lambda i, kk: (i, 0)))
        out_shape.append(jax.ShapeDtypeStruct((m, k), BF16))
    out_specs += [row_spec] * len(row_outs)
    out_shape += [jax.ShapeDtypeStruct((m, n), dt) for dt in row_outs]
    if transposed_out is not None:
        out_specs.append(pl.BlockSpec((n, tm), lambda i, kk: (0, i)))
        out_shape.append(jax.ShapeDtypeStruct((n, m), transposed_out))
    out_specs += [pl.BlockSpec((1, n), lambda i, kk: (0, 0))] * n_vec_out
    out_shape += [jax.ShapeDtypeStruct((1, n), F32)] * n_vec_out
    sec_specs, sec_args = [], []
    if second is not None:
        k2 = second[0].shape[1]
        sec_specs = [pl.BlockSpec((tm, k2), lambda i, kk: (i, 0)), pl.BlockSpec((k2, n), lambda i, kk: (0, 0))]
        sec_args = list(second)
    res = pl.pallas_call(
        body, name=name, grid=(gm, gk),
        in_specs=a_specs + [b_spec] + sec_specs + [row_spec] * nr
        + [pl.BlockSpec((1, v.shape[1]), lambda i, kk: (0, 0)) for v in vecs],
        out_specs=out_specs, out_shape=out_shape,
        scratch_shapes=[] if gk == 1 else [pltpu.VMEM((tm, n), F32)],
        compiler_params=_params(("arbitrary" if n_vec_out else "parallel", "arbitrary")),
    )(*([a] * n_a), b, *sec_args, *rows, *vecs)
    split = n_pro + len(row_outs) + n_t
    return list(res[:split]), list(res[split:])


def _pro_rms(a, vecs):
    return (a * _rms_scale(a) * vecs[0]).astype(BF16)


def _epi_plain(r, rows, vecs):
    return [r], []


def _epi_add_rms(r, rows, vecs):
    xn = r + rows[0]
    return [xn, xn * _rms_scale(xn) * vecs[0]], []


def _epi_rms_bwd(r, rows, vecs):
    dv, dg_rows = _rms_bwd_rows(rows[0], vecs[0], r)
    return [dv + rows[1]], [jnp.sum(dg_rows, axis=0, keepdims=True)]


def _mix_pre(z, conv_w8, gq, gkv, cos, sin_a, sin_b, *, ts=256):
    s = z.shape[0]
    n = s // ts
    cw = CONV_WIDTH

    def body(z_ref, xcp, xcn, cgp, cgn, w_ref, gq_ref, gkv_ref, cos_ref, sa_ref, sb_ref,
             yc_ref, qn_ref, kvn_ref, kr_ref):
        i = pl.program_id(0)
        xc = z_ref[:, 0:cw]
        bg = z_ref[:, cw:2 * cw]
        cg = z_ref[:, 2 * cw:3 * cw]
        m = cg * xc
        m_prev = jnp.where(i > 0, xcp[7:8, :] * cgp[7:8, :], 0.0)
        m_next = jnp.where(i < n - 1, xcn[0:1, :] * cgn[0:1, :], 0.0)
        cm = _shift_down(m, m_prev) * w_ref[0:1, :] + m * w_ref[1:2, :] + _shift_up(m, m_next) * w_ref[2:3, :]
        yc_ref[...] = (bg * cm).astype(BF16)
        ql = z_ref[:, 3 * cw:3 * cw + Q_LORA]
        qn_ref[...] = (ql * _rms_scale(ql) * gq_ref[...]).astype(BF16)
        kvl = z_ref[:, 3 * cw + Q_LORA:3 * cw + Q_LORA + KV_LORA]
        kvn_ref[...] = (kvl * _rms_scale(kvl) * gkv_ref[...]).astype(BF16)
        kr_ref[...] = _rope(z_ref[:, D_IN_PAD - HEAD_PAD:D_IN_PAD], cos_ref[...], sa_ref[...], sb_ref[...])

    xcp, xcn = _halo_specs(ts, s, cw, 0)
    cgp, cgn = _halo_specs(ts, s, cw, 2)
    tab = pl.BlockSpec((ts, HEAD_PAD), lambda i: (i, 0))
    return pl.pallas_call(
        body, name="mix_pre", grid=(n,),
        in_specs=[pl.BlockSpec((ts, D_IN_PAD), lambda i: (i, 0)), xcp, xcn, cgp, cgn,
                  pl.BlockSpec((8, cw), lambda i: (0, 0)), pl.BlockSpec((1, Q_LORA), lambda i: (0, 0)),
                  pl.BlockSpec((1, KV_LORA), lambda i: (0, 0)), tab, tab, tab],
        out_specs=[pl.BlockSpec((ts, cw), lambda i: (i, 0)), pl.BlockSpec((ts, Q_LORA), lambda i: (i, 0)),
                   pl.BlockSpec((ts, KV_LORA), lambda i: (i, 0)), tab],
        out_shape=[jax.ShapeDtypeStruct((s, cw), BF16), jax.ShapeDtypeStruct((s, Q_LORA), BF16),
                   jax.ShapeDtypeStruct((s, KV_LORA), BF16), jax.ShapeDtypeStruct((s, HEAD_PAD), F32)],
        compiler_params=_params(("parallel",)),
    )(z, z, z, z, z, conv_w8, gq, gkv, cos, sin_a, sin_b)


def _mix_bwd(z, dyc, dqn, dkvn, dkr, conv_w8, gq, gkv, cos, sin_a, sin_b, *, ts=256):
    s = z.shape[0]
    n = s // ts
    cw = CONV_WIDTH

    def body(z_ref, xcp, xcn, bgp, bgn, cgp, cgn, dyc_ref, dycp, dycn, dqn_ref, dkvn_ref, dkr_ref,
             w_ref, gq_ref, gkv_ref, cos_ref, sa_ref, sb_ref,
             dz_ref, dw0_ref, dw1_ref, dw2_ref, dgq_ref, dgkv_ref):
        i = pl.program_id(0)

        @pl.when(i == 0)
        def _():
            for r in (dw0_ref, dw1_ref, dw2_ref, dgq_ref, dgkv_ref):
                r[...] = jnp.zeros_like(r)

        xc = z_ref[:, 0:cw]
        bg = z_ref[:, cw:2 * cw]
        cg = z_ref[:, 2 * cw:3 * cw]
        w0, w1, w2 = w_ref[0:1, :], w_ref[1:2, :], w_ref[2:3, :]
        m = cg * xc
        m_dn = _shift_down(m, jnp.where(i > 0, xcp[7:8, :] * cgp[7:8, :], 0.0))
        m_up = _shift_up(m, jnp.where(i < n - 1, xcn[0:1, :] * cgn[0:1, :], 0.0))
        cm = m_dn * w0 + m * w1 + m_up * w2
        dyc_v = dyc_ref[...]
        dcm = dyc_v * bg
        dcm_dn = _shift_down(dcm, jnp.where(i > 0, dycp[7:8, :] * bgp[7:8, :], 0.0))
        dcm_up = _shift_up(dcm, jnp.where(i < n - 1, dycn[0:1, :] * bgn[0:1, :], 0.0))
        dm = dcm_up * w0 + dcm * w1 + dcm_dn * w2
        dz_ref[:, 0:cw] = (dm * cg).astype(BF16)
        dz_ref[:, cw:2 * cw] = (dyc_v * cm).astype(BF16)
        dz_ref[:, 2 * cw:3 * cw] = (dm * xc).astype(BF16)
        dw0_ref[...] += jnp.sum(dcm * m_dn, axis=0, keepdims=True)
        dw1_ref[...] += jnp.sum(dcm * m, axis=0, keepdims=True)
        dw2_ref[...] += jnp.sum(dcm * m_up, axis=0, keepdims=True)

        dql, dgq_rows = _rms_bwd_rows(z_ref[:, 3 * cw:3 * cw + Q_LORA], gq_ref[...], dqn_ref[...])
        dz_ref[:, 3 * cw:3 * cw + Q_LORA] = dql.astype(BF16)
        dgq_ref[...] += jnp.sum(dgq_rows, axis=0, keepdims=True)
        dkvl, dgkv_rows = _rms_bwd_rows(z_ref[:, 3 * cw + Q_LORA:3 * cw + Q_LORA + KV_LORA], gkv_ref[...],
                                        dkvn_ref[...])
        dz_ref[:, 3 * cw + Q_LORA:3 * cw + Q_LORA + KV_LORA] = dkvl.astype(BF16)
        dgkv_ref[...] += jnp.sum(dgkv_rows, axis=0, keepdims=True)

        lane = lax.broadcasted_iota(jnp.int32, (ts, HEAD_PAD), 1)
        rope_lane = (lane >= QK_NOPE) & (lane < QK_NOPE + QK_ROPE)
        dk = _rope_bwd(dkr_ref[...], cos_ref[...], sa_ref[...], sb_ref[...])
        dz_ref[:, D_IN_PAD - HEAD_PAD:D_IN_PAD] = jnp.where(rope_lane, dk, 0.0).astype(BF16)

    xcp, xcn = _halo_specs(ts, s, cw, 0)
    bgp, bgn = _halo_specs(ts, s, cw, 1)
    cgp, cgn = _halo_specs(ts, s, cw, 2)
    dycp, dycn = _halo_specs(ts, s, cw, 0)
    tab = pl.BlockSpec((ts, HEAD_PAD), lambda i: (i, 0))

    def vec(width):
        return pl.BlockSpec((1, width), lambda i: (0, 0))

    outs = pl.pallas_call(
        body, name="mix_bwd", grid=(n,),
        in_specs=[pl.BlockSpec((ts, D_IN_PAD), lambda i: (i, 0)), xcp, xcn, bgp, bgn, cgp, cgn,
                  pl.BlockSpec((ts, cw), lambda i: (i, 0)), dycp, dycn,
                  pl.BlockSpec((ts, Q_LORA), lambda i: (i, 0)), pl.BlockSpec((ts, KV_LORA), lambda i: (i, 0)), tab,
                  pl.BlockSpec((8, cw), lambda i: (0, 0)), vec(Q_LORA), vec(KV_LORA), tab, tab, tab],
        out_specs=[pl.BlockSpec((ts, D_IN_PAD), lambda i: (i, 0)), vec(cw), vec(cw), vec(cw), vec(Q_LORA),
                   vec(KV_LORA)],
        out_shape=[jax.ShapeDtypeStruct((s, D_IN_PAD), BF16)] + [jax.ShapeDtypeStruct((1, cw), F32)] * 3
        + [jax.ShapeDtypeStruct((1, Q_LORA), F32), jax.ShapeDtypeStruct((1, KV_LORA), F32)],
        compiler_params=_params(("arbitrary",)),
    )(z, z, z, z, z, z, z, dyc, dyc, dyc, dqn, dkvn, dkr, conv_w8, gq, gkv, cos, sin_a, sin_b)
    dz, dw0, dw1, dw2, dgq, dgkv = outs
    return dz, jnp.concatenate([dw0, dw1, dw2], axis=0), dgq, dgkv


def _qkv_proj(qn, kvn, kr, w_uq_p, w_kv_p, cos, sin_a, sin_b, *, ts=512):
    s = qn.shape[0]

    def body(qn_ref, kvn_ref, kr_ref, wq_ref, wkv_ref, cos_ref, sa_ref, sb_ref, q_ref, k_ref, v_ref, qt_ref):
        cos_v, sa, sb = cos_ref[...], sa_ref[...], sb_ref[...]
        q = jnp.dot(qn_ref[...], wq_ref[...], preferred_element_type=F32)
        kv = jnp.dot(kvn_ref[...], wkv_ref[...], preferred_element_type=F32)
        kr_v = kr_ref[...]
        lane = lax.broadcasted_iota(jnp.int32, (1, HEAD_PAD), 1)
        ones_lane = (lane == ONES_LANE).astype(F32)
        for h in range(N_HEADS):
            blk = slice(h * HEAD_PAD, (h + 1) * HEAD_PAD)
            q_h = _rope(q[:, blk], cos_v, sa, sb) * SM_SCALE
            q_ref[:, blk] = q_h.astype(BF16)
            qt_ref[blk, :] = q_h.T.astype(BF16)
            k_ref[:, blk] = (kv[:, blk] + kr_v).astype(BF16)
            v_ref[:, blk] = (kv[:, D_ATT + h * HEAD_PAD:D_ATT + (h + 1) * HEAD_PAD] + ones_lane).astype(BF16)

    tab = pl.BlockSpec((ts, HEAD_PAD), lambda i: (i, 0))
    wide = pl.BlockSpec((ts, D_ATT), lambda i: (i, 0))
    return pl.pallas_call(
        body, name="qkv_proj", grid=(s // ts,),
        in_specs=[pl.BlockSpec((ts, Q_LORA), lambda i: (i, 0)), pl.BlockSpec((ts, KV_LORA), lambda i: (i, 0)), tab,
                  pl.BlockSpec((Q_LORA, D_ATT), lambda i: (0, 0)), pl.BlockSpec((KV_LORA, 2 * D_ATT), lambda i: (0, 0)),
                  tab, tab, tab],
        out_specs=[wide, wide, wide, pl.BlockSpec((D_ATT, ts), lambda i: (0, i))],
        out_shape=[jax.ShapeDtypeStruct((s, D_ATT), BF16)] * 3 + [jax.ShapeDtypeStruct((D_ATT, s), BF16)],
        compiler_params=_params(("parallel",)),
    )(qn, kvn, kr, w_uq_p, w_kv_p, cos, sin_a, sin_b)


def _qk_bwd(dq, dk, cos, sin_a, sin_b, *, ts=256):
    s = dq.shape[0]

    def body(dq_ref, dk_ref, cos_ref, sa_ref, sb_ref, dqp_ref, dkr_ref):
        cos_v, sa, sb = cos_ref[...], sa_ref[...], sb_ref[...]
        tot = jnp.zeros((ts, HEAD_PAD), F32)
        for h in range(N_HEADS):
            blk = slice(h * HEAD_PAD, (h + 1) * HEAD_PAD)
            dqp_ref[:, blk] = _rope_bwd(dq_ref[:, blk], cos_v, sa, sb).astype(BF16)
            tot = tot + dk_ref[:, blk]
        dkr_ref[...] = tot

    tab = pl.BlockSpec((ts, HEAD_PAD), lambda i: (i, 0))
    wide = pl.BlockSpec((ts, D_ATT), lambda i: (i, 0))
    return pl.pallas_call(
        body, name="qk_bwd", grid=(s // ts,),
        in_specs=[wide, wide, tab, tab, tab], out_specs=[wide, tab],
        out_shape=[jax.ShapeDtypeStruct((s, D_ATT), BF16), jax.ShapeDtypeStruct((s, HEAD_PAD), F32)],
        compiler_params=_params(("parallel",)),
    )(dq, dk, cos, sin_a, sin_b)


_NT = (((1,), (1,)), ((), ()))
_TN = (((0,), (0,)), ((), ()))


def _flash_fwd(q, k, v, *, tq=1024, tk=1024, per_trip=8):
    s = q.shape[0]
    tq, tk = min(tq, s), min(tk, s)
    nk = s // tk
    per_trip = min(per_trip, nk)
    assert nk % per_trip == 0

    def body(q_ref, k_ref, v_ref, o_ref, lse_ref):
        qv = q_ref[...]

        def step(j, carry):
            m, acc = carry
            rows = pl.ds(pl.multiple_of(j * tk, tk), tk)
            sc = lax.dot_general(qv, k_ref[rows, :], _NT, preferred_element_type=F32)
            m_new = jnp.maximum(m, jnp.max(sc, axis=1, keepdims=True))
            p = jnp.exp(sc - m_new).astype(BF16)
            acc = jnp.exp(m - m_new) * acc + jnp.dot(p, v_ref[rows, :], preferred_element_type=F32)
            return m_new, acc

        def trip(t, carry):
            for c in range(per_trip):
                carry = step(per_trip * t + c, carry)
            return carry

        init = (jnp.full((tq, 1), -jnp.inf, F32), jnp.zeros((tq, HEAD_PAD), F32))
        m, acc = lax.fori_loop(0, nk // per_trip, trip, init)
        l = acc[:, ONES_LANE:ONES_LANE + 1]
        o_ref[...] = (acc / l).astype(BF16)
        lse_ref[...] = m + jnp.log(l)

    head = pl.BlockSpec((s, HEAD_PAD), lambda h, i: (0, h))
    return pl.pallas_call(
        body, name="flash_fwd", grid=(N_HEADS, s // tq),
        in_specs=[pl.BlockSpec((tq, HEAD_PAD), lambda h, i: (i, h)), head, head],
        out_specs=[pl.BlockSpec((tq, HEAD_PAD), lambda h, i: (i, h)),
                   pl.BlockSpec((None, tq, 1), lambda h, i: (h, i, 0))],
        out_shape=[jax.ShapeDtypeStruct((s, D_ATT), BF16), jax.ShapeDtypeStruct((N_HEADS, s, 1), F32)],
        compiler_params=_params(("parallel", "parallel")),
    )(q, k, v)


def _attn_delta(do, o, *, ts=512):
    s = do.shape[0]

    def body(do_ref, o_ref, dl_ref):
        for h in range(N_HEADS):
            blk = slice(h * HEAD_PAD, (h + 1) * HEAD_PAD)
            dl_ref[h] = jnp.sum(do_ref[:, blk].astype(F32) * o_ref[:, blk].astype(F32), axis=1, keepdims=True)

    wide = pl.BlockSpec((ts, D_ATT), lambda i: (i, 0))
    return pl.pallas_call(
        body, name="attn_delta", grid=(s // ts,), in_specs=[wide, wide],
        out_specs=pl.BlockSpec((N_HEADS, ts, 1), lambda i: (0, i, 0)),
        out_shape=jax.ShapeDtypeStruct((N_HEADS, s, 1), F32),
        compiler_params=_params(("parallel",)),
    )(do, o)


def _flash_bwd(q, qt, k, v, do, dot, lse, delta, after, *, tq=1024, tk=512, per_trip=8):
    s = q.shape[0]
    tq, tk = min(tq, s), min(tk, s)
    nq = s // tq
    per_trip = min(per_trip, nq)
    assert nq % per_trip == 0

    def body(q_ref, qt_ref, do_ref, dot_ref, lse_ref, dl_ref, k_ref, v_ref, after_ref, dq_ref, dk_ref, dv_ref):
        j = pl.program_id(1)

        @pl.when(j == 0)
        def _():
            dq_ref[...] = jnp.zeros_like(dq_ref)

        kv, vv = k_ref[...], v_ref[...]

        def chunk(i, dk_t, dv_t):
            at = pl.multiple_of(i * tq, tq)
            rows = pl.ds(at, tq)
            sc = lax.dot_general(q_ref[rows, :], kv, _NT, preferred_element_type=F32)
            p = jnp.exp(sc - lse_ref[rows, :])
            dp = lax.dot_general(do_ref[rows, :], vv, _NT, preferred_element_type=F32)
            ds = (p * (dp - dl_ref[rows, :])).astype(BF16)
            dv_t = dv_t + jnp.dot(dot_ref[:, rows], p.astype(BF16), preferred_element_type=F32)
            dk_t = dk_t + jnp.dot(qt_ref[:, rows], ds, preferred_element_type=F32)
            dq_ref[rows, :] += jnp.dot(ds, kv, preferred_element_type=F32)
            return dk_t, dv_t

        def step(i, carry):
            for c in range(per_trip):
                carry = chunk(per_trip * i + c, *carry)
            return carry

        zero = jnp.zeros((HEAD_PAD, tk), F32)
        dk_t, dv_t = lax.fori_loop(0, nq // per_trip, step, (zero, zero))
        dk_ref[...] = dk_t.T
        dv_ref[...] = dv_t.T

        @pl.when(j == pl.num_programs(1) - 1)
        def _():
            dq_ref[...] *= SM_SCALE

    head = pl.BlockSpec((s, HEAD_PAD), lambda h, j: (0, h))
    head_t = pl.BlockSpec((HEAD_PAD, s), lambda h, j: (h, 0))
    stat = pl.BlockSpec((None, s, 1), lambda h, j: (h, 0, 0))
    blk = pl.BlockSpec((tk, HEAD_PAD), lambda h, j: (j, h))
    return pl.pallas_call(
        body, name="flash_bwd", grid=(N_HEADS, s // tk),
        in_specs=[head, head_t, head, head_t, stat, stat, blk, blk, ANY],
        out_specs=[head, blk, blk],
        out_shape=[jax.ShapeDtypeStruct((s, D_ATT), F32)] * 3,
        compiler_params=_params(("parallel", "arbitrary")),
    )(q, qt, do, dot, lse, delta, k, v, after)


FFN_TC = 256
FFN_TG = 1408


FFN_HALO_BF16 = 16
FFN_HALO_F32 = 8


def _row_halo_specs(ts, s, halo, width):
    nb = ts // halo
    last = s // halo - 1
    prev = pl.BlockSpec((halo, width), lambda i, j: (jnp.maximum(i * nb - 1, 0), 0))
    nxt = pl.BlockSpec((halo, width), lambda i, j: (jnp.minimum((i + 1) * nb, last), 0))
    return prev, nxt


def _ext_rows(prev, main, nxt, first, last):
    return jnp.concatenate([jnp.where(first, jnp.zeros_like(prev), prev), main,
                            jnp.where(last, jnp.zeros_like(nxt), nxt)], axis=0)


def _ext_conv(a, w):
    a_dn = pltpu.roll(a, 1, 0)
    a_up = pltpu.roll(a, a.shape[0] - 1, 0)
    return a_dn * w[0:1, :] + a * w[1:2, :] + a_up * w[2:3, :], a_dn, a_up


def _ffn_pieces(tg):
    return [(off, min(FFN_TC, tg - off)) for off in range(0, tg, FFN_TC)]


def _ffn_fwd(hf, w_up, w, b, *, ts=512, tg=FFN_TG):
    s = hf.shape[0]
    n, ng, halo = s // ts, D_FF // tg, FFN_HALO_BF16

    def body(h_ref, hp_ref, hn_ref, wg_ref, wu_ref, cw_ref, cb_ref, a_ref, act_ref):
        i = pl.program_id(0)
        ext = _ext_rows(hp_ref[...], h_ref[...], hn_ref[...], i == 0, i == n - 1)
        for off, width in _ffn_pieces(tg):
            cols = slice(off, off + width)
            gate_up = []
            for half, w_ref in enumerate((wg_ref, wu_ref)):
                a_ext = jnp.dot(ext, w_ref[:, cols], preferred_element_type=F32)
                a_ref[half, :, cols] = a_ext[halo:halo + ts]
                conv = _ext_conv(a_ext, cw_ref[half, :, cols])[0]
                gate_up.append(conv[halo:halo + ts] + cb_ref[half, :, cols])
            g, u = gate_up
            act_ref[:, cols] = (g * _sigmoid(g) * u).astype(BF16)

    prev, nxt = _row_halo_specs(ts, s, halo, D_MODEL)
    return pl.pallas_call(
        body, name="ffn_fwd", grid=(n, ng),
        in_specs=[pl.BlockSpec((ts, D_MODEL), lambda i, j: (i, 0)), prev, nxt,
                  pl.BlockSpec((D_MODEL, tg), lambda i, j: (0, j)), pl.BlockSpec((D_MODEL, tg), lambda i, j: (0, j + ng)),
                  pl.BlockSpec((2, 8, tg), lambda i, j: (0, 0, j)), pl.BlockSpec((2, 1, tg), lambda i, j: (0, 0, j))],
        out_specs=[pl.BlockSpec((2, ts, tg), lambda i, j: (0, i, j)), pl.BlockSpec((ts, tg), lambda i, j: (i, j))],
        out_shape=[jax.ShapeDtypeStruct((2, s, D_FF), F32), jax.ShapeDtypeStruct((s, D_FF), BF16)],
        compiler_params=_params(("parallel", "parallel")),
    )(hf, hf, hf, w_up, w_up, w, b)


def _ffn_bwd(dx2, w_down, a_pre, w, b, *, ts=512, tg=FFN_TG):
    s = dx2.shape[0]
    n, ng, halo = s // ts, D_FF // tg, FFN_HALO_F32
    main = slice(halo, halo + ts)

    def body(dx_ref, dxp_ref, dxn_ref, wd_ref, a_ref, ap_ref, an_ref, cw_ref, cb_ref, o_ref, dw_ref, db_ref):
        i, j = pl.program_id(0), pl.program_id(1)
        first, last = i == 0, i == n - 1

        @pl.when(first & (j == 0))
        def _():
            dw_ref[...] = jnp.zeros_like(dw_ref)
            db_ref[...] = jnp.zeros_like(db_ref)

        dx_ext = _ext_rows(dxp_ref[...], dx_ref[...], dxn_ref[...], first, last).astype(BF16)
        for off, width in _ffn_pieces(tg):
            cols = slice(off, off + width)
            dact = lax.dot_general(dx_ext, wd_ref[cols, :], _NT, preferred_element_type=F32)
            halves = []
            for half in range(2):
                a_ext = _ext_rows(ap_ref[half, :, cols], a_ref[half, :, cols], an_ref[half, :, cols], first, last)
                conv, a_dn, a_up = _ext_conv(a_ext, cw_ref[half, :, cols])
                halves.append((conv + cb_ref[half, :, cols], a_dn, a_ext, a_up))
            g, u = halves[0][0], halves[1][0]
            sg = _sigmoid(g)
            grads = (dact * u * (sg * (1.0 + g * (1.0 - sg))), dact * (g * sg))
            for half in range(2):
                d = grads[half]
                _, a_dn, a_ext, a_up = halves[half]
                wv = cw_ref[half, :, cols]
                d_pre = (pltpu.roll(d, d.shape[0] - 1, 0) * wv[0:1, :] + d * wv[1:2, :]
                         + pltpu.roll(d, 1, 0) * wv[2:3, :])
                o_ref[half, :, cols] = d_pre[main].astype(BF16)
                dm = d[main]
                dw_ref[j, half, 0:1, cols] += jnp.sum(dm * a_dn[main], axis=0, keepdims=True)
                dw_ref[j, half, 1:2, cols] += jnp.sum(dm * a_ext[main], axis=0, keepdims=True)
                dw_ref[j, half, 2:3, cols] += jnp.sum(dm * a_up[main], axis=0, keepdims=True)
                db_ref[j, half, :, cols] += jnp.sum(dm, axis=0, keepdims=True)

    dxp, dxn = _row_halo_specs(ts, s, halo, D_MODEL)
    nb, lastb = ts // halo, s // halo - 1
    a_main = pl.BlockSpec((2, ts, tg), lambda i, j: (0, i, j))
    a_prev = pl.BlockSpec((2, halo, tg), lambda i, j: (0, jnp.maximum(i * nb - 1, 0), j))
    a_next = pl.BlockSpec((2, halo, tg), lambda i, j: (0, jnp.minimum((i + 1) * nb, lastb), j))
    da_pre, dw, db = pl.pallas_call(
        body, name="ffn_bwd", grid=(n, ng),
        in_specs=[pl.BlockSpec((ts, D_MODEL), lambda i, j: (i, 0)), dxp, dxn,
                  pl.BlockSpec((tg, D_MODEL), lambda i, j: (j, 0)), a_main, a_prev, a_next,
                  pl.BlockSpec((2, 8, tg), lambda i, j: (0, 0, j)), pl.BlockSpec((2, 1, tg), lambda i, j: (0, 0, j))],
        out_specs=[a_main, pl.BlockSpec((ng, 2, 8, tg), lambda i, j: (0, 0, 0, 0)),
                   pl.BlockSpec((ng, 2, 1, tg), lambda i, j: (0, 0, 0, 0))],
        out_shape=[jax.ShapeDtypeStruct((2, s, D_FF), BF16), jax.ShapeDtypeStruct((ng, 2, 8, tg), F32),
                   jax.ShapeDtypeStruct((ng, 2, 1, tg), F32)],
        compiler_params=_params(("arbitrary", "arbitrary")),
    )(dx2, dx2, dx2, w_down, a_pre, a_pre, a_pre, w, b)
    return (da_pre, dw.transpose(1, 2, 0, 3).reshape(2, 8, D_FF), db.transpose(1, 2, 0, 3).reshape(2, 1, D_FF))


def _ple_final(x2, n3, p, target, gf, w_pg, w_pp, *, ts=256):
    s, d = x2.shape
    dp = p.shape[1]

    def body(x2_ref, n3_ref, p_ref, t_ref, gf_ref, wg_ref, wp_ref, loss_ref, dx3_ref, dgl_ref, dpp_ref, dgf_ref):
        @pl.when(pl.program_id(0) == 0)
        def _():
            loss_ref[...] = jnp.zeros_like(loss_ref)
            dgf_ref[...] = jnp.zeros_like(dgf_ref)

        gate = _sigmoid(jnp.dot(n3_ref[...], wg_ref[...], preferred_element_type=F32))
        ppv = jnp.dot(p_ref[...].astype(BF16), wp_ref[...], preferred_element_type=F32)
        x3 = x2_ref[...] + gate * ppv
        gfv = gf_ref[...]
        err = x3 * _rms_scale(x3) * gfv - t_ref[...]
        loss_ref[...] += 0.5 * jnp.sum(jnp.mean(err * err, axis=-1, keepdims=True), axis=0, keepdims=True)
        dx3, dgf_rows = _rms_bwd_rows(x3, gfv, err * (1.0 / d))
        dgf_ref[...] += jnp.sum(dgf_rows, axis=0, keepdims=True)
        dx3_ref[...] = dx3
        dgl_ref[...] = (dx3 * ppv * gate * (1.0 - gate)).astype(BF16)
        dpp_ref[...] = (dx3 * gate).astype(BF16)

    row = pl.BlockSpec((ts, d), lambda i: (i, 0))
    vec = pl.BlockSpec((1, d), lambda i: (0, 0))
    return pl.pallas_call(
        body, name="ple_final", grid=(s // ts,),
        in_specs=[row, row, pl.BlockSpec((ts, dp), lambda i: (i, 0)), row, vec,
                  pl.BlockSpec((d, d), lambda i: (0, 0)), pl.BlockSpec((dp, d), lambda i: (0, 0))],
        out_specs=[pl.BlockSpec((1, 128), lambda i: (0, 0)), row, row, row, vec],
        out_shape=[jax.ShapeDtypeStruct((1, 128), F32), jax.ShapeDtypeStruct((s, d), F32),
                   jax.ShapeDtypeStruct((s, d), BF16), jax.ShapeDtypeStruct((s, d), BF16),
                   jax.ShapeDtypeStruct((1, d), F32)],
        compiler_params=_params(("arbitrary",)),
    )(x2, n3, p, target, gf, w_pg, w_pp)


def _row_tile(rows, cols, n_arrays, budget=12 << 20):
    best = None
    for t in range(8, rows + 1, 8):
        if rows % t == 0 and t * cols * 4 * n_arrays <= budget:
            best = t
    return rows if best is None else best


def _sum_slots(a, *, name):
    g, r, c = a.shape
    tr = _row_tile(r, c, g + 1)

    def body(*refs):
        tot = refs[0][...]
        for ref in refs[1:g]:
            tot = tot + ref[...]
        refs[g][...] = tot

    specs = [pl.BlockSpec((None, tr, c), functools.partial(lambda i, slot: (slot, i, 0), slot=k)) for k in range(g)]
    return pl.pallas_call(
        body, name=name, grid=(r // tr,), in_specs=specs, out_specs=pl.BlockSpec((tr, c), lambda i: (i, 0)),
        out_shape=jax.ShapeDtypeStruct((r, c), a.dtype), compiler_params=_params(("parallel",)),
    )(*([a] * g))


def _adamw(w, g, m, v, *, name):
    r, c = w.shape
    tr = _row_tile(r, c, 7)

    def body(w_ref, g_ref, m_ref, v_ref, d_ref, mo_ref, vo_ref):
        gv = g_ref[...]
        mn = ADAM_B1 * m_ref[...] + (1.0 - ADAM_B1) * gv
        vn = ADAM_B2 * v_ref[...] + (1.0 - ADAM_B2) * (gv * gv)
        m_hat = mn / (1.0 - ADAM_B1 ** ADAM_STEP)
        v_hat = vn / (1.0 - ADAM_B2 ** ADAM_STEP)
        d_ref[...] = -ADAM_LR * (m_hat / (jnp.sqrt(v_hat) + ADAM_EPS) + ADAM_WD * w_ref[...])
        mo_ref[...] = mn
        vo_ref[...] = vn

    blk = pl.BlockSpec((tr, c), lambda i: (i, 0))
    return pl.pallas_call(
        body, name=name, grid=(r // tr,), in_specs=[blk] * 4, out_specs=[blk] * 3,
        out_shape=[jax.ShapeDtypeStruct((r, c), F32)] * 3, compiler_params=_params(("parallel",)),
    )(w, g, m, v)


def _position():
    x, y, c = lax.axis_index("x"), lax.axis_index("y"), lax.axis_index("c")
    return x, y, c


def _other_chips(x, y):
    return [(1 - x, y), (x, 1 - y), (1 - x, 1 - y)]


def _stage_in(srcs, stage, sems):
    cps = [pltpu.make_async_copy(src, stage[a], sems.at[a]) for a, src in enumerate(srcs)]
    for cp in cps:
        cp.start()
    return cps


def _stage_out(staged, stage, dsts, sems):
    cps = []
    for a, dst in enumerate(dsts):
        staged[a].wait()
        cp = pltpu.make_async_copy(stage[a], dst, sems.at[a])
        cp.start()
        cps.append(cp)
    return cps


def _gather_chips(shards):
    n = len(shards)

    def body(*refs):
        ins, outs, stage = refs[:n], refs[n:2 * n], refs[2 * n:3 * n]
        send_sems, recv_sems, in_sems, out_sems = refs[3 * n:]
        x, y, c = _position()
        me = 2 * x + y
        chips = _other_chips(x, y)
        remote = []
        staged = _stage_in(ins, stage, in_sems)
        for a in range(n):
            for k, (px, py) in enumerate(chips):
                rc = pltpu.make_async_remote_copy(
                    src_ref=ins[a], dst_ref=outs[a].at[me], send_sem=send_sems.at[3 * a + k],
                    recv_sem=recv_sems.at[3 * a + k], device_id=(px, py, c), device_id_type=MESH)
                rc.start()
                remote.append(rc)
        local = _stage_out(staged, stage, [o.at[me] for o in outs], out_sems)
        for a in range(n):
            for k, (px, py) in enumerate(chips):
                pltpu.make_async_remote_copy(
                    src_ref=ins[a], dst_ref=outs[a].at[2 * px + py], send_sem=send_sems.at[3 * a + k],
                    recv_sem=recv_sems.at[3 * a + k], device_id=(px, py, c), device_id_type=MESH).wait_recv()
        for rc in remote:
            rc.wait_send()
        for cp in local:
            cp.wait()

    return pl.pallas_call(
        body, name="gather_chips", in_specs=[ANY] * n, out_specs=[ANY] * n,
        out_shape=[jax.ShapeDtypeStruct((N_CHIPS,) + s.shape, s.dtype) for s in shards],
        scratch_shapes=[pltpu.VMEM(s.shape, s.dtype) for s in shards]
        + [pltpu.SemaphoreType.DMA((3 * n,)), pltpu.SemaphoreType.DMA((3 * n,)),
           pltpu.SemaphoreType.DMA((n,)), pltpu.SemaphoreType.DMA((n,))],
        compiler_params=pltpu.CompilerParams(has_side_effects=True),
    )(*shards)


def _send_other_halves(grads, *, tag):
    n = len(grads)

    def body(*refs):
        ins, sib = refs[:n], refs[n:2 * n]
        send_sems, recv_sems = refs[2 * n:]
        x, y, c = _position()
        remote = []
        for a in range(n):
            half = ins[a].shape[1] // 2
            give = ins[a].at[:, pl.ds(pl.multiple_of((1 - c) * half, 8), half), :]
            rc = pltpu.make_async_remote_copy(
                src_ref=give, dst_ref=sib[a], send_sem=send_sems.at[a], recv_sem=recv_sems.at[a],
                device_id=(x, y, 1 - c), device_id_type=MESH)
            rc.start()
            remote.append(rc)
        for rc in remote:
            rc.wait_recv()
        for rc in remote:
            rc.wait_send()

    return pl.pallas_call(
        body, name="send_other_halves_" + tag, in_specs=[ANY] * n, out_specs=[ANY] * n,
        out_shape=[jax.ShapeDtypeStruct((g.shape[0], g.shape[1] // 2, g.shape[2]), g.dtype) for g in grads],
        scratch_shapes=[pltpu.SemaphoreType.DMA((n,)), pltpu.SemaphoreType.DMA((n,))],
        compiler_params=pltpu.CompilerParams(has_side_effects=True),
    )(*grads)


def _add_own_half(g4, sib, core, *, name):
    g, a2, c = sib.shape
    tr = _row_tile(a2, c, 4)

    def body(core_ref, a_ref, b_ref, o_ref, o16_ref):
        tot = a_ref[...] + b_ref[...]
        o_ref[...] = tot
        o16_ref[...] = tot.astype(BF16)

    blk = pl.BlockSpec((None, tr, c), lambda i, j, core_ref: (i, j, 0))
    return pl.pallas_call(
        body, name=name,
        grid_spec=pltpu.PrefetchScalarGridSpec(
            num_scalar_prefetch=1, grid=(g, a2 // tr),
            in_specs=[pl.BlockSpec((None, None, tr, c), lambda i, j, core_ref: (i, core_ref[0], j, 0)), blk],
            out_specs=[blk, blk]),
        out_shape=[jax.ShapeDtypeStruct(sib.shape, F32), jax.ShapeDtypeStruct(sib.shape, BF16)],
        compiler_params=_params(("parallel", "parallel")),
    )(core, g4.reshape(g, 2, a2, c), sib)


def _sum_chips(landed, own, chip, *, name):
    g, r, c = landed.shape
    tr = _row_tile(r, c, 5)

    def body(chip_ref, *refs):
        me = chip_ref[0]
        own_v = refs[g][...]
        tot = None
        for slot in range(g):
            term = jnp.where(me == slot, own_v, refs[slot][...].astype(F32))
            tot = term if tot is None else tot + term
        refs[g + 1][...] = tot

    def landed_spec(slot):
        return pl.BlockSpec((None, tr, c),
                            lambda i, chip_ref: (jnp.where(chip_ref[0] == slot, (slot + 1) % g, slot), i, 0))

    return pl.pallas_call(
        body, name=name,
        grid_spec=pltpu.PrefetchScalarGridSpec(
            num_scalar_prefetch=1, grid=(r // tr,),
            in_specs=[landed_spec(k) for k in range(g)]
            + [pl.BlockSpec((None, tr, c), lambda i, chip_ref: (chip_ref[0], i, 0))],
            out_specs=pl.BlockSpec((tr, c), lambda i, chip_ref: (i, 0))),
        out_shape=jax.ShapeDtypeStruct((r, c), F32), compiler_params=_params(("parallel",)),
    )(chip, *([landed] * g), own)


def _join_halves(halves):
    n = len(halves)

    def body(*refs):
        ins, outs, stage = refs[:n], refs[n:2 * n], refs[2 * n:3 * n]
        send_sems, recv_sems, in_sems, out_sems = refs[3 * n:]
        x, y, c = _position()
        remote = []
        staged = _stage_in(ins, stage, in_sems)
        for a in range(n):
            rc = pltpu.make_async_remote_copy(
                src_ref=ins[a], dst_ref=outs[a].at[c], send_sem=send_sems.at[a], recv_sem=recv_sems.at[a],
                device_id=(x, y, 1 - c), device_id_type=MESH)
            rc.start()
            remote.append(rc)
        local = _stage_out(staged, stage, [o.at[c] for o in outs], out_sems)
        for a in range(n):
            pltpu.make_async_remote_copy(
                src_ref=ins[a], dst_ref=outs[a].at[1 - c], send_sem=send_sems.at[a], recv_sem=recv_sems.at[a],
                device_id=(x, y, 1 - c), device_id_type=MESH).wait_recv()
        for rc in remote:
            rc.wait_send()
        for cp in local:
            cp.wait()

    return pl.pallas_call(
        body, name="join_halves", in_specs=[ANY] * n, out_specs=[ANY] * n,
        out_shape=[jax.ShapeDtypeStruct((2,) + h.shape, h.dtype) for h in halves],
        scratch_shapes=[pltpu.VMEM(h.shape, h.dtype) for h in halves]
        + [pltpu.SemaphoreType.DMA((n,)), pltpu.SemaphoreType.DMA((n,)), pltpu.SemaphoreType.DMA((n,)),
           pltpu.SemaphoreType.DMA((n,))],
        compiler_params=pltpu.CompilerParams(has_side_effects=True),
    )(*halves)


_HBM = pl.BlockSpec(memory_space=pltpu.HBM)
_SEM = pl.BlockSpec(memory_space=pltpu.SEMAPHORE)


def _chip_copies(srcs, lands, send_sems, recv_sems, scatter):
    x, y, c = _position()
    me = 2 * x + y
    outgoing, incoming = [], []
    for a, (src, land) in enumerate(zip(srcs, lands)):
        for k, (px, py) in enumerate(_other_chips(x, y)):
            peer = 2 * px + py
            sems = dict(send_sem=send_sems.at[3 * a + k], recv_sem=recv_sems.at[3 * a + k], device_id=(px, py, c),
                        device_id_type=MESH)
            outgoing.append(pltpu.make_async_remote_copy(
                src_ref=src.at[peer] if scatter else src, dst_ref=land.at[me], **sems))
            incoming.append(pltpu.make_async_remote_copy(
                src_ref=src.at[me] if scatter else src, dst_ref=land.at[peer], **sems))
    return outgoing, incoming


def _chips_start(srcs, *, scatter, name):
    n = len(srcs)
    lands = [lax.empty(a.shape if scatter else (N_CHIPS,) + a.shape, a.dtype) for a in srcs]

    def body(*refs):
        ins, send_sems, recv_sems, token = refs[:2 * n], refs[2 * n], refs[2 * n + 1], refs[-1]
        outgoing, _ = _chip_copies(ins[:n], ins[n:], send_sems, recv_sems, scatter)
        for cp in outgoing:
            cp.start()
        token[...] = jnp.zeros_like(token)

    bufs = list(srcs) + lands
    res = pl.pallas_call(
        body, name=name, in_specs=[_HBM] * (2 * n),
        out_specs=(_SEM, _SEM, *[_HBM] * (2 * n), pl.BlockSpec(memory_space=pltpu.VMEM)),
        out_shape=(pltpu.SemaphoreType.DMA((3 * n,)), pltpu.SemaphoreType.DMA((3 * n,)),
                   *[pltpu.HBM(a.shape, a.dtype) for a in bufs], jax.ShapeDtypeStruct((8, 128), F32)),
        input_output_aliases={i: 2 + i for i in range(2 * n)},
        compiler_params=pltpu.CompilerParams(has_side_effects=pltpu.SideEffectType.DATAFLOW_SIDE_EFFECTING),
    )(*[pltpu.with_memory_space_constraint(a, pltpu.HBM) for a in bufs])
    return res[0], res[1], list(res[2:2 + n]), list(res[2 + n:2 + 2 * n]), res[-1]


def _chips_wait(handle, after, *, scatter, name):
    send_sems, recv_sems, srcs, lands, _ = handle
    n = len(srcs)

    def body(*refs):
        ins, send_ref, recv_ref = refs[:2 * n], refs[2 * n], refs[2 * n + 1]
        outgoing, incoming = _chip_copies(ins[:n], ins[n:], send_ref, recv_ref, scatter)
        for cp in outgoing:
            cp.wait_send()
        for cp in incoming:
            cp.wait_recv()

    bufs = list(srcs) + list(lands)
    res = pl.pallas_call(
        body, name=name, in_specs=[_HBM] * (2 * n) + [_SEM, _SEM, ANY], out_specs=tuple([_HBM] * (2 * n)),
        out_shape=tuple(pltpu.HBM(a.shape, a.dtype) for a in bufs),
        input_output_aliases={i: i for i in range(2 * n)},
        compiler_params=pltpu.CompilerParams(has_side_effects=pltpu.SideEffectType.DATAFLOW_SIDE_EFFECTING),
    )(*bufs, send_sems, recv_sems, after)
    return list(res[:n]), list(res[n:])


def _gather_all(buf):
    def body(in_ref, out_ref, send_sems, recv_sems, local_sem):
        x, y, c = _position()
        me = 4 * x + 2 * y + c
        peers = [(x, y, 1 - c)] + [(px, py, pc) for (px, py) in _other_chips(x, y) for pc in (c, 1 - c)]
        cp = pltpu.make_async_copy(in_ref, out_ref.at[me], local_sem)
        cp.start()
        remote = []
        for k, peer in enumerate(peers):
            rc = pltpu.make_async_remote_copy(
                src_ref=in_ref, dst_ref=out_ref.at[me], send_sem=send_sems.at[k], recv_sem=recv_sems.at[k],
                device_id=peer, device_id_type=MESH)
            rc.start()
            remote.append(rc)
        for k, (px, py, pc) in enumerate(peers):
            pltpu.make_async_remote_copy(
                src_ref=in_ref, dst_ref=out_ref.at[4 * px + 2 * py + pc], send_sem=send_sems.at[k],
                recv_sem=recv_sems.at[k], device_id=(px, py, pc), device_id_type=MESH).wait_recv()
        for rc in remote:
            rc.wait_send()
        cp.wait()

    return pl.pallas_call(
        body, name="gather_all", in_specs=[ANY], out_specs=ANY,
        out_shape=jax.ShapeDtypeStruct((N_DEV,) + buf.shape, buf.dtype),
        scratch_shapes=[pltpu.SemaphoreType.DMA((N_DEV - 1,)), pltpu.SemaphoreType.DMA((N_DEV - 1,)),
                        pltpu.SemaphoreType.DMA],
        compiler_params=pltpu.CompilerParams(has_side_effects=True),
    )(buf)


def _cols_from_shards(g4):
    _, k, n = g4.shape
    return g4.transpose(1, 0, 2).reshape(k, N_CHIPS * n)


def _cols_to_shards(w):
    k, n = w.shape
    return w.reshape(k, N_CHIPS, n // N_CHIPS).transpose(1, 0, 2)


def _pad_heads(w, width):
    k = w.shape[0]
    w3 = w.reshape(k, N_HEADS, width)
    return jnp.pad(w3, ((0, 0), (0, 0), (0, HEAD_PAD - width))).reshape(k, D_ATT)


def _unpad_heads(w, width):
    k = w.shape[0]
    return w.reshape(k, N_HEADS, HEAD_PAD)[:, :, :width]


def _rope_tables(s):
    pos = jnp.arange(s, dtype=F32)
    inv_freq = ROPE_THETA ** (-jnp.arange(0, QK_ROPE, 2, dtype=F32) / QK_ROPE)
    ang = pos[:, None] * inv_freq[None, :]
    cos_h, sin_h = jnp.cos(ang), jnp.sin(ang)
    half = QK_ROPE // 2
    z = jnp.zeros((s, half), F32)
    ones = jnp.ones((s, QK_NOPE), F32)
    tail = jnp.zeros((s, HEAD_PAD - QK_NOPE - QK_ROPE), F32)
    cos = jnp.concatenate([ones, cos_h, cos_h, tail + 1.0], axis=1)
    sin_a = jnp.concatenate([ones * 0.0, -sin_h, z, tail], axis=1)
    sin_b = jnp.concatenate([ones * 0.0, z, sin_h, tail], axis=1)
    return cos, sin_a, sin_b


def _local_step(x, p, target, wts, late_weights, reduce_early, reduce_last):
    s = x.shape[0]
    cos, sin_a, sin_b = _rope_tables(s)
    g1, gq, gkv, g2, g3, gf = (wts[k] for k in ("norm_mix_g", "q_norm_g", "kv_norm_g", "norm_ffn_g", "ple_norm_g",
                                                 "final_norm_g"))
    w_in_p, w_uq_p, w_kv_p = wts["w_in_p"], wts["w_uq_p"], wts["w_kv_p"]
    conv_w8, fconv_w, fconv_b = wts["conv_w8"], wts["ffn_conv_w"], wts["ffn_conv_b"]

    (h, z), _ = _mm_fused(x, w_in_p, name="mm_in", prologue=_pro_rms, vecs=[g1], epilogue=_epi_plain, row_outs=[F32])
    y_conv, qn, kvn, kr = _mix_pre(z, conv_w8, gq, gkv, cos, sin_a, sin_b)
    q, k, v, q_t = _qkv_proj(qn, kvn, kr, w_uq_p, w_kv_p, cos, sin_a, sin_b)
    o, lse = _flash_fwd(q, k, v)
    late = late_weights(lse)
    w_o_a, w_o_b, w_up, w_down = late["w_o_a"], late["w_o_b"], late["w_up"], late["w_down"]
    w_pg, w_pp = late["w_ple_gate"], late["w_ple_proj"]
    (x1, hf), _ = _mm_fused(o, w_o_b, second=(y_conv, w_o_a), name="mm_o", rows=[x], vecs=[g2],
                            epilogue=_epi_add_rms, row_outs=[F32, BF16])
    a_pre, act = _ffn_fwd(hf, w_up, fconv_w, fconv_b)
    (x2, n3), _ = _mm_fused(act, w_down, name="mm_down", rows=[x1], vecs=[g3], epilogue=_epi_add_rms,
                            row_outs=[F32, BF16])
    loss, dx3, dgl, dpp, d_gf = _ple_final(x2, n3, p, target, gf, w_pg, w_pp)

    grads, early = {"final_norm_g": d_gf}, {}
    early["w_ple_proj"] = _mm(p, dpp, ta=True, name="mm_d_wpp", tm=256, tn=1024, tk=2048)
    early["w_ple_gate"] = _mm(n3, dgl, ta=True, name="mm_d_wpg", tm=1024, tn=1024, tk=2048)
    (dx2,), (grads["ple_norm_g"],) = _mm_fused(dgl, w_pg, tb=True, name="mm_d_n3", rows=[x2, dx3], vecs=[g3],
                                               epilogue=_epi_rms_bwd, row_outs=[F32], n_vec_out=1)
    early["w_down"] = _mm(act, dx2, ta=True, name="mm_d_wdown", tm=1408, tn=1024, tk=2048)
    da_pre, grads["ffn_conv_w"], grads["ffn_conv_b"] = _ffn_bwd(dx2, w_down, a_pre, fconv_w, fconv_b)
    early["w_up"] = _mm(hf, da_pre, ta=True, b_split=True, name="mm_d_wup", tm=1024, tn=1408, tk=2048,
                       o_shards=True)
    (dx1,), (grads["norm_ffn_g"],) = _mm_fused(da_pre, w_up, tb=True, a_split=True, name="mm_d_hf", rows=[x1, dx2],
                                               vecs=[g2], epilogue=_epi_rms_bwd, row_outs=[F32], n_vec_out=1)
    d_wo_a = _mm(y_conv, dx1, ta=True, name="mm_d_wo_conv", tm=512, tn=1024, tk=2048)
    d_wo_b = _mm(o, dx1, ta=True, name="mm_d_wo_att", tm=1024, tn=1024, tk=2048)
    early["w_o"] = jnp.concatenate([d_wo_a, d_wo_b.reshape(N_HEADS, HEAD_PAD, D_MODEL)[:, :V_HEAD]
                                    .reshape(N_HEADS * V_HEAD, D_MODEL)], axis=0)
    token, finish = reduce_early(early)
    dyc = _mm(dx1, w_o_a, tb=True, name="mm_d_yconv", tm=512, tn=512, tk=1024)
    (do, do_t), _ = _mm_fused(dx1, w_o_b, tb=True, name="mm_d_o", epilogue=_epi_plain, row_outs=[BF16],
                              transposed_out=BF16)
    delta = _attn_delta(do, o)
    dq, dk, dv = _flash_bwd(q, q_t, k, v, do, do_t, lse, delta, token)
    reduced_early = finish(dq)
    dq_pre, dkr = _qk_bwd(dq, dk, cos, sin_a, sin_b)
    grads["w_uq_p"] = _mm(qn, dq_pre, ta=True, name="mm_d_wuq", tm=256, tn=1024, tk=2048)
    dqn = _mm(dq_pre, w_uq_p, tb=True, name="mm_d_qn", tm=512, tn=256, tk=1024)
    grads["w_k_p"] = _mm(kvn, dk, ta=True, name="mm_d_wk", tm=128, tn=1024, tk=2048)
    grads["w_v_p"] = _mm(kvn, dv, ta=True, name="mm_d_wv", tm=128, tn=1024, tk=2048)
    dkvn_k = _mm(dk, w_kv_p[:, :D_ATT], tb=True, name="mm_d_kvn_k", tm=512, tn=128, tk=1024)
    dkvn = _mm(dv, w_kv_p[:, D_ATT:], tb=True, add=dkvn_k, name="mm_d_kvn_v", tm=512, tn=128, tk=1024)
    dz, grads["conv_w"], grads["q_norm_g"], grads["kv_norm_g"] = _mix_bwd(
        z, dyc, dqn, dkvn, dkr, conv_w8, gq, gkv, cos, sin_a, sin_b)
    grads["w_in_p"] = _mm(h, dz, ta=True, name="mm_d_win", tm=1024, tn=1024, tk=2048)
    token, finish = reduce_last({n: grads.pop(n) for n in ("w_in_p", "w_uq_p", "w_k_p", "w_v_p")})
    (grad_x,), (grads["norm_mix_g"],) = _mm_fused(dz, w_in_p, tb=True, name="mm_d_h", rows=[x, dx1],
                                                  vecs=[g1 + token[0, 0]], epilogue=_epi_rms_bwd, row_outs=[F32],
                                                  n_vec_out=1)
    return loss[0, 0], grad_x, grads, reduced_early, finish(grad_x)


_EARLY_W = ("w_in", "w_uq", "w_ukv")
_LATE_W = ("w_o", "w_up", "w_down", "w_ple_gate", "w_ple_proj")
_BIG = _EARLY_W + _LATE_W
_COL_SHARDED = ("w_in", "w_uq", "w_ukv", "w_up", "w_ple_proj")
_SMALL = ("norm_mix_g", "conv_w", "q_norm_g", "kv_norm_g", "norm_ffn_g", "ffn_conv_w", "ffn_conv_b", "ple_norm_g",
          "final_norm_g")


def _full_from_slots(n, g4):
    return _cols_from_shards(g4) if n in _COL_SHARDED else g4.reshape(-1, g4.shape[2])


def _shard_major(n, g):
    if g.ndim == 3:
        return g
    return _cols_to_shards(g) if n in _COL_SHARDED else g.reshape(N_CHIPS, g.shape[0] // N_CHIPS, g.shape[1])


def _early_weights(w):
    shards = [w[n][0].astype(BF16) for n in _EARLY_W]
    shards.append(jnp.pad(w["conv_w"][0], ((0, 5), (0, 0))))
    shards.append(jnp.pad(w["ffn_conv_w"][0], ((0, 5), (0, 0))))
    got = _gather_chips(shards)
    full = {n: _full_from_slots(n, g4) for n, g4 in zip(_EARLY_W, got)}
    full["conv_w8"] = _cols_from_shards(got[len(_EARLY_W)])
    full["ffn_conv_w8"] = _cols_from_shards(got[len(_EARLY_W) + 1])
    return _layout_early(full, w)


def _layout_early(full, w):
    out = {n: w[n] for n in ("norm_mix_g", "q_norm_g", "kv_norm_g", "norm_ffn_g", "ple_norm_g")}
    out["final_norm_g"] = w["final_norm_g"][None, :]
    w_in = full["w_in"]
    zc = jnp.zeros((D_MODEL, QK_NOPE), BF16)
    zt = jnp.zeros((D_MODEL, HEAD_PAD - QK_NOPE - QK_ROPE), BF16)
    out["w_in_p"] = jnp.concatenate([w_in[:, :D_IN - QK_ROPE], zc, w_in[:, D_IN - QK_ROPE:], zt], axis=1)
    out["w_uq_p"] = _pad_heads(full["w_uq"], QK_NOPE + QK_ROPE)
    kv3 = full["w_ukv"].reshape(KV_LORA, N_HEADS, QK_NOPE + V_HEAD)
    out["w_kv_p"] = jnp.concatenate([_pad_heads(kv3[:, :, :QK_NOPE].reshape(KV_LORA, -1), QK_NOPE),
                                     _pad_heads(kv3[:, :, QK_NOPE:].reshape(KV_LORA, -1), V_HEAD)], axis=1)
    out["conv_w8"] = full["conv_w8"]
    fw = full["ffn_conv_w8"]
    out["ffn_conv_w"] = jnp.stack([fw[:, :D_FF], fw[:, D_FF:]])
    out["ffn_conv_b"] = w["ffn_conv_b"].reshape(2, 1, D_FF)
    return out


def _layout_late(full):
    w_o = full["w_o"]
    out = {"w_o_a": w_o[:CONV_WIDTH]}
    out["w_o_b"] = jnp.pad(w_o[CONV_WIDTH:].reshape(N_HEADS, V_HEAD, D_MODEL),
                           ((0, 0), (0, HEAD_PAD - V_HEAD), (0, 0))).reshape(D_ATT, D_MODEL)
    for n in ("w_up", "w_down", "w_ple_gate", "w_ple_proj"):
        out[n] = full[n]
    return out


def _true_matrices(g):
    out = {}
    wp = g["w_in_p"]
    out["w_in"] = jnp.concatenate([wp[:, :D_IN - QK_ROPE], wp[:, D_IN_PAD - HEAD_PAD + QK_NOPE:
                                                              D_IN_PAD - HEAD_PAD + QK_NOPE + QK_ROPE]], axis=1)
    out["w_uq"] = _unpad_heads(g["w_uq_p"], QK_NOPE + QK_ROPE).reshape(Q_LORA, -1)
    out["w_ukv"] = jnp.concatenate([_unpad_heads(g["w_k_p"], QK_NOPE), _unpad_heads(g["w_v_p"], V_HEAD)],
                                   axis=2).reshape(KV_LORA, -1)
    return out


def _true_vectors(g):
    out = {}
    out["conv_w"] = g["conv_w"]
    fw = g["ffn_conv_w"]
    out["ffn_conv_w"] = jnp.concatenate([fw[0, :3], fw[1, :3]], axis=1)
    out["ffn_conv_b"] = g["ffn_conv_b"].reshape(1, 2 * D_FF)
    for n in ("norm_mix_g", "q_norm_g", "kv_norm_g", "norm_ffn_g", "ple_norm_g", "final_norm_g"):
        out[n] = g[n]
    return out


def _chip_partials(names, g, core, *, tag):
    g4 = [_shard_major(n, g[n]) for n in names]
    sib = _send_other_halves(g4, tag=tag)
    return [_add_own_half(a, b, core, name="add_cores_" + n) for n, a, b in zip(names, g4, sib)]


_SMALL_SIZES = {"norm_mix_g": D_MODEL, "conv_w": 3 * CONV_WIDTH, "q_norm_g": Q_LORA, "kv_norm_g": KV_LORA,
                "norm_ffn_g": D_MODEL, "ffn_conv_w": 6 * D_FF, "ffn_conv_b": 2 * D_FF, "ple_norm_g": D_MODEL,
                "final_norm_g": D_MODEL}


def _pack(parts, rows):
    flat = jnp.concatenate([a.reshape(-1) for a in parts])
    return jnp.pad(flat, (0, rows * 128 - flat.shape[0])).reshape(rows, 128)


def _unpack(buf, sizes):
    flat = buf.reshape(-1)
    out, at = [], 0
    for n in sizes:
        out.append(flat[at:at + n])
        at += n
    return out


def _reduce_small(g, loss):
    sizes = [1] + [_SMALL_SIZES[n] for n in _SMALL]
    rows = -(-sum(sizes) // 1024) * 8
    slots = _gather_all(_pack([loss] + [g[n] for n in _SMALL], rows))
    parts = _unpack(_sum_slots(slots, name="sum_small"), sizes)
    return parts[0][0], dict(zip(_SMALL, parts[1:]))


def kernel(x, p, norm_mix_g, w_in, conv_w, q_norm_g, w_uq, kv_norm_g, w_ukv, w_o, norm_ffn_g, w_up, ffn_conv_w, ffn_conv_b, w_down, ple_norm_g, w_ple_gate, w_ple_proj, final_norm_g, loss_target, m_norm_mix_g, m_w_in, m_conv_w, m_q_norm_g, m_w_uq, m_kv_norm_g, m_w_ukv, m_w_o, m_norm_ffn_g, m_w_up, m_ffn_conv_w, m_ffn_conv_b, m_w_down, m_ple_norm_g, m_w_ple_gate, m_w_ple_proj, m_final_norm_g, v_norm_mix_g, v_w_in, v_conv_w, v_q_norm_g, v_w_uq, v_kv_norm_g, v_w_ukv, v_w_o, v_norm_ffn_g, v_w_up, v_ffn_conv_w, v_ffn_conv_b, v_w_down, v_ple_norm_g, v_w_ple_gate, v_w_ple_proj, v_final_norm_g):
    names = ["norm_mix_g", "w_in", "conv_w", "q_norm_g", "w_uq", "kv_norm_g", "w_ukv", "w_o", "norm_ffn_g", "w_up",
             "ffn_conv_w", "ffn_conv_b", "w_down", "ple_norm_g", "w_ple_gate", "w_ple_proj", "final_norm_g"]
    w = dict(zip(names, (norm_mix_g, w_in, conv_w, q_norm_g, w_uq, kv_norm_g, w_ukv, w_o, norm_ffn_g, w_up,
                         ffn_conv_w, ffn_conv_b, w_down, ple_norm_g, w_ple_gate, w_ple_proj, final_norm_g)))
    m = dict(zip(names, (m_norm_mix_g, m_w_in, m_conv_w, m_q_norm_g, m_w_uq, m_kv_norm_g, m_w_ukv, m_w_o,
                         m_norm_ffn_g, m_w_up, m_ffn_conv_w, m_ffn_conv_b, m_w_down, m_ple_norm_g, m_w_ple_gate,
                         m_w_ple_proj, m_final_norm_g)))
    v = dict(zip(names, (v_norm_mix_g, v_w_in, v_conv_w, v_q_norm_g, v_w_uq, v_kv_norm_g, v_w_ukv, v_w_o,
                         v_norm_ffn_g, v_w_up, v_ffn_conv_w, v_ffn_conv_b, v_w_down, v_ple_norm_g, v_w_ple_gate,
                         v_w_ple_proj, v_final_norm_g)))

    core = lax.axis_index("c").astype(jnp.int32).reshape(1)
    chip = (2 * lax.axis_index("x") + lax.axis_index("y")).astype(jnp.int32).reshape(1)

    wts = _early_weights(w)
    gather = _chips_start([w[n][0].astype(BF16) for n in _LATE_W], scatter=False, name="gather_late_start")
    wts["norm_mix_g"] = wts["norm_mix_g"] + gather[4][0, 0]

    def late_weights(after):
        shards, landed = _chips_wait(gather, after, scatter=False, name="gather_late_wait")
        full = {n: _full_from_slots(n, lax.dynamic_update_slice(g4, own[None], (chip[0], 0, 0)))
                for n, own, g4 in zip(_LATE_W, shards, landed)}
        return _layout_late(full)

    def reduce_early(g):
        parts = _chip_partials(_LATE_W, g, core, tag="early")
        scatter = _chips_start([t16 for _, t16 in parts], scatter=True, name="scatter_early_start")

        def finish(after):
            _, landed = _chips_wait(scatter, after, scatter=True, name="scatter_early_wait")
            return [_sum_chips(a, t32, chip, name="sum_chips_" + n) for n, a, (t32, _) in zip(_LATE_W, landed, parts)]

        return scatter[4], finish

    def reduce_last(g):
        parts = _chip_partials(_EARLY_W, _true_matrices(g), core, tag="late")
        scatter = _chips_start([t16 for _, t16 in parts], scatter=True, name="scatter_late_start")

        def finish(after):
            _, landed = _chips_wait(scatter, after, scatter=True, name="scatter_late_wait")
            return [_sum_chips(a, t32, chip, name="sum_chips_" + n) for n, a, (t32, _) in zip(_EARLY_W, landed, parts)]

        return scatter[4], finish

    loss, grad_x, small_grads, halves_early, halves_last = _local_step(
        x[0], p[0, 0], loss_target[0], wts, late_weights, reduce_early, reduce_last)
    g_full = _true_vectors(small_grads)
    whole = _join_halves(halves_last + halves_early)
    big = {n: a.reshape(-1, a.shape[2]) for n, a in zip(_BIG, whole)}

    g_out, d_out, m_out, v_out = {}, {}, {}, {}
    for n in _BIG:
        shape = w[n].shape
        g = big[n]
        d, mn, vn = _adamw(w[n][0], g, m[n][0], v[n][0], name="adamw_" + n)
        g_out[n], d_out[n], m_out[n], v_out[n] = (a.reshape(shape) for a in (g, d, mn, vn))

    loss, small = _reduce_small(g_full, loss)
    chip = 2 * lax.axis_index("x") + lax.axis_index("y")
    g_small = {}
    for n in _SMALL:
        shape = w[n].shape
        g = small[n]
        if n in ("conv_w", "ffn_conv_w"):
            width = shape[-1]
            g = lax.dynamic_slice(g.reshape(3, N_CHIPS * width), (0, chip * width), (3, width))
        g_small[n] = g.reshape(shape)
    sizes = [g_small[n].size for n in _SMALL]
    rows = -(-sum(sizes) // 1024) * 8
    packed = [_pack([src[n] for n in _SMALL], rows) for src in (w, g_small, m, v)]
    d_s, m_s, v_s = _adamw(*packed, name="adamw_small")
    for n, d, mn, vn in zip(_SMALL, _unpack(d_s, sizes), _unpack(m_s, sizes), _unpack(v_s, sizes)):
        shape = w[n].shape
        g_out[n], d_out[n], m_out[n], v_out[n] = g_small[n], d.reshape(shape), mn.reshape(shape), vn.reshape(shape)

    return (loss, grad_x[None], *[g_out[n] for n in names], *[d_out[n] for n in names],
            *[m_out[n] for n in names], *[v_out[n] for n in names])
```

```python
import functools

import jax
import jax.numpy as jnp
from jax import lax
from jax.experimental import pallas as pl
from jax.experimental.pallas import tpu as pltpu

F32 = jnp.float32
BF16 = jnp.bfloat16

D_MODEL = 1024
CONV_WIDTH = 512
Q_LORA = 256
KV_LORA = 128
QK_NOPE = 64
QK_ROPE = 32
V_HEAD = 64
N_HEADS = 8
HEAD_PAD = 128
D_ATT = N_HEADS * HEAD_PAD
D_IN = 3 * CONV_WIDTH + Q_LORA + KV_LORA + QK_ROPE
D_IN_PAD = 3 * CONV_WIDTH + Q_LORA + KV_LORA + HEAD_PAD
D_FF = 2816
ROPE_THETA = 10000.0
EPS = 1e-6
SM_SCALE = (QK_NOPE + QK_ROPE) ** -0.5
ONES_LANE = V_HEAD

ADAM_LR = 0.001
ADAM_B1 = 0.9
ADAM_B2 = 0.999
ADAM_EPS = 1e-08
ADAM_WD = 0.01
ADAM_STEP = 10

N_CHIPS = 4
N_DEV = 8
MESH = pl.DeviceIdType.MESH
ANY = pl.BlockSpec(memory_space=pl.ANY)


def _params(sem):
    return pltpu.CompilerParams(dimension_semantics=sem)


MM_PIECE = 256


def _pieces(total, width=MM_PIECE):
    return [(off, min(width, total - off)) for off in range(0, total, width)]


def _mm(a, b, *, name, ta=False, tb=False, add=None, out_dtype=F32, tm=512, tn=512, tk=512, b_split=False,
        o_shards=False):
    k, m = a.shape if ta else a.shape[::-1]
    if b_split:
        _, kb, nh = b.shape
        n = 2 * nh
    elif tb:
        n, kb = b.shape
    else:
        kb, n = b.shape
    assert kb == k, (name, a.shape, b.shape)
    tm, tn, tk = min(tm, m), min(tn, n), min(tk, k)
    assert m % tm == 0 and n % tn == 0 and k % tk == 0, (name, m, n, k, tm, tn, tk)
    gm, gn, gk = m // tm, n // tn, k // tk

    a_spec = pl.BlockSpec((tk, tm), lambda i, j, kk: (kk, i)) if ta else pl.BlockSpec((tm, tk), lambda i, j, kk: (i, kk))
    if b_split:
        assert gn % 2 == 0
        b_spec = pl.BlockSpec((None, tk, tn), lambda i, j, kk: (j // (gn // 2), kk, j % (gn // 2)))
    elif tb:
        b_spec = pl.BlockSpec((tn, tk), lambda i, j, kk: (j, kk))
    else:
        b_spec = pl.BlockSpec((tk, tn), lambda i, j, kk: (kk, j))
    if o_shards:
        o_spec = pl.BlockSpec((None, tm, tn), lambda i, j, kk: (j, i, 0))
        o_shape = jax.ShapeDtypeStruct((gn, m, tn), out_dtype)
    else:
        o_spec = pl.BlockSpec((tm, tn), lambda i, j, kk: (i, j))
        o_shape = jax.ShapeDtypeStruct((m, n), out_dtype)
    dims = (((0 if ta else 1,), (1 if tb else 0,)), ((), ()))

    def body(*refs):
        a_ref, b_ref = refs[:2]
        add_ref = None if add is None else refs[2]
        o_ref = refs[2 if add is None else 3]
        acc_ref = None if gk == 1 else refs[-1]
        kk = pl.program_id(2)
        rhs = b_ref[...].astype(BF16)

        def finish(r, rows):
            if add_ref is not None:
                r = r + add_ref[rows, :]
            o_ref[rows, :] = r.astype(o_ref.dtype)

        for off, size in _pieces(tm):
            rows = slice(off, off + size)
            lhs = (a_ref[:, rows] if ta else a_ref[rows, :]).astype(BF16)
            part = lax.dot_general(lhs, rhs, dims, preferred_element_type=F32)
            if gk == 1:
                finish(part, rows)
            else:
                acc_ref[rows, :] = jnp.where(kk == 0, part, acc_ref[rows, :] + part)

        if gk > 1:
            @pl.when(kk == gk - 1)
            def _():
                finish(acc_ref[...], slice(None))

    in_specs = [a_spec, b_spec]
    args = [a, b]
    if add is not None:
        in_specs.append(pl.BlockSpec((tm, tn), lambda i, j, kk: (i, j)))
        args.append(add)
    return pl.pallas_call(
        body, name=name, grid=(gm, gn, gk), in_specs=in_specs, out_specs=o_spec, out_shape=o_shape,
        scratch_shapes=[] if gk == 1 else [pltpu.VMEM((tm, tn), F32)],
        compiler_params=_params(("parallel", "parallel", "arbitrary")),
    )(*args)


def _rms_scale(v):
    return lax.rsqrt(jnp.mean(v * v, axis=-1, keepdims=True) + EPS)


def _rms_bwd_rows(v, g, dy):
    r = _rms_scale(v)
    vh = v * r
    dyg = dy * g
    dv = r * (dyg - vh * jnp.mean(dyg * vh, axis=-1, keepdims=True))
    return dv, dy * vh


def _shift_down(v, first_row):
    row = lax.broadcasted_iota(jnp.int32, v.shape, 0)
    return jnp.where(row == 0, first_row, pltpu.roll(v, 1, 0))


def _shift_up(v, last_row):
    n = v.shape[0]
    row = lax.broadcasted_iota(jnp.int32, v.shape, 0)
    return jnp.where(row == n - 1, last_row, pltpu.roll(v, n - 1, 0))


def _rope(t, cos, sin_a, sin_b):
    return t * cos + pltpu.roll(t, HEAD_PAD - 16, 1) * sin_a + pltpu.roll(t, 16, 1) * sin_b


def _rope_bwd(d, cos, sin_a, sin_b):
    return d * cos + pltpu.roll(d * sin_a, 16, 1) + pltpu.roll(d * sin_b, HEAD_PAD - 16, 1)


def _sigmoid(v):
    return 1.0 / (1.0 + jnp.exp(-v))


def _halo_specs(ts, s, width, col):
    nb = ts // 8
    last = s // 8 - 1
    prev = pl.BlockSpec((8, width), lambda i: (jnp.maximum(i * nb - 1, 0), col))
    nxt = pl.BlockSpec((8, width), lambda i: (jnp.minimum((i + 1) * nb, last), col))
    return prev, nxt


def _mm_fused(a, b, *, name, epilogue, row_outs, rows=(), vecs=(), n_vec_out=0, tb=False, a_split=False,
              prologue=None, second=None, transposed_out=None, tm=512):
    if a_split:
        _, m, kh = a.shape
        k = 2 * kh
    else:
        m, k = a.shape
    n = b.shape[0] if tb else b.shape[1]
    assert (b.shape[1] if tb else b.shape[0]) == k, (name, a.shape, b.shape)
    assert m % tm == 0, (name, m, tm)
    n_a = 2 if a_split else 1
    nr, nv = len(rows), len(vecs)
    n_pro = 0 if prologue is None else 1
    n_sec = 0 if second is None else 2
    n_t = 0 if transposed_out is None else 1
    dims = (((1,), (1 if tb else 0,)), ((), ()))

    def body(*refs):
        a_refs, b_ref = refs[:n_a], refs[n_a]
        refs = refs[n_a + 1:]
        sec_refs = refs[:n_sec]
        row_refs, vec_refs = refs[n_sec:n_sec + nr], refs[n_sec + nr:n_sec + nr + nv]
        outs = refs[n_sec + nr + nv:]
        row_out_refs = outs[n_pro:n_pro + len(row_outs)]
        t_out_refs = outs[n_pro + len(row_outs):n_pro + len(row_outs) + n_t]
        vec_out_refs = outs[n_pro + len(row_outs) + n_t:n_pro + len(row_outs) + n_t + n_vec_out]
        vec_vals = [v[...] for v in vec_refs]
        if a_split:
            kh = k // 2
            rhs = [(b_ref[:, :kh], b_ref[:, kh:]) if tb else (b_ref[:kh, :], b_ref[kh:, :])][0]
            rhs = [h.astype(BF16) for h in rhs]
        else:
            rhs = [b_ref[...].astype(BF16)]
        vec_sums = [None] * n_vec_out

        for off, size in _pieces(tm):
            rs = slice(off, off + size)
            if prologue is None:
                lhs = [a_ref[rs, :].astype(BF16) for a_ref in a_refs]
            else:
                lhs = [prologue(a_refs[0][rs, :], vec_vals)]
                outs[0][rs, :] = lhs[0]
            r = lax.dot_general(lhs[0], rhs[0], dims, preferred_element_type=F32)
            for l2, r2 in zip(lhs[1:], rhs[1:]):
                r = r + lax.dot_general(l2, r2, dims, preferred_element_type=F32)
            if second is not None:
                r = r + jnp.dot(sec_refs[0][rs, :].astype(BF16), sec_refs[1][...].astype(BF16),
                                preferred_element_type=F32)
            row_vals, vec_parts = epilogue(r, [x[rs, :] for x in row_refs], vec_vals)
            for ref, val in zip(row_out_refs, row_vals):
                ref[rs, :] = val.astype(ref.dtype)
            for ref in t_out_refs:
                ref[:, rs] = row_vals[0].T.astype(ref.dtype)
            vec_sums = [p if t is None else t + p for t, p in zip(vec_sums, vec_parts)]

        if n_vec_out:
            @pl.when(pl.program_id(0) == 0)
            def _():
                for ref in vec_out_refs:
                    ref[...] = jnp.zeros_like(ref)

            for ref, val in zip(vec_out_refs, vec_sums):
                ref[...] += val

    if a_split:
        a_specs = [pl.BlockSpec((None, tm, k // 2), lambda i: (0, i, 0)),
                   pl.BlockSpec((None, tm, k // 2), lambda i: (1, i, 0))]
    else:
        a_specs = [pl.BlockSpec((tm, k), lambda i: (i, 0))]
    b_spec = pl.BlockSpec(b.shape, lambda i: (0, 0))
    row_spec = pl.BlockSpec((tm, n), lambda i: (i, 0))
    out_specs, out_shape = [], []
    if prologue is not None:
        out_specs.append(pl.BlockSpec((tm, k), lambda i: (i, 0)))
        out_shape.append(jax.ShapeDtypeStruct((m, k), BF16))
    out_specs += [row_spec] * len(row_outs)
    out_shape += [jax.ShapeDtypeStruct((m, n), dt) for dt in row_outs]
    if transposed_out is not None:
        out_specs.append(pl.BlockSpec((n, tm), lambda i: (0, i)))
        out_shape.append(jax.ShapeDtypeStruct((n, m), transposed_out))
    out_specs += [pl.BlockSpec((1, n), lambda i: (0, 0))] * n_vec_out
    out_shape += [jax.ShapeDtypeStruct((1, n), F32)] * n_vec_out
    sec_specs, sec_args = [], []
    if second is not None:
        k2 = second[0].shape[1]
        sec_specs = [pl.BlockSpec((tm, k2), lambda i: (i, 0)), pl.BlockSpec((k2, n), lambda i: (0, 0))]
        sec_args = list(second)
    res = pl.pallas_call(
        body, name=name, grid=(m // tm,),
        in_specs=a_specs + [b_spec] + sec_specs + [row_spec] * nr
        + [pl.BlockSpec((1, v.shape[1]), lambda i: (0, 0)) for v in vecs],
        out_specs=out_specs, out_shape=out_shape,
        compiler_params=_params(("arbitrary" if n_vec_out else "parallel",)),
    )(*([a] * n_a), b, *sec_args, *rows, *vecs)
    split = n_pro + len(row_outs) + n_t
    return list(res[:split]), list(res[split:])


def _pro_rms(a, vecs):
    return (a * _rms_scale(a) * vecs[0]).astype(BF16)


def _epi_plain(r, rows, vecs):
    return [r], []


def _epi_add_rms(r, rows, vecs):
    xn = r + rows[0]
    return [xn, xn * _rms_scale(xn) * vecs[0]], []


def _epi_rms_bwd(r, rows, vecs):
    dv, dg_rows = _rms_bwd_rows(rows[0], vecs[0], r)
    return [dv + rows[1]], [jnp.sum(dg_rows, axis=0, keepdims=True)]


def _mix_pre(z, conv_w8, gq, gkv, cos, sin_a, sin_b, *, ts=256):
    s = z.shape[0]
    n = s // ts
    cw = CONV_WIDTH

    def body(z_ref, xcp, xcn, cgp, cgn, w_ref, gq_ref, gkv_ref, cos_ref, sa_ref, sb_ref,
             yc_ref, qn_ref, kvn_ref, kr_ref):
        i = pl.program_id(0)
        xc = z_ref[:, 0:cw]
        bg = z_ref[:, cw:2 * cw]
        cg = z_ref[:, 2 * cw:3 * cw]
        m = cg * xc
        m_prev = jnp.where(i > 0, xcp[7:8, :] * cgp[7:8, :], 0.0)
        m_next = jnp.where(i < n - 1, xcn[0:1, :] * cgn[0:1, :], 0.0)
        cm = _shift_down(m, m_prev) * w_ref[0:1, :] + m * w_ref[1:2, :] + _shift_up(m, m_next) * w_ref[2:3, :]
        yc_ref[...] = (bg * cm).astype(BF16)
        ql = z_ref[:, 3 * cw:3 * cw + Q_LORA]
        qn_ref[...] = (ql * _rms_scale(ql) * gq_ref[...]).astype(BF16)
        kvl = z_ref[:, 3 * cw + Q_LORA:3 * cw + Q_LORA + KV_LORA]
        kvn_ref[...] = (kvl * _rms_scale(kvl) * gkv_ref[...]).astype(BF16)
        kr_ref[...] = _rope(z_ref[:, D_IN_PAD - HEAD_PAD:D_IN_PAD], cos_ref[...], sa_ref[...], sb_ref[...])

    xcp, xcn = _halo_specs(ts, s, cw, 0)
    cgp, cgn = _halo_specs(ts, s, cw, 2)
    tab = pl.BlockSpec((ts, HEAD_PAD), lambda i: (i, 0))
    return pl.pallas_call(
        body, name="mix_pre", grid=(n,),
        in_specs=[pl.BlockSpec((ts, D_IN_PAD), lambda i: (i, 0)), xcp, xcn, cgp, cgn,
                  pl.BlockSpec((8, cw), lambda i: (0, 0)), pl.BlockSpec((1, Q_LORA), lambda i: (0, 0)),
                  pl.BlockSpec((1, KV_LORA), lambda i: (0, 0)), tab, tab, tab],
        out_specs=[pl.BlockSpec((ts, cw), lambda i: (i, 0)), pl.BlockSpec((ts, Q_LORA), lambda i: (i, 0)),
                   pl.BlockSpec((ts, KV_LORA), lambda i: (i, 0)), tab],
        out_shape=[jax.ShapeDtypeStruct((s, cw), BF16), jax.ShapeDtypeStruct((s, Q_LORA), BF16),
                   jax.ShapeDtypeStruct((s, KV_LORA), BF16), jax.ShapeDtypeStruct((s, HEAD_PAD), F32)],
        compiler_params=_params(("parallel",)),
    )(z, z, z, z, z, conv_w8, gq, gkv, cos, sin_a, sin_b)


def _mix_bwd(z, dyc, dqn, dkvn, dkr, conv_w8, gq, gkv, cos, sin_a, sin_b, *, ts=256):
    s = z.shape[0]
    n = s // ts
    cw = CONV_WIDTH

    def body(z_ref, xcp, xcn, bgp, bgn, cgp, cgn, dyc_ref, dycp, dycn, dqn_ref, dkvn_ref, dkr_ref,
             w_ref, gq_ref, gkv_ref, cos_ref, sa_ref, sb_ref,
             dz_ref, dw0_ref, dw1_ref, dw2_ref, dgq_ref, dgkv_ref):
        i = pl.program_id(0)

        @pl.when(i == 0)
        def _():
            for r in (dw0_ref, dw1_ref, dw2_ref, dgq_ref, dgkv_ref):
                r[...] = jnp.zeros_like(r)

        xc = z_ref[:, 0:cw]
        bg = z_ref[:, cw:2 * cw]
        cg = z_ref[:, 2 * cw:3 * cw]
        w0, w1, w2 = w_ref[0:1, :], w_ref[1:2, :], w_ref[2:3, :]
        m = cg * xc
        m_dn = _shift_down(m, jnp.where(i > 0, xcp[7:8, :] * cgp[7:8, :], 0.0))
        m_up = _shift_up(m, jnp.where(i < n - 1, xcn[0:1, :] * cgn[0:1, :], 0.0))
        cm = m_dn * w0 + m * w1 + m_up * w2
        dyc_v = dyc_ref[...]
        dcm = dyc_v * bg
        dcm_dn = _shift_down(dcm, jnp.where(i > 0, dycp[7:8, :] * bgp[7:8, :], 0.0))
        dcm_up = _shift_up(dcm, jnp.where(i < n - 1, dycn[0:1, :] * bgn[0:1, :], 0.0))
        dm = dcm_up * w0 + dcm * w1 + dcm_dn * w2
        dz_ref[:, 0:cw] = (dm * cg).astype(BF16)
        dz_ref[:, cw:2 * cw] = (dyc_v * cm).astype(BF16)
        dz_ref[:, 2 * cw:3 * cw] = (dm * xc).astype(BF16)
        dw0_ref[...] += jnp.sum(dcm * m_dn, axis=0, keepdims=True)
        dw1_ref[...] += jnp.sum(dcm * m, axis=0, keepdims=True)
        dw2_ref[...] += jnp.sum(dcm * m_up, axis=0, keepdims=True)

        dql, dgq_rows = _rms_bwd_rows(z_ref[:, 3 * cw:3 * cw + Q_LORA], gq_ref[...], dqn_ref[...])
        dz_ref[:, 3 * cw:3 * cw + Q_LORA] = dql.astype(BF16)
        dgq_ref[...] += jnp.sum(dgq_rows, axis=0, keepdims=True)
        dkvl, dgkv_rows = _rms_bwd_rows(z_ref[:, 3 * cw + Q_LORA:3 * cw + Q_LORA + KV_LORA], gkv_ref[...],
                                        dkvn_ref[...])
        dz_ref[:, 3 * cw + Q_LORA:3 * cw + Q_LORA + KV_LORA] = dkvl.astype(BF16)
        dgkv_ref[...] += jnp.sum(dgkv_rows, axis=0, keepdims=True)

        lane = lax.broadcasted_iota(jnp.int32, (ts, HEAD_PAD), 1)
        rope_lane = (lane >= QK_NOPE) & (lane < QK_NOPE + QK_ROPE)
        dk = _rope_bwd(dkr_ref[...], cos_ref[...], sa_ref[...], sb_ref[...])
        dz_ref[:, D_IN_PAD - HEAD_PAD:D_IN_PAD] = jnp.where(rope_lane, dk, 0.0).astype(BF16)

    xcp, xcn = _halo_specs(ts, s, cw, 0)
    bgp, bgn = _halo_specs(ts, s, cw, 1)
    cgp, cgn = _halo_specs(ts, s, cw, 2)
    dycp, dycn = _halo_specs(ts, s, cw, 0)
    tab = pl.BlockSpec((ts, HEAD_PAD), lambda i: (i, 0))

    def vec(width):
        return pl.BlockSpec((1, width), lambda i: (0, 0))

    outs = pl.pallas_call(
        body, name="mix_bwd", grid=(n,),
        in_specs=[pl.BlockSpec((ts, D_IN_PAD), lambda i: (i, 0)), xcp, xcn, bgp, bgn, cgp, cgn,
                  pl.BlockSpec((ts, cw), lambda i: (i, 0)), dycp, dycn,
                  pl.BlockSpec((ts, Q_LORA), lambda i: (i, 0)), pl.BlockSpec((ts, KV_LORA), lambda i: (i, 0)), tab,
                  pl.BlockSpec((8, cw), lambda i: (0, 0)), vec(Q_LORA), vec(KV_LORA), tab, tab, tab],
        out_specs=[pl.BlockSpec((ts, D_IN_PAD), lambda i: (i, 0)), vec(cw), vec(cw), vec(cw), vec(Q_LORA),
                   vec(KV_LORA)],
        out_shape=[jax.ShapeDtypeStruct((s, D_IN_PAD), BF16)] + [jax.ShapeDtypeStruct((1, cw), F32)] * 3
        + [jax.ShapeDtypeStruct((1, Q_LORA), F32), jax.ShapeDtypeStruct((1, KV_LORA), F32)],
        compiler_params=_params(("arbitrary",)),
    )(z, z, z, z, z, z, z, dyc, dyc, dyc, dqn, dkvn, dkr, conv_w8, gq, gkv, cos, sin_a, sin_b)
    dz, dw0, dw1, dw2, dgq, dgkv = outs
    return dz, jnp.concatenate([dw0, dw1, dw2], axis=0), dgq, dgkv


def _qkv_proj(qn, kvn, kr, w_uq_p, w_kv_p, cos, sin_a, sin_b, *, ts=512):
    s = qn.shape[0]

    def body(qn_ref, kvn_ref, kr_ref, wq_ref, wkv_ref, cos_ref, sa_ref, sb_ref, q_ref, k_ref, v_ref, qt_ref):
        cos_v, sa, sb = cos_ref[...], sa_ref[...], sb_ref[...]
        q = jnp.dot(qn_ref[...], wq_ref[...], preferred_element_type=F32)
        kv = jnp.dot(kvn_ref[...], wkv_ref[...], preferred_element_type=F32)
        kr_v = kr_ref[...]
        lane = lax.broadcasted_iota(jnp.int32, (1, HEAD_PAD), 1)
        ones_lane = (lane == ONES_LANE).astype(F32)
        for h in range(N_HEADS):
            blk = slice(h * HEAD_PAD, (h + 1) * HEAD_PAD)
            q_h = _rope(q[:, blk], cos_v, sa, sb) * SM_SCALE
            q_ref[:, blk] = q_h.astype(BF16)
            qt_ref[blk, :] = q_h.T.astype(BF16)
            k_ref[:, blk] = (kv[:, blk] + kr_v).astype(BF16)
            v_ref[:, blk] = (kv[:, D_ATT + h * HEAD_PAD:D_ATT + (h + 1) * HEAD_PAD] + ones_lane).astype(BF16)

    tab = pl.BlockSpec((ts, HEAD_PAD), lambda i: (i, 0))
    wide = pl.BlockSpec((ts, D_ATT), lambda i: (i, 0))
    return pl.pallas_call(
        body, name="qkv_proj", grid=(s // ts,),
        in_specs=[pl.BlockSpec((ts, Q_LORA), lambda i: (i, 0)), pl.BlockSpec((ts, KV_LORA), lambda i: (i, 0)), tab,
                  pl.BlockSpec((Q_LORA, D_ATT), lambda i: (0, 0)), pl.BlockSpec((KV_LORA, 2 * D_ATT), lambda i: (0, 0)),
                  tab, tab, tab],
        out_specs=[wide, wide, wide, pl.BlockSpec((D_ATT, ts), lambda i: (0, i))],
        out_shape=[jax.ShapeDtypeStruct((s, D_ATT), BF16)] * 3 + [jax.ShapeDtypeStruct((D_ATT, s), BF16)],
        compiler_params=_params(("parallel",)),
    )(qn, kvn, kr, w_uq_p, w_kv_p, cos, sin_a, sin_b)


def _qk_bwd(dq, dk, cos, sin_a, sin_b, *, ts=256):
    s = dq.shape[0]

    def body(dq_ref, dk_ref, cos_ref, sa_ref, sb_ref, dqp_ref, dkr_ref):
        cos_v, sa, sb = cos_ref[...], sa_ref[...], sb_ref[...]
        tot = jnp.zeros((ts, HEAD_PAD), F32)
        for h in range(N_HEADS):
            blk = slice(h * HEAD_PAD, (h + 1) * HEAD_PAD)
            dqp_ref[:, blk] = _rope_bwd(dq_ref[:, blk], cos_v, sa, sb).astype(BF16)
            tot = tot + dk_ref[:, blk]
        dkr_ref[...] = tot

    tab = pl.BlockSpec((ts, HEAD_PAD), lambda i: (i, 0))
    wide = pl.BlockSpec((ts, D_ATT), lambda i: (i, 0))
    return pl.pallas_call(
        body, name="qk_bwd", grid=(s // ts,),
        in_specs=[wide, wide, tab, tab, tab], out_specs=[wide, tab],
        out_shape=[jax.ShapeDtypeStruct((s, D_ATT), BF16), jax.ShapeDtypeStruct((s, HEAD_PAD), F32)],
        compiler_params=_params(("parallel",)),
    )(dq, dk, cos, sin_a, sin_b)


_NT = (((1,), (1,)), ((), ()))


def _flash_fwd(q, k, v, *, tq=1024, tk=1024, per_trip=8):
    s = q.shape[0]
    tq, tk = min(tq, s), min(tk, s)
    nk = s // tk
    per_trip = min(per_trip, nk)
    assert nk % per_trip == 0

    def body(q_ref, k_ref, v_ref, o_ref, lse_ref):
        qv = q_ref[...]

        def step(j, carry):
            m, acc = carry
            rows = pl.ds(pl.multiple_of(j * tk, tk), tk)
            sc = lax.dot_general(qv, k_ref[rows, :], _NT, preferred_element_type=F32)
            m_new = jnp.maximum(m, jnp.max(sc, axis=1, keepdims=True))
            p = jnp.exp(sc - m_new).astype(BF16)
            acc = jnp.exp(m - m_new) * acc + jnp.dot(p, v_ref[rows, :], preferred_element_type=F32)
            return m_new, acc

        def trip(t, carry):
            for c in range(per_trip):
                carry = step(per_trip * t + c, carry)
            return carry

        init = (jnp.full((tq, 1), -jnp.inf, F32), jnp.zeros((tq, HEAD_PAD), F32))
        m, acc = lax.fori_loop(0, nk // per_trip, trip, init)
        l = acc[:, ONES_LANE:ONES_LANE + 1]
        o_ref[...] = (acc / l).astype(BF16)
        lse_ref[...] = m + jnp.log(l)

    head = pl.BlockSpec((s, HEAD_PAD), lambda h, i: (0, h))
    return pl.pallas_call(
        body, name="flash_fwd", grid=(N_HEADS, s // tq),
        in_specs=[pl.BlockSpec((tq, HEAD_PAD), lambda h, i: (i, h)), head, head],
        out_specs=[pl.BlockSpec((tq, HEAD_PAD), lambda h, i: (i, h)),
                   pl.BlockSpec((None, tq, 1), lambda h, i: (h, i, 0))],
        out_shape=[jax.ShapeDtypeStruct((s, D_ATT), BF16), jax.ShapeDtypeStruct((N_HEADS, s, 1), F32)],
        compiler_params=_params(("parallel", "parallel")),
    )(q, k, v)


def _attn_delta(do, o, *, ts=512):
    s = do.shape[0]

    def body(do_ref, o_ref, dl_ref):
        for h in range(N_HEADS):
            blk = slice(h * HEAD_PAD, (h + 1) * HEAD_PAD)
            dl_ref[h] = jnp.sum(do_ref[:, blk].astype(F32) * o_ref[:, blk].astype(F32), axis=1, keepdims=True)

    wide = pl.BlockSpec((ts, D_ATT), lambda i: (i, 0))
    return pl.pallas_call(
        body, name="attn_delta", grid=(s // ts,), in_specs=[wide, wide],
        out_specs=pl.BlockSpec((N_HEADS, ts, 1), lambda i: (0, i, 0)),
        out_shape=jax.ShapeDtypeStruct((N_HEADS, s, 1), F32),
        compiler_params=_params(("parallel",)),
    )(do, o)


def _flash_bwd(q, qt, k, v, do, dot, lse, delta, after, *, tq=1024, tk=512, per_trip=8):
    s = q.shape[0]
    tq, tk = min(tq, s), min(tk, s)
    nq = s // tq
    per_trip = min(per_trip, nq)
    assert nq % per_trip == 0

    def body(q_ref, qt_ref, do_ref, dot_ref, lse_ref, dl_ref, k_ref, v_ref, after_ref, dq_ref, dk_ref, dv_ref):
        j = pl.program_id(1)

        @pl.when(j == 0)
        def _():
            dq_ref[...] = jnp.zeros_like(dq_ref)

        kv, vv = k_ref[...], v_ref[...]

        def chunk(i, dk_t, dv_t):
            at = pl.multiple_of(i * tq, tq)
            rows = pl.ds(at, tq)
            sc = lax.dot_general(q_ref[rows, :], kv, _NT, preferred_element_type=F32)
            p = jnp.exp(sc - lse_ref[rows, :])
            dp = lax.dot_general(do_ref[rows, :], vv, _NT, preferred_element_type=F32)
            ds = (p * (dp - dl_ref[rows, :])).astype(BF16)
            dv_t = dv_t + jnp.dot(dot_ref[:, rows], p.astype(BF16), preferred_element_type=F32)
            dk_t = dk_t + jnp.dot(qt_ref[:, rows], ds, preferred_element_type=F32)
            dq_ref[rows, :] += jnp.dot(ds, kv, preferred_element_type=F32)
            return dk_t, dv_t

        def step(i, carry):
            for c in range(per_trip):
                carry = chunk(per_trip * i + c, *carry)
            return carry

        zero = jnp.zeros((HEAD_PAD, tk), F32)
        dk_t, dv_t = lax.fori_loop(0, nq // per_trip, step, (zero, zero))
        dk_ref[...] = dk_t.T
        dv_ref[...] = dv_t.T

        @pl.when(j == pl.num_programs(1) - 1)
        def _():
            dq_ref[...] *= SM_SCALE

    head = pl.BlockSpec((s, HEAD_PAD), lambda h, j: (0, h))
    head_t = pl.BlockSpec((HEAD_PAD, s), lambda h, j: (h, 0))
    stat = pl.BlockSpec((None, s, 1), lambda h, j: (h, 0, 0))
    blk = pl.BlockSpec((tk, HEAD_PAD), lambda h, j: (j, h))
    return pl.pallas_call(
        body, name="flash_bwd", grid=(N_HEADS, s // tk),
        in_specs=[head, head_t, head, head_t, stat, stat, blk, blk, ANY],
        out_specs=[head, blk, blk],
        out_shape=[jax.ShapeDtypeStruct((s, D_ATT), F32)] * 3,
        compiler_params=_params(("parallel", "arbitrary")),
    )(q, qt, do, dot, lse, delta, k, v, after)


FFN_TC = 256
FFN_TG = 1408


FFN_HALO_BF16 = 16
FFN_HALO_F32 = 8


def _row_halo_specs(ts, s, halo, width):
    nb = ts // halo
    last = s // halo - 1
    prev = pl.BlockSpec((halo, width), lambda i, j: (jnp.maximum(i * nb - 1, 0), 0))
    nxt = pl.BlockSpec((halo, width), lambda i, j: (jnp.minimum((i + 1) * nb, last), 0))
    return prev, nxt


def _ext_rows(prev, main, nxt, first, last):
    return jnp.concatenate([jnp.where(first, jnp.zeros_like(prev), prev), main,
                            jnp.where(last, jnp.zeros_like(nxt), nxt)], axis=0)


def _ext_conv(a, w):
    a_dn = pltpu.roll(a, 1, 0)
    a_up = pltpu.roll(a, a.shape[0] - 1, 0)
    return a_dn * w[0:1, :] + a * w[1:2, :] + a_up * w[2:3, :], a_dn, a_up


def _ffn_pieces(tg):
    return [(off, min(FFN_TC, tg - off)) for off in range(0, tg, FFN_TC)]


def _ffn_fwd(hf, w_up, w, b, *, ts=512, tg=FFN_TG):
    s = hf.shape[0]
    n, ng, halo = s // ts, D_FF // tg, FFN_HALO_BF16

    def body(h_ref, hp_ref, hn_ref, wg_ref, wu_ref, cw_ref, cb_ref, a_ref, act_ref):
        i = pl.program_id(0)
        ext = _ext_rows(hp_ref[...], h_ref[...], hn_ref[...], i == 0, i == n - 1)
        for off, width in _ffn_pieces(tg):
            cols = slice(off, off + width)
            gate_up = []
            for half, w_ref in enumerate((wg_ref, wu_ref)):
                a_ext = jnp.dot(ext, w_ref[:, cols], preferred_element_type=F32)
                a_ref[half, :, cols] = a_ext[halo:halo + ts]
                conv = _ext_conv(a_ext, cw_ref[half, :, cols])[0]
                gate_up.append(conv[halo:halo + ts] + cb_ref[half, :, cols])
            g, u = gate_up
            act_ref[:, cols] = (g * _sigmoid(g) * u).astype(BF16)

    prev, nxt = _row_halo_specs(ts, s, halo, D_MODEL)
    return pl.pallas_call(
        body, name="ffn_fwd", grid=(n, ng),
        in_specs=[pl.BlockSpec((ts, D_MODEL), lambda i, j: (i, 0)), prev, nxt,
                  pl.BlockSpec((D_MODEL, tg), lambda i, j: (0, j)), pl.BlockSpec((D_MODEL, tg), lambda i, j: (0, j + ng)),
                  pl.BlockSpec((2, 8, tg), lambda i, j: (0, 0, j)), pl.BlockSpec((2, 1, tg), lambda i, j: (0, 0, j))],
        out_specs=[pl.BlockSpec((2, ts, tg), lambda i, j: (0, i, j)), pl.BlockSpec((ts, tg), lambda i, j: (i, j))],
        out_shape=[jax.ShapeDtypeStruct((2, s, D_FF), F32), jax.ShapeDtypeStruct((s, D_FF), BF16)],
        compiler_params=_params(("parallel", "parallel")),
    )(hf, hf, hf, w_up, w_up, w, b)


def _ffn_bwd(dx2, w_down, a_pre, w, b, *, ts=512, tg=FFN_TG):
    s = dx2.shape[0]
    n, ng, halo = s // ts, D_FF // tg, FFN_HALO_F32
    main = slice(halo, halo + ts)

    def body(dx_ref, dxp_ref, dxn_ref, wd_ref, a_ref, ap_ref, an_ref, cw_ref, cb_ref, o_ref, dw_ref, db_ref):
        i, j = pl.program_id(0), pl.program_id(1)
        first, last = i == 0, i == n - 1

        @pl.when(first & (j == 0))
        def _():
            dw_ref[...] = jnp.zeros_like(dw_ref)
            db_ref[...] = jnp.zeros_like(db_ref)

        dx_ext = _ext_rows(dxp_ref[...], dx_ref[...], dxn_ref[...], first, last).astype(BF16)
        for off, width in _ffn_pieces(tg):
            cols = slice(off, off + width)
            dact = lax.dot_general(dx_ext, wd_ref[cols, :], _NT, preferred_element_type=F32)
            halves = []
            for half in range(2):
                a_ext = _ext_rows(ap_ref[half, :, cols], a_ref[half, :, cols], an_ref[half, :, cols], first, last)
                conv, a_dn, a_up = _ext_conv(a_ext, cw_ref[half, :, cols])
                halves.append((conv + cb_ref[half, :, cols], a_dn, a_ext, a_up))
            g, u = halves[0][0], halves[1][0]
            sg = _sigmoid(g)
            grads = (dact * u * (sg * (1.0 + g * (1.0 - sg))), dact * (g * sg))
            for half in range(2):
                d = grads[half]
                _, a_dn, a_ext, a_up = halves[half]
                wv = cw_ref[half, :, cols]
                d_pre = (pltpu.roll(d, d.shape[0] - 1, 0) * wv[0:1, :] + d * wv[1:2, :]
                         + pltpu.roll(d, 1, 0) * wv[2:3, :])
                o_ref[half, :, cols] = d_pre[main].astype(BF16)
                dm = d[main]
                dw_ref[j, half, 0:1, cols] += jnp.sum(dm * a_dn[main], axis=0, keepdims=True)
                dw_ref[j, half, 1:2, cols] += jnp.sum(dm * a_ext[main], axis=0, keepdims=True)
                dw_ref[j, half, 2:3, cols] += jnp.sum(dm * a_up[main], axis=0, keepdims=True)
                db_ref[j, half, :, cols] += jnp.sum(dm, axis=0, keepdims=True)

    dxp, dxn = _row_halo_specs(ts, s, halo, D_MODEL)
    nb, lastb = ts // halo, s // halo - 1
    a_main = pl.BlockSpec((2, ts, tg), lambda i, j: (0, i, j))
    a_prev = pl.BlockSpec((2, halo, tg), lambda i, j: (0, jnp.maximum(i * nb - 1, 0), j))
    a_next = pl.BlockSpec((2, halo, tg), lambda i, j: (0, jnp.minimum((i + 1) * nb, lastb), j))
    da_pre, dw, db = pl.pallas_call(
        body, name="ffn_bwd", grid=(n, ng),
        in_specs=[pl.BlockSpec((ts, D_MODEL), lambda i, j: (i, 0)), dxp, dxn,
                  pl.BlockSpec((tg, D_MODEL), lambda i, j: (j, 0)), a_main, a_prev, a_next,
                  pl.BlockSpec((2, 8, tg), lambda i, j: (0, 0, j)), pl.BlockSpec((2, 1, tg), lambda i, j: (0, 0, j))],
        out_specs=[a_main, pl.BlockSpec((ng, 2, 8, tg), lambda i, j: (0, 0, 0, 0)),
                   pl.BlockSpec((ng, 2, 1, tg), lambda i, j: (0, 0, 0, 0))],
        out_shape=[jax.ShapeDtypeStruct((2, s, D_FF), BF16), jax.ShapeDtypeStruct((ng, 2, 8, tg), F32),
                   jax.ShapeDtypeStruct((ng, 2, 1, tg), F32)],
        compiler_params=_params(("arbitrary", "arbitrary")),
    )(dx2, dx2, dx2, w_down, a_pre, a_pre, a_pre, w, b)
    return (da_pre, dw.transpose(1, 2, 0, 3).reshape(2, 8, D_FF), db.transpose(1, 2, 0, 3).reshape(2, 1, D_FF))


def _ple_final(x2, n3, p, target, gf, w_pg, w_pp, *, ts=256):
    s, d = x2.shape
    dp = p.shape[1]

    def body(x2_ref, n3_ref, p_ref, t_ref, gf_ref, wg_ref, wp_ref, loss_ref, dx3_ref, dgl_ref, dpp_ref, dgf_ref):
        @pl.when(pl.program_id(0) == 0)
        def _():
            loss_ref[...] = jnp.zeros_like(loss_ref)
            dgf_ref[...] = jnp.zeros_like(dgf_ref)

        gate = _sigmoid(jnp.dot(n3_ref[...], wg_ref[...], preferred_element_type=F32))
        ppv = jnp.dot(p_ref[...].astype(BF16), wp_ref[...], preferred_element_type=F32)
        x3 = x2_ref[...] + gate * ppv
        gfv = gf_ref[...]
        err = x3 * _rms_scale(x3) * gfv - t_ref[...]
        loss_ref[...] += 0.5 * jnp.sum(jnp.mean(err * err, axis=-1, keepdims=True), axis=0, keepdims=True)
        dx3, dgf_rows = _rms_bwd_rows(x3, gfv, err * (1.0 / d))
        dgf_ref[...] += jnp.sum(dgf_rows, axis=0, keepdims=True)
        dx3_ref[...] = dx3
        dgl_ref[...] = (dx3 * ppv * gate * (1.0 - gate)).astype(BF16)
        dpp_ref[...] = (dx3 * gate).astype(BF16)

    row = pl.BlockSpec((ts, d), lambda i: (i, 0))
    vec = pl.BlockSpec((1, d), lambda i: (0, 0))
    return pl.pallas_call(
        body, name="ple_final", grid=(s // ts,),
        in_specs=[row, row, pl.BlockSpec((ts, dp), lambda i: (i, 0)), row, vec,
                  pl.BlockSpec((d, d), lambda i: (0, 0)), pl.BlockSpec((dp, d), lambda i: (0, 0))],
        out_specs=[pl.BlockSpec((1, 128), lambda i: (0, 0)), row, row, row, vec],
        out_shape=[jax.ShapeDtypeStruct((1, 128), F32), jax.ShapeDtypeStruct((s, d), F32),
                   jax.ShapeDtypeStruct((s, d), BF16), jax.ShapeDtypeStruct((s, d), BF16),
                   jax.ShapeDtypeStruct((1, d), F32)],
        compiler_params=_params(("arbitrary",)),
    )(x2, n3, p, target, gf, w_pg, w_pp)


def _row_tile(rows, cols, n_arrays, budget=12 << 20):
    best = None
    for t in range(8, rows + 1, 8):
        if rows % t == 0 and t * cols * 4 * n_arrays <= budget:
            best = t
    return rows if best is None else best


def _sum_slots(a, *, name):
    g, r, c = a.shape
    tr = _row_tile(r, c, g + 1)

    def body(*refs):
        tot = refs[0][...]
        for ref in refs[1:g]:
            tot = tot + ref[...]
        refs[g][...] = tot

    specs = [pl.BlockSpec((None, tr, c), functools.partial(lambda i, slot: (slot, i, 0), slot=k)) for k in range(g)]
    return pl.pallas_call(
        body, name=name, grid=(r // tr,), in_specs=specs, out_specs=pl.BlockSpec((tr, c), lambda i: (i, 0)),
        out_shape=jax.ShapeDtypeStruct((r, c), a.dtype), compiler_params=_params(("parallel",)),
    )(*([a] * g))


def _adamw(w, g, m, v, *, name):
    r, c = w.shape
    tr = _row_tile(r, c, 7)

    def body(w_ref, g_ref, m_ref, v_ref, d_ref, mo_ref, vo_ref):
        gv = g_ref[...]
        mn = ADAM_B1 * m_ref[...] + (1.0 - ADAM_B1) * gv
        vn = ADAM_B2 * v_ref[...] + (1.0 - ADAM_B2) * (gv * gv)
        m_hat = mn / (1.0 - ADAM_B1 ** ADAM_STEP)
        v_hat = vn / (1.0 - ADAM_B2 ** ADAM_STEP)
        d_ref[...] = -ADAM_LR * (m_hat / (jnp.sqrt(v_hat) + ADAM_EPS) + ADAM_WD * w_ref[...])
        mo_ref[...] = mn
        vo_ref[...] = vn

    blk = pl.BlockSpec((tr, c), lambda i: (i, 0))
    return pl.pallas_call(
        body, name=name, grid=(r // tr,), in_specs=[blk] * 4, out_specs=[blk] * 3,
        out_shape=[jax.ShapeDtypeStruct((r, c), F32)] * 3, compiler_params=_params(("parallel",)),
    )(w, g, m, v)


def _position():
    x, y, c = lax.axis_index("x"), lax.axis_index("y"), lax.axis_index("c")
    return x, y, c


def _other_chips(x, y):
    return [(1 - x, y), (x, 1 - y), (1 - x, 1 - y)]


def _stage_in(srcs, stage, sems):
    cps = [pltpu.make_async_copy(src, stage[a], sems.at[a]) for a, src in enumerate(srcs)]
    for cp in cps:
        cp.start()
    return cps


def _stage_out(staged, stage, dsts, sems):
    cps = []
    for a, dst in enumerate(dsts):
        staged[a].wait()
        cp = pltpu.make_async_copy(stage[a], dst, sems.at[a])
        cp.start()
        cps.append(cp)
    return cps


def _gather_chips(shards):
    n = len(shards)

    def body(*refs):
        ins, outs, stage = refs[:n], refs[n:2 * n], refs[2 * n:3 * n]
        send_sems, recv_sems, in_sems, out_sems = refs[3 * n:]
        x, y, c = _position()
        me = 2 * x + y
        chips = _other_chips(x, y)
        remote = []
        staged = _stage_in(ins, stage, in_sems)
        for a in range(n):
            for k, (px, py) in enumerate(chips):
                rc = pltpu.make_async_remote_copy(
                    src_ref=ins[a], dst_ref=outs[a].at[me], send_sem=send_sems.at[3 * a + k],
                    recv_sem=recv_sems.at[3 * a + k], device_id=(px, py, c), device_id_type=MESH)
                rc.start()
                remote.append(rc)
        local = _stage_out(staged, stage, [o.at[me] for o in outs], out_sems)
        for a in range(n):
            for k, (px, py) in enumerate(chips):
                pltpu.make_async_remote_copy(
                    src_ref=ins[a], dst_ref=outs[a].at[2 * px + py], send_sem=send_sems.at[3 * a + k],
                    recv_sem=recv_sems.at[3 * a + k], device_id=(px, py, c), device_id_type=MESH).wait_recv()
        for rc in remote:
            rc.wait_send()
        for cp in local:
            cp.wait()

    return pl.pallas_call(
        body, name="gather_chips", in_specs=[ANY] * n, out_specs=[ANY] * n,
        out_shape=[jax.ShapeDtypeStruct((N_CHIPS,) + s.shape, s.dtype) for s in shards],
        scratch_shapes=[pltpu.VMEM(s.shape, s.dtype) for s in shards]
        + [pltpu.SemaphoreType.DMA((3 * n,)), pltpu.SemaphoreType.DMA((3 * n,)),
           pltpu.SemaphoreType.DMA((n,)), pltpu.SemaphoreType.DMA((n,))],
        compiler_params=pltpu.CompilerParams(has_side_effects=True),
    )(*shards)


def _send_other_halves(grads, *, tag):
    n = len(grads)

    def body(*refs):
        ins, sib = refs[:n], refs[n:2 * n]
        send_sems, recv_sems = refs[2 * n:]
        x, y, c = _position()
        remote = []
        for a in range(n):
            half = ins[a].shape[1] // 2
            give = ins[a].at[:, pl.ds(pl.multiple_of((1 - c) * half, 8), half), :]
            rc = pltpu.make_async_remote_copy(
                src_ref=give, dst_ref=sib[a], send_sem=send_sems.at[a], recv_sem=recv_sems.at[a],
                device_id=(x, y, 1 - c), device_id_type=MESH)
            rc.start()
            remote.append(rc)
        for rc in remote:
            rc.wait_recv()
        for rc in remote:
            rc.wait_send()

    return pl.pallas_call(
        body, name="send_other_halves_" + tag, in_specs=[ANY] * n, out_specs=[ANY] * n,
        out_shape=[jax.ShapeDtypeStruct((g.shape[0], g.shape[1] // 2, g.shape[2]), g.dtype) for g in grads],
        scratch_shapes=[pltpu.SemaphoreType.DMA((n,)), pltpu.SemaphoreType.DMA((n,))],
        compiler_params=pltpu.CompilerParams(has_side_effects=True),
    )(*grads)


def _add_own_half(g4, sib, core, *, name):
    g, a2, c = sib.shape
    tr = _row_tile(a2, c, 4)

    def body(core_ref, a_ref, b_ref, o_ref, o16_ref):
        tot = a_ref[...] + b_ref[...]
        o_ref[...] = tot
        o16_ref[...] = tot.astype(BF16)

    blk = pl.BlockSpec((None, tr, c), lambda i, j, core_ref: (i, j, 0))
    return pl.pallas_call(
        body, name=name,
        grid_spec=pltpu.PrefetchScalarGridSpec(
            num_scalar_prefetch=1, grid=(g, a2 // tr),
            in_specs=[pl.BlockSpec((None, None, tr, c), lambda i, j, core_ref: (i, core_ref[0], j, 0)), blk],
            out_specs=[blk, blk]),
        out_shape=[jax.ShapeDtypeStruct(sib.shape, F32), jax.ShapeDtypeStruct(sib.shape, BF16)],
        compiler_params=_params(("parallel", "parallel")),
    )(core, g4.reshape(g, 2, a2, c), sib)


def _sum_chips(landed, own, chip, *, name):
    g, r, c = landed.shape
    tr = _row_tile(r, c, 5)

    def body(chip_ref, *refs):
        me = chip_ref[0]
        own_v = refs[g][...]
        tot = None
        for slot in range(g):
            term = jnp.where(me == slot, own_v, refs[slot][...].astype(F32))
            tot = term if tot is None else tot + term
        refs[g + 1][...] = tot

    def landed_spec(slot):
        return pl.BlockSpec((None, tr, c),
                            lambda i, chip_ref: (jnp.where(chip_ref[0] == slot, (slot + 1) % g, slot), i, 0))

    return pl.pallas_call(
        body, name=name,
        grid_spec=pltpu.PrefetchScalarGridSpec(
            num_scalar_prefetch=1, grid=(r // tr,),
            in_specs=[landed_spec(k) for k in range(g)]
            + [pl.BlockSpec((None, tr, c), lambda i, chip_ref: (chip_ref[0], i, 0))],
            out_specs=pl.BlockSpec((tr, c), lambda i, chip_ref: (i, 0))),
        out_shape=jax.ShapeDtypeStruct((r, c), F32), compiler_params=_params(("parallel",)),
    )(chip, *([landed] * g), own)


def _join_halves(halves):
    n = len(halves)

    def body(*refs):
        ins, outs, stage = refs[:n], refs[n:2 * n], refs[2 * n:3 * n]
        send_sems, recv_sems, in_sems, out_sems = refs[3 * n:]
        x, y, c = _position()
        remote = []
        staged = _stage_in(ins, stage, in_sems)
        for a in range(n):
            rc = pltpu.make_async_remote_copy(
                src_ref=ins[a], dst_ref=outs[a].at[c], send_sem=send_sems.at[a], recv_sem=recv_sems.at[a],
                device_id=(x, y, 1 - c), device_id_type=MESH)
            rc.start()
            remote.append(rc)
        local = _stage_out(staged, stage, [o.at[c] for o in outs], out_sems)
        for a in range(n):
            pltpu.make_async_remote_copy(
                src_ref=ins[a], dst_ref=outs[a].at[1 - c], send_sem=send_sems.at[a], recv_sem=recv_sems.at[a],
                device_id=(x, y, 1 - c), device_id_type=MESH).wait_recv()
        for rc in remote:
            rc.wait_send()
        for cp in local:
            cp.wait()

    return pl.pallas_call(
        body, name="join_halves", in_specs=[ANY] * n, out_specs=[ANY] * n,
        out_shape=[jax.ShapeDtypeStruct((2,) + h.shape, h.dtype) for h in halves],
        scratch_shapes=[pltpu.VMEM(h.shape, h.dtype) for h in halves]
        + [pltpu.SemaphoreType.DMA((n,)), pltpu.SemaphoreType.DMA((n,)), pltpu.SemaphoreType.DMA((n,)),
           pltpu.SemaphoreType.DMA((n,))],
        compiler_params=pltpu.CompilerParams(has_side_effects=True),
    )(*halves)


_HBM = pl.BlockSpec(memory_space=pltpu.HBM)
_SEM = pl.BlockSpec(memory_space=pltpu.SEMAPHORE)


def _chip_copies(srcs, lands, send_sems, recv_sems, scatter):
    x, y, c = _position()
    me = 2 * x + y
    outgoing, incoming = [], []
    for a, (src, land) in enumerate(zip(srcs, lands)):
        for k, (px, py) in enumerate(_other_chips(x, y)):
            peer = 2 * px + py
            sems = dict(send_sem=send_sems.at[3 * a + k], recv_sem=recv_sems.at[3 * a + k], device_id=(px, py, c),
                        device_id_type=MESH)
            outgoing.append(pltpu.make_async_remote_copy(
                src_ref=src.at[peer] if scatter else src, dst_ref=land.at[me], **sems))
            incoming.append(pltpu.make_async_remote_copy(
                src_ref=src.at[me] if scatter else src, dst_ref=land.at[peer], **sems))
    return outgoing, incoming


def _chips_start(srcs, *, scatter, name):
    n = len(srcs)
    lands = [lax.empty(a.shape if scatter else (N_CHIPS,) + a.shape, a.dtype) for a in srcs]

    def body(*refs):
        ins, send_sems, recv_sems, token = refs[:2 * n], refs[2 * n], refs[2 * n + 1], refs[-1]
        outgoing, _ = _chip_copies(ins[:n], ins[n:], send_sems, recv_sems, scatter)
        for cp in outgoing:
            cp.start()
        token[...] = jnp.zeros_like(token)

    bufs = list(srcs) + lands
    res = pl.pallas_call(
        body, name=name, in_specs=[_HBM] * (2 * n),
        out_specs=(_SEM, _SEM, *[_HBM] * (2 * n), pl.BlockSpec(memory_space=pltpu.VMEM)),
        out_shape=(pltpu.SemaphoreType.DMA((3 * n,)), pltpu.SemaphoreType.DMA((3 * n,)),
                   *[pltpu.HBM(a.shape, a.dtype) for a in bufs], jax.ShapeDtypeStruct((8, 128), F32)),
        input_output_aliases={i: 2 + i for i in range(2 * n)},
        compiler_params=pltpu.CompilerParams(has_side_effects=pltpu.SideEffectType.DATAFLOW_SIDE_EFFECTING),
    )(*[pltpu.with_memory_space_constraint(a, pltpu.HBM) for a in bufs])
    return res[0], res[1], list(res[2:2 + n]), list(res[2 + n:2 + 2 * n]), res[-1]


def _chips_wait(handle, after, *, scatter, name):
    send_sems, recv_sems, srcs, lands, _ = handle
    n = len(srcs)

    def body(*refs):
        ins, send_ref, recv_ref = refs[:2 * n], refs[2 * n], refs[2 * n + 1]
        outgoing, incoming = _chip_copies(ins[:n], ins[n:], send_ref, recv_ref, scatter)
        for cp in outgoing:
            cp.wait_send()
        for cp in incoming:
            cp.wait_recv()

    bufs = list(srcs) + list(lands)
    res = pl.pallas_call(
        body, name=name, in_specs=[_HBM] * (2 * n) + [_SEM, _SEM, ANY], out_specs=tuple([_HBM] * (2 * n)),
        out_shape=tuple(pltpu.HBM(a.shape, a.dtype) for a in bufs),
        input_output_aliases={i: i for i in range(2 * n)},
        compiler_params=pltpu.CompilerParams(has_side_effects=pltpu.SideEffectType.DATAFLOW_SIDE_EFFECTING),
    )(*bufs, send_sems, recv_sems, after)
    return list(res[:n]), list(res[n:])


def _gather_all(buf):
    def body(in_ref, out_ref, send_sems, recv_sems, local_sem):
        x, y, c = _position()
        me = 4 * x + 2 * y + c
        peers = [(x, y, 1 - c)] + [(px, py, pc) for (px, py) in _other_chips(x, y) for pc in (c, 1 - c)]
        cp = pltpu.make_async_copy(in_ref, out_ref.at[me], local_sem)
        cp.start()
        remote = []
        for k, peer in enumerate(peers):
            rc = pltpu.make_async_remote_copy(
                src_ref=in_ref, dst_ref=out_ref.at[me], send_sem=send_sems.at[k], recv_sem=recv_sems.at[k],
                device_id=peer, device_id_type=MESH)
            rc.start()
            remote.append(rc)
        for k, (px, py, pc) in enumerate(peers):
            pltpu.make_async_remote_copy(
                src_ref=in_ref, dst_ref=out_ref.at[4 * px + 2 * py + pc], send_sem=send_sems.at[k],
                recv_sem=recv_sems.at[k], device_id=(px, py, pc), device_id_type=MESH).wait_recv()
        for rc in remote:
            rc.wait_send()
        cp.wait()

    return pl.pallas_call(
        body, name="gather_all", in_specs=[ANY], out_specs=ANY,
        out_shape=jax.ShapeDtypeStruct((N_DEV,) + buf.shape, buf.dtype),
        scratch_shapes=[pltpu.SemaphoreType.DMA((N_DEV - 1,)), pltpu.SemaphoreType.DMA((N_DEV - 1,)),
                        pltpu.SemaphoreType.DMA],
        compiler_params=pltpu.CompilerParams(has_side_effects=True),
    )(buf)


def _cols_from_shards(g4):
    _, k, n = g4.shape
    return g4.transpose(1, 0, 2).reshape(k, N_CHIPS * n)


def _cols_to_shards(w):
    k, n = w.shape
    return w.reshape(k, N_CHIPS, n // N_CHIPS).transpose(1, 0, 2)


def _pad_heads(w, width):
    k = w.shape[0]
    w3 = w.reshape(k, N_HEADS, width)
    return jnp.pad(w3, ((0, 0), (0, 0), (0, HEAD_PAD - width))).reshape(k, D_ATT)


def _unpad_heads(w, width):
    k = w.shape[0]
    return w.reshape(k, N_HEADS, HEAD_PAD)[:, :, :width]


def _rope_tables(s):
    pos = jnp.arange(s, dtype=F32)
    inv_freq = ROPE_THETA ** (-jnp.arange(0, QK_ROPE, 2, dtype=F32) / QK_ROPE)
    ang = pos[:, None] * inv_freq[None, :]
    cos_h, sin_h = jnp.cos(ang), jnp.sin(ang)
    half = QK_ROPE // 2
    z = jnp.zeros((s, half), F32)
    ones = jnp.ones((s, QK_NOPE), F32)
    tail = jnp.zeros((s, HEAD_PAD - QK_NOPE - QK_ROPE), F32)
    cos = jnp.concatenate([ones, cos_h, cos_h, tail + 1.0], axis=1)
    sin_a = jnp.concatenate([ones * 0.0, -sin_h, z, tail], axis=1)
    sin_b = jnp.concatenate([ones * 0.0, z, sin_h, tail], axis=1)
    return cos, sin_a, sin_b


def _local_step(x, p, target, wts, late_weights, reduce_early, reduce_last):
    s = x.shape[0]
    cos, sin_a, sin_b = _rope_tables(s)
    g1, gq, gkv, g2, g3, gf = (wts[k] for k in ("norm_mix_g", "q_norm_g", "kv_norm_g", "norm_ffn_g", "ple_norm_g",
                                                 "final_norm_g"))
    w_in_p, w_uq_p, w_kv_p = wts["w_in_p"], wts["w_uq_p"], wts["w_kv_p"]
    conv_w8, fconv_w, fconv_b = wts["conv_w8"], wts["ffn_conv_w"], wts["ffn_conv_b"]

    (h, z), _ = _mm_fused(x, w_in_p, name="mm_in", prologue=_pro_rms, vecs=[g1], epilogue=_epi_plain, row_outs=[F32])
    y_conv, qn, kvn, kr = _mix_pre(z, conv_w8, gq, gkv, cos, sin_a, sin_b)
    q, k, v, q_t = _qkv_proj(qn, kvn, kr, w_uq_p, w_kv_p, cos, sin_a, sin_b)
    o, lse = _flash_fwd(q, k, v)
    late = late_weights(lse)
    w_o_a, w_o_b, w_up, w_down = late["w_o_a"], late["w_o_b"], late["w_up"], late["w_down"]
    w_pg, w_pp = late["w_ple_gate"], late["w_ple_proj"]
    (x1, hf), _ = _mm_fused(o, w_o_b, second=(y_conv, w_o_a), name="mm_o", rows=[x], vecs=[g2],
                            epilogue=_epi_add_rms, row_outs=[F32, BF16])
    a_pre, act = _ffn_fwd(hf, w_up, fconv_w, fconv_b)
    (x2, n3), _ = _mm_fused(act, w_down, name="mm_down", rows=[x1], vecs=[g3], epilogue=_epi_add_rms,
                            row_outs=[F32, BF16])
    loss, dx3, dgl, dpp, d_gf = _ple_final(x2, n3, p, target, gf, w_pg, w_pp)

    grads, early = {"final_norm_g": d_gf}, {}
    early["w_ple_proj"] = _mm(p, dpp, ta=True, name="mm_d_wpp", tm=256, tn=1024, tk=2048)
    early["w_ple_gate"] = _mm(n3, dgl, ta=True, name="mm_d_wpg", tm=1024, tn=1024, tk=2048)
    (dx2,), (grads["ple_norm_g"],) = _mm_fused(dgl, w_pg, tb=True, name="mm_d_n3", rows=[x2, dx3], vecs=[g3],
                                               epilogue=_epi_rms_bwd, row_outs=[F32], n_vec_out=1)
    early["w_down"] = _mm(act, dx2, ta=True, name="mm_d_wdown", tm=1408, tn=1024, tk=2048)
    da_pre, grads["ffn_conv_w"], grads["ffn_conv_b"] = _ffn_bwd(dx2, w_down, a_pre, fconv_w, fconv_b)
    early["w_up"] = _mm(hf, da_pre, ta=True, b_split=True, name="mm_d_wup", tm=1024, tn=1408, tk=2048,
                       o_shards=True)
    (dx1,), (grads["norm_ffn_g"],) = _mm_fused(da_pre, w_up, tb=True, a_split=True, name="mm_d_hf", rows=[x1, dx2],
                                               vecs=[g2], epilogue=_epi_rms_bwd, row_outs=[F32], n_vec_out=1)
    d_wo_a = _mm(y_conv, dx1, ta=True, name="mm_d_wo_conv", tm=512, tn=1024, tk=2048)
    d_wo_b = _mm(o, dx1, ta=True, name="mm_d_wo_att", tm=1024, tn=1024, tk=2048)
    early["w_o"] = jnp.concatenate([d_wo_a, d_wo_b.reshape(N_HEADS, HEAD_PAD, D_MODEL)[:, :V_HEAD]
                                    .reshape(N_HEADS * V_HEAD, D_MODEL)], axis=0)
    token, finish = reduce_early(early)
    dyc = _mm(dx1, w_o_a, tb=True, name="mm_d_yconv", tm=512, tn=512, tk=1024)
    (do, do_t), _ = _mm_fused(dx1, w_o_b, tb=True, name="mm_d_o", epilogue=_epi_plain, row_outs=[BF16],
                              transposed_out=BF16)
    delta = _attn_delta(do, o)
    dq, dk, dv = _flash_bwd(q, q_t, k, v, do, do_t, lse, delta, token)
    reduced_early = finish(dq)
    dq_pre, dkr = _qk_bwd(dq, dk, cos, sin_a, sin_b)
    grads["w_uq_p"] = _mm(qn, dq_pre, ta=True, name="mm_d_wuq", tm=256, tn=1024, tk=2048)
    dqn = _mm(dq_pre, w_uq_p, tb=True, name="mm_d_qn", tm=512, tn=256, tk=1024)
    grads["w_k_p"] = _mm(kvn, dk, ta=True, name="mm_d_wk", tm=128, tn=1024, tk=2048)
    grads["w_v_p"] = _mm(kvn, dv, ta=True, name="mm_d_wv", tm=128, tn=1024, tk=2048)
    dkvn_k = _mm(dk, w_kv_p[:, :D_ATT], tb=True, name="mm_d_kvn_k", tm=512, tn=128, tk=1024)
    dkvn = _mm(dv, w_kv_p[:, D_ATT:], tb=True, add=dkvn_k, name="mm_d_kvn_v", tm=512, tn=128, tk=1024)
    dz, grads["conv_w"], grads["q_norm_g"], grads["kv_norm_g"] = _mix_bwd(
        z, dyc, dqn, dkvn, dkr, conv_w8, gq, gkv, cos, sin_a, sin_b)
    grads["w_in_p"] = _mm(h, dz, ta=True, name="mm_d_win", tm=1024, tn=1024, tk=2048)
    token, finish = reduce_last({n: grads.pop(n) for n in ("w_in_p", "w_uq_p", "w_k_p", "w_v_p")})
    (grad_x,), (grads["norm_mix_g"],) = _mm_fused(dz, w_in_p, tb=True, name="mm_d_h", rows=[x, dx1],
                                                  vecs=[g1 + token[0, 0]], epilogue=_epi_rms_bwd, row_outs=[F32],
                                                  n_vec_out=1)
    return loss[0, 0], grad_x, grads, reduced_early, finish(grad_x)


_EARLY_W = ("w_in", "w_uq", "w_ukv")
_LATE_W = ("w_o", "w_up", "w_down", "w_ple_gate", "w_ple_proj")
_BIG = _EARLY_W + _LATE_W
_COL_SHARDED = ("w_in", "w_uq", "w_ukv", "w_up", "w_ple_proj")
_SMALL = ("norm_mix_g", "conv_w", "q_norm_g", "kv_norm_g", "norm_ffn_g", "ffn_conv_w", "ffn_conv_b", "ple_norm_g",
          "final_norm_g")


def _full_from_slots(n, g4):
    return _cols_from_shards(g4) if n in _COL_SHARDED else g4.reshape(-1, g4.shape[2])


def _shard_major(n, g):
    if g.ndim == 3:
        return g
    return _cols_to_shards(g) if n in _COL_SHARDED else g.reshape(N_CHIPS, g.shape[0] // N_CHIPS, g.shape[1])


def _early_weights(w):
    shards = [w[n][0].astype(BF16) for n in _EARLY_W]
    shards.append(jnp.pad(w["conv_w"][0], ((0, 5), (0, 0))))
    shards.append(jnp.pad(w["ffn_conv_w"][0], ((0, 5), (0, 0))))
    got = _gather_chips(shards)
    full = {n: _full_from_slots(n, g4) for n, g4 in zip(_EARLY_W, got)}
    full["conv_w8"] = _cols_from_shards(got[len(_EARLY_W)])
    full["ffn_conv_w8"] = _cols_from_shards(got[len(_EARLY_W) + 1])
    return _layout_early(full, w)


def _layout_early(full, w):
    out = {n: w[n] for n in ("norm_mix_g", "q_norm_g", "kv_norm_g", "norm_ffn_g", "ple_norm_g")}
    out["final_norm_g"] = w["final_norm_g"][None, :]
    w_in = full["w_in"]
    zc = jnp.zeros((D_MODEL, QK_NOPE), BF16)
    zt = jnp.zeros((D_MODEL, HEAD_PAD - QK_NOPE - QK_ROPE), BF16)
    out["w_in_p"] = jnp.concatenate([w_in[:, :D_IN - QK_ROPE], zc, w_in[:, D_IN - QK_ROPE:], zt], axis=1)
    out["w_uq_p"] = _pad_heads(full["w_uq"], QK_NOPE + QK_ROPE)
    kv3 = full["w_ukv"].reshape(KV_LORA, N_HEADS, QK_NOPE + V_HEAD)
    out["w_kv_p"] = jnp.concatenate([_pad_heads(kv3[:, :, :QK_NOPE].reshape(KV_LORA, -1), QK_NOPE),
                                     _pad_heads(kv3[:, :, QK_NOPE:].reshape(KV_LORA, -1), V_HEAD)], axis=1)
    out["conv_w8"] = full["conv_w8"]
    fw = full["ffn_conv_w8"]
    out["ffn_conv_w"] = jnp.stack([fw[:, :D_FF], fw[:, D_FF:]])
    out["ffn_conv_b"] = w["ffn_conv_b"].reshape(2, 1, D_FF)
    return out


def _layout_late(full):
    w_o = full["w_o"]
    out = {"w_o_a": w_o[:CONV_WIDTH]}
    out["w_o_b"] = jnp.pad(w_o[CONV_WIDTH:].reshape(N_HEADS, V_HEAD, D_MODEL),
                           ((0, 0), (0, HEAD_PAD - V_HEAD), (0, 0))).reshape(D_ATT, D_MODEL)
    for n in ("w_up", "w_down", "w_ple_gate", "w_ple_proj"):
        out[n] = full[n]
    return out


def _true_matrices(g):
    out = {}
    wp = g["w_in_p"]
    out["w_in"] = jnp.concatenate([wp[:, :D_IN - QK_ROPE], wp[:, D_IN_PAD - HEAD_PAD + QK_NOPE:
                                                              D_IN_PAD - HEAD_PAD + QK_NOPE + QK_ROPE]], axis=1)
    out["w_uq"] = _unpad_heads(g["w_uq_p"], QK_NOPE + QK_ROPE).reshape(Q_LORA, -1)
    out["w_ukv"] = jnp.concatenate([_unpad_heads(g["w_k_p"], QK_NOPE), _unpad_heads(g["w_v_p"], V_HEAD)],
                                   axis=2).reshape(KV_LORA, -1)
    return out


def _true_vectors(g):
    out = {}
    out["conv_w"] = g["conv_w"]
    fw = g["ffn_conv_w"]
    out["ffn_conv_w"] = jnp.concatenate([fw[0, :3], fw[1, :3]], axis=1)
    out["ffn_conv_b"] = g["ffn_conv_b"].reshape(1, 2 * D_FF)
    for n in ("norm_mix_g", "q_norm_g", "kv_norm_g", "norm_ffn_g", "ple_norm_g", "final_norm_g"):
        out[n] = g[n]
    return out


def _chip_partials(names, g, core, *, tag):
    g4 = [_shard_major(n, g[n]) for n in names]
    sib = _send_other_halves(g4, tag=tag)
    return [_add_own_half(a, b, core, name="add_cores_" + n) for n, a, b in zip(names, g4, sib)]


_SMALL_SIZES = {"norm_mix_g": D_MODEL, "conv_w": 3 * CONV_WIDTH, "q_norm_g": Q_LORA, "kv_norm_g": KV_LORA,
                "norm_ffn_g": D_MODEL, "ffn_conv_w": 6 * D_FF, "ffn_conv_b": 2 * D_FF, "ple_norm_g": D_MODEL,
                "final_norm_g": D_MODEL}


def _pack(parts, rows):
    flat = jnp.concatenate([a.reshape(-1) for a in parts])
    return jnp.pad(flat, (0, rows * 128 - flat.shape[0])).reshape(rows, 128)


def _unpack(buf, sizes):
    flat = buf.reshape(-1)
    out, at = [], 0
    for n in sizes:
        out.append(flat[at:at + n])
        at += n
    return out


def _reduce_small(g, loss):
    sizes = [1] + [_SMALL_SIZES[n] for n in _SMALL]
    rows = -(-sum(sizes) // 1024) * 8
    slots = _gather_all(_pack([loss] + [g[n] for n in _SMALL], rows))
    parts = _unpack(_sum_slots(slots, name="sum_small"), sizes)
    return parts[0][0], dict(zip(_SMALL, parts[1:]))


def kernel(x, p, norm_mix_g, w_in, conv_w, q_norm_g, w_uq, kv_norm_g, w_ukv, w_o, norm_ffn_g, w_up, ffn_conv_w, ffn_conv_b, w_down, ple_norm_g, w_ple_gate, w_ple_proj, final_norm_g, loss_target, m_norm_mix_g, m_w_in, m_conv_w, m_q_norm_g, m_w_uq, m_kv_norm_g, m_w_ukv, m_w_o, m_norm_ffn_g, m_w_up, m_ffn_conv_w, m_ffn_conv_b, m_w_down, m_ple_norm_g, m_w_ple_gate, m_w_ple_proj, m_final_norm_g, v_norm_mix_g, v_w_in, v_conv_w, v_q_norm_g, v_w_uq, v_kv_norm_g, v_w_ukv, v_w_o, v_norm_ffn_g, v_w_up, v_ffn_conv_w, v_ffn_conv_b, v_w_down, v_ple_norm_g, v_w_ple_gate, v_w_ple_proj, v_final_norm_g):
    names = ["norm_mix_g", "w_in", "conv_w", "q_norm_g", "w_uq", "kv_norm_g", "w_ukv", "w_o", "norm_ffn_g", "w_up",
             "ffn_conv_w", "ffn_conv_b", "w_down", "ple_norm_g", "w_ple_gate", "w_ple_proj", "final_norm_g"]
    w = dict(zip(names, (norm_mix_g, w_in, conv_w, q_norm_g, w_uq, kv_norm_g, w_ukv, w_o, norm_ffn_g, w_up,
                         ffn_conv_w, ffn_conv_b, w_down, ple_norm_g, w_ple_gate, w_ple_proj, final_norm_g)))
    m = dict(zip(names, (m_norm_mix_g, m_w_in, m_conv_w, m_q_norm_g, m_w_uq, m_kv_norm_g, m_w_ukv, m_w_o,
                         m_norm_ffn_g, m_w_up, m_ffn_conv_w, m_ffn_conv_b, m_w_down, m_ple_norm_g, m_w_ple_gate,
                         m_w_ple_proj, m_final_norm_g)))
    v = dict(zip(names, (v_norm_mix_g, v_w_in, v_conv_w, v_q_norm_g, v_w_uq, v_kv_norm_g, v_w_ukv, v_w_o,
                         v_norm_ffn_g, v_w_up, v_ffn_conv_w, v_ffn_conv_b, v_w_down, v_ple_norm_g, v_w_ple_gate,
                         v_w_ple_proj, v_final_norm_g)))

    core = lax.axis_index("c").astype(jnp.int32).reshape(1)
    chip = (2 * lax.axis_index("x") + lax.axis_index("y")).astype(jnp.int32).reshape(1)

    wts = _early_weights(w)
    gather = _chips_start([w[n][0].astype(BF16) for n in _LATE_W], scatter=False, name="gather_late_start")
    wts["norm_mix_g"] = wts["norm_mix_g"] + gather[4][0, 0]

    def late_weights(after):
        shards, landed = _chips_wait(gather, after, scatter=False, name="gather_late_wait")
        full = {n: _full_from_slots(n, lax.dynamic_update_slice(g4, own[None], (chip[0], 0, 0)))
                for n, own, g4 in zip(_LATE_W, shards, landed)}
        return _layout_late(full)

    def reduce_early(g):
        parts = _chip_partials(_LATE_W, g, core, tag="early")
        scatter = _chips_start([t16 for _, t16 in parts], scatter=True, name="scatter_early_start")

        def finish(after):
            _, landed = _chips_wait(scatter, after, scatter=True, name="scatter_early_wait")
            return [_sum_chips(a, t32, chip, name="sum_chips_" + n) for n, a, (t32, _) in zip(_LATE_W, landed, parts)]

        return scatter[4], finish

    def reduce_last(g):
        parts = _chip_partials(_EARLY_W, _true_matrices(g), core, tag="late")
        scatter = _chips_start([t16 for _, t16 in parts], scatter=True, name="scatter_late_start")

        def finish(after):
            _, landed = _chips_wait(scatter, after, scatter=True, name="scatter_late_wait")
            return [_sum_chips(a, t32, chip, name="sum_chips_" + n) for n, a, (t32, _) in zip(_EARLY_W, landed, parts)]

        return scatter[4], finish

    loss, grad_x, small_grads, halves_early, halves_last = _local_step(
        x[0], p[0, 0], loss_target[0], wts, late_weights, reduce_early, reduce_last)
    g_full = _true_vectors(small_grads)
    whole = _join_halves(halves_last + halves_early)
    big = {n: a.reshape(-1, a.shape[2]) for n, a in zip(_BIG, whole)}

    g_out, d_out, m_out, v_out = {}, {}, {}, {}
    for n in _BIG:
        shape = w[n].shape
        g = big[n]
        d, mn, vn = _adamw(w[n][0], g, m[n][0], v[n][0], name="adamw_" + n)
        g_out[n], d_out[n], m_out[n], v_out[n] = (a.reshape(shape) for a in (g, d, mn, vn))

    loss, small = _reduce_small(g_full, loss)
    chip = 2 * lax.axis_index("x") + lax.axis_index("y")
    g_small = {}
    for n in _SMALL:
        shape = w[n].shape
        g = small[n]
        if n in ("conv_w", "ffn_conv_w"):
            width = shape[-1]
            g = lax.dynamic_slice(g.reshape(3, N_CHIPS * width), (0, chip * width), (3, width))
        g_small[n] = g.reshape(shape)
    sizes = [g_small[n].size for n in _SMALL]
    rows = -(-sum(sizes) // 1024) * 8
    packed = [_pack([src[n] for n in _SMALL], rows) for src in (w, g_small, m, v)]
    d_s, m_s, v_s = _adamw(*packed, name="adamw_small")
    for n, d, mn, vn in zip(_SMALL, _unpack(d_s, sizes), _unpack(m_s, sizes), _unpack(v_s, sizes)):
        shape = w[n].shape
        g_out[n], d_out[n], m_out[n], v_out[n] = g_small[n], d.reshape(shape), mn.reshape(shape), vn.reshape(shape)

    return (loss, grad_x[None], *[g_out[n] for n in names], *[d_out[n] for n in names],
            *[m_out[n] for n in names], *[v_out[n] for n in names])
```

```python
import functools

import jax
import jax.numpy as jnp
from jax import lax
from jax.experimental import pallas as pl
from jax.experimental.pallas import tpu as pltpu

F32 = jnp.float32
BF16 = jnp.bfloat16

D_MODEL = 1024
CONV_WIDTH = 512
Q_LORA = 256
KV_LORA = 128
QK_NOPE = 64
QK_ROPE = 32
V_HEAD = 64
N_HEADS = 8
HEAD_PAD = 128
D_ATT = N_HEADS * HEAD_PAD
D_IN = 3 * CONV_WIDTH + Q_LORA + KV_LORA + QK_ROPE
D_IN_PAD = 3 * CONV_WIDTH + Q_LORA + KV_LORA + HEAD_PAD
D_FF = 2816
ROPE_THETA = 10000.0
EPS = 1e-6
SM_SCALE = (QK_NOPE + QK_ROPE) ** -0.5
ONES_LANE = V_HEAD

ADAM_LR = 0.001
ADAM_B1 = 0.9
ADAM_B2 = 0.999
ADAM_EPS = 1e-08
ADAM_WD = 0.01
ADAM_STEP = 10

N_CHIPS = 4
N_DEV = 8
MESH = pl.DeviceIdType.MESH
ANY = pl.BlockSpec(memory_space=pl.ANY)


def _params(sem):
    return pltpu.CompilerParams(dimension_semantics=sem)


MM_PIECE = 256


def _pieces(total, width=MM_PIECE):
    return [(off, min(width, total - off)) for off in range(0, total, width)]


def _mm(a, b, *, name, ta=False, tb=False, add=None, out_dtype=F32, tm=512, tn=512, tk=512, b_split=False,
        o_shards=False):
    k, m = a.shape if ta else a.shape[::-1]
    if b_split:
        _, kb, nh = b.shape
        n = 2 * nh
    elif tb:
        n, kb = b.shape
    else:
        kb, n = b.shape
    assert kb == k, (name, a.shape, b.shape)
    tm, tn, tk = min(tm, m), min(tn, n), min(tk, k)
    assert m % tm == 0 and n % tn == 0 and k % tk == 0, (name, m, n, k, tm, tn, tk)
    gm, gn, gk = m // tm, n // tn, k // tk

    a_spec = pl.BlockSpec((tk, tm), lambda i, j, kk: (kk, i)) if ta else pl.BlockSpec((tm, tk), lambda i, j, kk: (i, kk))
    if b_split:
        assert gn % 2 == 0
        b_spec = pl.BlockSpec((None, tk, tn), lambda i, j, kk: (j // (gn // 2), kk, j % (gn // 2)))
    elif tb:
        b_spec = pl.BlockSpec((tn, tk), lambda i, j, kk: (j, kk))
    else:
        b_spec = pl.BlockSpec((tk, tn), lambda i, j, kk: (kk, j))
    if o_shards:
        o_spec = pl.BlockSpec((None, tm, tn), lambda i, j, kk: (j, i, 0))
        o_shape = jax.ShapeDtypeStruct((gn, m, tn), out_dtype)
    else:
        o_spec = pl.BlockSpec((tm, tn), lambda i, j, kk: (i, j))
        o_shape = jax.ShapeDtypeStruct((m, n), out_dtype)
    dims = (((0 if ta else 1,), (1 if tb else 0,)), ((), ()))

    def body(*refs):
        a_ref, b_ref = refs[:2]
        add_ref = None if add is None else refs[2]
        o_ref = refs[2 if add is None else 3]
        acc_ref = None if gk == 1 else refs[-1]
        kk = pl.program_id(2)
        rhs = b_ref[...].astype(BF16)

        def finish(r, rows):
            if add_ref is not None:
                r = r + add_ref[rows, :]
            o_ref[rows, :] = r.astype(o_ref.dtype)

        for off, size in _pieces(tm):
            rows = slice(off, off + size)
            lhs = (a_ref[:, rows] if ta else a_ref[rows, :]).astype(BF16)
            part = lax.dot_general(lhs, rhs, dims, preferred_element_type=F32)
            if gk == 1:
                finish(part, rows)
            else:
                acc_ref[rows, :] = jnp.where(kk == 0, part, acc_ref[rows, :] + part)

        if gk > 1:
            @pl.when(kk == gk - 1)
            def _():
                finish(acc_ref[...], slice(None))

    in_specs = [a_spec, b_spec]
    args = [a, b]
    if add is not None:
        in_specs.append(pl.BlockSpec((tm, tn), lambda i, j, kk: (i, j)))
        args.append(add)
    return pl.pallas_call(
        body, name=name, grid=(gm, gn, gk), in_specs=in_specs, out_specs=o_spec, out_shape=o_shape,
        scratch_shapes=[] if gk == 1 else [pltpu.VMEM((tm, tn), F32)],
        compiler_params=_params(("parallel", "parallel", "arbitrary")),
    )(*args)


def _rms_scale(v):
    return lax.rsqrt(jnp.mean(v * v, axis=-1, keepdims=True) + EPS)


def _rms_bwd_rows(v, g, dy):
    r = _rms_scale(v)
    vh = v * r
    dyg = dy * g
    dv = r * (dyg - vh * jnp.mean(dyg * vh, axis=-1, keepdims=True))
    return dv, dy * vh


def _shift_down(v, first_row):
    row = lax.broadcasted_iota(jnp.int32, v.shape, 0)
    return jnp.where(row == 0, first_row, pltpu.roll(v, 1, 0))


def _shift_up(v, last_row):
    n = v.shape[0]
    row = lax.broadcasted_iota(jnp.int32, v.shape, 0)
    return jnp.where(row == n - 1, last_row, pltpu.roll(v, n - 1, 0))


def _rope(t, cos, sin_a, sin_b):
    return t * cos + pltpu.roll(t, HEAD_PAD - 16, 1) * sin_a + pltpu.roll(t, 16, 1) * sin_b


def _rope_bwd(d, cos, sin_a, sin_b):
    return d * cos + pltpu.roll(d * sin_a, 16, 1) + pltpu.roll(d * sin_b, HEAD_PAD - 16, 1)


def _sigmoid(v):
    return 1.0 / (1.0 + jnp.exp(-v))


def _halo_specs(ts, s, width, col):
    nb = ts // 8
    last = s // 8 - 1
    prev = pl.BlockSpec((8, width), lambda i: (jnp.maximum(i * nb - 1, 0), col))
    nxt = pl.BlockSpec((8, width), lambda i: (jnp.minimum((i + 1) * nb, last), col))
    return prev, nxt


def _mm_fused(a, b, *, name, epilogue, row_outs, rows=(), vecs=(), n_vec_out=0, tb=False, a_split=False,
              prologue=None, second=None, transposed_out=None, tm=512):
    if a_split:
        _, m, kh = a.shape
        k = 2 * kh
    else:
        m, k = a.shape
    n = b.shape[0] if tb else b.shape[1]
    assert (b.shape[1] if tb else b.shape[0]) == k, (name, a.shape, b.shape)
    assert m % tm == 0, (name, m, tm)
    n_a = 2 if a_split else 1
    nr, nv = len(rows), len(vecs)
    n_pro = 0 if prologue is None else 1
    n_sec = 0 if second is None else 2
    n_t = 0 if transposed_out is None else 1
    dims = (((1,), (1 if tb else 0,)), ((), ()))

    def body(*refs):
        a_refs, b_ref = refs[:n_a], refs[n_a]
        refs = refs[n_a + 1:]
        sec_refs = refs[:n_sec]
        row_refs, vec_refs = refs[n_sec:n_sec + nr], refs[n_sec + nr:n_sec + nr + nv]
        outs = refs[n_sec + nr + nv:]
        row_out_refs = outs[n_pro:n_pro + len(row_outs)]
        t_out_refs = outs[n_pro + len(row_outs):n_pro + len(row_outs) + n_t]
        vec_out_refs = outs[n_pro + len(row_outs) + n_t:n_pro + len(row_outs) + n_t + n_vec_out]
        vec_vals = [v[...] for v in vec_refs]
        if a_split:
            kh = k // 2
            rhs = [(b_ref[:, :kh], b_ref[:, kh:]) if tb else (b_ref[:kh, :], b_ref[kh:, :])][0]
            rhs = [h.astype(BF16) for h in rhs]
        else:
            rhs = [b_ref[...].astype(BF16)]
        vec_sums = [None] * n_vec_out

        for off, size in _pieces(tm):
            rs = slice(off, off + size)
            if prologue is None:
                lhs = [a_ref[rs, :].astype(BF16) for a_ref in a_refs]
            else:
                lhs = [prologue(a_refs[0][rs, :], vec_vals)]
                outs[0][rs, :] = lhs[0]
            r = lax.dot_general(lhs[0], rhs[0], dims, preferred_element_type=F32)
            for l2, r2 in zip(lhs[1:], rhs[1:]):
                r = r + lax.dot_general(l2, r2, dims, preferred_element_type=F32)
            if second is not None:
                r = r + jnp.dot(sec_refs[0][rs, :].astype(BF16), sec_refs[1][...].astype(BF16),
                                preferred_element_type=F32)
            row_vals, vec_parts = epilogue(r, [x[rs, :] for x in row_refs], vec_vals)
            for ref, val in zip(row_out_refs, row_vals):
                ref[rs, :] = val.astype(ref.dtype)
            for ref in t_out_refs:
                ref[:, rs] = row_vals[0].T.astype(ref.dtype)
            vec_sums = [p if t is None else t + p for t, p in zip(vec_sums, vec_parts)]

        if n_vec_out:
            @pl.when(pl.program_id(0) == 0)
            def _():
                for ref in vec_out_refs:
                    ref[...] = jnp.zeros_like(ref)

            for ref, val in zip(vec_out_refs, vec_sums):
                ref[...] += val

    if a_split:
        a_specs = [pl.BlockSpec((None, tm, k // 2), lambda i: (0, i, 0)),
                   pl.BlockSpec((None, tm, k // 2), lambda i: (1, i, 0))]
    else:
        a_specs = [pl.BlockSpec((tm, k), lambda i: (i, 0))]
    b_spec = pl.BlockSpec(b.shape, lambda i: (0, 0))
    row_spec = pl.BlockSpec((tm, n), lambda i: (i, 0))
    out_specs, out_shape = [], []
    if prologue is not None:
        out_specs.append(pl.BlockSpec((tm, k), lambda i: (i, 0)))
        out_shape.append(jax.ShapeDtypeStruct((m, k), BF16))
    out_specs += [row_spec] * len(row_outs)
    out_shape += [jax.ShapeDtypeStruct((m, n), dt) for dt in row_outs]
    if transposed_out is not None:
        out_specs.append(pl.BlockSpec((n, tm), lambda i: (0, i)))
        out_shape.append(jax.ShapeDtypeStruct((n, m), transposed_out))
    out_specs += [pl.BlockSpec((1, n), lambda i: (0, 0))] * n_vec_out
    out_shape += [jax.ShapeDtypeStruct((1, n), F32)] * n_vec_out
    sec_specs, sec_args = [], []
    if second is not None:
        k2 = second[0].shape[1]
        sec_specs = [pl.BlockSpec((tm, k2), lambda i: (i, 0)), pl.BlockSpec((k2, n), lambda i: (0, 0))]
        sec_args = list(second)
    res = pl.pallas_call(
        body, name=name, grid=(m // tm,),
        in_specs=a_specs + [b_spec] + sec_specs + [row_spec] * nr
        + [pl.BlockSpec((1, v.shape[1]), lambda i: (0, 0)) for v in vecs],
        out_specs=out_specs, out_shape=out_shape,
        compiler_params=_params(("arbitrary" if n_vec_out else "parallel",)),
    )(*([a] * n_a), b, *sec_args, *rows, *vecs)
    split = n_pro + len(row_outs) + n_t
    return list(res[:split]), list(res[split:])


def _pro_rms(a, vecs):
    return (a * _rms_scale(a) * vecs[0]).astype(BF16)


def _epi_plain(r, rows, vecs):
    return [r], []


def _epi_add_rms(r, rows, vecs):
    xn = r + rows[0]
    return [xn, xn * _rms_scale(xn) * vecs[0]], []


def _epi_rms_bwd(r, rows, vecs):
    dv, dg_rows = _rms_bwd_rows(rows[0], vecs[0], r)
    return [dv + rows[1]], [jnp.sum(dg_rows, axis=0, keepdims=True)]


def _mix_pre(z, conv_w8, gq, gkv, cos, sin_a, sin_b, *, ts=256):
    s = z.shape[0]
    n = s // ts
    cw = CONV_WIDTH

    def body(z_ref, xcp, xcn, cgp, cgn, w_ref, gq_ref, gkv_ref, cos_ref, sa_ref, sb_ref,
             yc_ref, qn_ref, kvn_ref, kr_ref):
        i = pl.program_id(0)
        xc = z_ref[:, 0:cw]
        bg = z_ref[:, cw:2 * cw]
        cg = z_ref[:, 2 * cw:3 * cw]
        m = cg * xc
        m_prev = jnp.where(i > 0, xcp[7:8, :] * cgp[7:8, :], 0.0)
        m_next = jnp.where(i < n - 1, xcn[0:1, :] * cgn[0:1, :], 0.0)
        cm = _shift_down(m, m_prev) * w_ref[0:1, :] + m * w_ref[1:2, :] + _shift_up(m, m_next) * w_ref[2:3, :]
        yc_ref[...] = (bg * cm).astype(BF16)
        ql = z_ref[:, 3 * cw:3 * cw + Q_LORA]
        qn_ref[...] = (ql * _rms_scale(ql) * gq_ref[...]).astype(BF16)
        kvl = z_ref[:, 3 * cw + Q_LORA:3 * cw + Q_LORA + KV_LORA]
        kvn_ref[...] = (kvl * _rms_scale(kvl) * gkv_ref[...]).astype(BF16)
        kr_ref[...] = _rope(z_ref[:, D_IN_PAD - HEAD_PAD:D_IN_PAD], cos_ref[...], sa_ref[...], sb_ref[...])

    xcp, xcn = _halo_specs(ts, s, cw, 0)
    cgp, cgn = _halo_specs(ts, s, cw, 2)
    tab = pl.BlockSpec((ts, HEAD_PAD), lambda i: (i, 0))
    return pl.pallas_call(
        body, name="mix_pre", grid=(n,),
        in_specs=[pl.BlockSpec((ts, D_IN_PAD), lambda i: (i, 0)), xcp, xcn, cgp, cgn,
                  pl.BlockSpec((8, cw), lambda i: (0, 0)), pl.BlockSpec((1, Q_LORA), lambda i: (0, 0)),
                  pl.BlockSpec((1, KV_LORA), lambda i: (0, 0)), tab, tab, tab],
        out_specs=[pl.BlockSpec((ts, cw), lambda i: (i, 0)), pl.BlockSpec((ts, Q_LORA), lambda i: (i, 0)),
                   pl.BlockSpec((ts, KV_LORA), lambda i: (i, 0)), tab],
        out_shape=[jax.ShapeDtypeStruct((s, cw), BF16), jax.ShapeDtypeStruct((s, Q_LORA), BF16),
                   jax.ShapeDtypeStruct((s, KV_LORA), BF16), jax.ShapeDtypeStruct((s, HEAD_PAD), F32)],
        compiler_params=_params(("parallel",)),
    )(z, z, z, z, z, conv_w8, gq, gkv, cos, sin_a, sin_b)


def _mix_bwd(z, dyc, dqn, dkvn, dkr, conv_w8, gq, gkv, cos, sin_a, sin_b, *, ts=256):
    s = z.shape[0]
    n = s // ts
    cw = CONV_WIDTH

    def body(z_ref, xcp, xcn, bgp, bgn, cgp, cgn, dyc_ref, dycp, dycn, dqn_ref, dkvn_ref, dkr_ref,
             w_ref, gq_ref, gkv_ref, cos_ref, sa_ref, sb_ref,
             dz_ref, dw0_ref, dw1_ref, dw2_ref, dgq_ref, dgkv_ref):
        i = pl.program_id(0)

        @pl.when(i == 0)
        def _():
            for r in (dw0_ref, dw1_ref, dw2_ref, dgq_ref, dgkv_ref):
                r[...] = jnp.zeros_like(r)

        xc = z_ref[:, 0:cw]
        bg = z_ref[:, cw:2 * cw]
        cg = z_ref[:, 2 * cw:3 * cw]
        w0, w1, w2 = w_ref[0:1, :], w_ref[1:2, :], w_ref[2:3, :]
        m = cg * xc
        m_dn = _shift_down(m, jnp.where(i > 0, xcp[7:8, :] * cgp[7:8, :], 0.0))
        m_up = _shift_up(m, jnp.where(i < n - 1, xcn[0:1, :] * cgn[0:1, :], 0.0))
        cm = m_dn * w0 + m * w1 + m_up * w2
        dyc_v = dyc_ref[...]
        dcm = dyc_v * bg
        dcm_dn = _shift_down(dcm, jnp.where(i > 0, dycp[7:8, :] * bgp[7:8, :], 0.0))
        dcm_up = _shift_up(dcm, jnp.where(i < n - 1, dycn[0:1, :] * bgn[0:1, :], 0.0))
        dm = dcm_up * w0 + dcm * w1 + dcm_dn * w2
        dz_ref[:, 0:cw] = (dm * cg).astype(BF16)
        dz_ref[:, cw:2 * cw] = (dyc_v * cm).astype(BF16)
        dz_ref[:, 2 * cw:3 * cw] = (dm * xc).astype(BF16)
        dw0_ref[...] += jnp.sum(dcm * m_dn, axis=0, keepdims=True)
        dw1_ref[...] += jnp.sum(dcm * m, axis=0, keepdims=True)
        dw2_ref[...] += jnp.sum(dcm * m_up, axis=0, keepdims=True)

        dql, dgq_rows = _rms_bwd_rows(z_ref[:, 3 * cw:3 * cw + Q_LORA], gq_ref[...], dqn_ref[...])
        dz_ref[:, 3 * cw:3 * cw + Q_LORA] = dql.astype(BF16)
        dgq_ref[...] += jnp.sum(dgq_rows, axis=0, keepdims=True)
        dkvl, dgkv_rows = _rms_bwd_rows(z_ref[:, 3 * cw + Q_LORA:3 * cw + Q_LORA + KV_LORA], gkv_ref[...],
                                        dkvn_ref[...])
        dz_ref[:, 3 * cw + Q_LORA:3 * cw + Q_LORA + KV_LORA] = dkvl.astype(BF16)
        dgkv_ref[...] += jnp.sum(dgkv_rows, axis=0, keepdims=True)

        lane = lax.broadcasted_iota(jnp.int32, (ts, HEAD_PAD), 1)
        rope_lane = (lane >= QK_NOPE) & (lane < QK_NOPE + QK_ROPE)
        dk = _rope_bwd(dkr_ref[...], cos_ref[...], sa_ref[...], sb_ref[...])
        dz_ref[:, D_IN_PAD - HEAD_PAD:D_IN_PAD] = jnp.where(rope_lane, dk, 0.0).astype(BF16)

    xcp, xcn = _halo_specs(ts, s, cw, 0)
    bgp, bgn = _halo_specs(ts, s, cw, 1)
    cgp, cgn = _halo_specs(ts, s, cw, 2)
    dycp, dycn = _halo_specs(ts, s, cw, 0)
    tab = pl.BlockSpec((ts, HEAD_PAD), lambda i: (i, 0))

    def vec(width):
        return pl.BlockSpec((1, width), lambda i: (0, 0))

    outs = pl.pallas_call(
        body, name="mix_bwd", grid=(n,),
        in_specs=[pl.BlockSpec((ts, D_IN_PAD), lambda i: (i, 0)), xcp, xcn, bgp, bgn, cgp, cgn,
                  pl.BlockSpec((ts, cw), lambda i: (i, 0)), dycp, dycn,
                  pl.BlockSpec((ts, Q_LORA), lambda i: (i, 0)), pl.BlockSpec((ts, KV_LORA), lambda i: (i, 0)), tab,
                  pl.BlockSpec((8, cw), lambda i: (0, 0)), vec(Q_LORA), vec(KV_LORA), tab, tab, tab],
        out_specs=[pl.BlockSpec((ts, D_IN_PAD), lambda i: (i, 0)), vec(cw), vec(cw), vec(cw), vec(Q_LORA),
                   vec(KV_LORA)],
        out_shape=[jax.ShapeDtypeStruct((s, D_IN_PAD), BF16)] + [jax.ShapeDtypeStruct((1, cw), F32)] * 3
        + [jax.ShapeDtypeStruct((1, Q_LORA), F32), jax.ShapeDtypeStruct((1, KV_LORA), F32)],
        compiler_params=_params(("arbitrary",)),
    )(z, z, z, z, z, z, z, dyc, dyc, dyc, dqn, dkvn, dkr, conv_w8, gq, gkv, cos, sin_a, sin_b)
    dz, dw0, dw1, dw2, dgq, dgkv = outs
    return dz, jnp.concatenate([dw0, dw1, dw2], axis=0), dgq, dgkv


def _qkv_proj(qn, kvn, kr, w_uq_p, w_kv_p, cos, sin_a, sin_b, *, ts=512):
    s = qn.shape[0]

    def body(qn_ref, kvn_ref, kr_ref, wq_ref, wkv_ref, cos_ref, sa_ref, sb_ref, q_ref, k_ref, v_ref, qt_ref):
        cos_v, sa, sb = cos_ref[...], sa_ref[...], sb_ref[...]
        q = jnp.dot(qn_ref[...], wq_ref[...], preferred_element_type=F32)
        kv = jnp.dot(kvn_ref[...], wkv_ref[...], preferred_element_type=F32)
        kr_v = kr_ref[...]
        lane = lax.broadcasted_iota(jnp.int32, (1, HEAD_PAD), 1)
        ones_lane = (lane == ONES_LANE).astype(F32)
        for h in range(N_HEADS):
            blk = slice(h * HEAD_PAD, (h + 1) * HEAD_PAD)
            q_h = _rope(q[:, blk], cos_v, sa, sb) * SM_SCALE
            q_ref[:, blk] = q_h.astype(BF16)
            qt_ref[blk, :] = q_h.T.astype(BF16)
            k_ref[:, blk] = (kv[:, blk] + kr_v).astype(BF16)
            v_ref[:, blk] = (kv[:, D_ATT + h * HEAD_PAD:D_ATT + (h + 1) * HEAD_PAD] + ones_lane).astype(BF16)

    tab = pl.BlockSpec((ts, HEAD_PAD), lambda i: (i, 0))
    wide = pl.BlockSpec((ts, D_ATT), lambda i: (i, 0))
    return pl.pallas_call(
        body, name="qkv_proj", grid=(s // ts,),
        in_specs=[pl.BlockSpec((ts, Q_LORA), lambda i: (i, 0)), pl.BlockSpec((ts, KV_LORA), lambda i: (i, 0)), tab,
                  pl.BlockSpec((Q_LORA, D_ATT), lambda i: (0, 0)), pl.BlockSpec((KV_LORA, 2 * D_ATT), lambda i: (0, 0)),
                  tab, tab, tab],
        out_specs=[wide, wide, wide, pl.BlockSpec((D_ATT, ts), lambda i: (0, i))],
        out_shape=[jax.ShapeDtypeStruct((s, D_ATT), BF16)] * 3 + [jax.ShapeDtypeStruct((D_ATT, s), BF16)],
        compiler_params=_params(("parallel",)),
    )(qn, kvn, kr, w_uq_p, w_kv_p, cos, sin_a, sin_b)


def _qk_bwd(dq, dk, cos, sin_a, sin_b, *, ts=256):
    s = dq.shape[0]

    def body(dq_ref, dk_ref, cos_ref, sa_ref, sb_ref, dqp_ref, dkr_ref):
        cos_v, sa, sb = cos_ref[...], sa_ref[...], sb_ref[...]
        tot = jnp.zeros((ts, HEAD_PAD), F32)
        for h in range(N_HEADS):
            blk = slice(h * HEAD_PAD, (h + 1) * HEAD_PAD)
            dqp_ref[:, blk] = _rope_bwd(dq_ref[:, blk], cos_v, sa, sb).astype(BF16)
            tot = tot + dk_ref[:, blk]
        dkr_ref[...] = tot

    tab = pl.BlockSpec((ts, HEAD_PAD), lambda i: (i, 0))
    wide = pl.BlockSpec((ts, D_ATT), lambda i: (i, 0))
    return pl.pallas_call(
        body, name="qk_bwd", grid=(s // ts,),
        in_specs=[wide, wide, tab, tab, tab], out_specs=[wide, tab],
        out_shape=[jax.ShapeDtypeStruct((s, D_ATT), BF16), jax.ShapeDtypeStruct((s, HEAD_PAD), F32)],
        compiler_params=_params(("parallel",)),
    )(dq, dk, cos, sin_a, sin_b)


_NT = (((1,), (1,)), ((), ()))


def _flash_fwd(q, k, v, *, tq=1024, tk=1024, per_trip=8):
    s = q.shape[0]
    tq, tk = min(tq, s), min(tk, s)
    nk = s // tk
    per_trip = min(per_trip, nk)
    assert nk % per_trip == 0

    def body(q_ref, k_ref, v_ref, o_ref, lse_ref):
        qv = q_ref[...]

        def step(j, carry):
            m, acc = carry
            rows = pl.ds(pl.multiple_of(j * tk, tk), tk)
            sc = lax.dot_general(qv, k_ref[rows, :], _NT, preferred_element_type=F32)
            m_new = jnp.maximum(m, jnp.max(sc, axis=1, keepdims=True))
            p = jnp.exp(sc - m_new).astype(BF16)
            acc = jnp.exp(m - m_new) * acc + jnp.dot(p, v_ref[rows, :], preferred_element_type=F32)
            return m_new, acc

        def trip(t, carry):
            for c in range(per_trip):
                carry = step(per_trip * t + c, carry)
            return carry

        init = (jnp.full((tq, 1), -jnp.inf, F32), jnp.zeros((tq, HEAD_PAD), F32))
        m, acc = lax.fori_loop(0, nk // per_trip, trip, init)
        l = acc[:, ONES_LANE:ONES_LANE + 1]
        o_ref[...] = (acc / l).astype(BF16)
        lse_ref[...] = m + jnp.log(l)

    head = pl.BlockSpec((s, HEAD_PAD), lambda h, i: (0, h))
    return pl.pallas_call(
        body, name="flash_fwd", grid=(N_HEADS, s // tq),
        in_specs=[pl.BlockSpec((tq, HEAD_PAD), lambda h, i: (i, h)), head, head],
        out_specs=[pl.BlockSpec((tq, HEAD_PAD), lambda h, i: (i, h)),
                   pl.BlockSpec((None, tq, 1), lambda h, i: (h, i, 0))],
        out_shape=[jax.ShapeDtypeStruct((s, D_ATT), BF16), jax.ShapeDtypeStruct((N_HEADS, s, 1), F32)],
        compiler_params=_params(("parallel", "parallel")),
    )(q, k, v)


def _attn_delta(do, o, *, ts=512):
    s = do.shape[0]

    def body(do_ref, o_ref, dl_ref):
        for h in range(N_HEADS):
            blk = slice(h * HEAD_PAD, (h + 1) * HEAD_PAD)
            dl_ref[h] = jnp.sum(do_ref[:, blk].astype(F32) * o_ref[:, blk].astype(F32), axis=1, keepdims=True)

    wide = pl.BlockSpec((ts, D_ATT), lambda i: (i, 0))
    return pl.pallas_call(
        body, name="attn_delta", grid=(s // ts,), in_specs=[wide, wide],
        out_specs=pl.BlockSpec((N_HEADS, ts, 1), lambda i: (0, i, 0)),
        out_shape=jax.ShapeDtypeStruct((N_HEADS, s, 1), F32),
        compiler_params=_params(("parallel",)),
    )(do, o)


def _flash_bwd(q, qt, k, v, do, dot, lse, delta, after, *, tq=1024, tk=512, per_trip=8):
    s = q.shape[0]
    tq, tk = min(tq, s), min(tk, s)
    nq = s // tq
    per_trip = min(per_trip, nq)
    assert nq % per_trip == 0

    def body(q_ref, qt_ref, do_ref, dot_ref, lse_ref, dl_ref, k_ref, v_ref, after_ref, dq_ref, dk_ref, dv_ref):
        j = pl.program_id(1)

        @pl.when(j == 0)
        def _():
            dq_ref[...] = jnp.zeros_like(dq_ref)

        kv, vv = k_ref[...], v_ref[...]

        def chunk(i, dk_t, dv_t):
            at = pl.multiple_of(i * tq, tq)
            rows = pl.ds(at, tq)
            sc = lax.dot_general(q_ref[rows, :], kv, _NT, preferred_element_type=F32)
            p = jnp.exp(sc - lse_ref[rows, :])
            dp = lax.dot_general(do_ref[rows, :], vv, _NT, preferred_element_type=F32)
            ds = (p * (dp - dl_ref[rows, :])).astype(BF16)
            dv_t = dv_t + jnp.dot(dot_ref[:, rows], p.astype(BF16), preferred_element_type=F32)
            dk_t = dk_t + jnp.dot(qt_ref[:, rows], ds, preferred_element_type=F32)
            dq_ref[rows, :] += jnp.dot(ds, kv, preferred_element_type=F32)
            return dk_t, dv_t

        def step(i, carry):
            for c in range(per_trip):
                carry = chunk(per_trip * i + c, *carry)
            return carry

        zero = jnp.zeros((HEAD_PAD, tk), F32)
        dk_t, dv_t = lax.fori_loop(0, nq // per_trip, step, (zero, zero))
        dk_ref[...] = dk_t.T
        dv_ref[...] = dv_t.T

        @pl.when(j == pl.num_programs(1) - 1)
        def _():
            dq_ref[...] *= SM_SCALE

    head = pl.BlockSpec((s, HEAD_PAD), lambda h, j: (0, h))
    head_t = pl.BlockSpec((HEAD_PAD, s), lambda h, j: (h, 0))
    stat = pl.BlockSpec((None, s, 1), lambda h, j: (h, 0, 0))
    blk = pl.BlockSpec((tk, HEAD_PAD), lambda h, j: (j, h))
    return pl.pallas_call(
        body, name="flash_bwd", grid=(N_HEADS, s // tk),
        in_specs=[head, head_t, head, head_t, stat, stat, blk, blk, ANY],
        out_specs=[head, blk, blk],
        out_shape=[jax.ShapeDtypeStruct((s, D_ATT), F32)] * 3,
        compiler_params=_params(("parallel", "arbitrary")),
    )(q, qt, do, dot, lse, delta, k, v, after)


FFN_TC = 256
FFN_TG = 1408


FFN_HALO_BF16 = 16
FFN_HALO_F32 = 8


def _row_halo_specs(ts, s, halo, width):
    nb = ts // halo
    last = s // halo - 1
    prev = pl.BlockSpec((halo, width), lambda i, j: (jnp.maximum(i * nb - 1, 0), 0))
    nxt = pl.BlockSpec((halo, width), lambda i, j: (jnp.minimum((i + 1) * nb, last), 0))
    return prev, nxt


def _ext_rows(prev, main, nxt, first, last):
    return jnp.concatenate([jnp.where(first, jnp.zeros_like(prev), prev), main,
                            jnp.where(last, jnp.zeros_like(nxt), nxt)], axis=0)


def _ext_conv(a, w):
    a_dn = pltpu.roll(a, 1, 0)
    a_up = pltpu.roll(a, a.shape[0] - 1, 0)
    return a_dn * w[0:1, :] + a * w[1:2, :] + a_up * w[2:3, :], a_dn, a_up


def _ffn_pieces(tg):
    return [(off, min(FFN_TC, tg - off)) for off in range(0, tg, FFN_TC)]


def _ffn_fwd(hf, w_up, w, b, *, ts=512, tg=FFN_TG):
    s = hf.shape[0]
    n, ng, halo = s // ts, D_FF // tg, FFN_HALO_BF16

    def body(h_ref, hp_ref, hn_ref, wg_ref, wu_ref, cw_ref, cb_ref, a_ref, act_ref):
        i = pl.program_id(0)
        ext = _ext_rows(hp_ref[...], h_ref[...], hn_ref[...], i == 0, i == n - 1)
        for off, width in _ffn_pieces(tg):
            cols = slice(off, off + width)
            gate_up = []
            for half, w_ref in enumerate((wg_ref, wu_ref)):
                a_ext = jnp.dot(ext, w_ref[:, cols], preferred_element_type=F32)
                a_ref[half, :, cols] = a_ext[halo:halo + ts]
                conv = _ext_conv(a_ext, cw_ref[half, :, cols])[0]
                gate_up.append(conv[halo:halo + ts] + cb_ref[half, :, cols])
            g, u = gate_up
            act_ref[:, cols] = (g * _sigmoid(g) * u).astype(BF16)

    prev, nxt = _row_halo_specs(ts, s, halo, D_MODEL)
    return pl.pallas_call(
        body, name="ffn_fwd", grid=(n, ng),
        in_specs=[pl.BlockSpec((ts, D_MODEL), lambda i, j: (i, 0)), prev, nxt,
                  pl.BlockSpec((D_MODEL, tg), lambda i, j: (0, j)), pl.BlockSpec((D_MODEL, tg), lambda i, j: (0, j + ng)),
                  pl.BlockSpec((2, 8, tg), lambda i, j: (0, 0, j)), pl.BlockSpec((2, 1, tg), lambda i, j: (0, 0, j))],
        out_specs=[pl.BlockSpec((2, ts, tg), lambda i, j: (0, i, j)), pl.BlockSpec((ts, tg), lambda i, j: (i, j))],
        out_shape=[jax.ShapeDtypeStruct((2, s, D_FF), F32), jax.ShapeDtypeStruct((s, D_FF), BF16)],
        compiler_params=_params(("parallel", "parallel")),
    )(hf, hf, hf, w_up, w_up, w, b)


def _ffn_bwd(dx2, w_down, a_pre, w, b, *, ts=512, tg=FFN_TG):
    s = dx2.shape[0]
    n, ng, halo = s // ts, D_FF // tg, FFN_HALO_F32
    main = slice(halo, halo + ts)

    def body(dx_ref, dxp_ref, dxn_ref, wd_ref, a_ref, ap_ref, an_ref, cw_ref, cb_ref, o_ref, dw_ref, db_ref):
        i, j = pl.program_id(0), pl.program_id(1)
        first, last = i == 0, i == n - 1

        @pl.when(first & (j == 0))
        def _():
            dw_ref[...] = jnp.zeros_like(dw_ref)
            db_ref[...] = jnp.zeros_like(db_ref)

        dx_ext = _ext_rows(dxp_ref[...], dx_ref[...], dxn_ref[...], first, last).astype(BF16)
        for off, width in _ffn_pieces(tg):
            cols = slice(off, off + width)
            dact = lax.dot_general(dx_ext, wd_ref[cols, :], _NT, preferred_element_type=F32)
            halves = []
            for half in range(2):
                a_ext = _ext_rows(ap_ref[half, :, cols], a_ref[half, :, cols], an_ref[half, :, cols], first, last)
                conv, a_dn, a_up = _ext_conv(a_ext, cw_ref[half, :, cols])
                halves.append((conv + cb_ref[half, :, cols], a_dn, a_ext, a_up))
            g, u = halves[0][0], halves[1][0]
            sg = _sigmoid(g)
            grads = (dact * u * (sg * (1.0 + g * (1.0 - sg))), dact * (g * sg))
            for half in range(2):
                d = grads[half]
                _, a_dn, a_ext, a_up = halves[half]
                wv = cw_ref[half, :, cols]
                d_pre = (pltpu.roll(d, d.shape[0] - 1, 0) * wv[0:1, :] + d * wv[1:2, :]
                         + pltpu.roll(d, 1, 0) * wv[2:3, :])
                o_ref[half, :, cols] = d_pre[main].astype(BF16)
                dm = d[main]
                dw_ref[j, half, 0:1, cols] += jnp.sum(dm * a_dn[main], axis=0, keepdims=True)
                dw_ref[j, half, 1:2, cols] += jnp.sum(dm * a_ext[main], axis=0, keepdims=True)
                dw_ref[j, half, 2:3, cols] += jnp.sum(dm * a_up[main], axis=0, keepdims=True)
                db_ref[j, half, :, cols] += jnp.sum(dm, axis=0, keepdims=True)

    dxp, dxn = _row_halo_specs(ts, s, halo, D_MODEL)
    nb, lastb = ts // halo, s // halo - 1
    a_main = pl.BlockSpec((2, ts, tg), lambda i, j: (0, i, j))
    a_prev = pl.BlockSpec((2, halo, tg), lambda i, j: (0, jnp.maximum(i * nb - 1, 0), j))
    a_next = pl.BlockSpec((2, halo, tg), lambda i, j: (0, jnp.minimum((i + 1) * nb, lastb), j))
    da_pre, dw, db = pl.pallas_call(
        body, name="ffn_bwd", grid=(n, ng),
        in_specs=[pl.BlockSpec((ts, D_MODEL), lambda i, j: (i, 0)), dxp, dxn,
                  pl.BlockSpec((tg, D_MODEL), lambda i, j: (j, 0)), a_main, a_prev, a_next,
                  pl.BlockSpec((2, 8, tg), lambda i, j: (0, 0, j)), pl.BlockSpec((2, 1, tg), lambda i, j: (0, 0, j))],
        out_specs=[a_main, pl.BlockSpec((ng, 2, 8, tg), lambda i, j: (0, 0, 0, 0)),
                   pl.BlockSpec((ng, 2, 1, tg), lambda i, j: (0, 0, 0, 0))],
        out_shape=[jax.ShapeDtypeStruct((2, s, D_FF), BF16), jax.ShapeDtypeStruct((ng, 2, 8, tg), F32),
                   jax.ShapeDtypeStruct((ng, 2, 1, tg), F32)],
        compiler_params=_params(("arbitrary", "arbitrary")),
    )(dx2, dx2, dx2, w_down, a_pre, a_pre, a_pre, w, b)
    return (da_pre, dw.transpose(1, 2, 0, 3).reshape(2, 8, D_FF), db.transpose(1, 2, 0, 3).reshape(2, 1, D_FF))


def _ple_final(x2, n3, p, target, gf, w_pg, w_pp, *, ts=256):
    s, d = x2.shape
    dp = p.shape[1]

    def body(x2_ref, n3_ref, p_ref, t_ref, gf_ref, wg_ref, wp_ref, loss_ref, dx3_ref, dgl_ref, dpp_ref, dgf_ref):
        @pl.when(pl.program_id(0) == 0)
        def _():
            loss_ref[...] = jnp.zeros_like(loss_ref)
            dgf_ref[...] = jnp.zeros_like(dgf_ref)

        gate = _sigmoid(jnp.dot(n3_ref[...], wg_ref[...], preferred_element_type=F32))
        ppv = jnp.dot(p_ref[...].astype(BF16), wp_ref[...], preferred_element_type=F32)
        x3 = x2_ref[...] + gate * ppv
        gfv = gf_ref[...]
        err = x3 * _rms_scale(x3) * gfv - t_ref[...]
        loss_ref[...] += 0.5 * jnp.sum(jnp.mean(err * err, axis=-1, keepdims=True), axis=0, keepdims=True)
        dx3, dgf_rows = _rms_bwd_rows(x3, gfv, err * (1.0 / d))
        dgf_ref[...] += jnp.sum(dgf_rows, axis=0, keepdims=True)
        dx3_ref[...] = dx3
        dgl_ref[...] = (dx3 * ppv * gate * (1.0 - gate)).astype(BF16)
        dpp_ref[...] = (dx3 * gate).astype(BF16)

    row = pl.BlockSpec((ts, d), lambda i: (i, 0))
    vec = pl.BlockSpec((1, d), lambda i: (0, 0))
    return pl.pallas_call(
        body, name="ple_final", grid=(s // ts,),
        in_specs=[row, row, pl.BlockSpec((ts, dp), lambda i: (i, 0)), row, vec,
                  pl.BlockSpec((d, d), lambda i: (0, 0)), pl.BlockSpec((dp, d), lambda i: (0, 0))],
        out_specs=[pl.BlockSpec((1, 128), lambda i: (0, 0)), row, row, row, vec],
        out_shape=[jax.ShapeDtypeStruct((1, 128), F32), jax.ShapeDtypeStruct((s, d), F32),
                   jax.ShapeDtypeStruct((s, d), BF16), jax.ShapeDtypeStruct((s, d), BF16),
                   jax.ShapeDtypeStruct((1, d), F32)],
        compiler_params=_params(("arbitrary",)),
    )(x2, n3, p, target, gf, w_pg, w_pp)


def _row_tile(rows, cols, n_arrays, budget=12 << 20):
    best = None
    for t in range(8, rows + 1, 8):
        if rows % t == 0 and t * cols * 4 * n_arrays <= budget:
            best = t
    return rows if best is None else best


def _sum_slots(a, *, name):
    g, r, c = a.shape
    tr = _row_tile(r, c, g + 1)

    def body(*refs):
        tot = refs[0][...]
        for ref in refs[1:g]:
            tot = tot + ref[...]
        refs[g][...] = tot

    specs = [pl.BlockSpec((None, tr, c), functools.partial(lambda i, slot: (slot, i, 0), slot=k)) for k in range(g)]
    return pl.pallas_call(
        body, name=name, grid=(r // tr,), in_specs=specs, out_specs=pl.BlockSpec((tr, c), lambda i: (i, 0)),
        out_shape=jax.ShapeDtypeStruct((r, c), a.dtype), compiler_params=_params(("parallel",)),
    )(*([a] * g))


def _adamw_refs(w_ref, g_ref, m_ref, v_ref, d_ref, mo_ref, vo_ref):
    gv = g_ref[...]
    mn = ADAM_B1 * m_ref[...] + (1.0 - ADAM_B1) * gv
    vn = ADAM_B2 * v_ref[...] + (1.0 - ADAM_B2) * (gv * gv)
    m_hat = mn / (1.0 - ADAM_B1 ** ADAM_STEP)
    v_hat = vn / (1.0 - ADAM_B2 ** ADAM_STEP)
    d_ref[...] = -ADAM_LR * (m_hat / (jnp.sqrt(v_hat) + ADAM_EPS) + ADAM_WD * w_ref[...])
    mo_ref[...] = mn
    vo_ref[...] = vn


def _adamw_many(ws, gs, ms, vs, *, name):
    n = len(ws)

    def body(*refs):
        ins, outs = refs[:4 * n], refs[4 * n:]
        for a in range(n):
            _adamw_refs(ins[a], ins[n + a], ins[2 * n + a], ins[3 * n + a], outs[3 * a], outs[3 * a + 1],
                        outs[3 * a + 2])

    vm = pl.BlockSpec(memory_space=pltpu.VMEM)
    res = pl.pallas_call(
        body, name=name, in_specs=[vm] * (4 * n), out_specs=[vm] * (3 * n),
        out_shape=[jax.ShapeDtypeStruct(a.shape, F32) for a in ws for _ in range(3)],
    )(*ws, *gs, *ms, *vs)
    return [tuple(res[3 * a:3 * a + 3]) for a in range(n)]


def _adamw(w, g, m, v, *, name):
    r, c = w.shape
    tr = _row_tile(r, c, 7)
    body = _adamw_refs

    blk = pl.BlockSpec((tr, c), lambda i: (i, 0))
    return pl.pallas_call(
        body, name=name, grid=(r // tr,), in_specs=[blk] * 4, out_specs=[blk] * 3,
        out_shape=[jax.ShapeDtypeStruct((r, c), F32)] * 3, compiler_params=_params(("parallel",)),
    )(w, g, m, v)


def _position():
    x, y, c = lax.axis_index("x"), lax.axis_index("y"), lax.axis_index("c")
    return x, y, c


def _other_chips(x, y):
    return [(1 - x, y), (x, 1 - y), (1 - x, 1 - y)]


def _stage_in(srcs, stage, sems):
    cps = [pltpu.make_async_copy(src, stage[a], sems.at[a]) for a, src in enumerate(srcs)]
    for cp in cps:
        cp.start()
    return cps


def _stage_out(staged, stage, dsts, sems):
    cps = []
    for a, dst in enumerate(dsts):
        staged[a].wait()
        cp = pltpu.make_async_copy(stage[a], dst, sems.at[a])
        cp.start()
        cps.append(cp)
    return cps


def _send_other_halves(grads, *, tag):
    n = len(grads)

    def body(*refs):
        ins, sib = refs[:n], refs[n:2 * n]
        send_sems, recv_sems = refs[2 * n:]
        x, y, c = _position()
        remote = []
        for a in range(n):
            half = ins[a].shape[1] // 2
            give = ins[a].at[:, pl.ds(pl.multiple_of((1 - c) * half, 8), half), :]
            rc = pltpu.make_async_remote_copy(
                src_ref=give, dst_ref=sib[a], send_sem=send_sems.at[a], recv_sem=recv_sems.at[a],
                device_id=(x, y, 1 - c), device_id_type=MESH)
            rc.start()
            remote.append(rc)
        for rc in remote:
            rc.wait_recv()
        for rc in remote:
            rc.wait_send()

    return pl.pallas_call(
        body, name="send_other_halves_" + tag, in_specs=[ANY] * n, out_specs=[ANY] * n,
        out_shape=[jax.ShapeDtypeStruct((g.shape[0], g.shape[1] // 2, g.shape[2]), g.dtype) for g in grads],
        scratch_shapes=[pltpu.SemaphoreType.DMA((n,)), pltpu.SemaphoreType.DMA((n,))],
        compiler_params=pltpu.CompilerParams(has_side_effects=True),
    )(*grads)


def _add_own_half(g4, sib, core, *, name):
    g, a2, c = sib.shape
    tr = _row_tile(a2, c, 4)

    def body(core_ref, a_ref, b_ref, o_ref, o16_ref):
        tot = a_ref[...] + b_ref[...]
        o_ref[...] = tot
        o16_ref[...] = tot.astype(BF16)

    blk = pl.BlockSpec((None, tr, c), lambda i, j, core_ref: (i, j, 0))
    return pl.pallas_call(
        body, name=name,
        grid_spec=pltpu.PrefetchScalarGridSpec(
            num_scalar_prefetch=1, grid=(g, a2 // tr),
            in_specs=[pl.BlockSpec((None, None, tr, c), lambda i, j, core_ref: (i, core_ref[0], j, 0)), blk],
            out_specs=[blk, blk]),
        out_shape=[jax.ShapeDtypeStruct(sib.shape, F32), jax.ShapeDtypeStruct(sib.shape, BF16)],
        compiler_params=_params(("parallel", "parallel")),
    )(core, g4.reshape(g, 2, a2, c), sib)


def _sum_chips(landed, own, chip, *, name):
    g, r, c = landed.shape
    tr = _row_tile(r, c, 5)

    def body(chip_ref, *refs):
        me = chip_ref[0]
        own_v = refs[g][...]
        tot = None
        for slot in range(g):
            term = jnp.where(me == slot, own_v, refs[slot][...].astype(F32))
            tot = term if tot is None else tot + term
        refs[g + 1][...] = tot

    def landed_spec(slot):
        return pl.BlockSpec((None, tr, c),
                            lambda i, chip_ref: (jnp.where(chip_ref[0] == slot, (slot + 1) % g, slot), i, 0))

    return pl.pallas_call(
        body, name=name,
        grid_spec=pltpu.PrefetchScalarGridSpec(
            num_scalar_prefetch=1, grid=(r // tr,),
            in_specs=[landed_spec(k) for k in range(g)]
            + [pl.BlockSpec((None, tr, c), lambda i, chip_ref: (chip_ref[0], i, 0))],
            out_specs=pl.BlockSpec((tr, c), lambda i, chip_ref: (i, 0))),
        out_shape=jax.ShapeDtypeStruct((r, c), F32), compiler_params=_params(("parallel",)),
    )(chip, *([landed] * g), own)


def _join_halves(halves):
    n = len(halves)

    def body(*refs):
        ins, outs, stage = refs[:n], refs[n:2 * n], refs[2 * n:3 * n]
        send_sems, recv_sems, in_sems, out_sems = refs[3 * n:]
        x, y, c = _position()
        remote = []
        staged = _stage_in(ins, stage, in_sems)
        for a in range(n):
            rc = pltpu.make_async_remote_copy(
                src_ref=ins[a], dst_ref=outs[a].at[c], send_sem=send_sems.at[a], recv_sem=recv_sems.at[a],
                device_id=(x, y, 1 - c), device_id_type=MESH)
            rc.start()
            remote.append(rc)
        local = _stage_out(staged, stage, [o.at[c] for o in outs], out_sems)
        for a in range(n):
            pltpu.make_async_remote_copy(
                src_ref=ins[a], dst_ref=outs[a].at[1 - c], send_sem=send_sems.at[a], recv_sem=recv_sems.at[a],
                device_id=(x, y, 1 - c), device_id_type=MESH).wait_recv()
        for rc in remote:
            rc.wait_send()
        for cp in local:
            cp.wait()

    return pl.pallas_call(
        body, name="join_halves", in_specs=[ANY] * n, out_specs=[ANY] * n,
        out_shape=[jax.ShapeDtypeStruct((2,) + h.shape, h.dtype) for h in halves],
        scratch_shapes=[pltpu.VMEM(h.shape, h.dtype) for h in halves]
        + [pltpu.SemaphoreType.DMA((n,)), pltpu.SemaphoreType.DMA((n,)), pltpu.SemaphoreType.DMA((n,)),
           pltpu.SemaphoreType.DMA((n,))],
        compiler_params=pltpu.CompilerParams(has_side_effects=True),
    )(*halves)


_HBM = pl.BlockSpec(memory_space=pltpu.HBM)
_SEM = pl.BlockSpec(memory_space=pltpu.SEMAPHORE)


def _chip_copies(srcs, lands, send_sems, recv_sems, scatter):
    x, y, c = _position()
    me = 2 * x + y
    outgoing, incoming = [], []
    for a, (src, land) in enumerate(zip(srcs, lands)):
        for k, (px, py) in enumerate(_other_chips(x, y)):
            peer = 2 * px + py
            sems = dict(send_sem=send_sems.at[3 * a + k], recv_sem=recv_sems.at[3 * a + k], device_id=(px, py, c),
                        device_id_type=MESH)
            outgoing.append(pltpu.make_async_remote_copy(
                src_ref=src.at[peer] if scatter else src, dst_ref=land.at[me], **sems))
            incoming.append(pltpu.make_async_remote_copy(
                src_ref=src.at[me] if scatter else src, dst_ref=land.at[peer], **sems))
    return outgoing, incoming


def _chips_start(srcs, *, scatter, name):
    n = len(srcs)
    lands = [lax.empty(a.shape if scatter else (N_CHIPS,) + a.shape, a.dtype) for a in srcs]

    def body(*refs):
        ins, send_sems, recv_sems, token = refs[:2 * n], refs[2 * n], refs[2 * n + 1], refs[-1]
        outgoing, _ = _chip_copies(ins[:n], ins[n:], send_sems, recv_sems, scatter)
        for cp in outgoing:
            cp.start()
        token[...] = jnp.zeros_like(token)

    bufs = list(srcs) + lands
    res = pl.pallas_call(
        body, name=name, in_specs=[_HBM] * (2 * n),
        out_specs=(_SEM, _SEM, *[_HBM] * (2 * n), pl.BlockSpec(memory_space=pltpu.VMEM)),
        out_shape=(pltpu.SemaphoreType.DMA((3 * n,)), pltpu.SemaphoreType.DMA((3 * n,)),
                   *[pltpu.HBM(a.shape, a.dtype) for a in bufs], jax.ShapeDtypeStruct((8, 128), F32)),
        input_output_aliases={i: 2 + i for i in range(2 * n)},
        compiler_params=pltpu.CompilerParams(has_side_effects=pltpu.SideEffectType.DATAFLOW_SIDE_EFFECTING),
    )(*[pltpu.with_memory_space_constraint(a, pltpu.HBM) for a in bufs])
    return res[0], res[1], list(res[2:2 + n]), list(res[2 + n:2 + 2 * n]), res[-1]


def _chips_wait(handle, after, *, scatter, name):
    send_sems, recv_sems, srcs, lands, _ = handle
    n = len(srcs)

    def body(*refs):
        ins, send_ref, recv_ref = refs[:2 * n], refs[2 * n], refs[2 * n + 1]
        outgoing, incoming = _chip_copies(ins[:n], ins[n:], send_ref, recv_ref, scatter)
        for cp in outgoing:
            cp.wait_send()
        for cp in incoming:
            cp.wait_recv()

    bufs = list(srcs) + list(lands)
    res = pl.pallas_call(
        body, name=name, in_specs=[_HBM] * (2 * n) + [_SEM, _SEM, ANY], out_specs=tuple([_HBM] * (2 * n)),
        out_shape=tuple(pltpu.HBM(a.shape, a.dtype) for a in bufs),
        input_output_aliases={i: i for i in range(2 * n)},
        compiler_params=pltpu.CompilerParams(has_side_effects=pltpu.SideEffectType.DATAFLOW_SIDE_EFFECTING),
    )(*bufs, send_sems, recv_sems, after)
    return list(res[:n]), list(res[n:])


def _gather_all(buf):
    def body(in_ref, out_ref, send_sems, recv_sems, local_sem):
        x, y, c = _position()
        me = 4 * x + 2 * y + c
        peers = [(x, y, 1 - c)] + [(px, py, pc) for (px, py) in _other_chips(x, y) for pc in (c, 1 - c)]
        cp = pltpu.make_async_copy(in_ref, out_ref.at[me], local_sem)
        cp.start()
        remote = []
        for k, peer in enumerate(peers):
            rc = pltpu.make_async_remote_copy(
                src_ref=in_ref, dst_ref=out_ref.at[me], send_sem=send_sems.at[k], recv_sem=recv_sems.at[k],
                device_id=peer, device_id_type=MESH)
            rc.start()
            remote.append(rc)
        for k, (px, py, pc) in enumerate(peers):
            pltpu.make_async_remote_copy(
                src_ref=in_ref, dst_ref=out_ref.at[4 * px + 2 * py + pc], send_sem=send_sems.at[k],
                recv_sem=recv_sems.at[k], device_id=(px, py, pc), device_id_type=MESH).wait_recv()
        for rc in remote:
            rc.wait_send()
        cp.wait()

    return pl.pallas_call(
        body, name="gather_all", in_specs=[ANY], out_specs=ANY,
        out_shape=jax.ShapeDtypeStruct((N_DEV,) + buf.shape, buf.dtype),
        scratch_shapes=[pltpu.SemaphoreType.DMA((N_DEV - 1,)), pltpu.SemaphoreType.DMA((N_DEV - 1,)),
                        pltpu.SemaphoreType.DMA],
        compiler_params=pltpu.CompilerParams(has_side_effects=True),
    )(buf)


def _cols_from_shards(g4):
    _, k, n = g4.shape
    return g4.transpose(1, 0, 2).reshape(k, N_CHIPS * n)


def _cols_to_shards(w):
    k, n = w.shape
    return w.reshape(k, N_CHIPS, n // N_CHIPS).transpose(1, 0, 2)


def _pad_heads(w, width):
    k = w.shape[0]
    w3 = w.reshape(k, N_HEADS, width)
    return jnp.pad(w3, ((0, 0), (0, 0), (0, HEAD_PAD - width))).reshape(k, D_ATT)


def _unpad_heads(w, width):
    k = w.shape[0]
    return w.reshape(k, N_HEADS, HEAD_PAD)[:, :, :width]


def _rope_tables(s, after):
    pos, _ = lax.optimization_barrier((jnp.arange(s, dtype=F32), after))
    inv_freq = ROPE_THETA ** (-jnp.arange(0, QK_ROPE, 2, dtype=F32) / QK_ROPE)
    ang = pos[:, None] * inv_freq[None, :]
    cos_h, sin_h = jnp.cos(ang), jnp.sin(ang)
    half = QK_ROPE // 2
    z = jnp.zeros((s, half), F32)
    ones = jnp.ones((s, QK_NOPE), F32)
    tail = jnp.zeros((s, HEAD_PAD - QK_NOPE - QK_ROPE), F32)
    cos = jnp.concatenate([ones, cos_h, cos_h, tail + 1.0], axis=1)
    sin_a = jnp.concatenate([ones * 0.0, -sin_h, z, tail], axis=1)
    sin_b = jnp.concatenate([ones * 0.0, z, sin_h, tail], axis=1)
    return cos, sin_a, sin_b


def _local_step(x, p, target, wts, late_weights, reduce_early, reduce_last):
    s = x.shape[0]
    cos, sin_a, sin_b = wts["rope"]
    g1, gq, gkv, g2, g3, gf = (wts[k] for k in ("norm_mix_g", "q_norm_g", "kv_norm_g", "norm_ffn_g", "ple_norm_g",
                                                 "final_norm_g"))
    w_in_p, w_uq_p, w_kv_p = wts["w_in_p"], wts["w_uq_p"], wts["w_kv_p"]
    conv_w8, fconv_w, fconv_b = wts["conv_w8"], wts["ffn_conv_w"], wts["ffn_conv_b"]

    (h, z), _ = _mm_fused(x, w_in_p, name="mm_in", prologue=_pro_rms, vecs=[g1], epilogue=_epi_plain, row_outs=[F32])
    y_conv, qn, kvn, kr = _mix_pre(z, conv_w8, gq, gkv, cos, sin_a, sin_b)
    q, k, v, q_t = _qkv_proj(qn, kvn, kr, w_uq_p, w_kv_p, cos, sin_a, sin_b)
    o, lse = _flash_fwd(q, k, v)
    late = late_weights(lse)
    w_o_a, w_o_b, w_up, w_down = late["w_o_a"], late["w_o_b"], late["w_up"], late["w_down"]
    w_pg, w_pp = late["w_ple_gate"], late["w_ple_proj"]
    (x1, hf), _ = _mm_fused(o, w_o_b, second=(y_conv, w_o_a), name="mm_o", rows=[x], vecs=[g2],
                            epilogue=_epi_add_rms, row_outs=[F32, BF16])
    a_pre, act = _ffn_fwd(hf, w_up, fconv_w, fconv_b)
    (x2, n3), _ = _mm_fused(act, w_down, name="mm_down", rows=[x1], vecs=[g3], epilogue=_epi_add_rms,
                            row_outs=[F32, BF16])
    loss, dx3, dgl, dpp, d_gf = _ple_final(x2, n3, p, target, gf, w_pg, w_pp)

    grads, early = {"final_norm_g": d_gf}, {}
    early["w_ple_proj"] = _mm(p, dpp, ta=True, name="mm_d_wpp", tm=256, tn=1024, tk=2048)
    early["w_ple_gate"] = _mm(n3, dgl, ta=True, name="mm_d_wpg", tm=1024, tn=1024, tk=2048)
    (dx2,), (grads["ple_norm_g"],) = _mm_fused(dgl, w_pg, tb=True, name="mm_d_n3", rows=[x2, dx3], vecs=[g3],
                                               epilogue=_epi_rms_bwd, row_outs=[F32], n_vec_out=1)
    early["w_down"] = _mm(act, dx2, ta=True, name="mm_d_wdown", tm=1408, tn=1024, tk=2048)
    da_pre, grads["ffn_conv_w"], grads["ffn_conv_b"] = _ffn_bwd(dx2, w_down, a_pre, fconv_w, fconv_b)
    early["w_up"] = _mm(hf, da_pre, ta=True, b_split=True, name="mm_d_wup", tm=1024, tn=1408, tk=2048,
                       o_shards=True)
    (dx1,), (grads["norm_ffn_g"],) = _mm_fused(da_pre, w_up, tb=True, a_split=True, name="mm_d_hf", rows=[x1, dx2],
                                               vecs=[g2], epilogue=_epi_rms_bwd, row_outs=[F32], n_vec_out=1)
    d_wo_a = _mm(y_conv, dx1, ta=True, name="mm_d_wo_conv", tm=512, tn=1024, tk=2048)
    d_wo_b = _mm(o, dx1, ta=True, name="mm_d_wo_att", tm=1024, tn=1024, tk=2048)
    early["w_o"] = jnp.concatenate([d_wo_a, d_wo_b.reshape(N_HEADS, HEAD_PAD, D_MODEL)[:, :V_HEAD]
                                    .reshape(N_HEADS * V_HEAD, D_MODEL)], axis=0)
    token, finish = reduce_early(early)
    dyc = _mm(dx1, w_o_a, tb=True, name="mm_d_yconv", tm=512, tn=512, tk=1024)
    (do, do_t), _ = _mm_fused(dx1, w_o_b, tb=True, name="mm_d_o", epilogue=_epi_plain, row_outs=[BF16],
                              transposed_out=BF16)
    delta = _attn_delta(do, o)
    dq, dk, dv = _flash_bwd(q, q_t, k, v, do, do_t, lse, delta, token)
    reduced_early = finish(dq)
    dq_pre, dkr = _qk_bwd(dq, dk, cos, sin_a, sin_b)
    grads["w_uq_p"] = _mm(qn, dq_pre, ta=True, name="mm_d_wuq", tm=256, tn=1024, tk=2048)
    dqn = _mm(dq_pre, w_uq_p, tb=True, name="mm_d_qn", tm=512, tn=256, tk=1024)
    grads["w_k_p"] = _mm(kvn, dk, ta=True, name="mm_d_wk", tm=128, tn=1024, tk=2048)
    grads["w_v_p"] = _mm(kvn, dv, ta=True, name="mm_d_wv", tm=128, tn=1024, tk=2048)
    dkvn_k = _mm(dk, w_kv_p[:, :D_ATT], tb=True, name="mm_d_kvn_k", tm=512, tn=128, tk=1024)
    dkvn = _mm(dv, w_kv_p[:, D_ATT:], tb=True, add=dkvn_k, name="mm_d_kvn_v", tm=512, tn=128, tk=1024)
    dz, grads["conv_w"], grads["q_norm_g"], grads["kv_norm_g"] = _mix_bwd(
        z, dyc, dqn, dkvn, dkr, conv_w8, gq, gkv, cos, sin_a, sin_b)
    grads["w_in_p"] = _mm(h, dz, ta=True, name="mm_d_win", tm=1024, tn=1024, tk=2048)
    token, finish = reduce_last({n: grads.pop(n) for n in ("w_in_p", "w_uq_p", "w_k_p", "w_v_p")})
    (grad_x,), (grads["norm_mix_g"],) = _mm_fused(dz, w_in_p, tb=True, name="mm_d_h", rows=[x, dx1],
                                                  vecs=[g1 + token[0, 0]], epilogue=_epi_rms_bwd, row_outs=[F32],
                                                  n_vec_out=1)
    return loss[0, 0], grad_x, grads, reduced_early, finish(grad_x)


_EARLY_W = ("w_in", "w_uq", "w_ukv")
_LATE_W = ("w_o", "w_up", "w_down", "w_ple_gate", "w_ple_proj")
_BIG = _EARLY_W + _LATE_W
_COL_SHARDED = ("w_in", "w_uq", "w_ukv", "w_up", "w_ple_proj")
_SMALL = ("norm_mix_g", "conv_w", "q_norm_g", "kv_norm_g", "norm_ffn_g", "ffn_conv_w", "ffn_conv_b", "ple_norm_g",
          "final_norm_g")


def _full_from_slots(n, g4):
    return _cols_from_shards(g4) if n in _COL_SHARDED else g4.reshape(-1, g4.shape[2])


def _shard_major(n, g):
    if g.ndim == 3:
        return g
    return _cols_to_shards(g) if n in _COL_SHARDED else g.reshape(N_CHIPS, g.shape[0] // N_CHIPS, g.shape[1])


def _early_shards(w):
    shards = [w[n][0].astype(BF16) for n in _EARLY_W]
    shards.append(jnp.pad(w["conv_w"][0], ((0, 5), (0, 0))))
    shards.append(jnp.pad(w["ffn_conv_w"][0], ((0, 5), (0, 0))))
    return shards


def _fill_own_slot(landed, own, chip):
    return [lax.dynamic_update_slice(g4, a[None], (chip[0], 0, 0)) for g4, a in zip(landed, own)]


def _early_weights(got, w):
    full = {n: _full_from_slots(n, g4) for n, g4 in zip(_EARLY_W, got)}
    full["conv_w8"] = _cols_from_shards(got[len(_EARLY_W)])
    full["ffn_conv_w8"] = _cols_from_shards(got[len(_EARLY_W) + 1])
    return _layout_early(full, w)


def _layout_early(full, w):
    out = {n: w[n] for n in ("norm_mix_g", "q_norm_g", "kv_norm_g", "norm_ffn_g", "ple_norm_g")}
    out["final_norm_g"] = w["final_norm_g"][None, :]
    w_in = full["w_in"]
    zc = jnp.zeros((D_MODEL, QK_NOPE), BF16)
    zt = jnp.zeros((D_MODEL, HEAD_PAD - QK_NOPE - QK_ROPE), BF16)
    out["w_in_p"] = jnp.concatenate([w_in[:, :D_IN - QK_ROPE], zc, w_in[:, D_IN - QK_ROPE:], zt], axis=1)
    out["w_uq_p"] = _pad_heads(full["w_uq"], QK_NOPE + QK_ROPE)
    kv3 = full["w_ukv"].reshape(KV_LORA, N_HEADS, QK_NOPE + V_HEAD)
    out["w_kv_p"] = jnp.concatenate([_pad_heads(kv3[:, :, :QK_NOPE].reshape(KV_LORA, -1), QK_NOPE),
                                     _pad_heads(kv3[:, :, QK_NOPE:].reshape(KV_LORA, -1), V_HEAD)], axis=1)
    out["conv_w8"] = full["conv_w8"]
    fw = full["ffn_conv_w8"]
    out["ffn_conv_w"] = jnp.stack([fw[:, :D_FF], fw[:, D_FF:]])
    out["ffn_conv_b"] = w["ffn_conv_b"].reshape(2, 1, D_FF)
    return out


def _layout_late(full):
    w_o = full["w_o"]
    out = {"w_o_a": w_o[:CONV_WIDTH]}
    out["w_o_b"] = jnp.pad(w_o[CONV_WIDTH:].reshape(N_HEADS, V_HEAD, D_MODEL),
                           ((0, 0), (0, HEAD_PAD - V_HEAD), (0, 0))).reshape(D_ATT, D_MODEL)
    for n in ("w_up", "w_down", "w_ple_gate", "w_ple_proj"):
        out[n] = full[n]
    return out


def _true_matrices(g):
    out = {}
    wp = g["w_in_p"]
    out["w_in"] = jnp.concatenate([wp[:, :D_IN - QK_ROPE], wp[:, D_IN_PAD - HEAD_PAD + QK_NOPE:
                                                              D_IN_PAD - HEAD_PAD + QK_NOPE + QK_ROPE]], axis=1)
    out["w_uq"] = _unpad_heads(g["w_uq_p"], QK_NOPE + QK_ROPE).reshape(Q_LORA, -1)
    out["w_ukv"] = jnp.concatenate([_unpad_heads(g["w_k_p"], QK_NOPE), _unpad_heads(g["w_v_p"], V_HEAD)],
                                   axis=2).reshape(KV_LORA, -1)
    return out


def _true_vectors(g):
    out = {}
    out["conv_w"] = g["conv_w"]
    fw = g["ffn_conv_w"]
    out["ffn_conv_w"] = jnp.concatenate([fw[0, :3], fw[1, :3]], axis=1)
    out["ffn_conv_b"] = g["ffn_conv_b"].reshape(1, 2 * D_FF)
    for n in ("norm_mix_g", "q_norm_g", "kv_norm_g", "norm_ffn_g", "ple_norm_g", "final_norm_g"):
        out[n] = g[n]
    return out


def _chip_partials(names, g, core, *, tag):
    g4 = [_shard_major(n, g[n]) for n in names]
    sib = _send_other_halves(g4, tag=tag)
    return [_add_own_half(a, b, core, name="add_cores_" + n) for n, a, b in zip(names, g4, sib)]


_SMALL_SIZES = {"norm_mix_g": D_MODEL, "conv_w": 3 * CONV_WIDTH, "q_norm_g": Q_LORA, "kv_norm_g": KV_LORA,
                "norm_ffn_g": D_MODEL, "ffn_conv_w": 6 * D_FF, "ffn_conv_b": 2 * D_FF, "ple_norm_g": D_MODEL,
                "final_norm_g": D_MODEL}


def _pack(parts, rows):
    flat = jnp.concatenate([a.reshape(-1) for a in parts])
    return jnp.pad(flat, (0, rows * 128 - flat.shape[0])).reshape(rows, 128)


def _unpack(buf, sizes):
    flat = buf.reshape(-1)
    out, at = [], 0
    for n in sizes:
        out.append(flat[at:at + n])
        at += n
    return out


def _reduce_small(g, loss):
    sizes = [1] + [_SMALL_SIZES[n] for n in _SMALL]
    rows = -(-sum(sizes) // 1024) * 8
    slots = _gather_all(_pack([loss] + [g[n] for n in _SMALL], rows))
    parts = _unpack(_sum_slots(slots, name="sum_small"), sizes)
    return parts[0][0], dict(zip(_SMALL, parts[1:]))


def kernel(x, p, norm_mix_g, w_in, conv_w, q_norm_g, w_uq, kv_norm_g, w_ukv, w_o, norm_ffn_g, w_up, ffn_conv_w, ffn_conv_b, w_down, ple_norm_g, w_ple_gate, w_ple_proj, final_norm_g, loss_target, m_norm_mix_g, m_w_in, m_conv_w, m_q_norm_g, m_w_uq, m_kv_norm_g, m_w_ukv, m_w_o, m_norm_ffn_g, m_w_up, m_ffn_conv_w, m_ffn_conv_b, m_w_down, m_ple_norm_g, m_w_ple_gate, m_w_ple_proj, m_final_norm_g, v_norm_mix_g, v_w_in, v_conv_w, v_q_norm_g, v_w_uq, v_kv_norm_g, v_w_ukv, v_w_o, v_norm_ffn_g, v_w_up, v_ffn_conv_w, v_ffn_conv_b, v_w_down, v_ple_norm_g, v_w_ple_gate, v_w_ple_proj, v_final_norm_g):
    names = ["norm_mix_g", "w_in", "conv_w", "q_norm_g", "w_uq", "kv_norm_g", "w_ukv", "w_o", "norm_ffn_g", "w_up",
             "ffn_conv_w", "ffn_conv_b", "w_down", "ple_norm_g", "w_ple_gate", "w_ple_proj", "final_norm_g"]
    w = dict(zip(names, (norm_mix_g, w_in, conv_w, q_norm_g, w_uq, kv_norm_g, w_ukv, w_o, norm_ffn_g, w_up,
                         ffn_conv_w, ffn_conv_b, w_down, ple_norm_g, w_ple_gate, w_ple_proj, final_norm_g)))
    m = dict(zip(names, (m_norm_mix_g, m_w_in, m_conv_w, m_q_norm_g, m_w_uq, m_kv_norm_g, m_w_ukv, m_w_o,
                         m_norm_ffn_g, m_w_up, m_ffn_conv_w, m_ffn_conv_b, m_w_down, m_ple_norm_g, m_w_ple_gate,
                         m_w_ple_proj, m_final_norm_g)))
    v = dict(zip(names, (v_norm_mix_g, v_w_in, v_conv_w, v_q_norm_g, v_w_uq, v_kv_norm_g, v_w_ukv, v_w_o,
                         v_norm_ffn_g, v_w_up, v_ffn_conv_w, v_ffn_conv_b, v_w_down, v_ple_norm_g, v_w_ple_gate,
                         v_w_ple_proj, v_final_norm_g)))

    core = lax.axis_index("c").astype(jnp.int32).reshape(1)
    chip = (2 * lax.axis_index("x") + lax.axis_index("y")).astype(jnp.int32).reshape(1)

    first = _chips_start(_early_shards(w), scatter=False, name="gather_early_start")
    rope = _rope_tables(x.shape[1], first[4])
    own, landed = _chips_wait(first, rope[0], scatter=False, name="gather_early_wait")
    wts = _early_weights(_fill_own_slot(landed, own, chip), w)
    wts["rope"] = rope
    late_shards = [w[n][0].astype(BF16) for n in _LATE_W]
    late_shards[0], _ = lax.optimization_barrier((late_shards[0], own[0]))
    gather = _chips_start(late_shards, scatter=False, name="gather_late_start")
    wts["norm_mix_g"] = wts["norm_mix_g"] + gather[4][0, 0]

    def late_weights(after):
        shards, landed = _chips_wait(gather, after, scatter=False, name="gather_late_wait")
        return _layout_late({n: _full_from_slots(n, g4)
                             for n, g4 in zip(_LATE_W, _fill_own_slot(landed, shards, chip))})

    def reduce_early(g):
        parts = _chip_partials(_LATE_W, g, core, tag="early")
        scatter = _chips_start([t16 for _, t16 in parts], scatter=True, name="scatter_early_start")

        def finish(after):
            _, landed = _chips_wait(scatter, after, scatter=True, name="scatter_early_wait")
            return [_sum_chips(a, t32, chip, name="sum_chips_" + n) for n, a, (t32, _) in zip(_LATE_W, landed, parts)]

        return scatter[4], finish

    def reduce_last(g):
        parts = _chip_partials(_EARLY_W, _true_matrices(g), core, tag="late")
        scatter = _chips_start([t16 for _, t16 in parts], scatter=True, name="scatter_late_start")

        def finish(after):
            _, landed = _chips_wait(scatter, after, scatter=True, name="scatter_late_wait")
            return [_sum_chips(a, t32, chip, name="sum_chips_" + n) for n, a, (t32, _) in zip(_EARLY_W, landed, parts)]

        return scatter[4], finish

    loss, grad_x, small_grads, halves_early, halves_last = _local_step(
        x[0], p[0, 0], loss_target[0], wts, late_weights, reduce_early, reduce_last)
    g_full = _true_vectors(small_grads)
    whole = _join_halves(halves_last + halves_early)
    big = {n: a.reshape(-1, a.shape[2]) for n, a in zip(_BIG, whole)}

    g_out, d_out, m_out, v_out = {}, {}, {}, {}
    for n in _BIG:
        shape = w[n].shape
        g = big[n]
        d, mn, vn = _adamw(w[n][0], g, m[n][0], v[n][0], name="adamw_" + n)
        g_out[n], d_out[n], m_out[n], v_out[n] = (a.reshape(shape) for a in (g, d, mn, vn))

    loss, small = _reduce_small(g_full, loss)
    chip = 2 * lax.axis_index("x") + lax.axis_index("y")
    g_small = {}
    for n in _SMALL:
        shape = w[n].shape
        g = small[n]
        if n in ("conv_w", "ffn_conv_w"):
            width = shape[-1]
            g = lax.dynamic_slice(g.reshape(3, N_CHIPS * width), (0, chip * width), (3, width))
        g_small[n] = g.reshape(shape)
    flat = [[src[n].reshape(-1, src[n].shape[-1]) for n in _SMALL] for src in (w, g_small, m, v)]
    for n, (d, mn, vn) in zip(_SMALL, _adamw_many(*flat, name="adamw_small")):
        shape = w[n].shape
        g_out[n], d_out[n], m_out[n], v_out[n] = g_small[n], d.reshape(shape), mn.reshape(shape), vn.reshape(shape)

    return (loss, grad_x[None], *[g_out[n] for n in names], *[d_out[n] for n in names],
            *[m_out[n] for n in names], *[v_out[n] for n in names])
```

```python
import functools

import jax
import jax.numpy as jnp
from jax import lax
from jax.experimental import pallas as pl
from jax.experimental.pallas import tpu as pltpu

F32 = jnp.float32
BF16 = jnp.bfloat16

D_MODEL = 1024
CONV_WIDTH = 512
Q_LORA = 256
KV_LORA = 128
QK_NOPE = 64
QK_ROPE = 32
V_HEAD = 64
N_HEADS = 8
HEAD_PAD = 128
D_ATT = N_HEADS * HEAD_PAD
D_IN = 3 * CONV_WIDTH + Q_LORA + KV_LORA + QK_ROPE
D_IN_PAD = 3 * CONV_WIDTH + Q_LORA + KV_LORA + HEAD_PAD
D_FF = 2816
ROPE_THETA = 10000.0
EPS = 1e-6
SM_SCALE = (QK_NOPE + QK_ROPE) ** -0.5
ONES_LANE = V_HEAD

ADAM_LR = 0.001
ADAM_B1 = 0.9
ADAM_B2 = 0.999
ADAM_EPS = 1e-08
ADAM_WD = 0.01
ADAM_STEP = 10

N_CHIPS = 4
N_DEV = 8
MESH = pl.DeviceIdType.MESH
ANY = pl.BlockSpec(memory_space=pl.ANY)


def _params(sem):
    return pltpu.CompilerParams(dimension_semantics=sem)


MM_PIECE = 256


def _pieces(total, width=MM_PIECE):
    return [(off, min(width, total - off)) for off in range(0, total, width)]


def _mm(a, b, *, name, ta=False, tb=False, add=None, out_dtype=F32, tm=512, tn=512, tk=512, b_split=False,
        o_shards=False):
    k, m = a.shape if ta else a.shape[::-1]
    if b_split:
        _, kb, nh = b.shape
        n = 2 * nh
    elif tb:
        n, kb = b.shape
    else:
        kb, n = b.shape
    assert kb == k, (name, a.shape, b.shape)
    tm, tn, tk = min(tm, m), min(tn, n), min(tk, k)
    assert m % tm == 0 and n % tn == 0 and k % tk == 0, (name, m, n, k, tm, tn, tk)
    gm, gn, gk = m // tm, n // tn, k // tk

    a_spec = pl.BlockSpec((tk, tm), lambda i, j, kk: (kk, i)) if ta else pl.BlockSpec((tm, tk), lambda i, j, kk: (i, kk))
    if b_split:
        assert gn % 2 == 0
        b_spec = pl.BlockSpec((None, tk, tn), lambda i, j, kk: (j // (gn // 2), kk, j % (gn // 2)))
    elif tb:
        b_spec = pl.BlockSpec((tn, tk), lambda i, j, kk: (j, kk))
    else:
        b_spec = pl.BlockSpec((tk, tn), lambda i, j, kk: (kk, j))
    if o_shards:
        o_spec = pl.BlockSpec((None, tm, tn), lambda i, j, kk: (j, i, 0))
        o_shape = jax.ShapeDtypeStruct((gn, m, tn), out_dtype)
    else:
        o_spec = pl.BlockSpec((tm, tn), lambda i, j, kk: (i, j))
        o_shape = jax.ShapeDtypeStruct((m, n), out_dtype)
    dims = (((0 if ta else 1,), (1 if tb else 0,)), ((), ()))

    def body(*refs):
        a_ref, b_ref = refs[:2]
        add_ref = None if add is None else refs[2]
        o_ref = refs[2 if add is None else 3]
        acc_ref = None if gk == 1 else refs[-1]
        kk = pl.program_id(2)
        rhs = b_ref[...].astype(BF16)

        def finish(r, rows):
            if add_ref is not None:
                r = r + add_ref[rows, :]
            o_ref[rows, :] = r.astype(o_ref.dtype)

        for off, size in _pieces(tm):
            rows = slice(off, off + size)
            lhs = (a_ref[:, rows] if ta else a_ref[rows, :]).astype(BF16)
            part = lax.dot_general(lhs, rhs, dims, preferred_element_type=F32)
            if gk == 1:
                finish(part, rows)
            else:
                acc_ref[rows, :] = jnp.where(kk == 0, part, acc_ref[rows, :] + part)

        if gk > 1:
            @pl.when(kk == gk - 1)
            def _():
                finish(acc_ref[...], slice(None))

    in_specs = [a_spec, b_spec]
    args = [a, b]
    if add is not None:
        in_specs.append(pl.BlockSpec((tm, tn), lambda i, j, kk: (i, j)))
        args.append(add)
    return pl.pallas_call(
        body, name=name, grid=(gm, gn, gk), in_specs=in_specs, out_specs=o_spec, out_shape=o_shape,
        scratch_shapes=[] if gk == 1 else [pltpu.VMEM((tm, tn), F32)],
        compiler_params=_params(("parallel", "parallel", "arbitrary")),
    )(*args)


def _rms_scale(v):
    return lax.rsqrt(jnp.mean(v * v, axis=-1, keepdims=True) + EPS)


def _rms_bwd_rows(v, g, dy):
    r = _rms_scale(v)
    vh = v * r
    dyg = dy * g
    dv = r * (dyg - vh * jnp.mean(dyg * vh, axis=-1, keepdims=True))
    return dv, dy * vh


def _shift_down(v, first_row):
    row = lax.broadcasted_iota(jnp.int32, v.shape, 0)
    return jnp.where(row == 0, first_row, pltpu.roll(v, 1, 0))


def _shift_up(v, last_row):
    n = v.shape[0]
    row = lax.broadcasted_iota(jnp.int32, v.shape, 0)
    return jnp.where(row == n - 1, last_row, pltpu.roll(v, n - 1, 0))


def _rope(t, cos, sin_a, sin_b):
    return t * cos + pltpu.roll(t, HEAD_PAD - 16, 1) * sin_a + pltpu.roll(t, 16, 1) * sin_b


def _rope_bwd(d, cos, sin_a, sin_b):
    return d * cos + pltpu.roll(d * sin_a, 16, 1) + pltpu.roll(d * sin_b, HEAD_PAD - 16, 1)


def _sigmoid(v):
    return 1.0 / (1.0 + jnp.exp(-v))


def _halo_specs(ts, s, width, col):
    nb = ts // 8
    last = s // 8 - 1
    prev = pl.BlockSpec((8, width), lambda i: (jnp.maximum(i * nb - 1, 0), col))
    nxt = pl.BlockSpec((8, width), lambda i: (jnp.minimum((i + 1) * nb, last), col))
    return prev, nxt


def _mm_fused(a, b, *, name, epilogue, row_outs, rows=(), vecs=(), n_vec_out=0, tb=False, a_split=False,
              prologue=None, second=None, transposed_out=None, tm=512):
    if a_split:
        _, m, kh = a.shape
        k = 2 * kh
    else:
        m, k = a.shape
    n = b.shape[0] if tb else b.shape[1]
    assert (b.shape[1] if tb else b.shape[0]) == k, (name, a.shape, b.shape)
    assert m % tm == 0, (name, m, tm)
    n_a = 2 if a_split else 1
    nr, nv = len(rows), len(vecs)
    n_pro = 0 if prologue is None else 1
    n_sec = 0 if second is None else 2
    n_t = 0 if transposed_out is None else 1
    dims = (((1,), (1 if tb else 0,)), ((), ()))

    def body(*refs):
        a_refs, b_ref = refs[:n_a], refs[n_a]
        refs = refs[n_a + 1:]
        sec_refs = refs[:n_sec]
        row_refs, vec_refs = refs[n_sec:n_sec + nr], refs[n_sec + nr:n_sec + nr + nv]
        outs = refs[n_sec + nr + nv:]
        row_out_refs = outs[n_pro:n_pro + len(row_outs)]
        t_out_refs = outs[n_pro + len(row_outs):n_pro + len(row_outs) + n_t]
        vec_out_refs = outs[n_pro + len(row_outs) + n_t:n_pro + len(row_outs) + n_t + n_vec_out]
        vec_vals = [v[...] for v in vec_refs]
        if a_split:
            kh = k // 2
            rhs = [(b_ref[:, :kh], b_ref[:, kh:]) if tb else (b_ref[:kh, :], b_ref[kh:, :])][0]
            rhs = [h.astype(BF16) for h in rhs]
        else:
            rhs = [b_ref[...].astype(BF16)]
        vec_sums = [None] * n_vec_out

        for off, size in _pieces(tm):
            rs = slice(off, off + size)
            if prologue is None:
                lhs = [a_ref[rs, :].astype(BF16) for a_ref in a_refs]
            else:
                lhs = [prologue(a_refs[0][rs, :], vec_vals)]
                outs[0][rs, :] = lhs[0]
            r = lax.dot_general(lhs[0], rhs[0], dims, preferred_element_type=F32)
            for l2, r2 in zip(lhs[1:], rhs[1:]):
                r = r + lax.dot_general(l2, r2, dims, preferred_element_type=F32)
            if second is not None:
                r = r + jnp.dot(sec_refs[0][rs, :].astype(BF16), sec_refs[1][...].astype(BF16),
                                preferred_element_type=F32)
            row_vals, vec_parts = epilogue(r, [x[rs, :] for x in row_refs], vec_vals)
            for ref, val in zip(row_out_refs, row_vals):
                ref[rs, :] = val.astype(ref.dtype)
            for ref in t_out_refs:
                ref[:, rs] = row_vals[0].T.astype(ref.dtype)
            vec_sums = [p if t is None else t + p for t, p in zip(vec_sums, vec_parts)]

        if n_vec_out:
            @pl.when(pl.program_id(0) == 0)
            def _():
                for ref in vec_out_refs:
                    ref[...] = jnp.zeros_like(ref)

            for ref, val in zip(vec_out_refs, vec_sums):
                ref[...] += val

    if a_split:
        a_specs = [pl.BlockSpec((None, tm, k // 2), lambda i: (0, i, 0)),
                   pl.BlockSpec((None, tm, k // 2), lambda i: (1, i, 0))]
    else:
        a_specs = [pl.BlockSpec((tm, k), lambda i: (i, 0))]
    b_spec = pl.BlockSpec(b.shape, lambda i: (0, 0))
    row_spec = pl.BlockSpec((tm, n), lambda i: (i, 0))
    out_specs, out_shape = [], []
    if prologue is not None:
        out_specs.append(pl.BlockSpec((tm, k), lambda i: (i, 0)))
        out_shape.append(jax.ShapeDtypeStruct((m, k), BF16))
    out_specs += [row_spec] * len(row_outs)
    out_shape += [jax.ShapeDtypeStruct((m, n), dt) for dt in row_outs]
    if transposed_out is not None:
        out_specs.append(pl.BlockSpec((n, tm), lambda i: (0, i)))
        out_shape.append(jax.ShapeDtypeStruct((n, m), transposed_out))
    out_specs += [pl.BlockSpec((1, n), lambda i: (0, 0))] * n_vec_out
    out_shape += [jax.ShapeDtypeStruct((1, n), F32)] * n_vec_out
    sec_specs, sec_args = [], []
    if second is not None:
        k2 = second[0].shape[1]
        sec_specs = [pl.BlockSpec((tm, k2), lambda i: (i, 0)), pl.BlockSpec((k2, n), lambda i: (0, 0))]
        sec_args = list(second)
    res = pl.pallas_call(
        body, name=name, grid=(m // tm,),
        in_specs=a_specs + [b_spec] + sec_specs + [row_spec] * nr
        + [pl.BlockSpec((1, v.shape[1]), lambda i: (0, 0)) for v in vecs],
        out_specs=out_specs, out_shape=out_shape,
        compiler_params=_params(("arbitrary" if n_vec_out else "parallel",)),
    )(*([a] * n_a), b, *sec_args, *rows, *vecs)
    split = n_pro + len(row_outs) + n_t
    return list(res[:split]), list(res[split:])


def _pro_rms(a, vecs):
    return (a * _rms_scale(a) * vecs[0]).astype(BF16)


def _epi_plain(r, rows, vecs):
    return [r], []


def _epi_add_rms(r, rows, vecs):
    xn = r + rows[0]
    return [xn, xn * _rms_scale(xn) * vecs[0]], []


def _epi_rms_bwd(r, rows, vecs):
    dv, dg_rows = _rms_bwd_rows(rows[0], vecs[0], r)
    return [dv + rows[1]], [jnp.sum(dg_rows, axis=0, keepdims=True)]


def _mix_pre(z, conv_w8, gq, gkv, cos, sin_a, sin_b, *, ts=256):
    s = z.shape[0]
    n = s // ts
    cw = CONV_WIDTH

    def body(z_ref, xcp, xcn, cgp, cgn, w_ref, gq_ref, gkv_ref, cos_ref, sa_ref, sb_ref,
             yc_ref, qn_ref, kvn_ref, kr_ref):
        i = pl.program_id(0)
        xc = z_ref[:, 0:cw]
        bg = z_ref[:, cw:2 * cw]
        cg = z_ref[:, 2 * cw:3 * cw]
        m = cg * xc
        m_prev = jnp.where(i > 0, xcp[7:8, :] * cgp[7:8, :], 0.0)
        m_next = jnp.where(i < n - 1, xcn[0:1, :] * cgn[0:1, :], 0.0)
        cm = _shift_down(m, m_prev) * w_ref[0:1, :] + m * w_ref[1:2, :] + _shift_up(m, m_next) * w_ref[2:3, :]
        yc_ref[...] = (bg * cm).astype(BF16)
        ql = z_ref[:, 3 * cw:3 * cw + Q_LORA]
        qn_ref[...] = (ql * _rms_scale(ql) * gq_ref[...]).astype(BF16)
        kvl = z_ref[:, 3 * cw + Q_LORA:3 * cw + Q_LORA + KV_LORA]
        kvn_ref[...] = (kvl * _rms_scale(kvl) * gkv_ref[...]).astype(BF16)
        kr_ref[...] = _rope(z_ref[:, D_IN_PAD - HEAD_PAD:D_IN_PAD], cos_ref[...], sa_ref[...], sb_ref[...])

    xcp, xcn = _halo_specs(ts, s, cw, 0)
    cgp, cgn = _halo_specs(ts, s, cw, 2)
    tab = pl.BlockSpec((ts, HEAD_PAD), lambda i: (i, 0))
    return pl.pallas_call(
        body, name="mix_pre", grid=(n,),
        in_specs=[pl.BlockSpec((ts, D_IN_PAD), lambda i: (i, 0)), xcp, xcn, cgp, cgn,
                  pl.BlockSpec((8, cw), lambda i: (0, 0)), pl.BlockSpec((1, Q_LORA), lambda i: (0, 0)),
                  pl.BlockSpec((1, KV_LORA), lambda i: (0, 0)), tab, tab, tab],
        out_specs=[pl.BlockSpec((ts, cw), lambda i: (i, 0)), pl.BlockSpec((ts, Q_LORA), lambda i: (i, 0)),
                   pl.BlockSpec((ts, KV_LORA), lambda i: (i, 0)), tab],
        out_shape=[jax.ShapeDtypeStruct((s, cw), BF16), jax.ShapeDtypeStruct((s, Q_LORA), BF16),
                   jax.ShapeDtypeStruct((s, KV_LORA), BF16), jax.ShapeDtypeStruct((s, HEAD_PAD), F32)],
        compiler_params=_params(("parallel",)),
    )(z, z, z, z, z, conv_w8, gq, gkv, cos, sin_a, sin_b)


def _mix_bwd(z, dyc, dqn, dkvn, dkr, conv_w8, gq, gkv, cos, sin_a, sin_b, *, ts=256):
    s = z.shape[0]
    n = s // ts
    cw = CONV_WIDTH

    def body(z_ref, xcp, xcn, bgp, bgn, cgp, cgn, dyc_ref, dycp, dycn, dqn_ref, dkvn_ref, dkr_ref,
             w_ref, gq_ref, gkv_ref, cos_ref, sa_ref, sb_ref,
             dz_ref, dw0_ref, dw1_ref, dw2_ref, dgq_ref, dgkv_ref):
        i = pl.program_id(0)

        @pl.when(i == 0)
        def _():
            for r in (dw0_ref, dw1_ref, dw2_ref, dgq_ref, dgkv_ref):
                r[...] = jnp.zeros_like(r)

        xc = z_ref[:, 0:cw]
        bg = z_ref[:, cw:2 * cw]
        cg = z_ref[:, 2 * cw:3 * cw]
        w0, w1, w2 = w_ref[0:1, :], w_ref[1:2, :], w_ref[2:3, :]
        m = cg * xc
        m_dn = _shift_down(m, jnp.where(i > 0, xcp[7:8, :] * cgp[7:8, :], 0.0))
        m_up = _shift_up(m, jnp.where(i < n - 1, xcn[0:1, :] * cgn[0:1, :], 0.0))
        cm = m_dn * w0 + m * w1 + m_up * w2
        dyc_v = dyc_ref[...]
        dcm = dyc_v * bg
        dcm_dn = _shift_down(dcm, jnp.where(i > 0, dycp[7:8, :] * bgp[7:8, :], 0.0))
        dcm_up = _shift_up(dcm, jnp.where(i < n - 1, dycn[0:1, :] * bgn[0:1, :], 0.0))
        dm = dcm_up * w0 + dcm * w1 + dcm_dn * w2
        dz_ref[:, 0:cw] = (dm * cg).astype(BF16)
        dz_ref[:, cw:2 * cw] = (dyc_v * cm).astype(BF16)
        dz_ref[:, 2 * cw:3 * cw] = (dm * xc).astype(BF16)
        dw0_ref[...] += jnp.sum(dcm * m_dn, axis=0, keepdims=True)
        dw1_ref[...] += jnp.sum(dcm * m, axis=0, keepdims=True)
        dw2_ref[...] += jnp.sum(dcm * m_up, axis=0, keepdims=True)

        dql, dgq_rows = _rms_bwd_rows(z_ref[:, 3 * cw:3 * cw + Q_LORA], gq_ref[...], dqn_ref[...])
        dz_ref[:, 3 * cw:3 * cw + Q_LORA] = dql.astype(BF16)
        dgq_ref[...] += jnp.sum(dgq_rows, axis=0, keepdims=True)
        dkvl, dgkv_rows = _rms_bwd_rows(z_ref[:, 3 * cw + Q_LORA:3 * cw + Q_LORA + KV_LORA], gkv_ref[...],
                                        dkvn_ref[...])
        dz_ref[:, 3 * cw + Q_LORA:3 * cw + Q_LORA + KV_LORA] = dkvl.astype(BF16)
        dgkv_ref[...] += jnp.sum(dgkv_rows, axis=0, keepdims=True)

        lane = lax.broadcasted_iota(jnp.int32, (ts, HEAD_PAD), 1)
        rope_lane = (lane >= QK_NOPE) & (lane < QK_NOPE + QK_ROPE)
        dk = _rope_bwd(dkr_ref[...], cos_ref[...], sa_ref[...], sb_ref[...])
        dz_ref[:, D_IN_PAD - HEAD_PAD:D_IN_PAD] = jnp.where(rope_lane, dk, 0.0).astype(BF16)

    xcp, xcn = _halo_specs(ts, s, cw, 0)
    bgp, bgn = _halo_specs(ts, s, cw, 1)
    cgp, cgn = _halo_specs(ts, s, cw, 2)
    dycp, dycn = _halo_specs(ts, s, cw, 0)
    tab = pl.BlockSpec((ts, HEAD_PAD), lambda i: (i, 0))

    def vec(width):
        return pl.BlockSpec((1, width), lambda i: (0, 0))

    outs = pl.pallas_call(
        body, name="mix_bwd", grid=(n,),
        in_specs=[pl.BlockSpec((ts, D_IN_PAD), lambda i: (i, 0)), xcp, xcn, bgp, bgn, cgp, cgn,
                  pl.BlockSpec((ts, cw), lambda i: (i, 0)), dycp, dycn,
                  pl.BlockSpec((ts, Q_LORA), lambda i: (i, 0)), pl.BlockSpec((ts, KV_LORA), lambda i: (i, 0)), tab,
                  pl.BlockSpec((8, cw), lambda i: (0, 0)), vec(Q_LORA), vec(KV_LORA), tab, tab, tab],
        out_specs=[pl.BlockSpec((ts, D_IN_PAD), lambda i: (i, 0)), vec(cw), vec(cw), vec(cw), vec(Q_LORA),
                   vec(KV_LORA)],
        out_shape=[jax.ShapeDtypeStruct((s, D_IN_PAD), BF16)] + [jax.ShapeDtypeStruct((1, cw), F32)] * 3
        + [jax.ShapeDtypeStruct((1, Q_LORA), F32), jax.ShapeDtypeStruct((1, KV_LORA), F32)],
        compiler_params=_params(("arbitrary",)),
    )(z, z, z, z, z, z, z, dyc, dyc, dyc, dqn, dkvn, dkr, conv_w8, gq, gkv, cos, sin_a, sin_b)
    dz, dw0, dw1, dw2, dgq, dgkv = outs
    return dz, jnp.concatenate([dw0, dw1, dw2], axis=0), dgq, dgkv


def _qkv_proj(qn, kvn, kr, w_uq_p, w_kv_p, cos, sin_a, sin_b, *, ts=512):
    s = qn.shape[0]

    def body(qn_ref, kvn_ref, kr_ref, wq_ref, wkv_ref, cos_ref, sa_ref, sb_ref, q_ref, k_ref, v_ref, qt_ref):
        cos_v, sa, sb = cos_ref[...], sa_ref[...], sb_ref[...]
        q = jnp.dot(qn_ref[...], wq_ref[...], preferred_element_type=F32)
        kv = jnp.dot(kvn_ref[...], wkv_ref[...], preferred_element_type=F32)
        kr_v = kr_ref[...]
        lane = lax.broadcasted_iota(jnp.int32, (1, HEAD_PAD), 1)
        ones_lane = (lane == ONES_LANE).astype(F32)
        for h in range(N_HEADS):
            blk = slice(h * HEAD_PAD, (h + 1) * HEAD_PAD)
            q_h = _rope(q[:, blk], cos_v, sa, sb) * SM_SCALE
            q_ref[:, blk] = q_h.astype(BF16)
            qt_ref[blk, :] = q_h.T.astype(BF16)
            k_ref[:, blk] = (kv[:, blk] + kr_v).astype(BF16)
            v_ref[:, blk] = (kv[:, D_ATT + h * HEAD_PAD:D_ATT + (h + 1) * HEAD_PAD] + ones_lane).astype(BF16)

    tab = pl.BlockSpec((ts, HEAD_PAD), lambda i: (i, 0))
    wide = pl.BlockSpec((ts, D_ATT), lambda i: (i, 0))
    return pl.pallas_call(
        body, name="qkv_proj", grid=(s // ts,),
        in_specs=[pl.BlockSpec((ts, Q_LORA), lambda i: (i, 0)), pl.BlockSpec((ts, KV_LORA), lambda i: (i, 0)), tab,
                  pl.BlockSpec((Q_LORA, D_ATT), lambda i: (0, 0)), pl.BlockSpec((KV_LORA, 2 * D_ATT), lambda i: (0, 0)),
                  tab, tab, tab],
        out_specs=[wide, wide, wide, pl.BlockSpec((D_ATT, ts), lambda i: (0, i))],
        out_shape=[jax.ShapeDtypeStruct((s, D_ATT), BF16)] * 3 + [jax.ShapeDtypeStruct((D_ATT, s), BF16)],
        compiler_params=_params(("parallel",)),
    )(qn, kvn, kr, w_uq_p, w_kv_p, cos, sin_a, sin_b)


def _qk_bwd(dq, dk, cos, sin_a, sin_b, *, ts=256):
    s = dq.shape[0]

    def body(dq_ref, dk_ref, cos_ref, sa_ref, sb_ref, dqp_ref, dkr_ref):
        cos_v, sa, sb = cos_ref[...], sa_ref[...], sb_ref[...]
        tot = jnp.zeros((ts, HEAD_PAD), F32)
        for h in range(N_HEADS):
            blk = slice(h * HEAD_PAD, (h + 1) * HEAD_PAD)
            dqp_ref[:, blk] = _rope_bwd(dq_ref[:, blk], cos_v, sa, sb).astype(BF16)
            tot = tot + dk_ref[:, blk]
        dkr_ref[...] = tot

    tab = pl.BlockSpec((ts, HEAD_PAD), lambda i: (i, 0))
    wide = pl.BlockSpec((ts, D_ATT), lambda i: (i, 0))
    return pl.pallas_call(
        body, name="qk_bwd", grid=(s // ts,),
        in_specs=[wide, wide, tab, tab, tab], out_specs=[wide, tab],
        out_shape=[jax.ShapeDtypeStruct((s, D_ATT), BF16), jax.ShapeDtypeStruct((s, HEAD_PAD), F32)],
        compiler_params=_params(("parallel",)),
    )(dq, dk, cos, sin_a, sin_b)


_NT = (((1,), (1,)), ((), ()))


def _flash_fwd(q, k, v, *, tq=1024, tk=1024, per_trip=8):
    s = q.shape[0]
    tq, tk = min(tq, s), min(tk, s)
    nk = s // tk
    per_trip = min(per_trip, nk)
    assert nk % per_trip == 0

    def body(q_ref, k_ref, v_ref, o_ref, lse_ref):
        qv = q_ref[...]

        def step(j, carry):
            m, acc = carry
            rows = pl.ds(pl.multiple_of(j * tk, tk), tk)
            sc = lax.dot_general(qv, k_ref[rows, :], _NT, preferred_element_type=F32)
            m_new = jnp.maximum(m, jnp.max(sc, axis=1, keepdims=True))
            p = jnp.exp(sc - m_new).astype(BF16)
            acc = jnp.exp(m - m_new) * acc + jnp.dot(p, v_ref[rows, :], preferred_element_type=F32)
            return m_new, acc

        def trip(t, carry):
            for c in range(per_trip):
                carry = step(per_trip * t + c, carry)
            return carry

        init = (jnp.full((tq, 1), -jnp.inf, F32), jnp.zeros((tq, HEAD_PAD), F32))
        m, acc = lax.fori_loop(0, nk // per_trip, trip, init)
        l = acc[:, ONES_LANE:ONES_LANE + 1]
        o_ref[...] = (acc / l).astype(BF16)
        lse_ref[...] = m + jnp.log(l)

    head = pl.BlockSpec((s, HEAD_PAD), lambda h, i: (0, h))
    return pl.pallas_call(
        body, name="flash_fwd", grid=(N_HEADS, s // tq),
        in_specs=[pl.BlockSpec((tq, HEAD_PAD), lambda h, i: (i, h)), head, head],
        out_specs=[pl.BlockSpec((tq, HEAD_PAD), lambda h, i: (i, h)),
                   pl.BlockSpec((None, tq, 1), lambda h, i: (h, i, 0))],
        out_shape=[jax.ShapeDtypeStruct((s, D_ATT), BF16), jax.ShapeDtypeStruct((N_HEADS, s, 1), F32)],
        compiler_params=_params(("parallel", "parallel")),
    )(q, k, v)


def _attn_delta(do, o, *, ts=512):
    s = do.shape[0]

    def body(do_ref, o_ref, dl_ref):
        for h in range(N_HEADS):
            blk = slice(h * HEAD_PAD, (h + 1) * HEAD_PAD)
            dl_ref[h] = jnp.sum(do_ref[:, blk].astype(F32) * o_ref[:, blk].astype(F32), axis=1, keepdims=True)

    wide = pl.BlockSpec((ts, D_ATT), lambda i: (i, 0))
    return pl.pallas_call(
        body, name="attn_delta", grid=(s // ts,), in_specs=[wide, wide],
        out_specs=pl.BlockSpec((N_HEADS, ts, 1), lambda i: (0, i, 0)),
        out_shape=jax.ShapeDtypeStruct((N_HEADS, s, 1), F32),
        compiler_params=_params(("parallel",)),
    )(do, o)


def _flash_bwd(q, qt, k, v, do, dot, lse, delta, after, *, tq=1024, tk=512, per_trip=8):
    s = q.shape[0]
    tq, tk = min(tq, s), min(tk, s)
    nq = s // tq
    per_trip = min(per_trip, nq)
    assert nq % per_trip == 0

    def body(q_ref, qt_ref, do_ref, dot_ref, lse_ref, dl_ref, k_ref, v_ref, after_ref, dq_ref, dk_ref, dv_ref):
        j = pl.program_id(1)

        @pl.when(j == 0)
        def _():
            dq_ref[...] = jnp.zeros_like(dq_ref)

        kv, vv = k_ref[...], v_ref[...]

        def chunk(i, dk_t, dv_t):
            at = pl.multiple_of(i * tq, tq)
            rows = pl.ds(at, tq)
            sc = lax.dot_general(q_ref[rows, :], kv, _NT, preferred_element_type=F32)
            p = jnp.exp(sc - lse_ref[rows, :])
            dp = lax.dot_general(do_ref[rows, :], vv, _NT, preferred_element_type=F32)
            ds = (p * (dp - dl_ref[rows, :])).astype(BF16)
            dv_t = dv_t + jnp.dot(dot_ref[:, rows], p.astype(BF16), preferred_element_type=F32)
            dk_t = dk_t + jnp.dot(qt_ref[:, rows], ds, preferred_element_type=F32)
            dq_ref[rows, :] += jnp.dot(ds, kv, preferred_element_type=F32)
            return dk_t, dv_t

        def step(i, carry):
            for c in range(per_trip):
                carry = chunk(per_trip * i + c, *carry)
            return carry

        zero = jnp.zeros((HEAD_PAD, tk), F32)
        dk_t, dv_t = lax.fori_loop(0, nq // per_trip, step, (zero, zero))
        dk_ref[...] = dk_t.T
        dv_ref[...] = dv_t.T

        @pl.when(j == pl.num_programs(1) - 1)
        def _():
            dq_ref[...] *= SM_SCALE

    head = pl.BlockSpec((s, HEAD_PAD), lambda h, j: (0, h))
    head_t = pl.BlockSpec((HEAD_PAD, s), lambda h, j: (h, 0))
    stat = pl.BlockSpec((None, s, 1), lambda h, j: (h, 0, 0))
    blk = pl.BlockSpec((tk, HEAD_PAD), lambda h, j: (j, h))
    return pl.pallas_call(
        body, name="flash_bwd", grid=(N_HEADS, s // tk),
        in_specs=[head, head_t, head, head_t, stat, stat, blk, blk, ANY],
        out_specs=[head, blk, blk],
        out_shape=[jax.ShapeDtypeStruct((s, D_ATT), F32)] * 3,
        compiler_params=_params(("parallel", "arbitrary")),
    )(q, qt, do, dot, lse, delta, k, v, after)


FFN_TC = 256
FFN_TG = 1408


FFN_HALO_BF16 = 16
FFN_HALO_F32 = 8


def _row_halo_specs(ts, s, halo, width):
    nb = ts // halo
    last = s // halo - 1
    prev = pl.BlockSpec((halo, width), lambda i, j: (jnp.maximum(i * nb - 1, 0), 0))
    nxt = pl.BlockSpec((halo, width), lambda i, j: (jnp.minimum((i + 1) * nb, last), 0))
    return prev, nxt


def _ext_rows(prev, main, nxt, first, last):
    return jnp.concatenate([jnp.where(first, jnp.zeros_like(prev), prev), main,
                            jnp.where(last, jnp.zeros_like(nxt), nxt)], axis=0)


def _ext_conv(a, w):
    a_dn = pltpu.roll(a, 1, 0)
    a_up = pltpu.roll(a, a.shape[0] - 1, 0)
    return a_dn * w[0:1, :] + a * w[1:2, :] + a_up * w[2:3, :], a_dn, a_up


def _ffn_pieces(tg):
    return [(off, min(FFN_TC, tg - off)) for off in range(0, tg, FFN_TC)]


def _ffn_fwd(hf, w_up, w, b, *, ts=512, tg=FFN_TG):
    s = hf.shape[0]
    n, ng, halo = s // ts, D_FF // tg, FFN_HALO_BF16

    def body(h_ref, hp_ref, hn_ref, wg_ref, wu_ref, cw_ref, cb_ref, a_ref, act_ref):
        i = pl.program_id(0)
        ext = _ext_rows(hp_ref[...], h_ref[...], hn_ref[...], i == 0, i == n - 1)
        for off, width in _ffn_pieces(tg):
            cols = slice(off, off + width)
            gate_up = []
            for half, w_ref in enumerate((wg_ref, wu_ref)):
                a_ext = jnp.dot(ext, w_ref[:, cols], preferred_element_type=F32)
                a_ref[half, :, cols] = a_ext[halo:halo + ts]
                conv = _ext_conv(a_ext, cw_ref[half, :, cols])[0]
                gate_up.append(conv[halo:halo + ts] + cb_ref[half, :, cols])
            g, u = gate_up
            act_ref[:, cols] = (g * _sigmoid(g) * u).astype(BF16)

    prev, nxt = _row_halo_specs(ts, s, halo, D_MODEL)
    return pl.pallas_call(
        body, name="ffn_fwd", grid=(n, ng),
        in_specs=[pl.BlockSpec((ts, D_MODEL), lambda i, j: (i, 0)), prev, nxt,
                  pl.BlockSpec((D_MODEL, tg), lambda i, j: (0, j)), pl.BlockSpec((D_MODEL, tg), lambda i, j: (0, j + ng)),
                  pl.BlockSpec((2, 8, tg), lambda i, j: (0, 0, j)), pl.BlockSpec((2, 1, tg), lambda i, j: (0, 0, j))],
        out_specs=[pl.BlockSpec((2, ts, tg), lambda i, j: (0, i, j)), pl.BlockSpec((ts, tg), lambda i, j: (i, j))],
        out_shape=[jax.ShapeDtypeStruct((2, s, D_FF), F32), jax.ShapeDtypeStruct((s, D_FF), BF16)],
        compiler_params=_params(("parallel", "parallel")),
    )(hf, hf, hf, w_up, w_up, w, b)


def _ffn_bwd(dx2, w_down, a_pre, w, b, *, ts=512, tg=FFN_TG):
    s = dx2.shape[0]
    n, ng, halo = s // ts, D_FF // tg, FFN_HALO_F32
    main = slice(halo, halo + ts)

    def body(dx_ref, dxp_ref, dxn_ref, wd_ref, a_ref, ap_ref, an_ref, cw_ref, cb_ref, o_ref, dw_ref, db_ref):
        i, j = pl.program_id(0), pl.program_id(1)
        first, last = i == 0, i == n - 1

        @pl.when(first & (j == 0))
        def _():
            dw_ref[...] = jnp.zeros_like(dw_ref)
            db_ref[...] = jnp.zeros_like(db_ref)

        dx_ext = _ext_rows(dxp_ref[...], dx_ref[...], dxn_ref[...], first, last).astype(BF16)
        for off, width in _ffn_pieces(tg):
            cols = slice(off, off + width)
            dact = lax.dot_general(dx_ext, wd_ref[cols, :], _NT, preferred_element_type=F32)
            halves = []
            for half in range(2):
                a_ext = _ext_rows(ap_ref[half, :, cols], a_ref[half, :, cols], an_ref[half, :, cols], first, last)
                conv, a_dn, a_up = _ext_conv(a_ext, cw_ref[half, :, cols])
                halves.append((conv + cb_ref[half, :, cols], a_dn, a_ext, a_up))
            g, u = halves[0][0], halves[1][0]
            sg = _sigmoid(g)
            grads = (dact * u * (sg * (1.0 + g * (1.0 - sg))), dact * (g * sg))
            for half in range(2):
                d = grads[half]
                _, a_dn, a_ext, a_up = halves[half]
                wv = cw_ref[half, :, cols]
                d_pre = (pltpu.roll(d, d.shape[0] - 1, 0) * wv[0:1, :] + d * wv[1:2, :]
                         + pltpu.roll(d, 1, 0) * wv[2:3, :])
                o_ref[half, :, cols] = d_pre[main].astype(BF16)
                dm = d[main]
                dw_ref[j, half, 0:1, cols] += jnp.sum(dm * a_dn[main], axis=0, keepdims=True)
                dw_ref[j, half, 1:2, cols] += jnp.sum(dm * a_ext[main], axis=0, keepdims=True)
                dw_ref[j, half, 2:3, cols] += jnp.sum(dm * a_up[main], axis=0, keepdims=True)
                db_ref[j, half, :, cols] += jnp.sum(dm, axis=0, keepdims=True)

    dxp, dxn = _row_halo_specs(ts, s, halo, D_MODEL)
    nb, lastb = ts // halo, s // halo - 1
    a_main = pl.BlockSpec((2, ts, tg), lambda i, j: (0, i, j))
    a_prev = pl.BlockSpec((2, halo, tg), lambda i, j: (0, jnp.maximum(i * nb - 1, 0), j))
    a_next = pl.BlockSpec((2, halo, tg), lambda i, j: (0, jnp.minimum((i + 1) * nb, lastb), j))
    da_pre, dw, db = pl.pallas_call(
        body, name="ffn_bwd", grid=(n, ng),
        in_specs=[pl.BlockSpec((ts, D_MODEL), lambda i, j: (i, 0)), dxp, dxn,
                  pl.BlockSpec((tg, D_MODEL), lambda i, j: (j, 0)), a_main, a_prev, a_next,
                  pl.BlockSpec((2, 8, tg), lambda i, j: (0, 0, j)), pl.BlockSpec((2, 1, tg), lambda i, j: (0, 0, j))],
        out_specs=[a_main, pl.BlockSpec((ng, 2, 8, tg), lambda i, j: (0, 0, 0, 0)),
                   pl.BlockSpec((ng, 2, 1, tg), lambda i, j: (0, 0, 0, 0))],
        out_shape=[jax.ShapeDtypeStruct((2, s, D_FF), BF16), jax.ShapeDtypeStruct((ng, 2, 8, tg), F32),
                   jax.ShapeDtypeStruct((ng, 2, 1, tg), F32)],
        compiler_params=_params(("arbitrary", "arbitrary")),
    )(dx2, dx2, dx2, w_down, a_pre, a_pre, a_pre, w, b)
    return (da_pre, dw.transpose(1, 2, 0, 3).reshape(2, 8, D_FF), db.transpose(1, 2, 0, 3).reshape(2, 1, D_FF))


def _ple_final(x2, n3, p, target, gf, w_pg, w_pp, *, ts=256):
    s, d = x2.shape
    dp = p.shape[1]

    def body(x2_ref, n3_ref, p_ref, t_ref, gf_ref, wg_ref, wp_ref, loss_ref, dx3_ref, dgl_ref, dpp_ref, dgf_ref):
        @pl.when(pl.program_id(0) == 0)
        def _():
            loss_ref[...] = jnp.zeros_like(loss_ref)
            dgf_ref[...] = jnp.zeros_like(dgf_ref)

        gate = _sigmoid(jnp.dot(n3_ref[...], wg_ref[...], preferred_element_type=F32))
        ppv = jnp.dot(p_ref[...].astype(BF16), wp_ref[...], preferred_element_type=F32)
        x3 = x2_ref[...] + gate * ppv
        gfv = gf_ref[...]
        err = x3 * _rms_scale(x3) * gfv - t_ref[...]
        loss_ref[...] += 0.5 * jnp.sum(jnp.mean(err * err, axis=-1, keepdims=True), axis=0, keepdims=True)
        dx3, dgf_rows = _rms_bwd_rows(x3, gfv, err * (1.0 / d))
        dgf_ref[...] += jnp.sum(dgf_rows, axis=0, keepdims=True)
        dx3_ref[...] = dx3
        dgl_ref[...] = (dx3 * ppv * gate * (1.0 - gate)).astype(BF16)
        dpp_ref[...] = (dx3 * gate).astype(BF16)

    row = pl.BlockSpec((ts, d), lambda i: (i, 0))
    vec = pl.BlockSpec((1, d), lambda i: (0, 0))
    return pl.pallas_call(
        body, name="ple_final", grid=(s // ts,),
        in_specs=[row, row, pl.BlockSpec((ts, dp), lambda i: (i, 0)), row, vec,
                  pl.BlockSpec((d, d), lambda i: (0, 0)), pl.BlockSpec((dp, d), lambda i: (0, 0))],
        out_specs=[pl.BlockSpec((1, 128), lambda i: (0, 0)), row, row, row, vec],
        out_shape=[jax.ShapeDtypeStruct((1, 128), F32), jax.ShapeDtypeStruct((s, d), F32),
                   jax.ShapeDtypeStruct((s, d), BF16), jax.ShapeDtypeStruct((s, d), BF16),
                   jax.ShapeDtypeStruct((1, d), F32)],
        compiler_params=_params(("arbitrary",)),
    )(x2, n3, p, target, gf, w_pg, w_pp)


def _row_tile(rows, cols, n_arrays, budget=12 << 20):
    best = None
    for t in range(8, rows + 1, 8):
        if rows % t == 0 and t * cols * 4 * n_arrays <= budget:
            best = t
    return rows if best is None else best


def _sum_slots(a, *, name):
    g, r, c = a.shape
    tr = _row_tile(r, c, g + 1)

    def body(*refs):
        tot = refs[0][...]
        for ref in refs[1:g]:
            tot = tot + ref[...]
        refs[g][...] = tot

    specs = [pl.BlockSpec((None, tr, c), functools.partial(lambda i, slot: (slot, i, 0), slot=k)) for k in range(g)]
    return pl.pallas_call(
        body, name=name, grid=(r // tr,), in_specs=specs, out_specs=pl.BlockSpec((tr, c), lambda i: (i, 0)),
        out_shape=jax.ShapeDtypeStruct((r, c), a.dtype), compiler_params=_params(("parallel",)),
    )(*([a] * g))


def _adamw_refs(w_ref, g_ref, m_ref, v_ref, d_ref, mo_ref, vo_ref):
    gv = g_ref[...]
    mn = ADAM_B1 * m_ref[...] + (1.0 - ADAM_B1) * gv
    vn = ADAM_B2 * v_ref[...] + (1.0 - ADAM_B2) * (gv * gv)
    m_hat = mn / (1.0 - ADAM_B1 ** ADAM_STEP)
    v_hat = vn / (1.0 - ADAM_B2 ** ADAM_STEP)
    d_ref[...] = -ADAM_LR * (m_hat / (jnp.sqrt(v_hat) + ADAM_EPS) + ADAM_WD * w_ref[...])
    mo_ref[...] = mn
    vo_ref[...] = vn


def _adamw_many(ws, gs, ms, vs, *, name):
    n = len(ws)

    def body(*refs):
        ins, outs = refs[:4 * n], refs[4 * n:]
        for a in range(n):
            _adamw_refs(ins[a], ins[n + a], ins[2 * n + a], ins[3 * n + a], outs[3 * a], outs[3 * a + 1],
                        outs[3 * a + 2])

    vm = pl.BlockSpec(memory_space=pltpu.VMEM)
    res = pl.pallas_call(
        body, name=name, in_specs=[vm] * (4 * n), out_specs=[vm] * (3 * n),
        out_shape=[jax.ShapeDtypeStruct(a.shape, F32) for a in ws for _ in range(3)],
    )(*ws, *gs, *ms, *vs)
    return [tuple(res[3 * a:3 * a + 3]) for a in range(n)]


def _adamw(w, g, m, v, *, name):
    r, c = w.shape
    tr = _row_tile(r, c, 7)
    body = _adamw_refs

    blk = pl.BlockSpec((tr, c), lambda i: (i, 0))
    return pl.pallas_call(
        body, name=name, grid=(r // tr,), in_specs=[blk] * 4, out_specs=[blk] * 3,
        out_shape=[jax.ShapeDtypeStruct((r, c), F32)] * 3, compiler_params=_params(("parallel",)),
    )(w, g, m, v)


def _position():
    x, y, c = lax.axis_index("x"), lax.axis_index("y"), lax.axis_index("c")
    return x, y, c


def _other_chips(x, y):
    return [(1 - x, y), (x, 1 - y), (1 - x, 1 - y)]


def _stage_in(srcs, stage, sems):
    cps = [pltpu.make_async_copy(src, stage[a], sems.at[a]) for a, src in enumerate(srcs)]
    for cp in cps:
        cp.start()
    return cps


def _stage_out(staged, stage, dsts, sems):
    cps = []
    for a, dst in enumerate(dsts):
        staged[a].wait()
        cp = pltpu.make_async_copy(stage[a], dst, sems.at[a])
        cp.start()
        cps.append(cp)
    return cps


def _send_other_halves(grads, *, tag):
    n = len(grads)

    def body(*refs):
        ins, sib = refs[:n], refs[n:2 * n]
        send_sems, recv_sems = refs[2 * n:]
        x, y, c = _position()
        remote = []
        for a in range(n):
            half = ins[a].shape[1] // 2
            give = ins[a].at[:, pl.ds(pl.multiple_of((1 - c) * half, 8), half), :]
            rc = pltpu.make_async_remote_copy(
                src_ref=give, dst_ref=sib[a], send_sem=send_sems.at[a], recv_sem=recv_sems.at[a],
                device_id=(x, y, 1 - c), device_id_type=MESH)
            rc.start()
            remote.append(rc)
        for rc in remote:
            rc.wait_recv()
        for rc in remote:
            rc.wait_send()

    return pl.pallas_call(
        body, name="send_other_halves_" + tag, in_specs=[ANY] * n, out_specs=[ANY] * n,
        out_shape=[jax.ShapeDtypeStruct((g.shape[0], g.shape[1] // 2, g.shape[2]), g.dtype) for g in grads],
        scratch_shapes=[pltpu.SemaphoreType.DMA((n,)), pltpu.SemaphoreType.DMA((n,))],
        compiler_params=pltpu.CompilerParams(has_side_effects=True),
    )(*grads)


def _add_own_half(g4, sib, core, *, name):
    g, a2, c = sib.shape
    tr = _row_tile(a2, c, 4)

    def body(core_ref, a_ref, b_ref, o_ref, o16_ref):
        tot = a_ref[...] + b_ref[...]
        o_ref[...] = tot
        o16_ref[...] = tot.astype(BF16)

    blk = pl.BlockSpec((None, tr, c), lambda i, j, core_ref: (i, j, 0))
    return pl.pallas_call(
        body, name=name,
        grid_spec=pltpu.PrefetchScalarGridSpec(
            num_scalar_prefetch=1, grid=(g, a2 // tr),
            in_specs=[pl.BlockSpec((None, None, tr, c), lambda i, j, core_ref: (i, core_ref[0], j, 0)), blk],
            out_specs=[blk, blk]),
        out_shape=[jax.ShapeDtypeStruct(sib.shape, F32), jax.ShapeDtypeStruct(sib.shape, BF16)],
        compiler_params=_params(("parallel", "parallel")),
    )(core, g4.reshape(g, 2, a2, c), sib)


def _sum_chips(landed, own, chip, *, name):
    g, r, c = landed.shape
    tr = _row_tile(r, c, 5)

    def body(chip_ref, *refs):
        me = chip_ref[0]
        own_v = refs[g][...]
        tot = None
        for slot in range(g):
            term = jnp.where(me == slot, own_v, refs[slot][...].astype(F32))
            tot = term if tot is None else tot + term
        refs[g + 1][...] = tot

    def landed_spec(slot):
        return pl.BlockSpec((None, tr, c),
                            lambda i, chip_ref: (jnp.where(chip_ref[0] == slot, (slot + 1) % g, slot), i, 0))

    return pl.pallas_call(
        body, name=name,
        grid_spec=pltpu.PrefetchScalarGridSpec(
            num_scalar_prefetch=1, grid=(r // tr,),
            in_specs=[landed_spec(k) for k in range(g)]
            + [pl.BlockSpec((None, tr, c), lambda i, chip_ref: (chip_ref[0], i, 0))],
            out_specs=pl.BlockSpec((tr, c), lambda i, chip_ref: (i, 0))),
        out_shape=jax.ShapeDtypeStruct((r, c), F32), compiler_params=_params(("parallel",)),
    )(chip, *([landed] * g), own)


def _join_halves(halves):
    n = len(halves)

    def body(*refs):
        ins, outs, stage = refs[:n], refs[n:2 * n], refs[2 * n:3 * n]
        send_sems, recv_sems, in_sems, out_sems = refs[3 * n:]
        x, y, c = _position()
        remote = []
        staged = _stage_in(ins, stage, in_sems)
        for a in range(n):
            rc = pltpu.make_async_remote_copy(
                src_ref=ins[a], dst_ref=outs[a].at[c], send_sem=send_sems.at[a], recv_sem=recv_sems.at[a],
                device_id=(x, y, 1 - c), device_id_type=MESH)
            rc.start()
            remote.append(rc)
        local = _stage_out(staged, stage, [o.at[c] for o in outs], out_sems)
        for a in range(n):
            pltpu.make_async_remote_copy(
                src_ref=ins[a], dst_ref=outs[a].at[1 - c], send_sem=send_sems.at[a], recv_sem=recv_sems.at[a],
                device_id=(x, y, 1 - c), device_id_type=MESH).wait_recv()
        for rc in remote:
            rc.wait_send()
        for cp in local:
            cp.wait()

    return pl.pallas_call(
        body, name="join_halves", in_specs=[ANY] * n, out_specs=[ANY] * n,
        out_shape=[jax.ShapeDtypeStruct((2,) + h.shape, h.dtype) for h in halves],
        scratch_shapes=[pltpu.VMEM(h.shape, h.dtype) for h in halves]
        + [pltpu.SemaphoreType.DMA((n,)), pltpu.SemaphoreType.DMA((n,)), pltpu.SemaphoreType.DMA((n,)),
           pltpu.SemaphoreType.DMA((n,))],
        compiler_params=pltpu.CompilerParams(has_side_effects=True),
    )(*halves)


_HBM = pl.BlockSpec(memory_space=pltpu.HBM)
_SEM = pl.BlockSpec(memory_space=pltpu.SEMAPHORE)


def _chip_copies(srcs, lands, send_sems, recv_sems, scatter):
    x, y, c = _position()
    me = 2 * x + y
    outgoing, incoming = [], []
    for a, (src, land) in enumerate(zip(srcs, lands)):
        for k, (px, py) in enumerate(_other_chips(x, y)):
            peer = 2 * px + py
            sems = dict(send_sem=send_sems.at[3 * a + k], recv_sem=recv_sems.at[3 * a + k], device_id=(px, py, c),
                        device_id_type=MESH)
            outgoing.append(pltpu.make_async_remote_copy(
                src_ref=src.at[peer] if scatter else src, dst_ref=land.at[me], **sems))
            incoming.append(pltpu.make_async_remote_copy(
                src_ref=src.at[me] if scatter else src, dst_ref=land.at[peer], **sems))
    return outgoing, incoming


def _chips_start(srcs, *, scatter, name):
    n = len(srcs)
    lands = [lax.empty(a.shape if scatter else (N_CHIPS,) + a.shape, a.dtype) for a in srcs]

    def body(*refs):
        ins, send_sems, recv_sems, token = refs[:2 * n], refs[2 * n], refs[2 * n + 1], refs[-1]
        outgoing, _ = _chip_copies(ins[:n], ins[n:], send_sems, recv_sems, scatter)
        for cp in outgoing:
            cp.start()
        token[...] = jnp.zeros_like(token)

    bufs = list(srcs) + lands
    res = pl.pallas_call(
        body, name=name, in_specs=[_HBM] * (2 * n),
        out_specs=(_SEM, _SEM, *[_HBM] * (2 * n), pl.BlockSpec(memory_space=pltpu.VMEM)),
        out_shape=(pltpu.SemaphoreType.DMA((3 * n,)), pltpu.SemaphoreType.DMA((3 * n,)),
                   *[pltpu.HBM(a.shape, a.dtype) for a in bufs], jax.ShapeDtypeStruct((8, 128), F32)),
        input_output_aliases={i: 2 + i for i in range(2 * n)},
        compiler_params=pltpu.CompilerParams(has_side_effects=pltpu.SideEffectType.DATAFLOW_SIDE_EFFECTING),
    )(*[pltpu.with_memory_space_constraint(a, pltpu.HBM) for a in bufs])
    return res[0], res[1], list(res[2:2 + n]), list(res[2 + n:2 + 2 * n]), res[-1]


def _chips_wait(handle, after, *, scatter, name):
    send_sems, recv_sems, srcs, lands, _ = handle
    n = len(srcs)

    def body(*refs):
        ins, send_ref, recv_ref = refs[:2 * n], refs[2 * n], refs[2 * n + 1]
        outgoing, incoming = _chip_copies(ins[:n], ins[n:], send_ref, recv_ref, scatter)
        for cp in outgoing:
            cp.wait_send()
        for cp in incoming:
            cp.wait_recv()

    bufs = list(srcs) + list(lands)
    res = pl.pallas_call(
        body, name=name, in_specs=[_HBM] * (2 * n) + [_SEM, _SEM, ANY], out_specs=tuple([_HBM] * (2 * n)),
        out_shape=tuple(pltpu.HBM(a.shape, a.dtype) for a in bufs),
        input_output_aliases={i: i for i in range(2 * n)},
        compiler_params=pltpu.CompilerParams(has_side_effects=pltpu.SideEffectType.DATAFLOW_SIDE_EFFECTING),
    )(*bufs, send_sems, recv_sems, after)
    return list(res[:n]), list(res[n:])


def _gather_all(buf):
    def body(in_ref, out_ref, send_sems, recv_sems, local_sem):
        x, y, c = _position()
        me = 4 * x + 2 * y + c
        peers = [(x, y, 1 - c)] + [(px, py, pc) for (px, py) in _other_chips(x, y) for pc in (c, 1 - c)]
        cp = pltpu.make_async_copy(in_ref, out_ref.at[me], local_sem)
        cp.start()
        remote = []
        for k, peer in enumerate(peers):
            rc = pltpu.make_async_remote_copy(
                src_ref=in_ref, dst_ref=out_ref.at[me], send_sem=send_sems.at[k], recv_sem=recv_sems.at[k],
                device_id=peer, device_id_type=MESH)
            rc.start()
            remote.append(rc)
        for k, (px, py, pc) in enumerate(peers):
            pltpu.make_async_remote_copy(
                src_ref=in_ref, dst_ref=out_ref.at[4 * px + 2 * py + pc], send_sem=send_sems.at[k],
                recv_sem=recv_sems.at[k], device_id=(px, py, pc), device_id_type=MESH).wait_recv()
        for rc in remote:
            rc.wait_send()
        cp.wait()

    return pl.pallas_call(
        body, name="gather_all", in_specs=[ANY], out_specs=ANY,
        out_shape=jax.ShapeDtypeStruct((N_DEV,) + buf.shape, buf.dtype),
        scratch_shapes=[pltpu.SemaphoreType.DMA((N_DEV - 1,)), pltpu.SemaphoreType.DMA((N_DEV - 1,)),
                        pltpu.SemaphoreType.DMA],
        compiler_params=pltpu.CompilerParams(has_side_effects=True),
    )(buf)


def _cols_from_shards(g4):
    _, k, n = g4.shape
    return g4.transpose(1, 0, 2).reshape(k, N_CHIPS * n)


def _cols_to_shards(w):
    k, n = w.shape
    return w.reshape(k, N_CHIPS, n // N_CHIPS).transpose(1, 0, 2)


def _pad_heads(w, width):
    k = w.shape[0]
    w3 = w.reshape(k, N_HEADS, width)
    return jnp.pad(w3, ((0, 0), (0, 0), (0, HEAD_PAD - width))).reshape(k, D_ATT)


def _unpad_heads(w, width):
    k = w.shape[0]
    return w.reshape(k, N_HEADS, HEAD_PAD)[:, :, :width]


def _rope_tables(s, after):
    pos = jnp.arange(s, dtype=F32) + after
    inv_freq = ROPE_THETA ** (-jnp.arange(0, QK_ROPE, 2, dtype=F32) / QK_ROPE)
    ang = pos[:, None] * inv_freq[None, :]
    cos_h, sin_h = jnp.cos(ang), jnp.sin(ang)
    half = QK_ROPE // 2
    z = jnp.zeros((s, half), F32)
    ones = jnp.ones((s, QK_NOPE), F32)
    tail = jnp.zeros((s, HEAD_PAD - QK_NOPE - QK_ROPE), F32)
    cos = jnp.concatenate([ones, cos_h, cos_h, tail + 1.0], axis=1)
    sin_a = jnp.concatenate([ones * 0.0, -sin_h, z, tail], axis=1)
    sin_b = jnp.concatenate([ones * 0.0, z, sin_h, tail], axis=1)
    return cos, sin_a, sin_b


def _local_step(x, p, target, wts, late_weights, reduce_early, reduce_last):
    s = x.shape[0]
    cos, sin_a, sin_b = wts["rope"]
    g1, gq, gkv, g2, g3, gf = (wts[k] for k in ("norm_mix_g", "q_norm_g", "kv_norm_g", "norm_ffn_g", "ple_norm_g",
                                                 "final_norm_g"))
    w_in_p, w_uq_p, w_kv_p = wts["w_in_p"], wts["w_uq_p"], wts["w_kv_p"]
    conv_w8, fconv_w, fconv_b = wts["conv_w8"], wts["ffn_conv_w"], wts["ffn_conv_b"]

    (h, z), _ = _mm_fused(x, w_in_p, name="mm_in", prologue=_pro_rms, vecs=[g1], epilogue=_epi_plain, row_outs=[F32])
    y_conv, qn, kvn, kr = _mix_pre(z, conv_w8, gq, gkv, cos, sin_a, sin_b)
    q, k, v, q_t = _qkv_proj(qn, kvn, kr, w_uq_p, w_kv_p, cos, sin_a, sin_b)
    o, lse = _flash_fwd(q, k, v)
    late = late_weights(lse)
    w_o_a, w_o_b, w_up, w_down = late["w_o_a"], late["w_o_b"], late["w_up"], late["w_down"]
    w_pg, w_pp = late["w_ple_gate"], late["w_ple_proj"]
    (x1, hf), _ = _mm_fused(o, w_o_b, second=(y_conv, w_o_a), name="mm_o", rows=[x], vecs=[g2],
                            epilogue=_epi_add_rms, row_outs=[F32, BF16])
    a_pre, act = _ffn_fwd(hf, w_up, fconv_w, fconv_b)
    (x2, n3), _ = _mm_fused(act, w_down, name="mm_down", rows=[x1], vecs=[g3], epilogue=_epi_add_rms,
                            row_outs=[F32, BF16])
    loss, dx3, dgl, dpp, d_gf = _ple_final(x2, n3, p, target, gf, w_pg, w_pp)

    grads, early = {"final_norm_g": d_gf}, {}
    early["w_ple_proj"] = _mm(p, dpp, ta=True, name="mm_d_wpp", tm=256, tn=1024, tk=2048)
    early["w_ple_gate"] = _mm(n3, dgl, ta=True, name="mm_d_wpg", tm=1024, tn=1024, tk=2048)
    (dx2,), (grads["ple_norm_g"],) = _mm_fused(dgl, w_pg, tb=True, name="mm_d_n3", rows=[x2, dx3], vecs=[g3],
                                               epilogue=_epi_rms_bwd, row_outs=[F32], n_vec_out=1)
    early["w_down"] = _mm(act, dx2, ta=True, name="mm_d_wdown", tm=1408, tn=1024, tk=2048)
    da_pre, grads["ffn_conv_w"], grads["ffn_conv_b"] = _ffn_bwd(dx2, w_down, a_pre, fconv_w, fconv_b)
    early["w_up"] = _mm(hf, da_pre, ta=True, b_split=True, name="mm_d_wup", tm=1024, tn=1408, tk=2048,
                       o_shards=True)
    (dx1,), (grads["norm_ffn_g"],) = _mm_fused(da_pre, w_up, tb=True, a_split=True, name="mm_d_hf", rows=[x1, dx2],
                                               vecs=[g2], epilogue=_epi_rms_bwd, row_outs=[F32], n_vec_out=1)
    d_wo_a = _mm(y_conv, dx1, ta=True, name="mm_d_wo_conv", tm=512, tn=1024, tk=2048)
    d_wo_b = _mm(o, dx1, ta=True, name="mm_d_wo_att", tm=1024, tn=1024, tk=2048)
    early["w_o"] = jnp.concatenate([d_wo_a, d_wo_b.reshape(N_HEADS, HEAD_PAD, D_MODEL)[:, :V_HEAD]
                                    .reshape(N_HEADS * V_HEAD, D_MODEL)], axis=0)
    token, finish = reduce_early(early)
    dyc = _mm(dx1, w_o_a, tb=True, name="mm_d_yconv", tm=512, tn=512, tk=1024)
    (do, do_t), _ = _mm_fused(dx1, w_o_b, tb=True, name="mm_d_o", epilogue=_epi_plain, row_outs=[BF16],
                              transposed_out=BF16)
    delta = _attn_delta(do, o)
    dq, dk, dv = _flash_bwd(q, q_t, k, v, do, do_t, lse, delta, token)
    reduced_early = finish(dq)
    dq_pre, dkr = _qk_bwd(dq, dk, cos, sin_a, sin_b)
    grads["w_uq_p"] = _mm(qn, dq_pre, ta=True, name="mm_d_wuq", tm=256, tn=1024, tk=2048)
    dqn = _mm(dq_pre, w_uq_p, tb=True, name="mm_d_qn", tm=512, tn=256, tk=1024)
    grads["w_k_p"] = _mm(kvn, dk, ta=True, name="mm_d_wk", tm=128, tn=1024, tk=2048)
    grads["w_v_p"] = _mm(kvn, dv, ta=True, name="mm_d_wv", tm=128, tn=1024, tk=2048)
    dkvn_k = _mm(dk, w_kv_p[:, :D_ATT], tb=True, name="mm_d_kvn_k", tm=512, tn=128, tk=1024)
    dkvn = _mm(dv, w_kv_p[:, D_ATT:], tb=True, add=dkvn_k, name="mm_d_kvn_v", tm=512, tn=128, tk=1024)
    dz, grads["conv_w"], grads["q_norm_g"], grads["kv_norm_g"] = _mix_bwd(
        z, dyc, dqn, dkvn, dkr, conv_w8, gq, gkv, cos, sin_a, sin_b)
    grads["w_in_p"] = _mm(h, dz, ta=True, name="mm_d_win", tm=1024, tn=1024, tk=2048)
    token, finish = reduce_last({n: grads.pop(n) for n in ("w_in_p", "w_uq_p", "w_k_p", "w_v_p")})
    (grad_x,), (grads["norm_mix_g"],) = _mm_fused(dz, w_in_p, tb=True, name="mm_d_h", rows=[x, dx1],
                                                  vecs=[g1 + token[0, 0]], epilogue=_epi_rms_bwd, row_outs=[F32],
                                                  n_vec_out=1)
    return loss[0, 0], grad_x, grads, reduced_early, finish(grad_x)


_EARLY_W = ("w_in", "w_uq", "w_ukv")
_LATE_W = ("w_o", "w_up", "w_down", "w_ple_gate", "w_ple_proj")
_BIG = _EARLY_W + _LATE_W
_COL_SHARDED = ("w_in", "w_uq", "w_ukv", "w_up", "w_ple_proj")
_SMALL = ("norm_mix_g", "conv_w", "q_norm_g", "kv_norm_g", "norm_ffn_g", "ffn_conv_w", "ffn_conv_b", "ple_norm_g",
          "final_norm_g")


def _full_from_slots(n, g4):
    return _cols_from_shards(g4) if n in _COL_SHARDED else g4.reshape(-1, g4.shape[2])


def _shard_major(n, g):
    if g.ndim == 3:
        return g
    return _cols_to_shards(g) if n in _COL_SHARDED else g.reshape(N_CHIPS, g.shape[0] // N_CHIPS, g.shape[1])


def _early_shards(w):
    shards = [w[n][0].astype(BF16) for n in _EARLY_W]
    shards.append(jnp.pad(w["conv_w"][0], ((0, 5), (0, 0))))
    shards.append(jnp.pad(w["ffn_conv_w"][0], ((0, 5), (0, 0))))
    return shards


def _fill_own_slot(landed, own, chip):
    return [lax.dynamic_update_slice(g4, a[None], (chip[0], 0, 0)) for g4, a in zip(landed, own)]


def _early_weights(got, w):
    full = {n: _full_from_slots(n, g4) for n, g4 in zip(_EARLY_W, got)}
    full["conv_w8"] = _cols_from_shards(got[len(_EARLY_W)])
    full["ffn_conv_w8"] = _cols_from_shards(got[len(_EARLY_W) + 1])
    return _layout_early(full, w)


def _layout_early(full, w):
    out = {n: w[n] for n in ("norm_mix_g", "q_norm_g", "kv_norm_g", "norm_ffn_g", "ple_norm_g")}
    out["final_norm_g"] = w["final_norm_g"][None, :]
    w_in = full["w_in"]
    zc = jnp.zeros((D_MODEL, QK_NOPE), BF16)
    zt = jnp.zeros((D_MODEL, HEAD_PAD - QK_NOPE - QK_ROPE), BF16)
    out["w_in_p"] = jnp.concatenate([w_in[:, :D_IN - QK_ROPE], zc, w_in[:, D_IN - QK_ROPE:], zt], axis=1)
    out["w_uq_p"] = _pad_heads(full["w_uq"], QK_NOPE + QK_ROPE)
    kv3 = full["w_ukv"].reshape(KV_LORA, N_HEADS, QK_NOPE + V_HEAD)
    out["w_kv_p"] = jnp.concatenate([_pad_heads(kv3[:, :, :QK_NOPE].reshape(KV_LORA, -1), QK_NOPE),
                                     _pad_heads(kv3[:, :, QK_NOPE:].reshape(KV_LORA, -1), V_HEAD)], axis=1)
    out["conv_w8"] = full["conv_w8"]
    fw = full["ffn_conv_w8"]
    out["ffn_conv_w"] = jnp.stack([fw[:, :D_FF], fw[:, D_FF:]])
    out["ffn_conv_b"] = w["ffn_conv_b"].reshape(2, 1, D_FF)
    return out


def _layout_late(full):
    w_o = full["w_o"]
    out = {"w_o_a": w_o[:CONV_WIDTH]}
    out["w_o_b"] = jnp.pad(w_o[CONV_WIDTH:].reshape(N_HEADS, V_HEAD, D_MODEL),
                           ((0, 0), (0, HEAD_PAD - V_HEAD), (0, 0))).reshape(D_ATT, D_MODEL)
    for n in ("w_up", "w_down", "w_ple_gate", "w_ple_proj"):
        out[n] = full[n]
    return out


def _true_matrices(g):
    out = {}
    wp = g["w_in_p"]
    out["w_in"] = jnp.concatenate([wp[:, :D_IN - QK_ROPE], wp[:, D_IN_PAD - HEAD_PAD + QK_NOPE:
                                                              D_IN_PAD - HEAD_PAD + QK_NOPE + QK_ROPE]], axis=1)
    out["w_uq"] = _unpad_heads(g["w_uq_p"], QK_NOPE + QK_ROPE).reshape(Q_LORA, -1)
    out["w_ukv"] = jnp.concatenate([_unpad_heads(g["w_k_p"], QK_NOPE), _unpad_heads(g["w_v_p"], V_HEAD)],
                                   axis=2).reshape(KV_LORA, -1)
    return out


def _true_vectors(g):
    out = {}
    out["conv_w"] = g["conv_w"]
    fw = g["ffn_conv_w"]
    out["ffn_conv_w"] = jnp.concatenate([fw[0, :3], fw[1, :3]], axis=1)
    out["ffn_conv_b"] = g["ffn_conv_b"].reshape(1, 2 * D_FF)
    for n in ("norm_mix_g", "q_norm_g", "kv_norm_g", "norm_ffn_g", "ple_norm_g", "final_norm_g"):
        out[n] = g[n]
    return out


def _chip_partials(names, g, core, *, tag):
    g4 = [_shard_major(n, g[n]) for n in names]
    sib = _send_other_halves(g4, tag=tag)
    return [_add_own_half(a, b, core, name="add_cores_" + n) for n, a, b in zip(names, g4, sib)]


_SMALL_SIZES = {"norm_mix_g": D_MODEL, "conv_w": 3 * CONV_WIDTH, "q_norm_g": Q_LORA, "kv_norm_g": KV_LORA,
                "norm_ffn_g": D_MODEL, "ffn_conv_w": 6 * D_FF, "ffn_conv_b": 2 * D_FF, "ple_norm_g": D_MODEL,
                "final_norm_g": D_MODEL}


def _pack(parts, rows):
    flat = jnp.concatenate([a.reshape(-1) for a in parts])
    return jnp.pad(flat, (0, rows * 128 - flat.shape[0])).reshape(rows, 128)


def _unpack(buf, sizes):
    flat = buf.reshape(-1)
    out, at = [], 0
    for n in sizes:
        out.append(flat[at:at + n])
        at += n
    return out


def _reduce_small(g, loss):
    sizes = [1] + [_SMALL_SIZES[n] for n in _SMALL]
    rows = -(-sum(sizes) // 1024) * 8
    slots = _gather_all(_pack([loss] + [g[n] for n in _SMALL], rows))
    parts = _unpack(_sum_slots(slots, name="sum_small"), sizes)
    return parts[0][0], dict(zip(_SMALL, parts[1:]))


def kernel(x, p, norm_mix_g, w_in, conv_w, q_norm_g, w_uq, kv_norm_g, w_ukv, w_o, norm_ffn_g, w_up, ffn_conv_w, ffn_conv_b, w_down, ple_norm_g, w_ple_gate, w_ple_proj, final_norm_g, loss_target, m_norm_mix_g, m_w_in, m_conv_w, m_q_norm_g, m_w_uq, m_kv_norm_g, m_w_ukv, m_w_o, m_norm_ffn_g, m_w_up, m_ffn_conv_w, m_ffn_conv_b, m_w_down, m_ple_norm_g, m_w_ple_gate, m_w_ple_proj, m_final_norm_g, v_norm_mix_g, v_w_in, v_conv_w, v_q_norm_g, v_w_uq, v_kv_norm_g, v_w_ukv, v_w_o, v_norm_ffn_g, v_w_up, v_ffn_conv_w, v_ffn_conv_b, v_w_down, v_ple_norm_g, v_w_ple_gate, v_w_ple_proj, v_final_norm_g):
    names = ["norm_mix_g", "w_in", "conv_w", "q_norm_g", "w_uq", "kv_norm_g", "w_ukv", "w_o", "norm_ffn_g", "w_up",
             "ffn_conv_w", "ffn_conv_b", "w_down", "ple_norm_g", "w_ple_gate", "w_ple_proj", "final_norm_g"]
    w = dict(zip(names, (norm_mix_g, w_in, conv_w, q_norm_g, w_uq, kv_norm_g, w_ukv, w_o, norm_ffn_g, w_up,
                         ffn_conv_w, ffn_conv_b, w_down, ple_norm_g, w_ple_gate, w_ple_proj, final_norm_g)))
    m = dict(zip(names, (m_norm_mix_g, m_w_in, m_conv_w, m_q_norm_g, m_w_uq, m_kv_norm_g, m_w_ukv, m_w_o,
                         m_norm_ffn_g, m_w_up, m_ffn_conv_w, m_ffn_conv_b, m_w_down, m_ple_norm_g, m_w_ple_gate,
                         m_w_ple_proj, m_final_norm_g)))
    v = dict(zip(names, (v_norm_mix_g, v_w_in, v_conv_w, v_q_norm_g, v_w_uq, v_kv_norm_g, v_w_ukv, v_w_o,
                         v_norm_ffn_g, v_w_up, v_ffn_conv_w, v_ffn_conv_b, v_w_down, v_ple_norm_g, v_w_ple_gate,
                         v_w_ple_proj, v_final_norm_g)))

    core = lax.axis_index("c").astype(jnp.int32).reshape(1)
    chip = (2 * lax.axis_index("x") + lax.axis_index("y")).astype(jnp.int32).reshape(1)

    first = _chips_start(_early_shards(w), scatter=False, name="gather_early_start")
    rope = _rope_tables(x.shape[1], first[4][0, 0])
    late_shards = [w[n][0].astype(BF16) for n in _LATE_W]
    ready, *late_shards = lax.optimization_barrier((rope[0], *late_shards))
    own, landed = _chips_wait(first, ready, scatter=False, name="gather_early_wait")
    wts = _early_weights(_fill_own_slot(landed, own, chip), w)
    wts["rope"] = (ready,) + tuple(rope[1:])
    late_shards[0], _ = lax.optimization_barrier((late_shards[0], own[0]))
    gather = _chips_start(late_shards, scatter=False, name="gather_late_start")
    wts["norm_mix_g"] = wts["norm_mix_g"] + gather[4][0, 0]

    def late_weights(after):
        shards, landed = _chips_wait(gather, after, scatter=False, name="gather_late_wait")
        return _layout_late({n: _full_from_slots(n, g4)
                             for n, g4 in zip(_LATE_W, _fill_own_slot(landed, shards, chip))})

    def reduce_early(g):
        parts = _chip_partials(_LATE_W, g, core, tag="early")
        scatter = _chips_start([t16 for _, t16 in parts], scatter=True, name="scatter_early_start")

        def finish(after):
            _, landed = _chips_wait(scatter, after, scatter=True, name="scatter_early_wait")
            return [_sum_chips(a, t32, chip, name="sum_chips_" + n) for n, a, (t32, _) in zip(_LATE_W, landed, parts)]

        return scatter[4], finish

    def reduce_last(g):
        parts = _chip_partials(_EARLY_W, _true_matrices(g), core, tag="late")
        scatter = _chips_start([t16 for _, t16 in parts], scatter=True, name="scatter_late_start")

        def finish(after):
            _, landed = _chips_wait(scatter, after, scatter=True, name="scatter_late_wait")
            return [_sum_chips(a, t32, chip, name="sum_chips_" + n) for n, a, (t32, _) in zip(_EARLY_W, landed, parts)]

        return scatter[4], finish

    loss, grad_x, small_grads, halves_early, halves_last = _local_step(
        x[0], p[0, 0], loss_target[0], wts, late_weights, reduce_early, reduce_last)
    g_full = _true_vectors(small_grads)
    whole = _join_halves(halves_last + halves_early)
    big = {n: a.reshape(-1, a.shape[2]) for n, a in zip(_BIG, whole)}

    g_out, d_out, m_out, v_out = {}, {}, {}, {}
    for n in _BIG:
        shape = w[n].shape
        g = big[n]
        d, mn, vn = _adamw(w[n][0], g, m[n][0], v[n][0], name="adamw_" + n)
        g_out[n], d_out[n], m_out[n], v_out[n] = (a.reshape(shape) for a in (g, d, mn, vn))

    loss, small = _reduce_small(g_full, loss)
    chip = 2 * lax.axis_index("x") + lax.axis_index("y")
    g_small = {}
    for n in _SMALL:
        shape = w[n].shape
        g = small[n]
        if n in ("conv_w", "ffn_conv_w"):
            width = shape[-1]
            g = lax.dynamic_slice(g.reshape(3, N_CHIPS * width), (0, chip * width), (3, width))
        g_small[n] = g.reshape(shape)
    flat = [[src[n].reshape(-1, src[n].shape[-1]) for n in _SMALL] for src in (w, g_small, m, v)]
    for n, (d, mn, vn) in zip(_SMALL, _adamw_many(*flat, name="adamw_small")):
        shape = w[n].shape
        g_out[n], d_out[n], m_out[n], v_out[n] = g_small[n], d.reshape(shape), mn.reshape(shape), vn.reshape(shape)

    return (loss, grad_x[None], *[g_out[n] for n in names], *[d_out[n] for n in names],
            *[m_out[n] for n in names], *[v_out[n] for n in names])
```

```python
import functools

import jax
import jax.numpy as jnp
from jax import lax
from jax.experimental import pallas as pl
from jax.experimental.pallas import tpu as pltpu

F32 = jnp.float32
BF16 = jnp.bfloat16

D_MODEL = 1024
CONV_WIDTH = 512
Q_LORA = 256
KV_LORA = 128
QK_NOPE = 64
QK_ROPE = 32
V_HEAD = 64
N_HEADS = 8
HEAD_PAD = 128
D_ATT = N_HEADS * HEAD_PAD
D_IN = 3 * CONV_WIDTH + Q_LORA + KV_LORA + QK_ROPE
D_IN_PAD = 3 * CONV_WIDTH + Q_LORA + KV_LORA + HEAD_PAD
D_FF = 2816
ROPE_THETA = 10000.0
EPS = 1e-6
SM_SCALE = (QK_NOPE + QK_ROPE) ** -0.5
ONES_LANE = V_HEAD

ADAM_LR = 0.001
ADAM_B1 = 0.9
ADAM_B2 = 0.999
ADAM_EPS = 1e-08
ADAM_WD = 0.01
ADAM_STEP = 10

N_CHIPS = 4
N_DEV = 8
MESH = pl.DeviceIdType.MESH
ANY = pl.BlockSpec(memory_space=pl.ANY)


def _params(sem):
    return pltpu.CompilerParams(dimension_semantics=sem)


MM_PIECE = 256


def _pieces(total, width=MM_PIECE):
    return [(off, min(width, total - off)) for off in range(0, total, width)]


def _mm(a, b, *, name, ta=False, tb=False, add=None, out_dtype=F32, tm=512, tn=512, tk=512, b_split=False,
        o_shards=False):
    k, m = a.shape if ta else a.shape[::-1]
    if b_split:
        _, kb, nh = b.shape
        n = 2 * nh
    elif tb:
        n, kb = b.shape
    else:
        kb, n = b.shape
    assert kb == k, (name, a.shape, b.shape)
    tm, tn, tk = min(tm, m), min(tn, n), min(tk, k)
    assert m % tm == 0 and n % tn == 0 and k % tk == 0, (name, m, n, k, tm, tn, tk)
    gm, gn, gk = m // tm, n // tn, k // tk

    a_spec = pl.BlockSpec((tk, tm), lambda i, j, kk: (kk, i)) if ta else pl.BlockSpec((tm, tk), lambda i, j, kk: (i, kk))
    if b_split:
        assert gn % 2 == 0
        b_spec = pl.BlockSpec((None, tk, tn), lambda i, j, kk: (j // (gn // 2), kk, j % (gn // 2)))
    elif tb:
        b_spec = pl.BlockSpec((tn, tk), lambda i, j, kk: (j, kk))
    else:
        b_spec = pl.BlockSpec((tk, tn), lambda i, j, kk: (kk, j))
    if o_shards:
        o_spec = pl.BlockSpec((None, tm, tn), lambda i, j, kk: (j, i, 0))
        o_shape = jax.ShapeDtypeStruct((gn, m, tn), out_dtype)
    else:
        o_spec = pl.BlockSpec((tm, tn), lambda i, j, kk: (i, j))
        o_shape = jax.ShapeDtypeStruct((m, n), out_dtype)
    dims = (((0 if ta else 1,), (1 if tb else 0,)), ((), ()))

    def body(*refs):
        a_ref, b_ref = refs[:2]
        add_ref = None if add is None else refs[2]
        o_ref = refs[2 if add is None else 3]
        acc_ref = None if gk == 1 else refs[-1]
        kk = pl.program_id(2)
        rhs = b_ref[...].astype(BF16)

        def finish(r, rows):
            if add_ref is not None:
                r = r + add_ref[rows, :]
            o_ref[rows, :] = r.astype(o_ref.dtype)

        for off, size in _pieces(tm):
            rows = slice(off, off + size)
            lhs = (a_ref[:, rows] if ta else a_ref[rows, :]).astype(BF16)
            part = lax.dot_general(lhs, rhs, dims, preferred_element_type=F32)
            if gk == 1:
                finish(part, rows)
            else:
                acc_ref[rows, :] = jnp.where(kk == 0, part, acc_ref[rows, :] + part)

        if gk > 1:
            @pl.when(kk == gk - 1)
            def _():
                finish(acc_ref[...], slice(None))

    in_specs = [a_spec, b_spec]
    args = [a, b]
    if add is not None:
        in_specs.append(pl.BlockSpec((tm, tn), lambda i, j, kk: (i, j)))
        args.append(add)
    return pl.pallas_call(
        body, name=name, grid=(gm, gn, gk), in_specs=in_specs, out_specs=o_spec, out_shape=o_shape,
        scratch_shapes=[] if gk == 1 else [pltpu.VMEM((tm, tn), F32)],
        compiler_params=_params(("parallel", "parallel", "arbitrary")),
    )(*args)


def _rms_scale(v):
    return lax.rsqrt(jnp.mean(v * v, axis=-1, keepdims=True) + EPS)


def _rms_bwd_rows(v, g, dy):
    r = _rms_scale(v)
    vh = v * r
    dyg = dy * g
    dv = r * (dyg - vh * jnp.mean(dyg * vh, axis=-1, keepdims=True))
    return dv, dy * vh


def _shift_down(v, first_row):
    row = lax.broadcasted_iota(jnp.int32, v.shape, 0)
    return jnp.where(row == 0, first_row, pltpu.roll(v, 1, 0))


def _shift_up(v, last_row):
    n = v.shape[0]
    row = lax.broadcasted_iota(jnp.int32, v.shape, 0)
    return jnp.where(row == n - 1, last_row, pltpu.roll(v, n - 1, 0))


def _rope(t, cos, sin_a, sin_b):
    return t * cos + pltpu.roll(t, HEAD_PAD - 16, 1) * sin_a + pltpu.roll(t, 16, 1) * sin_b


def _rope_bwd(d, cos, sin_a, sin_b):
    return d * cos + pltpu.roll(d * sin_a, 16, 1) + pltpu.roll(d * sin_b, HEAD_PAD - 16, 1)


def _sigmoid(v):
    return 1.0 / (1.0 + jnp.exp(-v))


def _halo_specs(ts, s, width, col):
    nb = ts // 8
    last = s // 8 - 1
    prev = pl.BlockSpec((8, width), lambda i: (jnp.maximum(i * nb - 1, 0), col))
    nxt = pl.BlockSpec((8, width), lambda i: (jnp.minimum((i + 1) * nb, last), col))
    return prev, nxt


def _mm_fused(a, b, *, name, epilogue, row_outs, rows=(), vecs=(), n_vec_out=0, tb=False, a_split=False,
              prologue=None, second=None, transposed_out=None, tm=512):
    if a_split:
        _, m, kh = a.shape
        k = 2 * kh
    else:
        m, k = a.shape
    n = b.shape[0] if tb else b.shape[1]
    assert (b.shape[1] if tb else b.shape[0]) == k, (name, a.shape, b.shape)
    assert m % tm == 0, (name, m, tm)
    n_a = 2 if a_split else 1
    nr, nv = len(rows), len(vecs)
    n_pro = 0 if prologue is None else 1
    n_sec = 0 if second is None else 2
    n_t = 0 if transposed_out is None else 1
    dims = (((1,), (1 if tb else 0,)), ((), ()))

    def body(*refs):
        a_refs, b_ref = refs[:n_a], refs[n_a]
        refs = refs[n_a + 1:]
        sec_refs = refs[:n_sec]
        row_refs, vec_refs = refs[n_sec:n_sec + nr], refs[n_sec + nr:n_sec + nr + nv]
        outs = refs[n_sec + nr + nv:]
        row_out_refs = outs[n_pro:n_pro + len(row_outs)]
        t_out_refs = outs[n_pro + len(row_outs):n_pro + len(row_outs) + n_t]
        vec_out_refs = outs[n_pro + len(row_outs) + n_t:n_pro + len(row_outs) + n_t + n_vec_out]
        vec_vals = [v[...] for v in vec_refs]
        if a_split:
            kh = k // 2
            rhs = [(b_ref[:, :kh], b_ref[:, kh:]) if tb else (b_ref[:kh, :], b_ref[kh:, :])][0]
            rhs = [h.astype(BF16) for h in rhs]
        else:
            rhs = [b_ref[...].astype(BF16)]
        vec_sums = [None] * n_vec_out

        for off, size in _pieces(tm):
            rs = slice(off, off + size)
            if prologue is None:
                lhs = [a_ref[rs, :].astype(BF16) for a_ref in a_refs]
            else:
                lhs = [prologue(a_refs[0][rs, :], vec_vals)]
                outs[0][rs, :] = lhs[0]
            r = lax.dot_general(lhs[0], rhs[0], dims, preferred_element_type=F32)
            for l2, r2 in zip(lhs[1:], rhs[1:]):
                r = r + lax.dot_general(l2, r2, dims, preferred_element_type=F32)
            if second is not None:
                r = r + jnp.dot(sec_refs[0][rs, :].astype(BF16), sec_refs[1][...].astype(BF16),
                                preferred_element_type=F32)
            row_vals, vec_parts = epilogue(r, [x[rs, :] for x in row_refs], vec_vals)
            for ref, val in zip(row_out_refs, row_vals):
                ref[rs, :] = val.astype(ref.dtype)
            for ref in t_out_refs:
                ref[:, rs] = row_vals[0].T.astype(ref.dtype)
            vec_sums = [p if t is None else t + p for t, p in zip(vec_sums, vec_parts)]

        if n_vec_out:
            @pl.when(pl.program_id(0) == 0)
            def _():
                for ref in vec_out_refs:
                    ref[...] = jnp.zeros_like(ref)

            for ref, val in zip(vec_out_refs, vec_sums):
                ref[...] += val

    if a_split:
        a_specs = [pl.BlockSpec((None, tm, k // 2), lambda i: (0, i, 0)),
                   pl.BlockSpec((None, tm, k // 2), lambda i: (1, i, 0))]
    else:
        a_specs = [pl.BlockSpec((tm, k), lambda i: (i, 0))]
    b_spec = pl.BlockSpec(b.shape, lambda i: (0, 0))
    row_spec = pl.BlockSpec((tm, n), lambda i: (i, 0))
    out_specs, out_shape = [], []
    if prologue is not None:
        out_specs.append(pl.BlockSpec((tm, k), lambda i: (i, 0)))
        out_shape.append(jax.ShapeDtypeStruct((m, k), BF16))
    out_specs += [row_spec] * len(row_outs)
    out_shape += [jax.ShapeDtypeStruct((m, n), dt) for dt in row_outs]
    if transposed_out is not None:
        out_specs.append(pl.BlockSpec((n, tm), lambda i: (0, i)))
        out_shape.append(jax.ShapeDtypeStruct((n, m), transposed_out))
    out_specs += [pl.BlockSpec((1, n), lambda i: (0, 0))] * n_vec_out
    out_shape += [jax.ShapeDtypeStruct((1, n), F32)] * n_vec_out
    sec_specs, sec_args = [], []
    if second is not None:
        k2 = second[0].shape[1]
        sec_specs = [pl.BlockSpec((tm, k2), lambda i: (i, 0)), pl.BlockSpec((k2, n), lambda i: (0, 0))]
        sec_args = list(second)
    res = pl.pallas_call(
        body, name=name, grid=(m // tm,),
        in_specs=a_specs + [b_spec] + sec_specs + [row_spec] * nr
        + [pl.BlockSpec((1, v.shape[1]), lambda i: (0, 0)) for v in vecs],
        out_specs=out_specs, out_shape=out_shape,
        compiler_params=_params(("arbitrary" if n_vec_out else "parallel",)),
    )(*([a] * n_a), b, *sec_args, *rows, *vecs)
    split = n_pro + len(row_outs) + n_t
    return list(res[:split]), list(res[split:])


def _pro_rms(a, vecs):
    return (a * _rms_scale(a) * vecs[0]).astype(BF16)


def _epi_plain(r, rows, vecs):
    return [r], []


def _epi_add_rms(r, rows, vecs):
    xn = r + rows[0]
    return [xn, xn * _rms_scale(xn) * vecs[0]], []


def _epi_rms_bwd(r, rows, vecs):
    dv, dg_rows = _rms_bwd_rows(rows[0], vecs[0], r)
    return [dv + rows[1]], [jnp.sum(dg_rows, axis=0, keepdims=True)]


def _mix_pre(z, conv_w8, gq, gkv, cos, sin_a, sin_b, *, ts=256):
    s = z.shape[0]
    n = s // ts
    cw = CONV_WIDTH

    def body(z_ref, xcp, xcn, cgp, cgn, w_ref, gq_ref, gkv_ref, cos_ref, sa_ref, sb_ref,
             yc_ref, qn_ref, kvn_ref, kr_ref):
        i = pl.program_id(0)
        xc = z_ref[:, 0:cw]
        bg = z_ref[:, cw:2 * cw]
        cg = z_ref[:, 2 * cw:3 * cw]
        m = cg * xc
        m_prev = jnp.where(i > 0, xcp[7:8, :] * cgp[7:8, :], 0.0)
        m_next = jnp.where(i < n - 1, xcn[0:1, :] * cgn[0:1, :], 0.0)
        cm = _shift_down(m, m_prev) * w_ref[0:1, :] + m * w_ref[1:2, :] + _shift_up(m, m_next) * w_ref[2:3, :]
        yc_ref[...] = (bg * cm).astype(BF16)
        ql = z_ref[:, 3 * cw:3 * cw + Q_LORA]
        qn_ref[...] = (ql * _rms_scale(ql) * gq_ref[...]).astype(BF16)
        kvl = z_ref[:, 3 * cw + Q_LORA:3 * cw + Q_LORA + KV_LORA]
        kvn_ref[...] = (kvl * _rms_scale(kvl) * gkv_ref[...]).astype(BF16)
        kr_ref[...] = _rope(z_ref[:, D_IN_PAD - HEAD_PAD:D_IN_PAD], cos_ref[...], sa_ref[...], sb_ref[...])

    xcp, xcn = _halo_specs(ts, s, cw, 0)
    cgp, cgn = _halo_specs(ts, s, cw, 2)
    tab = pl.BlockSpec((ts, HEAD_PAD), lambda i: (i, 0))
    return pl.pallas_call(
        body, name="mix_pre", grid=(n,),
        in_specs=[pl.BlockSpec((ts, D_IN_PAD), lambda i: (i, 0)), xcp, xcn, cgp, cgn,
                  pl.BlockSpec((8, cw), lambda i: (0, 0)), pl.BlockSpec((1, Q_LORA), lambda i: (0, 0)),
                  pl.BlockSpec((1, KV_LORA), lambda i: (0, 0)), tab, tab, tab],
        out_specs=[pl.BlockSpec((ts, cw), lambda i: (i, 0)), pl.BlockSpec((ts, Q_LORA), lambda i: (i, 0)),
                   pl.BlockSpec((ts, KV_LORA), lambda i: (i, 0)), tab],
        out_shape=[jax.ShapeDtypeStruct((s, cw), BF16), jax.ShapeDtypeStruct((s, Q_LORA), BF16),
                   jax.ShapeDtypeStruct((s, KV_LORA), BF16), jax.ShapeDtypeStruct((s, HEAD_PAD), F32)],
        compiler_params=_params(("parallel",)),
    )(z, z, z, z, z, conv_w8, gq, gkv, cos, sin_a, sin_b)


def _mix_bwd(z, dyc, dqn, dkvn, dkr, conv_w8, gq, gkv, cos, sin_a, sin_b, *, ts=256):
    s = z.shape[0]
    n = s // ts
    cw = CONV_WIDTH

    def body(z_ref, xcp, xcn, bgp, bgn, cgp, cgn, dyc_ref, dycp, dycn, dqn_ref, dkvn_ref, dkr_ref,
             w_ref, gq_ref, gkv_ref, cos_ref, sa_ref, sb_ref,
             dz_ref, dw0_ref, dw1_ref, dw2_ref, dgq_ref, dgkv_ref):
        i = pl.program_id(0)

        @pl.when(i == 0)
        def _():
            for r in (dw0_ref, dw1_ref, dw2_ref, dgq_ref, dgkv_ref):
                r[...] = jnp.zeros_like(r)

        xc = z_ref[:, 0:cw]
        bg = z_ref[:, cw:2 * cw]
        cg = z_ref[:, 2 * cw:3 * cw]
        w0, w1, w2 = w_ref[0:1, :], w_ref[1:2, :], w_ref[2:3, :]
        m = cg * xc
        m_dn = _shift_down(m, jnp.where(i > 0, xcp[7:8, :] * cgp[7:8, :], 0.0))
        m_up = _shift_up(m, jnp.where(i < n - 1, xcn[0:1, :] * cgn[0:1, :], 0.0))
        cm = m_dn * w0 + m * w1 + m_up * w2
        dyc_v = dyc_ref[...]
        dcm = dyc_v * bg
        dcm_dn = _shift_down(dcm, jnp.where(i > 0, dycp[7:8, :] * bgp[7:8, :], 0.0))
        dcm_up = _shift_up(dcm, jnp.where(i < n - 1, dycn[0:1, :] * bgn[0:1, :], 0.0))
        dm = dcm_up * w0 + dcm * w1 + dcm_dn * w2
        dz_ref[:, 0:cw] = (dm * cg).astype(BF16)
        dz_ref[:, cw:2 * cw] = (dyc_v * cm).astype(BF16)
        dz_ref[:, 2 * cw:3 * cw] = (dm * xc).astype(BF16)
        dw0_ref[...] += jnp.sum(dcm * m_dn, axis=0, keepdims=True)
        dw1_ref[...] += jnp.sum(dcm * m, axis=0, keepdims=True)
        dw2_ref[...] += jnp.sum(dcm * m_up, axis=0, keepdims=True)

        dql, dgq_rows = _rms_bwd_rows(z_ref[:, 3 * cw:3 * cw + Q_LORA], gq_ref[...], dqn_ref[...])
        dz_ref[:, 3 * cw:3 * cw + Q_LORA] = dql.astype(BF16)
        dgq_ref[...] += jnp.sum(dgq_rows, axis=0, keepdims=True)
        dkvl, dgkv_rows = _rms_bwd_rows(z_ref[:, 3 * cw + Q_LORA:3 * cw + Q_LORA + KV_LORA], gkv_ref[...],
                                        dkvn_ref[...])
        dz_ref[:, 3 * cw + Q_LORA:3 * cw + Q_LORA + KV_LORA] = dkvl.astype(BF16)
        dgkv_ref[...] += jnp.sum(dgkv_rows, axis=0, keepdims=True)

        lane = lax.broadcasted_iota(jnp.int32, (ts, HEAD_PAD), 1)
        rope_lane = (lane >= QK_NOPE) & (lane < QK_NOPE + QK_ROPE)
        dk = _rope_bwd(dkr_ref[...], cos_ref[...], sa_ref[...], sb_ref[...])
        dz_ref[:, D_IN_PAD - HEAD_PAD:D_IN_PAD] = jnp.where(rope_lane, dk, 0.0).astype(BF16)

    xcp, xcn = _halo_specs(ts, s, cw, 0)
    bgp, bgn = _halo_specs(ts, s, cw, 1)
    cgp, cgn = _halo_specs(ts, s, cw, 2)
    dycp, dycn = _halo_specs(ts, s, cw, 0)
    tab = pl.BlockSpec((ts, HEAD_PAD), lambda i: (i, 0))

    def vec(width):
        return pl.BlockSpec((1, width), lambda i: (0, 0))

    outs = pl.pallas_call(
        body, name="mix_bwd", grid=(n,),
        in_specs=[pl.BlockSpec((ts, D_IN_PAD), lambda i: (i, 0)), xcp, xcn, bgp, bgn, cgp, cgn,
                  pl.BlockSpec((ts, cw), lambda i: (i, 0)), dycp, dycn,
                  pl.BlockSpec((ts, Q_LORA), lambda i: (i, 0)), pl.BlockSpec((ts, KV_LORA), lambda i: (i, 0)), tab,
                  pl.BlockSpec((8, cw), lambda i: (0, 0)), vec(Q_LORA), vec(KV_LORA), tab, tab, tab],
        out_specs=[pl.BlockSpec((ts, D_IN_PAD), lambda i: (i, 0)), vec(cw), vec(cw), vec(cw), vec(Q_LORA),
                   vec(KV_LORA)],
        out_shape=[jax.ShapeDtypeStruct((s, D_IN_PAD), BF16)] + [jax.ShapeDtypeStruct((1, cw), F32)] * 3
        + [jax.ShapeDtypeStruct((1, Q_LORA), F32), jax.ShapeDtypeStruct((1, KV_LORA), F32)],
        compiler_params=_params(("arbitrary",)),
    )(z, z, z, z, z, z, z, dyc, dyc, dyc, dqn, dkvn, dkr, conv_w8, gq, gkv, cos, sin_a, sin_b)
    dz, dw0, dw1, dw2, dgq, dgkv = outs
    return dz, jnp.concatenate([dw0, dw1, dw2], axis=0), dgq, dgkv


def _qkv_proj(qn, kvn, kr, w_uq_p, w_kv_p, cos, sin_a, sin_b, *, ts=512):
    s = qn.shape[0]

    def body(qn_ref, kvn_ref, kr_ref, wq_ref, wkv_ref, cos_ref, sa_ref, sb_ref, q_ref, k_ref, v_ref, qt_ref):
        cos_v, sa, sb = cos_ref[...], sa_ref[...], sb_ref[...]
        q = jnp.dot(qn_ref[...], wq_ref[...], preferred_element_type=F32)
        kv = jnp.dot(kvn_ref[...], wkv_ref[...], preferred_element_type=F32)
        kr_v = kr_ref[...]
        lane = lax.broadcasted_iota(jnp.int32, (1, HEAD_PAD), 1)
        ones_lane = (lane == ONES_LANE).astype(F32)
        for h in range(N_HEADS):
            blk = slice(h * HEAD_PAD, (h + 1) * HEAD_PAD)
            q_h = _rope(q[:, blk], cos_v, sa, sb) * SM_SCALE
            q_ref[:, blk] = q_h.astype(BF16)
            qt_ref[blk, :] = q_h.T.astype(BF16)
            k_ref[:, blk] = (kv[:, blk] + kr_v).astype(BF16)
            v_ref[:, blk] = (kv[:, D_ATT + h * HEAD_PAD:D_ATT + (h + 1) * HEAD_PAD] + ones_lane).astype(BF16)

    tab = pl.BlockSpec((ts, HEAD_PAD), lambda i: (i, 0))
    wide = pl.BlockSpec((ts, D_ATT), lambda i: (i, 0))
    return pl.pallas_call(
        body, name="qkv_proj", grid=(s // ts,),
        in_specs=[pl.BlockSpec((ts, Q_LORA), lambda i: (i, 0)), pl.BlockSpec((ts, KV_LORA), lambda i: (i, 0)), tab,
                  pl.BlockSpec((Q_LORA, D_ATT), lambda i: (0, 0)), pl.BlockSpec((KV_LORA, 2 * D_ATT), lambda i: (0, 0)),
                  tab, tab, tab],
        out_specs=[wide, wide, wide, pl.BlockSpec((D_ATT, ts), lambda i: (0, i))],
        out_shape=[jax.ShapeDtypeStruct((s, D_ATT), BF16)] * 3 + [jax.ShapeDtypeStruct((D_ATT, s), BF16)],
        compiler_params=_params(("parallel",)),
    )(qn, kvn, kr, w_uq_p, w_kv_p, cos, sin_a, sin_b)


def _qk_bwd(dq, dk, cos, sin_a, sin_b, *, ts=256):
    s = dq.shape[0]

    def body(dq_ref, dk_ref, cos_ref, sa_ref, sb_ref, dqp_ref, dkr_ref):
        cos_v, sa, sb = cos_ref[...], sa_ref[...], sb_ref[...]
        tot = jnp.zeros((ts, HEAD_PAD), F32)
        for h in range(N_HEADS):
            blk = slice(h * HEAD_PAD, (h + 1) * HEAD_PAD)
            dqp_ref[:, blk] = _rope_bwd(dq_ref[:, blk], cos_v, sa, sb).astype(BF16)
            tot = tot + dk_ref[:, blk]
        dkr_ref[...] = tot

    tab = pl.BlockSpec((ts, HEAD_PAD), lambda i: (i, 0))
    wide = pl.BlockSpec((ts, D_ATT), lambda i: (i, 0))
    return pl.pallas_call(
        body, name="qk_bwd", grid=(s // ts,),
        in_specs=[wide, wide, tab, tab, tab], out_specs=[wide, tab],
        out_shape=[jax.ShapeDtypeStruct((s, D_ATT), BF16), jax.ShapeDtypeStruct((s, HEAD_PAD), F32)],
        compiler_params=_params(("parallel",)),
    )(dq, dk, cos, sin_a, sin_b)


_NT = (((1,), (1,)), ((), ()))


def _flash_fwd(q, k, v, *, tq=1024, tk=1024, per_trip=8, q_per_step=2):
    s = q.shape[0]
    tq, tk = min(tq, s), min(tk, s)
    nk = s // tk
    per_trip = min(per_trip, nk)
    assert nk % per_trip == 0
    q_per_step = min(q_per_step, s // tq)

    def body(q_ref, k_ref, v_ref, o_ref, lse_ref):
        for blk in range(q_per_step):
            rows = pl.ds(blk * tq, tq)
            one_block(q_ref.at[rows, :], k_ref, v_ref, o_ref.at[rows, :], lse_ref.at[rows, :])

    def one_block(q_ref, k_ref, v_ref, o_ref, lse_ref):
        qv = q_ref[...]

        def step(j, carry):
            m, acc = carry
            rows = pl.ds(pl.multiple_of(j * tk, tk), tk)
            sc = lax.dot_general(qv, k_ref[rows, :], _NT, preferred_element_type=F32)
            m_new = jnp.maximum(m, jnp.max(sc, axis=1, keepdims=True))
            p = jnp.exp(sc - m_new).astype(BF16)
            acc = jnp.exp(m - m_new) * acc + jnp.dot(p, v_ref[rows, :], preferred_element_type=F32)
            return m_new, acc

        def trip(t, carry):
            for c in range(per_trip):
                carry = step(per_trip * t + c, carry)
            return carry

        init = (jnp.full((tq, 1), -jnp.inf, F32), jnp.zeros((tq, HEAD_PAD), F32))
        m, acc = lax.fori_loop(0, nk // per_trip, trip, init)
        l = acc[:, ONES_LANE:ONES_LANE + 1]
        o_ref[...] = (acc / l).astype(BF16)
        lse_ref[...] = m + jnp.log(l)

    head = pl.BlockSpec((s, HEAD_PAD), lambda h, i: (0, h))
    tq_step = q_per_step * tq
    return pl.pallas_call(
        body, name="flash_fwd", grid=(N_HEADS, s // tq_step),
        in_specs=[pl.BlockSpec((tq_step, HEAD_PAD), lambda h, i: (i, h)), head, head],
        out_specs=[pl.BlockSpec((tq_step, HEAD_PAD), lambda h, i: (i, h)),
                   pl.BlockSpec((None, tq_step, 1), lambda h, i: (h, i, 0))],
        out_shape=[jax.ShapeDtypeStruct((s, D_ATT), BF16), jax.ShapeDtypeStruct((N_HEADS, s, 1), F32)],
        compiler_params=_params(("parallel", "parallel")),
    )(q, k, v)


def _attn_delta(do, o, *, ts=512):
    s = do.shape[0]

    def body(do_ref, o_ref, dl_ref):
        for h in range(N_HEADS):
            blk = slice(h * HEAD_PAD, (h + 1) * HEAD_PAD)
            dl_ref[h] = jnp.sum(do_ref[:, blk].astype(F32) * o_ref[:, blk].astype(F32), axis=1, keepdims=True)

    wide = pl.BlockSpec((ts, D_ATT), lambda i: (i, 0))
    return pl.pallas_call(
        body, name="attn_delta", grid=(s // ts,), in_specs=[wide, wide],
        out_specs=pl.BlockSpec((N_HEADS, ts, 1), lambda i: (0, i, 0)),
        out_shape=jax.ShapeDtypeStruct((N_HEADS, s, 1), F32),
        compiler_params=_params(("parallel",)),
    )(do, o)


def _flash_bwd(q, qt, k, v, do, dot, lse, delta, after, *, tq=1024, tk=512, per_trip=8, kv_per_step=2):
    s = q.shape[0]
    tq, tk = min(tq, s), min(tk, s)
    nq = s // tq
    per_trip = min(per_trip, nq)
    assert nq % per_trip == 0
    kv_per_step = min(kv_per_step, s // tk)

    def body(q_ref, qt_ref, do_ref, dot_ref, lse_ref, dl_ref, k_ref, v_ref, after_ref, dq_ref, dk_ref, dv_ref):
        j = pl.program_id(1)

        @pl.when(j == 0)
        def _():
            dq_ref[...] = jnp.zeros_like(dq_ref)

        for blk in range(kv_per_step):
            one_block(q_ref, qt_ref, do_ref, dot_ref, lse_ref, dl_ref, dq_ref,
                      k_ref.at[pl.ds(blk * tk, tk), :], v_ref.at[pl.ds(blk * tk, tk), :],
                      dk_ref.at[pl.ds(blk * tk, tk), :], dv_ref.at[pl.ds(blk * tk, tk), :])

        @pl.when(j == pl.num_programs(1) - 1)
        def _():
            dq_ref[...] *= SM_SCALE

    def one_block(q_ref, qt_ref, do_ref, dot_ref, lse_ref, dl_ref, dq_ref, k_ref, v_ref, dk_ref, dv_ref):
        kv, vv = k_ref[...], v_ref[...]

        def chunk(i, dk_t, dv_t):
            at = pl.multiple_of(i * tq, tq)
            rows = pl.ds(at, tq)
            sc = lax.dot_general(q_ref[rows, :], kv, _NT, preferred_element_type=F32)
            p = jnp.exp(sc - lse_ref[rows, :])
            dp = lax.dot_general(do_ref[rows, :], vv, _NT, preferred_element_type=F32)
            ds = (p * (dp - dl_ref[rows, :])).astype(BF16)
            dv_t = dv_t + jnp.dot(dot_ref[:, rows], p.astype(BF16), preferred_element_type=F32)
            dk_t = dk_t + jnp.dot(qt_ref[:, rows], ds, preferred_element_type=F32)
            dq_ref[rows, :] += jnp.dot(ds, kv, preferred_element_type=F32)
            return dk_t, dv_t

        def step(i, carry):
            for c in range(per_trip):
                carry = chunk(per_trip * i + c, *carry)
            return carry

        zero = jnp.zeros((HEAD_PAD, tk), F32)
        dk_t, dv_t = lax.fori_loop(0, nq // per_trip, step, (zero, zero))
        dk_ref[...] = dk_t.T
        dv_ref[...] = dv_t.T

    head = pl.BlockSpec((s, HEAD_PAD), lambda h, j: (0, h))
    head_t = pl.BlockSpec((HEAD_PAD, s), lambda h, j: (h, 0))
    stat = pl.BlockSpec((None, s, 1), lambda h, j: (h, 0, 0))
    blk = pl.BlockSpec((kv_per_step * tk, HEAD_PAD), lambda h, j: (j, h))
    return pl.pallas_call(
        body, name="flash_bwd", grid=(N_HEADS, s // (kv_per_step * tk)),
        in_specs=[head, head_t, head, head_t, stat, stat, blk, blk, ANY],
        out_specs=[head, blk, blk],
        out_shape=[jax.ShapeDtypeStruct((s, D_ATT), F32)] * 3,
        compiler_params=_params(("parallel", "arbitrary")),
    )(q, qt, do, dot, lse, delta, k, v, after)


FFN_TC = 256
FFN_TG = 1408


FFN_HALO_BF16 = 16
FFN_HALO_F32 = 8


def _row_halo_specs(ts, s, halo, width):
    nb = ts // halo
    last = s // halo - 1
    prev = pl.BlockSpec((halo, width), lambda i, j: (jnp.maximum(i * nb - 1, 0), 0))
    nxt = pl.BlockSpec((halo, width), lambda i, j: (jnp.minimum((i + 1) * nb, last), 0))
    return prev, nxt


def _ext_rows(prev, main, nxt, first, last):
    return jnp.concatenate([jnp.where(first, jnp.zeros_like(prev), prev), main,
                            jnp.where(last, jnp.zeros_like(nxt), nxt)], axis=0)


def _ext_conv(a, w):
    a_dn = pltpu.roll(a, 1, 0)
    a_up = pltpu.roll(a, a.shape[0] - 1, 0)
    return a_dn * w[0:1, :] + a * w[1:2, :] + a_up * w[2:3, :], a_dn, a_up


def _ffn_pieces(tg):
    return [(off, min(FFN_TC, tg - off)) for off in range(0, tg, FFN_TC)]


def _ffn_fwd(hf, w_up, w, b, *, ts=512, tg=FFN_TG):
    s = hf.shape[0]
    n, ng, halo = s // ts, D_FF // tg, FFN_HALO_BF16

    def body(h_ref, hp_ref, hn_ref, wg_ref, wu_ref, cw_ref, cb_ref, a_ref, act_ref):
        i = pl.program_id(0)
        ext = _ext_rows(hp_ref[...], h_ref[...], hn_ref[...], i == 0, i == n - 1)
        for off, width in _ffn_pieces(tg):
            cols = slice(off, off + width)
            gate_up = []
            for half, w_ref in enumerate((wg_ref, wu_ref)):
                a_ext = jnp.dot(ext, w_ref[:, cols], preferred_element_type=F32)
                a_ref[half, :, cols] = a_ext[halo:halo + ts]
                conv = _ext_conv(a_ext, cw_ref[half, :, cols])[0]
                gate_up.append(conv[halo:halo + ts] + cb_ref[half, :, cols])
            g, u = gate_up
            act_ref[:, cols] = (g * _sigmoid(g) * u).astype(BF16)

    prev, nxt = _row_halo_specs(ts, s, halo, D_MODEL)
    return pl.pallas_call(
        body, name="ffn_fwd", grid=(n, ng),
        in_specs=[pl.BlockSpec((ts, D_MODEL), lambda i, j: (i, 0)), prev, nxt,
                  pl.BlockSpec((D_MODEL, tg), lambda i, j: (0, j)), pl.BlockSpec((D_MODEL, tg), lambda i, j: (0, j + ng)),
                  pl.BlockSpec((2, 8, tg), lambda i, j: (0, 0, j)), pl.BlockSpec((2, 1, tg), lambda i, j: (0, 0, j))],
        out_specs=[pl.BlockSpec((2, ts, tg), lambda i, j: (0, i, j)), pl.BlockSpec((ts, tg), lambda i, j: (i, j))],
        out_shape=[jax.ShapeDtypeStruct((2, s, D_FF), F32), jax.ShapeDtypeStruct((s, D_FF), BF16)],
        compiler_params=_params(("parallel", "parallel")),
    )(hf, hf, hf, w_up, w_up, w, b)


def _ffn_bwd(dx2, w_down, a_pre, w, b, *, ts=512, tg=FFN_TG):
    s = dx2.shape[0]
    n, ng, halo = s // ts, D_FF // tg, FFN_HALO_F32
    main = slice(halo, halo + ts)

    def body(dx_ref, dxp_ref, dxn_ref, wd_ref, a_ref, ap_ref, an_ref, cw_ref, cb_ref, o_ref, dw_ref, db_ref):
        i, j = pl.program_id(0), pl.program_id(1)
        first, last = i == 0, i == n - 1

        @pl.when(first & (j == 0))
        def _():
            dw_ref[...] = jnp.zeros_like(dw_ref)
            db_ref[...] = jnp.zeros_like(db_ref)

        dx_ext = _ext_rows(dxp_ref[...], dx_ref[...], dxn_ref[...], first, last).astype(BF16)
        for off, width in _ffn_pieces(tg):
            cols = slice(off, off + width)
            dact = lax.dot_general(dx_ext, wd_ref[cols, :], _NT, preferred_element_type=F32)
            halves = []
            for half in range(2):
                a_ext = _ext_rows(ap_ref[half, :, cols], a_ref[half, :, cols], an_ref[half, :, cols], first, last)
                conv, a_dn, a_up = _ext_conv(a_ext, cw_ref[half, :, cols])
                halves.append((conv + cb_ref[half, :, cols], a_dn, a_ext, a_up))
            g, u = halves[0][0], halves[1][0]
            sg = _sigmoid(g)
            grads = (dact * u * (sg * (1.0 + g * (1.0 - sg))), dact * (g * sg))
            for half in range(2):
                d = grads[half]
                _, a_dn, a_ext, a_up = halves[half]
                wv = cw_ref[half, :, cols]
                d_pre = (pltpu.roll(d, d.shape[0] - 1, 0) * wv[0:1, :] + d * wv[1:2, :]
                         + pltpu.roll(d, 1, 0) * wv[2:3, :])
                o_ref[half, :, cols] = d_pre[main].astype(BF16)
                dm = d[main]
                dw_ref[j, half, 0:1, cols] += jnp.sum(dm * a_dn[main], axis=0, keepdims=True)
                dw_ref[j, half, 1:2, cols] += jnp.sum(dm * a_ext[main], axis=0, keepdims=True)
                dw_ref[j, half, 2:3, cols] += jnp.sum(dm * a_up[main], axis=0, keepdims=True)
                db_ref[j, half, :, cols] += jnp.sum(dm, axis=0, keepdims=True)

    dxp, dxn = _row_halo_specs(ts, s, halo, D_MODEL)
    nb, lastb = ts // halo, s // halo - 1
    a_main = pl.BlockSpec((2, ts, tg), lambda i, j: (0, i, j))
    a_prev = pl.BlockSpec((2, halo, tg), lambda i, j: (0, jnp.maximum(i * nb - 1, 0), j))
    a_next = pl.BlockSpec((2, halo, tg), lambda i, j: (0, jnp.minimum((i + 1) * nb, lastb), j))
    da_pre, dw, db = pl.pallas_call(
        body, name="ffn_bwd", grid=(n, ng),
        in_specs=[pl.BlockSpec((ts, D_MODEL), lambda i, j: (i, 0)), dxp, dxn,
                  pl.BlockSpec((tg, D_MODEL), lambda i, j: (j, 0)), a_main, a_prev, a_next,
                  pl.BlockSpec((2, 8, tg), lambda i, j: (0, 0, j)), pl.BlockSpec((2, 1, tg), lambda i, j: (0, 0, j))],
        out_specs=[a_main, pl.BlockSpec((ng, 2, 8, tg), lambda i, j: (0, 0, 0, 0)),
                   pl.BlockSpec((ng, 2, 1, tg), lambda i, j: (0, 0, 0, 0))],
        out_shape=[jax.ShapeDtypeStruct((2, s, D_FF), BF16), jax.ShapeDtypeStruct((ng, 2, 8, tg), F32),
                   jax.ShapeDtypeStruct((ng, 2, 1, tg), F32)],
        compiler_params=_params(("arbitrary", "arbitrary")),
    )(dx2, dx2, dx2, w_down, a_pre, a_pre, a_pre, w, b)
    return (da_pre, dw.transpose(1, 2, 0, 3).reshape(2, 8, D_FF), db.transpose(1, 2, 0, 3).reshape(2, 1, D_FF))


def _ple_final(x2, n3, p, target, gf, w_pg, w_pp, *, ts=256):
    s, d = x2.shape
    dp = p.shape[1]

    def body(x2_ref, n3_ref, p_ref, t_ref, gf_ref, wg_ref, wp_ref, loss_ref, dx3_ref, dgl_ref, dpp_ref, dgf_ref):
        @pl.when(pl.program_id(0) == 0)
        def _():
            loss_ref[...] = jnp.zeros_like(loss_ref)
            dgf_ref[...] = jnp.zeros_like(dgf_ref)

        gate = _sigmoid(jnp.dot(n3_ref[...], wg_ref[...], preferred_element_type=F32))
        ppv = jnp.dot(p_ref[...].astype(BF16), wp_ref[...], preferred_element_type=F32)
        x3 = x2_ref[...] + gate * ppv
        gfv = gf_ref[...]
        err = x3 * _rms_scale(x3) * gfv - t_ref[...]
        loss_ref[...] += 0.5 * jnp.sum(jnp.mean(err * err, axis=-1, keepdims=True), axis=0, keepdims=True)
        dx3, dgf_rows = _rms_bwd_rows(x3, gfv, err * (1.0 / d))
        dgf_ref[...] += jnp.sum(dgf_rows, axis=0, keepdims=True)
        dx3_ref[...] = dx3
        dgl_ref[...] = (dx3 * ppv * gate * (1.0 - gate)).astype(BF16)
        dpp_ref[...] = (dx3 * gate).astype(BF16)

    row = pl.BlockSpec((ts, d), lambda i: (i, 0))
    vec = pl.BlockSpec((1, d), lambda i: (0, 0))
    return pl.pallas_call(
        body, name="ple_final", grid=(s // ts,),
        in_specs=[row, row, pl.BlockSpec((ts, dp), lambda i: (i, 0)), row, vec,
                  pl.BlockSpec((d, d), lambda i: (0, 0)), pl.BlockSpec((dp, d), lambda i: (0, 0))],
        out_specs=[pl.BlockSpec((1, 128), lambda i: (0, 0)), row, row, row, vec],
        out_shape=[jax.ShapeDtypeStruct((1, 128), F32), jax.ShapeDtypeStruct((s, d), F32),
                   jax.ShapeDtypeStruct((s, d), BF16), jax.ShapeDtypeStruct((s, d), BF16),
                   jax.ShapeDtypeStruct((1, d), F32)],
        compiler_params=_params(("arbitrary",)),
    )(x2, n3, p, target, gf, w_pg, w_pp)


def _row_tile(rows, cols, n_arrays, budget=12 << 20):
    best = None
    for t in range(8, rows + 1, 8):
        if rows % t == 0 and t * cols * 4 * n_arrays <= budget:
            best = t
    return rows if best is None else best


def _sum_slots(a, *, name):
    g, r, c = a.shape
    tr = _row_tile(r, c, g + 1)

    def body(*refs):
        tot = refs[0][...]
        for ref in refs[1:g]:
            tot = tot + ref[...]
        refs[g][...] = tot

    specs = [pl.BlockSpec((None, tr, c), functools.partial(lambda i, slot: (slot, i, 0), slot=k)) for k in range(g)]
    return pl.pallas_call(
        body, name=name, grid=(r // tr,), in_specs=specs, out_specs=pl.BlockSpec((tr, c), lambda i: (i, 0)),
        out_shape=jax.ShapeDtypeStruct((r, c), a.dtype), compiler_params=_params(("parallel",)),
    )(*([a] * g))


def _adamw_refs(w_ref, g_ref, m_ref, v_ref, d_ref, mo_ref, vo_ref):
    gv = g_ref[...]
    mn = ADAM_B1 * m_ref[...] + (1.0 - ADAM_B1) * gv
    vn = ADAM_B2 * v_ref[...] + (1.0 - ADAM_B2) * (gv * gv)
    m_hat = mn / (1.0 - ADAM_B1 ** ADAM_STEP)
    v_hat = vn / (1.0 - ADAM_B2 ** ADAM_STEP)
    d_ref[...] = -ADAM_LR * (m_hat / (jnp.sqrt(v_hat) + ADAM_EPS) + ADAM_WD * w_ref[...])
    mo_ref[...] = mn
    vo_ref[...] = vn


def _adamw_many(ws, gs, ms, vs, *, name):
    n = len(ws)

    def body(*refs):
        ins, outs = refs[:4 * n], refs[4 * n:]
        for a in range(n):
            _adamw_refs(ins[a], ins[n + a], ins[2 * n + a], ins[3 * n + a], outs[3 * a], outs[3 * a + 1],
                        outs[3 * a + 2])

    vm = pl.BlockSpec(memory_space=pltpu.VMEM)
    res = pl.pallas_call(
        body, name=name, in_specs=[vm] * (4 * n), out_specs=[vm] * (3 * n),
        out_shape=[jax.ShapeDtypeStruct(a.shape, F32) for a in ws for _ in range(3)],
    )(*ws, *gs, *ms, *vs)
    return [tuple(res[3 * a:3 * a + 3]) for a in range(n)]


def _adamw(w, g, m, v, *, name):
    r, c = w.shape
    tr = _row_tile(r, c, 7)
    body = _adamw_refs

    blk = pl.BlockSpec((tr, c), lambda i: (i, 0))
    return pl.pallas_call(
        body, name=name, grid=(r // tr,), in_specs=[blk] * 4, out_specs=[blk] * 3,
        out_shape=[jax.ShapeDtypeStruct((r, c), F32)] * 3, compiler_params=_params(("parallel",)),
    )(w, g, m, v)


def _position():
    x, y, c = lax.axis_index("x"), lax.axis_index("y"), lax.axis_index("c")
    return x, y, c


def _other_chips(x, y):
    return [(1 - x, y), (x, 1 - y), (1 - x, 1 - y)]


def _stage_in(srcs, stage, sems):
    cps = [pltpu.make_async_copy(src, stage[a], sems.at[a]) for a, src in enumerate(srcs)]
    for cp in cps:
        cp.start()
    return cps


def _stage_out(staged, stage, dsts, sems):
    cps = []
    for a, dst in enumerate(dsts):
        staged[a].wait()
        cp = pltpu.make_async_copy(stage[a], dst, sems.at[a])
        cp.start()
        cps.append(cp)
    return cps


def _send_other_halves(grads, *, tag):
    n = len(grads)

    def body(*refs):
        ins, sib = refs[:n], refs[n:2 * n]
        send_sems, recv_sems = refs[2 * n:]
        x, y, c = _position()
        remote = []
        for a in range(n):
            half = ins[a].shape[1] // 2
            give = ins[a].at[:, pl.ds(pl.multiple_of((1 - c) * half, 8), half), :]
            rc = pltpu.make_async_remote_copy(
                src_ref=give, dst_ref=sib[a], send_sem=send_sems.at[a], recv_sem=recv_sems.at[a],
                device_id=(x, y, 1 - c), device_id_type=MESH)
            rc.start()
            remote.append(rc)
        for rc in remote:
            rc.wait_recv()
        for rc in remote:
            rc.wait_send()

    return pl.pallas_call(
        body, name="send_other_halves_" + tag, in_specs=[ANY] * n, out_specs=[ANY] * n,
        out_shape=[jax.ShapeDtypeStruct((g.shape[0], g.shape[1] // 2, g.shape[2]), g.dtype) for g in grads],
        scratch_shapes=[pltpu.SemaphoreType.DMA((n,)), pltpu.SemaphoreType.DMA((n,))],
        compiler_params=pltpu.CompilerParams(has_side_effects=True),
    )(*grads)


def _add_own_half(g4, sib, core, *, name):
    g, a2, c = sib.shape
    tr = _row_tile(a2, c, 4)

    def body(core_ref, a_ref, b_ref, o_ref, o16_ref):
        tot = a_ref[...] + b_ref[...]
        o_ref[...] = tot
        o16_ref[...] = tot.astype(BF16)

    blk = pl.BlockSpec((None, tr, c), lambda i, j, core_ref: (i, j, 0))
    return pl.pallas_call(
        body, name=name,
        grid_spec=pltpu.PrefetchScalarGridSpec(
            num_scalar_prefetch=1, grid=(g, a2 // tr),
            in_specs=[pl.BlockSpec((None, None, tr, c), lambda i, j, core_ref: (i, core_ref[0], j, 0)), blk],
            out_specs=[blk, blk]),
        out_shape=[jax.ShapeDtypeStruct(sib.shape, F32), jax.ShapeDtypeStruct(sib.shape, BF16)],
        compiler_params=_params(("parallel", "parallel")),
    )(core, g4.reshape(g, 2, a2, c), sib)


def _sum_chips(landed, own, chip, *, name):
    g, r, c = landed.shape
    tr = _row_tile(r, c, 5)

    def body(chip_ref, *refs):
        me = chip_ref[0]
        own_v = refs[g][...]
        tot = None
        for slot in range(g):
            term = jnp.where(me == slot, own_v, refs[slot][...].astype(F32))
            tot = term if tot is None else tot + term
        refs[g + 1][...] = tot

    def landed_spec(slot):
        return pl.BlockSpec((None, tr, c),
                            lambda i, chip_ref: (jnp.where(chip_ref[0] == slot, (slot + 1) % g, slot), i, 0))

    return pl.pallas_call(
        body, name=name,
        grid_spec=pltpu.PrefetchScalarGridSpec(
            num_scalar_prefetch=1, grid=(r // tr,),
            in_specs=[landed_spec(k) for k in range(g)]
            + [pl.BlockSpec((None, tr, c), lambda i, chip_ref: (chip_ref[0], i, 0))],
            out_specs=pl.BlockSpec((tr, c), lambda i, chip_ref: (i, 0))),
        out_shape=jax.ShapeDtypeStruct((r, c), F32), compiler_params=_params(("parallel",)),
    )(chip, *([landed] * g), own)


def _join_halves(halves):
    n = len(halves)

    def body(*refs):
        ins, outs, stage = refs[:n], refs[n:2 * n], refs[2 * n:3 * n]
        send_sems, recv_sems, in_sems, out_sems = refs[3 * n:]
        x, y, c = _position()
        remote = []
        staged = _stage_in(ins, stage, in_sems)
        for a in range(n):
            rc = pltpu.make_async_remote_copy(
                src_ref=ins[a], dst_ref=outs[a].at[c], send_sem=send_sems.at[a], recv_sem=recv_sems.at[a],
                device_id=(x, y, 1 - c), device_id_type=MESH)
            rc.start()
            remote.append(rc)
        local = _stage_out(staged, stage, [o.at[c] for o in outs], out_sems)
        for a in range(n):
            pltpu.make_async_remote_copy(
                src_ref=ins[a], dst_ref=outs[a].at[1 - c], send_sem=send_sems.at[a], recv_sem=recv_sems.at[a],
                device_id=(x, y, 1 - c), device_id_type=MESH).wait_recv()
        for rc in remote:
            rc.wait_send()
        for cp in local:
            cp.wait()

    return pl.pallas_call(
        body, name="join_halves", in_specs=[ANY] * n, out_specs=[ANY] * n,
        out_shape=[jax.ShapeDtypeStruct((2,) + h.shape, h.dtype) for h in halves],
        scratch_shapes=[pltpu.VMEM(h.shape, h.dtype) for h in halves]
        + [pltpu.SemaphoreType.DMA((n,)), pltpu.SemaphoreType.DMA((n,)), pltpu.SemaphoreType.DMA((n,)),
           pltpu.SemaphoreType.DMA((n,))],
        compiler_params=pltpu.CompilerParams(has_side_effects=True),
    )(*halves)


_HBM = pl.BlockSpec(memory_space=pltpu.HBM)
_SEM = pl.BlockSpec(memory_space=pltpu.SEMAPHORE)


def _chip_copies(srcs, lands, send_sems, recv_sems, scatter):
    x, y, c = _position()
    me = 2 * x + y
    outgoing, incoming = [], []
    for a, (src, land) in enumerate(zip(srcs, lands)):
        for k, (px, py) in enumerate(_other_chips(x, y)):
            peer = 2 * px + py
            sems = dict(send_sem=send_sems.at[3 * a + k], recv_sem=recv_sems.at[3 * a + k], device_id=(px, py, c),
                        device_id_type=MESH)
            outgoing.append(pltpu.make_async_remote_copy(
                src_ref=src.at[peer] if scatter else src, dst_ref=land.at[me], **sems))
            incoming.append(pltpu.make_async_remote_copy(
                src_ref=src.at[me] if scatter else src, dst_ref=land.at[peer], **sems))
    return outgoing, incoming


def _chips_start(srcs, *, scatter, name):
    n = len(srcs)
    lands = [lax.empty(a.shape if scatter else (N_CHIPS,) + a.shape, a.dtype) for a in srcs]

    def body(*refs):
        ins, send_sems, recv_sems, token = refs[:2 * n], refs[2 * n], refs[2 * n + 1], refs[-1]
        outgoing, _ = _chip_copies(ins[:n], ins[n:], send_sems, recv_sems, scatter)
        for cp in outgoing:
            cp.start()
        token[...] = jnp.zeros_like(token)

    bufs = list(srcs) + lands
    res = pl.pallas_call(
        body, name=name, in_specs=[_HBM] * (2 * n),
        out_specs=(_SEM, _SEM, *[_HBM] * (2 * n), pl.BlockSpec(memory_space=pltpu.VMEM)),
        out_shape=(pltpu.SemaphoreType.DMA((3 * n,)), pltpu.SemaphoreType.DMA((3 * n,)),
                   *[pltpu.HBM(a.shape, a.dtype) for a in bufs], jax.ShapeDtypeStruct((8, 128), F32)),
        input_output_aliases={i: 2 + i for i in range(2 * n)},
        compiler_params=pltpu.CompilerParams(has_side_effects=pltpu.SideEffectType.DATAFLOW_SIDE_EFFECTING),
    )(*[pltpu.with_memory_space_constraint(a, pltpu.HBM) for a in bufs])
    return res[0], res[1], list(res[2:2 + n]), list(res[2 + n:2 + 2 * n]), res[-1]


def _chips_wait(handle, after, *, scatter, name):
    send_sems, recv_sems, srcs, lands, _ = handle
    n = len(srcs)

    def body(*refs):
        ins, send_ref, recv_ref = refs[:2 * n], refs[2 * n], refs[2 * n + 1]
        outgoing, incoming = _chip_copies(ins[:n], ins[n:], send_ref, recv_ref, scatter)
        for cp in outgoing:
            cp.wait_send()
        for cp in incoming:
            cp.wait_recv()

    bufs = list(srcs) + list(lands)
    res = pl.pallas_call(
        body, name=name, in_specs=[_HBM] * (2 * n) + [_SEM, _SEM, ANY], out_specs=tuple([_HBM] * (2 * n)),
        out_shape=tuple(pltpu.HBM(a.shape, a.dtype) for a in bufs),
        input_output_aliases={i: i for i in range(2 * n)},
        compiler_params=pltpu.CompilerParams(has_side_effects=pltpu.SideEffectType.DATAFLOW_SIDE_EFFECTING),
    )(*bufs, send_sems, recv_sems, after)
    return list(res[:n]), list(res[n:])


def _gather_all(buf):
    def body(in_ref, out_ref, send_sems, recv_sems, local_sem):
        x, y, c = _position()
        me = 4 * x + 2 * y + c
        peers = [(x, y, 1 - c)] + [(px, py, pc) for (px, py) in _other_chips(x, y) for pc in (c, 1 - c)]
        cp = pltpu.make_async_copy(in_ref, out_ref.at[me], local_sem)
        cp.start()
        remote = []
        for k, peer in enumerate(peers):
            rc = pltpu.make_async_remote_copy(
                src_ref=in_ref, dst_ref=out_ref.at[me], send_sem=send_sems.at[k], recv_sem=recv_sems.at[k],
                device_id=peer, device_id_type=MESH)
            rc.start()
            remote.append(rc)
        for k, (px, py, pc) in enumerate(peers):
            pltpu.make_async_remote_copy(
                src_ref=in_ref, dst_ref=out_ref.at[4 * px + 2 * py + pc], send_sem=send_sems.at[k],
                recv_sem=recv_sems.at[k], device_id=(px, py, pc), device_id_type=MESH).wait_recv()
        for rc in remote:
            rc.wait_send()
        cp.wait()

    return pl.pallas_call(
        body, name="gather_all", in_specs=[ANY], out_specs=ANY,
        out_shape=jax.ShapeDtypeStruct((N_DEV,) + buf.shape, buf.dtype),
        scratch_shapes=[pltpu.SemaphoreType.DMA((N_DEV - 1,)), pltpu.SemaphoreType.DMA((N_DEV - 1,)),
                        pltpu.SemaphoreType.DMA],
        compiler_params=pltpu.CompilerParams(has_side_effects=True),
    )(buf)


def _cols_from_shards(g4):
    _, k, n = g4.shape
    return g4.transpose(1, 0, 2).reshape(k, N_CHIPS * n)


def _cols_to_shards(w):
    k, n = w.shape
    return w.reshape(k, N_CHIPS, n // N_CHIPS).transpose(1, 0, 2)


def _pad_heads(w, width):
    k = w.shape[0]
    w3 = w.reshape(k, N_HEADS, width)
    return jnp.pad(w3, ((0, 0), (0, 0), (0, HEAD_PAD - width))).reshape(k, D_ATT)


def _unpad_heads(w, width):
    k = w.shape[0]
    return w.reshape(k, N_HEADS, HEAD_PAD)[:, :, :width]


def _rope_tables(s, after):
    pos = jnp.arange(s, dtype=F32) + after
    inv_freq = ROPE_THETA ** (-jnp.arange(0, QK_ROPE, 2, dtype=F32) / QK_ROPE)
    ang = pos[:, None] * inv_freq[None, :]
    cos_h, sin_h = jnp.cos(ang), jnp.sin(ang)
    half = QK_ROPE // 2
    z = jnp.zeros((s, half), F32)
    ones = jnp.ones((s, QK_NOPE), F32)
    tail = jnp.zeros((s, HEAD_PAD - QK_NOPE - QK_ROPE), F32)
    cos = jnp.concatenate([ones, cos_h, cos_h, tail + 1.0], axis=1)
    sin_a = jnp.concatenate([ones * 0.0, -sin_h, z, tail], axis=1)
    sin_b = jnp.concatenate([ones * 0.0, z, sin_h, tail], axis=1)
    return cos, sin_a, sin_b


def _local_step(x, p, target, wts, late_weights, reduce_early, reduce_last):
    s = x.shape[0]
    cos, sin_a, sin_b = wts["rope"]
    g1, gq, gkv, g2, g3, gf = (wts[k] for k in ("norm_mix_g", "q_norm_g", "kv_norm_g", "norm_ffn_g", "ple_norm_g",
                                                 "final_norm_g"))
    w_in_p, w_uq_p, w_kv_p = wts["w_in_p"], wts["w_uq_p"], wts["w_kv_p"]
    conv_w8, fconv_w, fconv_b = wts["conv_w8"], wts["ffn_conv_w"], wts["ffn_conv_b"]

    (h, z), _ = _mm_fused(x, w_in_p, name="mm_in", prologue=_pro_rms, vecs=[g1], epilogue=_epi_plain, row_outs=[F32])
    y_conv, qn, kvn, kr = _mix_pre(z, conv_w8, gq, gkv, cos, sin_a, sin_b)
    q, k, v, q_t = _qkv_proj(qn, kvn, kr, w_uq_p, w_kv_p, cos, sin_a, sin_b)
    o, lse = _flash_fwd(q, k, v)
    late = late_weights(lse)
    w_o_a, w_o_b, w_up, w_down = late["w_o_a"], late["w_o_b"], late["w_up"], late["w_down"]
    w_pg, w_pp = late["w_ple_gate"], late["w_ple_proj"]
    (x1, hf), _ = _mm_fused(o, w_o_b, second=(y_conv, w_o_a), name="mm_o", rows=[x], vecs=[g2],
                            epilogue=_epi_add_rms, row_outs=[F32, BF16])
    a_pre, act = _ffn_fwd(hf, w_up, fconv_w, fconv_b)
    (x2, n3), _ = _mm_fused(act, w_down, name="mm_down", rows=[x1], vecs=[g3], epilogue=_epi_add_rms,
                            row_outs=[F32, BF16])
    loss, dx3, dgl, dpp, d_gf = _ple_final(x2, n3, p, target, gf, w_pg, w_pp)

    grads, early = {"final_norm_g": d_gf}, {}
    early["w_ple_proj"] = _mm(p, dpp, ta=True, name="mm_d_wpp", tm=256, tn=1024, tk=2048)
    early["w_ple_gate"] = _mm(n3, dgl, ta=True, name="mm_d_wpg", tm=1024, tn=1024, tk=2048)
    (dx2,), (grads["ple_norm_g"],) = _mm_fused(dgl, w_pg, tb=True, name="mm_d_n3", rows=[x2, dx3], vecs=[g3],
                                               epilogue=_epi_rms_bwd, row_outs=[F32], n_vec_out=1)
    early["w_down"] = _mm(act, dx2, ta=True, name="mm_d_wdown", tm=1408, tn=1024, tk=2048)
    da_pre, grads["ffn_conv_w"], grads["ffn_conv_b"] = _ffn_bwd(dx2, w_down, a_pre, fconv_w, fconv_b)
    early["w_up"] = _mm(hf, da_pre, ta=True, b_split=True, name="mm_d_wup", tm=1024, tn=1408, tk=2048,
                       o_shards=True)
    (dx1,), (grads["norm_ffn_g"],) = _mm_fused(da_pre, w_up, tb=True, a_split=True, name="mm_d_hf", rows=[x1, dx2],
                                               vecs=[g2], epilogue=_epi_rms_bwd, row_outs=[F32], n_vec_out=1)
    d_wo_a = _mm(y_conv, dx1, ta=True, name="mm_d_wo_conv", tm=512, tn=1024, tk=2048)
    d_wo_b = _mm(o, dx1, ta=True, name="mm_d_wo_att", tm=1024, tn=1024, tk=2048)
    early["w_o"] = jnp.concatenate([d_wo_a, d_wo_b.reshape(N_HEADS, HEAD_PAD, D_MODEL)[:, :V_HEAD]
                                    .reshape(N_HEADS * V_HEAD, D_MODEL)], axis=0)
    token, finish = reduce_early(early)
    dyc = _mm(dx1, w_o_a, tb=True, name="mm_d_yconv", tm=512, tn=512, tk=1024)
    (do, do_t), _ = _mm_fused(dx1, w_o_b, tb=True, name="mm_d_o", epilogue=_epi_plain, row_outs=[BF16],
                              transposed_out=BF16)
    delta = _attn_delta(do, o)
    dq, dk, dv = _flash_bwd(q, q_t, k, v, do, do_t, lse, delta, token)
    reduced_early = finish(dq)
    dq_pre, dkr = _qk_bwd(dq, dk, cos, sin_a, sin_b)
    grads["w_uq_p"] = _mm(qn, dq_pre, ta=True, name="mm_d_wuq", tm=256, tn=1024, tk=2048)
    dqn = _mm(dq_pre, w_uq_p, tb=True, name="mm_d_qn", tm=512, tn=256, tk=1024)
    grads["w_k_p"] = _mm(kvn, dk, ta=True, name="mm_d_wk", tm=128, tn=1024, tk=2048)
    grads["w_v_p"] = _mm(kvn, dv, ta=True, name="mm_d_wv", tm=128, tn=1024, tk=2048)
    (dkvn,), _ = _mm_fused(dk, w_kv_p[:, :D_ATT], tb=True, second=(dv, w_kv_p[:, D_ATT:].T), name="mm_d_kvn",
                           epilogue=_epi_plain, row_outs=[F32])
    dz, grads["conv_w"], grads["q_norm_g"], grads["kv_norm_g"] = _mix_bwd(
        z, dyc, dqn, dkvn, dkr, conv_w8, gq, gkv, cos, sin_a, sin_b)
    grads["w_in_p"] = _mm(h, dz, ta=True, name="mm_d_win", tm=1024, tn=1024, tk=2048)
    token, finish = reduce_last({n: grads.pop(n) for n in ("w_in_p", "w_uq_p", "w_k_p", "w_v_p")})
    (grad_x,), (grads["norm_mix_g"],) = _mm_fused(dz, w_in_p, tb=True, name="mm_d_h", rows=[x, dx1],
                                                  vecs=[g1 + token[0, 0]], epilogue=_epi_rms_bwd, row_outs=[F32],
                                                  n_vec_out=1)
    return loss[0, 0], grad_x, grads, reduced_early, finish(grad_x)


_EARLY_W = ("w_in", "w_uq", "w_ukv")
_LATE_W = ("w_o", "w_up", "w_down", "w_ple_gate", "w_ple_proj")
_BIG = _EARLY_W + _LATE_W
_COL_SHARDED = ("w_in", "w_uq", "w_ukv", "w_up", "w_ple_proj")
_SMALL = ("norm_mix_g", "conv_w", "q_norm_g", "kv_norm_g", "norm_ffn_g", "ffn_conv_w", "ffn_conv_b", "ple_norm_g",
          "final_norm_g")


def _full_from_slots(n, g4):
    return _cols_from_shards(g4) if n in _COL_SHARDED else g4.reshape(-1, g4.shape[2])


def _shard_major(n, g):
    if g.ndim == 3:
        return g
    return _cols_to_shards(g) if n in _COL_SHARDED else g.reshape(N_CHIPS, g.shape[0] // N_CHIPS, g.shape[1])


def _early_shards(w):
    shards = [w[n][0].astype(BF16) for n in _EARLY_W]
    shards.append(jnp.pad(w["conv_w"][0], ((0, 5), (0, 0))))
    shards.append(jnp.pad(w["ffn_conv_w"][0], ((0, 5), (0, 0))))
    return shards


def _fill_own_slot(landed, own, chip):
    return [lax.dynamic_update_slice(g4, a[None], (chip[0], 0, 0)) for g4, a in zip(landed, own)]


def _early_weights(got, w):
    full = {n: _full_from_slots(n, g4) for n, g4 in zip(_EARLY_W, got)}
    full["conv_w8"] = _cols_from_shards(got[len(_EARLY_W)])
    full["ffn_conv_w8"] = _cols_from_shards(got[len(_EARLY_W) + 1])
    return _layout_early(full, w)


def _layout_early(full, w):
    out = {n: w[n] for n in ("norm_mix_g", "q_norm_g", "kv_norm_g", "norm_ffn_g", "ple_norm_g")}
    out["final_norm_g"] = w["final_norm_g"][None, :]
    w_in = full["w_in"]
    zc = jnp.zeros((D_MODEL, QK_NOPE), BF16)
    zt = jnp.zeros((D_MODEL, HEAD_PAD - QK_NOPE - QK_ROPE), BF16)
    out["w_in_p"] = jnp.concatenate([w_in[:, :D_IN - QK_ROPE], zc, w_in[:, D_IN - QK_ROPE:], zt], axis=1)
    out["w_uq_p"] = _pad_heads(full["w_uq"], QK_NOPE + QK_ROPE)
    kv3 = full["w_ukv"].reshape(KV_LORA, N_HEADS, QK_NOPE + V_HEAD)
    out["w_kv_p"] = jnp.concatenate([_pad_heads(kv3[:, :, :QK_NOPE].reshape(KV_LORA, -1), QK_NOPE),
                                     _pad_heads(kv3[:, :, QK_NOPE:].reshape(KV_LORA, -1), V_HEAD)], axis=1)
    out["conv_w8"] = full["conv_w8"]
    fw = full["ffn_conv_w8"]
    out["ffn_conv_w"] = jnp.stack([fw[:, :D_FF], fw[:, D_FF:]])
    out["ffn_conv_b"] = w["ffn_conv_b"].reshape(2, 1, D_FF)
    return out


def _layout_late(full):
    w_o = full["w_o"]
    out = {"w_o_a": w_o[:CONV_WIDTH]}
    out["w_o_b"] = jnp.pad(w_o[CONV_WIDTH:].reshape(N_HEADS, V_HEAD, D_MODEL),
                           ((0, 0), (0, HEAD_PAD - V_HEAD), (0, 0))).reshape(D_ATT, D_MODEL)
    for n in ("w_up", "w_down", "w_ple_gate", "w_ple_proj"):
        out[n] = full[n]
    return out


def _true_matrices(g):
    out = {}
    wp = g["w_in_p"]
    out["w_in"] = jnp.concatenate([wp[:, :D_IN - QK_ROPE], wp[:, D_IN_PAD - HEAD_PAD + QK_NOPE:
                                                              D_IN_PAD - HEAD_PAD + QK_NOPE + QK_ROPE]], axis=1)
    out["w_uq"] = _unpad_heads(g["w_uq_p"], QK_NOPE + QK_ROPE).reshape(Q_LORA, -1)
    out["w_ukv"] = jnp.concatenate([_unpad_heads(g["w_k_p"], QK_NOPE), _unpad_heads(g["w_v_p"], V_HEAD)],
                                   axis=2).reshape(KV_LORA, -1)
    return out


def _true_vectors(g):
    out = {}
    out["conv_w"] = g["conv_w"]
    fw = g["ffn_conv_w"]
    out["ffn_conv_w"] = jnp.concatenate([fw[0, :3], fw[1, :3]], axis=1)
    out["ffn_conv_b"] = g["ffn_conv_b"].reshape(1, 2 * D_FF)
    for n in ("norm_mix_g", "q_norm_g", "kv_norm_g", "norm_ffn_g", "ple_norm_g", "final_norm_g"):
        out[n] = g[n]
    return out


def _chip_partials(names, g, core, *, tag):
    g4 = [_shard_major(n, g[n]) for n in names]
    sib = _send_other_halves(g4, tag=tag)
    return [_add_own_half(a, b, core, name="add_cores_" + n) for n, a, b in zip(names, g4, sib)]


_SMALL_SIZES = {"norm_mix_g": D_MODEL, "conv_w": 3 * CONV_WIDTH, "q_norm_g": Q_LORA, "kv_norm_g": KV_LORA,
                "norm_ffn_g": D_MODEL, "ffn_conv_w": 6 * D_FF, "ffn_conv_b": 2 * D_FF, "ple_norm_g": D_MODEL,
                "final_norm_g": D_MODEL}


def _pack(parts, rows):
    flat = jnp.concatenate([a.reshape(-1) for a in parts])
    return jnp.pad(flat, (0, rows * 128 - flat.shape[0])).reshape(rows, 128)


def _unpack(buf, sizes):
    flat = buf.reshape(-1)
    out, at = [], 0
    for n in sizes:
        out.append(flat[at:at + n])
        at += n
    return out


def _reduce_small(g, loss):
    sizes = [1] + [_SMALL_SIZES[n] for n in _SMALL]
    rows = -(-sum(sizes) // 1024) * 8
    slots = _gather_all(_pack([loss] + [g[n] for n in _SMALL], rows))
    parts = _unpack(_sum_slots(slots, name="sum_small"), sizes)
    return parts[0][0], dict(zip(_SMALL, parts[1:]))


def kernel(x, p, norm_mix_g, w_in, conv_w, q_norm_g, w_uq, kv_norm_g, w_ukv, w_o, norm_ffn_g, w_up, ffn_conv_w, ffn_conv_b, w_down, ple_norm_g, w_ple_gate, w_ple_proj, final_norm_g, loss_target, m_norm_mix_g, m_w_in, m_conv_w, m_q_norm_g, m_w_uq, m_kv_norm_g, m_w_ukv, m_w_o, m_norm_ffn_g, m_w_up, m_ffn_conv_w, m_ffn_conv_b, m_w_down, m_ple_norm_g, m_w_ple_gate, m_w_ple_proj, m_final_norm_g, v_norm_mix_g, v_w_in, v_conv_w, v_q_norm_g, v_w_uq, v_kv_norm_g, v_w_ukv, v_w_o, v_norm_ffn_g, v_w_up, v_ffn_conv_w, v_ffn_conv_b, v_w_down, v_ple_norm_g, v_w_ple_gate, v_w_ple_proj, v_final_norm_g):
    names = ["norm_mix_g", "w_in", "conv_w", "q_norm_g", "w_uq", "kv_norm_g", "w_ukv", "w_o", "norm_ffn_g", "w_up",
             "ffn_conv_w", "ffn_conv_b", "w_down", "ple_norm_g", "w_ple_gate", "w_ple_proj", "final_norm_g"]
    w = dict(zip(names, (norm_mix_g, w_in, conv_w, q_norm_g, w_uq, kv_norm_g, w_ukv, w_o, norm_ffn_g, w_up,
                         ffn_conv_w, ffn_conv_b, w_down, ple_norm_g, w_ple_gate, w_ple_proj, final_norm_g)))
    m = dict(zip(names, (m_norm_mix_g, m_w_in, m_conv_w, m_q_norm_g, m_w_uq, m_kv_norm_g, m_w_ukv, m_w_o,
                         m_norm_ffn_g, m_w_up, m_ffn_conv_w, m_ffn_conv_b, m_w_down, m_ple_norm_g, m_w_ple_gate,
                         m_w_ple_proj, m_final_norm_g)))
    v = dict(zip(names, (v_norm_mix_g, v_w_in, v_conv_w, v_q_norm_g, v_w_uq, v_kv_norm_g, v_w_ukv, v_w_o,
                         v_norm_ffn_g, v_w_up, v_ffn_conv_w, v_ffn_conv_b, v_w_down, v_ple_norm_g, v_w_ple_gate,
                         v_w_ple_proj, v_final_norm_g)))

    core = lax.axis_index("c").astype(jnp.int32).reshape(1)
    chip = (2 * lax.axis_index("x") + lax.axis_index("y")).astype(jnp.int32).reshape(1)

    first = _chips_start(_early_shards(w), scatter=False, name="gather_early_start")
    rope = _rope_tables(x.shape[1], first[4][0, 0])
    late_shards = [w[n][0].astype(BF16) for n in _LATE_W]
    ready, *late_shards = lax.optimization_barrier((rope[0], *late_shards))
    own, landed = _chips_wait(first, ready, scatter=False, name="gather_early_wait")
    wts = _early_weights(_fill_own_slot(landed, own, chip), w)
    wts["rope"] = (ready,) + tuple(rope[1:])
    late_shards[0], _ = lax.optimization_barrier((late_shards[0], own[0]))
    gather = _chips_start(late_shards, scatter=False, name="gather_late_start")
    wts["norm_mix_g"] = wts["norm_mix_g"] + gather[4][0, 0]

    def late_weights(after):
        shards, landed = _chips_wait(gather, after, scatter=False, name="gather_late_wait")
        return _layout_late({n: _full_from_slots(n, g4)
                             for n, g4 in zip(_LATE_W, _fill_own_slot(landed, shards, chip))})

    def reduce_early(g):
        parts = _chip_partials(_LATE_W, g, core, tag="early")
        scatter = _chips_start([t16 for _, t16 in parts], scatter=True, name="scatter_early_start")

        def finish(after):
            _, landed = _chips_wait(scatter, after, scatter=True, name="scatter_early_wait")
            return [_sum_chips(a, t32, chip, name="sum_chips_" + n) for n, a, (t32, _) in zip(_LATE_W, landed, parts)]

        return scatter[4], finish

    def reduce_last(g):
        parts = _chip_partials(_EARLY_W, _true_matrices(g), core, tag="late")
        scatter = _chips_start([t16 for _, t16 in parts], scatter=True, name="scatter_late_start")

        def finish(after):
            _, landed = _chips_wait(scatter, after, scatter=True, name="scatter_late_wait")
            return [_sum_chips(a, t32, chip, name="sum_chips_" + n) for n, a, (t32, _) in zip(_EARLY_W, landed, parts)]

        return scatter[4], finish

    loss, grad_x, small_grads, halves_early, halves_last = _local_step(
        x[0], p[0, 0], loss_target[0], wts, late_weights, reduce_early, reduce_last)
    g_full = _true_vectors(small_grads)
    whole = _join_halves(halves_last + halves_early)
    big = {n: a.reshape(-1, a.shape[2]) for n, a in zip(_BIG, whole)}

    g_out, d_out, m_out, v_out = {}, {}, {}, {}
    for n in _BIG:
        shape = w[n].shape
        g = big[n]
        d, mn, vn = _adamw(w[n][0], g, m[n][0], v[n][0], name="adamw_" + n)
        g_out[n], d_out[n], m_out[n], v_out[n] = (a.reshape(shape) for a in (g, d, mn, vn))

    loss, small = _reduce_small(g_full, loss)
    chip = 2 * lax.axis_index("x") + lax.axis_index("y")
    g_small = {}
    for n in _SMALL:
        shape = w[n].shape
        g = small[n]
        if n in ("conv_w", "ffn_conv_w"):
            width = shape[-1]
            g = lax.dynamic_slice(g.reshape(3, N_CHIPS * width), (0, chip * width), (3, width))
        g_small[n] = g.reshape(shape)
    flat = [[src[n].reshape(-1, src[n].shape[-1]) for n in _SMALL] for src in (w, g_small, m, v)]
    for n, (d, mn, vn) in zip(_SMALL, _adamw_many(*flat, name="adamw_small")):
        shape = w[n].shape
        g_out[n], d_out[n], m_out[n], v_out[n] = g_small[n], d.reshape(shape), mn.reshape(shape), vn.reshape(shape)

    return (loss, grad_x[None], *[g_out[n] for n in names], *[d_out[n] for n in names],
            *[m_out[n] for n in names], *[v_out[n] for n in names])
```

```python
import functools

import jax
import jax.numpy as jnp
from jax import lax
from jax.experimental import pallas as pl
from jax.experimental.pallas import tpu as pltpu

F32 = jnp.float32
BF16 = jnp.bfloat16

D_MODEL = 1024
CONV_WIDTH = 512
Q_LORA = 256
KV_LORA = 128
QK_NOPE = 64
QK_ROPE = 32
V_HEAD = 64
N_HEADS = 8
HEAD_PAD = 128
D_ATT = N_HEADS * HEAD_PAD
D_IN = 3 * CONV_WIDTH + Q_LORA + KV_LORA + QK_ROPE
D_IN_PAD = 3 * CONV_WIDTH + Q_LORA + KV_LORA + HEAD_PAD
D_FF = 2816
ROPE_THETA = 10000.0
EPS = 1e-6
SM_SCALE = (QK_NOPE + QK_ROPE) ** -0.5
ONES_LANE = V_HEAD

ADAM_LR = 0.001
ADAM_B1 = 0.9
ADAM_B2 = 0.999
ADAM_EPS = 1e-08
ADAM_WD = 0.01
ADAM_STEP = 10

N_CHIPS = 4
N_DEV = 8
MESH = pl.DeviceIdType.MESH
ANY = pl.BlockSpec(memory_space=pl.ANY)


def _params(sem):
    return pltpu.CompilerParams(dimension_semantics=sem)


MM_PIECE = 256


def _pieces(total, width=MM_PIECE):
    return [(off, min(width, total - off)) for off in range(0, total, width)]


def _mm(a, b, *, name, ta=False, tb=False, add=None, out_dtype=F32, tm=512, tn=512, tk=512, b_split=False,
        o_shards=False):
    k, m = a.shape if ta else a.shape[::-1]
    if b_split:
        _, kb, nh = b.shape
        n = 2 * nh
    elif tb:
        n, kb = b.shape
    else:
        kb, n = b.shape
    assert kb == k, (name, a.shape, b.shape)
    tm, tn, tk = min(tm, m), min(tn, n), min(tk, k)
    assert m % tm == 0 and n % tn == 0 and k % tk == 0, (name, m, n, k, tm, tn, tk)
    gm, gn, gk = m // tm, n // tn, k // tk

    a_spec = pl.BlockSpec((tk, tm), lambda i, j, kk: (kk, i)) if ta else pl.BlockSpec((tm, tk), lambda i, j, kk: (i, kk))
    if b_split:
        assert gn % 2 == 0
        b_spec = pl.BlockSpec((None, tk, tn), lambda i, j, kk: (j // (gn // 2), kk, j % (gn // 2)))
    elif tb:
        b_spec = pl.BlockSpec((tn, tk), lambda i, j, kk: (j, kk))
    else:
        b_spec = pl.BlockSpec((tk, tn), lambda i, j, kk: (kk, j))
    if o_shards:
        o_spec = pl.BlockSpec((None, tm, tn), lambda i, j, kk: (j, i, 0))
        o_shape = jax.ShapeDtypeStruct((gn, m, tn), out_dtype)
    else:
        o_spec = pl.BlockSpec((tm, tn), lambda i, j, kk: (i, j))
        o_shape = jax.ShapeDtypeStruct((m, n), out_dtype)
    dims = (((0 if ta else 1,), (1 if tb else 0,)), ((), ()))

    def body(*refs):
        a_ref, b_ref = refs[:2]
        add_ref = None if add is None else refs[2]
        o_ref = refs[2 if add is None else 3]
        acc_ref = None if gk == 1 else refs[-1]
        kk = pl.program_id(2)
        rhs = b_ref[...].astype(BF16)

        def finish(r, rows):
            if add_ref is not None:
                r = r + add_ref[rows, :]
            o_ref[rows, :] = r.astype(o_ref.dtype)

        for off, size in _pieces(tm):
            rows = slice(off, off + size)
            lhs = (a_ref[:, rows] if ta else a_ref[rows, :]).astype(BF16)
            part = lax.dot_general(lhs, rhs, dims, preferred_element_type=F32)
            if gk == 1:
                finish(part, rows)
            else:
                acc_ref[rows, :] = jnp.where(kk == 0, part, acc_ref[rows, :] + part)

        if gk > 1:
            @pl.when(kk == gk - 1)
            def _():
                finish(acc_ref[...], slice(None))

    in_specs = [a_spec, b_spec]
    args = [a, b]
    if add is not None:
        in_specs.append(pl.BlockSpec((tm, tn), lambda i, j, kk: (i, j)))
        args.append(add)
    return pl.pallas_call(
        body, name=name, grid=(gm, gn, gk), in_specs=in_specs, out_specs=o_spec, out_shape=o_shape,
        scratch_shapes=[] if gk == 1 else [pltpu.VMEM((tm, tn), F32)],
        compiler_params=_params(("parallel", "parallel", "arbitrary")),
    )(*args)


def _rms_scale(v):
    return lax.rsqrt(jnp.mean(v * v, axis=-1, keepdims=True) + EPS)


def _rms_bwd_rows(v, g, dy):
    r = _rms_scale(v)
    vh = v * r
    dyg = dy * g
    dv = r * (dyg - vh * jnp.mean(dyg * vh, axis=-1, keepdims=True))
    return dv, dy * vh


def _shift_down(v, first_row):
    row = lax.broadcasted_iota(jnp.int32, v.shape, 0)
    return jnp.where(row == 0, first_row, pltpu.roll(v, 1, 0))


def _shift_up(v, last_row):
    n = v.shape[0]
    row = lax.broadcasted_iota(jnp.int32, v.shape, 0)
    return jnp.where(row == n - 1, last_row, pltpu.roll(v, n - 1, 0))


def _rope(t, cos, sin_a, sin_b):
    return t * cos + pltpu.roll(t, HEAD_PAD - 16, 1) * sin_a + pltpu.roll(t, 16, 1) * sin_b


def _rope_bwd(d, cos, sin_a, sin_b):
    return d * cos + pltpu.roll(d * sin_a, 16, 1) + pltpu.roll(d * sin_b, HEAD_PAD - 16, 1)


def _sigmoid(v):
    return 1.0 / (1.0 + jnp.exp(-v))


def _halo_specs(ts, s, width, col):
    nb = ts // 8
    last = s // 8 - 1
    prev = pl.BlockSpec((8, width), lambda i: (jnp.maximum(i * nb - 1, 0), col))
    nxt = pl.BlockSpec((8, width), lambda i: (jnp.minimum((i + 1) * nb, last), col))
    return prev, nxt


def _mm_fused(a, b, *, name, epilogue, row_outs, rows=(), vecs=(), n_vec_out=0, tb=False, a_split=False,
              prologue=None, second=None, transposed_out=None, tm=512):
    if a_split:
        _, m, kh = a.shape
        k = 2 * kh
    else:
        m, k = a.shape
    n = b.shape[0] if tb else b.shape[1]
    assert (b.shape[1] if tb else b.shape[0]) == k, (name, a.shape, b.shape)
    assert m % tm == 0, (name, m, tm)
    n_a = 2 if a_split else 1
    nr, nv = len(rows), len(vecs)
    n_pro = 0 if prologue is None else 1
    n_sec = 0 if second is None else 2
    n_t = 0 if transposed_out is None else 1
    dims = (((1,), (1 if tb else 0,)), ((), ()))

    def body(*refs):
        a_refs, b_ref = refs[:n_a], refs[n_a]
        refs = refs[n_a + 1:]
        sec_refs = refs[:n_sec]
        row_refs, vec_refs = refs[n_sec:n_sec + nr], refs[n_sec + nr:n_sec + nr + nv]
        outs = refs[n_sec + nr + nv:]
        row_out_refs = outs[n_pro:n_pro + len(row_outs)]
        t_out_refs = outs[n_pro + len(row_outs):n_pro + len(row_outs) + n_t]
        vec_out_refs = outs[n_pro + len(row_outs) + n_t:n_pro + len(row_outs) + n_t + n_vec_out]
        vec_vals = [v[...] for v in vec_refs]
        if a_split:
            kh = k // 2
            rhs = [(b_ref[:, :kh], b_ref[:, kh:]) if tb else (b_ref[:kh, :], b_ref[kh:, :])][0]
            rhs = [h.astype(BF16) for h in rhs]
        else:
            rhs = [b_ref[...].astype(BF16)]
        vec_sums = [None] * n_vec_out

        for off, size in _pieces(tm):
            rs = slice(off, off + size)
            if prologue is None:
                lhs = [a_ref[rs, :].astype(BF16) for a_ref in a_refs]
            else:
                lhs = [prologue(a_refs[0][rs, :], vec_vals)]
                outs[0][rs, :] = lhs[0]
            r = lax.dot_general(lhs[0], rhs[0], dims, preferred_element_type=F32)
            for l2, r2 in zip(lhs[1:], rhs[1:]):
                r = r + lax.dot_general(l2, r2, dims, preferred_element_type=F32)
            if second is not None:
                r = r + jnp.dot(sec_refs[0][rs, :].astype(BF16), sec_refs[1][...].astype(BF16),
                                preferred_element_type=F32)
            row_vals, vec_parts = epilogue(r, [x[rs, :] for x in row_refs], vec_vals)
            for ref, val in zip(row_out_refs, row_vals):
                ref[rs, :] = val.astype(ref.dtype)
            for ref in t_out_refs:
                ref[:, rs] = row_vals[0].T.astype(ref.dtype)
            vec_sums = [p if t is None else t + p for t, p in zip(vec_sums, vec_parts)]

        if n_vec_out:
            @pl.when(pl.program_id(0) == 0)
            def _():
                for ref in vec_out_refs:
                    ref[...] = jnp.zeros_like(ref)

            for ref, val in zip(vec_out_refs, vec_sums):
                ref[...] += val

    if a_split:
        a_specs = [pl.BlockSpec((None, tm, k // 2), lambda i: (0, i, 0)),
                   pl.BlockSpec((None, tm, k // 2), lambda i: (1, i, 0))]
    else:
        a_specs = [pl.BlockSpec((tm, k), lambda i: (i, 0))]
    b_spec = pl.BlockSpec(b.shape, lambda i: (0, 0))
    row_spec = pl.BlockSpec((tm, n), lambda i: (i, 0))
    out_specs, out_shape = [], []
    if prologue is not None:
        out_specs.append(pl.BlockSpec((tm, k), lambda i: (i, 0)))
        out_shape.append(jax.ShapeDtypeStruct((m, k), BF16))
    out_specs += [row_spec] * len(row_outs)
    out_shape += [jax.ShapeDtypeStruct((m, n), dt) for dt in row_outs]
    if transposed_out is not None:
        out_specs.append(pl.BlockSpec((n, tm), lambda i: (0, i)))
        out_shape.append(jax.ShapeDtypeStruct((n, m), transposed_out))
    out_specs += [pl.BlockSpec((1, n), lambda i: (0, 0))] * n_vec_out
    out_shape += [jax.ShapeDtypeStruct((1, n), F32)] * n_vec_out
    sec_specs, sec_args = [], []
    if second is not None:
        k2 = second[0].shape[1]
        sec_specs = [pl.BlockSpec((tm, k2), lambda i: (i, 0)), pl.BlockSpec((k2, n), lambda i: (0, 0))]
        sec_args = list(second)
    res = pl.pallas_call(
        body, name=name, grid=(m // tm,),
        in_specs=a_specs + [b_spec] + sec_specs + [row_spec] * nr
        + [pl.BlockSpec((1, v.shape[1]), lambda i: (0, 0)) for v in vecs],
        out_specs=out_specs, out_shape=out_shape,
        compiler_params=_params(("arbitrary" if n_vec_out else "parallel",)),
    )(*([a] * n_a), b, *sec_args, *rows, *vecs)
    split = n_pro + len(row_outs) + n_t
    return list(res[:split]), list(res[split:])


def _pro_rms(a, vecs):
    return (a * _rms_scale(a) * vecs[0]).astype(BF16)


def _epi_plain(r, rows, vecs):
    return [r], []


def _epi_add_rms(r, rows, vecs):
    xn = r + rows[0]
    return [xn, xn * _rms_scale(xn) * vecs[0]], []


def _epi_rms_bwd(r, rows, vecs):
    dv, dg_rows = _rms_bwd_rows(rows[0], vecs[0], r)
    return [dv + rows[1]], [jnp.sum(dg_rows, axis=0, keepdims=True)]


def _mix_pre(z, conv_w8, gq, gkv, cos, sin_a, sin_b, *, ts=256):
    s = z.shape[0]
    n = s // ts
    cw = CONV_WIDTH

    def body(z_ref, xcp, xcn, cgp, cgn, w_ref, gq_ref, gkv_ref, cos_ref, sa_ref, sb_ref,
             yc_ref, qn_ref, kvn_ref, kr_ref):
        i = pl.program_id(0)
        xc = z_ref[:, 0:cw]
        bg = z_ref[:, cw:2 * cw]
        cg = z_ref[:, 2 * cw:3 * cw]
        m = cg * xc
        m_prev = jnp.where(i > 0, xcp[7:8, :] * cgp[7:8, :], 0.0)
        m_next = jnp.where(i < n - 1, xcn[0:1, :] * cgn[0:1, :], 0.0)
        cm = _shift_down(m, m_prev) * w_ref[0:1, :] + m * w_ref[1:2, :] + _shift_up(m, m_next) * w_ref[2:3, :]
        yc_ref[...] = (bg * cm).astype(BF16)
        ql = z_ref[:, 3 * cw:3 * cw + Q_LORA]
        qn_ref[...] = (ql * _rms_scale(ql) * gq_ref[...]).astype(BF16)
        kvl = z_ref[:, 3 * cw + Q_LORA:3 * cw + Q_LORA + KV_LORA]
        kvn_ref[...] = (kvl * _rms_scale(kvl) * gkv_ref[...]).astype(BF16)
        kr_ref[...] = _rope(z_ref[:, D_IN_PAD - HEAD_PAD:D_IN_PAD], cos_ref[...], sa_ref[...], sb_ref[...])

    xcp, xcn = _halo_specs(ts, s, cw, 0)
    cgp, cgn = _halo_specs(ts, s, cw, 2)
    tab = pl.BlockSpec((ts, HEAD_PAD), lambda i: (i, 0))
    return pl.pallas_call(
        body, name="mix_pre", grid=(n,),
        in_specs=[pl.BlockSpec((ts, D_IN_PAD), lambda i: (i, 0)), xcp, xcn, cgp, cgn,
                  pl.BlockSpec((8, cw), lambda i: (0, 0)), pl.BlockSpec((1, Q_LORA), lambda i: (0, 0)),
                  pl.BlockSpec((1, KV_LORA), lambda i: (0, 0)), tab, tab, tab],
        out_specs=[pl.BlockSpec((ts, cw), lambda i: (i, 0)), pl.BlockSpec((ts, Q_LORA), lambda i: (i, 0)),
                   pl.BlockSpec((ts, KV_LORA), lambda i: (i, 0)), tab],
        out_shape=[jax.ShapeDtypeStruct((s, cw), BF16), jax.ShapeDtypeStruct((s, Q_LORA), BF16),
                   jax.ShapeDtypeStruct((s, KV_LORA), BF16), jax.ShapeDtypeStruct((s, HEAD_PAD), F32)],
        compiler_params=_params(("parallel",)),
    )(z, z, z, z, z, conv_w8, gq, gkv, cos, sin_a, sin_b)


def _mix_bwd(z, dyc, dqn, dkvn, dkr, conv_w8, gq, gkv, cos, sin_a, sin_b, *, ts=256):
    s = z.shape[0]
    n = s // ts
    cw = CONV_WIDTH

    def body(z_ref, xcp, xcn, bgp, bgn, cgp, cgn, dyc_ref, dycp, dycn, dqn_ref, dkvn_ref, dkr_ref,
             w_ref, gq_ref, gkv_ref, cos_ref, sa_ref, sb_ref,
             dz_ref, dw0_ref, dw1_ref, dw2_ref, dgq_ref, dgkv_ref):
        i = pl.program_id(0)

        @pl.when(i == 0)
        def _():
            for r in (dw0_ref, dw1_ref, dw2_ref, dgq_ref, dgkv_ref):
                r[...] = jnp.zeros_like(r)

        xc = z_ref[:, 0:cw]
        bg = z_ref[:, cw:2 * cw]
        cg = z_ref[:, 2 * cw:3 * cw]
        w0, w1, w2 = w_ref[0:1, :], w_ref[1:2, :], w_ref[2:3, :]
        m = cg * xc
        m_dn = _shift_down(m, jnp.where(i > 0, xcp[7:8, :] * cgp[7:8, :], 0.0))
        m_up = _shift_up(m, jnp.where(i < n - 1, xcn[0:1, :] * cgn[0:1, :], 0.0))
        cm = m_dn * w0 + m * w1 + m_up * w2
        dyc_v = dyc_ref[...]
        dcm = dyc_v * bg
        dcm_dn = _shift_down(dcm, jnp.where(i > 0, dycp[7:8, :] * bgp[7:8, :], 0.0))
        dcm_up = _shift_up(dcm, jnp.where(i < n - 1, dycn[0:1, :] * bgn[0:1, :], 0.0))
        dm = dcm_up * w0 + dcm * w1 + dcm_dn * w2
        dz_ref[:, 0:cw] = (dm * cg).astype(BF16)
        dz_ref[:, cw:2 * cw] = (dyc_v * cm).astype(BF16)
        dz_ref[:, 2 * cw:3 * cw] = (dm * xc).astype(BF16)
        dw0_ref[...] += jnp.sum(dcm * m_dn, axis=0, keepdims=True)
        dw1_ref[...] += jnp.sum(dcm * m, axis=0, keepdims=True)
        dw2_ref[...] += jnp.sum(dcm * m_up, axis=0, keepdims=True)

        dql, dgq_rows = _rms_bwd_rows(z_ref[:, 3 * cw:3 * cw + Q_LORA], gq_ref[...], dqn_ref[...])
        dz_ref[:, 3 * cw:3 * cw + Q_LORA] = dql.astype(BF16)
        dgq_ref[...] += jnp.sum(dgq_rows, axis=0, keepdims=True)
        dkvl, dgkv_rows = _rms_bwd_rows(z_ref[:, 3 * cw + Q_LORA:3 * cw + Q_LORA + KV_LORA], gkv_ref[...],
                                        dkvn_ref[...])
        dz_ref[:, 3 * cw + Q_LORA:3 * cw + Q_LORA + KV_LORA] = dkvl.astype(BF16)
        dgkv_ref[...] += jnp.sum(dgkv_rows, axis=0, keepdims=True)

        lane = lax.broadcasted_iota(jnp.int32, (ts, HEAD_PAD), 1)
        rope_lane = (lane >= QK_NOPE) & (lane < QK_NOPE + QK_ROPE)
        dk = _rope_bwd(dkr_ref[...], cos_ref[...], sa_ref[...], sb_ref[...])
        dz_ref[:, D_IN_PAD - HEAD_PAD:D_IN_PAD] = jnp.where(rope_lane, dk, 0.0).astype(BF16)

    xcp, xcn = _halo_specs(ts, s, cw, 0)
    bgp, bgn = _halo_specs(ts, s, cw, 1)
    cgp, cgn = _halo_specs(ts, s, cw, 2)
    dycp, dycn = _halo_specs(ts, s, cw, 0)
    tab = pl.BlockSpec((ts, HEAD_PAD), lambda i: (i, 0))

    def vec(width):
        return pl.BlockSpec((1, width), lambda i: (0, 0))

    outs = pl.pallas_call(
        body, name="mix_bwd", grid=(n,),
        in_specs=[pl.BlockSpec((ts, D_IN_PAD), lambda i: (i, 0)), xcp, xcn, bgp, bgn, cgp, cgn,
                  pl.BlockSpec((ts, cw), lambda i: (i, 0)), dycp, dycn,
                  pl.BlockSpec((ts, Q_LORA), lambda i: (i, 0)), pl.BlockSpec((ts, KV_LORA), lambda i: (i, 0)), tab,
                  pl.BlockSpec((8, cw), lambda i: (0, 0)), vec(Q_LORA), vec(KV_LORA), tab, tab, tab],
        out_specs=[pl.BlockSpec((ts, D_IN_PAD), lambda i: (i, 0)), vec(cw), vec(cw), vec(cw), vec(Q_LORA),
                   vec(KV_LORA)],
        out_shape=[jax.ShapeDtypeStruct((s, D_IN_PAD), BF16)] + [jax.ShapeDtypeStruct((1, cw), F32)] * 3
        + [jax.ShapeDtypeStruct((1, Q_LORA), F32), jax.ShapeDtypeStruct((1, KV_LORA), F32)],
        compiler_params=_params(("arbitrary",)),
    )(z, z, z, z, z, z, z, dyc, dyc, dyc, dqn, dkvn, dkr, conv_w8, gq, gkv, cos, sin_a, sin_b)
    dz, dw0, dw1, dw2, dgq, dgkv = outs
    return dz, jnp.concatenate([dw0, dw1, dw2], axis=0), dgq, dgkv


def _qkv_proj(qn, kvn, kr, w_uq_p, w_kv_p, cos, sin_a, sin_b, *, ts=512):
    s = qn.shape[0]

    def body(qn_ref, kvn_ref, kr_ref, wq_ref, wkv_ref, cos_ref, sa_ref, sb_ref, q_ref, k_ref, v_ref, qt_ref):
        cos_v, sa, sb = cos_ref[...], sa_ref[...], sb_ref[...]
        q = jnp.dot(qn_ref[...], wq_ref[...], preferred_element_type=F32)
        kv = jnp.dot(kvn_ref[...], wkv_ref[...], preferred_element_type=F32)
        kr_v = kr_ref[...]
        lane = lax.broadcasted_iota(jnp.int32, (1, HEAD_PAD), 1)
        ones_lane = (lane == ONES_LANE).astype(F32)
        for h in range(N_HEADS):
            blk = slice(h * HEAD_PAD, (h + 1) * HEAD_PAD)
            q_h = _rope(q[:, blk], cos_v, sa, sb) * SM_SCALE
            q_ref[:, blk] = q_h.astype(BF16)
            qt_ref[blk, :] = q_h.T.astype(BF16)
            k_ref[:, blk] = (kv[:, blk] + kr_v).astype(BF16)
            v_ref[:, blk] = (kv[:, D_ATT + h * HEAD_PAD:D_ATT + (h + 1) * HEAD_PAD] + ones_lane).astype(BF16)

    tab = pl.BlockSpec((ts, HEAD_PAD), lambda i: (i, 0))
    wide = pl.BlockSpec((ts, D_ATT), lambda i: (i, 0))
    return pl.pallas_call(
        body, name="qkv_proj", grid=(s // ts,),
        in_specs=[pl.BlockSpec((ts, Q_LORA), lambda i: (i, 0)), pl.BlockSpec((ts, KV_LORA), lambda i: (i, 0)), tab,
                  pl.BlockSpec((Q_LORA, D_ATT), lambda i: (0, 0)), pl.BlockSpec((KV_LORA, 2 * D_ATT), lambda i: (0, 0)),
                  tab, tab, tab],
        out_specs=[wide, wide, wide, pl.BlockSpec((D_ATT, ts), lambda i: (0, i))],
        out_shape=[jax.ShapeDtypeStruct((s, D_ATT), BF16)] * 3 + [jax.ShapeDtypeStruct((D_ATT, s), BF16)],
        compiler_params=_params(("parallel",)),
    )(qn, kvn, kr, w_uq_p, w_kv_p, cos, sin_a, sin_b)


def _qk_bwd(dq, dk, cos, sin_a, sin_b, *, ts=256):
    s = dq.shape[0]

    def body(dq_ref, dk_ref, cos_ref, sa_ref, sb_ref, dqp_ref, dkr_ref):
        cos_v, sa, sb = cos_ref[...], sa_ref[...], sb_ref[...]
        tot = jnp.zeros((ts, HEAD_PAD), F32)
        for h in range(N_HEADS):
            blk = slice(h * HEAD_PAD, (h + 1) * HEAD_PAD)
            dqp_ref[:, blk] = _rope_bwd(dq_ref[:, blk], cos_v, sa, sb).astype(BF16)
            tot = tot + dk_ref[:, blk]
        dkr_ref[...] = tot

    tab = pl.BlockSpec((ts, HEAD_PAD), lambda i: (i, 0))
    wide = pl.BlockSpec((ts, D_ATT), lambda i: (i, 0))
    return pl.pallas_call(
        body, name="qk_bwd", grid=(s // ts,),
        in_specs=[wide, wide, tab, tab, tab], out_specs=[wide, tab],
        out_shape=[jax.ShapeDtypeStruct((s, D_ATT), BF16), jax.ShapeDtypeStruct((s, HEAD_PAD), F32)],
        compiler_params=_params(("parallel",)),
    )(dq, dk, cos, sin_a, sin_b)


_NT = (((1,), (1,)), ((), ()))


def _flash_fwd(q, k, v, *, tq=1024, tk=1024, per_trip=8, q_per_step=2):
    s = q.shape[0]
    tq, tk = min(tq, s), min(tk, s)
    nk = s // tk
    per_trip = min(per_trip, nk)
    assert nk % per_trip == 0
    q_per_step = min(q_per_step, s // tq)

    def body(q_ref, k_ref, v_ref, o_ref, lse_ref):
        for blk in range(q_per_step):
            rows = pl.ds(blk * tq, tq)
            one_block(q_ref.at[rows, :], k_ref, v_ref, o_ref.at[rows, :], lse_ref.at[rows, :])

    def one_block(q_ref, k_ref, v_ref, o_ref, lse_ref):
        qv = q_ref[...]

        def step(j, carry):
            m, acc = carry
            rows = pl.ds(pl.multiple_of(j * tk, tk), tk)
            sc = lax.dot_general(qv, k_ref[rows, :], _NT, preferred_element_type=F32)
            m_new = jnp.maximum(m, jnp.max(sc, axis=1, keepdims=True))
            p = jnp.exp(sc - m_new).astype(BF16)
            acc = jnp.exp(m - m_new) * acc + jnp.dot(p, v_ref[rows, :], preferred_element_type=F32)
            return m_new, acc

        def trip(t, carry):
            for c in range(per_trip):
                carry = step(per_trip * t + c, carry)
            return carry

        init = (jnp.full((tq, 1), -jnp.inf, F32), jnp.zeros((tq, HEAD_PAD), F32))
        m, acc = lax.fori_loop(0, nk // per_trip, trip, init)
        l = acc[:, ONES_LANE:ONES_LANE + 1]
        o_ref[...] = (acc / l).astype(BF16)
        lse_ref[...] = m + jnp.log(l)

    head = pl.BlockSpec((s, HEAD_PAD), lambda h, i: (0, h))
    tq_step = q_per_step * tq
    return pl.pallas_call(
        body, name="flash_fwd", grid=(N_HEADS, s // tq_step),
        in_specs=[pl.BlockSpec((tq_step, HEAD_PAD), lambda h, i: (i, h)), head, head],
        out_specs=[pl.BlockSpec((tq_step, HEAD_PAD), lambda h, i: (i, h)),
                   pl.BlockSpec((None, tq_step, 1), lambda h, i: (h, i, 0))],
        out_shape=[jax.ShapeDtypeStruct((s, D_ATT), BF16), jax.ShapeDtypeStruct((N_HEADS, s, 1), F32)],
        compiler_params=_params(("parallel", "parallel")),
    )(q, k, v)


def _attn_delta(do, o, *, ts=512):
    s = do.shape[0]

    def body(do_ref, o_ref, dl_ref):
        for h in range(N_HEADS):
            blk = slice(h * HEAD_PAD, (h + 1) * HEAD_PAD)
            dl_ref[h] = jnp.sum(do_ref[:, blk].astype(F32) * o_ref[:, blk].astype(F32), axis=1, keepdims=True)

    wide = pl.BlockSpec((ts, D_ATT), lambda i: (i, 0))
    return pl.pallas_call(
        body, name="attn_delta", grid=(s // ts,), in_specs=[wide, wide],
        out_specs=pl.BlockSpec((N_HEADS, ts, 1), lambda i: (0, i, 0)),
        out_shape=jax.ShapeDtypeStruct((N_HEADS, s, 1), F32),
        compiler_params=_params(("parallel",)),
    )(do, o)


def _flash_bwd(q, qt, k, v, do, dot, lse, delta, after, *, tq=1024, tk=512, per_trip=8, kv_per_step=2):
    s = q.shape[0]
    tq, tk = min(tq, s), min(tk, s)
    nq = s // tq
    per_trip = min(per_trip, nq)
    assert nq % per_trip == 0
    kv_per_step = min(kv_per_step, s // tk)

    def body(q_ref, qt_ref, do_ref, dot_ref, lse_ref, dl_ref, k_ref, v_ref, after_ref, dq_ref, dk_ref, dv_ref):
        j = pl.program_id(1)

        @pl.when(j == 0)
        def _():
            dq_ref[...] = jnp.zeros_like(dq_ref)

        for blk in range(kv_per_step):
            one_block(q_ref, qt_ref, do_ref, dot_ref, lse_ref, dl_ref, dq_ref,
                      k_ref.at[pl.ds(blk * tk, tk), :], v_ref.at[pl.ds(blk * tk, tk), :],
                      dk_ref.at[pl.ds(blk * tk, tk), :], dv_ref.at[pl.ds(blk * tk, tk), :])

        @pl.when(j == pl.num_programs(1) - 1)
        def _():
            dq_ref[...] *= SM_SCALE

    def one_block(q_ref, qt_ref, do_ref, dot_ref, lse_ref, dl_ref, dq_ref, k_ref, v_ref, dk_ref, dv_ref):
        kv, vv = k_ref[...], v_ref[...]

        def chunk(i, dk_t, dv_t):
            at = pl.multiple_of(i * tq, tq)
            rows = pl.ds(at, tq)
            sc = lax.dot_general(q_ref[rows, :], kv, _NT, preferred_element_type=F32)
            p = jnp.exp(sc - lse_ref[rows, :])
            dp = lax.dot_general(do_ref[rows, :], vv, _NT, preferred_element_type=F32)
            ds = (p * (dp - dl_ref[rows, :])).astype(BF16)
            dv_t = dv_t + jnp.dot(dot_ref[:, rows], p.astype(BF16), preferred_element_type=F32)
            dk_t = dk_t + jnp.dot(qt_ref[:, rows], ds, preferred_element_type=F32)
            dq_ref[rows, :] += jnp.dot(ds, kv, preferred_element_type=F32)
            return dk_t, dv_t

        def step(i, carry):
            for c in range(per_trip):
                carry = chunk(per_trip * i + c, *carry)
            return carry

        zero = jnp.zeros((HEAD_PAD, tk), F32)
        dk_t, dv_t = lax.fori_loop(0, nq // per_trip, step, (zero, zero))
        dk_ref[...] = dk_t.T
        dv_ref[...] = dv_t.T

    head = pl.BlockSpec((s, HEAD_PAD), lambda h, j: (0, h))
    head_t = pl.BlockSpec((HEAD_PAD, s), lambda h, j: (h, 0))
    stat = pl.BlockSpec((None, s, 1), lambda h, j: (h, 0, 0))
    blk = pl.BlockSpec((kv_per_step * tk, HEAD_PAD), lambda h, j: (j, h))
    return pl.pallas_call(
        body, name="flash_bwd", grid=(N_HEADS, s // (kv_per_step * tk)),
        in_specs=[head, head_t, head, head_t, stat, stat, blk, blk, ANY],
        out_specs=[head, blk, blk],
        out_shape=[jax.ShapeDtypeStruct((s, D_ATT), F32)] * 3,
        compiler_params=_params(("parallel", "arbitrary")),
    )(q, qt, do, dot, lse, delta, k, v, after)


FFN_TC = 256
FFN_TG = 1408


FFN_HALO_BF16 = 16
FFN_HALO_F32 = 8


def _row_halo_specs(ts, s, halo, width):
    nb = ts // halo
    last = s // halo - 1
    prev = pl.BlockSpec((halo, width), lambda i, j: (jnp.maximum(i * nb - 1, 0), 0))
    nxt = pl.BlockSpec((halo, width), lambda i, j: (jnp.minimum((i + 1) * nb, last), 0))
    return prev, nxt


def _ext_rows(prev, main, nxt, first, last):
    return jnp.concatenate([jnp.where(first, jnp.zeros_like(prev), prev), main,
                            jnp.where(last, jnp.zeros_like(nxt), nxt)], axis=0)


def _ext_conv(a, w):
    a_dn = pltpu.roll(a, 1, 0)
    a_up = pltpu.roll(a, a.shape[0] - 1, 0)
    return a_dn * w[0:1, :] + a * w[1:2, :] + a_up * w[2:3, :], a_dn, a_up


def _ffn_pieces(tg):
    return [(off, min(FFN_TC, tg - off)) for off in range(0, tg, FFN_TC)]


def _ffn_fwd(hf, w_up, w, b, *, ts=1024, tg=FFN_TG):
    s = hf.shape[0]
    n, ng, halo = s // ts, D_FF // tg, FFN_HALO_BF16

    def body(h_ref, hp_ref, hn_ref, wg_ref, wu_ref, cw_ref, cb_ref, a_ref, act_ref):
        i = pl.program_id(0)
        ext = _ext_rows(hp_ref[...], h_ref[...], hn_ref[...], i == 0, i == n - 1)
        for off, width in _ffn_pieces(tg):
            cols = slice(off, off + width)
            gate_up = []
            for half, w_ref in enumerate((wg_ref, wu_ref)):
                a_ext = jnp.dot(ext, w_ref[:, cols], preferred_element_type=F32)
                a_ref[half, :, cols] = a_ext[halo:halo + ts]
                conv = _ext_conv(a_ext, cw_ref[half, :, cols])[0]
                gate_up.append(conv[halo:halo + ts] + cb_ref[half, :, cols])
            g, u = gate_up
            act_ref[:, cols] = (g * _sigmoid(g) * u).astype(BF16)

    prev, nxt = _row_halo_specs(ts, s, halo, D_MODEL)
    return pl.pallas_call(
        body, name="ffn_fwd", grid=(n, ng),
        in_specs=[pl.BlockSpec((ts, D_MODEL), lambda i, j: (i, 0)), prev, nxt,
                  pl.BlockSpec((D_MODEL, tg), lambda i, j: (0, j)), pl.BlockSpec((D_MODEL, tg), lambda i, j: (0, j + ng)),
                  pl.BlockSpec((2, 8, tg), lambda i, j: (0, 0, j)), pl.BlockSpec((2, 1, tg), lambda i, j: (0, 0, j))],
        out_specs=[pl.BlockSpec((2, ts, tg), lambda i, j: (0, i, j)), pl.BlockSpec((ts, tg), lambda i, j: (i, j))],
        out_shape=[jax.ShapeDtypeStruct((2, s, D_FF), F32), jax.ShapeDtypeStruct((s, D_FF), BF16)],
        compiler_params=_params(("parallel", "parallel")),
    )(hf, hf, hf, w_up, w_up, w, b)


def _ffn_bwd(dx2, w_down, a_pre, w, b, *, ts=1024, tg=FFN_TG):
    s = dx2.shape[0]
    n, ng, halo = s // ts, D_FF // tg, FFN_HALO_F32
    main = slice(halo, halo + ts)

    def body(dx_ref, dxp_ref, dxn_ref, wd_ref, a_ref, ap_ref, an_ref, cw_ref, cb_ref, o_ref, dw_ref, db_ref):
        i, j = pl.program_id(0), pl.program_id(1)
        first, last = i == 0, i == n - 1

        @pl.when(first & (j == 0))
        def _():
            dw_ref[...] = jnp.zeros_like(dw_ref)
            db_ref[...] = jnp.zeros_like(db_ref)

        dx_ext = _ext_rows(dxp_ref[...], dx_ref[...], dxn_ref[...], first, last).astype(BF16)
        for off, width in _ffn_pieces(tg):
            cols = slice(off, off + width)
            dact = lax.dot_general(dx_ext, wd_ref[cols, :], _NT, preferred_element_type=F32)
            halves = []
            for half in range(2):
                a_ext = _ext_rows(ap_ref[half, :, cols], a_ref[half, :, cols], an_ref[half, :, cols], first, last)
                conv, a_dn, a_up = _ext_conv(a_ext, cw_ref[half, :, cols])
                halves.append((conv + cb_ref[half, :, cols], a_dn, a_ext, a_up))
            g, u = halves[0][0], halves[1][0]
            sg = _sigmoid(g)
            grads = (dact * u * (sg * (1.0 + g * (1.0 - sg))), dact * (g * sg))
            for half in range(2):
                d = grads[half]
                _, a_dn, a_ext, a_up = halves[half]
                wv = cw_ref[half, :, cols]
                d_pre = (pltpu.roll(d, d.shape[0] - 1, 0) * wv[0:1, :] + d * wv[1:2, :]
                         + pltpu.roll(d, 1, 0) * wv[2:3, :])
                o_ref[half, :, cols] = d_pre[main].astype(BF16)
                dm = d[main]
                dw_ref[j, half, 0:1, cols] += jnp.sum(dm * a_dn[main], axis=0, keepdims=True)
                dw_ref[j, half, 1:2, cols] += jnp.sum(dm * a_ext[main], axis=0, keepdims=True)
                dw_ref[j, half, 2:3, cols] += jnp.sum(dm * a_up[main], axis=0, keepdims=True)
                db_ref[j, half, :, cols] += jnp.sum(dm, axis=0, keepdims=True)

    dxp, dxn = _row_halo_specs(ts, s, halo, D_MODEL)
    nb, lastb = ts // halo, s // halo - 1
    a_main = pl.BlockSpec((2, ts, tg), lambda i, j: (0, i, j))
    a_prev = pl.BlockSpec((2, halo, tg), lambda i, j: (0, jnp.maximum(i * nb - 1, 0), j))
    a_next = pl.BlockSpec((2, halo, tg), lambda i, j: (0, jnp.minimum((i + 1) * nb, lastb), j))
    da_pre, dw, db = pl.pallas_call(
        body, name="ffn_bwd", grid=(n, ng),
        in_specs=[pl.BlockSpec((ts, D_MODEL), lambda i, j: (i, 0)), dxp, dxn,
                  pl.BlockSpec((tg, D_MODEL), lambda i, j: (j, 0)), a_main, a_prev, a_next,
                  pl.BlockSpec((2, 8, tg), lambda i, j: (0, 0, j)), pl.BlockSpec((2, 1, tg), lambda i, j: (0, 0, j))],
        out_specs=[a_main, pl.BlockSpec((ng, 2, 8, tg), lambda i, j: (0, 0, 0, 0)),
                   pl.BlockSpec((ng, 2, 1, tg), lambda i, j: (0, 0, 0, 0))],
        out_shape=[jax.ShapeDtypeStruct((2, s, D_FF), BF16), jax.ShapeDtypeStruct((ng, 2, 8, tg), F32),
                   jax.ShapeDtypeStruct((ng, 2, 1, tg), F32)],
        compiler_params=_params(("arbitrary", "arbitrary")),
    )(dx2, dx2, dx2, w_down, a_pre, a_pre, a_pre, w, b)
    return (da_pre, dw.transpose(1, 2, 0, 3).reshape(2, 8, D_FF), db.transpose(1, 2, 0, 3).reshape(2, 1, D_FF))


def _ple_final(x2, n3, p, target, gf, w_pg, w_pp, *, ts=256):
    s, d = x2.shape
    dp = p.shape[1]

    def body(x2_ref, n3_ref, p_ref, t_ref, gf_ref, wg_ref, wp_ref, loss_ref, dx3_ref, dgl_ref, dpp_ref, dgf_ref):
        @pl.when(pl.program_id(0) == 0)
        def _():
            loss_ref[...] = jnp.zeros_like(loss_ref)
            dgf_ref[...] = jnp.zeros_like(dgf_ref)

        gate = _sigmoid(jnp.dot(n3_ref[...], wg_ref[...], preferred_element_type=F32))
        ppv = jnp.dot(p_ref[...].astype(BF16), wp_ref[...], preferred_element_type=F32)
        x3 = x2_ref[...] + gate * ppv
        gfv = gf_ref[...]
        err = x3 * _rms_scale(x3) * gfv - t_ref[...]
        loss_ref[...] += 0.5 * jnp.sum(jnp.mean(err * err, axis=-1, keepdims=True), axis=0, keepdims=True)
        dx3, dgf_rows = _rms_bwd_rows(x3, gfv, err * (1.0 / d))
        dgf_ref[...] += jnp.sum(dgf_rows, axis=0, keepdims=True)
        dx3_ref[...] = dx3
        dgl_ref[...] = (dx3 * ppv * gate * (1.0 - gate)).astype(BF16)
        dpp_ref[...] = (dx3 * gate).astype(BF16)

    row = pl.BlockSpec((ts, d), lambda i: (i, 0))
    vec = pl.BlockSpec((1, d), lambda i: (0, 0))
    return pl.pallas_call(
        body, name="ple_final", grid=(s // ts,),
        in_specs=[row, row, pl.BlockSpec((ts, dp), lambda i: (i, 0)), row, vec,
                  pl.BlockSpec((d, d), lambda i: (0, 0)), pl.BlockSpec((dp, d), lambda i: (0, 0))],
        out_specs=[pl.BlockSpec((1, 128), lambda i: (0, 0)), row, row, row, vec],
        out_shape=[jax.ShapeDtypeStruct((1, 128), F32), jax.ShapeDtypeStruct((s, d), F32),
                   jax.ShapeDtypeStruct((s, d), BF16), jax.ShapeDtypeStruct((s, d), BF16),
                   jax.ShapeDtypeStruct((1, d), F32)],
        compiler_params=_params(("arbitrary",)),
    )(x2, n3, p, target, gf, w_pg, w_pp)


def _row_tile(rows, cols, n_arrays, budget=12 << 20):
    best = None
    for t in range(8, rows + 1, 8):
        if rows % t == 0 and t * cols * 4 * n_arrays <= budget:
            best = t
    return rows if best is None else best


def _sum_slots(a, *, name):
    g, r, c = a.shape
    tr = _row_tile(r, c, g + 1)

    def body(*refs):
        tot = refs[0][...]
        for ref in refs[1:g]:
            tot = tot + ref[...]
        refs[g][...] = tot

    specs = [pl.BlockSpec((None, tr, c), functools.partial(lambda i, slot: (slot, i, 0), slot=k)) for k in range(g)]
    return pl.pallas_call(
        body, name=name, grid=(r // tr,), in_specs=specs, out_specs=pl.BlockSpec((tr, c), lambda i: (i, 0)),
        out_shape=jax.ShapeDtypeStruct((r, c), a.dtype), compiler_params=_params(("parallel",)),
    )(*([a] * g))


def _adamw_refs(w_ref, g_ref, m_ref, v_ref, d_ref, mo_ref, vo_ref):
    gv = g_ref[...]
    mn = ADAM_B1 * m_ref[...] + (1.0 - ADAM_B1) * gv
    vn = ADAM_B2 * v_ref[...] + (1.0 - ADAM_B2) * (gv * gv)
    m_hat = mn / (1.0 - ADAM_B1 ** ADAM_STEP)
    v_hat = vn / (1.0 - ADAM_B2 ** ADAM_STEP)
    d_ref[...] = -ADAM_LR * (m_hat / (jnp.sqrt(v_hat) + ADAM_EPS) + ADAM_WD * w_ref[...])
    mo_ref[...] = mn
    vo_ref[...] = vn


def _adamw_many(ws, gs, ms, vs, *, name):
    n = len(ws)

    def body(*refs):
        ins, outs = refs[:4 * n], refs[4 * n:]
        for a in range(n):
            _adamw_refs(ins[a], ins[n + a], ins[2 * n + a], ins[3 * n + a], outs[3 * a], outs[3 * a + 1],
                        outs[3 * a + 2])

    vm = pl.BlockSpec(memory_space=pltpu.VMEM)
    res = pl.pallas_call(
        body, name=name, in_specs=[vm] * (4 * n), out_specs=[vm] * (3 * n),
        out_shape=[jax.ShapeDtypeStruct(a.shape, F32) for a in ws for _ in range(3)],
    )(*ws, *gs, *ms, *vs)
    return [tuple(res[3 * a:3 * a + 3]) for a in range(n)]


def _adamw(w, g, m, v, *, name):
    r, c = w.shape
    tr = _row_tile(r, c, 7)
    body = _adamw_refs

    blk = pl.BlockSpec((tr, c), lambda i: (i, 0))
    return pl.pallas_call(
        body, name=name, grid=(r // tr,), in_specs=[blk] * 4, out_specs=[blk] * 3,
        out_shape=[jax.ShapeDtypeStruct((r, c), F32)] * 3, compiler_params=_params(("parallel",)),
    )(w, g, m, v)


def _position():
    x, y, c = lax.axis_index("x"), lax.axis_index("y"), lax.axis_index("c")
    return x, y, c


def _other_chips(x, y):
    return [(1 - x, y), (x, 1 - y), (1 - x, 1 - y)]


def _stage_in(srcs, stage, sems):
    cps = [pltpu.make_async_copy(src, stage[a], sems.at[a]) for a, src in enumerate(srcs)]
    for cp in cps:
        cp.start()
    return cps


def _stage_out(staged, stage, dsts, sems):
    cps = []
    for a, dst in enumerate(dsts):
        staged[a].wait()
        cp = pltpu.make_async_copy(stage[a], dst, sems.at[a])
        cp.start()
        cps.append(cp)
    return cps


def _send_other_halves(grads, *, tag):
    n = len(grads)

    def body(*refs):
        ins, sib = refs[:n], refs[n:2 * n]
        send_sems, recv_sems = refs[2 * n:]
        x, y, c = _position()
        remote = []
        for a in range(n):
            half = ins[a].shape[1] // 2
            give = ins[a].at[:, pl.ds(pl.multiple_of((1 - c) * half, 8), half), :]
            rc = pltpu.make_async_remote_copy(
                src_ref=give, dst_ref=sib[a], send_sem=send_sems.at[a], recv_sem=recv_sems.at[a],
                device_id=(x, y, 1 - c), device_id_type=MESH)
            rc.start()
            remote.append(rc)
        for rc in remote:
            rc.wait_recv()
        for rc in remote:
            rc.wait_send()

    return pl.pallas_call(
        body, name="send_other_halves_" + tag, in_specs=[ANY] * n, out_specs=[ANY] * n,
        out_shape=[jax.ShapeDtypeStruct((g.shape[0], g.shape[1] // 2, g.shape[2]), g.dtype) for g in grads],
        scratch_shapes=[pltpu.SemaphoreType.DMA((n,)), pltpu.SemaphoreType.DMA((n,))],
        compiler_params=pltpu.CompilerParams(has_side_effects=True),
    )(*grads)


def _add_own_half(g4, sib, core, *, name):
    g, a2, c = sib.shape
    tr = _row_tile(a2, c, 4)

    def body(core_ref, a_ref, b_ref, o_ref, o16_ref):
        tot = a_ref[...] + b_ref[...]
        o_ref[...] = tot
        o16_ref[...] = tot.astype(BF16)

    blk = pl.BlockSpec((None, tr, c), lambda i, j, core_ref: (i, j, 0))
    return pl.pallas_call(
        body, name=name,
        grid_spec=pltpu.PrefetchScalarGridSpec(
            num_scalar_prefetch=1, grid=(g, a2 // tr),
            in_specs=[pl.BlockSpec((None, None, tr, c), lambda i, j, core_ref: (i, core_ref[0], j, 0)), blk],
            out_specs=[blk, blk]),
        out_shape=[jax.ShapeDtypeStruct(sib.shape, F32), jax.ShapeDtypeStruct(sib.shape, BF16)],
        compiler_params=_params(("parallel", "parallel")),
    )(core, g4.reshape(g, 2, a2, c), sib)


def _sum_chips(landed, own, chip, *, name):
    g, r, c = landed.shape
    tr = _row_tile(r, c, 5)

    def body(chip_ref, *refs):
        me = chip_ref[0]
        own_v = refs[g][...]
        tot = None
        for slot in range(g):
            term = jnp.where(me == slot, own_v, refs[slot][...].astype(F32))
            tot = term if tot is None else tot + term
        refs[g + 1][...] = tot

    def landed_spec(slot):
        return pl.BlockSpec((None, tr, c),
                            lambda i, chip_ref: (jnp.where(chip_ref[0] == slot, (slot + 1) % g, slot), i, 0))

    return pl.pallas_call(
        body, name=name,
        grid_spec=pltpu.PrefetchScalarGridSpec(
            num_scalar_prefetch=1, grid=(r // tr,),
            in_specs=[landed_spec(k) for k in range(g)]
            + [pl.BlockSpec((None, tr, c), lambda i, chip_ref: (chip_ref[0], i, 0))],
            out_specs=pl.BlockSpec((tr, c), lambda i, chip_ref: (i, 0))),
        out_shape=jax.ShapeDtypeStruct((r, c), F32), compiler_params=_params(("parallel",)),
    )(chip, *([landed] * g), own)


def _join_halves(halves):
    n = len(halves)

    def body(*refs):
        ins, outs, stage = refs[:n], refs[n:2 * n], refs[2 * n:3 * n]
        send_sems, recv_sems, in_sems, out_sems = refs[3 * n:]
        x, y, c = _position()
        remote = []
        staged = _stage_in(ins, stage, in_sems)
        for a in range(n):
            rc = pltpu.make_async_remote_copy(
                src_ref=ins[a], dst_ref=outs[a].at[c], send_sem=send_sems.at[a], recv_sem=recv_sems.at[a],
                device_id=(x, y, 1 - c), device_id_type=MESH)
            rc.start()
            remote.append(rc)
        local = _stage_out(staged, stage, [o.at[c] for o in outs], out_sems)
        for a in range(n):
            pltpu.make_async_remote_copy(
                src_ref=ins[a], dst_ref=outs[a].at[1 - c], send_sem=send_sems.at[a], recv_sem=recv_sems.at[a],
                device_id=(x, y, 1 - c), device_id_type=MESH).wait_recv()
        for rc in remote:
            rc.wait_send()
        for cp in local:
            cp.wait()

    return pl.pallas_call(
        body, name="join_halves", in_specs=[ANY] * n, out_specs=[ANY] * n,
        out_shape=[jax.ShapeDtypeStruct((2,) + h.shape, h.dtype) for h in halves],
        scratch_shapes=[pltpu.VMEM(h.shape, h.dtype) for h in halves]
        + [pltpu.SemaphoreType.DMA((n,)), pltpu.SemaphoreType.DMA((n,)), pltpu.SemaphoreType.DMA((n,)),
           pltpu.SemaphoreType.DMA((n,))],
        compiler_params=pltpu.CompilerParams(has_side_effects=True),
    )(*halves)


_HBM = pl.BlockSpec(memory_space=pltpu.HBM)
_SEM = pl.BlockSpec(memory_space=pltpu.SEMAPHORE)


def _chip_copies(srcs, lands, send_sems, recv_sems, scatter):
    x, y, c = _position()
    me = 2 * x + y
    outgoing, incoming = [], []
    for a, (src, land) in enumerate(zip(srcs, lands)):
        for k, (px, py) in enumerate(_other_chips(x, y)):
            peer = 2 * px + py
            sems = dict(send_sem=send_sems.at[3 * a + k], recv_sem=recv_sems.at[3 * a + k], device_id=(px, py, c),
                        device_id_type=MESH)
            outgoing.append(pltpu.make_async_remote_copy(
                src_ref=src.at[peer] if scatter else src, dst_ref=land.at[me], **sems))
            incoming.append(pltpu.make_async_remote_copy(
                src_ref=src.at[me] if scatter else src, dst_ref=land.at[peer], **sems))
    return outgoing, incoming


def _chips_start(srcs, *, scatter, name):
    n = len(srcs)
    lands = [lax.empty(a.shape if scatter else (N_CHIPS,) + a.shape, a.dtype) for a in srcs]

    def body(*refs):
        ins, send_sems, recv_sems, token = refs[:2 * n], refs[2 * n], refs[2 * n + 1], refs[-1]
        outgoing, _ = _chip_copies(ins[:n], ins[n:], send_sems, recv_sems, scatter)
        for cp in outgoing:
            cp.start()
        token[...] = jnp.zeros_like(token)

    bufs = list(srcs) + lands
    res = pl.pallas_call(
        body, name=name, in_specs=[_HBM] * (2 * n),
        out_specs=(_SEM, _SEM, *[_HBM] * (2 * n), pl.BlockSpec(memory_space=pltpu.VMEM)),
        out_shape=(pltpu.SemaphoreType.DMA((3 * n,)), pltpu.SemaphoreType.DMA((3 * n,)),
                   *[pltpu.HBM(a.shape, a.dtype) for a in bufs], jax.ShapeDtypeStruct((8, 128), F32)),
        input_output_aliases={i: 2 + i for i in range(2 * n)},
        compiler_params=pltpu.CompilerParams(has_side_effects=pltpu.SideEffectType.DATAFLOW_SIDE_EFFECTING),
    )(*[pltpu.with_memory_space_constraint(a, pltpu.HBM) for a in bufs])
    return res[0], res[1], list(res[2:2 + n]), list(res[2 + n:2 + 2 * n]), res[-1]


def _chips_wait(handle, after, *, scatter, name):
    send_sems, recv_sems, srcs, lands, _ = handle
    n = len(srcs)

    def body(*refs):
        ins, send_ref, recv_ref = refs[:2 * n], refs[2 * n], refs[2 * n + 1]
        outgoing, incoming = _chip_copies(ins[:n], ins[n:], send_ref, recv_ref, scatter)
        for cp in outgoing:
            cp.wait_send()
        for cp in incoming:
            cp.wait_recv()

    bufs = list(srcs) + list(lands)
    res = pl.pallas_call(
        body, name=name, in_specs=[_HBM] * (2 * n) + [_SEM, _SEM, ANY], out_specs=tuple([_HBM] * (2 * n)),
        out_shape=tuple(pltpu.HBM(a.shape, a.dtype) for a in bufs),
        input_output_aliases={i: i for i in range(2 * n)},
        compiler_params=pltpu.CompilerParams(has_side_effects=pltpu.SideEffectType.DATAFLOW_SIDE_EFFECTING),
    )(*bufs, send_sems, recv_sems, after)
    return list(res[:n]), list(res[n:])


def _gather_all(buf):
    def body(in_ref, out_ref, send_sems, recv_sems, local_sem):
        x, y, c = _position()
        me = 4 * x + 2 * y + c
        peers = [(x, y, 1 - c)] + [(px, py, pc) for (px, py) in _other_chips(x, y) for pc in (c, 1 - c)]
        cp = pltpu.make_async_copy(in_ref, out_ref.at[me], local_sem)
        cp.start()
        remote = []
        for k, peer in enumerate(peers):
            rc = pltpu.make_async_remote_copy(
                src_ref=in_ref, dst_ref=out_ref.at[me], send_sem=send_sems.at[k], recv_sem=recv_sems.at[k],
                device_id=peer, device_id_type=MESH)
            rc.start()
            remote.append(rc)
        for k, (px, py, pc) in enumerate(peers):
            pltpu.make_async_remote_copy(
                src_ref=in_ref, dst_ref=out_ref.at[4 * px + 2 * py + pc], send_sem=send_sems.at[k],
                recv_sem=recv_sems.at[k], device_id=(px, py, pc), device_id_type=MESH).wait_recv()
        for rc in remote:
            rc.wait_send()
        cp.wait()

    return pl.pallas_call(
        body, name="gather_all", in_specs=[ANY], out_specs=ANY,
        out_shape=jax.ShapeDtypeStruct((N_DEV,) + buf.shape, buf.dtype),
        scratch_shapes=[pltpu.SemaphoreType.DMA((N_DEV - 1,)), pltpu.SemaphoreType.DMA((N_DEV - 1,)),
                        pltpu.SemaphoreType.DMA],
        compiler_params=pltpu.CompilerParams(has_side_effects=True),
    )(buf)


def _cols_from_shards(g4):
    _, k, n = g4.shape
    return g4.transpose(1, 0, 2).reshape(k, N_CHIPS * n)


def _cols_to_shards(w):
    k, n = w.shape
    return w.reshape(k, N_CHIPS, n // N_CHIPS).transpose(1, 0, 2)


def _pad_heads(w, width):
    k = w.shape[0]
    w3 = w.reshape(k, N_HEADS, width)
    return jnp.pad(w3, ((0, 0), (0, 0), (0, HEAD_PAD - width))).reshape(k, D_ATT)


def _unpad_heads(w, width):
    k = w.shape[0]
    return w.reshape(k, N_HEADS, HEAD_PAD)[:, :, :width]


def _rope_tables(s, after):
    pos = jnp.arange(s, dtype=F32) + after
    inv_freq = ROPE_THETA ** (-jnp.arange(0, QK_ROPE, 2, dtype=F32) / QK_ROPE)
    ang = pos[:, None] * inv_freq[None, :]
    cos_h, sin_h = jnp.cos(ang), jnp.sin(ang)
    half = QK_ROPE // 2
    z = jnp.zeros((s, half), F32)
    ones = jnp.ones((s, QK_NOPE), F32)
    tail = jnp.zeros((s, HEAD_PAD - QK_NOPE - QK_ROPE), F32)
    cos = jnp.concatenate([ones, cos_h, cos_h, tail + 1.0], axis=1)
    sin_a = jnp.concatenate([ones * 0.0, -sin_h, z, tail], axis=1)
    sin_b = jnp.concatenate([ones * 0.0, z, sin_h, tail], axis=1)
    return cos, sin_a, sin_b


def _local_step(x, p, target, wts, late_weights, reduce_early, reduce_last):
    s = x.shape[0]
    cos, sin_a, sin_b = wts["rope"]
    g1, gq, gkv, g2, g3, gf = (wts[k] for k in ("norm_mix_g", "q_norm_g", "kv_norm_g", "norm_ffn_g", "ple_norm_g",
                                                 "final_norm_g"))
    w_in_p, w_uq_p, w_kv_p = wts["w_in_p"], wts["w_uq_p"], wts["w_kv_p"]
    conv_w8, fconv_w, fconv_b = wts["conv_w8"], wts["ffn_conv_w"], wts["ffn_conv_b"]

    (h, z), _ = _mm_fused(x, w_in_p, name="mm_in", prologue=_pro_rms, vecs=[g1], epilogue=_epi_plain, row_outs=[F32])
    y_conv, qn, kvn, kr = _mix_pre(z, conv_w8, gq, gkv, cos, sin_a, sin_b)
    q, k, v, q_t = _qkv_proj(qn, kvn, kr, w_uq_p, w_kv_p, cos, sin_a, sin_b)
    o, lse = _flash_fwd(q, k, v)
    late = late_weights(lse)
    w_o_a, w_o_b, w_up, w_down = late["w_o_a"], late["w_o_b"], late["w_up"], late["w_down"]
    w_pg, w_pp = late["w_ple_gate"], late["w_ple_proj"]
    (x1, hf), _ = _mm_fused(o, w_o_b, second=(y_conv, w_o_a), name="mm_o", rows=[x], vecs=[g2],
                            epilogue=_epi_add_rms, row_outs=[F32, BF16])
    a_pre, act = _ffn_fwd(hf, w_up, fconv_w, fconv_b)
    (x2, n3), _ = _mm_fused(act, w_down, name="mm_down", rows=[x1], vecs=[g3], epilogue=_epi_add_rms,
                            row_outs=[F32, BF16])
    loss, dx3, dgl, dpp, d_gf = _ple_final(x2, n3, p, target, gf, w_pg, w_pp)

    grads, early = {"final_norm_g": d_gf}, {}
    early["w_ple_proj"] = _mm(p, dpp, ta=True, name="mm_d_wpp", tm=256, tn=1024, tk=2048)
    early["w_ple_gate"] = _mm(n3, dgl, ta=True, name="mm_d_wpg", tm=1024, tn=1024, tk=2048)
    (dx2,), (grads["ple_norm_g"],) = _mm_fused(dgl, w_pg, tb=True, name="mm_d_n3", rows=[x2, dx3], vecs=[g3],
                                               epilogue=_epi_rms_bwd, row_outs=[F32], n_vec_out=1)
    early["w_down"] = _mm(act, dx2, ta=True, name="mm_d_wdown", tm=1408, tn=1024, tk=2048)
    da_pre, grads["ffn_conv_w"], grads["ffn_conv_b"] = _ffn_bwd(dx2, w_down, a_pre, fconv_w, fconv_b)
    early["w_up"] = _mm(hf, da_pre, ta=True, b_split=True, name="mm_d_wup", tm=1024, tn=1408, tk=2048,
                       o_shards=True)
    (dx1,), (grads["norm_ffn_g"],) = _mm_fused(da_pre, w_up, tb=True, a_split=True, name="mm_d_hf", rows=[x1, dx2],
                                               vecs=[g2], epilogue=_epi_rms_bwd, row_outs=[F32], n_vec_out=1)
    d_wo_a = _mm(y_conv, dx1, ta=True, name="mm_d_wo_conv", tm=512, tn=1024, tk=2048)
    d_wo_b = _mm(o, dx1, ta=True, name="mm_d_wo_att", tm=1024, tn=1024, tk=2048)
    early["w_o"] = jnp.concatenate([d_wo_a, d_wo_b.reshape(N_HEADS, HEAD_PAD, D_MODEL)[:, :V_HEAD]
                                    .reshape(N_HEADS * V_HEAD, D_MODEL)], axis=0)
    token, finish = reduce_early(early)
    dyc = _mm(dx1, w_o_a, tb=True, name="mm_d_yconv", tm=512, tn=512, tk=1024)
    (do, do_t), _ = _mm_fused(dx1, w_o_b, tb=True, name="mm_d_o", epilogue=_epi_plain, row_outs=[BF16],
                              transposed_out=BF16)
    delta = _attn_delta(do, o)
    dq, dk, dv = _flash_bwd(q, q_t, k, v, do, do_t, lse, delta, token)
    reduced_early = finish(dq)
    dq_pre, dkr = _qk_bwd(dq, dk, cos, sin_a, sin_b)
    grads["w_uq_p"] = _mm(qn, dq_pre, ta=True, name="mm_d_wuq", tm=256, tn=1024, tk=2048)
    dqn = _mm(dq_pre, w_uq_p, tb=True, name="mm_d_qn", tm=512, tn=256, tk=1024)
    grads["w_k_p"] = _mm(kvn, dk, ta=True, name="mm_d_wk", tm=128, tn=1024, tk=2048)
    grads["w_v_p"] = _mm(kvn, dv, ta=True, name="mm_d_wv", tm=128, tn=1024, tk=2048)
    (dkvn,), _ = _mm_fused(dk, w_kv_p[:, :D_ATT], tb=True, second=(dv, w_kv_p[:, D_ATT:].T), name="mm_d_kvn",
                           epilogue=_epi_plain, row_outs=[F32])
    dz, grads["conv_w"], grads["q_norm_g"], grads["kv_norm_g"] = _mix_bwd(
        z, dyc, dqn, dkvn, dkr, conv_w8, gq, gkv, cos, sin_a, sin_b)
    grads["w_in_p"] = _mm(h, dz, ta=True, name="mm_d_win", tm=1024, tn=1024, tk=2048)
    token, finish = reduce_last({n: grads.pop(n) for n in ("w_in_p", "w_uq_p", "w_k_p", "w_v_p")})
    (grad_x,), (grads["norm_mix_g"],) = _mm_fused(dz, w_in_p, tb=True, name="mm_d_h", rows=[x, dx1],
                                                  vecs=[g1 + token[0, 0]], epilogue=_epi_rms_bwd, row_outs=[F32],
                                                  n_vec_out=1)
    return loss[0, 0], grad_x, grads, reduced_early, finish(grad_x)


_EARLY_W = ("w_in", "w_uq", "w_ukv")
_LATE_W = ("w_o", "w_up", "w_down", "w_ple_gate", "w_ple_proj")
_BIG = _EARLY_W + _LATE_W
_COL_SHARDED = ("w_in", "w_uq", "w_ukv", "w_up", "w_ple_proj")
_SMALL = ("norm_mix_g", "conv_w", "q_norm_g", "kv_norm_g", "norm_ffn_g", "ffn_conv_w", "ffn_conv_b", "ple_norm_g",
          "final_norm_g")


def _full_from_slots(n, g4):
    return _cols_from_shards(g4) if n in _COL_SHARDED else g4.reshape(-1, g4.shape[2])


def _shard_major(n, g):
    if g.ndim == 3:
        return g
    return _cols_to_shards(g) if n in _COL_SHARDED else g.reshape(N_CHIPS, g.shape[0] // N_CHIPS, g.shape[1])


def _early_shards(w):
    shards = [w[n][0].astype(BF16) for n in _EARLY_W]
    shards.append(jnp.pad(w["conv_w"][0], ((0, 5), (0, 0))))
    shards.append(jnp.pad(w["ffn_conv_w"][0], ((0, 5), (0, 0))))
    return shards


def _fill_own_slot(landed, own, chip):
    return [lax.dynamic_update_slice(g4, a[None], (chip[0], 0, 0)) for g4, a in zip(landed, own)]


def _early_weights(got, w):
    full = {n: _full_from_slots(n, g4) for n, g4 in zip(_EARLY_W, got)}
    full["conv_w8"] = _cols_from_shards(got[len(_EARLY_W)])
    full["ffn_conv_w8"] = _cols_from_shards(got[len(_EARLY_W) + 1])
    return _layout_early(full, w)


def _layout_early(full, w):
    out = {n: w[n] for n in ("norm_mix_g", "q_norm_g", "kv_norm_g", "norm_ffn_g", "ple_norm_g")}
    out["final_norm_g"] = w["final_norm_g"][None, :]
    w_in = full["w_in"]
    zc = jnp.zeros((D_MODEL, QK_NOPE), BF16)
    zt = jnp.zeros((D_MODEL, HEAD_PAD - QK_NOPE - QK_ROPE), BF16)
    out["w_in_p"] = jnp.concatenate([w_in[:, :D_IN - QK_ROPE], zc, w_in[:, D_IN - QK_ROPE:], zt], axis=1)
    out["w_uq_p"] = _pad_heads(full["w_uq"], QK_NOPE + QK_ROPE)
    kv3 = full["w_ukv"].reshape(KV_LORA, N_HEADS, QK_NOPE + V_HEAD)
    out["w_kv_p"] = jnp.concatenate([_pad_heads(kv3[:, :, :QK_NOPE].reshape(KV_LORA, -1), QK_NOPE),
                                     _pad_heads(kv3[:, :, QK_NOPE:].reshape(KV_LORA, -1), V_HEAD)], axis=1)
    out["conv_w8"] = full["conv_w8"]
    fw = full["ffn_conv_w8"]
    out["ffn_conv_w"] = jnp.stack([fw[:, :D_FF], fw[:, D_FF:]])
    out["ffn_conv_b"] = w["ffn_conv_b"].reshape(2, 1, D_FF)
    return out


def _layout_late(full):
    w_o = full["w_o"]
    out = {"w_o_a": w_o[:CONV_WIDTH]}
    out["w_o_b"] = jnp.pad(w_o[CONV_WIDTH:].reshape(N_HEADS, V_HEAD, D_MODEL),
                           ((0, 0), (0, HEAD_PAD - V_HEAD), (0, 0))).reshape(D_ATT, D_MODEL)
    for n in ("w_up", "w_down", "w_ple_gate", "w_ple_proj"):
        out[n] = full[n]
    return out


def _true_matrices(g):
    out = {}
    wp = g["w_in_p"]
    out["w_in"] = jnp.concatenate([wp[:, :D_IN - QK_ROPE], wp[:, D_IN_PAD - HEAD_PAD + QK_NOPE:
                                                              D_IN_PAD - HEAD_PAD + QK_NOPE + QK_ROPE]], axis=1)
    out["w_uq"] = _unpad_heads(g["w_uq_p"], QK_NOPE + QK_ROPE).reshape(Q_LORA, -1)
    out["w_ukv"] = jnp.concatenate([_unpad_heads(g["w_k_p"], QK_NOPE), _unpad_heads(g["w_v_p"], V_HEAD)],
                                   axis=2).reshape(KV_LORA, -1)
    return out


def _true_vectors(g):
    out = {}
    out["conv_w"] = g["conv_w"]
    fw = g["ffn_conv_w"]
    out["ffn_conv_w"] = jnp.concatenate([fw[0, :3], fw[1, :3]], axis=1)
    out["ffn_conv_b"] = g["ffn_conv_b"].reshape(1, 2 * D_FF)
    for n in ("norm_mix_g", "q_norm_g", "kv_norm_g", "norm_ffn_g", "ple_norm_g", "final_norm_g"):
        out[n] = g[n]
    return out


def _chip_partials(names, g, core, *, tag):
    g4 = [_shard_major(n, g[n]) for n in names]
    sib = _send_other_halves(g4, tag=tag)
    return [_add_own_half(a, b, core, name="add_cores_" + n) for n, a, b in zip(names, g4, sib)]


_SMALL_SIZES = {"norm_mix_g": D_MODEL, "conv_w": 3 * CONV_WIDTH, "q_norm_g": Q_LORA, "kv_norm_g": KV_LORA,
                "norm_ffn_g": D_MODEL, "ffn_conv_w": 6 * D_FF, "ffn_conv_b": 2 * D_FF, "ple_norm_g": D_MODEL,
                "final_norm_g": D_MODEL}


def _pack(parts, rows):
    flat = jnp.concatenate([a.reshape(-1) for a in parts])
    return jnp.pad(flat, (0, rows * 128 - flat.shape[0])).reshape(rows, 128)


def _unpack(buf, sizes):
    flat = buf.reshape(-1)
    out, at = [], 0
    for n in sizes:
        out.append(flat[at:at + n])
        at += n
    return out


def _reduce_small(g, loss):
    sizes = [1] + [_SMALL_SIZES[n] for n in _SMALL]
    rows = -(-sum(sizes) // 1024) * 8
    slots = _gather_all(_pack([loss] + [g[n] for n in _SMALL], rows))
    parts = _unpack(_sum_slots(slots, name="sum_small"), sizes)
    return parts[0][0], dict(zip(_SMALL, parts[1:]))


def kernel(x, p, norm_mix_g, w_in, conv_w, q_norm_g, w_uq, kv_norm_g, w_ukv, w_o, norm_ffn_g, w_up, ffn_conv_w, ffn_conv_b, w_down, ple_norm_g, w_ple_gate, w_ple_proj, final_norm_g, loss_target, m_norm_mix_g, m_w_in, m_conv_w, m_q_norm_g, m_w_uq, m_kv_norm_g, m_w_ukv, m_w_o, m_norm_ffn_g, m_w_up, m_ffn_conv_w, m_ffn_conv_b, m_w_down, m_ple_norm_g, m_w_ple_gate, m_w_ple_proj, m_final_norm_g, v_norm_mix_g, v_w_in, v_conv_w, v_q_norm_g, v_w_uq, v_kv_norm_g, v_w_ukv, v_w_o, v_norm_ffn_g, v_w_up, v_ffn_conv_w, v_ffn_conv_b, v_w_down, v_ple_norm_g, v_w_ple_gate, v_w_ple_proj, v_final_norm_g):
    names = ["norm_mix_g", "w_in", "conv_w", "q_norm_g", "w_uq", "kv_norm_g", "w_ukv", "w_o", "norm_ffn_g", "w_up",
             "ffn_conv_w", "ffn_conv_b", "w_down", "ple_norm_g", "w_ple_gate", "w_ple_proj", "final_norm_g"]
    w = dict(zip(names, (norm_mix_g, w_in, conv_w, q_norm_g, w_uq, kv_norm_g, w_ukv, w_o, norm_ffn_g, w_up,
                         ffn_conv_w, ffn_conv_b, w_down, ple_norm_g, w_ple_gate, w_ple_proj, final_norm_g)))
    m = dict(zip(names, (m_norm_mix_g, m_w_in, m_conv_w, m_q_norm_g, m_w_uq, m_kv_norm_g, m_w_ukv, m_w_o,
                         m_norm_ffn_g, m_w_up, m_ffn_conv_w, m_ffn_conv_b, m_w_down, m_ple_norm_g, m_w_ple_gate,
                         m_w_ple_proj, m_final_norm_g)))
    v = dict(zip(names, (v_norm_mix_g, v_w_in, v_conv_w, v_q_norm_g, v_w_uq, v_kv_norm_g, v_w_ukv, v_w_o,
                         v_norm_ffn_g, v_w_up, v_ffn_conv_w, v_ffn_conv_b, v_w_down, v_ple_norm_g, v_w_ple_gate,
                         v_w_ple_proj, v_final_norm_g)))

    core = lax.axis_index("c").astype(jnp.int32).reshape(1)
    chip = (2 * lax.axis_index("x") + lax.axis_index("y")).astype(jnp.int32).reshape(1)

    first = _chips_start(_early_shards(w), scatter=False, name="gather_early_start")
    rope = _rope_tables(x.shape[1], first[4][0, 0])
    late_shards = [w[n][0].astype(BF16) for n in _LATE_W]
    ready, *late_shards = lax.optimization_barrier((rope[0], *late_shards))
    own, landed = _chips_wait(first, ready, scatter=False, name="gather_early_wait")
    wts = _early_weights(_fill_own_slot(landed, own, chip), w)
    wts["rope"] = (ready,) + tuple(rope[1:])
    late_shards[0], _ = lax.optimization_barrier((late_shards[0], own[0]))
    gather = _chips_start(late_shards, scatter=False, name="gather_late_start")
    wts["norm_mix_g"] = wts["norm_mix_g"] + gather[4][0, 0]

    def late_weights(after):
        shards, landed = _chips_wait(gather, after, scatter=False, name="gather_late_wait")
        return _layout_late({n: _full_from_slots(n, g4)
                             for n, g4 in zip(_LATE_W, _fill_own_slot(landed, shards, chip))})

    def reduce_early(g):
        parts = _chip_partials(_LATE_W, g, core, tag="early")
        scatter = _chips_start([t16 for _, t16 in parts], scatter=True, name="scatter_early_start")

        def finish(after):
            _, landed = _chips_wait(scatter, after, scatter=True, name="scatter_early_wait")
            return [_sum_chips(a, t32, chip, name="sum_chips_" + n) for n, a, (t32, _) in zip(_LATE_W, landed, parts)]

        return scatter[4], finish

    def reduce_last(g):
        parts = _chip_partials(_EARLY_W, _true_matrices(g), core, tag="late")
        scatter = _chips_start([t16 for _, t16 in parts], scatter=True, name="scatter_late_start")

        def finish(after):
            _, landed = _chips_wait(scatter, after, scatter=True, name="scatter_late_wait")
            return [_sum_chips(a, t32, chip, name="sum_chips_" + n) for n, a, (t32, _) in zip(_EARLY_W, landed, parts)]

        return scatter[4], finish

    loss, grad_x, small_grads, halves_early, halves_last = _local_step(
        x[0], p[0, 0], loss_target[0], wts, late_weights, reduce_early, reduce_last)
    g_full = _true_vectors(small_grads)
    whole = _join_halves(halves_last + halves_early)
    big = {n: a.reshape(-1, a.shape[2]) for n, a in zip(_BIG, whole)}

    g_out, d_out, m_out, v_out = {}, {}, {}, {}
    for n in _BIG:
        shape = w[n].shape
        g = big[n]
        d, mn, vn = _adamw(w[n][0], g, m[n][0], v[n][0], name="adamw_" + n)
        g_out[n], d_out[n], m_out[n], v_out[n] = (a.reshape(shape) for a in (g, d, mn, vn))

    loss, small = _reduce_small(g_full, loss)
    chip = 2 * lax.axis_index("x") + lax.axis_index("y")
    g_small = {}
    for n in _SMALL:
        shape = w[n].shape
        g = small[n]
        if n in ("conv_w", "ffn_conv_w"):
            width = shape[-1]
            g = lax.dynamic_slice(g.reshape(3, N_CHIPS * width), (0, chip * width), (3, width))
        g_small[n] = g.reshape(shape)
    flat = [[src[n].reshape(-1, src[n].shape[-1]) for n in _SMALL] for src in (w, g_small, m, v)]
    for n, (d, mn, vn) in zip(_SMALL, _adamw_many(*flat, name="adamw_small")):
        shape = w[n].shape
        g_out[n], d_out[n], m_out[n], v_out[n] = g_small[n], d.reshape(shape), mn.reshape(shape), vn.reshape(shape)

    return (loss, grad_x[None], *[g_out[n] for n in names], *[d_out[n] for n in names],
            *[m_out[n] for n in names], *[v_out[n] for n in names])
```

```python
import functools

import jax
import jax.numpy as jnp
from jax import lax
from jax.experimental import pallas as pl
from jax.experimental.pallas import tpu as pltpu

F32 = jnp.float32
BF16 = jnp.bfloat16

D_MODEL = 1024
CONV_WIDTH = 512
Q_LORA = 256
KV_LORA = 128
QK_NOPE = 64
QK_ROPE = 32
V_HEAD = 64
N_HEADS = 8
HEAD_PAD = 128
D_ATT = N_HEADS * HEAD_PAD
D_IN = 3 * CONV_WIDTH + Q_LORA + KV_LORA + QK_ROPE
D_IN_PAD = 3 * CONV_WIDTH + Q_LORA + KV_LORA + HEAD_PAD
D_FF = 2816
ROPE_THETA = 10000.0
EPS = 1e-6
SM_SCALE = (QK_NOPE + QK_ROPE) ** -0.5
ONES_LANE = V_HEAD

ADAM_LR = 0.001
ADAM_B1 = 0.9
ADAM_B2 = 0.999
ADAM_EPS = 1e-08
ADAM_WD = 0.01
ADAM_STEP = 10

N_CHIPS = 4
N_DEV = 8
MESH = pl.DeviceIdType.MESH
ANY = pl.BlockSpec(memory_space=pl.ANY)


def _params(sem):
    return pltpu.CompilerParams(dimension_semantics=sem)


MM_PIECE = 256


def _pieces(total, width=MM_PIECE):
    return [(off, min(width, total - off)) for off in range(0, total, width)]


def _mm(a, b, *, name, ta=False, tb=False, add=None, out_dtype=F32, tm=512, tn=512, tk=512, b_split=False,
        o_shards=False):
    k, m = a.shape if ta else a.shape[::-1]
    if b_split:
        _, kb, nh = b.shape
        n = 2 * nh
    elif tb:
        n, kb = b.shape
    else:
        kb, n = b.shape
    assert kb == k, (name, a.shape, b.shape)
    tm, tn, tk = min(tm, m), min(tn, n), min(tk, k)
    assert m % tm == 0 and n % tn == 0 and k % tk == 0, (name, m, n, k, tm, tn, tk)
    gm, gn, gk = m // tm, n // tn, k // tk

    a_spec = pl.BlockSpec((tk, tm), lambda i, j, kk: (kk, i)) if ta else pl.BlockSpec((tm, tk), lambda i, j, kk: (i, kk))
    if b_split:
        assert gn % 2 == 0
        b_spec = pl.BlockSpec((None, tk, tn), lambda i, j, kk: (j // (gn // 2), kk, j % (gn // 2)))
    elif tb:
        b_spec = pl.BlockSpec((tn, tk), lambda i, j, kk: (j, kk))
    else:
        b_spec = pl.BlockSpec((tk, tn), lambda i, j, kk: (kk, j))
    if o_shards:
        o_spec = pl.BlockSpec((None, tm, tn), lambda i, j, kk: (j, i, 0))
        o_shape = jax.ShapeDtypeStruct((gn, m, tn), out_dtype)
    else:
        o_spec = pl.BlockSpec((tm, tn), lambda i, j, kk: (i, j))
        o_shape = jax.ShapeDtypeStruct((m, n), out_dtype)
    dims = (((0 if ta else 1,), (1 if tb else 0,)), ((), ()))

    def body(*refs):
        a_ref, b_ref = refs[:2]
        add_ref = None if add is None else refs[2]
        o_ref = refs[2 if add is None else 3]
        acc_ref = None if gk == 1 else refs[-1]
        kk = pl.program_id(2)
        rhs = b_ref[...].astype(BF16)

        def finish(r, rows):
            if add_ref is not None:
                r = r + add_ref[rows, :]
            o_ref[rows, :] = r.astype(o_ref.dtype)

        for off, size in _pieces(tm):
            rows = slice(off, off + size)
            lhs = (a_ref[:, rows] if ta else a_ref[rows, :]).astype(BF16)
            part = lax.dot_general(lhs, rhs, dims, preferred_element_type=F32)
            if gk == 1:
                finish(part, rows)
            else:
                acc_ref[rows, :] = jnp.where(kk == 0, part, acc_ref[rows, :] + part)

        if gk > 1:
            @pl.when(kk == gk - 1)
            def _():
                finish(acc_ref[...], slice(None))

    in_specs = [a_spec, b_spec]
    args = [a, b]
    if add is not None:
        in_specs.append(pl.BlockSpec((tm, tn), lambda i, j, kk: (i, j)))
        args.append(add)
    return pl.pallas_call(
        body, name=name, grid=(gm, gn, gk), in_specs=in_specs, out_specs=o_spec, out_shape=o_shape,
        scratch_shapes=[] if gk == 1 else [pltpu.VMEM((tm, tn), F32)],
        compiler_params=_params(("parallel", "parallel", "arbitrary")),
    )(*args)


def _rms_scale(v):
    return lax.rsqrt(jnp.mean(v * v, axis=-1, keepdims=True) + EPS)


def _rms_bwd_rows(v, g, dy):
    r = _rms_scale(v)
    vh = v * r
    dyg = dy * g
    dv = r * (dyg - vh * jnp.mean(dyg * vh, axis=-1, keepdims=True))
    return dv, dy * vh


def _shift_down(v, first_row):
    row = lax.broadcasted_iota(jnp.int32, v.shape, 0)
    return jnp.where(row == 0, first_row, pltpu.roll(v, 1, 0))


def _shift_up(v, last_row):
    n = v.shape[0]
    row = lax.broadcasted_iota(jnp.int32, v.shape, 0)
    return jnp.where(row == n - 1, last_row, pltpu.roll(v, n - 1, 0))


def _rope(t, cos, sin_a, sin_b):
    return t * cos + pltpu.roll(t, HEAD_PAD - 16, 1) * sin_a + pltpu.roll(t, 16, 1) * sin_b


def _rope_bwd(d, cos, sin_a, sin_b):
    return d * cos + pltpu.roll(d * sin_a, 16, 1) + pltpu.roll(d * sin_b, HEAD_PAD - 16, 1)


def _sigmoid(v):
    return 1.0 / (1.0 + jnp.exp(-v))


def _halo_specs(ts, s, width, col):
    nb = ts // 8
    last = s // 8 - 1
    prev = pl.BlockSpec((8, width), lambda i: (jnp.maximum(i * nb - 1, 0), col))
    nxt = pl.BlockSpec((8, width), lambda i: (jnp.minimum((i + 1) * nb, last), col))
    return prev, nxt


def _mm_fused(a, b, *, name, epilogue, row_outs, rows=(), vecs=(), n_vec_out=0, tb=False, a_split=False,
              prologue=None, second=None, transposed_out=None, tm=512):
    if a_split:
        _, m, kh = a.shape
        k = 2 * kh
    else:
        m, k = a.shape
    n = b.shape[0] if tb else b.shape[1]
    assert (b.shape[1] if tb else b.shape[0]) == k, (name, a.shape, b.shape)
    assert m % tm == 0, (name, m, tm)
    n_a = 2 if a_split else 1
    nr, nv = len(rows), len(vecs)
    n_pro = 0 if prologue is None else 1
    n_sec = 0 if second is None else 2
    n_t = 0 if transposed_out is None else 1
    dims = (((1,), (1 if tb else 0,)), ((), ()))

    def body(*refs):
        a_refs, b_ref = refs[:n_a], refs[n_a]
        refs = refs[n_a + 1:]
        sec_refs = refs[:n_sec]
        row_refs, vec_refs = refs[n_sec:n_sec + nr], refs[n_sec + nr:n_sec + nr + nv]
        outs = refs[n_sec + nr + nv:]
        row_out_refs = outs[n_pro:n_pro + len(row_outs)]
        t_out_refs = outs[n_pro + len(row_outs):n_pro + len(row_outs) + n_t]
        vec_out_refs = outs[n_pro + len(row_outs) + n_t:n_pro + len(row_outs) + n_t + n_vec_out]
        vec_vals = [v[...] for v in vec_refs]
        if a_split:
            kh = k // 2
            rhs = [(b_ref[:, :kh], b_ref[:, kh:]) if tb else (b_ref[:kh, :], b_ref[kh:, :])][0]
            rhs = [h.astype(BF16) for h in rhs]
        else:
            rhs = [b_ref[...].astype(BF16)]
        vec_sums = [None] * n_vec_out

        for off, size in _pieces(tm):
            rs = slice(off, off + size)
            if prologue is None:
                lhs = [a_ref[rs, :].astype(BF16) for a_ref in a_refs]
            else:
                lhs = [prologue(a_refs[0][rs, :], vec_vals)]
                outs[0][rs, :] = lhs[0]
            r = lax.dot_general(lhs[0], rhs[0], dims, preferred_element_type=F32)
            for l2, r2 in zip(lhs[1:], rhs[1:]):
                r = r + lax.dot_general(l2, r2, dims, preferred_element_type=F32)
            if second is not None:
                r = r + jnp.dot(sec_refs[0][rs, :].astype(BF16), sec_refs[1][...].astype(BF16),
                                preferred_element_type=F32)
            row_vals, vec_parts = epilogue(r, [x[rs, :] for x in row_refs], vec_vals)
            for ref, val in zip(row_out_refs, row_vals):
                ref[rs, :] = val.astype(ref.dtype)
            for ref in t_out_refs:
                ref[:, rs] = row_vals[0].T.astype(ref.dtype)
            vec_sums = [p if t is None else t + p for t, p in zip(vec_sums, vec_parts)]

        if n_vec_out:
            @pl.when(pl.program_id(0) == 0)
            def _():
                for ref in vec_out_refs:
                    ref[...] = jnp.zeros_like(ref)

            for ref, val in zip(vec_out_refs, vec_sums):
                ref[...] += val

    if a_split:
        a_specs = [pl.BlockSpec((None, tm, k // 2), lambda i: (0, i, 0)),
                   pl.BlockSpec((None, tm, k // 2), lambda i: (1, i, 0))]
    else:
        a_specs = [pl.BlockSpec((tm, k), lambda i: (i, 0))]
    b_spec = pl.BlockSpec(b.shape, lambda i: (0, 0))
    row_spec = pl.BlockSpec((tm, n), lambda i: (i, 0))
    out_specs, out_shape = [], []
    if prologue is not None:
        out_specs.append(pl.BlockSpec((tm, k), lambda i: (i, 0)))
        out_shape.append(jax.ShapeDtypeStruct((m, k), BF16))
    out_specs += [row_spec] * len(row_outs)
    out_shape += [jax.ShapeDtypeStruct((m, n), dt) for dt in row_outs]
    if transposed_out is not None:
        out_specs.append(pl.BlockSpec((n, tm), lambda i: (0, i)))
        out_shape.append(jax.ShapeDtypeStruct((n, m), transposed_out))
    out_specs += [pl.BlockSpec((1, n), lambda i: (0, 0))] * n_vec_out
    out_shape += [jax.ShapeDtypeStruct((1, n), F32)] * n_vec_out
    sec_specs, sec_args = [], []
    if second is not None:
        k2 = second[0].shape[1]
        sec_specs = [pl.BlockSpec((tm, k2), lambda i: (i, 0)), pl.BlockSpec((k2, n), lambda i: (0, 0))]
        sec_args = list(second)
    res = pl.pallas_call(
        body, name=name, grid=(m // tm,),
        in_specs=a_specs + [b_spec] + sec_specs + [row_spec] * nr
        + [pl.BlockSpec((1, v.shape[1]), lambda i: (0, 0)) for v in vecs],
        out_specs=out_specs, out_shape=out_shape,
        compiler_params=_params(("arbitrary" if n_vec_out else "parallel",)),
    )(*([a] * n_a), b, *sec_args, *rows, *vecs)
    split = n_pro + len(row_outs) + n_t
    return list(res[:split]), list(res[split:])


def _pro_rms(a, vecs):
    return (a * _rms_scale(a) * vecs[0]).astype(BF16)


def _epi_plain(r, rows, vecs):
    return [r], []


def _epi_add_rms(r, rows, vecs):
    xn = r + rows[0]
    return [xn, xn * _rms_scale(xn) * vecs[0]], []


def _epi_rms_bwd(r, rows, vecs):
    dv, dg_rows = _rms_bwd_rows(rows[0], vecs[0], r)
    return [dv + rows[1]], [jnp.sum(dg_rows, axis=0, keepdims=True)]


def _mix_pre(z, conv_w8, gq, gkv, cos, sin_a, sin_b, *, ts=512):
    s = z.shape[0]
    n = s // ts
    cw = CONV_WIDTH

    def body(z_ref, xcp, xcn, cgp, cgn, w_ref, gq_ref, gkv_ref, cos_ref, sa_ref, sb_ref,
             yc_ref, qn_ref, kvn_ref, kr_ref):
        i = pl.program_id(0)
        xc = z_ref[:, 0:cw]
        bg = z_ref[:, cw:2 * cw]
        cg = z_ref[:, 2 * cw:3 * cw]
        m = cg * xc
        m_prev = jnp.where(i > 0, xcp[7:8, :] * cgp[7:8, :], 0.0)
        m_next = jnp.where(i < n - 1, xcn[0:1, :] * cgn[0:1, :], 0.0)
        cm = _shift_down(m, m_prev) * w_ref[0:1, :] + m * w_ref[1:2, :] + _shift_up(m, m_next) * w_ref[2:3, :]
        yc_ref[...] = (bg * cm).astype(BF16)
        ql = z_ref[:, 3 * cw:3 * cw + Q_LORA]
        qn_ref[...] = (ql * _rms_scale(ql) * gq_ref[...]).astype(BF16)
        kvl = z_ref[:, 3 * cw + Q_LORA:3 * cw + Q_LORA + KV_LORA]
        kvn_ref[...] = (kvl * _rms_scale(kvl) * gkv_ref[...]).astype(BF16)
        kr_ref[...] = _rope(z_ref[:, D_IN_PAD - HEAD_PAD:D_IN_PAD], cos_ref[...], sa_ref[...], sb_ref[...])

    xcp, xcn = _halo_specs(ts, s, cw, 0)
    cgp, cgn = _halo_specs(ts, s, cw, 2)
    tab = pl.BlockSpec((ts, HEAD_PAD), lambda i: (i, 0))
    return pl.pallas_call(
        body, name="mix_pre", grid=(n,),
        in_specs=[pl.BlockSpec((ts, D_IN_PAD), lambda i: (i, 0)), xcp, xcn, cgp, cgn,
                  pl.BlockSpec((8, cw), lambda i: (0, 0)), pl.BlockSpec((1, Q_LORA), lambda i: (0, 0)),
                  pl.BlockSpec((1, KV_LORA), lambda i: (0, 0)), tab, tab, tab],
        out_specs=[pl.BlockSpec((ts, cw), lambda i: (i, 0)), pl.BlockSpec((ts, Q_LORA), lambda i: (i, 0)),
                   pl.BlockSpec((ts, KV_LORA), lambda i: (i, 0)), tab],
        out_shape=[jax.ShapeDtypeStruct((s, cw), BF16), jax.ShapeDtypeStruct((s, Q_LORA), BF16),
                   jax.ShapeDtypeStruct((s, KV_LORA), BF16), jax.ShapeDtypeStruct((s, HEAD_PAD), F32)],
        compiler_params=_params(("parallel",)),
    )(z, z, z, z, z, conv_w8, gq, gkv, cos, sin_a, sin_b)


def _mix_bwd(z, dyc, dqn, dkvn, dkr, conv_w8, gq, gkv, cos, sin_a, sin_b, *, ts=512):
    s = z.shape[0]
    n = s // ts
    cw = CONV_WIDTH

    def body(z_ref, xcp, xcn, bgp, bgn, cgp, cgn, dyc_ref, dycp, dycn, dqn_ref, dkvn_ref, dkr_ref,
             w_ref, gq_ref, gkv_ref, cos_ref, sa_ref, sb_ref,
             dz_ref, dw0_ref, dw1_ref, dw2_ref, dgq_ref, dgkv_ref):
        i = pl.program_id(0)

        @pl.when(i == 0)
        def _():
            for r in (dw0_ref, dw1_ref, dw2_ref, dgq_ref, dgkv_ref):
                r[...] = jnp.zeros_like(r)

        xc = z_ref[:, 0:cw]
        bg = z_ref[:, cw:2 * cw]
        cg = z_ref[:, 2 * cw:3 * cw]
        w0, w1, w2 = w_ref[0:1, :], w_ref[1:2, :], w_ref[2:3, :]
        m = cg * xc
        m_dn = _shift_down(m, jnp.where(i > 0, xcp[7:8, :] * cgp[7:8, :], 0.0))
        m_up = _shift_up(m, jnp.where(i < n - 1, xcn[0:1, :] * cgn[0:1, :], 0.0))
        cm = m_dn * w0 + m * w1 + m_up * w2
        dyc_v = dyc_ref[...]
        dcm = dyc_v * bg
        dcm_dn = _shift_down(dcm, jnp.where(i > 0, dycp[7:8, :] * bgp[7:8, :], 0.0))
        dcm_up = _shift_up(dcm, jnp.where(i < n - 1, dycn[0:1, :] * bgn[0:1, :], 0.0))
        dm = dcm_up * w0 + dcm * w1 + dcm_dn * w2
        dz_ref[:, 0:cw] = (dm * cg).astype(BF16)
        dz_ref[:, cw:2 * cw] = (dyc_v * cm).astype(BF16)
        dz_ref[:, 2 * cw:3 * cw] = (dm * xc).astype(BF16)
        dw0_ref[...] += jnp.sum(dcm * m_dn, axis=0, keepdims=True)
        dw1_ref[...] += jnp.sum(dcm * m, axis=0, keepdims=True)
        dw2_ref[...] += jnp.sum(dcm * m_up, axis=0, keepdims=True)

        dql, dgq_rows = _rms_bwd_rows(z_ref[:, 3 * cw:3 * cw + Q_LORA], gq_ref[...], dqn_ref[...])
        dz_ref[:, 3 * cw:3 * cw + Q_LORA] = dql.astype(BF16)
        dgq_ref[...] += jnp.sum(dgq_rows, axis=0, keepdims=True)
        dkvl, dgkv_rows = _rms_bwd_rows(z_ref[:, 3 * cw + Q_LORA:3 * cw + Q_LORA + KV_LORA], gkv_ref[...],
                                        dkvn_ref[...])
        dz_ref[:, 3 * cw + Q_LORA:3 * cw + Q_LORA + KV_LORA] = dkvl.astype(BF16)
        dgkv_ref[...] += jnp.sum(dgkv_rows, axis=0, keepdims=True)

        lane = lax.broadcasted_iota(jnp.int32, (ts, HEAD_PAD), 1)
        rope_lane = (lane >= QK_NOPE) & (lane < QK_NOPE + QK_ROPE)
        dk = _rope_bwd(dkr_ref[...], cos_ref[...], sa_ref[...], sb_ref[...])
        dz_ref[:, D_IN_PAD - HEAD_PAD:D_IN_PAD] = jnp.where(rope_lane, dk, 0.0).astype(BF16)

    xcp, xcn = _halo_specs(ts, s, cw, 0)
    bgp, bgn = _halo_specs(ts, s, cw, 1)
    cgp, cgn = _halo_specs(ts, s, cw, 2)
    dycp, dycn = _halo_specs(ts, s, cw, 0)
    tab = pl.BlockSpec((ts, HEAD_PAD), lambda i: (i, 0))

    def vec(width):
        return pl.BlockSpec((1, width), lambda i: (0, 0))

    outs = pl.pallas_call(
        body, name="mix_bwd", grid=(n,),
        in_specs=[pl.BlockSpec((ts, D_IN_PAD), lambda i: (i, 0)), xcp, xcn, bgp, bgn, cgp, cgn,
                  pl.BlockSpec((ts, cw), lambda i: (i, 0)), dycp, dycn,
                  pl.BlockSpec((ts, Q_LORA), lambda i: (i, 0)), pl.BlockSpec((ts, KV_LORA), lambda i: (i, 0)), tab,
                  pl.BlockSpec((8, cw), lambda i: (0, 0)), vec(Q_LORA), vec(KV_LORA), tab, tab, tab],
        out_specs=[pl.BlockSpec((ts, D_IN_PAD), lambda i: (i, 0)), vec(cw), vec(cw), vec(cw), vec(Q_LORA),
                   vec(KV_LORA)],
        out_shape=[jax.ShapeDtypeStruct((s, D_IN_PAD), BF16)] + [jax.ShapeDtypeStruct((1, cw), F32)] * 3
        + [jax.ShapeDtypeStruct((1, Q_LORA), F32), jax.ShapeDtypeStruct((1, KV_LORA), F32)],
        compiler_params=_params(("arbitrary",)),
    )(z, z, z, z, z, z, z, dyc, dyc, dyc, dqn, dkvn, dkr, conv_w8, gq, gkv, cos, sin_a, sin_b)
    dz, dw0, dw1, dw2, dgq, dgkv = outs
    return dz, jnp.concatenate([dw0, dw1, dw2], axis=0), dgq, dgkv


def _qkv_proj(qn, kvn, kr, w_uq_p, w_kv_p, cos, sin_a, sin_b, *, ts=512):
    s = qn.shape[0]

    def body(qn_ref, kvn_ref, kr_ref, wq_ref, wkv_ref, cos_ref, sa_ref, sb_ref, q_ref, k_ref, v_ref, qt_ref):
        cos_v, sa, sb = cos_ref[...], sa_ref[...], sb_ref[...]
        q = jnp.dot(qn_ref[...], wq_ref[...], preferred_element_type=F32)
        kv = jnp.dot(kvn_ref[...], wkv_ref[...], preferred_element_type=F32)
        kr_v = kr_ref[...]
        lane = lax.broadcasted_iota(jnp.int32, (1, HEAD_PAD), 1)
        ones_lane = (lane == ONES_LANE).astype(F32)
        for h in range(N_HEADS):
            blk = slice(h * HEAD_PAD, (h + 1) * HEAD_PAD)
            q_h = _rope(q[:, blk], cos_v, sa, sb) * SM_SCALE
            q_ref[:, blk] = q_h.astype(BF16)
            qt_ref[blk, :] = q_h.T.astype(BF16)
            k_ref[:, blk] = (kv[:, blk] + kr_v).astype(BF16)
            v_ref[:, blk] = (kv[:, D_ATT + h * HEAD_PAD:D_ATT + (h + 1) * HEAD_PAD] + ones_lane).astype(BF16)

    tab = pl.BlockSpec((ts, HEAD_PAD), lambda i: (i, 0))
    wide = pl.BlockSpec((ts, D_ATT), lambda i: (i, 0))
    return pl.pallas_call(
        body, name="qkv_proj", grid=(s // ts,),
        in_specs=[pl.BlockSpec((ts, Q_LORA), lambda i: (i, 0)), pl.BlockSpec((ts, KV_LORA), lambda i: (i, 0)), tab,
                  pl.BlockSpec((Q_LORA, D_ATT), lambda i: (0, 0)), pl.BlockSpec((KV_LORA, 2 * D_ATT), lambda i: (0, 0)),
                  tab, tab, tab],
        out_specs=[wide, wide, wide, pl.BlockSpec((D_ATT, ts), lambda i: (0, i))],
        out_shape=[jax.ShapeDtypeStruct((s, D_ATT), BF16)] * 3 + [jax.ShapeDtypeStruct((D_ATT, s), BF16)],
        compiler_params=_params(("parallel",)),
    )(qn, kvn, kr, w_uq_p, w_kv_p, cos, sin_a, sin_b)


def _qk_bwd(dq, dk, cos, sin_a, sin_b, *, ts=512):
    s = dq.shape[0]

    def body(dq_ref, dk_ref, cos_ref, sa_ref, sb_ref, dqp_ref, dkr_ref):
        cos_v, sa, sb = cos_ref[...], sa_ref[...], sb_ref[...]
        tot = jnp.zeros((ts, HEAD_PAD), F32)
        for h in range(N_HEADS):
            blk = slice(h * HEAD_PAD, (h + 1) * HEAD_PAD)
            dqp_ref[:, blk] = _rope_bwd(dq_ref[:, blk], cos_v, sa, sb).astype(BF16)
            tot = tot + dk_ref[:, blk]
        dkr_ref[...] = tot

    tab = pl.BlockSpec((ts, HEAD_PAD), lambda i: (i, 0))
    wide = pl.BlockSpec((ts, D_ATT), lambda i: (i, 0))
    return pl.pallas_call(
        body, name="qk_bwd", grid=(s // ts,),
        in_specs=[wide, wide, tab, tab, tab], out_specs=[wide, tab],
        out_shape=[jax.ShapeDtypeStruct((s, D_ATT), BF16), jax.ShapeDtypeStruct((s, HEAD_PAD), F32)],
        compiler_params=_params(("parallel",)),
    )(dq, dk, cos, sin_a, sin_b)


_NT = (((1,), (1,)), ((), ()))


def _flash_fwd(q, k, v, *, tq=1024, tk=1024, per_trip=8, q_per_step=2):
    s = q.shape[0]
    tq, tk = min(tq, s), min(tk, s)
    nk = s // tk
    per_trip = min(per_trip, nk)
    assert nk % per_trip == 0
    q_per_step = min(q_per_step, s // tq)

    def body(q_ref, k_ref, v_ref, o_ref, lse_ref):
        for blk in range(q_per_step):
            rows = pl.ds(blk * tq, tq)
            one_block(q_ref.at[rows, :], k_ref, v_ref, o_ref.at[rows, :], lse_ref.at[rows, :])

    def one_block(q_ref, k_ref, v_ref, o_ref, lse_ref):
        qv = q_ref[...]

        def step(j, carry):
            m, acc = carry
            rows = pl.ds(pl.multiple_of(j * tk, tk), tk)
            sc = lax.dot_general(qv, k_ref[rows, :], _NT, preferred_element_type=F32)
            m_new = jnp.maximum(m, jnp.max(sc, axis=1, keepdims=True))
            p = jnp.exp(sc - m_new).astype(BF16)
            acc = jnp.exp(m - m_new) * acc + jnp.dot(p, v_ref[rows, :], preferred_element_type=F32)
            return m_new, acc

        def trip(t, carry):
            for c in range(per_trip):
                carry = step(per_trip * t + c, carry)
            return carry

        init = (jnp.full((tq, 1), -jnp.inf, F32), jnp.zeros((tq, HEAD_PAD), F32))
        m, acc = lax.fori_loop(0, nk // per_trip, trip, init)
        l = acc[:, ONES_LANE:ONES_LANE + 1]
        o_ref[...] = (acc / l).astype(BF16)
        lse_ref[...] = m + jnp.log(l)

    head = pl.BlockSpec((s, HEAD_PAD), lambda h, i: (0, h))
    tq_step = q_per_step * tq
    return pl.pallas_call(
        body, name="flash_fwd", grid=(N_HEADS, s // tq_step),
        in_specs=[pl.BlockSpec((tq_step, HEAD_PAD), lambda h, i: (i, h)), head, head],
        out_specs=[pl.BlockSpec((tq_step, HEAD_PAD), lambda h, i: (i, h)),
                   pl.BlockSpec((None, tq_step, 1), lambda h, i: (h, i, 0))],
        out_shape=[jax.ShapeDtypeStruct((s, D_ATT), BF16), jax.ShapeDtypeStruct((N_HEADS, s, 1), F32)],
        compiler_params=_params(("parallel", "parallel")),
    )(q, k, v)


def _attn_delta(do, o, *, ts=512):
    s = do.shape[0]

    def body(do_ref, o_ref, dl_ref):
        for h in range(N_HEADS):
            blk = slice(h * HEAD_PAD, (h + 1) * HEAD_PAD)
            dl_ref[h] = jnp.sum(do_ref[:, blk].astype(F32) * o_ref[:, blk].astype(F32), axis=1, keepdims=True)

    wide = pl.BlockSpec((ts, D_ATT), lambda i: (i, 0))
    return pl.pallas_call(
        body, name="attn_delta", grid=(s // ts,), in_specs=[wide, wide],
        out_specs=pl.BlockSpec((N_HEADS, ts, 1), lambda i: (0, i, 0)),
        out_shape=jax.ShapeDtypeStruct((N_HEADS, s, 1), F32),
        compiler_params=_params(("parallel",)),
    )(do, o)


def _flash_bwd(q, qt, k, v, do, dot, lse, delta, after, *, tq=1024, tk=512, per_trip=8, kv_per_step=2):
    s = q.shape[0]
    tq, tk = min(tq, s), min(tk, s)
    nq = s // tq
    per_trip = min(per_trip, nq)
    assert nq % per_trip == 0
    kv_per_step = min(kv_per_step, s // tk)

    def body(q_ref, qt_ref, do_ref, dot_ref, lse_ref, dl_ref, k_ref, v_ref, after_ref, dq_ref, dk_ref, dv_ref):
        j = pl.program_id(1)

        @pl.when(j == 0)
        def _():
            dq_ref[...] = jnp.zeros_like(dq_ref)

        for blk in range(kv_per_step):
            one_block(q_ref, qt_ref, do_ref, dot_ref, lse_ref, dl_ref, dq_ref,
                      k_ref.at[pl.ds(blk * tk, tk), :], v_ref.at[pl.ds(blk * tk, tk), :],
                      dk_ref.at[pl.ds(blk * tk, tk), :], dv_ref.at[pl.ds(blk * tk, tk), :])

        @pl.when(j == pl.num_programs(1) - 1)
        def _():
            dq_ref[...] *= SM_SCALE

    def one_block(q_ref, qt_ref, do_ref, dot_ref, lse_ref, dl_ref, dq_ref, k_ref, v_ref, dk_ref, dv_ref):
        kv, vv = k_ref[...], v_ref[...]

        def chunk(i, dk_t, dv_t):
            at = pl.multiple_of(i * tq, tq)
            rows = pl.ds(at, tq)
            sc = lax.dot_general(q_ref[rows, :], kv, _NT, preferred_element_type=F32)
            p = jnp.exp(sc - lse_ref[rows, :])
            dp = lax.dot_general(do_ref[rows, :], vv, _NT, preferred_element_type=F32)
            ds = (p * (dp - dl_ref[rows, :])).astype(BF16)
            dv_t = dv_t + jnp.dot(dot_ref[:, rows], p.astype(BF16), preferred_element_type=F32)
            dk_t = dk_t + jnp.dot(qt_ref[:, rows], ds, preferred_element_type=F32)
            dq_ref[rows, :] += jnp.dot(ds, kv, preferred_element_type=F32)
            return dk_t, dv_t

        def step(i, carry):
            for c in range(per_trip):
                carry = chunk(per_trip * i + c, *carry)
            return carry

        zero = jnp.zeros((HEAD_PAD, tk), F32)
        dk_t, dv_t = lax.fori_loop(0, nq // per_trip, step, (zero, zero))
        dk_ref[...] = dk_t.T
        dv_ref[...] = dv_t.T

    head = pl.BlockSpec((s, HEAD_PAD), lambda h, j: (0, h))
    head_t = pl.BlockSpec((HEAD_PAD, s), lambda h, j: (h, 0))
    stat = pl.BlockSpec((None, s, 1), lambda h, j: (h, 0, 0))
    blk = pl.BlockSpec((kv_per_step * tk, HEAD_PAD), lambda h, j: (j, h))
    return pl.pallas_call(
        body, name="flash_bwd", grid=(N_HEADS, s // (kv_per_step * tk)),
        in_specs=[head, head_t, head, head_t, stat, stat, blk, blk, ANY],
        out_specs=[head, blk, blk],
        out_shape=[jax.ShapeDtypeStruct((s, D_ATT), F32)] * 3,
        compiler_params=_params(("parallel", "arbitrary")),
    )(q, qt, do, dot, lse, delta, k, v, after)


FFN_TC = 256
FFN_TG = 1408


FFN_HALO_BF16 = 16
FFN_HALO_F32 = 8


def _row_halo_specs(ts, s, halo, width):
    nb = ts // halo
    last = s // halo - 1
    prev = pl.BlockSpec((halo, width), lambda i, j: (jnp.maximum(i * nb - 1, 0), 0))
    nxt = pl.BlockSpec((halo, width), lambda i, j: (jnp.minimum((i + 1) * nb, last), 0))
    return prev, nxt


def _ext_rows(prev, main, nxt, first, last):
    return jnp.concatenate([jnp.where(first, jnp.zeros_like(prev), prev), main,
                            jnp.where(last, jnp.zeros_like(nxt), nxt)], axis=0)


def _ext_conv(a, w):
    a_dn = pltpu.roll(a, 1, 0)
    a_up = pltpu.roll(a, a.shape[0] - 1, 0)
    return a_dn * w[0:1, :] + a * w[1:2, :] + a_up * w[2:3, :], a_dn, a_up


def _ffn_pieces(tg):
    return [(off, min(FFN_TC, tg - off)) for off in range(0, tg, FFN_TC)]


def _ffn_fwd(hf, w_up, w, b, *, ts=1024, tg=FFN_TG):
    s = hf.shape[0]
    n, ng, halo = s // ts, D_FF // tg, FFN_HALO_BF16

    def body(h_ref, hp_ref, hn_ref, wg_ref, wu_ref, cw_ref, cb_ref, a_ref, act_ref):
        i = pl.program_id(0)
        ext = _ext_rows(hp_ref[...], h_ref[...], hn_ref[...], i == 0, i == n - 1)
        for off, width in _ffn_pieces(tg):
            cols = slice(off, off + width)
            gate_up = []
            for half, w_ref in enumerate((wg_ref, wu_ref)):
                a_ext = jnp.dot(ext, w_ref[:, cols], preferred_element_type=F32)
                a_ref[half, :, cols] = a_ext[halo:halo + ts]
                conv = _ext_conv(a_ext, cw_ref[half, :, cols])[0]
                gate_up.append(conv[halo:halo + ts] + cb_ref[half, :, cols])
            g, u = gate_up
            act_ref[:, cols] = (g * _sigmoid(g) * u).astype(BF16)

    prev, nxt = _row_halo_specs(ts, s, halo, D_MODEL)
    return pl.pallas_call(
        body, name="ffn_fwd", grid=(n, ng),
        in_specs=[pl.BlockSpec((ts, D_MODEL), lambda i, j: (i, 0)), prev, nxt,
                  pl.BlockSpec((D_MODEL, tg), lambda i, j: (0, j)), pl.BlockSpec((D_MODEL, tg), lambda i, j: (0, j + ng)),
                  pl.BlockSpec((2, 8, tg), lambda i, j: (0, 0, j)), pl.BlockSpec((2, 1, tg), lambda i, j: (0, 0, j))],
        out_specs=[pl.BlockSpec((2, ts, tg), lambda i, j: (0, i, j)), pl.BlockSpec((ts, tg), lambda i, j: (i, j))],
        out_shape=[jax.ShapeDtypeStruct((2, s, D_FF), F32), jax.ShapeDtypeStruct((s, D_FF), BF16)],
        compiler_params=_params(("parallel", "parallel")),
    )(hf, hf, hf, w_up, w_up, w, b)


def _ffn_bwd(dx2, w_down, a_pre, w, b, *, ts=1024, tg=FFN_TG):
    s = dx2.shape[0]
    n, ng, halo = s // ts, D_FF // tg, FFN_HALO_F32
    main = slice(halo, halo + ts)

    def body(dx_ref, dxp_ref, dxn_ref, wd_ref, a_ref, ap_ref, an_ref, cw_ref, cb_ref, o_ref, dw_ref, db_ref):
        i, j = pl.program_id(0), pl.program_id(1)
        first, last = i == 0, i == n - 1

        @pl.when(first & (j == 0))
        def _():
            dw_ref[...] = jnp.zeros_like(dw_ref)
            db_ref[...] = jnp.zeros_like(db_ref)

        dx_ext = _ext_rows(dxp_ref[...], dx_ref[...], dxn_ref[...], first, last).astype(BF16)
        for off, width in _ffn_pieces(tg):
            cols = slice(off, off + width)
            dact = lax.dot_general(dx_ext, wd_ref[cols, :], _NT, preferred_element_type=F32)
            halves = []
            for half in range(2):
                a_ext = _ext_rows(ap_ref[half, :, cols], a_ref[half, :, cols], an_ref[half, :, cols], first, last)
                conv, a_dn, a_up = _ext_conv(a_ext, cw_ref[half, :, cols])
                halves.append((conv + cb_ref[half, :, cols], a_dn, a_ext, a_up))
            g, u = halves[0][0], halves[1][0]
            sg = _sigmoid(g)
            grads = (dact * u * (sg * (1.0 + g * (1.0 - sg))), dact * (g * sg))
            for half in range(2):
                d = grads[half]
                _, a_dn, a_ext, a_up = halves[half]
                wv = cw_ref[half, :, cols]
                d_pre = (pltpu.roll(d, d.shape[0] - 1, 0) * wv[0:1, :] + d * wv[1:2, :]
                         + pltpu.roll(d, 1, 0) * wv[2:3, :])
                o_ref[half, :, cols] = d_pre[main].astype(BF16)
                dm = d[main]
                dw_ref[j, half, 0:1, cols] += jnp.sum(dm * a_dn[main], axis=0, keepdims=True)
                dw_ref[j, half, 1:2, cols] += jnp.sum(dm * a_ext[main], axis=0, keepdims=True)
                dw_ref[j, half, 2:3, cols] += jnp.sum(dm * a_up[main], axis=0, keepdims=True)
                db_ref[j, half, :, cols] += jnp.sum(dm, axis=0, keepdims=True)

    dxp, dxn = _row_halo_specs(ts, s, halo, D_MODEL)
    nb, lastb = ts // halo, s // halo - 1
    a_main = pl.BlockSpec((2, ts, tg), lambda i, j: (0, i, j))
    a_prev = pl.BlockSpec((2, halo, tg), lambda i, j: (0, jnp.maximum(i * nb - 1, 0), j))
    a_next = pl.BlockSpec((2, halo, tg), lambda i, j: (0, jnp.minimum((i + 1) * nb, lastb), j))
    da_pre, dw, db = pl.pallas_call(
        body, name="ffn_bwd", grid=(n, ng),
        in_specs=[pl.BlockSpec((ts, D_MODEL), lambda i, j: (i, 0)), dxp, dxn,
                  pl.BlockSpec((tg, D_MODEL), lambda i, j: (j, 0)), a_main, a_prev, a_next,
                  pl.BlockSpec((2, 8, tg), lambda i, j: (0, 0, j)), pl.BlockSpec((2, 1, tg), lambda i, j: (0, 0, j))],
        out_specs=[a_main, pl.BlockSpec((ng, 2, 8, tg), lambda i, j: (0, 0, 0, 0)),
                   pl.BlockSpec((ng, 2, 1, tg), lambda i, j: (0, 0, 0, 0))],
        out_shape=[jax.ShapeDtypeStruct((2, s, D_FF), BF16), jax.ShapeDtypeStruct((ng, 2, 8, tg), F32),
                   jax.ShapeDtypeStruct((ng, 2, 1, tg), F32)],
        compiler_params=_params(("arbitrary", "arbitrary")),
    )(dx2, dx2, dx2, w_down, a_pre, a_pre, a_pre, w, b)
    return (da_pre, dw.transpose(1, 2, 0, 3).reshape(2, 8, D_FF), db.transpose(1, 2, 0, 3).reshape(2, 1, D_FF))


def _ple_final(x2, n3, p, target, gf, w_pg, w_pp, *, ts=512):
    s, d = x2.shape
    dp = p.shape[1]

    def body(x2_ref, n3_ref, p_ref, t_ref, gf_ref, wg_ref, wp_ref, loss_ref, dx3_ref, dgl_ref, dpp_ref, dgf_ref):
        @pl.when(pl.program_id(0) == 0)
        def _():
            loss_ref[...] = jnp.zeros_like(loss_ref)
            dgf_ref[...] = jnp.zeros_like(dgf_ref)

        gate = _sigmoid(jnp.dot(n3_ref[...], wg_ref[...], preferred_element_type=F32))
        ppv = jnp.dot(p_ref[...].astype(BF16), wp_ref[...], preferred_element_type=F32)
        x3 = x2_ref[...] + gate * ppv
        gfv = gf_ref[...]
        err = x3 * _rms_scale(x3) * gfv - t_ref[...]
        loss_ref[...] += 0.5 * jnp.sum(jnp.mean(err * err, axis=-1, keepdims=True), axis=0, keepdims=True)
        dx3, dgf_rows = _rms_bwd_rows(x3, gfv, err * (1.0 / d))
        dgf_ref[...] += jnp.sum(dgf_rows, axis=0, keepdims=True)
        dx3_ref[...] = dx3
        dgl_ref[...] = (dx3 * ppv * gate * (1.0 - gate)).astype(BF16)
        dpp_ref[...] = (dx3 * gate).astype(BF16)

    row = pl.BlockSpec((ts, d), lambda i: (i, 0))
    vec = pl.BlockSpec((1, d), lambda i: (0, 0))
    return pl.pallas_call(
        body, name="ple_final", grid=(s // ts,),
        in_specs=[row, row, pl.BlockSpec((ts, dp), lambda i: (i, 0)), row, vec,
                  pl.BlockSpec((d, d), lambda i: (0, 0)), pl.BlockSpec((dp, d), lambda i: (0, 0))],
        out_specs=[pl.BlockSpec((1, 128), lambda i: (0, 0)), row, row, row, vec],
        out_shape=[jax.ShapeDtypeStruct((1, 128), F32), jax.ShapeDtypeStruct((s, d), F32),
                   jax.ShapeDtypeStruct((s, d), BF16), jax.ShapeDtypeStruct((s, d), BF16),
                   jax.ShapeDtypeStruct((1, d), F32)],
        compiler_params=_params(("arbitrary",)),
    )(x2, n3, p, target, gf, w_pg, w_pp)


def _row_tile(rows, cols, n_arrays, budget=12 << 20):
    best = None
    for t in range(8, rows + 1, 8):
        if rows % t == 0 and t * cols * 4 * n_arrays <= budget:
            best = t
    return rows if best is None else best


def _sum_slots(a, *, name):
    g, r, c = a.shape
    tr = _row_tile(r, c, g + 1)

    def body(*refs):
        tot = refs[0][...]
        for ref in refs[1:g]:
            tot = tot + ref[...]
        refs[g][...] = tot

    specs = [pl.BlockSpec((None, tr, c), functools.partial(lambda i, slot: (slot, i, 0), slot=k)) for k in range(g)]
    return pl.pallas_call(
        body, name=name, grid=(r // tr,), in_specs=specs, out_specs=pl.BlockSpec((tr, c), lambda i: (i, 0)),
        out_shape=jax.ShapeDtypeStruct((r, c), a.dtype), compiler_params=_params(("parallel",)),
    )(*([a] * g))


def _adamw_refs(w_ref, g_ref, m_ref, v_ref, d_ref, mo_ref, vo_ref):
    gv = g_ref[...]
    mn = ADAM_B1 * m_ref[...] + (1.0 - ADAM_B1) * gv
    vn = ADAM_B2 * v_ref[...] + (1.0 - ADAM_B2) * (gv * gv)
    m_hat = mn / (1.0 - ADAM_B1 ** ADAM_STEP)
    v_hat = vn / (1.0 - ADAM_B2 ** ADAM_STEP)
    d_ref[...] = -ADAM_LR * (m_hat / (jnp.sqrt(v_hat) + ADAM_EPS) + ADAM_WD * w_ref[...])
    mo_ref[...] = mn
    vo_ref[...] = vn


def _adamw_many(ws, gs, ms, vs, *, name):
    n = len(ws)

    def body(*refs):
        ins, outs = refs[:4 * n], refs[4 * n:]
        for a in range(n):
            _adamw_refs(ins[a], ins[n + a], ins[2 * n + a], ins[3 * n + a], outs[3 * a], outs[3 * a + 1],
                        outs[3 * a + 2])

    vm = pl.BlockSpec(memory_space=pltpu.VMEM)
    res = pl.pallas_call(
        body, name=name, in_specs=[vm] * (4 * n), out_specs=[vm] * (3 * n),
        out_shape=[jax.ShapeDtypeStruct(a.shape, F32) for a in ws for _ in range(3)],
    )(*ws, *gs, *ms, *vs)
    return [tuple(res[3 * a:3 * a + 3]) for a in range(n)]


def _adamw(w, g, m, v, *, name):
    r, c = w.shape
    tr = _row_tile(r, c, 7)
    body = _adamw_refs

    blk = pl.BlockSpec((tr, c), lambda i: (i, 0))
    return pl.pallas_call(
        body, name=name, grid=(r // tr,), in_specs=[blk] * 4, out_specs=[blk] * 3,
        out_shape=[jax.ShapeDtypeStruct((r, c), F32)] * 3, compiler_params=_params(("parallel",)),
    )(w, g, m, v)


def _position():
    x, y, c = lax.axis_index("x"), lax.axis_index("y"), lax.axis_index("c")
    return x, y, c


def _other_chips(x, y):
    return [(1 - x, y), (x, 1 - y), (1 - x, 1 - y)]


def _stage_in(srcs, stage, sems):
    cps = [pltpu.make_async_copy(src, stage[a], sems.at[a]) for a, src in enumerate(srcs)]
    for cp in cps:
        cp.start()
    return cps


def _stage_out(staged, stage, dsts, sems):
    cps = []
    for a, dst in enumerate(dsts):
        staged[a].wait()
        cp = pltpu.make_async_copy(stage[a], dst, sems.at[a])
        cp.start()
        cps.append(cp)
    return cps


def _send_other_halves(grads, *, tag):
    n = len(grads)

    def body(*refs):
        ins, sib = refs[:n], refs[n:2 * n]
        send_sems, recv_sems = refs[2 * n:]
        x, y, c = _position()
        remote = []
        for a in range(n):
            half = ins[a].shape[1] // 2
            give = ins[a].at[:, pl.ds(pl.multiple_of((1 - c) * half, 8), half), :]
            rc = pltpu.make_async_remote_copy(
                src_ref=give, dst_ref=sib[a], send_sem=send_sems.at[a], recv_sem=recv_sems.at[a],
                device_id=(x, y, 1 - c), device_id_type=MESH)
            rc.start()
            remote.append(rc)
        for rc in remote:
            rc.wait_recv()
        for rc in remote:
            rc.wait_send()

    return pl.pallas_call(
        body, name="send_other_halves_" + tag, in_specs=[ANY] * n, out_specs=[ANY] * n,
        out_shape=[jax.ShapeDtypeStruct((g.shape[0], g.shape[1] // 2, g.shape[2]), g.dtype) for g in grads],
        scratch_shapes=[pltpu.SemaphoreType.DMA((n,)), pltpu.SemaphoreType.DMA((n,))],
        compiler_params=pltpu.CompilerParams(has_side_effects=True),
    )(*grads)


def _add_own_half(g4, sib, core, *, name):
    g, a2, c = sib.shape
    tr = _row_tile(a2, c, 4)

    def body(core_ref, a_ref, b_ref, o_ref, o16_ref):
        tot = a_ref[...] + b_ref[...]
        o_ref[...] = tot
        o16_ref[...] = tot.astype(BF16)

    blk = pl.BlockSpec((None, tr, c), lambda i, j, core_ref: (i, j, 0))
    return pl.pallas_call(
        body, name=name,
        grid_spec=pltpu.PrefetchScalarGridSpec(
            num_scalar_prefetch=1, grid=(g, a2 // tr),
            in_specs=[pl.BlockSpec((None, None, tr, c), lambda i, j, core_ref: (i, core_ref[0], j, 0)), blk],
            out_specs=[blk, blk]),
        out_shape=[jax.ShapeDtypeStruct(sib.shape, F32), jax.ShapeDtypeStruct(sib.shape, BF16)],
        compiler_params=_params(("parallel", "parallel")),
    )(core, g4.reshape(g, 2, a2, c), sib)


def _sum_chips(landed, own, chip, *, name):
    g, r, c = landed.shape
    tr = _row_tile(r, c, 5)

    def body(chip_ref, *refs):
        me = chip_ref[0]
        own_v = refs[g][...]
        tot = None
        for slot in range(g):
            term = jnp.where(me == slot, own_v, refs[slot][...].astype(F32))
            tot = term if tot is None else tot + term
        refs[g + 1][...] = tot

    def landed_spec(slot):
        return pl.BlockSpec((None, tr, c),
                            lambda i, chip_ref: (jnp.where(chip_ref[0] == slot, (slot + 1) % g, slot), i, 0))

    return pl.pallas_call(
        body, name=name,
        grid_spec=pltpu.PrefetchScalarGridSpec(
            num_scalar_prefetch=1, grid=(r // tr,),
            in_specs=[landed_spec(k) for k in range(g)]
            + [pl.BlockSpec((None, tr, c), lambda i, chip_ref: (chip_ref[0], i, 0))],
            out_specs=pl.BlockSpec((tr, c), lambda i, chip_ref: (i, 0))),
        out_shape=jax.ShapeDtypeStruct((r, c), F32), compiler_params=_params(("parallel",)),
    )(chip, *([landed] * g), own)


def _join_halves(halves):
    n = len(halves)

    def body(*refs):
        ins, outs, stage = refs[:n], refs[n:2 * n], refs[2 * n:3 * n]
        send_sems, recv_sems, in_sems, out_sems = refs[3 * n:]
        x, y, c = _position()
        remote = []
        staged = _stage_in(ins, stage, in_sems)
        for a in range(n):
            rc = pltpu.make_async_remote_copy(
                src_ref=ins[a], dst_ref=outs[a].at[c], send_sem=send_sems.at[a], recv_sem=recv_sems.at[a],
                device_id=(x, y, 1 - c), device_id_type=MESH)
            rc.start()
            remote.append(rc)
        local = _stage_out(staged, stage, [o.at[c] for o in outs], out_sems)
        for a in range(n):
            pltpu.make_async_remote_copy(
                src_ref=ins[a], dst_ref=outs[a].at[1 - c], send_sem=send_sems.at[a], recv_sem=recv_sems.at[a],
                device_id=(x, y, 1 - c), device_id_type=MESH).wait_recv()
        for rc in remote:
            rc.wait_send()
        for cp in local:
            cp.wait()

    return pl.pallas_call(
        body, name="join_halves", in_specs=[ANY] * n, out_specs=[ANY] * n,
        out_shape=[jax.ShapeDtypeStruct((2,) + h.shape, h.dtype) for h in halves],
        scratch_shapes=[pltpu.VMEM(h.shape, h.dtype) for h in halves]
        + [pltpu.SemaphoreType.DMA((n,)), pltpu.SemaphoreType.DMA((n,)), pltpu.SemaphoreType.DMA((n,)),
           pltpu.SemaphoreType.DMA((n,))],
        compiler_params=pltpu.CompilerParams(has_side_effects=True),
    )(*halves)


_HBM = pl.BlockSpec(memory_space=pltpu.HBM)
_SEM = pl.BlockSpec(memory_space=pltpu.SEMAPHORE)


def _chip_copies(srcs, lands, send_sems, recv_sems, scatter):
    x, y, c = _position()
    me = 2 * x + y
    outgoing, incoming = [], []
    for a, (src, land) in enumerate(zip(srcs, lands)):
        for k, (px, py) in enumerate(_other_chips(x, y)):
            peer = 2 * px + py
            sems = dict(send_sem=send_sems.at[3 * a + k], recv_sem=recv_sems.at[3 * a + k], device_id=(px, py, c),
                        device_id_type=MESH)
            outgoing.append(pltpu.make_async_remote_copy(
                src_ref=src.at[peer] if scatter else src, dst_ref=land.at[me], **sems))
            incoming.append(pltpu.make_async_remote_copy(
                src_ref=src.at[me] if scatter else src, dst_ref=land.at[peer], **sems))
    return outgoing, incoming


def _chips_start(srcs, *, scatter, name):
    n = len(srcs)
    lands = [lax.empty(a.shape if scatter else (N_CHIPS,) + a.shape, a.dtype) for a in srcs]

    def body(*refs):
        ins, send_sems, recv_sems, token = refs[:2 * n], refs[2 * n], refs[2 * n + 1], refs[-1]
        outgoing, _ = _chip_copies(ins[:n], ins[n:], send_sems, recv_sems, scatter)
        for cp in outgoing:
            cp.start()
        token[...] = jnp.zeros_like(token)

    bufs = list(srcs) + lands
    res = pl.pallas_call(
        body, name=name, in_specs=[_HBM] * (2 * n),
        out_specs=(_SEM, _SEM, *[_HBM] * (2 * n), pl.BlockSpec(memory_space=pltpu.VMEM)),
        out_shape=(pltpu.SemaphoreType.DMA((3 * n,)), pltpu.SemaphoreType.DMA((3 * n,)),
                   *[pltpu.HBM(a.shape, a.dtype) for a in bufs], jax.ShapeDtypeStruct((8, 128), F32)),
        input_output_aliases={i: 2 + i for i in range(2 * n)},
        compiler_params=pltpu.CompilerParams(has_side_effects=pltpu.SideEffectType.DATAFLOW_SIDE_EFFECTING),
    )(*[pltpu.with_memory_space_constraint(a, pltpu.HBM) for a in bufs])
    return res[0], res[1], list(res[2:2 + n]), list(res[2 + n:2 + 2 * n]), res[-1]


def _chips_wait(handle, after, *, scatter, name):
    send_sems, recv_sems, srcs, lands, _ = handle
    n = len(srcs)

    def body(*refs):
        ins, send_ref, recv_ref = refs[:2 * n], refs[2 * n], refs[2 * n + 1]
        outgoing, incoming = _chip_copies(ins[:n], ins[n:], send_ref, recv_ref, scatter)
        for cp in outgoing:
            cp.wait_send()
        for cp in incoming:
            cp.wait_recv()

    bufs = list(srcs) + list(lands)
    res = pl.pallas_call(
        body, name=name, in_specs=[_HBM] * (2 * n) + [_SEM, _SEM, ANY], out_specs=tuple([_HBM] * (2 * n)),
        out_shape=tuple(pltpu.HBM(a.shape, a.dtype) for a in bufs),
        input_output_aliases={i: i for i in range(2 * n)},
        compiler_params=pltpu.CompilerParams(has_side_effects=pltpu.SideEffectType.DATAFLOW_SIDE_EFFECTING),
    )(*bufs, send_sems, recv_sems, after)
    return list(res[:n]), list(res[n:])


def _gather_all(buf):
    def body(in_ref, out_ref, send_sems, recv_sems, local_sem):
        x, y, c = _position()
        me = 4 * x + 2 * y + c
        peers = [(x, y, 1 - c)] + [(px, py, pc) for (px, py) in _other_chips(x, y) for pc in (c, 1 - c)]
        cp = pltpu.make_async_copy(in_ref, out_ref.at[me], local_sem)
        cp.start()
        remote = []
        for k, peer in enumerate(peers):
            rc = pltpu.make_async_remote_copy(
                src_ref=in_ref, dst_ref=out_ref.at[me], send_sem=send_sems.at[k], recv_sem=recv_sems.at[k],
                device_id=peer, device_id_type=MESH)
            rc.start()
            remote.append(rc)
        for k, (px, py, pc) in enumerate(peers):
            pltpu.make_async_remote_copy(
                src_ref=in_ref, dst_ref=out_ref.at[4 * px + 2 * py + pc], send_sem=send_sems.at[k],
                recv_sem=recv_sems.at[k], device_id=(px, py, pc), device_id_type=MESH).wait_recv()
        for rc in remote:
            rc.wait_send()
        cp.wait()

    return pl.pallas_call(
        body, name="gather_all", in_specs=[ANY], out_specs=ANY,
        out_shape=jax.ShapeDtypeStruct((N_DEV,) + buf.shape, buf.dtype),
        scratch_shapes=[pltpu.SemaphoreType.DMA((N_DEV - 1,)), pltpu.SemaphoreType.DMA((N_DEV - 1,)),
                        pltpu.SemaphoreType.DMA],
        compiler_params=pltpu.CompilerParams(has_side_effects=True),
    )(buf)


def _cols_from_shards(g4):
    _, k, n = g4.shape
    return g4.transpose(1, 0, 2).reshape(k, N_CHIPS * n)


def _cols_to_shards(w):
    k, n = w.shape
    return w.reshape(k, N_CHIPS, n // N_CHIPS).transpose(1, 0, 2)


def _pad_heads(w, width):
    k = w.shape[0]
    w3 = w.reshape(k, N_HEADS, width)
    return jnp.pad(w3, ((0, 0), (0, 0), (0, HEAD_PAD - width))).reshape(k, D_ATT)


def _unpad_heads(w, width):
    k = w.shape[0]
    return w.reshape(k, N_HEADS, HEAD_PAD)[:, :, :width]


def _rope_tables(s, after):
    pos = jnp.arange(s, dtype=F32) + after
    inv_freq = ROPE_THETA ** (-jnp.arange(0, QK_ROPE, 2, dtype=F32) / QK_ROPE)
    ang = pos[:, None] * inv_freq[None, :]
    cos_h, sin_h = jnp.cos(ang), jnp.sin(ang)
    half = QK_ROPE // 2
    z = jnp.zeros((s, half), F32)
    ones = jnp.ones((s, QK_NOPE), F32)
    tail = jnp.zeros((s, HEAD_PAD - QK_NOPE - QK_ROPE), F32)
    cos = jnp.concatenate([ones, cos_h, cos_h, tail + 1.0], axis=1)
    sin_a = jnp.concatenate([ones * 0.0, -sin_h, z, tail], axis=1)
    sin_b = jnp.concatenate([ones * 0.0, z, sin_h, tail], axis=1)
    return cos, sin_a, sin_b


def _local_step(x, p, target, wts, late_weights, reduce_early, reduce_last):
    s = x.shape[0]
    cos, sin_a, sin_b = wts["rope"]
    g1, gq, gkv, g2, g3, gf = (wts[k] for k in ("norm_mix_g", "q_norm_g", "kv_norm_g", "norm_ffn_g", "ple_norm_g",
                                                 "final_norm_g"))
    w_in_p, w_uq_p, w_kv_p = wts["w_in_p"], wts["w_uq_p"], wts["w_kv_p"]
    conv_w8, fconv_w, fconv_b = wts["conv_w8"], wts["ffn_conv_w"], wts["ffn_conv_b"]

    (h, z), _ = _mm_fused(x, w_in_p, name="mm_in", prologue=_pro_rms, vecs=[g1], epilogue=_epi_plain, row_outs=[F32])
    y_conv, qn, kvn, kr = _mix_pre(z, conv_w8, gq, gkv, cos, sin_a, sin_b)
    q, k, v, q_t = _qkv_proj(qn, kvn, kr, w_uq_p, w_kv_p, cos, sin_a, sin_b)
    o, lse = _flash_fwd(q, k, v)
    late = late_weights(lse)
    w_o_a, w_o_b, w_up, w_down = late["w_o_a"], late["w_o_b"], late["w_up"], late["w_down"]
    w_pg, w_pp = late["w_ple_gate"], late["w_ple_proj"]
    (x1, hf), _ = _mm_fused(o, w_o_b, second=(y_conv, w_o_a), name="mm_o", rows=[x], vecs=[g2],
                            epilogue=_epi_add_rms, row_outs=[F32, BF16])
    a_pre, act = _ffn_fwd(hf, w_up, fconv_w, fconv_b)
    (x2, n3), _ = _mm_fused(act, w_down, name="mm_down", rows=[x1], vecs=[g3], epilogue=_epi_add_rms,
                            row_outs=[F32, BF16])
    loss, dx3, dgl, dpp, d_gf = _ple_final(x2, n3, p, target, gf, w_pg, w_pp)

    grads, early = {"final_norm_g": d_gf}, {}
    early["w_ple_proj"] = _mm(p, dpp, ta=True, name="mm_d_wpp", tm=256, tn=1024, tk=2048)
    early["w_ple_gate"] = _mm(n3, dgl, ta=True, name="mm_d_wpg", tm=1024, tn=1024, tk=2048)
    (dx2,), (grads["ple_norm_g"],) = _mm_fused(dgl, w_pg, tb=True, name="mm_d_n3", rows=[x2, dx3], vecs=[g3],
                                               epilogue=_epi_rms_bwd, row_outs=[F32], n_vec_out=1)
    early["w_down"] = _mm(act, dx2, ta=True, name="mm_d_wdown", tm=1408, tn=1024, tk=2048)
    da_pre, grads["ffn_conv_w"], grads["ffn_conv_b"] = _ffn_bwd(dx2, w_down, a_pre, fconv_w, fconv_b)
    early["w_up"] = _mm(hf, da_pre, ta=True, b_split=True, name="mm_d_wup", tm=1024, tn=1408, tk=2048,
                       o_shards=True)
    (dx1,), (grads["norm_ffn_g"],) = _mm_fused(da_pre, w_up, tb=True, a_split=True, name="mm_d_hf", rows=[x1, dx2],
                                               vecs=[g2], epilogue=_epi_rms_bwd, row_outs=[F32], n_vec_out=1)
    d_wo_a = _mm(y_conv, dx1, ta=True, name="mm_d_wo_conv", tm=512, tn=1024, tk=2048)
    d_wo_b = _mm(o, dx1, ta=True, name="mm_d_wo_att", tm=1024, tn=1024, tk=2048)
    early["w_o"] = jnp.concatenate([d_wo_a, d_wo_b.reshape(N_HEADS, HEAD_PAD, D_MODEL)[:, :V_HEAD]
                                    .reshape(N_HEADS * V_HEAD, D_MODEL)], axis=0)
    token, finish = reduce_early(early)
    dyc = _mm(dx1, w_o_a, tb=True, name="mm_d_yconv", tm=512, tn=512, tk=1024)
    (do, do_t), _ = _mm_fused(dx1, w_o_b, tb=True, name="mm_d_o", epilogue=_epi_plain, row_outs=[BF16],
                              transposed_out=BF16)
    delta = _attn_delta(do, o)
    dq, dk, dv = _flash_bwd(q, q_t, k, v, do, do_t, lse, delta, token)
    reduced_early = finish(dq)
    dq_pre, dkr = _qk_bwd(dq, dk, cos, sin_a, sin_b)
    grads["w_uq_p"] = _mm(qn, dq_pre, ta=True, name="mm_d_wuq", tm=256, tn=1024, tk=2048)
    dqn = _mm(dq_pre, w_uq_p, tb=True, name="mm_d_qn", tm=512, tn=256, tk=1024)
    grads["w_k_p"] = _mm(kvn, dk, ta=True, name="mm_d_wk", tm=128, tn=1024, tk=2048)
    grads["w_v_p"] = _mm(kvn, dv, ta=True, name="mm_d_wv", tm=128, tn=1024, tk=2048)
    (dkvn,), _ = _mm_fused(dk, w_kv_p[:, :D_ATT], tb=True, second=(dv, w_kv_p[:, D_ATT:].T), name="mm_d_kvn",
                           epilogue=_epi_plain, row_outs=[F32])
    dz, grads["conv_w"], grads["q_norm_g"], grads["kv_norm_g"] = _mix_bwd(
        z, dyc, dqn, dkvn, dkr, conv_w8, gq, gkv, cos, sin_a, sin_b)
    grads["w_in_p"] = _mm(h, dz, ta=True, name="mm_d_win", tm=1024, tn=1024, tk=2048)
    token, finish = reduce_last({n: grads.pop(n) for n in ("w_in_p", "w_uq_p", "w_k_p", "w_v_p")})
    (grad_x,), (grads["norm_mix_g"],) = _mm_fused(dz, w_in_p, tb=True, name="mm_d_h", rows=[x, dx1],
                                                  vecs=[g1 + token[0, 0]], epilogue=_epi_rms_bwd, row_outs=[F32],
                                                  n_vec_out=1)
    return loss[0, 0], grad_x, grads, reduced_early, finish(grad_x)


_EARLY_W = ("w_in", "w_uq", "w_ukv")
_LATE_W = ("w_o", "w_up", "w_down", "w_ple_gate", "w_ple_proj")
_BIG = _EARLY_W + _LATE_W
_COL_SHARDED = ("w_in", "w_uq", "w_ukv", "w_up", "w_ple_proj")
_SMALL = ("norm_mix_g", "conv_w", "q_norm_g", "kv_norm_g", "norm_ffn_g", "ffn_conv_w", "ffn_conv_b", "ple_norm_g",
          "final_norm_g")


def _full_from_slots(n, g4):
    return _cols_from_shards(g4) if n in _COL_SHARDED else g4.reshape(-1, g4.shape[2])


def _shard_major(n, g):
    if g.ndim == 3:
        return g
    return _cols_to_shards(g) if n in _COL_SHARDED else g.reshape(N_CHIPS, g.shape[0] // N_CHIPS, g.shape[1])


def _early_shards(w):
    shards = [w[n][0].astype(BF16) for n in _EARLY_W]
    shards.append(jnp.pad(w["conv_w"][0], ((0, 5), (0, 0))))
    shards.append(jnp.pad(w["ffn_conv_w"][0], ((0, 5), (0, 0))))
    return shards


def _fill_own_slot(landed, own, chip):
    return [lax.dynamic_update_slice(g4, a[None], (chip[0], 0, 0)) for g4, a in zip(landed, own)]


def _early_weights(got, w):
    full = {n: _full_from_slots(n, g4) for n, g4 in zip(_EARLY_W, got)}
    full["conv_w8"] = _cols_from_shards(got[len(_EARLY_W)])
    full["ffn_conv_w8"] = _cols_from_shards(got[len(_EARLY_W) + 1])
    return _layout_early(full, w)


def _layout_early(full, w):
    out = {n: w[n] for n in ("norm_mix_g", "q_norm_g", "kv_norm_g", "norm_ffn_g", "ple_norm_g")}
    out["final_norm_g"] = w["final_norm_g"][None, :]
    w_in = full["w_in"]
    zc = jnp.zeros((D_MODEL, QK_NOPE), BF16)
    zt = jnp.zeros((D_MODEL, HEAD_PAD - QK_NOPE - QK_ROPE), BF16)
    out["w_in_p"] = jnp.concatenate([w_in[:, :D_IN - QK_ROPE], zc, w_in[:, D_IN - QK_ROPE:], zt], axis=1)
    out["w_uq_p"] = _pad_heads(full["w_uq"], QK_NOPE + QK_ROPE)
    kv3 = full["w_ukv"].reshape(KV_LORA, N_HEADS, QK_NOPE + V_HEAD)
    out["w_kv_p"] = jnp.concatenate([_pad_heads(kv3[:, :, :QK_NOPE].reshape(KV_LORA, -1), QK_NOPE),
                                     _pad_heads(kv3[:, :, QK_NOPE:].reshape(KV_LORA, -1), V_HEAD)], axis=1)
    out["conv_w8"] = full["conv_w8"]
    fw = full["ffn_conv_w8"]
    out["ffn_conv_w"] = jnp.stack([fw[:, :D_FF], fw[:, D_FF:]])
    out["ffn_conv_b"] = w["ffn_conv_b"].reshape(2, 1, D_FF)
    return out


def _layout_late(full):
    w_o = full["w_o"]
    out = {"w_o_a": w_o[:CONV_WIDTH]}
    out["w_o_b"] = jnp.pad(w_o[CONV_WIDTH:].reshape(N_HEADS, V_HEAD, D_MODEL),
                           ((0, 0), (0, HEAD_PAD - V_HEAD), (0, 0))).reshape(D_ATT, D_MODEL)
    for n in ("w_up", "w_down", "w_ple_gate", "w_ple_proj"):
        out[n] = full[n]
    return out


def _true_matrices(g):
    out = {}
    wp = g["w_in_p"]
    out["w_in"] = jnp.concatenate([wp[:, :D_IN - QK_ROPE], wp[:, D_IN_PAD - HEAD_PAD + QK_NOPE:
                                                              D_IN_PAD - HEAD_PAD + QK_NOPE + QK_ROPE]], axis=1)
    out["w_uq"] = _unpad_heads(g["w_uq_p"], QK_NOPE + QK_ROPE).reshape(Q_LORA, -1)
    out["w_ukv"] = jnp.concatenate([_unpad_heads(g["w_k_p"], QK_NOPE), _unpad_heads(g["w_v_p"], V_HEAD)],
                                   axis=2).reshape(KV_LORA, -1)
    return out


def _true_vectors(g):
    out = {}
    out["conv_w"] = g["conv_w"]
    fw = g["ffn_conv_w"]
    out["ffn_conv_w"] = jnp.concatenate([fw[0, :3], fw[1, :3]], axis=1)
    out["ffn_conv_b"] = g["ffn_conv_b"].reshape(1, 2 * D_FF)
    for n in ("norm_mix_g", "q_norm_g", "kv_norm_g", "norm_ffn_g", "ple_norm_g", "final_norm_g"):
        out[n] = g[n]
    return out


def _chip_partials(names, g, core, *, tag):
    g4 = [_shard_major(n, g[n]) for n in names]
    sib = _send_other_halves(g4, tag=tag)
    return [_add_own_half(a, b, core, name="add_cores_" + n) for n, a, b in zip(names, g4, sib)]


_SMALL_SIZES = {"norm_mix_g": D_MODEL, "conv_w": 3 * CONV_WIDTH, "q_norm_g": Q_LORA, "kv_norm_g": KV_LORA,
                "norm_ffn_g": D_MODEL, "ffn_conv_w": 6 * D_FF, "ffn_conv_b": 2 * D_FF, "ple_norm_g": D_MODEL,
                "final_norm_g": D_MODEL}


def _pack(parts, rows):
    flat = jnp.concatenate([a.reshape(-1) for a in parts])
    return jnp.pad(flat, (0, rows * 128 - flat.shape[0])).reshape(rows, 128)


def _unpack(buf, sizes):
    flat = buf.reshape(-1)
    out, at = [], 0
    for n in sizes:
        out.append(flat[at:at + n])
        at += n
    return out


def _reduce_small(g, loss):
    sizes = [1] + [_SMALL_SIZES[n] for n in _SMALL]
    rows = -(-sum(sizes) // 1024) * 8
    slots = _gather_all(_pack([loss] + [g[n] for n in _SMALL], rows))
    parts = _unpack(_sum_slots(slots, name="sum_small"), sizes)
    return parts[0][0], dict(zip(_SMALL, parts[1:]))


def kernel(x, p, norm_mix_g, w_in, conv_w, q_norm_g, w_uq, kv_norm_g, w_ukv, w_o, norm_ffn_g, w_up, ffn_conv_w, ffn_conv_b, w_down, ple_norm_g, w_ple_gate, w_ple_proj, final_norm_g, loss_target, m_norm_mix_g, m_w_in, m_conv_w, m_q_norm_g, m_w_uq, m_kv_norm_g, m_w_ukv, m_w_o, m_norm_ffn_g, m_w_up, m_ffn_conv_w, m_ffn_conv_b, m_w_down, m_ple_norm_g, m_w_ple_gate, m_w_ple_proj, m_final_norm_g, v_norm_mix_g, v_w_in, v_conv_w, v_q_norm_g, v_w_uq, v_kv_norm_g, v_w_ukv, v_w_o, v_norm_ffn_g, v_w_up, v_ffn_conv_w, v_ffn_conv_b, v_w_down, v_ple_norm_g, v_w_ple_gate, v_w_ple_proj, v_final_norm_g):
    names = ["norm_mix_g", "w_in", "conv_w", "q_norm_g", "w_uq", "kv_norm_g", "w_ukv", "w_o", "norm_ffn_g", "w_up",
             "ffn_conv_w", "ffn_conv_b", "w_down", "ple_norm_g", "w_ple_gate", "w_ple_proj", "final_norm_g"]
    w = dict(zip(names, (norm_mix_g, w_in, conv_w, q_norm_g, w_uq, kv_norm_g, w_ukv, w_o, norm_ffn_g, w_up,
                         ffn_conv_w, ffn_conv_b, w_down, ple_norm_g, w_ple_gate, w_ple_proj, final_norm_g)))
    m = dict(zip(names, (m_norm_mix_g, m_w_in, m_conv_w, m_q_norm_g, m_w_uq, m_kv_norm_g, m_w_ukv, m_w_o,
                         m_norm_ffn_g, m_w_up, m_ffn_conv_w, m_ffn_conv_b, m_w_down, m_ple_norm_g, m_w_ple_gate,
                         m_w_ple_proj, m_final_norm_g)))
    v = dict(zip(names, (v_norm_mix_g, v_w_in, v_conv_w, v_q_norm_g, v_w_uq, v_kv_norm_g, v_w_ukv, v_w_o,
                         v_norm_ffn_g, v_w_up, v_ffn_conv_w, v_ffn_conv_b, v_w_down, v_ple_norm_g, v_w_ple_gate,
                         v_w_ple_proj, v_final_norm_g)))

    core = lax.axis_index("c").astype(jnp.int32).reshape(1)
    chip = (2 * lax.axis_index("x") + lax.axis_index("y")).astype(jnp.int32).reshape(1)

    first = _chips_start(_early_shards(w), scatter=False, name="gather_early_start")
    rope = _rope_tables(x.shape[1], first[4][0, 0])
    late_shards = [w[n][0].astype(BF16) for n in _LATE_W]
    ready, *late_shards = lax.optimization_barrier((rope[0], *late_shards))
    own, landed = _chips_wait(first, ready, scatter=False, name="gather_early_wait")
    wts = _early_weights(_fill_own_slot(landed, own, chip), w)
    wts["rope"] = (ready,) + tuple(rope[1:])
    late_shards[0], _ = lax.optimization_barrier((late_shards[0], own[0]))
    gather = _chips_start(late_shards, scatter=False, name="gather_late_start")
    wts["norm_mix_g"] = wts["norm_mix_g"] + gather[4][0, 0]

    def late_weights(after):
        shards, landed = _chips_wait(gather, after, scatter=False, name="gather_late_wait")
        return _layout_late({n: _full_from_slots(n, g4)
                             for n, g4 in zip(_LATE_W, _fill_own_slot(landed, shards, chip))})

    def reduce_early(g):
        parts = _chip_partials(_LATE_W, g, core, tag="early")
        scatter = _chips_start([t16 for _, t16 in parts], scatter=True, name="scatter_early_start")

        def finish(after):
            _, landed = _chips_wait(scatter, after, scatter=True, name="scatter_early_wait")
            return [_sum_chips(a, t32, chip, name="sum_chips_" + n) for n, a, (t32, _) in zip(_LATE_W, landed, parts)]

        return scatter[4], finish

    def reduce_last(g):
        parts = _chip_partials(_EARLY_W, _true_matrices(g), core, tag="late")
        scatter = _chips_start([t16 for _, t16 in parts], scatter=True, name="scatter_late_start")

        def finish(after):
            _, landed = _chips_wait(scatter, after, scatter=True, name="scatter_late_wait")
            return [_sum_chips(a, t32, chip, name="sum_chips_" + n) for n, a, (t32, _) in zip(_EARLY_W, landed, parts)]

        return scatter[4], finish

    loss, grad_x, small_grads, halves_early, halves_last = _local_step(
        x[0], p[0, 0], loss_target[0], wts, late_weights, reduce_early, reduce_last)
    g_full = _true_vectors(small_grads)
    whole = _join_halves(halves_last + halves_early)
    big = {n: a.reshape(-1, a.shape[2]) for n, a in zip(_BIG, whole)}

    g_out, d_out, m_out, v_out = {}, {}, {}, {}
    for n in _BIG:
        shape = w[n].shape
        g = big[n]
        d, mn, vn = _adamw(w[n][0], g, m[n][0], v[n][0], name="adamw_" + n)
        g_out[n], d_out[n], m_out[n], v_out[n] = (a.reshape(shape) for a in (g, d, mn, vn))

    loss, small = _reduce_small(g_full, loss)
    chip = 2 * lax.axis_index("x") + lax.axis_index("y")
    g_small = {}
    for n in _SMALL:
        shape = w[n].shape
        g = small[n]
        if n in ("conv_w", "ffn_conv_w"):
            width = shape[-1]
            g = lax.dynamic_slice(g.reshape(3, N_CHIPS * width), (0, chip * width), (3, width))
        g_small[n] = g.reshape(shape)
    flat = [[src[n].reshape(-1, src[n].shape[-1]) for n in _SMALL] for src in (w, g_small, m, v)]
    for n, (d, mn, vn) in zip(_SMALL, _adamw_many(*flat, name="adamw_small")):
        shape = w[n].shape
        g_out[n], d_out[n], m_out[n], v_out[n] = g_small[n], d.reshape(shape), mn.reshape(shape), vn.reshape(shape)

    return (loss, grad_x[None], *[g_out[n] for n in names], *[d_out[n] for n in names],
            *[m_out[n] for n in names], *[v_out[n] for n in names])
```

```python
import functools

import jax
import jax.numpy as jnp
from jax import lax
from jax.experimental import pallas as pl
from jax.experimental.pallas import tpu as pltpu

F32 = jnp.float32
BF16 = jnp.bfloat16

D_MODEL = 1024
CONV_WIDTH = 512
Q_LORA = 256
KV_LORA = 128
QK_NOPE = 64
QK_ROPE = 32
V_HEAD = 64
N_HEADS = 8
HEAD_PAD = 128
D_ATT = N_HEADS * HEAD_PAD
D_IN = 3 * CONV_WIDTH + Q_LORA + KV_LORA + QK_ROPE
D_IN_PAD = 3 * CONV_WIDTH + Q_LORA + KV_LORA + HEAD_PAD
D_FF = 2816
ROPE_THETA = 10000.0
EPS = 1e-6
SM_SCALE = (QK_NOPE + QK_ROPE) ** -0.5
ONES_LANE = V_HEAD

ADAM_LR = 0.001
ADAM_B1 = 0.9
ADAM_B2 = 0.999
ADAM_EPS = 1e-08
ADAM_WD = 0.01
ADAM_STEP = 10

N_CHIPS = 4
N_DEV = 8
MESH = pl.DeviceIdType.MESH
ANY = pl.BlockSpec(memory_space=pl.ANY)


def _params(sem):
    return pltpu.CompilerParams(dimension_semantics=sem)


MM_PIECE = 256


def _pieces(total, width=MM_PIECE):
    return [(off, min(width, total - off)) for off in range(0, total, width)]


def _mm(a, b, *, name, ta=False, tb=False, add=None, out_dtype=F32, tm=512, tn=512, tk=512, b_split=False,
        o_shards=False):
    k, m = a.shape if ta else a.shape[::-1]
    if b_split:
        _, kb, nh = b.shape
        n = 2 * nh
    elif tb:
        n, kb = b.shape
    else:
        kb, n = b.shape
    assert kb == k, (name, a.shape, b.shape)
    tm, tn, tk = min(tm, m), min(tn, n), min(tk, k)
    assert m % tm == 0 and n % tn == 0 and k % tk == 0, (name, m, n, k, tm, tn, tk)
    gm, gn, gk = m // tm, n // tn, k // tk

    a_spec = pl.BlockSpec((tk, tm), lambda i, j, kk: (kk, i)) if ta else pl.BlockSpec((tm, tk), lambda i, j, kk: (i, kk))
    if b_split:
        assert gn % 2 == 0
        b_spec = pl.BlockSpec((None, tk, tn), lambda i, j, kk: (j // (gn // 2), kk, j % (gn // 2)))
    elif tb:
        b_spec = pl.BlockSpec((tn, tk), lambda i, j, kk: (j, kk))
    else:
        b_spec = pl.BlockSpec((tk, tn), lambda i, j, kk: (kk, j))
    if o_shards:
        o_spec = pl.BlockSpec((None, tm, tn), lambda i, j, kk: (j, i, 0))
        o_shape = jax.ShapeDtypeStruct((gn, m, tn), out_dtype)
    else:
        o_spec = pl.BlockSpec((tm, tn), lambda i, j, kk: (i, j))
        o_shape = jax.ShapeDtypeStruct((m, n), out_dtype)
    dims = (((0 if ta else 1,), (1 if tb else 0,)), ((), ()))

    def body(*refs):
        a_ref, b_ref = refs[:2]
        add_ref = None if add is None else refs[2]
        o_ref = refs[2 if add is None else 3]
        acc_ref = None if gk == 1 else refs[-1]
        kk = pl.program_id(2)
        rhs = b_ref[...].astype(BF16)

        def finish(r, rows):
            if add_ref is not None:
                r = r + add_ref[rows, :]
            o_ref[rows, :] = r.astype(o_ref.dtype)

        for off, size in _pieces(tm):
            rows = slice(off, off + size)
            lhs = (a_ref[:, rows] if ta else a_ref[rows, :]).astype(BF16)
            part = lax.dot_general(lhs, rhs, dims, preferred_element_type=F32)
            if gk == 1:
                finish(part, rows)
            else:
                acc_ref[rows, :] = jnp.where(kk == 0, part, acc_ref[rows, :] + part)

        if gk > 1:
            @pl.when(kk == gk - 1)
            def _():
                finish(acc_ref[...], slice(None))

    in_specs = [a_spec, b_spec]
    args = [a, b]
    if add is not None:
        in_specs.append(pl.BlockSpec((tm, tn), lambda i, j, kk: (i, j)))
        args.append(add)
    return pl.pallas_call(
        body, name=name, grid=(gm, gn, gk), in_specs=in_specs, out_specs=o_spec, out_shape=o_shape,
        scratch_shapes=[] if gk == 1 else [pltpu.VMEM((tm, tn), F32)],
        compiler_params=_params(("parallel", "parallel", "arbitrary")),
    )(*args)


def _rms_scale(v):
    return lax.rsqrt(jnp.mean(v * v, axis=-1, keepdims=True) + EPS)


def _rms_bwd_rows(v, g, dy):
    r = _rms_scale(v)
    vh = v * r
    dyg = dy * g
    dv = r * (dyg - vh * jnp.mean(dyg * vh, axis=-1, keepdims=True))
    return dv, dy * vh


def _shift_down(v, first_row):
    row = lax.broadcasted_iota(jnp.int32, v.shape, 0)
    return jnp.where(row == 0, first_row, pltpu.roll(v, 1, 0))


def _shift_up(v, last_row):
    n = v.shape[0]
    row = lax.broadcasted_iota(jnp.int32, v.shape, 0)
    return jnp.where(row == n - 1, last_row, pltpu.roll(v, n - 1, 0))


def _rope(t, cos, sin_a, sin_b):
    return t * cos + pltpu.roll(t, HEAD_PAD - 16, 1) * sin_a + pltpu.roll(t, 16, 1) * sin_b


def _rope_bwd(d, cos, sin_a, sin_b):
    return d * cos + pltpu.roll(d * sin_a, 16, 1) + pltpu.roll(d * sin_b, HEAD_PAD - 16, 1)


def _sigmoid(v):
    return 1.0 / (1.0 + jnp.exp(-v))


def _halo_specs(ts, s, width, col):
    nb = ts // 8
    last = s // 8 - 1
    prev = pl.BlockSpec((8, width), lambda i: (jnp.maximum(i * nb - 1, 0), col))
    nxt = pl.BlockSpec((8, width), lambda i: (jnp.minimum((i + 1) * nb, last), col))
    return prev, nxt


def _mm_fused(a, b, *, name, epilogue, row_outs, rows=(), vecs=(), n_vec_out=0, tb=False, a_split=False,
              prologue=None, second=None, transposed_out=None, tm=512):
    if a_split:
        _, m, kh = a.shape
        k = 2 * kh
    else:
        m, k = a.shape
    n = b.shape[0] if tb else b.shape[1]
    assert (b.shape[1] if tb else b.shape[0]) == k, (name, a.shape, b.shape)
    assert m % tm == 0, (name, m, tm)
    n_a = 2 if a_split else 1
    nr, nv = len(rows), len(vecs)
    n_pro = 0 if prologue is None else 1
    n_sec = 0 if second is None else 2
    n_t = 0 if transposed_out is None else 1
    dims = (((1,), (1 if tb else 0,)), ((), ()))

    def body(*refs):
        a_refs, b_ref = refs[:n_a], refs[n_a]
        refs = refs[n_a + 1:]
        sec_refs = refs[:n_sec]
        row_refs, vec_refs = refs[n_sec:n_sec + nr], refs[n_sec + nr:n_sec + nr + nv]
        outs = refs[n_sec + nr + nv:]
        row_out_refs = outs[n_pro:n_pro + len(row_outs)]
        t_out_refs = outs[n_pro + len(row_outs):n_pro + len(row_outs) + n_t]
        vec_out_refs = outs[n_pro + len(row_outs) + n_t:n_pro + len(row_outs) + n_t + n_vec_out]
        vec_vals = [v[...] for v in vec_refs]
        if a_split:
            kh = k // 2
            rhs = [(b_ref[:, :kh], b_ref[:, kh:]) if tb else (b_ref[:kh, :], b_ref[kh:, :])][0]
            rhs = [h.astype(BF16) for h in rhs]
        else:
            rhs = [b_ref[...].astype(BF16)]
        vec_sums = [None] * n_vec_out

        for off, size in _pieces(tm):
            rs = slice(off, off + size)
            if prologue is None:
                lhs = [a_ref[rs, :].astype(BF16) for a_ref in a_refs]
            else:
                lhs = [prologue(a_refs[0][rs, :], vec_vals)]
                outs[0][rs, :] = lhs[0]
            r = lax.dot_general(lhs[0], rhs[0], dims, preferred_element_type=F32)
            for l2, r2 in zip(lhs[1:], rhs[1:]):
                r = r + lax.dot_general(l2, r2, dims, preferred_element_type=F32)
            if second is not None:
                r = r + jnp.dot(sec_refs[0][rs, :].astype(BF16), sec_refs[1][...].astype(BF16),
                                preferred_element_type=F32)
            row_vals, vec_parts = epilogue(r, [x[rs, :] for x in row_refs], vec_vals)
            for ref, val in zip(row_out_refs, row_vals):
                ref[rs, :] = val.astype(ref.dtype)
            for ref in t_out_refs:
                ref[:, rs] = row_vals[0].T.astype(ref.dtype)
            vec_sums = [p if t is None else t + p for t, p in zip(vec_sums, vec_parts)]

        if n_vec_out:
            @pl.when(pl.program_id(0) == 0)
            def _():
                for ref in vec_out_refs:
                    ref[...] = jnp.zeros_like(ref)

            for ref, val in zip(vec_out_refs, vec_sums):
                ref[...] += val

    if a_split:
        a_specs = [pl.BlockSpec((None, tm, k // 2), lambda i: (0, i, 0)),
                   pl.BlockSpec((None, tm, k // 2), lambda i: (1, i, 0))]
    else:
        a_specs = [pl.BlockSpec((tm, k), lambda i: (i, 0))]
    b_spec = pl.BlockSpec(b.shape, lambda i: (0, 0))
    row_spec = pl.BlockSpec((tm, n), lambda i: (i, 0))
    out_specs, out_shape = [], []
    if prologue is not None:
        out_specs.append(pl.BlockSpec((tm, k), lambda i: (i, 0)))
        out_shape.append(jax.ShapeDtypeStruct((m, k), BF16))
    out_specs += [row_spec] * len(row_outs)
    out_shape += [jax.ShapeDtypeStruct((m, n), dt) for dt in row_outs]
    if transposed_out is not None:
        out_specs.append(pl.BlockSpec((n, tm), lambda i: (0, i)))
        out_shape.append(jax.ShapeDtypeStruct((n, m), transposed_out))
    out_specs += [pl.BlockSpec((1, n), lambda i: (0, 0))] * n_vec_out
    out_shape += [jax.ShapeDtypeStruct((1, n), F32)] * n_vec_out
    sec_specs, sec_args = [], []
    if second is not None:
        k2 = second[0].shape[1]
        sec_specs = [pl.BlockSpec((tm, k2), lambda i: (i, 0)), pl.BlockSpec((k2, n), lambda i: (0, 0))]
        sec_args = list(second)
    res = pl.pallas_call(
        body, name=name, grid=(m // tm,),
        in_specs=a_specs + [b_spec] + sec_specs + [row_spec] * nr
        + [pl.BlockSpec((1, v.shape[1]), lambda i: (0, 0)) for v in vecs],
        out_specs=out_specs, out_shape=out_shape,
        compiler_params=_params(("arbitrary" if n_vec_out else "parallel",)),
    )(*([a] * n_a), b, *sec_args, *rows, *vecs)
    split = n_pro + len(row_outs) + n_t
    return list(res[:split]), list(res[split:])


def _pro_rms(a, vecs):
    return (a * _rms_scale(a) * vecs[0]).astype(BF16)


def _epi_plain(r, rows, vecs):
    return [r], []


def _epi_add_rms(r, rows, vecs):
    xn = r + rows[0]
    return [xn, xn * _rms_scale(xn) * vecs[0]], []


def _epi_rms_bwd(r, rows, vecs):
    dv, dg_rows = _rms_bwd_rows(rows[0], vecs[0], r)
    return [dv + rows[1]], [jnp.sum(dg_rows, axis=0, keepdims=True)]


def _mix_pre(z, conv_w8, gq, gkv, cos, sin_a, sin_b, *, ts=1024):
    s = z.shape[0]
    n = s // ts
    cw = CONV_WIDTH

    def body(z_ref, xcp, xcn, cgp, cgn, w_ref, gq_ref, gkv_ref, cos_ref, sa_ref, sb_ref,
             yc_ref, qn_ref, kvn_ref, kr_ref):
        i = pl.program_id(0)
        xc = z_ref[:, 0:cw]
        bg = z_ref[:, cw:2 * cw]
        cg = z_ref[:, 2 * cw:3 * cw]
        m = cg * xc
        m_prev = jnp.where(i > 0, xcp[7:8, :] * cgp[7:8, :], 0.0)
        m_next = jnp.where(i < n - 1, xcn[0:1, :] * cgn[0:1, :], 0.0)
        cm = _shift_down(m, m_prev) * w_ref[0:1, :] + m * w_ref[1:2, :] + _shift_up(m, m_next) * w_ref[2:3, :]
        yc_ref[...] = (bg * cm).astype(BF16)
        ql = z_ref[:, 3 * cw:3 * cw + Q_LORA]
        qn_ref[...] = (ql * _rms_scale(ql) * gq_ref[...]).astype(BF16)
        kvl = z_ref[:, 3 * cw + Q_LORA:3 * cw + Q_LORA + KV_LORA]
        kvn_ref[...] = (kvl * _rms_scale(kvl) * gkv_ref[...]).astype(BF16)
        kr_ref[...] = _rope(z_ref[:, D_IN_PAD - HEAD_PAD:D_IN_PAD], cos_ref[...], sa_ref[...], sb_ref[...])

    xcp, xcn = _halo_specs(ts, s, cw, 0)
    cgp, cgn = _halo_specs(ts, s, cw, 2)
    tab = pl.BlockSpec((ts, HEAD_PAD), lambda i: (i, 0))
    return pl.pallas_call(
        body, name="mix_pre", grid=(n,),
        in_specs=[pl.BlockSpec((ts, D_IN_PAD), lambda i: (i, 0)), xcp, xcn, cgp, cgn,
                  pl.BlockSpec((8, cw), lambda i: (0, 0)), pl.BlockSpec((1, Q_LORA), lambda i: (0, 0)),
                  pl.BlockSpec((1, KV_LORA), lambda i: (0, 0)), tab, tab, tab],
        out_specs=[pl.BlockSpec((ts, cw), lambda i: (i, 0)), pl.BlockSpec((ts, Q_LORA), lambda i: (i, 0)),
                   pl.BlockSpec((ts, KV_LORA), lambda i: (i, 0)), tab],
        out_shape=[jax.ShapeDtypeStruct((s, cw), BF16), jax.ShapeDtypeStruct((s, Q_LORA), BF16),
                   jax.ShapeDtypeStruct((s, KV_LORA), BF16), jax.ShapeDtypeStruct((s, HEAD_PAD), F32)],
        compiler_params=_params(("parallel",)),
    )(z, z, z, z, z, conv_w8, gq, gkv, cos, sin_a, sin_b)


def _mix_bwd(z, dyc, dqn, dkvn, dkr, conv_w8, gq, gkv, cos, sin_a, sin_b, *, ts=1024):
    s = z.shape[0]
    n = s // ts
    cw = CONV_WIDTH

    def body(z_ref, xcp, xcn, bgp, bgn, cgp, cgn, dyc_ref, dycp, dycn, dqn_ref, dkvn_ref, dkr_ref,
             w_ref, gq_ref, gkv_ref, cos_ref, sa_ref, sb_ref,
             dz_ref, dw0_ref, dw1_ref, dw2_ref, dgq_ref, dgkv_ref):
        i = pl.program_id(0)

        @pl.when(i == 0)
        def _():
            for r in (dw0_ref, dw1_ref, dw2_ref, dgq_ref, dgkv_ref):
                r[...] = jnp.zeros_like(r)

        xc = z_ref[:, 0:cw]
        bg = z_ref[:, cw:2 * cw]
        cg = z_ref[:, 2 * cw:3 * cw]
        w0, w1, w2 = w_ref[0:1, :], w_ref[1:2, :], w_ref[2:3, :]
        m = cg * xc
        m_dn = _shift_down(m, jnp.where(i > 0, xcp[7:8, :] * cgp[7:8, :], 0.0))
        m_up = _shift_up(m, jnp.where(i < n - 1, xcn[0:1, :] * cgn[0:1, :], 0.0))
        cm = m_dn * w0 + m * w1 + m_up * w2
        dyc_v = dyc_ref[...]
        dcm = dyc_v * bg
        dcm_dn = _shift_down(dcm, jnp.where(i > 0, dycp[7:8, :] * bgp[7:8, :], 0.0))
        dcm_up = _shift_up(dcm, jnp.where(i < n - 1, dycn[0:1, :] * bgn[0:1, :], 0.0))
        dm = dcm_up * w0 + dcm * w1 + dcm_dn * w2
        dz_ref[:, 0:cw] = (dm * cg).astype(BF16)
        dz_ref[:, cw:2 * cw] = (dyc_v * cm).astype(BF16)
        dz_ref[:, 2 * cw:3 * cw] = (dm * xc).astype(BF16)
        dw0_ref[...] += jnp.sum(dcm * m_dn, axis=0, keepdims=True)
        dw1_ref[...] += jnp.sum(dcm * m, axis=0, keepdims=True)
        dw2_ref[...] += jnp.sum(dcm * m_up, axis=0, keepdims=True)

        dql, dgq_rows = _rms_bwd_rows(z_ref[:, 3 * cw:3 * cw + Q_LORA], gq_ref[...], dqn_ref[...])
        dz_ref[:, 3 * cw:3 * cw + Q_LORA] = dql.astype(BF16)
        dgq_ref[...] += jnp.sum(dgq_rows, axis=0, keepdims=True)
        dkvl, dgkv_rows = _rms_bwd_rows(z_ref[:, 3 * cw + Q_LORA:3 * cw + Q_LORA + KV_LORA], gkv_ref[...],
                                        dkvn_ref[...])
        dz_ref[:, 3 * cw + Q_LORA:3 * cw + Q_LORA + KV_LORA] = dkvl.astype(BF16)
        dgkv_ref[...] += jnp.sum(dgkv_rows, axis=0, keepdims=True)

        lane = lax.broadcasted_iota(jnp.int32, (ts, HEAD_PAD), 1)
        rope_lane = (lane >= QK_NOPE) & (lane < QK_NOPE + QK_ROPE)
        dk = _rope_bwd(dkr_ref[...], cos_ref[...], sa_ref[...], sb_ref[...])
        dz_ref[:, D_IN_PAD - HEAD_PAD:D_IN_PAD] = jnp.where(rope_lane, dk, 0.0).astype(BF16)

    xcp, xcn = _halo_specs(ts, s, cw, 0)
    bgp, bgn = _halo_specs(ts, s, cw, 1)
    cgp, cgn = _halo_specs(ts, s, cw, 2)
    dycp, dycn = _halo_specs(ts, s, cw, 0)
    tab = pl.BlockSpec((ts, HEAD_PAD), lambda i: (i, 0))

    def vec(width):
        return pl.BlockSpec((1, width), lambda i: (0, 0))

    outs = pl.pallas_call(
        body, name="mix_bwd", grid=(n,),
        in_specs=[pl.BlockSpec((ts, D_IN_PAD), lambda i: (i, 0)), xcp, xcn, bgp, bgn, cgp, cgn,
                  pl.BlockSpec((ts, cw), lambda i: (i, 0)), dycp, dycn,
                  pl.BlockSpec((ts, Q_LORA), lambda i: (i, 0)), pl.BlockSpec((ts, KV_LORA), lambda i: (i, 0)), tab,
                  pl.BlockSpec((8, cw), lambda i: (0, 0)), vec(Q_LORA), vec(KV_LORA), tab, tab, tab],
        out_specs=[pl.BlockSpec((ts, D_IN_PAD), lambda i: (i, 0)), vec(cw), vec(cw), vec(cw), vec(Q_LORA),
                   vec(KV_LORA)],
        out_shape=[jax.ShapeDtypeStruct((s, D_IN_PAD), BF16)] + [jax.ShapeDtypeStruct((1, cw), F32)] * 3
        + [jax.ShapeDtypeStruct((1, Q_LORA), F32), jax.ShapeDtypeStruct((1, KV_LORA), F32)],
        compiler_params=_params(("arbitrary",)),
    )(z, z, z, z, z, z, z, dyc, dyc, dyc, dqn, dkvn, dkr, conv_w8, gq, gkv, cos, sin_a, sin_b)
    dz, dw0, dw1, dw2, dgq, dgkv = outs
    return dz, jnp.concatenate([dw0, dw1, dw2], axis=0), dgq, dgkv


def _qkv_proj(qn, kvn, kr, w_uq_p, w_kv_p, cos, sin_a, sin_b, *, ts=1024):
    s = qn.shape[0]

    def body(qn_ref, kvn_ref, kr_ref, wq_ref, wkv_ref, cos_ref, sa_ref, sb_ref, q_ref, k_ref, v_ref, qt_ref):
        cos_v, sa, sb = cos_ref[...], sa_ref[...], sb_ref[...]
        q = jnp.dot(qn_ref[...], wq_ref[...], preferred_element_type=F32)
        kv = jnp.dot(kvn_ref[...], wkv_ref[...], preferred_element_type=F32)
        kr_v = kr_ref[...]
        lane = lax.broadcasted_iota(jnp.int32, (1, HEAD_PAD), 1)
        ones_lane = (lane == ONES_LANE).astype(F32)
        for h in range(N_HEADS):
            blk = slice(h * HEAD_PAD, (h + 1) * HEAD_PAD)
            q_h = _rope(q[:, blk], cos_v, sa, sb) * SM_SCALE
            q_ref[:, blk] = q_h.astype(BF16)
            qt_ref[blk, :] = q_h.T.astype(BF16)
            k_ref[:, blk] = (kv[:, blk] + kr_v).astype(BF16)
            v_ref[:, blk] = (kv[:, D_ATT + h * HEAD_PAD:D_ATT + (h + 1) * HEAD_PAD] + ones_lane).astype(BF16)

    tab = pl.BlockSpec((ts, HEAD_PAD), lambda i: (i, 0))
    wide = pl.BlockSpec((ts, D_ATT), lambda i: (i, 0))
    return pl.pallas_call(
        body, name="qkv_proj", grid=(s // ts,),
        in_specs=[pl.BlockSpec((ts, Q_LORA), lambda i: (i, 0)), pl.BlockSpec((ts, KV_LORA), lambda i: (i, 0)), tab,
                  pl.BlockSpec((Q_LORA, D_ATT), lambda i: (0, 0)), pl.BlockSpec((KV_LORA, 2 * D_ATT), lambda i: (0, 0)),
                  tab, tab, tab],
        out_specs=[wide, wide, wide, pl.BlockSpec((D_ATT, ts), lambda i: (0, i))],
        out_shape=[jax.ShapeDtypeStruct((s, D_ATT), BF16)] * 3 + [jax.ShapeDtypeStruct((D_ATT, s), BF16)],
        compiler_params=_params(("parallel",)),
    )(qn, kvn, kr, w_uq_p, w_kv_p, cos, sin_a, sin_b)


def _qk_bwd(dq, dk, cos, sin_a, sin_b, *, ts=1024):
    s = dq.shape[0]

    def body(dq_ref, dk_ref, cos_ref, sa_ref, sb_ref, dqp_ref, dkr_ref):
        cos_v, sa, sb = cos_ref[...], sa_ref[...], sb_ref[...]
        tot = jnp.zeros((ts, HEAD_PAD), F32)
        for h in range(N_HEADS):
            blk = slice(h * HEAD_PAD, (h + 1) * HEAD_PAD)
            dqp_ref[:, blk] = _rope_bwd(dq_ref[:, blk], cos_v, sa, sb).astype(BF16)
            tot = tot + dk_ref[:, blk]
        dkr_ref[...] = tot

    tab = pl.BlockSpec((ts, HEAD_PAD), lambda i: (i, 0))
    wide = pl.BlockSpec((ts, D_ATT), lambda i: (i, 0))
    return pl.pallas_call(
        body, name="qk_bwd", grid=(s // ts,),
        in_specs=[wide, wide, tab, tab, tab], out_specs=[wide, tab],
        out_shape=[jax.ShapeDtypeStruct((s, D_ATT), BF16), jax.ShapeDtypeStruct((s, HEAD_PAD), F32)],
        compiler_params=_params(("parallel",)),
    )(dq, dk, cos, sin_a, sin_b)


_NT = (((1,), (1,)), ((), ()))


def _flash_fwd(q, k, v, *, tq=1024, tk=1024, per_trip=8, q_per_step=2):
    s = q.shape[0]
    tq, tk = min(tq, s), min(tk, s)
    nk = s // tk
    per_trip = min(per_trip, nk)
    assert nk % per_trip == 0
    q_per_step = min(q_per_step, s // tq)

    def body(q_ref, k_ref, v_ref, o_ref, lse_ref):
        for blk in range(q_per_step):
            rows = pl.ds(blk * tq, tq)
            one_block(q_ref.at[rows, :], k_ref, v_ref, o_ref.at[rows, :], lse_ref.at[rows, :])

    def one_block(q_ref, k_ref, v_ref, o_ref, lse_ref):
        qv = q_ref[...]

        def step(j, carry):
            m, acc = carry
            rows = pl.ds(pl.multiple_of(j * tk, tk), tk)
            sc = lax.dot_general(qv, k_ref[rows, :], _NT, preferred_element_type=F32)
            m_new = jnp.maximum(m, jnp.max(sc, axis=1, keepdims=True))
            p = jnp.exp(sc - m_new).astype(BF16)
            acc = jnp.exp(m - m_new) * acc + jnp.dot(p, v_ref[rows, :], preferred_element_type=F32)
            return m_new, acc

        def trip(t, carry):
            for c in range(per_trip):
                carry = step(per_trip * t + c, carry)
            return carry

        init = (jnp.full((tq, 1), -jnp.inf, F32), jnp.zeros((tq, HEAD_PAD), F32))
        m, acc = lax.fori_loop(0, nk // per_trip, trip, init)
        l = acc[:, ONES_LANE:ONES_LANE + 1]
        o_ref[...] = (acc / l).astype(BF16)
        lse_ref[...] = m + jnp.log(l)

    head = pl.BlockSpec((s, HEAD_PAD), lambda h, i: (0, h))
    tq_step = q_per_step * tq
    return pl.pallas_call(
        body, name="flash_fwd", grid=(N_HEADS, s // tq_step),
        in_specs=[pl.BlockSpec((tq_step, HEAD_PAD), lambda h, i: (i, h)), head, head],
        out_specs=[pl.BlockSpec((tq_step, HEAD_PAD), lambda h, i: (i, h)),
                   pl.BlockSpec((None, tq_step, 1), lambda h, i: (h, i, 0))],
        out_shape=[jax.ShapeDtypeStruct((s, D_ATT), BF16), jax.ShapeDtypeStruct((N_HEADS, s, 1), F32)],
        compiler_params=_params(("parallel", "parallel")),
    )(q, k, v)


def _attn_delta(do, o, *, ts=1024):
    s = do.shape[0]

    def body(do_ref, o_ref, dl_ref):
        for h in range(N_HEADS):
            blk = slice(h * HEAD_PAD, (h + 1) * HEAD_PAD)
            dl_ref[h] = jnp.sum(do_ref[:, blk].astype(F32) * o_ref[:, blk].astype(F32), axis=1, keepdims=True)

    wide = pl.BlockSpec((ts, D_ATT), lambda i: (i, 0))
    return pl.pallas_call(
        body, name="attn_delta", grid=(s // ts,), in_specs=[wide, wide],
        out_specs=pl.BlockSpec((N_HEADS, ts, 1), lambda i: (0, i, 0)),
        out_shape=jax.ShapeDtypeStruct((N_HEADS, s, 1), F32),
        compiler_params=_params(("parallel",)),
    )(do, o)


def _flash_bwd(q, qt, k, v, do, dot, lse, delta, after, *, tq=1024, tk=512, per_trip=8, kv_per_step=2):
    s = q.shape[0]
    tq, tk = min(tq, s), min(tk, s)
    nq = s // tq
    per_trip = min(per_trip, nq)
    assert nq % per_trip == 0
    kv_per_step = min(kv_per_step, s // tk)

    def body(q_ref, qt_ref, do_ref, dot_ref, lse_ref, dl_ref, k_ref, v_ref, after_ref, dq_ref, dk_ref, dv_ref):
        j = pl.program_id(1)

        @pl.when(j == 0)
        def _():
            dq_ref[...] = jnp.zeros_like(dq_ref)

        for blk in range(kv_per_step):
            one_block(q_ref, qt_ref, do_ref, dot_ref, lse_ref, dl_ref, dq_ref,
                      k_ref.at[pl.ds(blk * tk, tk), :], v_ref.at[pl.ds(blk * tk, tk), :],
                      dk_ref.at[pl.ds(blk * tk, tk), :], dv_ref.at[pl.ds(blk * tk, tk), :])

        @pl.when(j == pl.num_programs(1) - 1)
        def _():
            dq_ref[...] *= SM_SCALE

    def one_block(q_ref, qt_ref, do_ref, dot_ref, lse_ref, dl_ref, dq_ref, k_ref, v_ref, dk_ref, dv_ref):
        kv, vv = k_ref[...], v_ref[...]

        def chunk(i, dk_t, dv_t):
            at = pl.multiple_of(i * tq, tq)
            rows = pl.ds(at, tq)
            sc = lax.dot_general(q_ref[rows, :], kv, _NT, preferred_element_type=F32)
            p = jnp.exp(sc - lse_ref[rows, :])
            dp = lax.dot_general(do_ref[rows, :], vv, _NT, preferred_element_type=F32)
            ds = (p * (dp - dl_ref[rows, :])).astype(BF16)
            dv_t = dv_t + jnp.dot(dot_ref[:, rows], p.astype(BF16), preferred_element_type=F32)
            dk_t = dk_t + jnp.dot(qt_ref[:, rows], ds, preferred_element_type=F32)
            dq_ref[rows, :] += jnp.dot(ds, kv, preferred_element_type=F32)
            return dk_t, dv_t

        def step(i, carry):
            for c in range(per_trip):
                carry = chunk(per_trip * i + c, *carry)
            return carry

        zero = jnp.zeros((HEAD_PAD, tk), F32)
        dk_t, dv_t = lax.fori_loop(0, nq // per_trip, step, (zero, zero))
        dk_ref[...] = dk_t.T
        dv_ref[...] = dv_t.T

    head = pl.BlockSpec((s, HEAD_PAD), lambda h, j: (0, h))
    head_t = pl.BlockSpec((HEAD_PAD, s), lambda h, j: (h, 0))
    stat = pl.BlockSpec((None, s, 1), lambda h, j: (h, 0, 0))
    blk = pl.BlockSpec((kv_per_step * tk, HEAD_PAD), lambda h, j: (j, h))
    return pl.pallas_call(
        body, name="flash_bwd", grid=(N_HEADS, s // (kv_per_step * tk)),
        in_specs=[head, head_t, head, head_t, stat, stat, blk, blk, ANY],
        out_specs=[head, blk, blk],
        out_shape=[jax.ShapeDtypeStruct((s, D_ATT), F32)] * 3,
        compiler_params=_params(("parallel", "arbitrary")),
    )(q, qt, do, dot, lse, delta, k, v, after)


FFN_TC = 256
FFN_TG = 1408


FFN_HALO_BF16 = 16
FFN_HALO_F32 = 8


def _row_halo_specs(ts, s, halo, width):
    nb = ts // halo
    last = s // halo - 1
    prev = pl.BlockSpec((halo, width), lambda i, j: (jnp.maximum(i * nb - 1, 0), 0))
    nxt = pl.BlockSpec((halo, width), lambda i, j: (jnp.minimum((i + 1) * nb, last), 0))
    return prev, nxt


def _ext_rows(prev, main, nxt, first, last):
    return jnp.concatenate([jnp.where(first, jnp.zeros_like(prev), prev), main,
                            jnp.where(last, jnp.zeros_like(nxt), nxt)], axis=0)


def _ext_conv(a, w):
    a_dn = pltpu.roll(a, 1, 0)
    a_up = pltpu.roll(a, a.shape[0] - 1, 0)
    return a_dn * w[0:1, :] + a * w[1:2, :] + a_up * w[2:3, :], a_dn, a_up


def _ffn_pieces(tg):
    return [(off, min(FFN_TC, tg - off)) for off in range(0, tg, FFN_TC)]


def _ffn_fwd(hf, w_up, w, b, *, ts=1024, tg=FFN_TG):
    s = hf.shape[0]
    n, ng, halo = s // ts, D_FF // tg, FFN_HALO_BF16

    def body(h_ref, hp_ref, hn_ref, wg_ref, wu_ref, cw_ref, cb_ref, a_ref, act_ref):
        i = pl.program_id(0)
        ext = _ext_rows(hp_ref[...], h_ref[...], hn_ref[...], i == 0, i == n - 1)
        for off, width in _ffn_pieces(tg):
            cols = slice(off, off + width)
            gate_up = []
            for half, w_ref in enumerate((wg_ref, wu_ref)):
                a_ext = jnp.dot(ext, w_ref[:, cols], preferred_element_type=F32)
                a_ref[half, :, cols] = a_ext[halo:halo + ts]
                conv = _ext_conv(a_ext, cw_ref[half, :, cols])[0]
                gate_up.append(conv[halo:halo + ts] + cb_ref[half, :, cols])
            g, u = gate_up
            act_ref[:, cols] = (g * _sigmoid(g) * u).astype(BF16)

    prev, nxt = _row_halo_specs(ts, s, halo, D_MODEL)
    return pl.pallas_call(
        body, name="ffn_fwd", grid=(n, ng),
        in_specs=[pl.BlockSpec((ts, D_MODEL), lambda i, j: (i, 0)), prev, nxt,
                  pl.BlockSpec((D_MODEL, tg), lambda i, j: (0, j)), pl.BlockSpec((D_MODEL, tg), lambda i, j: (0, j + ng)),
                  pl.BlockSpec((2, 8, tg), lambda i, j: (0, 0, j)), pl.BlockSpec((2, 1, tg), lambda i, j: (0, 0, j))],
        out_specs=[pl.BlockSpec((2, ts, tg), lambda i, j: (0, i, j)), pl.BlockSpec((ts, tg), lambda i, j: (i, j))],
        out_shape=[jax.ShapeDtypeStruct((2, s, D_FF), F32), jax.ShapeDtypeStruct((s, D_FF), BF16)],
        compiler_params=_params(("parallel", "parallel")),
    )(hf, hf, hf, w_up, w_up, w, b)


def _ffn_bwd(dx2, w_down, a_pre, w, b, *, ts=1024, tg=FFN_TG):
    s = dx2.shape[0]
    n, ng, halo = s // ts, D_FF // tg, FFN_HALO_F32
    main = slice(halo, halo + ts)

    def body(dx_ref, dxp_ref, dxn_ref, wd_ref, a_ref, ap_ref, an_ref, cw_ref, cb_ref, o_ref, dw_ref, db_ref):
        i, j = pl.program_id(0), pl.program_id(1)
        first, last = i == 0, i == n - 1

        @pl.when(first & (j == 0))
        def _():
            dw_ref[...] = jnp.zeros_like(dw_ref)
            db_ref[...] = jnp.zeros_like(db_ref)

        dx_ext = _ext_rows(dxp_ref[...], dx_ref[...], dxn_ref[...], first, last).astype(BF16)
        for off, width in _ffn_pieces(tg):
            cols = slice(off, off + width)
            dact = lax.dot_general(dx_ext, wd_ref[cols, :], _NT, preferred_element_type=F32)
            halves = []
            for half in range(2):
                a_ext = _ext_rows(ap_ref[half, :, cols], a_ref[half, :, cols], an_ref[half, :, cols], first, last)
                conv, a_dn, a_up = _ext_conv(a_ext, cw_ref[half, :, cols])
                halves.append((conv + cb_ref[half, :, cols], a_dn, a_ext, a_up))
            g, u = halves[0][0], halves[1][0]
            sg = _sigmoid(g)
            grads = (dact * u * (sg * (1.0 + g * (1.0 - sg))), dact * (g * sg))
            for half in range(2):
                d = grads[half]
                _, a_dn, a_ext, a_up = halves[half]
                wv = cw_ref[half, :, cols]
                d_pre = (pltpu.roll(d, d.shape[0] - 1, 0) * wv[0:1, :] + d * wv[1:2, :]
                         + pltpu.roll(d, 1, 0) * wv[2:3, :])
                o_ref[half, :, cols] = d_pre[main].astype(BF16)
                dm = d[main]
                dw_ref[j, half, 0:1, cols] += jnp.sum(dm * a_dn[main], axis=0, keepdims=True)
                dw_ref[j, half, 1:2, cols] += jnp.sum(dm * a_ext[main], axis=0, keepdims=True)
                dw_ref[j, half, 2:3, cols] += jnp.sum(dm * a_up[main], axis=0, keepdims=True)
                db_ref[j, half, :, cols] += jnp.sum(dm, axis=0, keepdims=True)

    dxp, dxn = _row_halo_specs(ts, s, halo, D_MODEL)
    nb, lastb = ts // halo, s // halo - 1
    a_main = pl.BlockSpec((2, ts, tg), lambda i, j: (0, i, j))
    a_prev = pl.BlockSpec((2, halo, tg), lambda i, j: (0, jnp.maximum(i * nb - 1, 0), j))
    a_next = pl.BlockSpec((2, halo, tg), lambda i, j: (0, jnp.minimum((i + 1) * nb, lastb), j))
    da_pre, dw, db = pl.pallas_call(
        body, name="ffn_bwd", grid=(n, ng),
        in_specs=[pl.BlockSpec((ts, D_MODEL), lambda i, j: (i, 0)), dxp, dxn,
                  pl.BlockSpec((tg, D_MODEL), lambda i, j: (j, 0)), a_main, a_prev, a_next,
                  pl.BlockSpec((2, 8, tg), lambda i, j: (0, 0, j)), pl.BlockSpec((2, 1, tg), lambda i, j: (0, 0, j))],
        out_specs=[a_main, pl.BlockSpec((ng, 2, 8, tg), lambda i, j: (0, 0, 0, 0)),
                   pl.BlockSpec((ng, 2, 1, tg), lambda i, j: (0, 0, 0, 0))],
        out_shape=[jax.ShapeDtypeStruct((2, s, D_FF), BF16), jax.ShapeDtypeStruct((ng, 2, 8, tg), F32),
                   jax.ShapeDtypeStruct((ng, 2, 1, tg), F32)],
        compiler_params=_params(("arbitrary", "arbitrary")),
    )(dx2, dx2, dx2, w_down, a_pre, a_pre, a_pre, w, b)
    return (da_pre, dw.transpose(1, 2, 0, 3).reshape(2, 8, D_FF), db.transpose(1, 2, 0, 3).reshape(2, 1, D_FF))


def _ple_final(x2, n3, p, target, gf, w_pg, w_pp, *, ts=512):
    s, d = x2.shape
    dp = p.shape[1]

    def body(x2_ref, n3_ref, p_ref, t_ref, gf_ref, wg_ref, wp_ref, loss_ref, dx3_ref, dgl_ref, dpp_ref, dgf_ref):
        @pl.when(pl.program_id(0) == 0)
        def _():
            loss_ref[...] = jnp.zeros_like(loss_ref)
            dgf_ref[...] = jnp.zeros_like(dgf_ref)

        gate = _sigmoid(jnp.dot(n3_ref[...], wg_ref[...], preferred_element_type=F32))
        ppv = jnp.dot(p_ref[...].astype(BF16), wp_ref[...], preferred_element_type=F32)
        x3 = x2_ref[...] + gate * ppv
        gfv = gf_ref[...]
        err = x3 * _rms_scale(x3) * gfv - t_ref[...]
        loss_ref[...] += 0.5 * jnp.sum(jnp.mean(err * err, axis=-1, keepdims=True), axis=0, keepdims=True)
        dx3, dgf_rows = _rms_bwd_rows(x3, gfv, err * (1.0 / d))
        dgf_ref[...] += jnp.sum(dgf_rows, axis=0, keepdims=True)
        dx3_ref[...] = dx3
        dgl_ref[...] = (dx3 * ppv * gate * (1.0 - gate)).astype(BF16)
        dpp_ref[...] = (dx3 * gate).astype(BF16)

    row = pl.BlockSpec((ts, d), lambda i: (i, 0))
    vec = pl.BlockSpec((1, d), lambda i: (0, 0))
    return pl.pallas_call(
        body, name="ple_final", grid=(s // ts,),
        in_specs=[row, row, pl.BlockSpec((ts, dp), lambda i: (i, 0)), row, vec,
                  pl.BlockSpec((d, d), lambda i: (0, 0)), pl.BlockSpec((dp, d), lambda i: (0, 0))],
        out_specs=[pl.BlockSpec((1, 128), lambda i: (0, 0)), row, row, row, vec],
        out_shape=[jax.ShapeDtypeStruct((1, 128), F32), jax.ShapeDtypeStruct((s, d), F32),
                   jax.ShapeDtypeStruct((s, d), BF16), jax.ShapeDtypeStruct((s, d), BF16),
                   jax.ShapeDtypeStruct((1, d), F32)],
        compiler_params=_params(("arbitrary",)),
    )(x2, n3, p, target, gf, w_pg, w_pp)


def _row_tile(rows, cols, n_arrays, budget=12 << 20):
    best = None
    for t in range(8, rows + 1, 8):
        if rows % t == 0 and t * cols * 4 * n_arrays <= budget:
            best = t
    return rows if best is None else best


def _sum_slots(a, *, name):
    g, r, c = a.shape
    tr = _row_tile(r, c, g + 1)

    def body(*refs):
        tot = refs[0][...]
        for ref in refs[1:g]:
            tot = tot + ref[...]
        refs[g][...] = tot

    specs = [pl.BlockSpec((None, tr, c), functools.partial(lambda i, slot: (slot, i, 0), slot=k)) for k in range(g)]
    return pl.pallas_call(
        body, name=name, grid=(r // tr,), in_specs=specs, out_specs=pl.BlockSpec((tr, c), lambda i: (i, 0)),
        out_shape=jax.ShapeDtypeStruct((r, c), a.dtype), compiler_params=_params(("parallel",)),
    )(*([a] * g))


def _adamw_refs(w_ref, g_ref, m_ref, v_ref, d_ref, mo_ref, vo_ref):
    gv = g_ref[...]
    mn = ADAM_B1 * m_ref[...] + (1.0 - ADAM_B1) * gv
    vn = ADAM_B2 * v_ref[...] + (1.0 - ADAM_B2) * (gv * gv)
    m_hat = mn / (1.0 - ADAM_B1 ** ADAM_STEP)
    v_hat = vn / (1.0 - ADAM_B2 ** ADAM_STEP)
    d_ref[...] = -ADAM_LR * (m_hat / (jnp.sqrt(v_hat) + ADAM_EPS) + ADAM_WD * w_ref[...])
    mo_ref[...] = mn
    vo_ref[...] = vn


def _adamw_many(ws, gs, ms, vs, *, name):
    n = len(ws)

    def body(*refs):
        ins, outs = refs[:4 * n], refs[4 * n:]
        for a in range(n):
            _adamw_refs(ins[a], ins[n + a], ins[2 * n + a], ins[3 * n + a], outs[3 * a], outs[3 * a + 1],
                        outs[3 * a + 2])

    vm = pl.BlockSpec(memory_space=pltpu.VMEM)
    res = pl.pallas_call(
        body, name=name, in_specs=[vm] * (4 * n), out_specs=[vm] * (3 * n),
        out_shape=[jax.ShapeDtypeStruct(a.shape, F32) for a in ws for _ in range(3)],
    )(*ws, *gs, *ms, *vs)
    return [tuple(res[3 * a:3 * a + 3]) for a in range(n)]


def _adamw(w, g, m, v, *, name):
    r, c = w.shape
    tr = _row_tile(r, c, 7)
    body = _adamw_refs

    blk = pl.BlockSpec((tr, c), lambda i: (i, 0))
    return pl.pallas_call(
        body, name=name, grid=(r // tr,), in_specs=[blk] * 4, out_specs=[blk] * 3,
        out_shape=[jax.ShapeDtypeStruct((r, c), F32)] * 3, compiler_params=_params(("parallel",)),
    )(w, g, m, v)


def _position():
    x, y, c = lax.axis_index("x"), lax.axis_index("y"), lax.axis_index("c")
    return x, y, c


def _other_chips(x, y):
    return [(1 - x, y), (x, 1 - y), (1 - x, 1 - y)]


def _stage_in(srcs, stage, sems):
    cps = [pltpu.make_async_copy(src, stage[a], sems.at[a]) for a, src in enumerate(srcs)]
    for cp in cps:
        cp.start()
    return cps


def _stage_out(staged, stage, dsts, sems):
    cps = []
    for a, dst in enumerate(dsts):
        staged[a].wait()
        cp = pltpu.make_async_copy(stage[a], dst, sems.at[a])
        cp.start()
        cps.append(cp)
    return cps


def _send_other_halves(grads, *, tag):
    n = len(grads)

    def body(*refs):
        ins, sib = refs[:n], refs[n:2 * n]
        send_sems, recv_sems = refs[2 * n:]
        x, y, c = _position()
        remote = []
        for a in range(n):
            half = ins[a].shape[1] // 2
            give = ins[a].at[:, pl.ds(pl.multiple_of((1 - c) * half, 8), half), :]
            rc = pltpu.make_async_remote_copy(
                src_ref=give, dst_ref=sib[a], send_sem=send_sems.at[a], recv_sem=recv_sems.at[a],
                device_id=(x, y, 1 - c), device_id_type=MESH)
            rc.start()
            remote.append(rc)
        for rc in remote:
            rc.wait_recv()
        for rc in remote:
            rc.wait_send()

    return pl.pallas_call(
        body, name="send_other_halves_" + tag, in_specs=[ANY] * n, out_specs=[ANY] * n,
        out_shape=[jax.ShapeDtypeStruct((g.shape[0], g.shape[1] // 2, g.shape[2]), g.dtype) for g in grads],
        scratch_shapes=[pltpu.SemaphoreType.DMA((n,)), pltpu.SemaphoreType.DMA((n,))],
        compiler_params=pltpu.CompilerParams(has_side_effects=True),
    )(*grads)


def _add_own_half(g4, sib, core, *, name):
    g, a2, c = sib.shape
    tr = _row_tile(a2, c, 4)

    def body(core_ref, a_ref, b_ref, o_ref, o16_ref):
        tot = a_ref[...] + b_ref[...]
        o_ref[...] = tot
        o16_ref[...] = tot.astype(BF16)

    blk = pl.BlockSpec((None, tr, c), lambda i, j, core_ref: (i, j, 0))
    return pl.pallas_call(
        body, name=name,
        grid_spec=pltpu.PrefetchScalarGridSpec(
            num_scalar_prefetch=1, grid=(g, a2 // tr),
            in_specs=[pl.BlockSpec((None, None, tr, c), lambda i, j, core_ref: (i, core_ref[0], j, 0)), blk],
            out_specs=[blk, blk]),
        out_shape=[jax.ShapeDtypeStruct(sib.shape, F32), jax.ShapeDtypeStruct(sib.shape, BF16)],
        compiler_params=_params(("parallel", "parallel")),
    )(core, g4.reshape(g, 2, a2, c), sib)


def _sum_chips(landed, own, chip, *, name):
    g, r, c = landed.shape
    tr = _row_tile(r, c, 5)

    def body(chip_ref, *refs):
        me = chip_ref[0]
        own_v = refs[g][...]
        tot = None
        for slot in range(g):
            term = jnp.where(me == slot, own_v, refs[slot][...].astype(F32))
            tot = term if tot is None else tot + term
        refs[g + 1][...] = tot

    def landed_spec(slot):
        return pl.BlockSpec((None, tr, c),
                            lambda i, chip_ref: (jnp.where(chip_ref[0] == slot, (slot + 1) % g, slot), i, 0))

    return pl.pallas_call(
        body, name=name,
        grid_spec=pltpu.PrefetchScalarGridSpec(
            num_scalar_prefetch=1, grid=(r // tr,),
            in_specs=[landed_spec(k) for k in range(g)]
            + [pl.BlockSpec((None, tr, c), lambda i, chip_ref: (chip_ref[0], i, 0))],
            out_specs=pl.BlockSpec((tr, c), lambda i, chip_ref: (i, 0))),
        out_shape=jax.ShapeDtypeStruct((r, c), F32), compiler_params=_params(("parallel",)),
    )(chip, *([landed] * g), own)


def _join_halves(halves):
    n = len(halves)

    def body(*refs):
        ins, outs, stage = refs[:n], refs[n:2 * n], refs[2 * n:3 * n]
        send_sems, recv_sems, in_sems, out_sems = refs[3 * n:]
        x, y, c = _position()
        remote = []
        staged = _stage_in(ins, stage, in_sems)
        for a in range(n):
            rc = pltpu.make_async_remote_copy(
                src_ref=ins[a], dst_ref=outs[a].at[c], send_sem=send_sems.at[a], recv_sem=recv_sems.at[a],
                device_id=(x, y, 1 - c), device_id_type=MESH)
            rc.start()
            remote.append(rc)
        local = _stage_out(staged, stage, [o.at[c] for o in outs], out_sems)
        for a in range(n):
            pltpu.make_async_remote_copy(
                src_ref=ins[a], dst_ref=outs[a].at[1 - c], send_sem=send_sems.at[a], recv_sem=recv_sems.at[a],
                device_id=(x, y, 1 - c), device_id_type=MESH).wait_recv()
        for rc in remote:
            rc.wait_send()
        for cp in local:
            cp.wait()

    return pl.pallas_call(
        body, name="join_halves", in_specs=[ANY] * n, out_specs=[ANY] * n,
        out_shape=[jax.ShapeDtypeStruct((2,) + h.shape, h.dtype) for h in halves],
        scratch_shapes=[pltpu.VMEM(h.shape, h.dtype) for h in halves]
        + [pltpu.SemaphoreType.DMA((n,)), pltpu.SemaphoreType.DMA((n,)), pltpu.SemaphoreType.DMA((n,)),
           pltpu.SemaphoreType.DMA((n,))],
        compiler_params=pltpu.CompilerParams(has_side_effects=True),
    )(*halves)


_HBM = pl.BlockSpec(memory_space=pltpu.HBM)
_SEM = pl.BlockSpec(memory_space=pltpu.SEMAPHORE)


def _chip_copies(srcs, lands, send_sems, recv_sems, scatter):
    x, y, c = _position()
    me = 2 * x + y
    outgoing, incoming = [], []
    for a, (src, land) in enumerate(zip(srcs, lands)):
        for k, (px, py) in enumerate(_other_chips(x, y)):
            peer = 2 * px + py
            sems = dict(send_sem=send_sems.at[3 * a + k], recv_sem=recv_sems.at[3 * a + k], device_id=(px, py, c),
                        device_id_type=MESH)
            outgoing.append(pltpu.make_async_remote_copy(
                src_ref=src.at[peer] if scatter else src, dst_ref=land.at[me], **sems))
            incoming.append(pltpu.make_async_remote_copy(
                src_ref=src.at[me] if scatter else src, dst_ref=land.at[peer], **sems))
    return outgoing, incoming


def _chips_start(srcs, *, scatter, name):
    n = len(srcs)
    lands = [lax.empty(a.shape if scatter else (N_CHIPS,) + a.shape, a.dtype) for a in srcs]

    def body(*refs):
        ins, send_sems, recv_sems, token = refs[:2 * n], refs[2 * n], refs[2 * n + 1], refs[-1]
        outgoing, _ = _chip_copies(ins[:n], ins[n:], send_sems, recv_sems, scatter)
        for cp in outgoing:
            cp.start()
        token[...] = jnp.zeros_like(token)

    bufs = list(srcs) + lands
    res = pl.pallas_call(
        body, name=name, in_specs=[_HBM] * (2 * n),
        out_specs=(_SEM, _SEM, *[_HBM] * (2 * n), pl.BlockSpec(memory_space=pltpu.VMEM)),
        out_shape=(pltpu.SemaphoreType.DMA((3 * n,)), pltpu.SemaphoreType.DMA((3 * n,)),
                   *[pltpu.HBM(a.shape, a.dtype) for a in bufs], jax.ShapeDtypeStruct((8, 128), F32)),
        input_output_aliases={i: 2 + i for i in range(2 * n)},
        compiler_params=pltpu.CompilerParams(has_side_effects=pltpu.SideEffectType.DATAFLOW_SIDE_EFFECTING),
    )(*[pltpu.with_memory_space_constraint(a, pltpu.HBM) for a in bufs])
    return res[0], res[1], list(res[2:2 + n]), list(res[2 + n:2 + 2 * n]), res[-1]


def _chips_wait(handle, after, *, scatter, name):
    send_sems, recv_sems, srcs, lands, _ = handle
    n = len(srcs)

    def body(*refs):
        ins, send_ref, recv_ref = refs[:2 * n], refs[2 * n], refs[2 * n + 1]
        outgoing, incoming = _chip_copies(ins[:n], ins[n:], send_ref, recv_ref, scatter)
        for cp in outgoing:
            cp.wait_send()
        for cp in incoming:
            cp.wait_recv()

    bufs = list(srcs) + list(lands)
    res = pl.pallas_call(
        body, name=name, in_specs=[_HBM] * (2 * n) + [_SEM, _SEM, ANY], out_specs=tuple([_HBM] * (2 * n)),
        out_shape=tuple(pltpu.HBM(a.shape, a.dtype) for a in bufs),
        input_output_aliases={i: i for i in range(2 * n)},
        compiler_params=pltpu.CompilerParams(has_side_effects=pltpu.SideEffectType.DATAFLOW_SIDE_EFFECTING),
    )(*bufs, send_sems, recv_sems, after)
    return list(res[:n]), list(res[n:])


def _gather_all(buf):
    def body(in_ref, out_ref, send_sems, recv_sems, local_sem):
        x, y, c = _position()
        me = 4 * x + 2 * y + c
        peers = [(x, y, 1 - c)] + [(px, py, pc) for (px, py) in _other_chips(x, y) for pc in (c, 1 - c)]
        cp = pltpu.make_async_copy(in_ref, out_ref.at[me], local_sem)
        cp.start()
        remote = []
        for k, peer in enumerate(peers):
            rc = pltpu.make_async_remote_copy(
                src_ref=in_ref, dst_ref=out_ref.at[me], send_sem=send_sems.at[k], recv_sem=recv_sems.at[k],
                device_id=peer, device_id_type=MESH)
            rc.start()
            remote.append(rc)
        for k, (px, py, pc) in enumerate(peers):
            pltpu.make_async_remote_copy(
                src_ref=in_ref, dst_ref=out_ref.at[4 * px + 2 * py + pc], send_sem=send_sems.at[k],
                recv_sem=recv_sems.at[k], device_id=(px, py, pc), device_id_type=MESH).wait_recv()
        for rc in remote:
            rc.wait_send()
        cp.wait()

    return pl.pallas_call(
        body, name="gather_all", in_specs=[ANY], out_specs=ANY,
        out_shape=jax.ShapeDtypeStruct((N_DEV,) + buf.shape, buf.dtype),
        scratch_shapes=[pltpu.SemaphoreType.DMA((N_DEV - 1,)), pltpu.SemaphoreType.DMA((N_DEV - 1,)),
                        pltpu.SemaphoreType.DMA],
        compiler_params=pltpu.CompilerParams(has_side_effects=True),
    )(buf)


def _cols_from_shards(g4):
    _, k, n = g4.shape
    return g4.transpose(1, 0, 2).reshape(k, N_CHIPS * n)


def _cols_to_shards(w):
    k, n = w.shape
    return w.reshape(k, N_CHIPS, n // N_CHIPS).transpose(1, 0, 2)


def _pad_heads(w, width):
    k = w.shape[0]
    w3 = w.reshape(k, N_HEADS, width)
    return jnp.pad(w3, ((0, 0), (0, 0), (0, HEAD_PAD - width))).reshape(k, D_ATT)


def _unpad_heads(w, width):
    k = w.shape[0]
    return w.reshape(k, N_HEADS, HEAD_PAD)[:, :, :width]


def _rope_tables(s, after):
    pos = jnp.arange(s, dtype=F32) + after
    inv_freq = ROPE_THETA ** (-jnp.arange(0, QK_ROPE, 2, dtype=F32) / QK_ROPE)
    ang = pos[:, None] * inv_freq[None, :]
    cos_h, sin_h = jnp.cos(ang), jnp.sin(ang)
    half = QK_ROPE // 2
    z = jnp.zeros((s, half), F32)
    ones = jnp.ones((s, QK_NOPE), F32)
    tail = jnp.zeros((s, HEAD_PAD - QK_NOPE - QK_ROPE), F32)
    cos = jnp.concatenate([ones, cos_h, cos_h, tail + 1.0], axis=1)
    sin_a = jnp.concatenate([ones * 0.0, -sin_h, z, tail], axis=1)
    sin_b = jnp.concatenate([ones * 0.0, z, sin_h, tail], axis=1)
    return cos, sin_a, sin_b


def _local_step(x, p, target, wts, late_weights, reduce_early, reduce_last):
    s = x.shape[0]
    cos, sin_a, sin_b = wts["rope"]
    g1, gq, gkv, g2, g3, gf = (wts[k] for k in ("norm_mix_g", "q_norm_g", "kv_norm_g", "norm_ffn_g", "ple_norm_g",
                                                 "final_norm_g"))
    w_in_p, w_uq_p, w_kv_p = wts["w_in_p"], wts["w_uq_p"], wts["w_kv_p"]
    conv_w8, fconv_w, fconv_b = wts["conv_w8"], wts["ffn_conv_w"], wts["ffn_conv_b"]

    (h, z), _ = _mm_fused(x, w_in_p, name="mm_in", prologue=_pro_rms, vecs=[g1], epilogue=_epi_plain, row_outs=[F32],
                          tm=1024)
    y_conv, qn, kvn, kr = _mix_pre(z, conv_w8, gq, gkv, cos, sin_a, sin_b)
    q, k, v, q_t = _qkv_proj(qn, kvn, kr, w_uq_p, w_kv_p, cos, sin_a, sin_b)
    o, lse = _flash_fwd(q, k, v)
    late = late_weights(lse)
    w_o_a, w_o_b, w_up, w_down = late["w_o_a"], late["w_o_b"], late["w_up"], late["w_down"]
    w_pg, w_pp = late["w_ple_gate"], late["w_ple_proj"]
    (x1, hf), _ = _mm_fused(o, w_o_b, second=(y_conv, w_o_a), name="mm_o", rows=[x], vecs=[g2],
                            epilogue=_epi_add_rms, row_outs=[F32, BF16], tm=1024)
    a_pre, act = _ffn_fwd(hf, w_up, fconv_w, fconv_b)
    (x2, n3), _ = _mm_fused(act, w_down, name="mm_down", rows=[x1], vecs=[g3], epilogue=_epi_add_rms,
                            row_outs=[F32, BF16], tm=1024)
    loss, dx3, dgl, dpp, d_gf = _ple_final(x2, n3, p, target, gf, w_pg, w_pp)

    grads, early = {"final_norm_g": d_gf}, {}
    early["w_ple_proj"] = _mm(p, dpp, ta=True, name="mm_d_wpp", tm=256, tn=1024, tk=2048)
    early["w_ple_gate"] = _mm(n3, dgl, ta=True, name="mm_d_wpg", tm=1024, tn=1024, tk=2048)
    (dx2,), (grads["ple_norm_g"],) = _mm_fused(dgl, w_pg, tb=True, name="mm_d_n3", rows=[x2, dx3], vecs=[g3],
                                               epilogue=_epi_rms_bwd, row_outs=[F32], n_vec_out=1, tm=1024)
    early["w_down"] = _mm(act, dx2, ta=True, name="mm_d_wdown", tm=1408, tn=1024, tk=2048)
    da_pre, grads["ffn_conv_w"], grads["ffn_conv_b"] = _ffn_bwd(dx2, w_down, a_pre, fconv_w, fconv_b)
    early["w_up"] = _mm(hf, da_pre, ta=True, b_split=True, name="mm_d_wup", tm=1024, tn=1408, tk=2048,
                       o_shards=True)
    (dx1,), (grads["norm_ffn_g"],) = _mm_fused(da_pre, w_up, tb=True, a_split=True, name="mm_d_hf", rows=[x1, dx2],
                                               vecs=[g2], epilogue=_epi_rms_bwd, row_outs=[F32], n_vec_out=1)
    d_wo_a = _mm(y_conv, dx1, ta=True, name="mm_d_wo_conv", tm=512, tn=1024, tk=2048)
    d_wo_b = _mm(o, dx1, ta=True, name="mm_d_wo_att", tm=1024, tn=1024, tk=2048)
    early["w_o"] = jnp.concatenate([d_wo_a, d_wo_b.reshape(N_HEADS, HEAD_PAD, D_MODEL)[:, :V_HEAD]
                                    .reshape(N_HEADS * V_HEAD, D_MODEL)], axis=0)
    token, finish = reduce_early(early)
    dyc = _mm(dx1, w_o_a, tb=True, name="mm_d_yconv", tm=512, tn=512, tk=1024)
    (do, do_t), _ = _mm_fused(dx1, w_o_b, tb=True, name="mm_d_o", epilogue=_epi_plain, row_outs=[BF16],
                              transposed_out=BF16, tm=1024)
    delta = _attn_delta(do, o)
    dq, dk, dv = _flash_bwd(q, q_t, k, v, do, do_t, lse, delta, token)
    reduced_early = finish(dq)
    dq_pre, dkr = _qk_bwd(dq, dk, cos, sin_a, sin_b)
    grads["w_uq_p"] = _mm(qn, dq_pre, ta=True, name="mm_d_wuq", tm=256, tn=1024, tk=2048)
    dqn = _mm(dq_pre, w_uq_p, tb=True, name="mm_d_qn", tm=512, tn=256, tk=1024)
    grads["w_k_p"] = _mm(kvn, dk, ta=True, name="mm_d_wk", tm=128, tn=1024, tk=2048)
    grads["w_v_p"] = _mm(kvn, dv, ta=True, name="mm_d_wv", tm=128, tn=1024, tk=2048)
    (dkvn,), _ = _mm_fused(dk, w_kv_p[:, :D_ATT], tb=True, second=(dv, w_kv_p[:, D_ATT:].T), name="mm_d_kvn",
                           epilogue=_epi_plain, row_outs=[F32])
    dz, grads["conv_w"], grads["q_norm_g"], grads["kv_norm_g"] = _mix_bwd(
        z, dyc, dqn, dkvn, dkr, conv_w8, gq, gkv, cos, sin_a, sin_b)
    grads["w_in_p"] = _mm(h, dz, ta=True, name="mm_d_win", tm=1024, tn=1024, tk=2048)
    token, finish = reduce_last({n: grads.pop(n) for n in ("w_in_p", "w_uq_p", "w_k_p", "w_v_p")})
    (grad_x,), (grads["norm_mix_g"],) = _mm_fused(dz, w_in_p, tb=True, name="mm_d_h", rows=[x, dx1],
                                                  vecs=[g1 + token[0, 0]], epilogue=_epi_rms_bwd, row_outs=[F32],
                                                  n_vec_out=1, tm=1024)
    return loss[0, 0], grad_x, grads, reduced_early, finish(grad_x)


_EARLY_W = ("w_in", "w_uq", "w_ukv")
_LATE_W = ("w_o", "w_up", "w_down", "w_ple_gate", "w_ple_proj")
_BIG = _EARLY_W + _LATE_W
_COL_SHARDED = ("w_in", "w_uq", "w_ukv", "w_up", "w_ple_proj")
_SMALL = ("norm_mix_g", "conv_w", "q_norm_g", "kv_norm_g", "norm_ffn_g", "ffn_conv_w", "ffn_conv_b", "ple_norm_g",
          "final_norm_g")


def _full_from_slots(n, g4):
    return _cols_from_shards(g4) if n in _COL_SHARDED else g4.reshape(-1, g4.shape[2])


def _shard_major(n, g):
    if g.ndim == 3:
        return g
    return _cols_to_shards(g) if n in _COL_SHARDED else g.reshape(N_CHIPS, g.shape[0] // N_CHIPS, g.shape[1])


def _early_shards(w):
    shards = [w[n][0].astype(BF16) for n in _EARLY_W]
    shards.append(jnp.pad(w["conv_w"][0], ((0, 5), (0, 0))))
    shards.append(jnp.pad(w["ffn_conv_w"][0], ((0, 5), (0, 0))))
    return shards


def _fill_own_slot(landed, own, chip):
    return [lax.dynamic_update_slice(g4, a[None], (chip[0], 0, 0)) for g4, a in zip(landed, own)]


def _early_weights(got, w):
    full = {n: _full_from_slots(n, g4) for n, g4 in zip(_EARLY_W, got)}
    full["conv_w8"] = _cols_from_shards(got[len(_EARLY_W)])
    full["ffn_conv_w8"] = _cols_from_shards(got[len(_EARLY_W) + 1])
    return _layout_early(full, w)


def _layout_early(full, w):
    out = {n: w[n] for n in ("norm_mix_g", "q_norm_g", "kv_norm_g", "norm_ffn_g", "ple_norm_g")}
    out["final_norm_g"] = w["final_norm_g"][None, :]
    w_in = full["w_in"]
    zc = jnp.zeros((D_MODEL, QK_NOPE), BF16)
    zt = jnp.zeros((D_MODEL, HEAD_PAD - QK_NOPE - QK_ROPE), BF16)
    out["w_in_p"] = jnp.concatenate([w_in[:, :D_IN - QK_ROPE], zc, w_in[:, D_IN - QK_ROPE:], zt], axis=1)
    out["w_uq_p"] = _pad_heads(full["w_uq"], QK_NOPE + QK_ROPE)
    kv3 = full["w_ukv"].reshape(KV_LORA, N_HEADS, QK_NOPE + V_HEAD)
    out["w_kv_p"] = jnp.concatenate([_pad_heads(kv3[:, :, :QK_NOPE].reshape(KV_LORA, -1), QK_NOPE),
                                     _pad_heads(kv3[:, :, QK_NOPE:].reshape(KV_LORA, -1), V_HEAD)], axis=1)
    out["conv_w8"] = full["conv_w8"]
    fw = full["ffn_conv_w8"]
    out["ffn_conv_w"] = jnp.stack([fw[:, :D_FF], fw[:, D_FF:]])
    out["ffn_conv_b"] = w["ffn_conv_b"].reshape(2, 1, D_FF)
    return out


def _layout_late(full):
    w_o = full["w_o"]
    out = {"w_o_a": w_o[:CONV_WIDTH]}
    out["w_o_b"] = jnp.pad(w_o[CONV_WIDTH:].reshape(N_HEADS, V_HEAD, D_MODEL),
                           ((0, 0), (0, HEAD_PAD - V_HEAD), (0, 0))).reshape(D_ATT, D_MODEL)
    for n in ("w_up", "w_down", "w_ple_gate", "w_ple_proj"):
        out[n] = full[n]
    return out


def _true_matrices(g):
    out = {}
    wp = g["w_in_p"]
    out["w_in"] = jnp.concatenate([wp[:, :D_IN - QK_ROPE], wp[:, D_IN_PAD - HEAD_PAD + QK_NOPE:
                                                              D_IN_PAD - HEAD_PAD + QK_NOPE + QK_ROPE]], axis=1)
    out["w_uq"] = _unpad_heads(g["w_uq_p"], QK_NOPE + QK_ROPE).reshape(Q_LORA, -1)
    out["w_ukv"] = jnp.concatenate([_unpad_heads(g["w_k_p"], QK_NOPE), _unpad_heads(g["w_v_p"], V_HEAD)],
                                   axis=2).reshape(KV_LORA, -1)
    return out


def _true_vectors(g):
    out = {}
    out["conv_w"] = g["conv_w"]
    fw = g["ffn_conv_w"]
    out["ffn_conv_w"] = jnp.concatenate([fw[0, :3], fw[1, :3]], axis=1)
    out["ffn_conv_b"] = g["ffn_conv_b"].reshape(1, 2 * D_FF)
    for n in ("norm_mix_g", "q_norm_g", "kv_norm_g", "norm_ffn_g", "ple_norm_g", "final_norm_g"):
        out[n] = g[n]
    return out


def _chip_partials(names, g, core, *, tag):
    g4 = [_shard_major(n, g[n]) for n in names]
    sib = _send_other_halves(g4, tag=tag)
    return [_add_own_half(a, b, core, name="add_cores_" + n) for n, a, b in zip(names, g4, sib)]


_SMALL_SIZES = {"norm_mix_g": D_MODEL, "conv_w": 3 * CONV_WIDTH, "q_norm_g": Q_LORA, "kv_norm_g": KV_LORA,
                "norm_ffn_g": D_MODEL, "ffn_conv_w": 6 * D_FF, "ffn_conv_b": 2 * D_FF, "ple_norm_g": D_MODEL,
                "final_norm_g": D_MODEL}


def _pack(parts, rows):
    flat = jnp.concatenate([a.reshape(-1) for a in parts])
    return jnp.pad(flat, (0, rows * 128 - flat.shape[0])).reshape(rows, 128)


def _unpack(buf, sizes):
    flat = buf.reshape(-1)
    out, at = [], 0
    for n in sizes:
        out.append(flat[at:at + n])
        at += n
    return out


def _reduce_small(g, loss):
    sizes = [1] + [_SMALL_SIZES[n] for n in _SMALL]
    rows = -(-sum(sizes) // 1024) * 8
    slots = _gather_all(_pack([loss] + [g[n] for n in _SMALL], rows))
    parts = _unpack(_sum_slots(slots, name="sum_small"), sizes)
    return parts[0][0], dict(zip(_SMALL, parts[1:]))


def kernel(x, p, norm_mix_g, w_in, conv_w, q_norm_g, w_uq, kv_norm_g, w_ukv, w_o, norm_ffn_g, w_up, ffn_conv_w, ffn_conv_b, w_down, ple_norm_g, w_ple_gate, w_ple_proj, final_norm_g, loss_target, m_norm_mix_g, m_w_in, m_conv_w, m_q_norm_g, m_w_uq, m_kv_norm_g, m_w_ukv, m_w_o, m_norm_ffn_g, m_w_up, m_ffn_conv_w, m_ffn_conv_b, m_w_down, m_ple_norm_g, m_w_ple_gate, m_w_ple_proj, m_final_norm_g, v_norm_mix_g, v_w_in, v_conv_w, v_q_norm_g, v_w_uq, v_kv_norm_g, v_w_ukv, v_w_o, v_norm_ffn_g, v_w_up, v_ffn_conv_w, v_ffn_conv_b, v_w_down, v_ple_norm_g, v_w_ple_gate, v_w_ple_proj, v_final_norm_g):
    names = ["norm_mix_g", "w_in", "conv_w", "q_norm_g", "w_uq", "kv_norm_g", "w_ukv", "w_o", "norm_ffn_g", "w_up",
             "ffn_conv_w", "ffn_conv_b", "w_down", "ple_norm_g", "w_ple_gate", "w_ple_proj", "final_norm_g"]
    w = dict(zip(names, (norm_mix_g, w_in, conv_w, q_norm_g, w_uq, kv_norm_g, w_ukv, w_o, norm_ffn_g, w_up,
                         ffn_conv_w, ffn_conv_b, w_down, ple_norm_g, w_ple_gate, w_ple_proj, final_norm_g)))
    m = dict(zip(names, (m_norm_mix_g, m_w_in, m_conv_w, m_q_norm_g, m_w_uq, m_kv_norm_g, m_w_ukv, m_w_o,
                         m_norm_ffn_g, m_w_up, m_ffn_conv_w, m_ffn_conv_b, m_w_down, m_ple_norm_g, m_w_ple_gate,
                         m_w_ple_proj, m_final_norm_g)))
    v = dict(zip(names, (v_norm_mix_g, v_w_in, v_conv_w, v_q_norm_g, v_w_uq, v_kv_norm_g, v_w_ukv, v_w_o,
                         v_norm_ffn_g, v_w_up, v_ffn_conv_w, v_ffn_conv_b, v_w_down, v_ple_norm_g, v_w_ple_gate,
                         v_w_ple_proj, v_final_norm_g)))

    core = lax.axis_index("c").astype(jnp.int32).reshape(1)
    chip = (2 * lax.axis_index("x") + lax.axis_index("y")).astype(jnp.int32).reshape(1)

    first = _chips_start(_early_shards(w), scatter=False, name="gather_early_start")
    rope = _rope_tables(x.shape[1], first[4][0, 0])
    late_shards = [w[n][0].astype(BF16) for n in _LATE_W]
    ready, *late_shards = lax.optimization_barrier((rope[0], *late_shards))
    own, landed = _chips_wait(first, ready, scatter=False, name="gather_early_wait")
    wts = _early_weights(_fill_own_slot(landed, own, chip), w)
    wts["rope"] = (ready,) + tuple(rope[1:])
    late_shards[0], _ = lax.optimization_barrier((late_shards[0], own[0]))
    gather = _chips_start(late_shards, scatter=False, name="gather_late_start")
    wts["norm_mix_g"] = wts["norm_mix_g"] + gather[4][0, 0]

    def late_weights(after):
        shards, landed = _chips_wait(gather, after, scatter=False, name="gather_late_wait")
        return _layout_late({n: _full_from_slots(n, g4)
                             for n, g4 in zip(_LATE_W, _fill_own_slot(landed, shards, chip))})

    def reduce_early(g):
        parts = _chip_partials(_LATE_W, g, core, tag="early")
        scatter = _chips_start([t16 for _, t16 in parts], scatter=True, name="scatter_early_start")

        def finish(after):
            _, landed = _chips_wait(scatter, after, scatter=True, name="scatter_early_wait")
            return [_sum_chips(a, t32, chip, name="sum_chips_" + n) for n, a, (t32, _) in zip(_LATE_W, landed, parts)]

        return scatter[4], finish

    def reduce_last(g):
        parts = _chip_partials(_EARLY_W, _true_matrices(g), core, tag="late")
        scatter = _chips_start([t16 for _, t16 in parts], scatter=True, name="scatter_late_start")

        def finish(after):
            _, landed = _chips_wait(scatter, after, scatter=True, name="scatter_late_wait")
            return [_sum_chips(a, t32, chip, name="sum_chips_" + n) for n, a, (t32, _) in zip(_EARLY_W, landed, parts)]

        return scatter[4], finish

    loss, grad_x, small_grads, halves_early, halves_last = _local_step(
        x[0], p[0, 0], loss_target[0], wts, late_weights, reduce_early, reduce_last)
    g_full = _true_vectors(small_grads)
    whole = _join_halves(halves_last + halves_early)
    big = {n: a.reshape(-1, a.shape[2]) for n, a in zip(_BIG, whole)}

    g_out, d_out, m_out, v_out = {}, {}, {}, {}
    for n in _BIG:
        shape = w[n].shape
        g = big[n]
        d, mn, vn = _adamw(w[n][0], g, m[n][0], v[n][0], name="adamw_" + n)
        g_out[n], d_out[n], m_out[n], v_out[n] = (a.reshape(shape) for a in (g, d, mn, vn))

    loss, small = _reduce_small(g_full, loss)
    chip = 2 * lax.axis_index("x") + lax.axis_index("y")
    g_small = {}
    for n in _SMALL:
        shape = w[n].shape
        g = small[n]
        if n in ("conv_w", "ffn_conv_w"):
            width = shape[-1]
            g = lax.dynamic_slice(g.reshape(3, N_CHIPS * width), (0, chip * width), (3, width))
        g_small[n] = g.reshape(shape)
    flat = [[src[n].reshape(-1, src[n].shape[-1]) for n in _SMALL] for src in (w, g_small, m, v)]
    for n, (d, mn, vn) in zip(_SMALL, _adamw_many(*flat, name="adamw_small")):
        shape = w[n].shape
        g_out[n], d_out[n], m_out[n], v_out[n] = g_small[n], d.reshape(shape), mn.reshape(shape), vn.reshape(shape)

    return (loss, grad_x[None], *[g_out[n] for n in names], *[d_out[n] for n in names],
            *[m_out[n] for n in names], *[v_out[n] for n in names])
```

```python
import functools

import jax
import jax.numpy as jnp
from jax import lax
from jax.experimental import pallas as pl
from jax.experimental.pallas import tpu as pltpu

F32 = jnp.float32
BF16 = jnp.bfloat16

D_MODEL = 1024
CONV_WIDTH = 512
Q_LORA = 256
KV_LORA = 128
QK_NOPE = 64
QK_ROPE = 32
V_HEAD = 64
N_HEADS = 8
HEAD_PAD = 128
D_ATT = N_HEADS * HEAD_PAD
D_IN = 3 * CONV_WIDTH + Q_LORA + KV_LORA + QK_ROPE
D_IN_PAD = 3 * CONV_WIDTH + Q_LORA + KV_LORA + HEAD_PAD
D_FF = 2816
ROPE_THETA = 10000.0
EPS = 1e-6
SM_SCALE = (QK_NOPE + QK_ROPE) ** -0.5
ONES_LANE = V_HEAD

ADAM_LR = 0.001
ADAM_B1 = 0.9
ADAM_B2 = 0.999
ADAM_EPS = 1e-08
ADAM_WD = 0.01
ADAM_STEP = 10

N_CHIPS = 4
N_DEV = 8
MESH = pl.DeviceIdType.MESH
ANY = pl.BlockSpec(memory_space=pl.ANY)


def _params(sem):
    return pltpu.CompilerParams(dimension_semantics=sem)


MM_PIECE = 256


def _pieces(total, width=MM_PIECE):
    return [(off, min(width, total - off)) for off in range(0, total, width)]


def _mm(a, b, *, name, ta=False, tb=False, add=None, out_dtype=F32, tm=512, tn=512, tk=512, b_split=False,
        o_shards=False):
    k, m = a.shape if ta else a.shape[::-1]
    if b_split:
        _, kb, nh = b.shape
        n = 2 * nh
    elif tb:
        n, kb = b.shape
    else:
        kb, n = b.shape
    assert kb == k, (name, a.shape, b.shape)
    tm, tn, tk = min(tm, m), min(tn, n), min(tk, k)
    assert m % tm == 0 and n % tn == 0 and k % tk == 0, (name, m, n, k, tm, tn, tk)
    gm, gn, gk = m // tm, n // tn, k // tk

    a_spec = pl.BlockSpec((tk, tm), lambda i, j, kk: (kk, i)) if ta else pl.BlockSpec((tm, tk), lambda i, j, kk: (i, kk))
    if b_split:
        assert gn % 2 == 0
        b_spec = pl.BlockSpec((None, tk, tn), lambda i, j, kk: (j // (gn // 2), kk, j % (gn // 2)))
    elif tb:
        b_spec = pl.BlockSpec((tn, tk), lambda i, j, kk: (j, kk))
    else:
        b_spec = pl.BlockSpec((tk, tn), lambda i, j, kk: (kk, j))
    if o_shards:
        o_spec = pl.BlockSpec((None, tm, tn), lambda i, j, kk: (j, i, 0))
        o_shape = jax.ShapeDtypeStruct((gn, m, tn), out_dtype)
    else:
        o_spec = pl.BlockSpec((tm, tn), lambda i, j, kk: (i, j))
        o_shape = jax.ShapeDtypeStruct((m, n), out_dtype)
    dims = (((0 if ta else 1,), (1 if tb else 0,)), ((), ()))

    def body(*refs):
        a_ref, b_ref = refs[:2]
        add_ref = None if add is None else refs[2]
        o_ref = refs[2 if add is None else 3]
        acc_ref = None if gk == 1 else refs[-1]
        kk = pl.program_id(2)
        rhs = b_ref[...].astype(BF16)

        def finish(r, rows):
            if add_ref is not None:
                r = r + add_ref[rows, :]
            o_ref[rows, :] = r.astype(o_ref.dtype)

        for off, size in _pieces(tm):
            rows = slice(off, off + size)
            lhs = (a_ref[:, rows] if ta else a_ref[rows, :]).astype(BF16)
            part = lax.dot_general(lhs, rhs, dims, preferred_element_type=F32)
            if gk == 1:
                finish(part, rows)
            else:
                acc_ref[rows, :] = jnp.where(kk == 0, part, acc_ref[rows, :] + part)

        if gk > 1:
            @pl.when(kk == gk - 1)
            def _():
                finish(acc_ref[...], slice(None))

    in_specs = [a_spec, b_spec]
    args = [a, b]
    if add is not None:
        in_specs.append(pl.BlockSpec((tm, tn), lambda i, j, kk: (i, j)))
        args.append(add)
    return pl.pallas_call(
        body, name=name, grid=(gm, gn, gk), in_specs=in_specs, out_specs=o_spec, out_shape=o_shape,
        scratch_shapes=[] if gk == 1 else [pltpu.VMEM((tm, tn), F32)],
        compiler_params=_params(("parallel", "parallel", "arbitrary")),
    )(*args)


def _rms_scale(v):
    return lax.rsqrt(jnp.mean(v * v, axis=-1, keepdims=True) + EPS)


def _rms_bwd_rows(v, g, dy):
    r = _rms_scale(v)
    vh = v * r
    dyg = dy * g
    dv = r * (dyg - vh * jnp.mean(dyg * vh, axis=-1, keepdims=True))
    return dv, dy * vh


def _shift_down(v, first_row):
    row = lax.broadcasted_iota(jnp.int32, v.shape, 0)
    return jnp.where(row == 0, first_row, pltpu.roll(v, 1, 0))


def _shift_up(v, last_row):
    n = v.shape[0]
    row = lax.broadcasted_iota(jnp.int32, v.shape, 0)
    return jnp.where(row == n - 1, last_row, pltpu.roll(v, n - 1, 0))


def _rope(t, cos, sin_a, sin_b):
    return t * cos + pltpu.roll(t, HEAD_PAD - 16, 1) * sin_a + pltpu.roll(t, 16, 1) * sin_b


def _rope_bwd(d, cos, sin_a, sin_b):
    return d * cos + pltpu.roll(d * sin_a, 16, 1) + pltpu.roll(d * sin_b, HEAD_PAD - 16, 1)


def _sigmoid(v):
    return 1.0 / (1.0 + jnp.exp(-v))


def _halo_specs(ts, s, width, col):
    nb = ts // 8
    last = s // 8 - 1
    prev = pl.BlockSpec((8, width), lambda i: (jnp.maximum(i * nb - 1, 0), col))
    nxt = pl.BlockSpec((8, width), lambda i: (jnp.minimum((i + 1) * nb, last), col))
    return prev, nxt


def _mm_fused(a, b, *, name, epilogue, row_outs, rows=(), vecs=(), n_vec_out=0, tb=False, a_split=False,
              prologue=None, second=None, transposed_out=None, tm=512):
    if a_split:
        _, m, kh = a.shape
        k = 2 * kh
    else:
        m, k = a.shape
    n = b.shape[0] if tb else b.shape[1]
    assert (b.shape[1] if tb else b.shape[0]) == k, (name, a.shape, b.shape)
    assert m % tm == 0, (name, m, tm)
    n_a = 2 if a_split else 1
    nr, nv = len(rows), len(vecs)
    n_pro = 0 if prologue is None else 1
    n_sec = 0 if second is None else 2
    n_t = 0 if transposed_out is None else 1
    dims = (((1,), (1 if tb else 0,)), ((), ()))

    def body(*refs):
        a_refs, b_ref = refs[:n_a], refs[n_a]
        refs = refs[n_a + 1:]
        sec_refs = refs[:n_sec]
        row_refs, vec_refs = refs[n_sec:n_sec + nr], refs[n_sec + nr:n_sec + nr + nv]
        outs = refs[n_sec + nr + nv:]
        row_out_refs = outs[n_pro:n_pro + len(row_outs)]
        t_out_refs = outs[n_pro + len(row_outs):n_pro + len(row_outs) + n_t]
        vec_out_refs = outs[n_pro + len(row_outs) + n_t:n_pro + len(row_outs) + n_t + n_vec_out]
        vec_vals = [v[...] for v in vec_refs]
        if a_split:
            kh = k // 2
            rhs = [(b_ref[:, :kh], b_ref[:, kh:]) if tb else (b_ref[:kh, :], b_ref[kh:, :])][0]
            rhs = [h.astype(BF16) for h in rhs]
        else:
            rhs = [b_ref[...].astype(BF16)]
        vec_sums = [None] * n_vec_out

        for off, size in _pieces(tm):
            rs = slice(off, off + size)
            if prologue is None:
                lhs = [a_ref[rs, :].astype(BF16) for a_ref in a_refs]
            else:
                lhs = [prologue(a_refs[0][rs, :], vec_vals)]
                outs[0][rs, :] = lhs[0]
            r = lax.dot_general(lhs[0], rhs[0], dims, preferred_element_type=F32)
            for l2, r2 in zip(lhs[1:], rhs[1:]):
                r = r + lax.dot_general(l2, r2, dims, preferred_element_type=F32)
            if second is not None:
                r = r + jnp.dot(sec_refs[0][rs, :].astype(BF16), sec_refs[1][...].astype(BF16),
                                preferred_element_type=F32)
            row_vals, vec_parts = epilogue(r, [x[rs, :] for x in row_refs], vec_vals)
            for ref, val in zip(row_out_refs, row_vals):
                ref[rs, :] = val.astype(ref.dtype)
            for ref in t_out_refs:
                ref[:, rs] = row_vals[0].T.astype(ref.dtype)
            vec_sums = [p if t is None else t + p for t, p in zip(vec_sums, vec_parts)]

        if n_vec_out:
            @pl.when(pl.program_id(0) == 0)
            def _():
                for ref in vec_out_refs:
                    ref[...] = jnp.zeros_like(ref)

            for ref, val in zip(vec_out_refs, vec_sums):
                ref[...] += val

    if a_split:
        a_specs = [pl.BlockSpec((None, tm, k // 2), lambda i: (0, i, 0)),
                   pl.BlockSpec((None, tm, k // 2), lambda i: (1, i, 0))]
    else:
        a_specs = [pl.BlockSpec((tm, k), lambda i: (i, 0))]
    b_spec = pl.BlockSpec(b.shape, lambda i: (0, 0))
    row_spec = pl.BlockSpec((tm, n), lambda i: (i, 0))
    out_specs, out_shape = [], []
    if prologue is not None:
        out_specs.append(pl.BlockSpec((tm, k), lambda i: (i, 0)))
        out_shape.append(jax.ShapeDtypeStruct((m, k), BF16))
    out_specs += [row_spec] * len(row_outs)
    out_shape += [jax.ShapeDtypeStruct((m, n), dt) for dt in row_outs]
    if transposed_out is not None:
        out_specs.append(pl.BlockSpec((n, tm), lambda i: (0, i)))
        out_shape.append(jax.ShapeDtypeStruct((n, m), transposed_out))
    out_specs += [pl.BlockSpec((1, n), lambda i: (0, 0))] * n_vec_out
    out_shape += [jax.ShapeDtypeStruct((1, n), F32)] * n_vec_out
    sec_specs, sec_args = [], []
    if second is not None:
        k2 = second[0].shape[1]
        sec_specs = [pl.BlockSpec((tm, k2), lambda i: (i, 0)), pl.BlockSpec((k2, n), lambda i: (0, 0))]
        sec_args = list(second)
    res = pl.pallas_call(
        body, name=name, grid=(m // tm,),
        in_specs=a_specs + [b_spec] + sec_specs + [row_spec] * nr
        + [pl.BlockSpec((1, v.shape[1]), lambda i: (0, 0)) for v in vecs],
        out_specs=out_specs, out_shape=out_shape,
        compiler_params=_params(("arbitrary" if n_vec_out else "parallel",)),
    )(*([a] * n_a), b, *sec_args, *rows, *vecs)
    split = n_pro + len(row_outs) + n_t
    return list(res[:split]), list(res[split:])


def _pro_rms(a, vecs):
    return (a * _rms_scale(a) * vecs[0]).astype(BF16)


def _epi_plain(r, rows, vecs):
    return [r], []


def _epi_add_rms(r, rows, vecs):
    xn = r + rows[0]
    return [xn, xn * _rms_scale(xn) * vecs[0]], []


def _epi_rms_bwd(r, rows, vecs):
    dv, dg_rows = _rms_bwd_rows(rows[0], vecs[0], r)
    return [dv + rows[1]], [jnp.sum(dg_rows, axis=0, keepdims=True)]


def _mix_pre(z, conv_w8, gq, gkv, cos, sin_a, sin_b, *, ts=1024):
    s = z.shape[0]
    n = s // ts
    cw = CONV_WIDTH

    def body(z_ref, xcp, xcn, cgp, cgn, w_ref, gq_ref, gkv_ref, cos_ref, sa_ref, sb_ref,
             yc_ref, qn_ref, kvn_ref, kr_ref):
        i = pl.program_id(0)
        xc = z_ref[:, 0:cw]
        bg = z_ref[:, cw:2 * cw]
        cg = z_ref[:, 2 * cw:3 * cw]
        m = cg * xc
        m_prev = jnp.where(i > 0, xcp[7:8, :] * cgp[7:8, :], 0.0)
        m_next = jnp.where(i < n - 1, xcn[0:1, :] * cgn[0:1, :], 0.0)
        cm = _shift_down(m, m_prev) * w_ref[0:1, :] + m * w_ref[1:2, :] + _shift_up(m, m_next) * w_ref[2:3, :]
        yc_ref[...] = (bg * cm).astype(BF16)
        ql = z_ref[:, 3 * cw:3 * cw + Q_LORA]
        qn_ref[...] = (ql * _rms_scale(ql) * gq_ref[...]).astype(BF16)
        kvl = z_ref[:, 3 * cw + Q_LORA:3 * cw + Q_LORA + KV_LORA]
        kvn_ref[...] = (kvl * _rms_scale(kvl) * gkv_ref[...]).astype(BF16)
        kr_ref[...] = _rope(z_ref[:, D_IN_PAD - HEAD_PAD:D_IN_PAD], cos_ref[...], sa_ref[...], sb_ref[...])

    xcp, xcn = _halo_specs(ts, s, cw, 0)
    cgp, cgn = _halo_specs(ts, s, cw, 2)
    tab = pl.BlockSpec((ts, HEAD_PAD), lambda i: (i, 0))
    return pl.pallas_call(
        body, name="mix_pre", grid=(n,),
        in_specs=[pl.BlockSpec((ts, D_IN_PAD), lambda i: (i, 0)), xcp, xcn, cgp, cgn,
                  pl.BlockSpec((8, cw), lambda i: (0, 0)), pl.BlockSpec((1, Q_LORA), lambda i: (0, 0)),
                  pl.BlockSpec((1, KV_LORA), lambda i: (0, 0)), tab, tab, tab],
        out_specs=[pl.BlockSpec((ts, cw), lambda i: (i, 0)), pl.BlockSpec((ts, Q_LORA), lambda i: (i, 0)),
                   pl.BlockSpec((ts, KV_LORA), lambda i: (i, 0)), tab],
        out_shape=[jax.ShapeDtypeStruct((s, cw), BF16), jax.ShapeDtypeStruct((s, Q_LORA), BF16),
                   jax.ShapeDtypeStruct((s, KV_LORA), BF16), jax.ShapeDtypeStruct((s, HEAD_PAD), F32)],
        compiler_params=_params(("parallel",)),
    )(z, z, z, z, z, conv_w8, gq, gkv, cos, sin_a, sin_b)


def _mix_bwd(z, dyc, dqn, dkvn, dkr, conv_w8, gq, gkv, cos, sin_a, sin_b, *, ts=1024):
    s = z.shape[0]
    n = s // ts
    cw = CONV_WIDTH

    def body(z_ref, xcp, xcn, bgp, bgn, cgp, cgn, dyc_ref, dycp, dycn, dqn_ref, dkvn_ref, dkr_ref,
             w_ref, gq_ref, gkv_ref, cos_ref, sa_ref, sb_ref,
             dz_ref, dw0_ref, dw1_ref, dw2_ref, dgq_ref, dgkv_ref):
        i = pl.program_id(0)

        @pl.when(i == 0)
        def _():
            for r in (dw0_ref, dw1_ref, dw2_ref, dgq_ref, dgkv_ref):
                r[...] = jnp.zeros_like(r)

        xc = z_ref[:, 0:cw]
        bg = z_ref[:, cw:2 * cw]
        cg = z_ref[:, 2 * cw:3 * cw]
        w0, w1, w2 = w_ref[0:1, :], w_ref[1:2, :], w_ref[2:3, :]
        m = cg * xc
        m_dn = _shift_down(m, jnp.where(i > 0, xcp[7:8, :] * cgp[7:8, :], 0.0))
        m_up = _shift_up(m, jnp.where(i < n - 1, xcn[0:1, :] * cgn[0:1, :], 0.0))
        cm = m_dn * w0 + m * w1 + m_up * w2
        dyc_v = dyc_ref[...]
        dcm = dyc_v * bg
        dcm_dn = _shift_down(dcm, jnp.where(i > 0, dycp[7:8, :] * bgp[7:8, :], 0.0))
        dcm_up = _shift_up(dcm, jnp.where(i < n - 1, dycn[0:1, :] * bgn[0:1, :], 0.0))
        dm = dcm_up * w0 + dcm * w1 + dcm_dn * w2
        dz_ref[:, 0:cw] = (dm * cg).astype(BF16)
        dz_ref[:, cw:2 * cw] = (dyc_v * cm).astype(BF16)
        dz_ref[:, 2 * cw:3 * cw] = (dm * xc).astype(BF16)
        dw0_ref[...] += jnp.sum(dcm * m_dn, axis=0, keepdims=True)
        dw1_ref[...] += jnp.sum(dcm * m, axis=0, keepdims=True)
        dw2_ref[...] += jnp.sum(dcm * m_up, axis=0, keepdims=True)

        dql, dgq_rows = _rms_bwd_rows(z_ref[:, 3 * cw:3 * cw + Q_LORA], gq_ref[...], dqn_ref[...])
        dz_ref[:, 3 * cw:3 * cw + Q_LORA] = dql.astype(BF16)
        dgq_ref[...] += jnp.sum(dgq_rows, axis=0, keepdims=True)
        dkvl, dgkv_rows = _rms_bwd_rows(z_ref[:, 3 * cw + Q_LORA:3 * cw + Q_LORA + KV_LORA], gkv_ref[...],
                                        dkvn_ref[...])
        dz_ref[:, 3 * cw + Q_LORA:3 * cw + Q_LORA + KV_LORA] = dkvl.astype(BF16)
        dgkv_ref[...] += jnp.sum(dgkv_rows, axis=0, keepdims=True)

        lane = lax.broadcasted_iota(jnp.int32, (ts, HEAD_PAD), 1)
        rope_lane = (lane >= QK_NOPE) & (lane < QK_NOPE + QK_ROPE)
        dk = _rope_bwd(dkr_ref[...], cos_ref[...], sa_ref[...], sb_ref[...])
        dz_ref[:, D_IN_PAD - HEAD_PAD:D_IN_PAD] = jnp.where(rope_lane, dk, 0.0).astype(BF16)

    xcp, xcn = _halo_specs(ts, s, cw, 0)
    bgp, bgn = _halo_specs(ts, s, cw, 1)
    cgp, cgn = _halo_specs(ts, s, cw, 2)
    dycp, dycn = _halo_specs(ts, s, cw, 0)
    tab = pl.BlockSpec((ts, HEAD_PAD), lambda i: (i, 0))

    def vec(width):
        return pl.BlockSpec((1, width), lambda i: (0, 0))

    outs = pl.pallas_call(
        body, name="mix_bwd", grid=(n,),
        in_specs=[pl.BlockSpec((ts, D_IN_PAD), lambda i: (i, 0)), xcp, xcn, bgp, bgn, cgp, cgn,
                  pl.BlockSpec((ts, cw), lambda i: (i, 0)), dycp, dycn,
                  pl.BlockSpec((ts, Q_LORA), lambda i: (i, 0)), pl.BlockSpec((ts, KV_LORA), lambda i: (i, 0)), tab,
                  pl.BlockSpec((8, cw), lambda i: (0, 0)), vec(Q_LORA), vec(KV_LORA), tab, tab, tab],
        out_specs=[pl.BlockSpec((ts, D_IN_PAD), lambda i: (i, 0)), vec(cw), vec(cw), vec(cw), vec(Q_LORA),
                   vec(KV_LORA)],
        out_shape=[jax.ShapeDtypeStruct((s, D_IN_PAD), BF16)] + [jax.ShapeDtypeStruct((1, cw), F32)] * 3
        + [jax.ShapeDtypeStruct((1, Q_LORA), F32), jax.ShapeDtypeStruct((1, KV_LORA), F32)],
        compiler_params=_params(("arbitrary",)),
    )(z, z, z, z, z, z, z, dyc, dyc, dyc, dqn, dkvn, dkr, conv_w8, gq, gkv, cos, sin_a, sin_b)
    dz, dw0, dw1, dw2, dgq, dgkv = outs
    return dz, jnp.concatenate([dw0, dw1, dw2], axis=0), dgq, dgkv


def _qkv_proj(qn, kvn, kr, w_uq_p, w_kv_p, cos, sin_a, sin_b, *, ts=1024):
    s = qn.shape[0]

    def body(qn_ref, kvn_ref, kr_ref, wq_ref, wkv_ref, cos_ref, sa_ref, sb_ref, q_ref, k_ref, v_ref, qt_ref):
        cos_v, sa, sb = cos_ref[...], sa_ref[...], sb_ref[...]
        q = jnp.dot(qn_ref[...], wq_ref[...], preferred_element_type=F32)
        kv = jnp.dot(kvn_ref[...], wkv_ref[...], preferred_element_type=F32)
        kr_v = kr_ref[...]
        lane = lax.broadcasted_iota(jnp.int32, (1, HEAD_PAD), 1)
        ones_lane = (lane == ONES_LANE).astype(F32)
        for h in range(N_HEADS):
            blk = slice(h * HEAD_PAD, (h + 1) * HEAD_PAD)
            q_h = _rope(q[:, blk], cos_v, sa, sb) * SM_SCALE
            q_ref[:, blk] = q_h.astype(BF16)
            qt_ref[blk, :] = q_h.T.astype(BF16)
            k_ref[:, blk] = (kv[:, blk] + kr_v).astype(BF16)
            v_ref[:, blk] = (kv[:, D_ATT + h * HEAD_PAD:D_ATT + (h + 1) * HEAD_PAD] + ones_lane).astype(BF16)

    tab = pl.BlockSpec((ts, HEAD_PAD), lambda i: (i, 0))
    wide = pl.BlockSpec((ts, D_ATT), lambda i: (i, 0))
    return pl.pallas_call(
        body, name="qkv_proj", grid=(s // ts,),
        in_specs=[pl.BlockSpec((ts, Q_LORA), lambda i: (i, 0)), pl.BlockSpec((ts, KV_LORA), lambda i: (i, 0)), tab,
                  pl.BlockSpec((Q_LORA, D_ATT), lambda i: (0, 0)), pl.BlockSpec((KV_LORA, 2 * D_ATT), lambda i: (0, 0)),
                  tab, tab, tab],
        out_specs=[wide, wide, wide, pl.BlockSpec((D_ATT, ts), lambda i: (0, i))],
        out_shape=[jax.ShapeDtypeStruct((s, D_ATT), BF16)] * 3 + [jax.ShapeDtypeStruct((D_ATT, s), BF16)],
        compiler_params=_params(("parallel",)),
    )(qn, kvn, kr, w_uq_p, w_kv_p, cos, sin_a, sin_b)


def _qk_bwd(dq, dk, cos, sin_a, sin_b, *, ts=1024):
    s = dq.shape[0]

    def body(dq_ref, dk_ref, cos_ref, sa_ref, sb_ref, dqp_ref, dkr_ref):
        cos_v, sa, sb = cos_ref[...], sa_ref[...], sb_ref[...]
        tot = jnp.zeros((ts, HEAD_PAD), F32)
        for h in range(N_HEADS):
            blk = slice(h * HEAD_PAD, (h + 1) * HEAD_PAD)
            dqp_ref[:, blk] = _rope_bwd(dq_ref[:, blk], cos_v, sa, sb).astype(BF16)
            tot = tot + dk_ref[:, blk]
        dkr_ref[...] = tot

    tab = pl.BlockSpec((ts, HEAD_PAD), lambda i: (i, 0))
    wide = pl.BlockSpec((ts, D_ATT), lambda i: (i, 0))
    return pl.pallas_call(
        body, name="qk_bwd", grid=(s // ts,),
        in_specs=[wide, wide, tab, tab, tab], out_specs=[wide, tab],
        out_shape=[jax.ShapeDtypeStruct((s, D_ATT), BF16), jax.ShapeDtypeStruct((s, HEAD_PAD), F32)],
        compiler_params=_params(("parallel",)),
    )(dq, dk, cos, sin_a, sin_b)


_NT = (((1,), (1,)), ((), ()))


def _flash_fwd(q, k, v, *, tq=1024, tk=1024, per_trip=8, q_per_step=2):
    s = q.shape[0]
    tq, tk = min(tq, s), min(tk, s)
    nk = s // tk
    per_trip = min(per_trip, nk)
    assert nk % per_trip == 0
    q_per_step = min(q_per_step, s // tq)

    def body(q_ref, k_ref, v_ref, o_ref, lse_ref):
        for blk in range(q_per_step):
            rows = pl.ds(blk * tq, tq)
            one_block(q_ref.at[rows, :], k_ref, v_ref, o_ref.at[rows, :], lse_ref.at[rows, :])

    def one_block(q_ref, k_ref, v_ref, o_ref, lse_ref):
        qv = q_ref[...]

        def step(j, carry):
            m, acc = carry
            rows = pl.ds(pl.multiple_of(j * tk, tk), tk)
            sc = lax.dot_general(qv, k_ref[rows, :], _NT, preferred_element_type=F32)
            m_new = jnp.maximum(m, jnp.max(sc, axis=1, keepdims=True))
            p = jnp.exp(sc - m_new).astype(BF16)
            acc = jnp.exp(m - m_new) * acc + jnp.dot(p, v_ref[rows, :], preferred_element_type=F32)
            return m_new, acc

        def trip(t, carry):
            for c in range(per_trip):
                carry = step(per_trip * t + c, carry)
            return carry

        init = (jnp.full((tq, 1), -jnp.inf, F32), jnp.zeros((tq, HEAD_PAD), F32))
        m, acc = lax.fori_loop(0, nk // per_trip, trip, init)
        l = acc[:, ONES_LANE:ONES_LANE + 1]
        o_ref[...] = (acc / l).astype(BF16)
        lse_ref[...] = m + jnp.log(l)

    head = pl.BlockSpec((s, HEAD_PAD), lambda h, i: (0, h))
    tq_step = q_per_step * tq
    return pl.pallas_call(
        body, name="flash_fwd", grid=(N_HEADS, s // tq_step),
        in_specs=[pl.BlockSpec((tq_step, HEAD_PAD), lambda h, i: (i, h)), head, head],
        out_specs=[pl.BlockSpec((tq_step, HEAD_PAD), lambda h, i: (i, h)),
                   pl.BlockSpec((None, tq_step, 1), lambda h, i: (h, i, 0))],
        out_shape=[jax.ShapeDtypeStruct((s, D_ATT), BF16), jax.ShapeDtypeStruct((N_HEADS, s, 1), F32)],
        compiler_params=_params(("parallel", "parallel")),
    )(q, k, v)


def _attn_delta(do, o, *, ts=1024):
    s = do.shape[0]

    def body(do_ref, o_ref, dl_ref):
        for h in range(N_HEADS):
            blk = slice(h * HEAD_PAD, (h + 1) * HEAD_PAD)
            dl_ref[h] = jnp.sum(do_ref[:, blk].astype(F32) * o_ref[:, blk].astype(F32), axis=1, keepdims=True)

    wide = pl.BlockSpec((ts, D_ATT), lambda i: (i, 0))
    return pl.pallas_call(
        body, name="attn_delta", grid=(s // ts,), in_specs=[wide, wide],
        out_specs=pl.BlockSpec((N_HEADS, ts, 1), lambda i: (0, i, 0)),
        out_shape=jax.ShapeDtypeStruct((N_HEADS, s, 1), F32),
        compiler_params=_params(("parallel",)),
    )(do, o)


def _flash_bwd(q, qt, k, v, do, dot, lse, delta, after, *, tq=1024, tk=512, per_trip=8, kv_per_step=2):
    s = q.shape[0]
    tq, tk = min(tq, s), min(tk, s)
    nq = s // tq
    per_trip = min(per_trip, nq)
    assert nq % per_trip == 0
    kv_per_step = min(kv_per_step, s // tk)

    def body(q_ref, qt_ref, do_ref, dot_ref, lse_ref, dl_ref, k_ref, v_ref, after_ref, dq_ref, dk_ref, dv_ref):
        j = pl.program_id(1)

        @pl.when(j == 0)
        def _():
            dq_ref[...] = jnp.zeros_like(dq_ref)

        for blk in range(kv_per_step):
            one_block(q_ref, qt_ref, do_ref, dot_ref, lse_ref, dl_ref, dq_ref,
                      k_ref.at[pl.ds(blk * tk, tk), :], v_ref.at[pl.ds(blk * tk, tk), :],
                      dk_ref.at[pl.ds(blk * tk, tk), :], dv_ref.at[pl.ds(blk * tk, tk), :])

        @pl.when(j == pl.num_programs(1) - 1)
        def _():
            dq_ref[...] *= SM_SCALE

    def one_block(q_ref, qt_ref, do_ref, dot_ref, lse_ref, dl_ref, dq_ref, k_ref, v_ref, dk_ref, dv_ref):
        kv, vv = k_ref[...], v_ref[...]

        def chunk(i, dk_t, dv_t):
            at = pl.multiple_of(i * tq, tq)
            rows = pl.ds(at, tq)
            sc = lax.dot_general(q_ref[rows, :], kv, _NT, preferred_element_type=F32)
            p = jnp.exp(sc - lse_ref[rows, :])
            dp = lax.dot_general(do_ref[rows, :], vv, _NT, preferred_element_type=F32)
            ds = (p * (dp - dl_ref[rows, :])).astype(BF16)
            dv_t = dv_t + jnp.dot(dot_ref[:, rows], p.astype(BF16), preferred_element_type=F32)
            dk_t = dk_t + jnp.dot(qt_ref[:, rows], ds, preferred_element_type=F32)
            dq_ref[rows, :] += jnp.dot(ds, kv, preferred_element_type=F32)
            return dk_t, dv_t

        def step(i, carry):
            for c in range(per_trip):
                carry = chunk(per_trip * i + c, *carry)
            return carry

        zero = jnp.zeros((HEAD_PAD, tk), F32)
        dk_t, dv_t = lax.fori_loop(0, nq // per_trip, step, (zero, zero))
        dk_ref[...] = dk_t.T
        dv_ref[...] = dv_t.T

    head = pl.BlockSpec((s, HEAD_PAD), lambda h, j: (0, h))
    head_t = pl.BlockSpec((HEAD_PAD, s), lambda h, j: (h, 0))
    stat = pl.BlockSpec((None, s, 1), lambda h, j: (h, 0, 0))
    blk = pl.BlockSpec((kv_per_step * tk, HEAD_PAD), lambda h, j: (j, h))
    return pl.pallas_call(
        body, name="flash_bwd", grid=(N_HEADS, s // (kv_per_step * tk)),
        in_specs=[head, head_t, head, head_t, stat, stat, blk, blk, ANY],
        out_specs=[head, blk, blk],
        out_shape=[jax.ShapeDtypeStruct((s, D_ATT), F32)] * 3,
        compiler_params=_params(("parallel", "arbitrary")),
    )(q, qt, do, dot, lse, delta, k, v, after)


FFN_TC = 512
FFN_TG = 1408


FFN_HALO_BF16 = 16
FFN_HALO_F32 = 8


def _row_halo_specs(ts, s, halo, width):
    nb = ts // halo
    last = s // halo - 1
    prev = pl.BlockSpec((halo, width), lambda i, j: (jnp.maximum(i * nb - 1, 0), 0))
    nxt = pl.BlockSpec((halo, width), lambda i, j: (jnp.minimum((i + 1) * nb, last), 0))
    return prev, nxt


def _ext_rows(prev, main, nxt, first, last):
    return jnp.concatenate([jnp.where(first, jnp.zeros_like(prev), prev), main,
                            jnp.where(last, jnp.zeros_like(nxt), nxt)], axis=0)


def _ext_conv(a, w):
    a_dn = pltpu.roll(a, 1, 0)
    a_up = pltpu.roll(a, a.shape[0] - 1, 0)
    return a_dn * w[0:1, :] + a * w[1:2, :] + a_up * w[2:3, :], a_dn, a_up


def _ffn_pieces(tg):
    return [(off, min(FFN_TC, tg - off)) for off in range(0, tg, FFN_TC)]


def _ffn_fwd(hf, w_up, w, b, *, ts=1024, tg=FFN_TG):
    s = hf.shape[0]
    n, ng, halo = s // ts, D_FF // tg, FFN_HALO_BF16

    def body(h_ref, hp_ref, hn_ref, wg_ref, wu_ref, cw_ref, cb_ref, a_ref, act_ref):
        i = pl.program_id(0)
        ext = _ext_rows(hp_ref[...], h_ref[...], hn_ref[...], i == 0, i == n - 1)
        for off, width in _ffn_pieces(tg):
            cols = slice(off, off + width)
            gate_up = []
            for half, w_ref in enumerate((wg_ref, wu_ref)):
                a_ext = jnp.dot(ext, w_ref[:, cols], preferred_element_type=F32)
                a_ref[half, :, cols] = a_ext[halo:halo + ts]
                conv = _ext_conv(a_ext, cw_ref[half, :, cols])[0]
                gate_up.append(conv[halo:halo + ts] + cb_ref[half, :, cols])
            g, u = gate_up
            act_ref[:, cols] = (g * _sigmoid(g) * u).astype(BF16)

    prev, nxt = _row_halo_specs(ts, s, halo, D_MODEL)
    return pl.pallas_call(
        body, name="ffn_fwd", grid=(n, ng),
        in_specs=[pl.BlockSpec((ts, D_MODEL), lambda i, j: (i, 0)), prev, nxt,
                  pl.BlockSpec((D_MODEL, tg), lambda i, j: (0, j)), pl.BlockSpec((D_MODEL, tg), lambda i, j: (0, j + ng)),
                  pl.BlockSpec((2, 8, tg), lambda i, j: (0, 0, j)), pl.BlockSpec((2, 1, tg), lambda i, j: (0, 0, j))],
        out_specs=[pl.BlockSpec((2, ts, tg), lambda i, j: (0, i, j)), pl.BlockSpec((ts, tg), lambda i, j: (i, j))],
        out_shape=[jax.ShapeDtypeStruct((2, s, D_FF), F32), jax.ShapeDtypeStruct((s, D_FF), BF16)],
        compiler_params=_params(("parallel", "parallel")),
    )(hf, hf, hf, w_up, w_up, w, b)


def _ffn_bwd(dx2, w_down, a_pre, w, b, *, ts=1024, tg=FFN_TG):
    s = dx2.shape[0]
    n, ng, halo = s // ts, D_FF // tg, FFN_HALO_F32
    main = slice(halo, halo + ts)

    def body(dx_ref, dxp_ref, dxn_ref, wd_ref, a_ref, ap_ref, an_ref, cw_ref, cb_ref, o_ref, dw_ref, db_ref):
        i, j = pl.program_id(0), pl.program_id(1)
        first, last = i == 0, i == n - 1

        @pl.when(first & (j == 0))
        def _():
            dw_ref[...] = jnp.zeros_like(dw_ref)
            db_ref[...] = jnp.zeros_like(db_ref)

        dx_ext = _ext_rows(dxp_ref[...], dx_ref[...], dxn_ref[...], first, last).astype(BF16)
        for off, width in _ffn_pieces(tg):
            cols = slice(off, off + width)
            dact = lax.dot_general(dx_ext, wd_ref[cols, :], _NT, preferred_element_type=F32)
            halves = []
            for half in range(2):
                a_ext = _ext_rows(ap_ref[half, :, cols], a_ref[half, :, cols], an_ref[half, :, cols], first, last)
                conv, a_dn, a_up = _ext_conv(a_ext, cw_ref[half, :, cols])
                halves.append((conv + cb_ref[half, :, cols], a_dn, a_ext, a_up))
            g, u = halves[0][0], halves[1][0]
            sg = _sigmoid(g)
            grads = (dact * u * (sg * (1.0 + g * (1.0 - sg))), dact * (g * sg))
            for half in range(2):
                d = grads[half]
                _, a_dn, a_ext, a_up = halves[half]
                wv = cw_ref[half, :, cols]
                d_pre = (pltpu.roll(d, d.shape[0] - 1, 0) * wv[0:1, :] + d * wv[1:2, :]
                         + pltpu.roll(d, 1, 0) * wv[2:3, :])
                o_ref[half, :, cols] = d_pre[main].astype(BF16)
                dm = d[main]
                dw_ref[j, half, 0:1, cols] += jnp.sum(dm * a_dn[main], axis=0, keepdims=True)
                dw_ref[j, half, 1:2, cols] += jnp.sum(dm * a_ext[main], axis=0, keepdims=True)
                dw_ref[j, half, 2:3, cols] += jnp.sum(dm * a_up[main], axis=0, keepdims=True)
                db_ref[j, half, :, cols] += jnp.sum(dm, axis=0, keepdims=True)

    dxp, dxn = _row_halo_specs(ts, s, halo, D_MODEL)
    nb, lastb = ts // halo, s // halo - 1
    a_main = pl.BlockSpec((2, ts, tg), lambda i, j: (0, i, j))
    a_prev = pl.BlockSpec((2, halo, tg), lambda i, j: (0, jnp.maximum(i * nb - 1, 0), j))
    a_next = pl.BlockSpec((2, halo, tg), lambda i, j: (0, jnp.minimum((i + 1) * nb, lastb), j))
    da_pre, dw, db = pl.pallas_call(
        body, name="ffn_bwd", grid=(n, ng),
        in_specs=[pl.BlockSpec((ts, D_MODEL), lambda i, j: (i, 0)), dxp, dxn,
                  pl.BlockSpec((tg, D_MODEL), lambda i, j: (j, 0)), a_main, a_prev, a_next,
                  pl.BlockSpec((2, 8, tg), lambda i, j: (0, 0, j)), pl.BlockSpec((2, 1, tg), lambda i, j: (0, 0, j))],
        out_specs=[a_main, pl.BlockSpec((ng, 2, 8, tg), lambda i, j: (0, 0, 0, 0)),
                   pl.BlockSpec((ng, 2, 1, tg), lambda i, j: (0, 0, 0, 0))],
        out_shape=[jax.ShapeDtypeStruct((2, s, D_FF), BF16), jax.ShapeDtypeStruct((ng, 2, 8, tg), F32),
                   jax.ShapeDtypeStruct((ng, 2, 1, tg), F32)],
        compiler_params=_params(("arbitrary", "arbitrary")),
    )(dx2, dx2, dx2, w_down, a_pre, a_pre, a_pre, w, b)
    return (da_pre, dw.transpose(1, 2, 0, 3).reshape(2, 8, D_FF), db.transpose(1, 2, 0, 3).reshape(2, 1, D_FF))


def _ple_final(x2, n3, p, target, gf, w_pg, w_pp, *, ts=512):
    s, d = x2.shape
    dp = p.shape[1]

    def body(x2_ref, n3_ref, p_ref, t_ref, gf_ref, wg_ref, wp_ref, loss_ref, dx3_ref, dgl_ref, dpp_ref, dgf_ref):
        @pl.when(pl.program_id(0) == 0)
        def _():
            loss_ref[...] = jnp.zeros_like(loss_ref)
            dgf_ref[...] = jnp.zeros_like(dgf_ref)

        gate = _sigmoid(jnp.dot(n3_ref[...], wg_ref[...], preferred_element_type=F32))
        ppv = jnp.dot(p_ref[...].astype(BF16), wp_ref[...], preferred_element_type=F32)
        x3 = x2_ref[...] + gate * ppv
        gfv = gf_ref[...]
        err = x3 * _rms_scale(x3) * gfv - t_ref[...]
        loss_ref[...] += 0.5 * jnp.sum(jnp.mean(err * err, axis=-1, keepdims=True), axis=0, keepdims=True)
        dx3, dgf_rows = _rms_bwd_rows(x3, gfv, err * (1.0 / d))
        dgf_ref[...] += jnp.sum(dgf_rows, axis=0, keepdims=True)
        dx3_ref[...] = dx3
        dgl_ref[...] = (dx3 * ppv * gate * (1.0 - gate)).astype(BF16)
        dpp_ref[...] = (dx3 * gate).astype(BF16)

    row = pl.BlockSpec((ts, d), lambda i: (i, 0))
    vec = pl.BlockSpec((1, d), lambda i: (0, 0))
    return pl.pallas_call(
        body, name="ple_final", grid=(s // ts,),
        in_specs=[row, row, pl.BlockSpec((ts, dp), lambda i: (i, 0)), row, vec,
                  pl.BlockSpec((d, d), lambda i: (0, 0)), pl.BlockSpec((dp, d), lambda i: (0, 0))],
        out_specs=[pl.BlockSpec((1, 128), lambda i: (0, 0)), row, row, row, vec],
        out_shape=[jax.ShapeDtypeStruct((1, 128), F32), jax.ShapeDtypeStruct((s, d), F32),
                   jax.ShapeDtypeStruct((s, d), BF16), jax.ShapeDtypeStruct((s, d), BF16),
                   jax.ShapeDtypeStruct((1, d), F32)],
        compiler_params=_params(("arbitrary",)),
    )(x2, n3, p, target, gf, w_pg, w_pp)


def _row_tile(rows, cols, n_arrays, budget=12 << 20):
    best = None
    for t in range(8, rows + 1, 8):
        if rows % t == 0 and t * cols * 4 * n_arrays <= budget:
            best = t
    return rows if best is None else best


def _sum_slots(a, *, name):
    g, r, c = a.shape
    tr = _row_tile(r, c, g + 1)

    def body(*refs):
        tot = refs[0][...]
        for ref in refs[1:g]:
            tot = tot + ref[...]
        refs[g][...] = tot

    specs = [pl.BlockSpec((None, tr, c), functools.partial(lambda i, slot: (slot, i, 0), slot=k)) for k in range(g)]
    return pl.pallas_call(
        body, name=name, grid=(r // tr,), in_specs=specs, out_specs=pl.BlockSpec((tr, c), lambda i: (i, 0)),
        out_shape=jax.ShapeDtypeStruct((r, c), a.dtype), compiler_params=_params(("parallel",)),
    )(*([a] * g))


def _adamw_refs(w_ref, g_ref, m_ref, v_ref, d_ref, mo_ref, vo_ref):
    gv = g_ref[...]
    mn = ADAM_B1 * m_ref[...] + (1.0 - ADAM_B1) * gv
    vn = ADAM_B2 * v_ref[...] + (1.0 - ADAM_B2) * (gv * gv)
    m_hat = mn / (1.0 - ADAM_B1 ** ADAM_STEP)
    v_hat = vn / (1.0 - ADAM_B2 ** ADAM_STEP)
    d_ref[...] = -ADAM_LR * (m_hat / (jnp.sqrt(v_hat) + ADAM_EPS) + ADAM_WD * w_ref[...])
    mo_ref[...] = mn
    vo_ref[...] = vn


def _adamw_many(ws, gs, ms, vs, *, name):
    n = len(ws)

    def body(*refs):
        ins, outs = refs[:4 * n], refs[4 * n:]
        for a in range(n):
            _adamw_refs(ins[a], ins[n + a], ins[2 * n + a], ins[3 * n + a], outs[3 * a], outs[3 * a + 1],
                        outs[3 * a + 2])

    vm = pl.BlockSpec(memory_space=pltpu.VMEM)
    res = pl.pallas_call(
        body, name=name, in_specs=[vm] * (4 * n), out_specs=[vm] * (3 * n),
        out_shape=[jax.ShapeDtypeStruct(a.shape, F32) for a in ws for _ in range(3)],
    )(*ws, *gs, *ms, *vs)
    return [tuple(res[3 * a:3 * a + 3]) for a in range(n)]


def _adamw(w, g, m, v, *, name):
    r, c = w.shape
    tr = _row_tile(r, c, 7)
    body = _adamw_refs

    blk = pl.BlockSpec((tr, c), lambda i: (i, 0))
    return pl.pallas_call(
        body, name=name, grid=(r // tr,), in_specs=[blk] * 4, out_specs=[blk] * 3,
        out_shape=[jax.ShapeDtypeStruct((r, c), F32)] * 3, compiler_params=_params(("parallel",)),
    )(w, g, m, v)


def _position():
    x, y, c = lax.axis_index("x"), lax.axis_index("y"), lax.axis_index("c")
    return x, y, c


def _other_chips(x, y):
    return [(1 - x, y), (x, 1 - y), (1 - x, 1 - y)]


def _stage_in(srcs, stage, sems):
    cps = [pltpu.make_async_copy(src, stage[a], sems.at[a]) for a, src in enumerate(srcs)]
    for cp in cps:
        cp.start()
    return cps


def _stage_out(staged, stage, dsts, sems):
    cps = []
    for a, dst in enumerate(dsts):
        staged[a].wait()
        cp = pltpu.make_async_copy(stage[a], dst, sems.at[a])
        cp.start()
        cps.append(cp)
    return cps


def _send_other_halves(grads, *, tag):
    n = len(grads)

    def body(*refs):
        ins, sib = refs[:n], refs[n:2 * n]
        send_sems, recv_sems = refs[2 * n:]
        x, y, c = _position()
        remote = []
        for a in range(n):
            half = ins[a].shape[1] // 2
            give = ins[a].at[:, pl.ds(pl.multiple_of((1 - c) * half, 8), half), :]
            rc = pltpu.make_async_remote_copy(
                src_ref=give, dst_ref=sib[a], send_sem=send_sems.at[a], recv_sem=recv_sems.at[a],
                device_id=(x, y, 1 - c), device_id_type=MESH)
            rc.start()
            remote.append(rc)
        for rc in remote:
            rc.wait_recv()
        for rc in remote:
            rc.wait_send()

    return pl.pallas_call(
        body, name="send_other_halves_" + tag, in_specs=[ANY] * n, out_specs=[ANY] * n,
        out_shape=[jax.ShapeDtypeStruct((g.shape[0], g.shape[1] // 2, g.shape[2]), g.dtype) for g in grads],
        scratch_shapes=[pltpu.SemaphoreType.DMA((n,)), pltpu.SemaphoreType.DMA((n,))],
        compiler_params=pltpu.CompilerParams(has_side_effects=True),
    )(*grads)


def _add_own_half(g4, sib, core, *, name):
    g, a2, c = sib.shape
    tr = _row_tile(a2, c, 4)

    def body(core_ref, a_ref, b_ref, o_ref, o16_ref):
        tot = a_ref[...] + b_ref[...]
        o_ref[...] = tot
        o16_ref[...] = tot.astype(BF16)

    blk = pl.BlockSpec((None, tr, c), lambda i, j, core_ref: (i, j, 0))
    return pl.pallas_call(
        body, name=name,
        grid_spec=pltpu.PrefetchScalarGridSpec(
            num_scalar_prefetch=1, grid=(g, a2 // tr),
            in_specs=[pl.BlockSpec((None, None, tr, c), lambda i, j, core_ref: (i, core_ref[0], j, 0)), blk],
            out_specs=[blk, blk]),
        out_shape=[jax.ShapeDtypeStruct(sib.shape, F32), jax.ShapeDtypeStruct(sib.shape, BF16)],
        compiler_params=_params(("parallel", "parallel")),
    )(core, g4.reshape(g, 2, a2, c), sib)


def _sum_chips(landed, own, chip, *, name):
    g, r, c = landed.shape
    tr = _row_tile(r, c, 5)

    def body(chip_ref, *refs):
        me = chip_ref[0]
        own_v = refs[g][...]
        tot = None
        for slot in range(g):
            term = jnp.where(me == slot, own_v, refs[slot][...].astype(F32))
            tot = term if tot is None else tot + term
        refs[g + 1][...] = tot

    def landed_spec(slot):
        return pl.BlockSpec((None, tr, c),
                            lambda i, chip_ref: (jnp.where(chip_ref[0] == slot, (slot + 1) % g, slot), i, 0))

    return pl.pallas_call(
        body, name=name,
        grid_spec=pltpu.PrefetchScalarGridSpec(
            num_scalar_prefetch=1, grid=(r // tr,),
            in_specs=[landed_spec(k) for k in range(g)]
            + [pl.BlockSpec((None, tr, c), lambda i, chip_ref: (chip_ref[0], i, 0))],
            out_specs=pl.BlockSpec((tr, c), lambda i, chip_ref: (i, 0))),
        out_shape=jax.ShapeDtypeStruct((r, c), F32), compiler_params=_params(("parallel",)),
    )(chip, *([landed] * g), own)


def _join_halves(halves):
    n = len(halves)

    def body(*refs):
        ins, outs, stage = refs[:n], refs[n:2 * n], refs[2 * n:3 * n]
        send_sems, recv_sems, in_sems, out_sems = refs[3 * n:]
        x, y, c = _position()
        remote = []
        staged = _stage_in(ins, stage, in_sems)
        for a in range(n):
            rc = pltpu.make_async_remote_copy(
                src_ref=ins[a], dst_ref=outs[a].at[c], send_sem=send_sems.at[a], recv_sem=recv_sems.at[a],
                device_id=(x, y, 1 - c), device_id_type=MESH)
            rc.start()
            remote.append(rc)
        local = _stage_out(staged, stage, [o.at[c] for o in outs], out_sems)
        for a in range(n):
            pltpu.make_async_remote_copy(
                src_ref=ins[a], dst_ref=outs[a].at[1 - c], send_sem=send_sems.at[a], recv_sem=recv_sems.at[a],
                device_id=(x, y, 1 - c), device_id_type=MESH).wait_recv()
        for rc in remote:
            rc.wait_send()
        for cp in local:
            cp.wait()

    return pl.pallas_call(
        body, name="join_halves", in_specs=[ANY] * n, out_specs=[ANY] * n,
        out_shape=[jax.ShapeDtypeStruct((2,) + h.shape, h.dtype) for h in halves],
        scratch_shapes=[pltpu.VMEM(h.shape, h.dtype) for h in halves]
        + [pltpu.SemaphoreType.DMA((n,)), pltpu.SemaphoreType.DMA((n,)), pltpu.SemaphoreType.DMA((n,)),
           pltpu.SemaphoreType.DMA((n,))],
        compiler_params=pltpu.CompilerParams(has_side_effects=True),
    )(*halves)


_HBM = pl.BlockSpec(memory_space=pltpu.HBM)
_SEM = pl.BlockSpec(memory_space=pltpu.SEMAPHORE)


def _chip_copies(srcs, lands, send_sems, recv_sems, scatter):
    x, y, c = _position()
    me = 2 * x + y
    outgoing, incoming = [], []
    for a, (src, land) in enumerate(zip(srcs, lands)):
        for k, (px, py) in enumerate(_other_chips(x, y)):
            peer = 2 * px + py
            sems = dict(send_sem=send_sems.at[3 * a + k], recv_sem=recv_sems.at[3 * a + k], device_id=(px, py, c),
                        device_id_type=MESH)
            outgoing.append(pltpu.make_async_remote_copy(
                src_ref=src.at[peer] if scatter else src, dst_ref=land.at[me], **sems))
            incoming.append(pltpu.make_async_remote_copy(
                src_ref=src.at[me] if scatter else src, dst_ref=land.at[peer], **sems))
    return outgoing, incoming


def _chips_start(srcs, *, scatter, name):
    n = len(srcs)
    lands = [lax.empty(a.shape if scatter else (N_CHIPS,) + a.shape, a.dtype) for a in srcs]

    def body(*refs):
        ins, send_sems, recv_sems, token = refs[:2 * n], refs[2 * n], refs[2 * n + 1], refs[-1]
        outgoing, _ = _chip_copies(ins[:n], ins[n:], send_sems, recv_sems, scatter)
        for cp in outgoing:
            cp.start()
        token[...] = jnp.zeros_like(token)

    bufs = list(srcs) + lands
    res = pl.pallas_call(
        body, name=name, in_specs=[_HBM] * (2 * n),
        out_specs=(_SEM, _SEM, *[_HBM] * (2 * n), pl.BlockSpec(memory_space=pltpu.VMEM)),
        out_shape=(pltpu.SemaphoreType.DMA((3 * n,)), pltpu.SemaphoreType.DMA((3 * n,)),
                   *[pltpu.HBM(a.shape, a.dtype) for a in bufs], jax.ShapeDtypeStruct((8, 128), F32)),
        input_output_aliases={i: 2 + i for i in range(2 * n)},
        compiler_params=pltpu.CompilerParams(has_side_effects=pltpu.SideEffectType.DATAFLOW_SIDE_EFFECTING),
    )(*[pltpu.with_memory_space_constraint(a, pltpu.HBM) for a in bufs])
    return res[0], res[1], list(res[2:2 + n]), list(res[2 + n:2 + 2 * n]), res[-1]


def _chips_wait(handle, after, *, scatter, name):
    send_sems, recv_sems, srcs, lands, _ = handle
    n = len(srcs)

    def body(*refs):
        ins, send_ref, recv_ref = refs[:2 * n], refs[2 * n], refs[2 * n + 1]
        outgoing, incoming = _chip_copies(ins[:n], ins[n:], send_ref, recv_ref, scatter)
        for cp in outgoing:
            cp.wait_send()
        for cp in incoming:
            cp.wait_recv()

    bufs = list(srcs) + list(lands)
    res = pl.pallas_call(
        body, name=name, in_specs=[_HBM] * (2 * n) + [_SEM, _SEM, ANY], out_specs=tuple([_HBM] * (2 * n)),
        out_shape=tuple(pltpu.HBM(a.shape, a.dtype) for a in bufs),
        input_output_aliases={i: i for i in range(2 * n)},
        compiler_params=pltpu.CompilerParams(has_side_effects=pltpu.SideEffectType.DATAFLOW_SIDE_EFFECTING),
    )(*bufs, send_sems, recv_sems, after)
    return list(res[:n]), list(res[n:])


def _gather_all(buf):
    def body(in_ref, out_ref, send_sems, recv_sems, local_sem):
        x, y, c = _position()
        me = 4 * x + 2 * y + c
        peers = [(x, y, 1 - c)] + [(px, py, pc) for (px, py) in _other_chips(x, y) for pc in (c, 1 - c)]
        cp = pltpu.make_async_copy(in_ref, out_ref.at[me], local_sem)
        cp.start()
        remote = []
        for k, peer in enumerate(peers):
            rc = pltpu.make_async_remote_copy(
                src_ref=in_ref, dst_ref=out_ref.at[me], send_sem=send_sems.at[k], recv_sem=recv_sems.at[k],
                device_id=peer, device_id_type=MESH)
            rc.start()
            remote.append(rc)
        for k, (px, py, pc) in enumerate(peers):
            pltpu.make_async_remote_copy(
                src_ref=in_ref, dst_ref=out_ref.at[4 * px + 2 * py + pc], send_sem=send_sems.at[k],
                recv_sem=recv_sems.at[k], device_id=(px, py, pc), device_id_type=MESH).wait_recv()
        for rc in remote:
            rc.wait_send()
        cp.wait()

    return pl.pallas_call(
        body, name="gather_all", in_specs=[ANY], out_specs=ANY,
        out_shape=jax.ShapeDtypeStruct((N_DEV,) + buf.shape, buf.dtype),
        scratch_shapes=[pltpu.SemaphoreType.DMA((N_DEV - 1,)), pltpu.SemaphoreType.DMA((N_DEV - 1,)),
                        pltpu.SemaphoreType.DMA],
        compiler_params=pltpu.CompilerParams(has_side_effects=True),
    )(buf)


def _cols_from_shards(g4):
    _, k, n = g4.shape
    return g4.transpose(1, 0, 2).reshape(k, N_CHIPS * n)


def _cols_to_shards(w):
    k, n = w.shape
    return w.reshape(k, N_CHIPS, n // N_CHIPS).transpose(1, 0, 2)


def _pad_heads(w, width):
    k = w.shape[0]
    w3 = w.reshape(k, N_HEADS, width)
    return jnp.pad(w3, ((0, 0), (0, 0), (0, HEAD_PAD - width))).reshape(k, D_ATT)


def _unpad_heads(w, width):
    k = w.shape[0]
    return w.reshape(k, N_HEADS, HEAD_PAD)[:, :, :width]


def _rope_tables(s, after):
    pos = jnp.arange(s, dtype=F32) + after
    inv_freq = ROPE_THETA ** (-jnp.arange(0, QK_ROPE, 2, dtype=F32) / QK_ROPE)
    ang = pos[:, None] * inv_freq[None, :]
    cos_h, sin_h = jnp.cos(ang), jnp.sin(ang)
    half = QK_ROPE // 2
    z = jnp.zeros((s, half), F32)
    ones = jnp.ones((s, QK_NOPE), F32)
    tail = jnp.zeros((s, HEAD_PAD - QK_NOPE - QK_ROPE), F32)
    cos = jnp.concatenate([ones, cos_h, cos_h, tail + 1.0], axis=1)
    sin_a = jnp.concatenate([ones * 0.0, -sin_h, z, tail], axis=1)
    sin_b = jnp.concatenate([ones * 0.0, z, sin_h, tail], axis=1)
    return cos, sin_a, sin_b


def _local_step(x, p, target, wts, late_weights, reduce_early, reduce_last):
    s = x.shape[0]
    cos, sin_a, sin_b = wts["rope"]
    g1, gq, gkv, g2, g3, gf = (wts[k] for k in ("norm_mix_g", "q_norm_g", "kv_norm_g", "norm_ffn_g", "ple_norm_g",
                                                 "final_norm_g"))
    w_in_p, w_uq_p, w_kv_p = wts["w_in_p"], wts["w_uq_p"], wts["w_kv_p"]
    conv_w8, fconv_w, fconv_b = wts["conv_w8"], wts["ffn_conv_w"], wts["ffn_conv_b"]

    (h, z), _ = _mm_fused(x, w_in_p, name="mm_in", prologue=_pro_rms, vecs=[g1], epilogue=_epi_plain, row_outs=[F32],
                          tm=1024)
    y_conv, qn, kvn, kr = _mix_pre(z, conv_w8, gq, gkv, cos, sin_a, sin_b)
    q, k, v, q_t = _qkv_proj(qn, kvn, kr, w_uq_p, w_kv_p, cos, sin_a, sin_b)
    o, lse = _flash_fwd(q, k, v)
    late = late_weights(lse)
    w_o_a, w_o_b, w_up, w_down = late["w_o_a"], late["w_o_b"], late["w_up"], late["w_down"]
    w_pg, w_pp = late["w_ple_gate"], late["w_ple_proj"]
    (x1, hf), _ = _mm_fused(o, w_o_b, second=(y_conv, w_o_a), name="mm_o", rows=[x], vecs=[g2],
                            epilogue=_epi_add_rms, row_outs=[F32, BF16], tm=1024)
    a_pre, act = _ffn_fwd(hf, w_up, fconv_w, fconv_b)
    (x2, n3), _ = _mm_fused(act, w_down, name="mm_down", rows=[x1], vecs=[g3], epilogue=_epi_add_rms,
                            row_outs=[F32, BF16], tm=1024)
    loss, dx3, dgl, dpp, d_gf = _ple_final(x2, n3, p, target, gf, w_pg, w_pp)

    grads, early = {"final_norm_g": d_gf}, {}
    early["w_ple_proj"] = _mm(p, dpp, ta=True, name="mm_d_wpp", tm=256, tn=1024, tk=2048)
    early["w_ple_gate"] = _mm(n3, dgl, ta=True, name="mm_d_wpg", tm=1024, tn=1024, tk=2048)
    (dx2,), (grads["ple_norm_g"],) = _mm_fused(dgl, w_pg, tb=True, name="mm_d_n3", rows=[x2, dx3], vecs=[g3],
                                               epilogue=_epi_rms_bwd, row_outs=[F32], n_vec_out=1, tm=1024)
    early["w_down"] = _mm(act, dx2, ta=True, name="mm_d_wdown", tm=1408, tn=1024, tk=2048)
    da_pre, grads["ffn_conv_w"], grads["ffn_conv_b"] = _ffn_bwd(dx2, w_down, a_pre, fconv_w, fconv_b)
    early["w_up"] = _mm(hf, da_pre, ta=True, b_split=True, name="mm_d_wup", tm=1024, tn=1408, tk=2048,
                       o_shards=True)
    (dx1,), (grads["norm_ffn_g"],) = _mm_fused(da_pre, w_up, tb=True, a_split=True, name="mm_d_hf", rows=[x1, dx2],
                                               vecs=[g2], epilogue=_epi_rms_bwd, row_outs=[F32], n_vec_out=1)
    d_wo_a = _mm(y_conv, dx1, ta=True, name="mm_d_wo_conv", tm=512, tn=1024, tk=2048)
    d_wo_b = _mm(o, dx1, ta=True, name="mm_d_wo_att", tm=1024, tn=1024, tk=2048)
    early["w_o"] = jnp.concatenate([d_wo_a, d_wo_b.reshape(N_HEADS, HEAD_PAD, D_MODEL)[:, :V_HEAD]
                                    .reshape(N_HEADS * V_HEAD, D_MODEL)], axis=0)
    token, finish = reduce_early(early)
    dyc = _mm(dx1, w_o_a, tb=True, name="mm_d_yconv", tm=512, tn=512, tk=1024)
    (do, do_t), _ = _mm_fused(dx1, w_o_b, tb=True, name="mm_d_o", epilogue=_epi_plain, row_outs=[BF16],
                              transposed_out=BF16, tm=1024)
    delta = _attn_delta(do, o)
    dq, dk, dv = _flash_bwd(q, q_t, k, v, do, do_t, lse, delta, token)
    reduced_early = finish(dq)
    dq_pre, dkr = _qk_bwd(dq, dk, cos, sin_a, sin_b)
    grads["w_uq_p"] = _mm(qn, dq_pre, ta=True, name="mm_d_wuq", tm=256, tn=1024, tk=2048)
    dqn = _mm(dq_pre, w_uq_p, tb=True, name="mm_d_qn", tm=512, tn=256, tk=1024)
    grads["w_k_p"] = _mm(kvn, dk, ta=True, name="mm_d_wk", tm=128, tn=1024, tk=2048)
    grads["w_v_p"] = _mm(kvn, dv, ta=True, name="mm_d_wv", tm=128, tn=1024, tk=2048)
    (dkvn,), _ = _mm_fused(dk, w_kv_p[:, :D_ATT], tb=True, second=(dv, w_kv_p[:, D_ATT:].T), name="mm_d_kvn",
                           epilogue=_epi_plain, row_outs=[F32])
    dz, grads["conv_w"], grads["q_norm_g"], grads["kv_norm_g"] = _mix_bwd(
        z, dyc, dqn, dkvn, dkr, conv_w8, gq, gkv, cos, sin_a, sin_b)
    grads["w_in_p"] = _mm(h, dz, ta=True, name="mm_d_win", tm=1024, tn=1024, tk=2048)
    token, finish = reduce_last({n: grads.pop(n) for n in ("w_in_p", "w_uq_p", "w_k_p", "w_v_p")})
    (grad_x,), (grads["norm_mix_g"],) = _mm_fused(dz, w_in_p, tb=True, name="mm_d_h", rows=[x, dx1],
                                                  vecs=[g1 + token[0, 0]], epilogue=_epi_rms_bwd, row_outs=[F32],
                                                  n_vec_out=1, tm=1024)
    return loss[0, 0], grad_x, grads, reduced_early, finish(grad_x)


_EARLY_W = ("w_in", "w_uq", "w_ukv")
_LATE_W = ("w_o", "w_up", "w_down", "w_ple_gate", "w_ple_proj")
_BIG = _EARLY_W + _LATE_W
_COL_SHARDED = ("w_in", "w_uq", "w_ukv", "w_up", "w_ple_proj")
_SMALL = ("norm_mix_g", "conv_w", "q_norm_g", "kv_norm_g", "norm_ffn_g", "ffn_conv_w", "ffn_conv_b", "ple_norm_g",
          "final_norm_g")


def _full_from_slots(n, g4):
    return _cols_from_shards(g4) if n in _COL_SHARDED else g4.reshape(-1, g4.shape[2])


def _shard_major(n, g):
    if g.ndim == 3:
        return g
    return _cols_to_shards(g) if n in _COL_SHARDED else g.reshape(N_CHIPS, g.shape[0] // N_CHIPS, g.shape[1])


def _early_shards(w):
    shards = [w[n][0].astype(BF16) for n in _EARLY_W]
    shards.append(jnp.pad(w["conv_w"][0], ((0, 5), (0, 0))))
    shards.append(jnp.pad(w["ffn_conv_w"][0], ((0, 5), (0, 0))))
    return shards


def _fill_own_slot(landed, own, chip):
    return [lax.dynamic_update_slice(g4, a[None], (chip[0], 0, 0)) for g4, a in zip(landed, own)]


def _early_weights(got, w):
    full = {n: _full_from_slots(n, g4) for n, g4 in zip(_EARLY_W, got)}
    full["conv_w8"] = _cols_from_shards(got[len(_EARLY_W)])
    full["ffn_conv_w8"] = _cols_from_shards(got[len(_EARLY_W) + 1])
    return _layout_early(full, w)


def _layout_early(full, w):
    out = {n: w[n] for n in ("norm_mix_g", "q_norm_g", "kv_norm_g", "norm_ffn_g", "ple_norm_g")}
    out["final_norm_g"] = w["final_norm_g"][None, :]
    w_in = full["w_in"]
    zc = jnp.zeros((D_MODEL, QK_NOPE), BF16)
    zt = jnp.zeros((D_MODEL, HEAD_PAD - QK_NOPE - QK_ROPE), BF16)
    out["w_in_p"] = jnp.concatenate([w_in[:, :D_IN - QK_ROPE], zc, w_in[:, D_IN - QK_ROPE:], zt], axis=1)
    out["w_uq_p"] = _pad_heads(full["w_uq"], QK_NOPE + QK_ROPE)
    kv3 = full["w_ukv"].reshape(KV_LORA, N_HEADS, QK_NOPE + V_HEAD)
    out["w_kv_p"] = jnp.concatenate([_pad_heads(kv3[:, :, :QK_NOPE].reshape(KV_LORA, -1), QK_NOPE),
                                     _pad_heads(kv3[:, :, QK_NOPE:].reshape(KV_LORA, -1), V_HEAD)], axis=1)
    out["conv_w8"] = full["conv_w8"]
    fw = full["ffn_conv_w8"]
    out["ffn_conv_w"] = jnp.stack([fw[:, :D_FF], fw[:, D_FF:]])
    out["ffn_conv_b"] = w["ffn_conv_b"].reshape(2, 1, D_FF)
    return out


def _layout_late(full):
    w_o = full["w_o"]
    out = {"w_o_a": w_o[:CONV_WIDTH]}
    out["w_o_b"] = jnp.pad(w_o[CONV_WIDTH:].reshape(N_HEADS, V_HEAD, D_MODEL),
                           ((0, 0), (0, HEAD_PAD - V_HEAD), (0, 0))).reshape(D_ATT, D_MODEL)
    for n in ("w_up", "w_down", "w_ple_gate", "w_ple_proj"):
        out[n] = full[n]
    return out


def _true_matrices(g):
    out = {}
    wp = g["w_in_p"]
    out["w_in"] = jnp.concatenate([wp[:, :D_IN - QK_ROPE], wp[:, D_IN_PAD - HEAD_PAD + QK_NOPE:
                                                              D_IN_PAD - HEAD_PAD + QK_NOPE + QK_ROPE]], axis=1)
    out["w_uq"] = _unpad_heads(g["w_uq_p"], QK_NOPE + QK_ROPE).reshape(Q_LORA, -1)
    out["w_ukv"] = jnp.concatenate([_unpad_heads(g["w_k_p"], QK_NOPE), _unpad_heads(g["w_v_p"], V_HEAD)],
                                   axis=2).reshape(KV_LORA, -1)
    return out


def _true_vectors(g):
    out = {}
    out["conv_w"] = g["conv_w"]
    fw = g["ffn_conv_w"]
    out["ffn_conv_w"] = jnp.concatenate([fw[0, :3], fw[1, :3]], axis=1)
    out["ffn_conv_b"] = g["ffn_conv_b"].reshape(1, 2 * D_FF)
    for n in ("norm_mix_g", "q_norm_g", "kv_norm_g", "norm_ffn_g", "ple_norm_g", "final_norm_g"):
        out[n] = g[n]
    return out


def _chip_partials(names, g, core, *, tag):
    g4 = [_shard_major(n, g[n]) for n in names]
    sib = _send_other_halves(g4, tag=tag)
    return [_add_own_half(a, b, core, name="add_cores_" + n) for n, a, b in zip(names, g4, sib)]


_SMALL_SIZES = {"norm_mix_g": D_MODEL, "conv_w": 3 * CONV_WIDTH, "q_norm_g": Q_LORA, "kv_norm_g": KV_LORA,
                "norm_ffn_g": D_MODEL, "ffn_conv_w": 6 * D_FF, "ffn_conv_b": 2 * D_FF, "ple_norm_g": D_MODEL,
                "final_norm_g": D_MODEL}


def _pack(parts, rows):
    flat = jnp.concatenate([a.reshape(-1) for a in parts])
    return jnp.pad(flat, (0, rows * 128 - flat.shape[0])).reshape(rows, 128)


def _unpack(buf, sizes):
    flat = buf.reshape(-1)
    out, at = [], 0
    for n in sizes:
        out.append(flat[at:at + n])
        at += n
    return out


def _reduce_small(g, loss):
    sizes = [1] + [_SMALL_SIZES[n] for n in _SMALL]
    rows = -(-sum(sizes) // 1024) * 8
    slots = _gather_all(_pack([loss] + [g[n] for n in _SMALL], rows))
    parts = _unpack(_sum_slots(slots, name="sum_small"), sizes)
    return parts[0][0], dict(zip(_SMALL, parts[1:]))


def kernel(x, p, norm_mix_g, w_in, conv_w, q_norm_g, w_uq, kv_norm_g, w_ukv, w_o, norm_ffn_g, w_up, ffn_conv_w, ffn_conv_b, w_down, ple_norm_g, w_ple_gate, w_ple_proj, final_norm_g, loss_target, m_norm_mix_g, m_w_in, m_conv_w, m_q_norm_g, m_w_uq, m_kv_norm_g, m_w_ukv, m_w_o, m_norm_ffn_g, m_w_up, m_ffn_conv_w, m_ffn_conv_b, m_w_down, m_ple_norm_g, m_w_ple_gate, m_w_ple_proj, m_final_norm_g, v_norm_mix_g, v_w_in, v_conv_w, v_q_norm_g, v_w_uq, v_kv_norm_g, v_w_ukv, v_w_o, v_norm_ffn_g, v_w_up, v_ffn_conv_w, v_ffn_conv_b, v_w_down, v_ple_norm_g, v_w_ple_gate, v_w_ple_proj, v_final_norm_g):
    names = ["norm_mix_g", "w_in", "conv_w", "q_norm_g", "w_uq", "kv_norm_g", "w_ukv", "w_o", "norm_ffn_g", "w_up",
             "ffn_conv_w", "ffn_conv_b", "w_down", "ple_norm_g", "w_ple_gate", "w_ple_proj", "final_norm_g"]
    w = dict(zip(names, (norm_mix_g, w_in, conv_w, q_norm_g, w_uq, kv_norm_g, w_ukv, w_o, norm_ffn_g, w_up,
                         ffn_conv_w, ffn_conv_b, w_down, ple_norm_g, w_ple_gate, w_ple_proj, final_norm_g)))
    m = dict(zip(names, (m_norm_mix_g, m_w_in, m_conv_w, m_q_norm_g, m_w_uq, m_kv_norm_g, m_w_ukv, m_w_o,
                         m_norm_ffn_g, m_w_up, m_ffn_conv_w, m_ffn_conv_b, m_w_down, m_ple_norm_g, m_w_ple_gate,
                         m_w_ple_proj, m_final_norm_g)))
    v = dict(zip(names, (v_norm_mix_g, v_w_in, v_conv_w, v_q_norm_g, v_w_uq, v_kv_norm_g, v_w_ukv, v_w_o,
                         v_norm_ffn_g, v_w_up, v_ffn_conv_w, v_ffn_conv_b, v_w_down, v_ple_norm_g, v_w_ple_gate,
                         v_w_ple_proj, v_final_norm_g)))

    core = lax.axis_index("c").astype(jnp.int32).reshape(1)
    chip = (2 * lax.axis_index("x") + lax.axis_index("y")).astype(jnp.int32).reshape(1)

    first = _chips_start(_early_shards(w), scatter=False, name="gather_early_start")
    rope = _rope_tables(x.shape[1], first[4][0, 0])
    late_shards = [w[n][0].astype(BF16) for n in _LATE_W]
    ready, *late_shards = lax.optimization_barrier((rope[0], *late_shards))
    own, landed = _chips_wait(first, ready, scatter=False, name="gather_early_wait")
    wts = _early_weights(_fill_own_slot(landed, own, chip), w)
    wts["rope"] = (ready,) + tuple(rope[1:])
    late_shards[0], _ = lax.optimization_barrier((late_shards[0], own[0]))
    gather = _chips_start(late_shards, scatter=False, name="gather_late_start")
    wts["norm_mix_g"] = wts["norm_mix_g"] + gather[4][0, 0]

    def late_weights(after):
        shards, landed = _chips_wait(gather, after, scatter=False, name="gather_late_wait")
        return _layout_late({n: _full_from_slots(n, g4)
                             for n, g4 in zip(_LATE_W, _fill_own_slot(landed, shards, chip))})

    def reduce_early(g):
        parts = _chip_partials(_LATE_W, g, core, tag="early")
        scatter = _chips_start([t16 for _, t16 in parts], scatter=True, name="scatter_early_start")

        def finish(after):
            _, landed = _chips_wait(scatter, after, scatter=True, name="scatter_early_wait")
            return [_sum_chips(a, t32, chip, name="sum_chips_" + n) for n, a, (t32, _) in zip(_LATE_W, landed, parts)]

        return scatter[4], finish

    def reduce_last(g):
        parts = _chip_partials(_EARLY_W, _true_matrices(g), core, tag="late")
        scatter = _chips_start([t16 for _, t16 in parts], scatter=True, name="scatter_late_start")

        def finish(after):
            _, landed = _chips_wait(scatter, after, scatter=True, name="scatter_late_wait")
            return [_sum_chips(a, t32, chip, name="sum_chips_" + n) for n, a, (t32, _) in zip(_EARLY_W, landed, parts)]

        return scatter[4], finish

    loss, grad_x, small_grads, halves_early, halves_last = _local_step(
        x[0], p[0, 0], loss_target[0], wts, late_weights, reduce_early, reduce_last)
    g_full = _true_vectors(small_grads)
    whole = _join_halves(halves_last + halves_early)
    big = {n: a.reshape(-1, a.shape[2]) for n, a in zip(_BIG, whole)}

    g_out, d_out, m_out, v_out = {}, {}, {}, {}
    for n in _BIG:
        shape = w[n].shape
        g = big[n]
        d, mn, vn = _adamw(w[n][0], g, m[n][0], v[n][0], name="adamw_" + n)
        g_out[n], d_out[n], m_out[n], v_out[n] = (a.reshape(shape) for a in (g, d, mn, vn))

    loss, small = _reduce_small(g_full, loss)
    chip = 2 * lax.axis_index("x") + lax.axis_index("y")
    g_small = {}
    for n in _SMALL:
        shape = w[n].shape
        g = small[n]
        if n in ("conv_w", "ffn_conv_w"):
            width = shape[-1]
            g = lax.dynamic_slice(g.reshape(3, N_CHIPS * width), (0, chip * width), (3, width))
        g_small[n] = g.reshape(shape)
    flat = [[src[n].reshape(-1, src[n].shape[-1]) for n in _SMALL] for src in (w, g_small, m, v)]
    for n, (d, mn, vn) in zip(_SMALL, _adamw_many(*flat, name="adamw_small")):
        shape = w[n].shape
        g_out[n], d_out[n], m_out[n], v_out[n] = g_small[n], d.reshape(shape), mn.reshape(shape), vn.reshape(shape)

    return (loss, grad_x[None], *[g_out[n] for n in names], *[d_out[n] for n in names],
            *[m_out[n] for n in names], *[v_out[n] for n in names])
```

```python
import functools

import jax
import jax.numpy as jnp
from jax import lax
from jax.experimental import pallas as pl
from jax.experimental.pallas import tpu as pltpu

F32 = jnp.float32
BF16 = jnp.bfloat16

D_MODEL = 1024
CONV_WIDTH = 512
Q_LORA = 256
KV_LORA = 128
QK_NOPE = 64
QK_ROPE = 32
V_HEAD = 64
N_HEADS = 8
HEAD_PAD = 128
D_ATT = N_HEADS * HEAD_PAD
D_IN = 3 * CONV_WIDTH + Q_LORA + KV_LORA + QK_ROPE
D_IN_PAD = 3 * CONV_WIDTH + Q_LORA + KV_LORA + HEAD_PAD
D_FF = 2816
ROPE_THETA = 10000.0
EPS = 1e-6
SM_SCALE = (QK_NOPE + QK_ROPE) ** -0.5
ONES_LANE = V_HEAD

ADAM_LR = 0.001
ADAM_B1 = 0.9
ADAM_B2 = 0.999
ADAM_EPS = 1e-08
ADAM_WD = 0.01
ADAM_STEP = 10

N_CHIPS = 4
N_DEV = 8
MESH = pl.DeviceIdType.MESH
ANY = pl.BlockSpec(memory_space=pl.ANY)


def _params(sem):
    return pltpu.CompilerParams(dimension_semantics=sem)


MM_PIECE = 256


def _pieces(total, width=MM_PIECE):
    return [(off, min(width, total - off)) for off in range(0, total, width)]


def _mm(a, b, *, name, ta=False, tb=False, add=None, out_dtype=F32, tm=512, tn=512, tk=512, b_split=False,
        o_shards=False):
    k, m = a.shape if ta else a.shape[::-1]
    if b_split:
        _, kb, nh = b.shape
        n = 2 * nh
    elif tb:
        n, kb = b.shape
    else:
        kb, n = b.shape
    assert kb == k, (name, a.shape, b.shape)
    tm, tn, tk = min(tm, m), min(tn, n), min(tk, k)
    assert m % tm == 0 and n % tn == 0 and k % tk == 0, (name, m, n, k, tm, tn, tk)
    gm, gn, gk = m // tm, n // tn, k // tk

    a_spec = pl.BlockSpec((tk, tm), lambda i, j, kk: (kk, i)) if ta else pl.BlockSpec((tm, tk), lambda i, j, kk: (i, kk))
    if b_split:
        assert gn % 2 == 0
        b_spec = pl.BlockSpec((None, tk, tn), lambda i, j, kk: (j // (gn // 2), kk, j % (gn // 2)))
    elif tb:
        b_spec = pl.BlockSpec((tn, tk), lambda i, j, kk: (j, kk))
    else:
        b_spec = pl.BlockSpec((tk, tn), lambda i, j, kk: (kk, j))
    if o_shards:
        o_spec = pl.BlockSpec((None, tm, tn), lambda i, j, kk: (j, i, 0))
        o_shape = jax.ShapeDtypeStruct((gn, m, tn), out_dtype)
    else:
        o_spec = pl.BlockSpec((tm, tn), lambda i, j, kk: (i, j))
        o_shape = jax.ShapeDtypeStruct((m, n), out_dtype)
    dims = (((0 if ta else 1,), (1 if tb else 0,)), ((), ()))

    def body(*refs):
        a_ref, b_ref = refs[:2]
        add_ref = None if add is None else refs[2]
        o_ref = refs[2 if add is None else 3]
        acc_ref = None if gk == 1 else refs[-1]
        kk = pl.program_id(2)
        rhs = b_ref[...].astype(BF16)

        def finish(r, rows):
            if add_ref is not None:
                r = r + add_ref[rows, :]
            o_ref[rows, :] = r.astype(o_ref.dtype)

        for off, size in _pieces(tm):
            rows = slice(off, off + size)
            lhs = (a_ref[:, rows] if ta else a_ref[rows, :]).astype(BF16)
            part = lax.dot_general(lhs, rhs, dims, preferred_element_type=F32)
            if gk == 1:
                finish(part, rows)
            else:
                acc_ref[rows, :] = jnp.where(kk == 0, part, acc_ref[rows, :] + part)

        if gk > 1:
            @pl.when(kk == gk - 1)
            def _():
                finish(acc_ref[...], slice(None))

    in_specs = [a_spec, b_spec]
    args = [a, b]
    if add is not None:
        in_specs.append(pl.BlockSpec((tm, tn), lambda i, j, kk: (i, j)))
        args.append(add)
    return pl.pallas_call(
        body, name=name, grid=(gm, gn, gk), in_specs=in_specs, out_specs=o_spec, out_shape=o_shape,
        scratch_shapes=[] if gk == 1 else [pltpu.VMEM((tm, tn), F32)],
        compiler_params=_params(("parallel", "parallel", "arbitrary")),
    )(*args)


def _rms_scale(v):
    return lax.rsqrt(jnp.mean(v * v, axis=-1, keepdims=True) + EPS)


def _rms_bwd_rows(v, g, dy):
    r = _rms_scale(v)
    vh = v * r
    dyg = dy * g
    dv = r * (dyg - vh * jnp.mean(dyg * vh, axis=-1, keepdims=True))
    return dv, dy * vh


def _shift_down(v, first_row):
    row = lax.broadcasted_iota(jnp.int32, v.shape, 0)
    return jnp.where(row == 0, first_row, pltpu.roll(v, 1, 0))


def _shift_up(v, last_row):
    n = v.shape[0]
    row = lax.broadcasted_iota(jnp.int32, v.shape, 0)
    return jnp.where(row == n - 1, last_row, pltpu.roll(v, n - 1, 0))


def _rope(t, cos, sin_a, sin_b):
    return t * cos + pltpu.roll(t, HEAD_PAD - 16, 1) * sin_a + pltpu.roll(t, 16, 1) * sin_b


def _rope_bwd(d, cos, sin_a, sin_b):
    return d * cos + pltpu.roll(d * sin_a, 16, 1) + pltpu.roll(d * sin_b, HEAD_PAD - 16, 1)


def _sigmoid(v):
    return 1.0 / (1.0 + jnp.exp(-v))


def _halo_specs(ts, s, width, col):
    nb = ts // 8
    last = s // 8 - 1
    prev = pl.BlockSpec((8, width), lambda i: (jnp.maximum(i * nb - 1, 0), col))
    nxt = pl.BlockSpec((8, width), lambda i: (jnp.minimum((i + 1) * nb, last), col))
    return prev, nxt


def _mm_fused(a, b, *, name, epilogue, row_outs, rows=(), vecs=(), n_vec_out=0, tb=False, a_split=False,
              prologue=None, second=None, transposed_out=None, tm=512):
    if a_split:
        _, m, kh = a.shape
        k = 2 * kh
    else:
        m, k = a.shape
    n = b.shape[0] if tb else b.shape[1]
    assert (b.shape[1] if tb else b.shape[0]) == k, (name, a.shape, b.shape)
    assert m % tm == 0, (name, m, tm)
    n_a = 2 if a_split else 1
    nr, nv = len(rows), len(vecs)
    n_pro = 0 if prologue is None else 1
    n_sec = 0 if second is None else 2
    n_t = 0 if transposed_out is None else 1
    dims = (((1,), (1 if tb else 0,)), ((), ()))

    def body(*refs):
        a_refs, b_ref = refs[:n_a], refs[n_a]
        refs = refs[n_a + 1:]
        sec_refs = refs[:n_sec]
        row_refs, vec_refs = refs[n_sec:n_sec + nr], refs[n_sec + nr:n_sec + nr + nv]
        outs = refs[n_sec + nr + nv:]
        row_out_refs = outs[n_pro:n_pro + len(row_outs)]
        t_out_refs = outs[n_pro + len(row_outs):n_pro + len(row_outs) + n_t]
        vec_out_refs = outs[n_pro + len(row_outs) + n_t:n_pro + len(row_outs) + n_t + n_vec_out]
        vec_vals = [v[...] for v in vec_refs]
        if a_split:
            kh = k // 2
            rhs = [(b_ref[:, :kh], b_ref[:, kh:]) if tb else (b_ref[:kh, :], b_ref[kh:, :])][0]
            rhs = [h.astype(BF16) for h in rhs]
        else:
            rhs = [b_ref[...].astype(BF16)]
        vec_sums = [None] * n_vec_out

        for off, size in _pieces(tm):
            rs = slice(off, off + size)
            if prologue is None:
                lhs = [a_ref[rs, :].astype(BF16) for a_ref in a_refs]
            else:
                lhs = [prologue(a_refs[0][rs, :], vec_vals)]
                outs[0][rs, :] = lhs[0]
            r = lax.dot_general(lhs[0], rhs[0], dims, preferred_element_type=F32)
            for l2, r2 in zip(lhs[1:], rhs[1:]):
                r = r + lax.dot_general(l2, r2, dims, preferred_element_type=F32)
            if second is not None:
                r = r + jnp.dot(sec_refs[0][rs, :].astype(BF16), sec_refs[1][...].astype(BF16),
                                preferred_element_type=F32)
            row_vals, vec_parts = epilogue(r, [x[rs, :] for x in row_refs], vec_vals)
            for ref, val in zip(row_out_refs, row_vals):
                ref[rs, :] = val.astype(ref.dtype)
            for ref in t_out_refs:
                ref[:, rs] = row_vals[0].T.astype(ref.dtype)
            vec_sums = [p if t is None else t + p for t, p in zip(vec_sums, vec_parts)]

        if n_vec_out:
            @pl.when(pl.program_id(0) == 0)
            def _():
                for ref in vec_out_refs:
                    ref[...] = jnp.zeros_like(ref)

            for ref, val in zip(vec_out_refs, vec_sums):
                ref[...] += val

    if a_split:
        a_specs = [pl.BlockSpec((None, tm, k // 2), lambda i: (0, i, 0)),
                   pl.BlockSpec((None, tm, k // 2), lambda i: (1, i, 0))]
    else:
        a_specs = [pl.BlockSpec((tm, k), lambda i: (i, 0))]
    b_spec = pl.BlockSpec(b.shape, lambda i: (0, 0))
    row_spec = pl.BlockSpec((tm, n), lambda i: (i, 0))
    out_specs, out_shape = [], []
    if prologue is not None:
        out_specs.append(pl.BlockSpec((tm, k), lambda i: (i, 0)))
        out_shape.append(jax.ShapeDtypeStruct((m, k), BF16))
    out_specs += [row_spec] * len(row_outs)
    out_shape += [jax.ShapeDtypeStruct((m, n), dt) for dt in row_outs]
    if transposed_out is not None:
        out_specs.append(pl.BlockSpec((n, tm), lambda i: (0, i)))
        out_shape.append(jax.ShapeDtypeStruct((n, m), transposed_out))
    out_specs += [pl.BlockSpec((1, n), lambda i: (0, 0))] * n_vec_out
    out_shape += [jax.ShapeDtypeStruct((1, n), F32)] * n_vec_out
    sec_specs, sec_args = [], []
    if second is not None:
        k2 = second[0].shape[1]
        sec_specs = [pl.BlockSpec((tm, k2), lambda i: (i, 0)), pl.BlockSpec((k2, n), lambda i: (0, 0))]
        sec_args = list(second)
    res = pl.pallas_call(
        body, name=name, grid=(m // tm,),
        in_specs=a_specs + [b_spec] + sec_specs + [row_spec] * nr
        + [pl.BlockSpec((1, v.shape[1]), lambda i: (0, 0)) for v in vecs],
        out_specs=out_specs, out_shape=out_shape,
        compiler_params=_params(("arbitrary" if n_vec_out else "parallel",)),
    )(*([a] * n_a), b, *sec_args, *rows, *vecs)
    split = n_pro + len(row_outs) + n_t
    return list(res[:split]), list(res[split:])


def _pro_rms(a, vecs):
    return (a * _rms_scale(a) * vecs[0]).astype(BF16)


def _epi_plain(r, rows, vecs):
    return [r], []


def _epi_add_rms(r, rows, vecs):
    xn = r + rows[0]
    return [xn, xn * _rms_scale(xn) * vecs[0]], []


def _epi_rms_bwd(r, rows, vecs):
    dv, dg_rows = _rms_bwd_rows(rows[0], vecs[0], r)
    return [dv + rows[1]], [jnp.sum(dg_rows, axis=0, keepdims=True)]


def _mix_pre(z, conv_w8, gq, gkv, cos, sin_a, sin_b, *, ts=1024):
    s = z.shape[0]
    n = s // ts
    cw = CONV_WIDTH

    def body(z_ref, xcp, xcn, cgp, cgn, w_ref, gq_ref, gkv_ref, cos_ref, sa_ref, sb_ref,
             yc_ref, qn_ref, kvn_ref, kr_ref):
        i = pl.program_id(0)
        xc = z_ref[:, 0:cw]
        bg = z_ref[:, cw:2 * cw]
        cg = z_ref[:, 2 * cw:3 * cw]
        m = cg * xc
        m_prev = jnp.where(i > 0, xcp[7:8, :] * cgp[7:8, :], 0.0)
        m_next = jnp.where(i < n - 1, xcn[0:1, :] * cgn[0:1, :], 0.0)
        cm = _shift_down(m, m_prev) * w_ref[0:1, :] + m * w_ref[1:2, :] + _shift_up(m, m_next) * w_ref[2:3, :]
        yc_ref[...] = (bg * cm).astype(BF16)
        ql = z_ref[:, 3 * cw:3 * cw + Q_LORA]
        qn_ref[...] = (ql * _rms_scale(ql) * gq_ref[...]).astype(BF16)
        kvl = z_ref[:, 3 * cw + Q_LORA:3 * cw + Q_LORA + KV_LORA]
        kvn_ref[...] = (kvl * _rms_scale(kvl) * gkv_ref[...]).astype(BF16)
        kr_ref[...] = _rope(z_ref[:, D_IN_PAD - HEAD_PAD:D_IN_PAD], cos_ref[...], sa_ref[...], sb_ref[...])

    xcp, xcn = _halo_specs(ts, s, cw, 0)
    cgp, cgn = _halo_specs(ts, s, cw, 2)
    tab = pl.BlockSpec((ts, HEAD_PAD), lambda i: (i, 0))
    return pl.pallas_call(
        body, name="mix_pre", grid=(n,),
        in_specs=[pl.BlockSpec((ts, D_IN_PAD), lambda i: (i, 0)), xcp, xcn, cgp, cgn,
                  pl.BlockSpec((8, cw), lambda i: (0, 0)), pl.BlockSpec((1, Q_LORA), lambda i: (0, 0)),
                  pl.BlockSpec((1, KV_LORA), lambda i: (0, 0)), tab, tab, tab],
        out_specs=[pl.BlockSpec((ts, cw), lambda i: (i, 0)), pl.BlockSpec((ts, Q_LORA), lambda i: (i, 0)),
                   pl.BlockSpec((ts, KV_LORA), lambda i: (i, 0)), tab],
        out_shape=[jax.ShapeDtypeStruct((s, cw), BF16), jax.ShapeDtypeStruct((s, Q_LORA), BF16),
                   jax.ShapeDtypeStruct((s, KV_LORA), BF16), jax.ShapeDtypeStruct((s, HEAD_PAD), F32)],
        compiler_params=_params(("parallel",)),
    )(z, z, z, z, z, conv_w8, gq, gkv, cos, sin_a, sin_b)


def _mix_bwd(z, dyc, dqn, dkvn, dkr, conv_w8, gq, gkv, cos, sin_a, sin_b, *, ts=1024):
    s = z.shape[0]
    n = s // ts
    cw = CONV_WIDTH

    def body(z_ref, xcp, xcn, bgp, bgn, cgp, cgn, dyc_ref, dycp, dycn, dqn_ref, dkvn_ref, dkr_ref,
             w_ref, gq_ref, gkv_ref, cos_ref, sa_ref, sb_ref,
             dz_ref, dw0_ref, dw1_ref, dw2_ref, dgq_ref, dgkv_ref):
        i = pl.program_id(0)

        @pl.when(i == 0)
        def _():
            for r in (dw0_ref, dw1_ref, dw2_ref, dgq_ref, dgkv_ref):
                r[...] = jnp.zeros_like(r)

        xc = z_ref[:, 0:cw]
        bg = z_ref[:, cw:2 * cw]
        cg = z_ref[:, 2 * cw:3 * cw]
        w0, w1, w2 = w_ref[0:1, :], w_ref[1:2, :], w_ref[2:3, :]
        m = cg * xc
        m_dn = _shift_down(m, jnp.where(i > 0, xcp[7:8, :] * cgp[7:8, :], 0.0))
        m_up = _shift_up(m, jnp.where(i < n - 1, xcn[0:1, :] * cgn[0:1, :], 0.0))
        cm = m_dn * w0 + m * w1 + m_up * w2
        dyc_v = dyc_ref[...]
        dcm = dyc_v * bg
        dcm_dn = _shift_down(dcm, jnp.where(i > 0, dycp[7:8, :] * bgp[7:8, :], 0.0))
        dcm_up = _shift_up(dcm, jnp.where(i < n - 1, dycn[0:1, :] * bgn[0:1, :], 0.0))
        dm = dcm_up * w0 + dcm * w1 + dcm_dn * w2
        dz_ref[:, 0:cw] = (dm * cg).astype(BF16)
        dz_ref[:, cw:2 * cw] = (dyc_v * cm).astype(BF16)
        dz_ref[:, 2 * cw:3 * cw] = (dm * xc).astype(BF16)
        dw0_ref[...] += jnp.sum(dcm * m_dn, axis=0, keepdims=True)
        dw1_ref[...] += jnp.sum(dcm * m, axis=0, keepdims=True)
        dw2_ref[...] += jnp.sum(dcm * m_up, axis=0, keepdims=True)

        dql, dgq_rows = _rms_bwd_rows(z_ref[:, 3 * cw:3 * cw + Q_LORA], gq_ref[...], dqn_ref[...])
        dz_ref[:, 3 * cw:3 * cw + Q_LORA] = dql.astype(BF16)
        dgq_ref[...] += jnp.sum(dgq_rows, axis=0, keepdims=True)
        dkvl, dgkv_rows = _rms_bwd_rows(z_ref[:, 3 * cw + Q_LORA:3 * cw + Q_LORA + KV_LORA], gkv_ref[...],
                                        dkvn_ref[...])
        dz_ref[:, 3 * cw + Q_LORA:3 * cw + Q_LORA + KV_LORA] = dkvl.astype(BF16)
        dgkv_ref[...] += jnp.sum(dgkv_rows, axis=0, keepdims=True)

        lane = lax.broadcasted_iota(jnp.int32, (ts, HEAD_PAD), 1)
        rope_lane = (lane >= QK_NOPE) & (lane < QK_NOPE + QK_ROPE)
        dk = _rope_bwd(dkr_ref[...], cos_ref[...], sa_ref[...], sb_ref[...])
        dz_ref[:, D_IN_PAD - HEAD_PAD:D_IN_PAD] = jnp.where(rope_lane, dk, 0.0).astype(BF16)

    xcp, xcn = _halo_specs(ts, s, cw, 0)
    bgp, bgn = _halo_specs(ts, s, cw, 1)
    cgp, cgn = _halo_specs(ts, s, cw, 2)
    dycp, dycn = _halo_specs(ts, s, cw, 0)
    tab = pl.BlockSpec((ts, HEAD_PAD), lambda i: (i, 0))

    def vec(width):
        return pl.BlockSpec((1, width), lambda i: (0, 0))

    outs = pl.pallas_call(
        body, name="mix_bwd", grid=(n,),
        in_specs=[pl.BlockSpec((ts, D_IN_PAD), lambda i: (i, 0)), xcp, xcn, bgp, bgn, cgp, cgn,
                  pl.BlockSpec((ts, cw), lambda i: (i, 0)), dycp, dycn,
                  pl.BlockSpec((ts, Q_LORA), lambda i: (i, 0)), pl.BlockSpec((ts, KV_LORA), lambda i: (i, 0)), tab,
                  pl.BlockSpec((8, cw), lambda i: (0, 0)), vec(Q_LORA), vec(KV_LORA), tab, tab, tab],
        out_specs=[pl.BlockSpec((ts, D_IN_PAD), lambda i: (i, 0)), vec(cw), vec(cw), vec(cw), vec(Q_LORA),
                   vec(KV_LORA)],
        out_shape=[jax.ShapeDtypeStruct((s, D_IN_PAD), BF16)] + [jax.ShapeDtypeStruct((1, cw), F32)] * 3
        + [jax.ShapeDtypeStruct((1, Q_LORA), F32), jax.ShapeDtypeStruct((1, KV_LORA), F32)],
        compiler_params=_params(("arbitrary",)),
    )(z, z, z, z, z, z, z, dyc, dyc, dyc, dqn, dkvn, dkr, conv_w8, gq, gkv, cos, sin_a, sin_b)
    dz, dw0, dw1, dw2, dgq, dgkv = outs
    return dz, jnp.concatenate([dw0, dw1, dw2], axis=0), dgq, dgkv


def _qkv_proj(qn, kvn, kr, w_uq_p, w_kv_p, cos, sin_a, sin_b, *, ts=1024):
    s = qn.shape[0]

    def body(qn_ref, kvn_ref, kr_ref, wq_ref, wkv_ref, cos_ref, sa_ref, sb_ref, q_ref, k_ref, v_ref, qt_ref):
        cos_v, sa, sb = cos_ref[...], sa_ref[...], sb_ref[...]
        q = jnp.dot(qn_ref[...], wq_ref[...], preferred_element_type=F32)
        kv = jnp.dot(kvn_ref[...], wkv_ref[...], preferred_element_type=F32)
        kr_v = kr_ref[...]
        lane = lax.broadcasted_iota(jnp.int32, (1, HEAD_PAD), 1)
        ones_lane = (lane == ONES_LANE).astype(F32)
        for h in range(N_HEADS):
            blk = slice(h * HEAD_PAD, (h + 1) * HEAD_PAD)
            q_h = _rope(q[:, blk], cos_v, sa, sb) * SM_SCALE
            q_ref[:, blk] = q_h.astype(BF16)
            qt_ref[blk, :] = q_h.T.astype(BF16)
            k_ref[:, blk] = (kv[:, blk] + kr_v).astype(BF16)
            v_ref[:, blk] = (kv[:, D_ATT + h * HEAD_PAD:D_ATT + (h + 1) * HEAD_PAD] + ones_lane).astype(BF16)

    tab = pl.BlockSpec((ts, HEAD_PAD), lambda i: (i, 0))
    wide = pl.BlockSpec((ts, D_ATT), lambda i: (i, 0))
    return pl.pallas_call(
        body, name="qkv_proj", grid=(s // ts,),
        in_specs=[pl.BlockSpec((ts, Q_LORA), lambda i: (i, 0)), pl.BlockSpec((ts, KV_LORA), lambda i: (i, 0)), tab,
                  pl.BlockSpec((Q_LORA, D_ATT), lambda i: (0, 0)), pl.BlockSpec((KV_LORA, 2 * D_ATT), lambda i: (0, 0)),
                  tab, tab, tab],
        out_specs=[wide, wide, wide, pl.BlockSpec((D_ATT, ts), lambda i: (0, i))],
        out_shape=[jax.ShapeDtypeStruct((s, D_ATT), BF16)] * 3 + [jax.ShapeDtypeStruct((D_ATT, s), BF16)],
        compiler_params=_params(("parallel",)),
    )(qn, kvn, kr, w_uq_p, w_kv_p, cos, sin_a, sin_b)


def _qk_bwd(dq, dk, cos, sin_a, sin_b, *, ts=1024):
    s = dq.shape[0]

    def body(dq_ref, dk_ref, cos_ref, sa_ref, sb_ref, dqp_ref, dkr_ref):
        cos_v, sa, sb = cos_ref[...], sa_ref[...], sb_ref[...]
        tot = jnp.zeros((ts, HEAD_PAD), F32)
        for h in range(N_HEADS):
            blk = slice(h * HEAD_PAD, (h + 1) * HEAD_PAD)
            dqp_ref[:, blk] = _rope_bwd(dq_ref[:, blk], cos_v, sa, sb).astype(BF16)
            tot = tot + dk_ref[:, blk]
        dkr_ref[...] = tot

    tab = pl.BlockSpec((ts, HEAD_PAD), lambda i: (i, 0))
    wide = pl.BlockSpec((ts, D_ATT), lambda i: (i, 0))
    return pl.pallas_call(
        body, name="qk_bwd", grid=(s // ts,),
        in_specs=[wide, wide, tab, tab, tab], out_specs=[wide, tab],
        out_shape=[jax.ShapeDtypeStruct((s, D_ATT), BF16), jax.ShapeDtypeStruct((s, HEAD_PAD), F32)],
        compiler_params=_params(("parallel",)),
    )(dq, dk, cos, sin_a, sin_b)


_NT = (((1,), (1,)), ((), ()))


def _flash_fwd(q, k, v, *, tq=1024, tk=1024, per_trip=8, q_per_step=2):
    s = q.shape[0]
    tq, tk = min(tq, s), min(tk, s)
    nk = s // tk
    per_trip = min(per_trip, nk)
    assert nk % per_trip == 0
    q_per_step = min(q_per_step, s // tq)

    def body(q_ref, k_ref, v_ref, o_ref, lse_ref):
        for blk in range(q_per_step):
            rows = pl.ds(blk * tq, tq)
            one_block(q_ref.at[rows, :], k_ref, v_ref, o_ref.at[rows, :], lse_ref.at[rows, :])

    def one_block(q_ref, k_ref, v_ref, o_ref, lse_ref):
        qv = q_ref[...]

        def step(j, carry):
            m, acc = carry
            rows = pl.ds(pl.multiple_of(j * tk, tk), tk)
            sc = lax.dot_general(qv, k_ref[rows, :], _NT, preferred_element_type=F32)
            m_new = jnp.maximum(m, jnp.max(sc, axis=1, keepdims=True))
            p = jnp.exp(sc - m_new).astype(BF16)
            acc = jnp.exp(m - m_new) * acc + jnp.dot(p, v_ref[rows, :], preferred_element_type=F32)
            return m_new, acc

        def trip(t, carry):
            for c in range(per_trip):
                carry = step(per_trip * t + c, carry)
            return carry

        init = (jnp.full((tq, 1), -jnp.inf, F32), jnp.zeros((tq, HEAD_PAD), F32))
        m, acc = lax.fori_loop(0, nk // per_trip, trip, init)
        l = acc[:, ONES_LANE:ONES_LANE + 1]
        o_ref[...] = (acc / l).astype(BF16)
        lse_ref[...] = m + jnp.log(l)

    head = pl.BlockSpec((s, HEAD_PAD), lambda h, i: (0, h))
    tq_step = q_per_step * tq
    return pl.pallas_call(
        body, name="flash_fwd", grid=(N_HEADS, s // tq_step),
        in_specs=[pl.BlockSpec((tq_step, HEAD_PAD), lambda h, i: (i, h)), head, head],
        out_specs=[pl.BlockSpec((tq_step, HEAD_PAD), lambda h, i: (i, h)),
                   pl.BlockSpec((None, tq_step, 1), lambda h, i: (h, i, 0))],
        out_shape=[jax.ShapeDtypeStruct((s, D_ATT), BF16), jax.ShapeDtypeStruct((N_HEADS, s, 1), F32)],
        compiler_params=_params(("parallel", "parallel")),
    )(q, k, v)


def _attn_delta(do, o, *, ts=1024):
    s = do.shape[0]

    def body(do_ref, o_ref, dl_ref):
        for h in range(N_HEADS):
            blk = slice(h * HEAD_PAD, (h + 1) * HEAD_PAD)
            dl_ref[h] = jnp.sum(do_ref[:, blk].astype(F32) * o_ref[:, blk].astype(F32), axis=1, keepdims=True)

    wide = pl.BlockSpec((ts, D_ATT), lambda i: (i, 0))
    return pl.pallas_call(
        body, name="attn_delta", grid=(s // ts,), in_specs=[wide, wide],
        out_specs=pl.BlockSpec((N_HEADS, ts, 1), lambda i: (0, i, 0)),
        out_shape=jax.ShapeDtypeStruct((N_HEADS, s, 1), F32),
        compiler_params=_params(("parallel",)),
    )(do, o)


def _flash_bwd(q, qt, k, v, do, dot, lse, delta, after, *, tq=1024, tk=512, per_trip=8, kv_per_step=2):
    s = q.shape[0]
    tq, tk = min(tq, s), min(tk, s)
    nq = s // tq
    per_trip = min(per_trip, nq)
    assert nq % per_trip == 0
    kv_per_step = min(kv_per_step, s // tk)

    def body(q_ref, qt_ref, do_ref, dot_ref, lse_ref, dl_ref, k_ref, v_ref, after_ref, dq_ref, dk_ref, dv_ref):
        j = pl.program_id(1)

        @pl.when(j == 0)
        def _():
            dq_ref[...] = jnp.zeros_like(dq_ref)

        for blk in range(kv_per_step):
            one_block(q_ref, qt_ref, do_ref, dot_ref, lse_ref, dl_ref, dq_ref,
                      k_ref.at[pl.ds(blk * tk, tk), :], v_ref.at[pl.ds(blk * tk, tk), :],
                      dk_ref.at[pl.ds(blk * tk, tk), :], dv_ref.at[pl.ds(blk * tk, tk), :])

        @pl.when(j == pl.num_programs(1) - 1)
        def _():
            dq_ref[...] *= SM_SCALE

    def one_block(q_ref, qt_ref, do_ref, dot_ref, lse_ref, dl_ref, dq_ref, k_ref, v_ref, dk_ref, dv_ref):
        kv, vv = k_ref[...], v_ref[...]

        def chunk(i, dk_t, dv_t):
            at = pl.multiple_of(i * tq, tq)
            rows = pl.ds(at, tq)
            sc = lax.dot_general(q_ref[rows, :], kv, _NT, preferred_element_type=F32)
            p = jnp.exp(sc - lse_ref[rows, :])
            dp = lax.dot_general(do_ref[rows, :], vv, _NT, preferred_element_type=F32)
            ds = (p * (dp - dl_ref[rows, :])).astype(BF16)
            dv_t = dv_t + jnp.dot(dot_ref[:, rows], p.astype(BF16), preferred_element_type=F32)
            dk_t = dk_t + jnp.dot(qt_ref[:, rows], ds, preferred_element_type=F32)
            dq_ref[rows, :] += jnp.dot(ds, kv, preferred_element_type=F32)
            return dk_t, dv_t

        def step(i, carry):
            for c in range(per_trip):
                carry = chunk(per_trip * i + c, *carry)
            return carry

        zero = jnp.zeros((HEAD_PAD, tk), F32)
        dk_t, dv_t = lax.fori_loop(0, nq // per_trip, step, (zero, zero))
        dk_ref[...] = dk_t.T
        dv_ref[...] = dv_t.T

    head = pl.BlockSpec((s, HEAD_PAD), lambda h, j: (0, h))
    head_t = pl.BlockSpec((HEAD_PAD, s), lambda h, j: (h, 0))
    stat = pl.BlockSpec((None, s, 1), lambda h, j: (h, 0, 0))
    blk = pl.BlockSpec((kv_per_step * tk, HEAD_PAD), lambda h, j: (j, h))
    return pl.pallas_call(
        body, name="flash_bwd", grid=(N_HEADS, s // (kv_per_step * tk)),
        in_specs=[head, head_t, head, head_t, stat, stat, blk, blk, ANY],
        out_specs=[head, blk, blk],
        out_shape=[jax.ShapeDtypeStruct((s, D_ATT), F32)] * 3,
        compiler_params=_params(("parallel", "arbitrary")),
    )(q, qt, do, dot, lse, delta, k, v, after)


FFN_TC = 512
FFN_TG = 1408


FFN_HALO_BF16 = 16
FFN_HALO_F32 = 8


def _row_halo_specs(ts, s, halo, width):
    nb = ts // halo
    last = s // halo - 1
    prev = pl.BlockSpec((halo, width), lambda i, j: (jnp.maximum(i * nb - 1, 0), 0))
    nxt = pl.BlockSpec((halo, width), lambda i, j: (jnp.minimum((i + 1) * nb, last), 0))
    return prev, nxt


def _ext_rows(prev, main, nxt, first, last):
    return jnp.concatenate([jnp.where(first, jnp.zeros_like(prev), prev), main,
                            jnp.where(last, jnp.zeros_like(nxt), nxt)], axis=0)


def _ext_conv(a, w):
    a_dn = pltpu.roll(a, 1, 0)
    a_up = pltpu.roll(a, a.shape[0] - 1, 0)
    return a_dn * w[0:1, :] + a * w[1:2, :] + a_up * w[2:3, :], a_dn, a_up


def _ffn_pieces(tg):
    return [(off, min(FFN_TC, tg - off)) for off in range(0, tg, FFN_TC)]


def _ffn_fwd(hf, w_up, w, b, *, ts=1024, tg=FFN_TG):
    s = hf.shape[0]
    n, ng, halo = s // ts, D_FF // tg, FFN_HALO_BF16

    def body(h_ref, hp_ref, hn_ref, wg_ref, wu_ref, cw_ref, cb_ref, a_ref, act_ref):
        i = pl.program_id(0)
        ext = _ext_rows(hp_ref[...], h_ref[...], hn_ref[...], i == 0, i == n - 1)
        for off, width in _ffn_pieces(tg):
            cols = slice(off, off + width)
            gate_up = []
            for half, w_ref in enumerate((wg_ref, wu_ref)):
                a_ext = jnp.dot(ext, w_ref[:, cols], preferred_element_type=F32)
                a_ref[half, :, cols] = a_ext[halo:halo + ts]
                conv = _ext_conv(a_ext, cw_ref[half, :, cols])[0]
                gate_up.append(conv[halo:halo + ts] + cb_ref[half, :, cols])
            g, u = gate_up
            act_ref[:, cols] = (g * _sigmoid(g) * u).astype(BF16)

    prev, nxt = _row_halo_specs(ts, s, halo, D_MODEL)
    return pl.pallas_call(
        body, name="ffn_fwd", grid=(n, ng),
        in_specs=[pl.BlockSpec((ts, D_MODEL), lambda i, j: (i, 0)), prev, nxt,
                  pl.BlockSpec((D_MODEL, tg), lambda i, j: (0, j)), pl.BlockSpec((D_MODEL, tg), lambda i, j: (0, j + ng)),
                  pl.BlockSpec((2, 8, tg), lambda i, j: (0, 0, j)), pl.BlockSpec((2, 1, tg), lambda i, j: (0, 0, j))],
        out_specs=[pl.BlockSpec((2, ts, tg), lambda i, j: (0, i, j)), pl.BlockSpec((ts, tg), lambda i, j: (i, j))],
        out_shape=[jax.ShapeDtypeStruct((2, s, D_FF), F32), jax.ShapeDtypeStruct((s, D_FF), BF16)],
        compiler_params=_params(("parallel", "parallel")),
    )(hf, hf, hf, w_up, w_up, w, b)


def _ffn_bwd(dx2, w_down, a_pre, w, b, *, ts=1024, tg=FFN_TG):
    s = dx2.shape[0]
    n, ng, halo = s // ts, D_FF // tg, FFN_HALO_F32
    main = slice(halo, halo + ts)

    def body(dx_ref, dxp_ref, dxn_ref, wd_ref, a_ref, ap_ref, an_ref, cw_ref, cb_ref, o_ref, dw_ref, db_ref):
        i, j = pl.program_id(0), pl.program_id(1)
        first, last = i == 0, i == n - 1

        @pl.when(first & (j == 0))
        def _():
            dw_ref[...] = jnp.zeros_like(dw_ref)
            db_ref[...] = jnp.zeros_like(db_ref)

        dx_ext = _ext_rows(dxp_ref[...], dx_ref[...], dxn_ref[...], first, last).astype(BF16)
        for off, width in _ffn_pieces(tg):
            cols = slice(off, off + width)
            dact = lax.dot_general(dx_ext, wd_ref[cols, :], _NT, preferred_element_type=F32)
            halves = []
            for half in range(2):
                a_ext = _ext_rows(ap_ref[half, :, cols], a_ref[half, :, cols], an_ref[half, :, cols], first, last)
                conv, a_dn, a_up = _ext_conv(a_ext, cw_ref[half, :, cols])
                halves.append((conv + cb_ref[half, :, cols], a_dn, a_ext, a_up))
            g, u = halves[0][0], halves[1][0]
            sg = _sigmoid(g)
            grads = (dact * u * (sg * (1.0 + g * (1.0 - sg))), dact * (g * sg))
            for half in range(2):
                d = grads[half]
                _, a_dn, a_ext, a_up = halves[half]
                wv = cw_ref[half, :, cols]
                d_pre = (pltpu.roll(d, d.shape[0] - 1, 0) * wv[0:1, :] + d * wv[1:2, :]
                         + pltpu.roll(d, 1, 0) * wv[2:3, :])
                o_ref[half, :, cols] = d_pre[main].astype(BF16)
                dm = d[main]
                dw_ref[j, half, 0:1, cols] += jnp.sum(dm * a_dn[main], axis=0, keepdims=True)
                dw_ref[j, half, 1:2, cols] += jnp.sum(dm * a_ext[main], axis=0, keepdims=True)
                dw_ref[j, half, 2:3, cols] += jnp.sum(dm * a_up[main], axis=0, keepdims=True)
                db_ref[j, half, :, cols] += jnp.sum(dm, axis=0, keepdims=True)

    dxp, dxn = _row_halo_specs(ts, s, halo, D_MODEL)
    nb, lastb = ts // halo, s // halo - 1
    a_main = pl.BlockSpec((2, ts, tg), lambda i, j: (0, i, j))
    a_prev = pl.BlockSpec((2, halo, tg), lambda i, j: (0, jnp.maximum(i * nb - 1, 0), j))
    a_next = pl.BlockSpec((2, halo, tg), lambda i, j: (0, jnp.minimum((i + 1) * nb, lastb), j))
    da_pre, dw, db = pl.pallas_call(
        body, name="ffn_bwd", grid=(n, ng),
        in_specs=[pl.BlockSpec((ts, D_MODEL), lambda i, j: (i, 0)), dxp, dxn,
                  pl.BlockSpec((tg, D_MODEL), lambda i, j: (j, 0)), a_main, a_prev, a_next,
                  pl.BlockSpec((2, 8, tg), lambda i, j: (0, 0, j)), pl.BlockSpec((2, 1, tg), lambda i, j: (0, 0, j))],
        out_specs=[a_main, pl.BlockSpec((ng, 2, 8, tg), lambda i, j: (0, 0, 0, 0)),
                   pl.BlockSpec((ng, 2, 1, tg), lambda i, j: (0, 0, 0, 0))],
        out_shape=[jax.ShapeDtypeStruct((2, s, D_FF), BF16), jax.ShapeDtypeStruct((ng, 2, 8, tg), F32),
                   jax.ShapeDtypeStruct((ng, 2, 1, tg), F32)],
        compiler_params=_params(("arbitrary", "arbitrary")),
    )(dx2, dx2, dx2, w_down, a_pre, a_pre, a_pre, w, b)
    return (da_pre, dw.transpose(1, 2, 0, 3).reshape(2, 8, D_FF), db.transpose(1, 2, 0, 3).reshape(2, 1, D_FF))


def _ple_final(x2, n3, p, target, gf, w_pg, w_pp, *, ts=512):
    s, d = x2.shape
    dp = p.shape[1]

    def body(x2_ref, n3_ref, p_ref, t_ref, gf_ref, wg_ref, wp_ref, loss_ref, dx3_ref, dgl_ref, dpp_ref, dgf_ref):
        @pl.when(pl.program_id(0) == 0)
        def _():
            loss_ref[...] = jnp.zeros_like(loss_ref)
            dgf_ref[...] = jnp.zeros_like(dgf_ref)

        gfv = gf_ref[...]
        wg, wp = wg_ref[...], wp_ref[...]
        for off, size in _pieces(ts):
            rs = slice(off, off + size)
            gate = _sigmoid(jnp.dot(n3_ref[rs, :], wg, preferred_element_type=F32))
            ppv = jnp.dot(p_ref[rs, :].astype(BF16), wp, preferred_element_type=F32)
            x3 = x2_ref[rs, :] + gate * ppv
            err = x3 * _rms_scale(x3) * gfv - t_ref[rs, :]
            loss_ref[...] += 0.5 * jnp.sum(jnp.mean(err * err, axis=-1, keepdims=True), axis=0, keepdims=True)
            dx3, dgf_rows = _rms_bwd_rows(x3, gfv, err * (1.0 / d))
            dgf_ref[...] += jnp.sum(dgf_rows, axis=0, keepdims=True)
            dx3_ref[rs, :] = dx3
            dgl_ref[rs, :] = (dx3 * ppv * gate * (1.0 - gate)).astype(BF16)
            dpp_ref[rs, :] = (dx3 * gate).astype(BF16)

    row = pl.BlockSpec((ts, d), lambda i: (i, 0))
    vec = pl.BlockSpec((1, d), lambda i: (0, 0))
    return pl.pallas_call(
        body, name="ple_final", grid=(s // ts,),
        in_specs=[row, row, pl.BlockSpec((ts, dp), lambda i: (i, 0)), row, vec,
                  pl.BlockSpec((d, d), lambda i: (0, 0)), pl.BlockSpec((dp, d), lambda i: (0, 0))],
        out_specs=[pl.BlockSpec((1, 128), lambda i: (0, 0)), row, row, row, vec],
        out_shape=[jax.ShapeDtypeStruct((1, 128), F32), jax.ShapeDtypeStruct((s, d), F32),
                   jax.ShapeDtypeStruct((s, d), BF16), jax.ShapeDtypeStruct((s, d), BF16),
                   jax.ShapeDtypeStruct((1, d), F32)],
        compiler_params=_params(("arbitrary",)),
    )(x2, n3, p, target, gf, w_pg, w_pp)


def _row_tile(rows, cols, n_arrays, budget=12 << 20):
    best = None
    for t in range(8, rows + 1, 8):
        if rows % t == 0 and t * cols * 4 * n_arrays <= budget:
            best = t
    return rows if best is None else best


def _sum_slots(a, *, name):
    g, r, c = a.shape
    tr = _row_tile(r, c, g + 1)

    def body(*refs):
        tot = refs[0][...]
        for ref in refs[1:g]:
            tot = tot + ref[...]
        refs[g][...] = tot

    specs = [pl.BlockSpec((None, tr, c), functools.partial(lambda i, slot: (slot, i, 0), slot=k)) for k in range(g)]
    return pl.pallas_call(
        body, name=name, grid=(r // tr,), in_specs=specs, out_specs=pl.BlockSpec((tr, c), lambda i: (i, 0)),
        out_shape=jax.ShapeDtypeStruct((r, c), a.dtype), compiler_params=_params(("parallel",)),
    )(*([a] * g))


def _adamw_refs(w_ref, g_ref, m_ref, v_ref, d_ref, mo_ref, vo_ref):
    gv = g_ref[...]
    mn = ADAM_B1 * m_ref[...] + (1.0 - ADAM_B1) * gv
    vn = ADAM_B2 * v_ref[...] + (1.0 - ADAM_B2) * (gv * gv)
    m_hat = mn / (1.0 - ADAM_B1 ** ADAM_STEP)
    v_hat = vn / (1.0 - ADAM_B2 ** ADAM_STEP)
    d_ref[...] = -ADAM_LR * (m_hat / (jnp.sqrt(v_hat) + ADAM_EPS) + ADAM_WD * w_ref[...])
    mo_ref[...] = mn
    vo_ref[...] = vn


def _adamw_many(ws, gs, ms, vs, *, name):
    n = len(ws)

    def body(*refs):
        ins, outs = refs[:4 * n], refs[4 * n:]
        for a in range(n):
            _adamw_refs(ins[a], ins[n + a], ins[2 * n + a], ins[3 * n + a], outs[3 * a], outs[3 * a + 1],
                        outs[3 * a + 2])

    vm = pl.BlockSpec(memory_space=pltpu.VMEM)
    res = pl.pallas_call(
        body, name=name, in_specs=[vm] * (4 * n), out_specs=[vm] * (3 * n),
        out_shape=[jax.ShapeDtypeStruct(a.shape, F32) for a in ws for _ in range(3)],
    )(*ws, *gs, *ms, *vs)
    return [tuple(res[3 * a:3 * a + 3]) for a in range(n)]


def _adamw(w, g, m, v, *, name):
    r, c = w.shape
    tr = _row_tile(r, c, 7)
    body = _adamw_refs

    blk = pl.BlockSpec((tr, c), lambda i: (i, 0))
    return pl.pallas_call(
        body, name=name, grid=(r // tr,), in_specs=[blk] * 4, out_specs=[blk] * 3,
        out_shape=[jax.ShapeDtypeStruct((r, c), F32)] * 3, compiler_params=_params(("parallel",)),
    )(w, g, m, v)


def _position():
    x, y, c = lax.axis_index("x"), lax.axis_index("y"), lax.axis_index("c")
    return x, y, c


def _other_chips(x, y):
    return [(1 - x, y), (x, 1 - y), (1 - x, 1 - y)]


def _stage_in(srcs, stage, sems):
    cps = [pltpu.make_async_copy(src, stage[a], sems.at[a]) for a, src in enumerate(srcs)]
    for cp in cps:
        cp.start()
    return cps


def _stage_out(staged, stage, dsts, sems):
    cps = []
    for a, dst in enumerate(dsts):
        staged[a].wait()
        cp = pltpu.make_async_copy(stage[a], dst, sems.at[a])
        cp.start()
        cps.append(cp)
    return cps


def _send_other_halves(grads, *, tag):
    n = len(grads)

    def body(*refs):
        ins, sib = refs[:n], refs[n:2 * n]
        send_sems, recv_sems = refs[2 * n:]
        x, y, c = _position()
        remote = []
        for a in range(n):
            half = ins[a].shape[1] // 2
            give = ins[a].at[:, pl.ds(pl.multiple_of((1 - c) * half, 8), half), :]
            rc = pltpu.make_async_remote_copy(
                src_ref=give, dst_ref=sib[a], send_sem=send_sems.at[a], recv_sem=recv_sems.at[a],
                device_id=(x, y, 1 - c), device_id_type=MESH)
            rc.start()
            remote.append(rc)
        for rc in remote:
            rc.wait_recv()
        for rc in remote:
            rc.wait_send()

    return pl.pallas_call(
        body, name="send_other_halves_" + tag, in_specs=[ANY] * n, out_specs=[ANY] * n,
        out_shape=[jax.ShapeDtypeStruct((g.shape[0], g.shape[1] // 2, g.shape[2]), g.dtype) for g in grads],
        scratch_shapes=[pltpu.SemaphoreType.DMA((n,)), pltpu.SemaphoreType.DMA((n,))],
        compiler_params=pltpu.CompilerParams(has_side_effects=True),
    )(*grads)


def _add_own_half(g4, sib, core, *, name):
    g, a2, c = sib.shape
    tr = _row_tile(a2, c, 4)

    def body(core_ref, a_ref, b_ref, o_ref, o16_ref):
        tot = a_ref[...] + b_ref[...]
        o_ref[...] = tot
        o16_ref[...] = tot.astype(BF16)

    blk = pl.BlockSpec((None, tr, c), lambda i, j, core_ref: (i, j, 0))
    return pl.pallas_call(
        body, name=name,
        grid_spec=pltpu.PrefetchScalarGridSpec(
            num_scalar_prefetch=1, grid=(g, a2 // tr),
            in_specs=[pl.BlockSpec((None, None, tr, c), lambda i, j, core_ref: (i, core_ref[0], j, 0)), blk],
            out_specs=[blk, blk]),
        out_shape=[jax.ShapeDtypeStruct(sib.shape, F32), jax.ShapeDtypeStruct(sib.shape, BF16)],
        compiler_params=_params(("parallel", "parallel")),
    )(core, g4.reshape(g, 2, a2, c), sib)


def _sum_chips(landed, own, chip, *, name):
    g, r, c = landed.shape
    tr = _row_tile(r, c, 5)

    def body(chip_ref, *refs):
        me = chip_ref[0]
        own_v = refs[g][...]
        tot = None
        for slot in range(g):
            term = jnp.where(me == slot, own_v, refs[slot][...].astype(F32))
            tot = term if tot is None else tot + term
        refs[g + 1][...] = tot

    def landed_spec(slot):
        return pl.BlockSpec((None, tr, c),
                            lambda i, chip_ref: (jnp.where(chip_ref[0] == slot, (slot + 1) % g, slot), i, 0))

    return pl.pallas_call(
        body, name=name,
        grid_spec=pltpu.PrefetchScalarGridSpec(
            num_scalar_prefetch=1, grid=(r // tr,),
            in_specs=[landed_spec(k) for k in range(g)]
            + [pl.BlockSpec((None, tr, c), lambda i, chip_ref: (chip_ref[0], i, 0))],
            out_specs=pl.BlockSpec((tr, c), lambda i, chip_ref: (i, 0))),
        out_shape=jax.ShapeDtypeStruct((r, c), F32), compiler_params=_params(("parallel",)),
    )(chip, *([landed] * g), own)


def _join_halves(halves):
    n = len(halves)

    def body(*refs):
        ins, outs, stage = refs[:n], refs[n:2 * n], refs[2 * n:3 * n]
        send_sems, recv_sems, in_sems, out_sems = refs[3 * n:]
        x, y, c = _position()
        remote = []
        staged = _stage_in(ins, stage, in_sems)
        for a in range(n):
            rc = pltpu.make_async_remote_copy(
                src_ref=ins[a], dst_ref=outs[a].at[c], send_sem=send_sems.at[a], recv_sem=recv_sems.at[a],
                device_id=(x, y, 1 - c), device_id_type=MESH)
            rc.start()
            remote.append(rc)
        local = _stage_out(staged, stage, [o.at[c] for o in outs], out_sems)
        for a in range(n):
            pltpu.make_async_remote_copy(
                src_ref=ins[a], dst_ref=outs[a].at[1 - c], send_sem=send_sems.at[a], recv_sem=recv_sems.at[a],
                device_id=(x, y, 1 - c), device_id_type=MESH).wait_recv()
        for rc in remote:
            rc.wait_send()
        for cp in local:
            cp.wait()

    return pl.pallas_call(
        body, name="join_halves", in_specs=[ANY] * n, out_specs=[ANY] * n,
        out_shape=[jax.ShapeDtypeStruct((2,) + h.shape, h.dtype) for h in halves],
        scratch_shapes=[pltpu.VMEM(h.shape, h.dtype) for h in halves]
        + [pltpu.SemaphoreType.DMA((n,)), pltpu.SemaphoreType.DMA((n,)), pltpu.SemaphoreType.DMA((n,)),
           pltpu.SemaphoreType.DMA((n,))],
        compiler_params=pltpu.CompilerParams(has_side_effects=True),
    )(*halves)


_HBM = pl.BlockSpec(memory_space=pltpu.HBM)
_SEM = pl.BlockSpec(memory_space=pltpu.SEMAPHORE)


def _chip_copies(srcs, lands, send_sems, recv_sems, scatter):
    x, y, c = _position()
    me = 2 * x + y
    outgoing, incoming = [], []
    for a, (src, land) in enumerate(zip(srcs, lands)):
        for k, (px, py) in enumerate(_other_chips(x, y)):
            peer = 2 * px + py
            sems = dict(send_sem=send_sems.at[3 * a + k], recv_sem=recv_sems.at[3 * a + k], device_id=(px, py, c),
                        device_id_type=MESH)
            outgoing.append(pltpu.make_async_remote_copy(
                src_ref=src.at[peer] if scatter else src, dst_ref=land.at[me], **sems))
            incoming.append(pltpu.make_async_remote_copy(
                src_ref=src.at[me] if scatter else src, dst_ref=land.at[peer], **sems))
    return outgoing, incoming


def _chips_start(srcs, *, scatter, name):
    n = len(srcs)
    lands = [lax.empty(a.shape if scatter else (N_CHIPS,) + a.shape, a.dtype) for a in srcs]

    def body(*refs):
        ins, send_sems, recv_sems, token = refs[:2 * n], refs[2 * n], refs[2 * n + 1], refs[-1]
        outgoing, _ = _chip_copies(ins[:n], ins[n:], send_sems, recv_sems, scatter)
        for cp in outgoing:
            cp.start()
        token[...] = jnp.zeros_like(token)

    bufs = list(srcs) + lands
    res = pl.pallas_call(
        body, name=name, in_specs=[_HBM] * (2 * n),
        out_specs=(_SEM, _SEM, *[_HBM] * (2 * n), pl.BlockSpec(memory_space=pltpu.VMEM)),
        out_shape=(pltpu.SemaphoreType.DMA((3 * n,)), pltpu.SemaphoreType.DMA((3 * n,)),
                   *[pltpu.HBM(a.shape, a.dtype) for a in bufs], jax.ShapeDtypeStruct((8, 128), F32)),
        input_output_aliases={i: 2 + i for i in range(2 * n)},
        compiler_params=pltpu.CompilerParams(has_side_effects=pltpu.SideEffectType.DATAFLOW_SIDE_EFFECTING),
    )(*[pltpu.with_memory_space_constraint(a, pltpu.HBM) for a in bufs])
    return res[0], res[1], list(res[2:2 + n]), list(res[2 + n:2 + 2 * n]), res[-1]


def _chips_wait(handle, after, *, scatter, name):
    send_sems, recv_sems, srcs, lands, _ = handle
    n = len(srcs)

    def body(*refs):
        ins, send_ref, recv_ref = refs[:2 * n], refs[2 * n], refs[2 * n + 1]
        outgoing, incoming = _chip_copies(ins[:n], ins[n:], send_ref, recv_ref, scatter)
        for cp in outgoing:
            cp.wait_send()
        for cp in incoming:
            cp.wait_recv()

    bufs = list(srcs) + list(lands)
    res = pl.pallas_call(
        body, name=name, in_specs=[_HBM] * (2 * n) + [_SEM, _SEM, ANY], out_specs=tuple([_HBM] * (2 * n)),
        out_shape=tuple(pltpu.HBM(a.shape, a.dtype) for a in bufs),
        input_output_aliases={i: i for i in range(2 * n)},
        compiler_params=pltpu.CompilerParams(has_side_effects=pltpu.SideEffectType.DATAFLOW_SIDE_EFFECTING),
    )(*bufs, send_sems, recv_sems, after)
    return list(res[:n]), list(res[n:])


def _gather_all(buf):
    def body(in_ref, out_ref, send_sems, recv_sems, local_sem):
        x, y, c = _position()
        me = 4 * x + 2 * y + c
        peers = [(x, y, 1 - c)] + [(px, py, pc) for (px, py) in _other_chips(x, y) for pc in (c, 1 - c)]
        cp = pltpu.make_async_copy(in_ref, out_ref.at[me], local_sem)
        cp.start()
        remote = []
        for k, peer in enumerate(peers):
            rc = pltpu.make_async_remote_copy(
                src_ref=in_ref, dst_ref=out_ref.at[me], send_sem=send_sems.at[k], recv_sem=recv_sems.at[k],
                device_id=peer, device_id_type=MESH)
            rc.start()
            remote.append(rc)
        for k, (px, py, pc) in enumerate(peers):
            pltpu.make_async_remote_copy(
                src_ref=in_ref, dst_ref=out_ref.at[4 * px + 2 * py + pc], send_sem=send_sems.at[k],
                recv_sem=recv_sems.at[k], device_id=(px, py, pc), device_id_type=MESH).wait_recv()
        for rc in remote:
            rc.wait_send()
        cp.wait()

    return pl.pallas_call(
        body, name="gather_all", in_specs=[ANY], out_specs=ANY,
        out_shape=jax.ShapeDtypeStruct((N_DEV,) + buf.shape, buf.dtype),
        scratch_shapes=[pltpu.SemaphoreType.DMA((N_DEV - 1,)), pltpu.SemaphoreType.DMA((N_DEV - 1,)),
                        pltpu.SemaphoreType.DMA],
        compiler_params=pltpu.CompilerParams(has_side_effects=True),
    )(buf)


def _cols_from_shards(g4):
    _, k, n = g4.shape
    return g4.transpose(1, 0, 2).reshape(k, N_CHIPS * n)


def _cols_to_shards(w):
    k, n = w.shape
    return w.reshape(k, N_CHIPS, n // N_CHIPS).transpose(1, 0, 2)


def _pad_heads(w, width):
    k = w.shape[0]
    w3 = w.reshape(k, N_HEADS, width)
    return jnp.pad(w3, ((0, 0), (0, 0), (0, HEAD_PAD - width))).reshape(k, D_ATT)


def _unpad_heads(w, width):
    k = w.shape[0]
    return w.reshape(k, N_HEADS, HEAD_PAD)[:, :, :width]


def _rope_tables(s, after):
    pos = jnp.arange(s, dtype=F32) + after
    inv_freq = ROPE_THETA ** (-jnp.arange(0, QK_ROPE, 2, dtype=F32) / QK_ROPE)
    ang = pos[:, None] * inv_freq[None, :]
    cos_h, sin_h = jnp.cos(ang), jnp.sin(ang)
    half = QK_ROPE // 2
    z = jnp.zeros((s, half), F32)
    ones = jnp.ones((s, QK_NOPE), F32)
    tail = jnp.zeros((s, HEAD_PAD - QK_NOPE - QK_ROPE), F32)
    cos = jnp.concatenate([ones, cos_h, cos_h, tail + 1.0], axis=1)
    sin_a = jnp.concatenate([ones * 0.0, -sin_h, z, tail], axis=1)
    sin_b = jnp.concatenate([ones * 0.0, z, sin_h, tail], axis=1)
    return cos, sin_a, sin_b


def _local_step(x, p, target, wts, late_weights, reduce_early, reduce_last):
    s = x.shape[0]
    cos, sin_a, sin_b = wts["rope"]
    g1, gq, gkv, g2, g3, gf = (wts[k] for k in ("norm_mix_g", "q_norm_g", "kv_norm_g", "norm_ffn_g", "ple_norm_g",
                                                 "final_norm_g"))
    w_in_p, w_uq_p, w_kv_p = wts["w_in_p"], wts["w_uq_p"], wts["w_kv_p"]
    conv_w8, fconv_w, fconv_b = wts["conv_w8"], wts["ffn_conv_w"], wts["ffn_conv_b"]

    (h, z), _ = _mm_fused(x, w_in_p, name="mm_in", prologue=_pro_rms, vecs=[g1], epilogue=_epi_plain, row_outs=[F32],
                          tm=1024)
    y_conv, qn, kvn, kr = _mix_pre(z, conv_w8, gq, gkv, cos, sin_a, sin_b)
    q, k, v, q_t = _qkv_proj(qn, kvn, kr, w_uq_p, w_kv_p, cos, sin_a, sin_b)
    o, lse = _flash_fwd(q, k, v)
    late = late_weights(lse)
    w_o_a, w_o_b, w_up, w_down = late["w_o_a"], late["w_o_b"], late["w_up"], late["w_down"]
    w_pg, w_pp = late["w_ple_gate"], late["w_ple_proj"]
    (x1, hf), _ = _mm_fused(o, w_o_b, second=(y_conv, w_o_a), name="mm_o", rows=[x], vecs=[g2],
                            epilogue=_epi_add_rms, row_outs=[F32, BF16], tm=1024)
    a_pre, act = _ffn_fwd(hf, w_up, fconv_w, fconv_b)
    (x2, n3), _ = _mm_fused(act, w_down, name="mm_down", rows=[x1], vecs=[g3], epilogue=_epi_add_rms,
                            row_outs=[F32, BF16], tm=1024)
    loss, dx3, dgl, dpp, d_gf = _ple_final(x2, n3, p, target, gf, w_pg, w_pp)

    grads, early = {"final_norm_g": d_gf}, {}
    early["w_ple_proj"] = _mm(p, dpp, ta=True, name="mm_d_wpp", tm=256, tn=1024, tk=2048)
    early["w_ple_gate"] = _mm(n3, dgl, ta=True, name="mm_d_wpg", tm=1024, tn=1024, tk=2048)
    (dx2,), (grads["ple_norm_g"],) = _mm_fused(dgl, w_pg, tb=True, name="mm_d_n3", rows=[x2, dx3], vecs=[g3],
                                               epilogue=_epi_rms_bwd, row_outs=[F32], n_vec_out=1, tm=1024)
    early["w_down"] = _mm(act, dx2, ta=True, name="mm_d_wdown", tm=1408, tn=1024, tk=2048)
    da_pre, grads["ffn_conv_w"], grads["ffn_conv_b"] = _ffn_bwd(dx2, w_down, a_pre, fconv_w, fconv_b)
    early["w_up"] = _mm(hf, da_pre, ta=True, b_split=True, name="mm_d_wup", tm=1024, tn=1408, tk=2048,
                       o_shards=True)
    (dx1,), (grads["norm_ffn_g"],) = _mm_fused(da_pre, w_up, tb=True, a_split=True, name="mm_d_hf", rows=[x1, dx2],
                                               vecs=[g2], epilogue=_epi_rms_bwd, row_outs=[F32], n_vec_out=1)
    d_wo_a = _mm(y_conv, dx1, ta=True, name="mm_d_wo_conv", tm=512, tn=1024, tk=2048)
    d_wo_b = _mm(o, dx1, ta=True, name="mm_d_wo_att", tm=1024, tn=1024, tk=2048)
    early["w_o"] = jnp.concatenate([d_wo_a, d_wo_b.reshape(N_HEADS, HEAD_PAD, D_MODEL)[:, :V_HEAD]
                                    .reshape(N_HEADS * V_HEAD, D_MODEL)], axis=0)
    token, finish = reduce_early(early)
    dyc = _mm(dx1, w_o_a, tb=True, name="mm_d_yconv", tm=512, tn=512, tk=1024)
    (do, do_t), _ = _mm_fused(dx1, w_o_b, tb=True, name="mm_d_o", epilogue=_epi_plain, row_outs=[BF16],
                              transposed_out=BF16, tm=1024)
    delta = _attn_delta(do, o)
    dq, dk, dv = _flash_bwd(q, q_t, k, v, do, do_t, lse, delta, token)
    reduced_early = finish(dq)
    dq_pre, dkr = _qk_bwd(dq, dk, cos, sin_a, sin_b)
    grads["w_uq_p"] = _mm(qn, dq_pre, ta=True, name="mm_d_wuq", tm=256, tn=1024, tk=2048)
    dqn = _mm(dq_pre, w_uq_p, tb=True, name="mm_d_qn", tm=512, tn=256, tk=1024)
    grads["w_k_p"] = _mm(kvn, dk, ta=True, name="mm_d_wk", tm=128, tn=1024, tk=2048)
    grads["w_v_p"] = _mm(kvn, dv, ta=True, name="mm_d_wv", tm=128, tn=1024, tk=2048)
    (dkvn,), _ = _mm_fused(dk, w_kv_p[:, :D_ATT], tb=True, second=(dv, w_kv_p[:, D_ATT:].T), name="mm_d_kvn",
                           epilogue=_epi_plain, row_outs=[F32])
    dz, grads["conv_w"], grads["q_norm_g"], grads["kv_norm_g"] = _mix_bwd(
        z, dyc, dqn, dkvn, dkr, conv_w8, gq, gkv, cos, sin_a, sin_b)
    grads["w_in_p"] = _mm(h, dz, ta=True, name="mm_d_win", tm=1024, tn=1024, tk=2048)
    token, finish = reduce_last({n: grads.pop(n) for n in ("w_in_p", "w_uq_p", "w_k_p", "w_v_p")})
    (grad_x,), (grads["norm_mix_g"],) = _mm_fused(dz, w_in_p, tb=True, name="mm_d_h", rows=[x, dx1],
                                                  vecs=[g1 + token[0, 0]], epilogue=_epi_rms_bwd, row_outs=[F32],
                                                  n_vec_out=1, tm=1024)
    return loss[0, 0], grad_x, grads, reduced_early, finish(grad_x)


_EARLY_W = ("w_in", "w_uq", "w_ukv")
_LATE_W = ("w_o", "w_up", "w_down", "w_ple_gate", "w_ple_proj")
_BIG = _EARLY_W + _LATE_W
_COL_SHARDED = ("w_in", "w_uq", "w_ukv", "w_up", "w_ple_proj")
_SMALL = ("norm_mix_g", "conv_w", "q_norm_g", "kv_norm_g", "norm_ffn_g", "ffn_conv_w", "ffn_conv_b", "ple_norm_g",
          "final_norm_g")


def _full_from_slots(n, g4):
    return _cols_from_shards(g4) if n in _COL_SHARDED else g4.reshape(-1, g4.shape[2])


def _shard_major(n, g):
    if g.ndim == 3:
        return g
    return _cols_to_shards(g) if n in _COL_SHARDED else g.reshape(N_CHIPS, g.shape[0] // N_CHIPS, g.shape[1])


def _early_shards(w):
    shards = [w[n][0].astype(BF16) for n in _EARLY_W]
    shards.append(jnp.pad(w["conv_w"][0], ((0, 5), (0, 0))))
    shards.append(jnp.pad(w["ffn_conv_w"][0], ((0, 5), (0, 0))))
    return shards


def _fill_own_slot(landed, own, chip):
    return [lax.dynamic_update_slice(g4, a[None], (chip[0], 0, 0)) for g4, a in zip(landed, own)]


def _early_weights(got, w):
    full = {n: _full_from_slots(n, g4) for n, g4 in zip(_EARLY_W, got)}
    full["conv_w8"] = _cols_from_shards(got[len(_EARLY_W)])
    full["ffn_conv_w8"] = _cols_from_shards(got[len(_EARLY_W) + 1])
    return _layout_early(full, w)


def _layout_early(full, w):
    out = {n: w[n] for n in ("norm_mix_g", "q_norm_g", "kv_norm_g", "norm_ffn_g", "ple_norm_g")}
    out["final_norm_g"] = w["final_norm_g"][None, :]
    w_in = full["w_in"]
    zc = jnp.zeros((D_MODEL, QK_NOPE), BF16)
    zt = jnp.zeros((D_MODEL, HEAD_PAD - QK_NOPE - QK_ROPE), BF16)
    out["w_in_p"] = jnp.concatenate([w_in[:, :D_IN - QK_ROPE], zc, w_in[:, D_IN - QK_ROPE:], zt], axis=1)
    out["w_uq_p"] = _pad_heads(full["w_uq"], QK_NOPE + QK_ROPE)
    kv3 = full["w_ukv"].reshape(KV_LORA, N_HEADS, QK_NOPE + V_HEAD)
    out["w_kv_p"] = jnp.concatenate([_pad_heads(kv3[:, :, :QK_NOPE].reshape(KV_LORA, -1), QK_NOPE),
                                     _pad_heads(kv3[:, :, QK_NOPE:].reshape(KV_LORA, -1), V_HEAD)], axis=1)
    out["conv_w8"] = full["conv_w8"]
    fw = full["ffn_conv_w8"]
    out["ffn_conv_w"] = jnp.stack([fw[:, :D_FF], fw[:, D_FF:]])
    out["ffn_conv_b"] = w["ffn_conv_b"].reshape(2, 1, D_FF)
    return out


def _layout_late(full):
    w_o = full["w_o"]
    out = {"w_o_a": w_o[:CONV_WIDTH]}
    out["w_o_b"] = jnp.pad(w_o[CONV_WIDTH:].reshape(N_HEADS, V_HEAD, D_MODEL),
                           ((0, 0), (0, HEAD_PAD - V_HEAD), (0, 0))).reshape(D_ATT, D_MODEL)
    for n in ("w_up", "w_down", "w_ple_gate", "w_ple_proj"):
        out[n] = full[n]
    return out


def _true_matrices(g):
    out = {}
    wp = g["w_in_p"]
    out["w_in"] = jnp.concatenate([wp[:, :D_IN - QK_ROPE], wp[:, D_IN_PAD - HEAD_PAD + QK_NOPE:
                                                              D_IN_PAD - HEAD_PAD + QK_NOPE + QK_ROPE]], axis=1)
    out["w_uq"] = _unpad_heads(g["w_uq_p"], QK_NOPE + QK_ROPE).reshape(Q_LORA, -1)
    out["w_ukv"] = jnp.concatenate([_unpad_heads(g["w_k_p"], QK_NOPE), _unpad_heads(g["w_v_p"], V_HEAD)],
                                   axis=2).reshape(KV_LORA, -1)
    return out


def _true_vectors(g):
    out = {}
    out["conv_w"] = g["conv_w"]
    fw = g["ffn_conv_w"]
    out["ffn_conv_w"] = jnp.concatenate([fw[0, :3], fw[1, :3]], axis=1)
    out["ffn_conv_b"] = g["ffn_conv_b"].reshape(1, 2 * D_FF)
    for n in ("norm_mix_g", "q_norm_g", "kv_norm_g", "norm_ffn_g", "ple_norm_g", "final_norm_g"):
        out[n] = g[n]
    return out


def _chip_partials(names, g, core, *, tag):
    g4 = [_shard_major(n, g[n]) for n in names]
    sib = _send_other_halves(g4, tag=tag)
    return [_add_own_half(a, b, core, name="add_cores_" + n) for n, a, b in zip(names, g4, sib)]


_SMALL_SIZES = {"norm_mix_g": D_MODEL, "conv_w": 3 * CONV_WIDTH, "q_norm_g": Q_LORA, "kv_norm_g": KV_LORA,
                "norm_ffn_g": D_MODEL, "ffn_conv_w": 6 * D_FF, "ffn_conv_b": 2 * D_FF, "ple_norm_g": D_MODEL,
                "final_norm_g": D_MODEL}


def _pack(parts, rows):
    flat = jnp.concatenate([a.reshape(-1) for a in parts])
    return jnp.pad(flat, (0, rows * 128 - flat.shape[0])).reshape(rows, 128)


def _unpack(buf, sizes):
    flat = buf.reshape(-1)
    out, at = [], 0
    for n in sizes:
        out.append(flat[at:at + n])
        at += n
    return out


def _reduce_small(g, loss):
    sizes = [1] + [_SMALL_SIZES[n] for n in _SMALL]
    rows = -(-sum(sizes) // 1024) * 8
    slots = _gather_all(_pack([loss] + [g[n] for n in _SMALL], rows))
    parts = _unpack(_sum_slots(slots, name="sum_small"), sizes)
    return parts[0][0], dict(zip(_SMALL, parts[1:]))


def kernel(x, p, norm_mix_g, w_in, conv_w, q_norm_g, w_uq, kv_norm_g, w_ukv, w_o, norm_ffn_g, w_up, ffn_conv_w, ffn_conv_b, w_down, ple_norm_g, w_ple_gate, w_ple_proj, final_norm_g, loss_target, m_norm_mix_g, m_w_in, m_conv_w, m_q_norm_g, m_w_uq, m_kv_norm_g, m_w_ukv, m_w_o, m_norm_ffn_g, m_w_up, m_ffn_conv_w, m_ffn_conv_b, m_w_down, m_ple_norm_g, m_w_ple_gate, m_w_ple_proj, m_final_norm_g, v_norm_mix_g, v_w_in, v_conv_w, v_q_norm_g, v_w_uq, v_kv_norm_g, v_w_ukv, v_w_o, v_norm_ffn_g, v_w_up, v_ffn_conv_w, v_ffn_conv_b, v_w_down, v_ple_norm_g, v_w_ple_gate, v_w_ple_proj, v_final_norm_g):
    names = ["norm_mix_g", "w_in", "conv_w", "q_norm_g", "w_uq", "kv_norm_g", "w_ukv", "w_o", "norm_ffn_g", "w_up",
             "ffn_conv_w", "ffn_conv_b", "w_down", "ple_norm_g", "w_ple_gate", "w_ple_proj", "final_norm_g"]
    w = dict(zip(names, (norm_mix_g, w_in, conv_w, q_norm_g, w_uq, kv_norm_g, w_ukv, w_o, norm_ffn_g, w_up,
                         ffn_conv_w, ffn_conv_b, w_down, ple_norm_g, w_ple_gate, w_ple_proj, final_norm_g)))
    m = dict(zip(names, (m_norm_mix_g, m_w_in, m_conv_w, m_q_norm_g, m_w_uq, m_kv_norm_g, m_w_ukv, m_w_o,
                         m_norm_ffn_g, m_w_up, m_ffn_conv_w, m_ffn_conv_b, m_w_down, m_ple_norm_g, m_w_ple_gate,
                         m_w_ple_proj, m_final_norm_g)))
    v = dict(zip(names, (v_norm_mix_g, v_w_in, v_conv_w, v_q_norm_g, v_w_uq, v_kv_norm_g, v_w_ukv, v_w_o,
                         v_norm_ffn_g, v_w_up, v_ffn_conv_w, v_ffn_conv_b, v_w_down, v_ple_norm_g, v_w_ple_gate,
                         v_w_ple_proj, v_final_norm_g)))

    core = lax.axis_index("c").astype(jnp.int32).reshape(1)
    chip = (2 * lax.axis_index("x") + lax.axis_index("y")).astype(jnp.int32).reshape(1)

    first = _chips_start(_early_shards(w), scatter=False, name="gather_early_start")
    rope = _rope_tables(x.shape[1], first[4][0, 0])
    late_shards = [w[n][0].astype(BF16) for n in _LATE_W]
    ready, *late_shards = lax.optimization_barrier((rope[0], *late_shards))
    own, landed = _chips_wait(first, ready, scatter=False, name="gather_early_wait")
    wts = _early_weights(_fill_own_slot(landed, own, chip), w)
    wts["rope"] = (ready,) + tuple(rope[1:])
    late_shards[0], _ = lax.optimization_barrier((late_shards[0], own[0]))
    gather = _chips_start(late_shards, scatter=False, name="gather_late_start")
    wts["norm_mix_g"] = wts["norm_mix_g"] + gather[4][0, 0]

    def late_weights(after):
        shards, landed = _chips_wait(gather, after, scatter=False, name="gather_late_wait")
        return _layout_late({n: _full_from_slots(n, g4)
                             for n, g4 in zip(_LATE_W, _fill_own_slot(landed, shards, chip))})

    def reduce_early(g):
        parts = _chip_partials(_LATE_W, g, core, tag="early")
        scatter = _chips_start([t16 for _, t16 in parts], scatter=True, name="scatter_early_start")

        def finish(after):
            _, landed = _chips_wait(scatter, after, scatter=True, name="scatter_early_wait")
            return [_sum_chips(a, t32, chip, name="sum_chips_" + n) for n, a, (t32, _) in zip(_LATE_W, landed, parts)]

        return scatter[4], finish

    def reduce_last(g):
        parts = _chip_partials(_EARLY_W, _true_matrices(g), core, tag="late")
        scatter = _chips_start([t16 for _, t16 in parts], scatter=True, name="scatter_late_start")

        def finish(after):
            _, landed = _chips_wait(scatter, after, scatter=True, name="scatter_late_wait")
            return [_sum_chips(a, t32, chip, name="sum_chips_" + n) for n, a, (t32, _) in zip(_EARLY_W, landed, parts)]

        return scatter[4], finish

    loss, grad_x, small_grads, halves_early, halves_last = _local_step(
        x[0], p[0, 0], loss_target[0], wts, late_weights, reduce_early, reduce_last)
    g_full = _true_vectors(small_grads)
    whole = _join_halves(halves_last + halves_early)
    big = {n: a.reshape(-1, a.shape[2]) for n, a in zip(_BIG, whole)}

    g_out, d_out, m_out, v_out = {}, {}, {}, {}
    for n in _BIG:
        shape = w[n].shape
        g = big[n]
        d, mn, vn = _adamw(w[n][0], g, m[n][0], v[n][0], name="adamw_" + n)
        g_out[n], d_out[n], m_out[n], v_out[n] = (a.reshape(shape) for a in (g, d, mn, vn))

    loss, small = _reduce_small(g_full, loss)
    chip = 2 * lax.axis_index("x") + lax.axis_index("y")
    g_small = {}
    for n in _SMALL:
        shape = w[n].shape
        g = small[n]
        if n in ("conv_w", "ffn_conv_w"):
            width = shape[-1]
            g = lax.dynamic_slice(g.reshape(3, N_CHIPS * width), (0, chip * width), (3, width))
        g_small[n] = g.reshape(shape)
    flat = [[src[n].reshape(-1, src[n].shape[-1]) for n in _SMALL] for src in (w, g_small, m, v)]
    for n, (d, mn, vn) in zip(_SMALL, _adamw_many(*flat, name="adamw_small")):
        shape = w[n].shape
        g_out[n], d_out[n], m_out[n], v_out[n] = g_small[n], d.reshape(shape), mn.reshape(shape), vn.reshape(shape)

    return (loss, grad_x[None], *[g_out[n] for n in names], *[d_out[n] for n in names],
            *[m_out[n] for n in names], *[v_out[n] for n in names])
```
